```python
import math
import jax, jax.numpy as jnp
from jax import lax
import numpy as np


D_MODEL = 1024
BATCH = 32
SEQ = 2048
DEPTH = 2

GRID_W = 64
CTX_LEN = 256
HEAD_DIM = 128
ROPE_THETA = 10000.0
EPS = 1e-6
N_MOD = 9
D_FF = 2816

A_HEADS = 4
A_KV_HEADS = 2
A_GROUP = A_HEADS // A_KV_HEADS
Q_BLOCK = 128
POOL_WINDOWS = (2, 4, 8, 16)
POOL_GROUP = 128
POOL_DIM = POOL_GROUP * len(POOL_WINDOWS)
A_Q_DIM = A_HEADS * HEAD_DIM
A_KV_DIM = A_KV_HEADS * HEAD_DIM
AB_IN = A_Q_DIM + 2 * A_KV_DIM + POOL_DIM
AB_OUT = A_Q_DIM + POOL_DIM

C_HEADS = 8
C_DIM = C_HEADS * HEAD_DIM
CONV_W = 3
CHUNK = 64
C_IN = 4 * C_DIM + 4 * C_HEADS

N_EVEN = (DEPTH + 1) // 2
N_ODD = DEPTH // 2

kernel_name = 'hybrid_dit_gqa_pool_gdn_macaron'


def rms_norm(t, g):
    tf = t.astype(jnp.float32)
    y = tf * lax.rsqrt(jnp.mean(tf * tf, axis=-1, keepdims=True) + EPS)
    return (y * g.astype(jnp.float32)).astype(t.dtype)


def l2_norm(t):
    return t * lax.rsqrt(jnp.sum(t * t, axis=-1, keepdims=True) + EPS)


def adaln(cond, w, b):
    m = jax.nn.silu(cond) @ w + b
    return m.reshape(cond.shape[0], 1, N_MOD, cond.shape[-1])


def modulate(t, g, m, s):
    return rms_norm(t, g) * (1 + m[:, :, 3 * s + 1]) + m[:, :, 3 * s]


def swiglu(t, wg, wu, wd):
    return (jax.nn.silu(t @ wg) * (t @ wu)) @ wd


def axial_rope_tables(rows):
    row = jnp.repeat(jnp.arange(rows), GRID_W).astype(jnp.float32)
    col = jnp.tile(jnp.arange(GRID_W), rows).astype(jnp.float32)
    half = HEAD_DIM // 2
    inv_freq = jnp.power(ROPE_THETA, -jnp.arange(0, half, 2, dtype=jnp.float32) / half)
    ang = jnp.stack([row[:, None] * inv_freq, col[:, None] * inv_freq], 0)
    return jnp.cos(ang), jnp.sin(ang)


def apply_axial_rope(t, cos, sin):
    tf = t.astype(jnp.float32)
    segs = tf.reshape(tf.shape[:-1] + (2, 2, HEAD_DIM // 4))
    x1, x2 = segs[..., 0, :], segs[..., 1, :]
    cb = cos.transpose(1, 0, 2)[None, :, None]
    sb = sin.transpose(1, 0, 2)[None, :, None]
    out = jnp.stack([x1 * cb - x2 * sb, x2 * cb + x1 * sb], axis=-2)
    return out.reshape(t.shape).astype(t.dtype)


def block_attention(q, k, v):
    B, N, Hq, Dh = q.shape
    nb = N // Q_BLOCK
    qb = q.reshape(B, nb, Q_BLOCK, A_KV_HEADS, A_GROUP, Dh).transpose(1, 0, 2, 3, 4, 5)
    scale = Dh ** -0.5

    def one_block(qblk):
        s = jnp.einsum('bqhgd,bkhd->bhgqk', qblk, k).astype(jnp.float32) * scale
        p = jax.nn.softmax(s, axis=-1).astype(v.dtype)
        return jnp.einsum('bhgqk,bkhd->bqhgd', p, v)

    o = lax.map(one_block, qb)
    return o.transpose(1, 0, 2, 3, 4, 5).reshape(B, N, Hq * Dh)


def multiscale_pool(u, pool_w, pool_scale):
    B, N, _ = u.shape
    uf = u.astype(jnp.float32).reshape(B, N, len(POOL_WINDOWS), POOL_GROUP)
    csum = jnp.concatenate([jnp.zeros_like(uf[:, :1]), jnp.cumsum(uf, axis=1)], axis=1)
    t = jnp.arange(N)
    means = []
    for gi, w in enumerate(POOL_WINDOWS):
        lo = jnp.clip(t - w // 2, 0, N)
        hi = jnp.clip(t + w - w // 2, 0, N)
        cg = csum[:, :, gi]
        cnt = (hi - lo).astype(jnp.float32)[None, :, None]
        means.append((cg[:, hi] - cg[:, lo]) / cnt)
    pooled = (jnp.stack(means, axis=2) - uf).astype(u.dtype)
    y = jnp.einsum('bngc,gcd->bngd', pooled, pool_w).reshape(B, N, POOL_DIM)
    return y * pool_scale


def mix_attn_pool(xn, hn, cos, sin, w_in, q_g, k_g, pool_w, pool_scale, w_out, ctx_out):
    def project(t):
        p = t @ w_in
        B, N, _ = p.shape
        q = p[..., :A_Q_DIM].reshape(B, N, A_HEADS, HEAD_DIM)
        k = p[..., A_Q_DIM:A_Q_DIM + A_KV_DIM].reshape(B, N, A_KV_HEADS, HEAD_DIM)
        v = p[..., A_Q_DIM + A_KV_DIM:A_Q_DIM + 2 * A_KV_DIM].reshape(B, N, A_KV_HEADS, HEAD_DIM)
        u = p[..., A_Q_DIM + 2 * A_KV_DIM:]
        return rms_norm(q, q_g), rms_norm(k, k_g), v, u

    qx, kx, vx, ux = project(xn)
    qh, kh, vh, uh = project(hn)
    qx = apply_axial_rope(qx, cos, sin)
    kx = apply_axial_rope(kx, cos, sin)
    k_all = jnp.concatenate([kh, kx], axis=1)
    v_all = jnp.concatenate([vh, vx], axis=1)
    yx = jnp.concatenate([block_attention(qx, k_all, v_all), multiscale_pool(ux, pool_w, pool_scale)], axis=-1) @ w_out
    yh = None
    if ctx_out:
        yh = jnp.concatenate([block_attention(qh, kh, vh), multiscale_pool(uh, pool_w, pool_scale)], axis=-1) @ w_out
    return yx, yh


def short_conv(t, w):
    return lax.conv_general_dilated(t, w[:, None, :], window_strides=(1,),
                                    padding=((CONV_W // 2, CONV_W // 2),),
                                    dimension_numbers=('NWC', 'WIO', 'NWC'),
                                    feature_group_count=t.shape[-1])


def gated_delta_chunked(q, k, v, g, beta, s0, need_out):
    B, N, H, Dk = q.shape
    Dv = v.shape[-1]
    nc = N // CHUNK

    def chunked(t):
        t = t.reshape((B, nc, CHUNK, H) + t.shape[3:])
        return jnp.moveaxis(t, 3, 2)

    q, k, v, g, beta = (chunked(t) for t in (q, k, v, g, beta))
    gc = jnp.cumsum(g, axis=-1)
    idx = jnp.arange(CHUNK)
    incl = idx[:, None] >= idx[None, :]
    strict = idx[:, None] > idx[None, :]
    decay = jnp.exp(jnp.where(incl, gc[..., :, None] - gc[..., None, :], -jnp.inf))
    kb = k * beta[..., None]
    a_mat = jnp.where(strict, jnp.einsum('bnhid,bnhjd->bnhij', kb, k) * decay, 0.0) + jnp.eye(CHUNK, dtype=jnp.float32)
    rhs = jnp.concatenate([v * beta[..., None], kb * jnp.exp(gc)[..., None]], axis=-1)
    sol = lax.linalg.triangular_solve(a_mat, rhs, left_side=True, lower=True, unit_diagonal=True)
    u, w = sol[..., :Dv], sol[..., Dv:]
    k_tail = k * jnp.exp(gc[..., -1:] - gc)[..., None]
    g_tot = jnp.exp(gc[..., -1])
    xs = [u, w, k_tail, g_tot]
    if need_out:
        qk = jnp.where(incl, jnp.einsum('bnhid,bnhjd->bnhij', q, k) * decay, 0.0)
        xs += [q * jnp.exp(gc)[..., None], qk]
    xs = tuple(jnp.moveaxis(t, 1, 0) for t in xs)

    def step(state, inp):
        u_c, w_c, kt_c, gt_c = inp[:4]
        v_new = u_c - jnp.einsum('bhcd,bhde->bhce', w_c, state)
        new_state = state * gt_c[..., None, None] + jnp.einsum('bhcd,bhce->bhde', kt_c, v_new)
        if need_out:
            qd_c, qk_c = inp[4:]
            o = jnp.einsum('bhcd,bhde->bhce', qd_c, state) + jnp.einsum('bhij,bhje->bhie', qk_c, v_new)
            return new_state, o
        return new_state, None

    s_fin, o = lax.scan(step, s0, xs)
    if need_out:
        o = jnp.moveaxis(jnp.moveaxis(o, 0, 1), 2, 3).reshape(B, N, H, Dv)
    return o, s_fin


def mix_gated_delta(xn, hn, w_in, conv_w, a_log, dt_bias, o_g, w_out, ctx_out):
    def prep(t):
        p = t @ w_in
        B, N, _ = p.shape
        qkv = jax.nn.silu(short_conv(p[..., :3 * C_DIM], conv_w)).astype(jnp.float32)
        qkv = qkv.reshape(B, N, 3, C_HEADS, HEAD_DIM)
        q = l2_norm(qkv[:, :, 0]) * (HEAD_DIM ** -0.5)
        k = l2_norm(qkv[:, :, 1])
        v = qkv[:, :, 2]
        z = p[..., 3 * C_DIM:4 * C_DIM].reshape(B, N, C_HEADS, HEAD_DIM)
        ab = p[..., 4 * C_DIM:].astype(jnp.float32).reshape(B, N, 2, 2, C_HEADS)
        g = -jnp.exp(a_log.astype(jnp.float32)) * jax.nn.softplus(ab[:, :, :, 0] + dt_bias.astype(jnp.float32))
        beta = jax.nn.sigmoid(ab[:, :, :, 1])
        return q, k, v, z, g, beta

    qx, kx, vx, zx, gx, bx = prep(xn)
    qh, kh, vh, zh, gh, bh = prep(hn)
    s0 = jnp.zeros((xn.shape[0], C_HEADS, HEAD_DIM, HEAD_DIM), jnp.float32)
    outs_x, outs_h = [], []
    for d in range(2):
        flip = (lambda t: jnp.flip(t, axis=1)) if d == 1 else (lambda t: t)
        o_h, s_h = gated_delta_chunked(flip(qh), flip(kh), flip(vh), flip(gh[:, :, d]), flip(bh[:, :, d]), s0, ctx_out)
        o_x, _ = gated_delta_chunked(flip(qx), flip(kx), flip(vx), flip(gx[:, :, d]), flip(bx[:, :, d]), s_h, True)
        outs_x.append(flip(o_x))
        if ctx_out:
            outs_h.append(flip(o_h))

    def finish(o, z, ref):
        y = rms_norm(o, o_g) * jax.nn.silu(z.astype(jnp.float32))
        return y.reshape(ref.shape[0], ref.shape[1], C_DIM).astype(ref.dtype) @ w_out

    yx = finish(outs_x[0] + outs_x[1], zx, xn)
    yh = finish(outs_h[0] + outs_h[1], zh, hn) if ctx_out else None
    return yx, yh


def _fwd_setup_inputs(seed: int = 0) -> dict:
    key = jax.random.key(seed)
    ks = jax.random.split(key, 24)
    D, F = D_MODEL, D_FF

    def nrm(k, shape, s):
        return jax.random.normal(k, shape, jnp.float32) * s

    dt = jnp.exp(jax.random.uniform(ks[19], (N_ODD, 2, C_HEADS), jnp.float32,
                                    minval=math.log(1e-3), maxval=math.log(1e-1)))
    return {
        'x': nrm(ks[0], (BATCH, SEQ, D), 1.0),
        'c': nrm(ks[1], (BATCH, D), 1.0),
        'ctx': nrm(ks[2], (BATCH, CTX_LEN, D), 1.0),
        'c_ctx': nrm(ks[3], (D,), 1.0),
        'w_mod': nrm(ks[4], (DEPTH, D, N_MOD * D), 0.5 * D ** -0.5),
        'b_mod': nrm(ks[5], (DEPTH, N_MOD * D), 0.01),
        'norm_g': 1.0 + nrm(ks[6], (DEPTH, 3, D), 0.02),
        'ffn_wg': nrm(ks[7], (DEPTH, 2, D, F), D ** -0.5),
        'ffn_wu': nrm(ks[8], (DEPTH, 2, D, F), D ** -0.5),
        'ffn_wd': nrm(ks[9], (DEPTH, 2, F, D), F ** -0.5),
        'ab_w_in': nrm(ks[10], (N_EVEN, D, AB_IN), D ** -0.5),
        'ab_q_norm': 1.0 + nrm(ks[11], (N_EVEN, HEAD_DIM), 0.02),
        'ab_k_norm': 1.0 + nrm(ks[12], (N_EVEN, HEAD_DIM), 0.02),
        'pool_w': nrm(ks[13], (N_EVEN, len(POOL_WINDOWS), POOL_GROUP, POOL_GROUP), POOL_GROUP ** -0.5),
        'pool_scale': 1.0 + nrm(ks[14], (N_EVEN, POOL_DIM), 0.1),
        'ab_w_out': nrm(ks[15], (N_EVEN, AB_OUT, D), AB_OUT ** -0.5),
        'gdn_w_in': nrm(ks[16], (N_ODD, D, C_IN), D ** -0.5),
        'gdn_conv_w': nrm(ks[17], (N_ODD, CONV_W, 3 * C_DIM), CONV_W ** -0.5),
        'gdn_a_log': jnp.log(jax.random.uniform(ks[18], (N_ODD, 2, C_HEADS), jnp.float32, minval=1.0, maxval=16.0)),
        'gdn_dt_bias': dt + jnp.log(-jnp.expm1(-dt)),
        'gdn_o_norm': 1.0 + nrm(ks[20], (N_ODD, HEAD_DIM), 0.02),
        'gdn_w_out': nrm(ks[21], (N_ODD, C_DIM, D), C_DIM ** -0.5),
    }


def _fwd_reference(x, c, ctx, c_ctx, w_mod, b_mod, norm_g, ffn_wg, ffn_wu, ffn_wd,
              ab_w_in, ab_q_norm, ab_k_norm, pool_w, pool_scale, ab_w_out,
              gdn_w_in, gdn_conv_w, gdn_a_log, gdn_dt_bias, gdn_o_norm, gdn_w_out):
    rows = x.shape[1] // GRID_W
    cos, sin = axial_rope_tables(rows)
    h = ctx
    for i in range(DEPTH):
        last = i == DEPTH - 1
        mx = adaln(c, w_mod[i], b_mod[i])
        mh = adaln(c_ctx[None], w_mod[i], b_mod[i])
        x = x + 0.5 * mx[:, :, 2] * swiglu(modulate(x, norm_g[i, 0], mx, 0), ffn_wg[i, 0], ffn_wu[i, 0], ffn_wd[i, 0])
        h = h + 0.5 * mh[:, :, 2] * swiglu(modulate(h, norm_g[i, 0], mh, 0), ffn_wg[i, 0], ffn_wu[i, 0], ffn_wd[i, 0])
        xn = modulate(x, norm_g[i, 1], mx, 1)
        hn = modulate(h, norm_g[i, 1], mh, 1)
        j = i // 2
        if i % 2 == 0:
            yx, yh = mix_attn_pool(xn, hn, cos, sin, ab_w_in[j], ab_q_norm[j], ab_k_norm[j],
                                   pool_w[j], pool_scale[j], ab_w_out[j], not last)
        else:
            yx, yh = mix_gated_delta(xn, hn, gdn_w_in[j], gdn_conv_w[j], gdn_a_log[j], gdn_dt_bias[j],
                                     gdn_o_norm[j], gdn_w_out[j], not last)
        x = x + mx[:, :, 5] * yx
        x = x + 0.5 * mx[:, :, 8] * swiglu(modulate(x, norm_g[i, 2], mx, 2), ffn_wg[i, 1], ffn_wu[i, 1], ffn_wd[i, 1])
        if not last:
            h = h + mh[:, :, 5] * yh
            h = h + 0.5 * mh[:, :, 8] * swiglu(modulate(h, norm_g[i, 2], mh, 2), ffn_wg[i, 1], ffn_wu[i, 1], ffn_wd[i, 1])
    return x


import jax as _jax
import jax.numpy as _jnp

TWIN_FORMAT = 'train_step'
FWD_PARAMS = ['x', 'c', 'ctx', 'c_ctx', 'w_mod', 'b_mod', 'norm_g', 'ffn_wg', 'ffn_wu', 'ffn_wd', 'ab_w_in', 'ab_q_norm', 'ab_k_norm', 'pool_w', 'pool_scale', 'ab_w_out', 'gdn_w_in', 'gdn_conv_w', 'gdn_a_log', 'gdn_dt_bias', 'gdn_o_norm', 'gdn_w_out']
TWIN_WEIGHTS = ['c_ctx', 'w_mod', 'b_mod', 'norm_g', 'ffn_wg', 'ffn_wu', 'ffn_wd', 'ab_w_in', 'ab_q_norm', 'ab_k_norm', 'pool_w', 'pool_scale', 'ab_w_out', 'gdn_w_in', 'gdn_conv_w', 'gdn_a_log', 'gdn_dt_bias', 'gdn_o_norm', 'gdn_w_out']
TWIN_DIFF_INPUT = 'x'
TWIN_INPUTS = ['x', 'c', 'ctx', 'c_ctx', 'w_mod', 'b_mod', 'norm_g', 'ffn_wg', 'ffn_wu', 'ffn_wd', 'ab_w_in', 'ab_q_norm', 'ab_k_norm', 'pool_w', 'pool_scale', 'ab_w_out', 'gdn_w_in', 'gdn_conv_w', 'gdn_a_log', 'gdn_dt_bias', 'gdn_o_norm', 'gdn_w_out', 'loss_target', 'm_c_ctx', 'm_w_mod', 'm_b_mod', 'm_norm_g', 'm_ffn_wg', 'm_ffn_wu', 'm_ffn_wd', 'm_ab_w_in', 'm_ab_q_norm', 'm_ab_k_norm', 'm_pool_w', 'm_pool_scale', 'm_ab_w_out', 'm_gdn_w_in', 'm_gdn_conv_w', 'm_gdn_a_log', 'm_gdn_dt_bias', 'm_gdn_o_norm', 'm_gdn_w_out', 'v_c_ctx', 'v_w_mod', 'v_b_mod', 'v_norm_g', 'v_ffn_wg', 'v_ffn_wu', 'v_ffn_wd', 'v_ab_w_in', 'v_ab_q_norm', 'v_ab_k_norm', 'v_pool_w', 'v_pool_scale', 'v_ab_w_out', 'v_gdn_w_in', 'v_gdn_conv_w', 'v_gdn_a_log', 'v_gdn_dt_bias', 'v_gdn_o_norm', 'v_gdn_w_out']
TWIN_OUTPUTS = ['loss', 'grad_x', 'grad_c_ctx', 'grad_w_mod', 'grad_b_mod', 'grad_norm_g', 'grad_ffn_wg', 'grad_ffn_wu', 'grad_ffn_wd', 'grad_ab_w_in', 'grad_ab_q_norm', 'grad_ab_k_norm', 'grad_pool_w', 'grad_pool_scale', 'grad_ab_w_out', 'grad_gdn_w_in', 'grad_gdn_conv_w', 'grad_gdn_a_log', 'grad_gdn_dt_bias', 'grad_gdn_o_norm', 'grad_gdn_w_out', 'delta_c_ctx', 'delta_w_mod', 'delta_b_mod', 'delta_norm_g', 'delta_ffn_wg', 'delta_ffn_wu', 'delta_ffn_wd', 'delta_ab_w_in', 'delta_ab_q_norm', 'delta_ab_k_norm', 'delta_pool_w', 'delta_pool_scale', 'delta_ab_w_out', 'delta_gdn_w_in', 'delta_gdn_conv_w', 'delta_gdn_a_log', 'delta_gdn_dt_bias', 'delta_gdn_o_norm', 'delta_gdn_w_out', 'new_m_c_ctx', 'new_m_w_mod', 'new_m_b_mod', 'new_m_norm_g', 'new_m_ffn_wg', 'new_m_ffn_wu', 'new_m_ffn_wd', 'new_m_ab_w_in', 'new_m_ab_q_norm', 'new_m_ab_k_norm', 'new_m_pool_w', 'new_m_pool_scale', 'new_m_ab_w_out', 'new_m_gdn_w_in', 'new_m_gdn_conv_w', 'new_m_gdn_a_log', 'new_m_gdn_dt_bias', 'new_m_gdn_o_norm', 'new_m_gdn_w_out', 'new_v_c_ctx', 'new_v_w_mod', 'new_v_b_mod', 'new_v_norm_g', 'new_v_ffn_wg', 'new_v_ffn_wu', 'new_v_ffn_wd', 'new_v_ab_w_in', 'new_v_ab_q_norm', 'new_v_ab_k_norm', 'new_v_pool_w', 'new_v_pool_scale', 'new_v_ab_w_out', 'new_v_gdn_w_in', 'new_v_gdn_conv_w', 'new_v_gdn_a_log', 'new_v_gdn_dt_bias', 'new_v_gdn_o_norm', 'new_v_gdn_w_out']
TWIN_LEAF_KINDS = {'loss': 'loss', 'grad_x': 'grad_x', 'grad_c_ctx': 'grad_w', 'grad_w_mod': 'grad_w', 'grad_b_mod': 'grad_w', 'grad_norm_g': 'grad_w', 'grad_ffn_wg': 'grad_w', 'grad_ffn_wu': 'grad_w', 'grad_ffn_wd': 'grad_w', 'grad_ab_w_in': 'grad_w', 'grad_ab_q_norm': 'grad_w', 'grad_ab_k_norm': 'grad_w', 'grad_pool_w': 'grad_w', 'grad_pool_scale': 'grad_w', 'grad_ab_w_out': 'grad_w', 'grad_gdn_w_in': 'grad_w', 'grad_gdn_conv_w': 'grad_w', 'grad_gdn_a_log': 'grad_w', 'grad_gdn_dt_bias': 'grad_w', 'grad_gdn_o_norm': 'grad_w', 'grad_gdn_w_out': 'grad_w', 'delta_c_ctx': 'delta_w', 'delta_w_mod': 'delta_w', 'delta_b_mod': 'delta_w', 'delta_norm_g': 'delta_w', 'delta_ffn_wg': 'delta_w', 'delta_ffn_wu': 'delta_w', 'delta_ffn_wd': 'delta_w', 'delta_ab_w_in': 'delta_w', 'delta_ab_q_norm': 'delta_w', 'delta_ab_k_norm': 'delta_w', 'delta_pool_w': 'delta_w', 'delta_pool_scale': 'delta_w', 'delta_ab_w_out': 'delta_w', 'delta_gdn_w_in': 'delta_w', 'delta_gdn_conv_w': 'delta_w', 'delta_gdn_a_log': 'delta_w', 'delta_gdn_dt_bias': 'delta_w', 'delta_gdn_o_norm': 'delta_w', 'delta_gdn_w_out': 'delta_w', 'new_m_c_ctx': 'new_m', 'new_m_w_mod': 'new_m', 'new_m_b_mod': 'new_m', 'new_m_norm_g': 'new_m', 'new_m_ffn_wg': 'new_m', 'new_m_ffn_wu': 'new_m', 'new_m_ffn_wd': 'new_m', 'new_m_ab_w_in': 'new_m', 'new_m_ab_q_norm': 'new_m', 'new_m_ab_k_norm': 'new_m', 'new_m_pool_w': 'new_m', 'new_m_pool_scale': 'new_m', 'new_m_ab_w_out': 'new_m', 'new_m_gdn_w_in': 'new_m', 'new_m_gdn_conv_w': 'new_m', 'new_m_gdn_a_log': 'new_m', 'new_m_gdn_dt_bias': 'new_m', 'new_m_gdn_o_norm': 'new_m', 'new_m_gdn_w_out': 'new_m', 'new_v_c_ctx': 'new_v', 'new_v_w_mod': 'new_v', 'new_v_b_mod': 'new_v', 'new_v_norm_g': 'new_v', 'new_v_ffn_wg': 'new_v', 'new_v_ffn_wu': 'new_v', 'new_v_ffn_wd': 'new_v', 'new_v_ab_w_in': 'new_v', 'new_v_ab_q_norm': 'new_v', 'new_v_ab_k_norm': 'new_v', 'new_v_pool_w': 'new_v', 'new_v_pool_scale': 'new_v', 'new_v_ab_w_out': 'new_v', 'new_v_gdn_w_in': 'new_v', 'new_v_gdn_conv_w': 'new_v', 'new_v_gdn_a_log': 'new_v', 'new_v_gdn_dt_bias': 'new_v', 'new_v_gdn_o_norm': 'new_v', 'new_v_gdn_w_out': 'new_v'}


def _forward(args):
    return _fwd_reference(*[args[k] for k in FWD_PARAMS])


def _output_shape():
    out = _jax.eval_shape(lambda: _forward(_fwd_setup_inputs(0)))
    return out.shape, out.dtype

N_MICROBATCH = 1
ADAM_LR = 0.001
ADAM_B1 = 0.9
ADAM_B2 = 0.999
ADAM_EPS = 1e-08
ADAM_WD = 0.01
ADAM_STEP = 10
PER_EXAMPLE_BATCH_AXIS = {'x': 0, 'c': 0, 'ctx': 0, 'loss_target': 0}
SHARED_INPUTS = []
_WEIGHT_DTYPES = {'c_ctx': _jnp.float32, 'w_mod': _jnp.float32, 'b_mod': _jnp.float32, 'norm_g': _jnp.float32, 'ffn_wg': _jnp.float32, 'ffn_wu': _jnp.float32, 'ffn_wd': _jnp.float32, 'ab_w_in': _jnp.float32, 'ab_q_norm': _jnp.float32, 'ab_k_norm': _jnp.float32, 'pool_w': _jnp.float32, 'pool_scale': _jnp.float32, 'ab_w_out': _jnp.float32, 'gdn_w_in': _jnp.float32, 'gdn_conv_w': _jnp.float32, 'gdn_a_log': _jnp.float32, 'gdn_dt_bias': _jnp.float32, 'gdn_o_norm': _jnp.float32, 'gdn_w_out': _jnp.float32}
MOMENT_SCALE = {'c_ctx': 4.190889e-02, 'w_mod': 1.040953e+00, 'b_mod': 2.175695e+00, 'norm_g': 2.052031e+00, 'ffn_wg': 3.538138e-02, 'ffn_wu': 3.450030e-02, 'ffn_wd': 5.628742e-02, 'ab_w_in': 2.231771e-01, 'ab_q_norm': 3.313372e-02, 'ab_k_norm': 3.284356e-02, 'pool_w': 3.964878e-01, 'pool_scale': 5.266369e+00, 'ab_w_out': 2.552411e-01, 'gdn_w_in': 2.451475e-01, 'gdn_conv_w': 2.544678e-01, 'gdn_a_log': 9.506645e-01, 'gdn_dt_bias': 9.108736e-01, 'gdn_o_norm': 2.305778e+01, 'gdn_w_out': 2.141848e-01}


def _to_microbatches(a, axis):
    t = _jnp.moveaxis(a, axis, 0)
    t = t.reshape((N_MICROBATCH, t.shape[0] // N_MICROBATCH) + t.shape[1:])
    return _jnp.moveaxis(t, 1, axis + 1)


def setup_inputs(seed: int = 0) -> dict:
    inp = _fwd_setup_inputs(seed)
    key = _jax.random.fold_in(_jax.random.key(seed), 7919)
    shape, _ = _output_shape()
    out = dict(inp)
    out["loss_target"] = _jax.random.normal(_jax.random.fold_in(key, 0), shape, _jnp.float32)
    for i, name in enumerate(TWIN_WEIGHTS):
        w = inp[name].astype(_jnp.float32)
        if MOMENT_SCALE is None:
            s = _jnp.sqrt(_jnp.mean(_jnp.square(w)) + 1e-30)
        else:
            s = MOMENT_SCALE[name]
        km, kv = _jax.random.split(_jax.random.fold_in(key, i + 1))
        out[name] = w
        out["m_" + name] = s * _jax.random.normal(km, w.shape, _jnp.float32)
        out["v_" + name] = (s * s) * _jax.random.uniform(kv, w.shape, _jnp.float32, 0.5, 1.5)
    if N_MICROBATCH > 1:
        for name, axis in PER_EXAMPLE_BATCH_AXIS.items():
            out[name] = _to_microbatches(out[name], axis)
    return {'x': out['x'], 'c': out['c'], 'ctx': out['ctx'], 'c_ctx': out['c_ctx'], 'w_mod': out['w_mod'], 'b_mod': out['b_mod'], 'norm_g': out['norm_g'], 'ffn_wg': out['ffn_wg'], 'ffn_wu': out['ffn_wu'], 'ffn_wd': out['ffn_wd'], 'ab_w_in': out['ab_w_in'], 'ab_q_norm': out['ab_q_norm'], 'ab_k_norm': out['ab_k_norm'], 'pool_w': out['pool_w'], 'pool_scale': out['pool_scale'], 'ab_w_out': out['ab_w_out'], 'gdn_w_in': out['gdn_w_in'], 'gdn_conv_w': out['gdn_conv_w'], 'gdn_a_log': out['gdn_a_log'], 'gdn_dt_bias': out['gdn_dt_bias'], 'gdn_o_norm': out['gdn_o_norm'], 'gdn_w_out': out['gdn_w_out'], 'loss_target': out['loss_target'], 'm_c_ctx': out['m_c_ctx'], 'm_w_mod': out['m_w_mod'], 'm_b_mod': out['m_b_mod'], 'm_norm_g': out['m_norm_g'], 'm_ffn_wg': out['m_ffn_wg'], 'm_ffn_wu': out['m_ffn_wu'], 'm_ffn_wd': out['m_ffn_wd'], 'm_ab_w_in': out['m_ab_w_in'], 'm_ab_q_norm': out['m_ab_q_norm'], 'm_ab_k_norm': out['m_ab_k_norm'], 'm_pool_w': out['m_pool_w'], 'm_pool_scale': out['m_pool_scale'], 'm_ab_w_out': out['m_ab_w_out'], 'm_gdn_w_in': out['m_gdn_w_in'], 'm_gdn_conv_w': out['m_gdn_conv_w'], 'm_gdn_a_log': out['m_gdn_a_log'], 'm_gdn_dt_bias': out['m_gdn_dt_bias'], 'm_gdn_o_norm': out['m_gdn_o_norm'], 'm_gdn_w_out': out['m_gdn_w_out'], 'v_c_ctx': out['v_c_ctx'], 'v_w_mod': out['v_w_mod'], 'v_b_mod': out['v_b_mod'], 'v_norm_g': out['v_norm_g'], 'v_ffn_wg': out['v_ffn_wg'], 'v_ffn_wu': out['v_ffn_wu'], 'v_ffn_wd': out['v_ffn_wd'], 'v_ab_w_in': out['v_ab_w_in'], 'v_ab_q_norm': out['v_ab_q_norm'], 'v_ab_k_norm': out['v_ab_k_norm'], 'v_pool_w': out['v_pool_w'], 'v_pool_scale': out['v_pool_scale'], 'v_ab_w_out': out['v_ab_w_out'], 'v_gdn_w_in': out['v_gdn_w_in'], 'v_gdn_conv_w': out['v_gdn_conv_w'], 'v_gdn_a_log': out['v_gdn_a_log'], 'v_gdn_dt_bias': out['v_gdn_dt_bias'], 'v_gdn_o_norm': out['v_gdn_o_norm'], 'v_gdn_w_out': out['v_gdn_w_out']}


def _loss(weights, diff, rest, loss_target):
    with _jax.named_scope("forward"):
        args = {**rest, TWIN_DIFF_INPUT: diff, **{k: w.astype(_WEIGHT_DTYPES[k]) for k, w in weights.items()}}
        y = _forward(args)
    with _jax.named_scope("loss_head"):
        err = _jnp.square(y.astype(_jnp.float32) - loss_target)
        return 0.5 * _jnp.sum(_jnp.mean(err, axis=-1)) if err.ndim else 0.5 * err


def _adamw(w, g, m, v):
    m = ADAM_B1 * m + (1.0 - ADAM_B1) * g
    v = ADAM_B2 * v + (1.0 - ADAM_B2) * _jnp.square(g)
    m_hat = m / (1.0 - ADAM_B1 ** ADAM_STEP)
    v_hat = v / (1.0 - ADAM_B2 ** ADAM_STEP)
    delta = -ADAM_LR * (m_hat / (_jnp.sqrt(v_hat) + ADAM_EPS) + ADAM_WD * w)
    return delta, m, v


def reference(x, c, ctx, c_ctx, w_mod, b_mod, norm_g, ffn_wg, ffn_wu, ffn_wd, ab_w_in, ab_q_norm, ab_k_norm, pool_w, pool_scale, ab_w_out, gdn_w_in, gdn_conv_w, gdn_a_log, gdn_dt_bias, gdn_o_norm, gdn_w_out, loss_target, m_c_ctx, m_w_mod, m_b_mod, m_norm_g, m_ffn_wg, m_ffn_wu, m_ffn_wd, m_ab_w_in, m_ab_q_norm, m_ab_k_norm, m_pool_w, m_pool_scale, m_ab_w_out, m_gdn_w_in, m_gdn_conv_w, m_gdn_a_log, m_gdn_dt_bias, m_gdn_o_norm, m_gdn_w_out, v_c_ctx, v_w_mod, v_b_mod, v_norm_g, v_ffn_wg, v_ffn_wu, v_ffn_wd, v_ab_w_in, v_ab_q_norm, v_ab_k_norm, v_pool_w, v_pool_scale, v_ab_w_out, v_gdn_w_in, v_gdn_conv_w, v_gdn_a_log, v_gdn_dt_bias, v_gdn_o_norm, v_gdn_w_out):
    given = dict(x=x, c=c, ctx=ctx, c_ctx=c_ctx, w_mod=w_mod, b_mod=b_mod, norm_g=norm_g, ffn_wg=ffn_wg, ffn_wu=ffn_wu, ffn_wd=ffn_wd, ab_w_in=ab_w_in, ab_q_norm=ab_q_norm, ab_k_norm=ab_k_norm, pool_w=pool_w, pool_scale=pool_scale, ab_w_out=ab_w_out, gdn_w_in=gdn_w_in, gdn_conv_w=gdn_conv_w, gdn_a_log=gdn_a_log, gdn_dt_bias=gdn_dt_bias, gdn_o_norm=gdn_o_norm, gdn_w_out=gdn_w_out, loss_target=loss_target, m_c_ctx=m_c_ctx, m_w_mod=m_w_mod, m_b_mod=m_b_mod, m_norm_g=m_norm_g, m_ffn_wg=m_ffn_wg, m_ffn_wu=m_ffn_wu, m_ffn_wd=m_ffn_wd, m_ab_w_in=m_ab_w_in, m_ab_q_norm=m_ab_q_norm, m_ab_k_norm=m_ab_k_norm, m_pool_w=m_pool_w, m_pool_scale=m_pool_scale, m_ab_w_out=m_ab_w_out, m_gdn_w_in=m_gdn_w_in, m_gdn_conv_w=m_gdn_conv_w, m_gdn_a_log=m_gdn_a_log, m_gdn_dt_bias=m_gdn_dt_bias, m_gdn_o_norm=m_gdn_o_norm, m_gdn_w_out=m_gdn_w_out, v_c_ctx=v_c_ctx, v_w_mod=v_w_mod, v_b_mod=v_b_mod, v_norm_g=v_norm_g, v_ffn_wg=v_ffn_wg, v_ffn_wu=v_ffn_wu, v_ffn_wd=v_ffn_wd, v_ab_w_in=v_ab_w_in, v_ab_q_norm=v_ab_q_norm, v_ab_k_norm=v_ab_k_norm, v_pool_w=v_pool_w, v_pool_scale=v_pool_scale, v_ab_w_out=v_ab_w_out, v_gdn_w_in=v_gdn_w_in, v_gdn_conv_w=v_gdn_conv_w, v_gdn_a_log=v_gdn_a_log, v_gdn_dt_bias=v_gdn_dt_bias, v_gdn_o_norm=v_gdn_o_norm, v_gdn_w_out=v_gdn_w_out)
    weights = {n: given[n] for n in TWIN_WEIGHTS}
    shared = {n: given[n] for n in SHARED_INPUTS}
    per_example = {n: given[n] for n in ['x', 'c', 'ctx']}
    grad_fn = _jax.value_and_grad(_loss, argnums=(0, 1))

    def one_microbatch(ex, loss_target):
        ex = dict(ex)
        diff = ex.pop(TWIN_DIFF_INPUT)
        return grad_fn(weights, diff, {**shared, **ex}, loss_target)

    if N_MICROBATCH == 1:
        loss, (grad_w, grad_x) = one_microbatch(per_example, given["loss_target"])
    else:
        def body(carry, xs):
            loss_sum, grad_sum = carry
            l_k, (gw_k, gx_k) = one_microbatch(xs[0], xs[1])
            with _jax.named_scope("update"):
                return (loss_sum + l_k, _jax.tree.map(_jnp.add, grad_sum, gw_k)), gx_k

        init = (_jnp.zeros((), _jnp.float32), _jax.tree.map(_jnp.zeros_like, weights))
        (loss, grad_w), grad_x = _jax.lax.scan(body, init, (per_example, given["loss_target"]))
    with _jax.named_scope("update"):
        delta_w, new_m, new_v = {}, {}, {}
        for n in TWIN_WEIGHTS:
            delta_w[n], new_m[n], new_v[n] = _adamw(weights[n], grad_w[n], given["m_" + n], given["v_" + n])
    return (loss, grad_x, *[grad_w[n] for n in TWIN_WEIGHTS], *[delta_w[n] for n in TWIN_WEIGHTS],
            *[new_m[n] for n in TWIN_WEIGHTS], *[new_v[n] for n in TWIN_WEIGHTS])
```

```python
import functools
import math
from typing import NamedTuple

import jax
import jax.numpy as jnp
from jax import lax
from jax.experimental import pallas as pl
from jax.experimental.pallas import tpu as pltpu

F32, BF16 = jnp.float32, jnp.bfloat16
EPS = 1e-6
HEAD = 128
CHUNK = 64
GRID_W = 64
ROPE_THETA = 10000.0
POOL_WINDOWS = (2, 4, 8, 16)
A_HEADS, A_KV = 4, 2
C_HEADS = 8
N_DEV = 8
AXES = ("x", "y", "c")
ADAM_LR, ADAM_B1, ADAM_B2, ADAM_EPS, ADAM_WD, ADAM_STEP = 0.001, 0.9, 0.999, 1e-08, 0.01, 10
LANE = 128
VMEM_LIMIT = 56 * 1024 * 1024
HI = lax.Precision.HIGHEST
NEG = -1e30


def _cp():
    return pltpu.CompilerParams(vmem_limit_bytes=VMEM_LIMIT)


def _sds(shape, dtype):
    return jax.ShapeDtypeStruct(tuple(shape), dtype)


def _dot(a, b):
    return jnp.dot(a.astype(BF16), b.astype(BF16), preferred_element_type=F32)


def _dot_nt(a, b):
    return lax.dot_general(a.astype(BF16), b.astype(BF16), (((1,), (1,)), ((), ())), preferred_element_type=F32)


def _dot_tn(a, b):
    return lax.dot_general(a.astype(BF16), b.astype(BF16), (((0,), (0,)), ((), ())), preferred_element_type=F32)


def _hdot(a, b):
    return jnp.dot(a, b, preferred_element_type=F32, precision=HI)


def _hdot_nt(a, b):
    return lax.dot_general(a, b, (((1,), (1,)), ((), ())), preferred_element_type=F32, precision=HI)


def _hdot_tn(a, b):
    return lax.dot_general(a, b, (((0,), (0,)), ((), ())), preferred_element_type=F32, precision=HI)


def _pick(n, cap):
    if n <= cap:
        return n
    best = None
    for t in range(LANE, cap + 1, LANE):
        if n % t == 0:
            best = t
    assert best is not None, (n, cap)
    return best


class Dims(NamedTuple):
    Bl: int
    N: int
    M: int
    D: int
    F: int

    @property
    def TM(self):
        return min(256, self.M)

    @property
    def Tx(self):
        return self.Bl * self.N

    @property
    def Th(self):
        return self.Bl * self.M

    @property
    def T(self):
        return self.Tx + self.Th

    @property
    def ntx(self):
        return self.Tx // self.TM

    @property
    def nt(self):
        return self.T // self.TM

    @property
    def tps(self):
        return self.N // self.TM

    @property
    def G(self):
        return self.Bl + 1


def _grp(i, dm):
    return jnp.where(i < dm.ntx, i // dm.tps, dm.Bl)


def _first_of_group(i, dm):
    return jnp.where(i < dm.ntx, i % dm.tps == 0, i == dm.ntx)


def _acc(ref, idx, val, first):
    @pl.when(first)
    def _():
        ref[idx] = val

    @pl.when(jnp.logical_not(first))
    def _():
        ref[idx] += val


def _modulate(x, gain, shift, scale):
    y = x * lax.rsqrt(jnp.mean(x * x, axis=-1, keepdims=True) + EPS)
    return (y * gain) * (1.0 + scale) + shift


def _silu(x):
    return x * jax.nn.sigmoid(x)


@functools.partial(jax.custom_vjp, nondiff_argnums=(1,))
def _shift_rows(a, k):
    n = a.shape[0]
    if k == 0:
        return a
    r = lax.broadcasted_iota(jnp.int32, a.shape, 0)
    rolled = pltpu.roll(a, (-k) % n, 0)
    ok = (r + k >= 0) & (r + k < n)
    return jnp.where(ok, rolled, 0.0)


def _shift_rows_fwd(a, k):
    return _shift_rows(a, k), None


def _shift_rows_bwd(k, _, d):
    return (_shift_rows(d, -k),)


_shift_rows.defvjp(_shift_rows_fwd, _shift_rows_bwd)


@functools.partial(jax.custom_vjp, nondiff_argnums=(1,))
def _roll_lanes(a, s):
    return pltpu.roll(a, s % LANE, 1)


def _roll_lanes_fwd(a, s):
    return _roll_lanes(a, s), None


def _roll_lanes_bwd(s, _, d):
    return (_roll_lanes(d, -s),)


_roll_lanes.defvjp(_roll_lanes_fwd, _roll_lanes_bwd)


def _rope(t, cs, sneg, spos):
    return t * cs + _roll_lanes(t, 96) * sneg + _roll_lanes(t, 32) * spos


def _rope_tables(dm):
    rows = dm.N // GRID_W
    row = jnp.repeat(jnp.arange(rows), GRID_W).astype(F32)
    col = jnp.tile(jnp.arange(GRID_W), rows).astype(F32)
    half = HEAD // 2
    inv_freq = jnp.power(ROPE_THETA, -jnp.arange(0, half, 2, dtype=F32) / half)
    ar, ac = row[:, None] * inv_freq, col[:, None] * inv_freq
    cs = jnp.concatenate([jnp.cos(ar), jnp.cos(ar), jnp.cos(ac), jnp.cos(ac)], axis=1)
    z = jnp.zeros_like(ar)
    sneg = jnp.concatenate([-jnp.sin(ar), z, -jnp.sin(ac), z], axis=1)
    spos = jnp.concatenate([z, jnp.sin(ar), z, jnp.sin(ac)], axis=1)
    pad1 = jnp.ones((dm.TM, HEAD), F32)
    pad0 = jnp.zeros((dm.TM, HEAD), F32)
    return (jnp.concatenate([cs, pad1], 0), jnp.concatenate([sneg, pad0], 0), jnp.concatenate([spos, pad0], 0))


def all_gather(xs, name):
    R, C = xs.shape

    def body(x_ref, out_ref, send_sems, recv_sems, local_sem):
        x, y, c = lax.axis_index("x"), lax.axis_index("y"), lax.axis_index("c")
        me, sibling = (x, y, c), (x, y, 1 - c)
        chips = [(1 - x, y), (x, 1 - y), (1 - x, 1 - y)]

        def slot(px, py, pc):
            return out_ref.at[4 * px + 2 * py + pc]

        def copy(k, block, to, src=None):
            return pltpu.make_async_remote_copy(
                src_ref=slot(*block) if src is None else src, dst_ref=slot(*block),
                send_sem=send_sems.at[k], recv_sem=recv_sems.at[k],
                device_id=to, device_id_type=pl.DeviceIdType.MESH)

        mine = pltpu.make_async_copy(x_ref, slot(*me), local_sem)
        mine.start()
        first = [copy(0, me, sibling, src=x_ref)]
        first += [copy(1 + j, me, (*chip, c), src=x_ref) for j, chip in enumerate(chips)]
        for cp in first:
            cp.start()
        passed = [copy(4 + j, (*chip, c), sibling) for j, chip in enumerate(chips)]
        for j, chip in enumerate(chips):
            copy(1 + j, (*chip, c), me).wait_recv()
            passed[j].start()
        copy(0, sibling, me).wait_recv()
        for j, chip in enumerate(chips):
            copy(4 + j, (*chip, 1 - c), me).wait_recv()
        for cp in first + passed:
            cp.wait_send()
        mine.wait()

    return pl.pallas_call(
        body, name=name, out_shape=_sds((N_DEV, R, C), xs.dtype),
        in_specs=[pl.BlockSpec(memory_space=pl.ANY)], out_specs=pl.BlockSpec(memory_space=pl.ANY),
        scratch_shapes=[pltpu.SemaphoreType.DMA((7,)), pltpu.SemaphoreType.DMA((7,)), pltpu.SemaphoreType.DMA],
    )(xs)


def scatter_blocks(xs, name):
    _, R, C = xs.shape
    flips = [(0, 0, 1), (0, 1, 0), (0, 1, 1), (1, 0, 0), (1, 0, 1), (1, 1, 0), (1, 1, 1)]

    def body(x_ref, out_ref, send_sems, recv_sems, local_sem):
        x, y, c = lax.axis_index("x"), lax.axis_index("y"), lax.axis_index("c")
        me = 4 * x + 2 * y + c

        def peer(f):
            return tuple(1 - v if d else v for v, d in zip((x, y, c), f))

        def lin(p):
            return 4 * p[0] + 2 * p[1] + p[2]

        mine = pltpu.make_async_copy(x_ref.at[me], out_ref.at[me], local_sem)
        mine.start()
        copies = []
        for k, f in enumerate(flips):
            p = peer(f)
            copies.append(pltpu.make_async_remote_copy(
                src_ref=x_ref.at[lin(p)], dst_ref=out_ref.at[me],
                send_sem=send_sems.at[k], recv_sem=recv_sems.at[k],
                device_id=p, device_id_type=pl.DeviceIdType.MESH))
        for cp in copies:
            cp.start()
        for cp in copies:
            cp.wait_send()
            cp.wait_recv()
        mine.wait()

    return pl.pallas_call(
        body, name=name, out_shape=_sds((N_DEV, R, C), xs.dtype),
        in_specs=[pl.BlockSpec(memory_space=pl.ANY)], out_specs=pl.BlockSpec(memory_space=pl.ANY),
        scratch_shapes=[pltpu.SemaphoreType.DMA((7,)), pltpu.SemaphoreType.DMA((7,)), pltpu.SemaphoreType.DMA],
    )(xs)


def _mod_spec(dm, nidx):
    if nidx == 1:
        return pl.BlockSpec((1, 9, dm.D), lambda i: (_grp(i, dm), 0, 0))
    return pl.BlockSpec((1, 9, dm.D), lambda i, k: (_grp(i, dm), 0, 0))


def ffn_fwd(X, MOD, gain, wg, wu, wd, s0, dm, ntiles, name):
    TM, D, F = dm.TM, dm.D, dm.F
    FK = _pick(F, 512)
    nk = F // FK
    rows = ntiles * TM

    def body(x_ref, m_ref, g_ref, wg_ref, wu_ref, wd_ref, xo_ref, y_ref, xn_s, acc_s):
        k = pl.program_id(1)

        @pl.when(k == 0)
        def _():
            m = m_ref[0]
            xn = _modulate(x_ref[...], g_ref[...], m[s0:s0 + 1], m[s0 + 1:s0 + 2])
            xn_s[...] = xn.astype(BF16)
            acc_s[...] = jnp.zeros_like(acc_s)

        xn = xn_s[...]
        g = jnp.dot(xn, wg_ref[...], preferred_element_type=F32)
        u = jnp.dot(xn, wu_ref[...], preferred_element_type=F32)
        h = _silu(g) * u
        acc_s[...] += jnp.dot(h.astype(BF16), wd_ref[...], preferred_element_type=F32)

        @pl.when(k == nk - 1)
        def _():
            m = m_ref[0]
            y = acc_s[...]
            y_ref[...] = y
            xo_ref[...] = x_ref[...] + (0.5 * m[s0 + 2:s0 + 3]) * y

    row = pl.BlockSpec((TM, D), lambda i, k: (i, 0))
    return pl.pallas_call(
        body, name=name, grid=(ntiles, nk),
        in_specs=[row, _mod_spec(dm, 2), pl.BlockSpec((1, D), lambda i, k: (0, 0)),
                  pl.BlockSpec((D, FK), lambda i, k: (0, k)), pl.BlockSpec((D, FK), lambda i, k: (0, k)),
                  pl.BlockSpec((FK, D), lambda i, k: (k, 0))],
        out_specs=[row, row],
        out_shape=[_sds((rows, D), F32), _sds((rows, D), F32)],
        scratch_shapes=[pltpu.VMEM((TM, D), BF16), pltpu.VMEM((TM, D), F32)],
        compiler_params=_cp(),
    )(X, MOD, gain, wg, wu, wd)


def ffn_bwd(X, dXo, Y, MOD, gain, wg, wu, wdT, wgT, wuT, s0, dm, ntiles, name):
    TM, D, F = dm.TM, dm.D, dm.F
    FK = _pick(F, 512)
    nk = F // FK
    rows = ntiles * TM
    ngr = dm.G if ntiles == dm.nt else dm.Bl

    def body(x_ref, dxo_ref, y_ref, m_ref, g_ref, wg_ref, wu_ref, wdT_ref, wgT_ref, wuT_ref,
             dxi_ref, xn_ref, do_ref, h_ref, dg_ref, du_ref, dm_ref, dgain_ref, xn_s, do_s, dxn_s):
        i, k = pl.program_id(0), pl.program_id(1)

        @pl.when(k == 0)
        def _():
            m = m_ref[0]
            xn = _modulate(x_ref[...], g_ref[...], m[s0:s0 + 1], m[s0 + 1:s0 + 2])
            xn_s[...] = xn.astype(BF16)
            do_s[...] = ((0.5 * m[s0 + 2:s0 + 3]) * dxo_ref[...]).astype(BF16)
            dxn_s[...] = jnp.zeros_like(dxn_s)

        xn = xn_s[...]
        g = jnp.dot(xn, wg_ref[...], preferred_element_type=F32)
        u = jnp.dot(xn, wu_ref[...], preferred_element_type=F32)
        sg = jax.nn.sigmoid(g)
        si = g * sg
        dh = jnp.dot(do_s[...], wdT_ref[...], preferred_element_type=F32)
        dg = (dh * u * (sg * (1.0 + g * (1.0 - sg)))).astype(BF16)
        du = (dh * si).astype(BF16)
        dxn_s[...] += (jnp.dot(dg, wgT_ref[...], preferred_element_type=F32)
                       + jnp.dot(du, wuT_ref[...], preferred_element_type=F32))
        h_ref[...] = (si * u).astype(BF16)
        dg_ref[...] = dg
        du_ref[...] = du

        @pl.when(k == nk - 1)
        def _():
            m = m_ref[0]
            _, vjp = jax.vjp(_modulate, x_ref[...], g_ref[...], m[s0:s0 + 1], m[s0 + 1:s0 + 2])
            dx, dgain, dshift, dscale = vjp(dxn_s[...])
            dxo = dxo_ref[...]
            dxi_ref[...] = dxo + dx
            xn_ref[...] = xn_s[...]
            do_ref[...] = do_s[...]
            dgate = jnp.sum(0.5 * dxo * y_ref[...], axis=0, keepdims=True)
            first = _first_of_group(i, dm)
            _acc(dm_ref, (0, pl.ds(0, 1), slice(None)), dshift, first)
            _acc(dm_ref, (0, pl.ds(1, 1), slice(None)), dscale, first)
            _acc(dm_ref, (0, pl.ds(2, 1), slice(None)), dgate, first)
            _acc(dgain_ref, (slice(None), slice(None)), dgain, i == 0)

    row = pl.BlockSpec((TM, D), lambda i, k: (i, 0))
    wide = pl.BlockSpec((TM, FK), lambda i, k: (i, k))
    return pl.pallas_call(
        body, name=name, grid=(ntiles, nk),
        in_specs=[row, row, row, _mod_spec(dm, 2), pl.BlockSpec((1, D), lambda i, k: (0, 0)),
                  pl.BlockSpec((D, FK), lambda i, k: (0, k)), pl.BlockSpec((D, FK), lambda i, k: (0, k)),
                  pl.BlockSpec((D, FK), lambda i, k: (0, k)),
                  pl.BlockSpec((FK, D), lambda i, k: (k, 0)), pl.BlockSpec((FK, D), lambda i, k: (k, 0))],
        out_specs=[row, row, row, wide, wide, wide,
                   pl.BlockSpec((1, 3, D), lambda i, k: (_grp(i, dm), 0, 0)),
                   pl.BlockSpec((1, D), lambda i, k: (0, 0))],
        out_shape=[_sds((rows, D), F32), _sds((rows, D), BF16), _sds((rows, D), BF16),
                   _sds((rows, F), BF16), _sds((rows, F), BF16), _sds((rows, F), BF16),
                   _sds((ngr, 3, D), F32), _sds((1, D), F32)],
        scratch_shapes=[pltpu.VMEM((TM, D), BF16), pltpu.VMEM((TM, D), BF16), pltpu.VMEM((TM, D), F32)],
        compiler_params=_cp(),
    )(X, dXo, Y, MOD, gain, wg, wu, wdT, wgT, wuT)


def atb(A, B, rows, tt, name):
    Ka, Nb = A.shape[1], B.shape[1]
    tk, tn = _pick(Ka, 1024), _pick(Nb, 1536)
    nT = rows // tt

    def body(a_ref, b_ref, o_ref):
        @pl.when(pl.program_id(2) == 0)
        def _():
            o_ref[...] = jnp.zeros_like(o_ref)

        o_ref[...] += _dot_tn(a_ref[...], b_ref[...])

    return pl.pallas_call(
        body, name=name, grid=(Ka // tk, Nb // tn, nT),
        in_specs=[pl.BlockSpec((tt, tk), lambda i, j, t: (t, i)), pl.BlockSpec((tt, tn), lambda i, j, t: (t, j))],
        out_specs=pl.BlockSpec((tk, tn), lambda i, j, t: (i, j)),
        out_shape=_sds((Ka, Nb), F32), compiler_params=_cp(),
    )(A, B)


def modmm(X, MOD, gain, W, s0, dm, name):
    TM, D = dm.TM, dm.D
    Nc = W.shape[1]
    tn = _pick(Nc, 1536)
    nj = Nc // tn

    def body(x_ref, m_ref, g_ref, w_ref, p_ref, xn_ref):
        @pl.when(pl.program_id(1) == 0)
        def _():
            m = m_ref[0]
            xn_ref[...] = _modulate(x_ref[...], g_ref[...], m[s0:s0 + 1], m[s0 + 1:s0 + 2]).astype(BF16)

        p_ref[...] = jnp.dot(xn_ref[...], w_ref[...], preferred_element_type=F32)

    row = pl.BlockSpec((TM, D), lambda i, j: (i, 0))
    return pl.pallas_call(
        body, name=name, grid=(dm.nt, nj),
        in_specs=[row, _mod_spec(dm, 2), pl.BlockSpec((1, D), lambda i, j: (0, 0)),
                  pl.BlockSpec((D, tn), lambda i, j: (0, j))],
        out_specs=[pl.BlockSpec((TM, tn), lambda i, j: (i, j)), row],
        out_shape=[_sds((dm.T, Nc), F32), _sds((dm.T, D), BF16)],
        compiler_params=_cp(),
    )(X, MOD, gain, W)


def mixin_bwd(dP, WT, X, dXres, MOD, gain, s0, dm, name):
    TM, D = dm.TM, dm.D
    K = dP.shape[1]

    def body(dp_ref, wt_ref, x_ref, dr_ref, m_ref, g_ref, dx_ref, dm_ref, dgain_ref):
        i = pl.program_id(0)
        dxn = jnp.dot(dp_ref[...], wt_ref[...], preferred_element_type=F32)
        m = m_ref[0]
        _, vjp = jax.vjp(_modulate, x_ref[...], g_ref[...], m[s0:s0 + 1], m[s0 + 1:s0 + 2])
        dx, dgain, dshift, dscale = vjp(dxn)
        dx_ref[...] = dr_ref[...] + dx
        first = _first_of_group(i, dm)
        _acc(dm_ref, (0, pl.ds(0, 1), slice(None)), dshift, first)
        _acc(dm_ref, (0, pl.ds(1, 1), slice(None)), dscale, first)
        _acc(dgain_ref, (slice(None), slice(None)), dgain, i == 0)

    row = pl.BlockSpec((TM, D), lambda i: (i, 0))
    return pl.pallas_call(
        body, name=name, grid=(dm.nt,),
        in_specs=[pl.BlockSpec((TM, K), lambda i: (i, 0)), pl.BlockSpec((K, D), lambda i: (0, 0)), row, row,
                  _mod_spec(dm, 1), pl.BlockSpec((1, D), lambda i: (0, 0))],
        out_specs=[row, pl.BlockSpec((1, 2, D), lambda i: (_grp(i, dm), 0, 0)), pl.BlockSpec((1, D), lambda i: (0, 0))],
        out_shape=[_sds((dm.T, D), F32), _sds((dm.G, 2, D), F32), _sds((1, D), F32)],
        compiler_params=_cp(),
    )(dP, WT, X, dXres, MOD, gain)


def proj_res(As, Ws, X, MOD, dm, ntiles, name):
    TM, D = dm.TM, dm.D
    n = len(As)
    rows = ntiles * TM

    def body(*refs):
        a_refs, w_refs = refs[:n], refs[n:2 * n]
        x_ref, m_ref, xo_ref, y_ref = refs[2 * n:]
        y = jnp.dot(a_refs[0][...], w_refs[0][...], preferred_element_type=F32)
        for a, w in zip(a_refs[1:], w_refs[1:]):
            y += jnp.dot(a[...], w[...], preferred_element_type=F32)
        y_ref[...] = y
        xo_ref[...] = x_ref[...] + m_ref[0][5:6] * y

    row = pl.BlockSpec((TM, D), lambda i: (i, 0))
    return pl.pallas_call(
        body, name=name, grid=(ntiles,),
        in_specs=[pl.BlockSpec((TM, a.shape[1]), lambda i: (i, 0)) for a in As]
        + [pl.BlockSpec(w.shape, lambda i: (0, 0)) for w in Ws] + [row, _mod_spec(dm, 1)],
        out_specs=[row, row], out_shape=[_sds((rows, D), F32), _sds((rows, D), F32)],
        compiler_params=_cp(),
    )(*As, *Ws, X, MOD)


def proj_res_bwd(dXo, Y, MOD, WTs, dm, ntiles, name):
    TM, D = dm.TM, dm.D
    n = len(WTs)
    rows = ntiles * TM
    ngr = dm.G if ntiles == dm.nt else dm.Bl

    def body(*refs):
        dxo_ref, y_ref, m_ref = refs[:3]
        wt_refs = refs[3:3 + n]
        dy_ref = refs[3 + n]
        da_refs = refs[4 + n:4 + 2 * n]
        dgate_ref = refs[4 + 2 * n]
        i = pl.program_id(0)
        dxo = dxo_ref[...]
        dy = (m_ref[0][5:6] * dxo).astype(BF16)
        dy_ref[...] = dy
        for wt, da in zip(wt_refs, da_refs):
            da[...] = jnp.dot(dy, wt[...], preferred_element_type=F32)
        dgate = jnp.sum(dxo * y_ref[...], axis=0, keepdims=True)
        _acc(dgate_ref, (0, slice(None), slice(None)), dgate, _first_of_group(i, dm))

    row = pl.BlockSpec((TM, D), lambda i: (i, 0))
    return pl.pallas_call(
        body, name=name, grid=(ntiles,),
        in_specs=[row, row, _mod_spec(dm, 1)] + [pl.BlockSpec(w.shape, lambda i: (0, 0)) for w in WTs],
        out_specs=[row] + [pl.BlockSpec((TM, w.shape[1]), lambda i: (i, 0)) for w in WTs]
        + [pl.BlockSpec((1, 1, D), lambda i: (_grp(i, dm), 0, 0))],
        out_shape=[_sds((rows, D), BF16)] + [_sds((rows, w.shape[1]), F32) for w in WTs] + [_sds((ngr, 1, D), F32)],
        compiler_params=_cp(),
    )(dXo, Y, MOD, *WTs)


def loss_head(Xf, target, dm, name):
    TM, D = dm.TM, dm.D

    def body(x_ref, t_ref, l_ref, dx_ref, acc_s):
        i = pl.program_id(0)
        e = x_ref[...] - t_ref[...]
        dx_ref[...] = e * (1.0 / D)

        @pl.when(i == 0)
        def _():
            acc_s[...] = jnp.zeros_like(acc_s)

        acc_s[...] += jnp.sum(e * e, axis=0, keepdims=True)

        @pl.when(i == dm.ntx - 1)
        def _():
            tot = jnp.sum(acc_s[...], axis=1, keepdims=True) * (0.5 / D)
            l_ref[...] = jnp.broadcast_to(tot, (1, LANE))

    row = pl.BlockSpec((TM, D), lambda i: (i, 0))
    return pl.pallas_call(
        body, name=name, grid=(dm.ntx,), in_specs=[row, row],
        out_specs=[pl.BlockSpec((1, LANE), lambda i: (0, 0)), row],
        out_shape=[_sds((1, LANE), F32), _sds((dm.Tx, D), F32)],
        scratch_shapes=[pltpu.VMEM((1, D), F32)], compiler_params=_cp(),
    )(Xf, target)


def _qk_fn(p, gain, cs, sneg, spos):
    y = p * lax.rsqrt(jnp.mean(p * p, axis=-1, keepdims=True) + EPS) * gain
    return _rope(y, cs, sneg, spos)


def _tab_specs(dm, swap):
    def idx(i):
        return jnp.where(i < dm.ntx, i % dm.tps, dm.tps)
    if swap:
        return [pl.BlockSpec((dm.TM, HEAD), lambda j, i: (idx(i), 0))] * 3
    return [pl.BlockSpec((dm.TM, HEAD), lambda i, j: (idx(i), 0))] * 3


def qkv_prep(P0, qkg, tabs, dm, name):
    TM = dm.TM

    def body(p_ref, g_ref, cs_ref, sn_ref, sp_ref, o_ref):
        j = pl.program_id(1)

        @pl.when(j < 6)
        def _():
            o_ref[...] = _qk_fn(p_ref[...], g_ref[0], cs_ref[...], sn_ref[...], sp_ref[...]).astype(BF16)

        @pl.when(j >= 6)
        def _():
            o_ref[...] = p_ref[...].astype(BF16)

    blk = pl.BlockSpec((TM, HEAD), lambda i, j: (i, j))
    return pl.pallas_call(
        body, name=name, grid=(dm.nt, 8),
        in_specs=[blk, pl.BlockSpec((1, 1, HEAD), lambda i, j: (jnp.minimum(j // 4, 1), 0, 0))] + _tab_specs(dm, False),
        out_specs=blk, out_shape=_sds((dm.T, 8 * HEAD), BF16), compiler_params=_cp(),
    )(P0, qkg, *tabs)


def qkv_prep_bwd(P0, dQKV, qkg, tabs, dm, name):
    TM = dm.TM

    def body(p_ref, d_ref, g_ref, cs_ref, sn_ref, sp_ref, dp_ref, dg_ref):
        j, i = pl.program_id(0), pl.program_id(1)
        first = (i == 0) & ((j == 0) | (j == 4))

        @pl.when(j < 6)
        def _():
            _, vjp = jax.vjp(_qk_fn, p_ref[...], g_ref[0], cs_ref[...], sn_ref[...], sp_ref[...])
            dp, dg = vjp(d_ref[...])[:2]
            dp_ref[...] = dp
            _acc(dg_ref, (0, slice(None), slice(None)), dg, first)

        @pl.when(j >= 6)
        def _():
            dp_ref[...] = d_ref[...]

    blk = pl.BlockSpec((TM, HEAD), lambda j, i: (i, j))
    return pl.pallas_call(
        body, name=name, grid=(8, dm.nt),
        in_specs=[blk, blk, pl.BlockSpec((1, 1, HEAD), lambda j, i: (jnp.minimum(j // 4, 1), 0, 0))] + _tab_specs(dm, True),
        out_specs=[blk, pl.BlockSpec((1, 1, HEAD), lambda j, i: (jnp.minimum(j // 4, 1), 0, 0))],
        out_shape=[_sds((dm.T, 8 * HEAD), F32), _sds((2, 1, HEAD), F32)], compiler_params=_cp(),
    )(P0, dQKV, qkg, *tabs)


def _softmax2(sx, sh):
    m = jnp.max(sh, axis=-1, keepdims=True)
    if sx is not None:
        m = jnp.maximum(m, jnp.max(sx, axis=-1, keepdims=True))
    eh = jnp.exp(sh - m)
    l = jnp.sum(eh, axis=-1, keepdims=True)
    ex = None
    if sx is not None:
        ex = jnp.exp(sx - m)
        l = l + jnp.sum(ex, axis=-1, keepdims=True)
    inv = 1.0 / l
    return (None if ex is None else ex * inv), eh * inv


def _attn_geometry(dm, with_x):
    TQ = dm.TM
    if with_x:
        nq, qoff = dm.N // TQ, 0
    else:
        nq, qoff = dm.M // TQ, dm.Tx // TQ
    hoff = dm.Tx // dm.M
    return TQ, nq, qoff, hoff


def attn_fwd(QKV, dm, with_x, name):
    TQ, nq, qoff, hoff = _attn_geometry(dm, with_x)
    scale = HEAD ** -0.5
    rows = dm.Tx if with_x else dm.Th

    def body(*refs):
        if with_x:
            q_ref, kh_ref, vh_ref, kx_ref, vx_ref, o_ref = refs
        else:
            q_ref, kh_ref, vh_ref, o_ref = refs
        q = q_ref[...]
        sh = _dot_nt(q, kh_ref[...]) * scale
        sx = _dot_nt(q, kx_ref[...]) * scale if with_x else None
        px, ph = _softmax2(sx, sh)
        o = _dot(ph, vh_ref[...])
        if with_x:
            o = o + _dot(px, vx_ref[...])
        o_ref[...] = o.astype(BF16)

    qs = pl.BlockSpec((TQ, HEAD), lambda b, kv, g, qi: (qoff + b * nq + qi, kv * 2 + g))
    in_specs = [qs, pl.BlockSpec((dm.M, HEAD), lambda b, kv, g, qi: (hoff + b, 4 + kv)),
                pl.BlockSpec((dm.M, HEAD), lambda b, kv, g, qi: (hoff + b, 6 + kv))]
    args = [QKV, QKV, QKV]
    if with_x:
        in_specs += [pl.BlockSpec((dm.N, HEAD), lambda b, kv, g, qi: (b, 4 + kv)),
                     pl.BlockSpec((dm.N, HEAD), lambda b, kv, g, qi: (b, 6 + kv))]
        args += [QKV, QKV]
    return pl.pallas_call(
        body, name=name, grid=(dm.Bl, A_KV, 2, nq), in_specs=in_specs,
        out_specs=pl.BlockSpec((TQ, HEAD), lambda b, kv, g, qi: (b * nq + qi, kv * 2 + g)),
        out_shape=_sds((rows, A_HEADS * HEAD), BF16), compiler_params=_cp(),
    )(*args)


def attn_bwd(QKV, dO, dm, with_x, init, name):
    TQ, nq, qoff, hoff = _attn_geometry(dm, with_x)
    scale = HEAD ** -0.5
    rows = dm.Tx if with_x else dm.Th

    def body(*refs):
        if with_x:
            (q_ref, kh_ref, vh_ref, kx_ref, vx_ref, do_ref, ikh_ref, ivh_ref,
             dq_ref, dkh_ref, dvh_ref, dkx_ref, dvx_ref) = refs
        else:
            q_ref, kh_ref, vh_ref, do_ref, dq_ref, dkh_ref, dvh_ref = refs
        g, qi = pl.program_id(2), pl.program_id(3)
        q = q_ref[...]
        kh, vh = kh_ref[...], vh_ref[...]
        sh = _dot_nt(q, kh) * scale
        sx = _dot_nt(q, kx_ref[...]) * scale if with_x else None
        px, ph = _softmax2(sx, sh)
        dob = do_ref[...].astype(BF16)
        dph = _dot_nt(dob, vh)
        delta = jnp.sum(dph * ph, axis=-1, keepdims=True)
        if with_x:
            dpx = _dot_nt(dob, vx_ref[...])
            delta = delta + jnp.sum(dpx * px, axis=-1, keepdims=True)
        dsh = (ph * (dph - delta) * scale).astype(BF16)
        dq = _dot(dsh, kh)
        first = (g == 0) & (qi == 0)

        @pl.when(first)
        def _():
            if with_x:
                dkh_ref[...] = ikh_ref[...]
                dvh_ref[...] = ivh_ref[...]
                dkx_ref[...] = jnp.zeros_like(dkx_ref)
                dvx_ref[...] = jnp.zeros_like(dvx_ref)
            else:
                dkh_ref[...] = jnp.zeros_like(dkh_ref)
                dvh_ref[...] = jnp.zeros_like(dvh_ref)

        dkh_ref[...] += _dot_tn(dsh, q)
        dvh_ref[...] += _dot_tn(ph, dob)
        if with_x:
            dsx = (px * (dpx - delta) * scale).astype(BF16)
            dq = dq + _dot(dsx, kx_ref[...])
            dkx_ref[...] += _dot_tn(dsx, q)
            dvx_ref[...] += _dot_tn(px, dob)
        dq_ref[...] = dq

    qs = pl.BlockSpec((TQ, HEAD), lambda b, kv, g, qi: (qoff + b * nq + qi, kv * 2 + g))
    hs = lambda c0: pl.BlockSpec((dm.M, HEAD), lambda b, kv, g, qi: (hoff + b, c0 + kv))
    xs = lambda c0: pl.BlockSpec((dm.N, HEAD), lambda b, kv, g, qi: (b, c0 + kv))
    dos = pl.BlockSpec((TQ, HEAD), lambda b, kv, g, qi: (b * nq + qi, kv * 2 + g))
    acc_h = pl.BlockSpec((dm.M, HEAD), lambda b, kv, g, qi: (b, kv))
    acc_x = pl.BlockSpec((dm.N, HEAD), lambda b, kv, g, qi: (b, kv))
    in_specs, args = [qs, hs(4), hs(6)], [QKV, QKV, QKV]
    out_specs = [dos, acc_h, acc_h]
    out_shape = [_sds((rows, A_HEADS * HEAD), F32), _sds((dm.Th, A_KV * HEAD), F32), _sds((dm.Th, A_KV * HEAD), F32)]
    if with_x:
        in_specs += [xs(4), xs(6), dos, acc_h, acc_h]
        args += [QKV, QKV, dO, init[0], init[1]]
        out_specs += [acc_x, acc_x]
        out_shape += [_sds((dm.Tx, A_KV * HEAD), F32), _sds((dm.Tx, A_KV * HEAD), F32)]
    else:
        in_specs += [dos]
        args += [dO]
    return pl.pallas_call(
        body, name=name, grid=(dm.Bl, A_KV, 2, nq), in_specs=in_specs, out_specs=out_specs,
        out_shape=out_shape, compiler_params=_cp(),
    )(*args)


def _pool_mean(u, w):
    n = u.shape[0]
    t = lax.broadcasted_iota(jnp.int32, (n, 1), 0)
    cnt = (jnp.clip(t + (w - w // 2), 0, n) - jnp.clip(t - w // 2, 0, n)).astype(F32)
    s = _shift_rows(u, -(w // 2))
    for j in range(-(w // 2) + 1, w - w // 2):
        s = s + _shift_rows(u, j)
    return s / cnt - u


def pool_fwd(P0, pw, pscale, dm, on_x, name):
    n, off, rows = (dm.N, 0, dm.Tx) if on_x else (dm.M, dm.Tx // dm.M, dm.Th)
    ng = len(POOL_WINDOWS)

    def body(u_ref, w_ref, s_ref, o_ref):
        for g, w in enumerate(POOL_WINDOWS):
            cols = pl.ds(g * HEAD, HEAD)
            pooled = _pool_mean(u_ref[:, cols], w)
            o_ref[:, cols] = (_dot(pooled, w_ref[g]) * s_ref[:, cols]).astype(BF16)

    return pl.pallas_call(
        body, name=name, grid=(dm.Bl,),
        in_specs=[pl.BlockSpec((n, ng * HEAD), lambda b: (off + b, 2)),
                  pl.BlockSpec((ng, HEAD, HEAD), lambda b: (0, 0, 0)), pl.BlockSpec((1, ng * HEAD), lambda b: (0, 0))],
        out_specs=pl.BlockSpec((n, ng * HEAD), lambda b: (b, 0)),
        out_shape=_sds((rows, ng * HEAD), BF16), compiler_params=_cp(),
    )(P0, pw, pscale)


def pool_bwd(P0, dY, pw, pwT, pscale, dm, on_x, name):
    n, off, rows = (dm.N, 0, dm.Tx) if on_x else (dm.M, dm.Tx // dm.M, dm.Th)
    ng = len(POOL_WINDOWS)

    def body(u_ref, dy_ref, w_ref, wt_ref, s_ref, du_ref, dw_ref, ds_ref):
        b = pl.program_id(0)
        for g, w in enumerate(POOL_WINDOWS):
            cols = pl.ds(g * HEAD, HEAD)
            pooled, vjp = jax.vjp(lambda u: _pool_mean(u, w), u_ref[:, cols])
            pre = _dot(pooled, w_ref[g])
            dy = dy_ref[:, cols]
            dpre = dy * s_ref[:, cols]
            du_ref[:, cols] = vjp(_dot(dpre, wt_ref[g]))[0]
            _acc(dw_ref, (g, slice(None), slice(None)), _dot_tn(pooled, dpre), b == 0)
            _acc(ds_ref, (slice(None), cols), jnp.sum(dy * pre, axis=0, keepdims=True), b == 0)

    full = pl.BlockSpec((ng, HEAD, HEAD), lambda b: (0, 0, 0))
    vec = pl.BlockSpec((1, ng * HEAD), lambda b: (0, 0))
    return pl.pallas_call(
        body, name=name, grid=(dm.Bl,),
        in_specs=[pl.BlockSpec((n, ng * HEAD), lambda b: (off + b, 2)), pl.BlockSpec((n, ng * HEAD), lambda b: (b, 0)),
                  full, full, vec],
        out_specs=[pl.BlockSpec((n, ng * HEAD), lambda b: (b, 0)), full, vec],
        out_shape=[_sds((rows, ng * HEAD), F32), _sds((ng, HEAD, HEAD), F32), _sds((1, ng * HEAD), F32)],
        compiler_params=_cp(),
    )(P0, dY, pw, pwT, pscale)


def _conv_fn(p, w0, w1, w2, kind):
    c = w0 * _shift_rows(p, -1) + w1 * p + w2 * _shift_rows(p, 1)
    a = _silu(c)
    if kind == 2:
        return a
    a = a * lax.rsqrt(jnp.sum(a * a, axis=-1, keepdims=True) + EPS)
    return a * (HEAD ** -0.5) if kind == 0 else a


def gdn_prep(P1, conv_w, dm, on_x, name):
    n, off, rows = (dm.N, 0, dm.Tx) if on_x else (dm.M, dm.Tx // dm.M, dm.Th)

    def body(p_ref, w_ref, o_ref):
        j = pl.program_id(1)
        p, w = p_ref[...], w_ref[...]
        for kind in range(3):
            @pl.when(j // C_HEADS == kind)
            def _():
                o_ref[...] = _conv_fn(p, w[0:1], w[1:2], w[2:3], kind)

    return pl.pallas_call(
        body, name=name, grid=(dm.Bl, 3 * C_HEADS),
        in_specs=[pl.BlockSpec((n, HEAD), lambda b, j: (off + b, j)), pl.BlockSpec((3, HEAD), lambda b, j: (0, j))],
        out_specs=pl.BlockSpec((n, HEAD), lambda b, j: (b, j)),
        out_shape=_sds((rows, 3 * C_HEADS * HEAD), F32), compiler_params=_cp(),
    )(P1, conv_w)


def gdn_prep_bwd(P1, dQ, conv_w, dm, on_x, name):
    n, off, rows = (dm.N, 0, dm.Tx) if on_x else (dm.M, dm.Tx // dm.M, dm.Th)

    def body(p_ref, d_ref, w_ref, dp_ref, dw_ref):
        j, b = pl.program_id(0), pl.program_id(1)
        p, w = p_ref[...], w_ref[...]
        for kind in range(3):
            @pl.when(j // C_HEADS == kind)
            def _():
                _, vjp = jax.vjp(functools.partial(_conv_fn, kind=kind), p, w[0:1], w[1:2], w[2:3])
                dp, d0, d1, d2 = vjp(d_ref[...])
                dp_ref[...] = dp
                _acc(dw_ref, (pl.ds(0, 1), slice(None)), d0, b == 0)
                _acc(dw_ref, (pl.ds(1, 1), slice(None)), d1, b == 0)
                _acc(dw_ref, (pl.ds(2, 1), slice(None)), d2, b == 0)

    return pl.pallas_call(
        body, name=name, grid=(3 * C_HEADS, dm.Bl),
        in_specs=[pl.BlockSpec((n, HEAD), lambda j, b: (off + b, j)), pl.BlockSpec((n, HEAD), lambda j, b: (b, j)),
                  pl.BlockSpec((3, HEAD), lambda j, b: (0, j))],
        out_specs=[pl.BlockSpec((n, HEAD), lambda j, b: (b, j)), pl.BlockSpec((3, HEAD), lambda j, b: (0, j))],
        out_shape=[_sds((rows, 3 * C_HEADS * HEAD), F32), _sds((3, 3 * C_HEADS * HEAD), F32)],
        compiler_params=_cp(),
    )(P1, dQ, conv_w)


def _gate_fn(ab, par):
    lane = lax.broadcasted_iota(jnp.int32, ab.shape, 1)
    is_a = (lane % 16) < C_HEADS
    g = -jnp.exp(par[0:1]) * jax.nn.softplus(ab + par[1:2])
    return jnp.where(lane < 4 * C_HEADS, jnp.where(is_a, g, jax.nn.sigmoid(ab)), 0.0)


def _col(blk, idx):
    lane = lax.broadcasted_iota(jnp.int32, blk.shape, 1)
    return jnp.sum(jnp.where(lane == idx, blk, 0.0), axis=1, keepdims=True)


def _chunk_masks(rev):
    ii = lax.broadcasted_iota(jnp.int32, (CHUNK, CHUNK), 0)
    jj = lax.broadcasted_iota(jnp.int32, (CHUNK, CHUNK), 1)
    ahead = jnp.where(rev, jj - ii, ii - jj)
    return ahead >= 0, ahead > 0, (ii == jj).astype(F32)


def _inv_unit_tri(nmat, eye):
    x = eye - nmat
    p = _hdot(nmat, nmat)
    step = 2
    while True:
        x = x + _hdot(x, p)
        step *= 2
        if step >= CHUNK:
            break
        p = _hdot(p, p)
    return x


def _chunk_common(q, k, v, g, beta, rev):
    incl, strict, eye = _chunk_masks(rev)
    tri = incl.astype(F32)
    gb = jnp.broadcast_to(g, (CHUNK, CHUNK))
    gcol = _hdot(tri, gb)
    grow = lax.dot_general(gb, tri, (((0,), (1,)), ((), ())), preferred_element_type=F32, precision=HI)
    total = jnp.sum(g, axis=0, keepdims=True)
    e = jnp.exp(jnp.where(incl, gcol - grow, NEG))
    gc = gcol[:, 0:1]
    eg, et, gt = jnp.exp(gc), jnp.exp(total - gc), jnp.exp(total)
    kb = k * beta
    kk = _dot_nt(kb, k)
    nmat = jnp.where(strict, kk * e, 0.0)
    ainv = _inv_unit_tri(nmat, eye)
    rhs = jnp.concatenate([v * beta, kb * eg], axis=1)
    sol = _hdot(ainv, rhs)
    qq = _dot_nt(q, k)
    return dict(incl=incl, strict=strict, tri=tri, e=e, eg=eg, et=et, gt=gt, kb=kb, kk=kk, ainv=ainv,
                sol=sol, qq=qq, total=total)


def _chunk_fwd(q, k, v, g, beta, rev):
    c = _chunk_common(q, k, v, g, beta, rev)
    u, w = c["sol"][:, :HEAD], c["sol"][:, HEAD:]
    qk = jnp.where(c["incl"], c["qq"] * c["e"], 0.0)
    return u, w, k * c["et"], q * c["eg"], qk, c["gt"]


def _chunk_bwd(q, k, v, g, beta, rev, du, dw, dkt, dqd, dqk, dgt):
    c = _chunk_common(q, k, v, g, beta, rev)
    incl, strict, e, eg, et, gt, kb = c["incl"], c["strict"], c["e"], c["eg"], c["et"], c["gt"], c["kb"]
    drhs = _hdot_tn(c["ainv"], jnp.concatenate([du, dw], axis=1))
    dn = jnp.where(strict, -_hdot_nt(drhs, c["sol"]), 0.0)
    dkk = dn * e
    dqq = jnp.where(incl, dqk, 0.0) * e
    de = dn * c["kk"] + jnp.where(incl, dqk, 0.0) * c["qq"]
    dq = _dot(dqq, k) + dqd * eg
    dk = _dot_tn(dqq, q) + _dot_tn(dkk, kb) + dkt * et
    dkb = _dot(dkk, k) + drhs[:, HEAD:] * eg
    dv = drhs[:, :HEAD] * beta
    dbeta = jnp.sum(drhs[:, :HEAD] * v + dkb * k, axis=1, keepdims=True)
    dk = dk + dkb * beta
    deg = jnp.sum(drhs[:, HEAD:] * kb + dqd * q, axis=1, keepdims=True)
    dd = de * e
    dtd = jnp.sum(dkt * k, axis=1, keepdims=True) * et
    dgc = deg * eg - dtd + jnp.sum(dd, axis=1, keepdims=True) - jnp.sum(dd.T, axis=1, keepdims=True)
    dtotal = jnp.sum(dtd, axis=0, keepdims=True) + dgt * gt
    dg = _hdot_tn(c["tri"], jnp.broadcast_to(dgc, (CHUNK, CHUNK)))[:, 0:1] + dtotal
    return dq, dk, dv, dg, dbeta


def _lane_blk(h):
    return pl.ds(pl.multiple_of(h * HEAD, HEAD), HEAD)


def gdn_chunk_pre(QKVg, P1, par, dm, name):
    nch = dm.T // CHUNK
    HD = C_HEADS * HEAD
    abcol = (4 * HD) // LANE

    def body(x_ref, ab_ref, par_ref, u_ref, w_ref, kt_ref, qd_ref, qk_ref, gt_ref):
        d = pl.program_id(1)
        rev = d == 1
        gb = _gate_fn(ab_ref[...], par_ref[...])

        def head(h, carry):
            cols = _lane_blk(h)
            q = x_ref[:, cols]
            k = x_ref[:, _lane_blk(C_HEADS + h)]
            v = x_ref[:, _lane_blk(2 * C_HEADS + h)]
            u, w, kt, qd, qk, gt = _chunk_fwd(q, k, v, _col(gb, d * 16 + h), _col(gb, d * 16 + 8 + h), rev)
            u_ref[0, :, cols] = u
            w_ref[0, :, cols] = w.astype(BF16)
            kt_ref[0, :, cols] = kt.astype(BF16)
            qd_ref[0, :, cols] = qd.astype(BF16)
            qk_ref[0, :, cols] = jnp.concatenate([qk, jnp.zeros_like(qk)], axis=1).astype(BF16)
            gt_ref[0, 0, pl.ds(h, 1), :] = jnp.broadcast_to(gt, (1, HEAD))
            return carry

        lax.fori_loop(0, C_HEADS, head, 0)

    big = pl.BlockSpec((1, CHUNK, HD), lambda i, d: (d, i, 0))
    return pl.pallas_call(
        body, name=name, grid=(nch, 2),
        in_specs=[pl.BlockSpec((CHUNK, 3 * HD), lambda i, d: (i, 0)), pl.BlockSpec((CHUNK, LANE), lambda i, d: (i, abcol)),
                  pl.BlockSpec((2, LANE), lambda i, d: (0, 0))],
        out_specs=[big, big, big, big, big, pl.BlockSpec((1, 1, C_HEADS, HEAD), lambda i, d: (d, i, 0, 0))],
        out_shape=[_sds((2, dm.T, HD), F32), _sds((2, dm.T, HD), BF16), _sds((2, dm.T, HD), BF16),
                   _sds((2, dm.T, HD), BF16), _sds((2, dm.T, HD), BF16), _sds((2, nch, C_HEADS, HEAD), F32)],
        compiler_params=_cp(),
    )(QKVg, P1, par)


def gdn_chunk_pre_bwd(QKVg, P1, par, dU, dW, dKT, dQD, dQK, dGT, dm, name):
    nch = dm.T // CHUNK
    HD = C_HEADS * HEAD
    abcol = (4 * HD) // LANE

    def body(x_ref, ab_ref, par_ref, du_ref, dw_ref, dkt_ref, dqd_ref, dqk_ref, dgt_ref, dx_ref, dab_ref, dpar_ref, dgb_s):
        i, d = pl.program_id(0), pl.program_id(1)
        rev = d == 1
        ab, par = ab_ref[...], par_ref[...]
        gb, gate_vjp = jax.vjp(_gate_fn, ab, par)
        dgb_s[...] = jnp.zeros_like(dgb_s)
        lane = lax.broadcasted_iota(jnp.int32, (CHUNK, LANE), 1)

        def head(h, carry):
            cols = _lane_blk(h)
            q = x_ref[:, cols]
            k = x_ref[:, _lane_blk(C_HEADS + h)]
            v = x_ref[:, _lane_blk(2 * C_HEADS + h)]
            dgt = dgt_ref[0, 0, pl.ds(h, 1), pl.ds(0, 1)]
            dq, dk, dv, dg, dbeta = _chunk_bwd(
                q, k, v, _col(gb, d * 16 + h), _col(gb, d * 16 + 8 + h), rev,
                du_ref[0, :, cols], dw_ref[0, :, cols], dkt_ref[0, :, cols], dqd_ref[0, :, cols],
                dqk_ref[0, :, cols][:, :CHUNK], dgt)
            first = d == 0
            _acc(dx_ref, (slice(None), cols), dq, first)
            _acc(dx_ref, (slice(None), _lane_blk(C_HEADS + h)), dk, first)
            _acc(dx_ref, (slice(None), _lane_blk(2 * C_HEADS + h)), dv, first)
            dgb_s[...] += jnp.where(lane == d * 16 + h, dg, 0.0) + jnp.where(lane == d * 16 + 8 + h, dbeta, 0.0)
            return carry

        lax.fori_loop(0, C_HEADS, head, 0)
        dab, dpar = gate_vjp(dgb_s[...])
        _acc(dab_ref, (slice(None), slice(None)), dab, d == 0)
        _acc(dpar_ref, (slice(None), slice(None)), dpar, (i == 0) & (d == 0))

    big = pl.BlockSpec((1, CHUNK, HD), lambda i, d: (d, i, 0))
    return pl.pallas_call(
        body, name=name, grid=(nch, 2),
        in_specs=[pl.BlockSpec((CHUNK, 3 * HD), lambda i, d: (i, 0)), pl.BlockSpec((CHUNK, LANE), lambda i, d: (i, abcol)),
                  pl.BlockSpec((2, LANE), lambda i, d: (0, 0)), big, big, big, big, big,
                  pl.BlockSpec((1, 1, C_HEADS, HEAD), lambda i, d: (d, i, 0, 0))],
        out_specs=[pl.BlockSpec((CHUNK, 3 * HD), lambda i, d: (i, 0)), pl.BlockSpec((CHUNK, LANE), lambda i, d: (i, 0)),
                   pl.BlockSpec((2, LANE), lambda i, d: (0, 0))],
        out_shape=[_sds((dm.T, 3 * HD), F32), _sds((dm.T, LANE), F32), _sds((2, LANE), F32)],
        scratch_shapes=[pltpu.VMEM((CHUNK, LANE), F32)], compiler_params=_cp(),
    )(QKVg, P1, par, dU, dW, dKT, dQD, dQK, dGT)


def _scan_chunk(b, d, c, dm):
    nh, nx = dm.M // CHUNK, dm.N // CHUNK
    in_h = c < nh
    pos_h = jnp.where(d == 0, c, nh - 1 - c)
    pos_x = jnp.where(d == 0, c - nh, nx - 1 - (c - nh))
    return jnp.where(in_h, dm.Tx // CHUNK + b * nh + pos_h, b * nx + pos_x)


def gdn_scan_fwd(U, W, KT, QD, QK, GT, dm, name):
    nch = dm.T // CHUNK
    HD = C_HEADS * HEAD
    nsc = (dm.M + dm.N) // CHUNK

    def body(u_ref, w_ref, kt_ref, qd_ref, qk_ref, gt_ref, o_ref, ss_ref, s_s):
        @pl.when(pl.program_id(2) == 0)
        def _():
            s_s[...] = jnp.zeros_like(s_s)

        for h in range(C_HEADS):
            rows, cols = pl.ds(h * HEAD, HEAD), pl.ds(h * HEAD, HEAD)
            s = s_s[rows, :]
            ss_ref[0, 0, rows, :] = s
            sb = s.astype(BF16)
            vn = u_ref[0, :, cols] - jnp.dot(w_ref[0, :, cols], sb, preferred_element_type=F32)
            vnb = vn.astype(BF16)
            qk = qk_ref[0, :, pl.ds(h * HEAD, CHUNK)]
            o_ref[0, :, cols] = (jnp.dot(qd_ref[0, :, cols], sb, preferred_element_type=F32)
                                 + jnp.dot(qk, vnb, preferred_element_type=F32))
            s_s[rows, :] = s * gt_ref[0, 0, pl.ds(h, 1), :] + _dot_tn(kt_ref[0, :, cols], vnb)

    big = pl.BlockSpec((1, CHUNK, HD), lambda b, d, c: (d, _scan_chunk(b, d, c, dm), 0))
    return pl.pallas_call(
        body, name=name, grid=(dm.Bl, 2, nsc),
        in_specs=[big, big, big, big, big,
                  pl.BlockSpec((1, 1, C_HEADS, HEAD), lambda b, d, c: (d, _scan_chunk(b, d, c, dm), 0, 0))],
        out_specs=[big, pl.BlockSpec((1, 1, HD, HEAD), lambda b, d, c: (d, _scan_chunk(b, d, c, dm), 0, 0))],
        out_shape=[_sds((2, dm.T, HD), F32), _sds((2, nch, HD, HEAD), F32)],
        scratch_shapes=[pltpu.VMEM((HD, HEAD), F32)], compiler_params=_cp(),
    )(U, W, KT, QD, QK, GT)


def gdn_scan_bwd(dO, SS, U, W, KT, QD, QK, GT, dm, name):
    nch = dm.T // CHUNK
    HD = C_HEADS * HEAD
    nsc = (dm.M + dm.N) // CHUNK

    def body(do_ref, ss_ref, u_ref, w_ref, kt_ref, qd_ref, qk_ref, gt_ref,
             du_ref, dw_ref, dkt_ref, dqd_ref, dqk_ref, dgt_ref, ds_s):
        @pl.when(pl.program_id(2) == 0)
        def _():
            ds_s[...] = jnp.zeros_like(ds_s)

        for h in range(C_HEADS):
            rows, cols = pl.ds(h * HEAD, HEAD), pl.ds(h * HEAD, HEAD)
            s = ss_ref[0, 0, rows, :]
            sb = s.astype(BF16)
            w, kt, qd = w_ref[0, :, cols], kt_ref[0, :, cols], qd_ref[0, :, cols]
            qk = qk_ref[0, :, pl.ds(h * HEAD, CHUNK)]
            gt = gt_ref[0, 0, pl.ds(h, 1), :]
            vnb = (u_ref[0, :, cols] - jnp.dot(w, sb, preferred_element_type=F32)).astype(BF16)
            dob = do_ref[:, cols].astype(BF16)
            dsn = ds_s[rows, :]
            dsnb = dsn.astype(BF16)
            dqd_ref[0, :, cols] = _dot_nt(dob, sb)
            dqk = _dot_nt(dob, vnb)
            dqk_ref[0, :, cols] = jnp.concatenate([dqk, jnp.zeros_like(dqk)], axis=1)
            dvn = _dot_tn(qk, dob) + jnp.dot(kt, dsnb, preferred_element_type=F32)
            dvnb = dvn.astype(BF16)
            dkt_ref[0, :, cols] = _dot_nt(vnb, dsnb)
            du_ref[0, :, cols] = dvn
            dw_ref[0, :, cols] = -_dot_nt(dvnb, sb)
            dgt_ref[0, 0, pl.ds(h, 1), :] = jnp.broadcast_to(jnp.sum(dsn * s, keepdims=True), (1, HEAD))
            ds_s[rows, :] = dsn * gt + _dot_tn(qd, dob) - _dot_tn(w, dvnb)

    def mem(b, d, c):
        return _scan_chunk(b, d, nsc - 1 - c, dm)

    big = pl.BlockSpec((1, CHUNK, HD), lambda b, d, c: (d, mem(b, d, c), 0))
    gts = pl.BlockSpec((1, 1, C_HEADS, HEAD), lambda b, d, c: (d, mem(b, d, c), 0, 0))
    return pl.pallas_call(
        body, name=name, grid=(dm.Bl, 2, nsc),
        in_specs=[pl.BlockSpec((CHUNK, HD), lambda b, d, c: (mem(b, d, c), 0)),
                  pl.BlockSpec((1, 1, HD, HEAD), lambda b, d, c: (d, mem(b, d, c), 0, 0)), big, big, big, big, big, gts],
        out_specs=[big, big, big, big, big, gts],
        out_shape=[_sds((2, dm.T, HD), F32)] * 5 + [_sds((2, nch, C_HEADS, HEAD), F32)],
        scratch_shapes=[pltpu.VMEM((HD, HEAD), F32)], compiler_params=_cp(),
    )(dO, SS, U, W, KT, QD, QK, GT)


def _finish_fn(o, z, gain):
    y = o * lax.rsqrt(jnp.mean(o * o, axis=-1, keepdims=True) + EPS) * gain
    return y * _silu(z)


def gdn_finish(O, P1, og, dm, name):
    TM = dm.TM
    HD = C_HEADS * HEAD
    zc = (3 * HD) // HEAD

    def body(o0_ref, o1_ref, z_ref, g_ref, y_ref):
        y_ref[...] = _finish_fn(o0_ref[0] + o1_ref[0], z_ref[...], g_ref[...]).astype(BF16)

    return pl.pallas_call(
        body, name=name, grid=(dm.ntx, C_HEADS),
        in_specs=[pl.BlockSpec((1, TM, HEAD), lambda i, j: (0, i, j)), pl.BlockSpec((1, TM, HEAD), lambda i, j: (1, i, j)),
                  pl.BlockSpec((TM, HEAD), lambda i, j: (i, zc + j)), pl.BlockSpec((1, HEAD), lambda i, j: (0, 0))],
        out_specs=pl.BlockSpec((TM, HEAD), lambda i, j: (i, j)),
        out_shape=_sds((dm.Tx, HD), BF16), compiler_params=_cp(),
    )(O, O, P1, og)


def gdn_finish_bwd(O, P1, og, dY, dm, name):
    TM = dm.TM
    HD = C_HEADS * HEAD
    zc = (3 * HD) // HEAD

    def body(o0_ref, o1_ref, z_ref, g_ref, dy_ref, do_ref, dz_ref, dg_ref):
        i, j = pl.program_id(0), pl.program_id(1)
        _, vjp = jax.vjp(_finish_fn, o0_ref[0] + o1_ref[0], z_ref[...], g_ref[...])
        do, dz, dg = vjp(dy_ref[...])
        do_ref[...] = do
        dz_ref[...] = dz
        _acc(dg_ref, (slice(None), slice(None)), dg, (i == 0) & (j == 0))

    blk = pl.BlockSpec((TM, HEAD), lambda i, j: (i, j))
    return pl.pallas_call(
        body, name=name, grid=(dm.ntx, C_HEADS),
        in_specs=[pl.BlockSpec((1, TM, HEAD), lambda i, j: (0, i, j)), pl.BlockSpec((1, TM, HEAD), lambda i, j: (1, i, j)),
                  pl.BlockSpec((TM, HEAD), lambda i, j: (i, zc + j)), pl.BlockSpec((1, HEAD), lambda i, j: (0, 0)), blk],
        out_specs=[blk, blk, pl.BlockSpec((1, HEAD), lambda i, j: (0, 0))],
        out_shape=[_sds((dm.Tx, HD), F32), _sds((dm.Tx, HD), F32), _sds((1, HEAD), F32)],
        compiler_params=_cp(),
    )(O, O, P1, og, dY)


def adaln_fwd(c_ext, w_mod, b_loc, name):
    R, D = c_ext.shape
    nl = w_mod.shape[2]
    tn = _pick(nl, 384)

    def body(c_ref, w_ref, b_ref, o_ref):
        o_ref[0] = _dot(_silu(c_ref[...]), w_ref[0]) + b_ref[0]

    return pl.pallas_call(
        body, name=name, grid=(2, nl // tn),
        in_specs=[pl.BlockSpec((R, D), lambda l, j: (0, 0)), pl.BlockSpec((1, D, tn), lambda l, j: (l, 0, j)),
                  pl.BlockSpec((1, 1, tn), lambda l, j: (l, 0, j))],
        out_specs=pl.BlockSpec((1, R, tn), lambda l, j: (l, 0, j)),
        out_shape=_sds((2, R, nl), F32), compiler_params=_cp(),
    )(c_ext, w_mod, b_loc)


def adaln_bwd(c_ext, c_ctx, w_mod, dmx, dmh, nb, name):
    R, D = c_ext.shape
    nl = w_mod.shape[2]
    tn = _pick(nl, 384)
    nj = nl // tn

    def body(c_ref, cc_ref, w_ref, dmx_ref, dmh_ref, gw_ref, dc_ref):
        l, j = pl.program_id(0), pl.program_id(1)
        dh = dmh_ref[0, 0:1, :]
        for k in range(1, N_DEV):
            dh = dh + dmh_ref[0, k:k + 1, :]
        row = lax.broadcasted_iota(jnp.int32, (R, tn), 0)
        dmat = dmx_ref[0] + jnp.where(row == nb, dh, 0.0)
        gw_ref[0] = _dot_tn(_silu(c_ref[...]), dmat)
        part = _dot_nt(jnp.broadcast_to(dh, (8, tn)), w_ref[0])[0:1]
        _acc(dc_ref, (slice(None), slice(None)), part, (l == 0) & (j == 0))

        @pl.when((l == 1) & (j == nj - 1))
        def _():
            cc = cc_ref[...]
            sg = jax.nn.sigmoid(cc)
            dc_ref[...] = dc_ref[...] * (sg * (1.0 + cc * (1.0 - sg)))

    return pl.pallas_call(
        body, name=name, grid=(2, nj),
        in_specs=[pl.BlockSpec((R, D), lambda l, j: (0, 0)), pl.BlockSpec((1, D), lambda l, j: (0, 0)),
                  pl.BlockSpec((1, D, tn), lambda l, j: (l, 0, j)), pl.BlockSpec((1, R, tn), lambda l, j: (l, 0, j)),
                  pl.BlockSpec((1, N_DEV, tn), lambda l, j: (l, 0, j))],
        out_specs=[pl.BlockSpec((1, D, tn), lambda l, j: (l, 0, j)), pl.BlockSpec((1, D), lambda l, j: (0, 0))],
        out_shape=[_sds((2, D, nl), F32), _sds((1, D), F32)], compiler_params=_cp(),
    )(c_ext, c_ctx, w_mod, dmx, dmh)


def bmod_grad(dmx, dmh, name):
    _, R, n9 = dmx.shape

    def body(dmx_ref, dmh_ref, o_ref):
        o_ref[0] = jnp.sum(dmx_ref[0], axis=0, keepdims=True) + jnp.sum(dmh_ref[0], axis=0, keepdims=True)

    return pl.pallas_call(
        body, name=name, grid=(2,),
        in_specs=[pl.BlockSpec((1, R, n9), lambda l: (l, 0, 0)), pl.BlockSpec((1, N_DEV, n9), lambda l: (l, 0, 0))],
        out_specs=pl.BlockSpec((1, 1, n9), lambda l: (l, 0, 0)), out_shape=_sds((2, 1, n9), F32),
        compiler_params=_cp(),
    )(dmx, dmh)


def adamw(gs, w, m, v, name):
    S, R, C = gs.shape
    cap = max(8, (1 << 20) // (S * C))
    tr = R
    if R > cap:
        tr = max(t for t in range(8, cap + 1, 8) if R % t == 0)

    def body(g_ref, w_ref, m_ref, v_ref, go_ref, d_ref, mo_ref, vo_ref):
        g = g_ref[0]
        for k in range(1, S):
            g = g + g_ref[k]
        mn = ADAM_B1 * m_ref[...] + (1.0 - ADAM_B1) * g
        vn = ADAM_B2 * v_ref[...] + (1.0 - ADAM_B2) * jnp.square(g)
        m_hat = mn / (1.0 - ADAM_B1 ** ADAM_STEP)
        v_hat = vn / (1.0 - ADAM_B2 ** ADAM_STEP)
        go_ref[...] = g
        d_ref[...] = -ADAM_LR * (m_hat / (jnp.sqrt(v_hat) + ADAM_EPS) + ADAM_WD * w_ref[...])
        mo_ref[...] = mn
        vo_ref[...] = vn

    blk = pl.BlockSpec((tr, C), lambda i: (i, 0))
    return pl.pallas_call(
        body, name=name, grid=(R // tr,),
        in_specs=[pl.BlockSpec((S, tr, C), lambda i: (0, i, 0)), blk, blk, blk],
        out_specs=[blk] * 4, out_shape=[_sds((R, C), F32)] * 4, compiler_params=_cp(),
    )(gs, w, m, v)


def _gather_flat(parts, dtype, name):
    flat = jnp.concatenate([p.astype(dtype).reshape(-1) for p in parts])
    n = flat.shape[0]
    pad = (-n) % LANE
    if pad:
        flat = jnp.concatenate([flat, jnp.zeros((pad,), dtype)])
    got = all_gather(flat.reshape(-1, LANE), name).reshape(N_DEV, -1)
    out, off = [], 0
    for p in parts:
        out.append(got[:, off:off + p.size].reshape((N_DEV,) + p.shape))
        off += p.size
    return out


def _cols_full(g):
    return g.transpose(1, 0, 2).reshape(g.shape[1], -1)


def _cols_split(full):
    K = full.shape[0]
    return full.reshape(K, N_DEV, -1).transpose(1, 0, 2).reshape(N_DEV, -1)


def kernel(x, c, ctx, c_ctx, w_mod, b_mod, norm_g, ffn_wg, ffn_wu, ffn_wd, ab_w_in, ab_q_norm, ab_k_norm, pool_w, pool_scale, ab_w_out, gdn_w_in, gdn_conv_w, gdn_a_log, gdn_dt_bias, gdn_o_norm, gdn_w_out, loss_target, m_c_ctx, m_w_mod, m_b_mod, m_norm_g, m_ffn_wg, m_ffn_wu, m_ffn_wd, m_ab_w_in, m_ab_q_norm, m_ab_k_norm, m_pool_w, m_pool_scale, m_ab_w_out, m_gdn_w_in, m_gdn_conv_w, m_gdn_a_log, m_gdn_dt_bias, m_gdn_o_norm, m_gdn_w_out, v_c_ctx, v_w_mod, v_b_mod, v_norm_g, v_ffn_wg, v_ffn_wu, v_ffn_wd, v_ab_w_in, v_ab_q_norm, v_ab_k_norm, v_pool_w, v_pool_scale, v_ab_w_out, v_gdn_w_in, v_gdn_conv_w, v_gdn_a_log, v_gdn_dt_bias, v_gdn_o_norm, v_gdn_w_out):
    Bl, N, D = x.shape
    M = ctx.shape[1]
    F = ffn_wd.shape[2] * N_DEV
    dm = Dims(Bl, N, M, D, F)
    TM, Tx, Th, T, G = dm.TM, dm.Tx, dm.Th, dm.T, dm.G
    HD = C_HEADS * HEAD
    me = 4 * lax.axis_index("x") + 2 * lax.axis_index("y") + lax.axis_index("c")
    nb = N_DEV * Bl
    R = -(-(nb + 1) // 8) * 8
    nl = w_mod.shape[2]
    n_gdn = gdn_w_in.shape[2] * N_DEV
    n_gdn_pad = -(-n_gdn // LANE) * LANE

    big = [ffn_wg, ffn_wu, ffn_wd, ab_w_in, ab_w_out, gdn_w_in, gdn_w_out]
    g_wg, g_wu, g_wd, g_abin, g_about, g_gin, g_gout = _gather_flat(big, BF16, "gather_weights")
    g_c, g_ng, g_cw = _gather_flat([c, norm_g, gdn_conv_w], F32, "gather_small")
    WG = [[_cols_full(g_wg[:, l, s]) for s in range(2)] for l in range(2)]
    WU = [[_cols_full(g_wu[:, l, s]) for s in range(2)] for l in range(2)]
    WD = [[g_wd[:, l, s].reshape(F, D) for s in range(2)] for l in range(2)]
    W_ABIN = _cols_full(g_abin[:, 0])
    W_ABOUT = g_about[:, 0].reshape(-1, D)
    W_GIN = jnp.pad(_cols_full(g_gin[:, 0]), ((0, 0), (0, n_gdn_pad - n_gdn)))
    W_GOUT = g_gout[:, 0].reshape(-1, D)
    gains = g_ng.transpose(1, 2, 0, 3).reshape(2, 3, 1, D)
    conv_w = g_cw[:, 0].transpose(1, 0, 2).reshape(3, -1)

    c_all = g_c.reshape(nb, D)
    c_ext = jnp.concatenate([c_all, c_ctx[None], jnp.zeros((R - nb - 1, D), F32)], 0)
    b_loc = lax.dynamic_slice_in_dim(b_mod, me * nl, nl, axis=1).reshape(2, 1, nl)
    mod_loc = adaln_fwd(c_ext, w_mod, b_loc, "adaln_fwd")
    (g_mod,) = _gather_flat([mod_loc], F32, "gather_mod")
    mod_full = g_mod.transpose(1, 2, 0, 3).reshape(2, R, 9 * D)
    MOD = []
    for l in range(2):
        mine = lax.dynamic_slice_in_dim(mod_full[l], me * Bl, Bl, axis=0)
        MOD.append(jnp.concatenate([mine, mod_full[l, nb:nb + 1]], 0).reshape(G, 9, D))

    tabs = _rope_tables(dm)
    qkg = jnp.stack([ab_q_norm, ab_k_norm])
    pw = pool_w[0].astype(BF16)
    pwT = pool_w[0].transpose(0, 2, 1).astype(BF16)
    par = jnp.stack([jnp.pad(jnp.pad(p[0], ((0, 0), (0, 8))).reshape(-1), (0, LANE - 32))
                     for p in (gdn_a_log, gdn_dt_bias)])

    X0 = jnp.concatenate([x.reshape(Tx, D), ctx.reshape(Th, D)], 0)
    nt, ntx = dm.nt, dm.ntx
    X1, Y1 = ffn_fwd(X0, MOD[0], gains[0, 0], WG[0][0], WU[0][0], WD[0][0], 0, dm, nt, "ffn_fwd_00")
    P0, XN0 = modmm(X1, MOD[0], gains[0, 1], W_ABIN, 3, dm, "ab_in_proj")
    QKV = qkv_prep(P0, qkg, tabs, dm, "qkv_prep")
    ATT = jnp.concatenate([attn_fwd(QKV, dm, True, "attn_fwd_x"), attn_fwd(QKV, dm, False, "attn_fwd_h")], 0)
    POOL = jnp.concatenate([pool_fwd(P0, pw, pool_scale, dm, True, "pool_fwd_x"),
                            pool_fwd(P0, pw, pool_scale, dm, False, "pool_fwd_h")], 0)
    na = A_HEADS * HEAD
    X2, YM0 = proj_res([ATT, POOL], [W_ABOUT[:na], W_ABOUT[na:]], X1, MOD[0], dm, nt, "ab_out_proj")
    X3, Y3 = ffn_fwd(X2, MOD[0], gains[0, 2], WG[0][1], WU[0][1], WD[0][1], 6, dm, nt, "ffn_fwd_01")
    X4, Y4 = ffn_fwd(X3, MOD[1], gains[1, 0], WG[1][0], WU[1][0], WD[1][0], 0, dm, nt, "ffn_fwd_10")
    P1, XN1 = modmm(X4, MOD[1], gains[1, 1], W_GIN, 3, dm, "gdn_in_proj")
    QKVg = jnp.concatenate([gdn_prep(P1, conv_w, dm, True, "gdn_prep_x"), gdn_prep(P1, conv_w, dm, False, "gdn_prep_h")], 0)
    U, W, KT, QD, QK, GT = gdn_chunk_pre(QKVg, P1, par, dm, "gdn_chunk_pre")
    O, SS = gdn_scan_fwd(U, W, KT, QD, QK, GT, dm, "gdn_scan_fwd")
    FIN = gdn_finish(O, P1, gdn_o_norm, dm, "gdn_finish")
    X5, YM1 = proj_res([FIN], [W_GOUT], X4[:Tx], MOD[1], dm, ntx, "gdn_out_proj")
    X6, Y6 = ffn_fwd(X5, MOD[1], gains[1, 2], WG[1][1], WU[1][1], WD[1][1], 6, dm, ntx, "ffn_fwd_11")
    lvec, dX6 = loss_head(X6, loss_target.reshape(Tx, D), dm, "loss_head")
    loss = lax.psum(lvec[0, 0], AXES)

    zrow = lambda a: jnp.concatenate([a, jnp.zeros((G - a.shape[0],) + a.shape[1:], F32)], 0) if a.shape[0] < G else a
    tr = lambda w: w.T

    def ffn_back(Xin, dXo, Y, l, s, s0, ntiles, tag):
        rows = ntiles * TM
        dXi, XNb, DOb, Hb, DGb, DUb, dmod, dgain = ffn_bwd(
            Xin, dXo, Y, MOD[l], gains[l, 2 * s], WG[l][s], WU[l][s], tr(WD[l][s]), tr(WG[l][s]), tr(WU[l][s]),
            s0, dm, ntiles, "ffn_bwd_" + tag)
        dwg = atb(XNb, DGb, rows, TM, "ffn_dwg_" + tag)
        dwu = atb(XNb, DUb, rows, TM, "ffn_dwu_" + tag)
        dwd = atb(Hb, DOb, rows, TM, "ffn_dwd_" + tag)
        return dXi, zrow(dmod), dgain, dwg, dwu, dwd

    dX5, dmod_12, dgain_12, dwg11, dwu11, dwd11 = ffn_back(X5, dX6, Y6, 1, 1, 6, ntx, "11")
    DY1, dFIN, dgate_1 = proj_res_bwd(dX5, YM1, MOD[1], [tr(W_GOUT)], dm, ntx, "gdn_out_proj_bwd")
    d_gout = atb(FIN, DY1, Tx, TM, "gdn_dwout")
    dOsum, dZ, d_onorm = gdn_finish_bwd(O, P1, gdn_o_norm, dFIN, dm, "gdn_finish_bwd")
    dO_all = jnp.concatenate([dOsum, jnp.zeros((Th, HD), F32)], 0)
    dU, dW, dKT, dQD, dQK, dGT = gdn_scan_bwd(dO_all, SS, U, W, KT, QD, QK, GT, dm, "gdn_scan_bwd")
    dQKVg, dAB, dPAR = gdn_chunk_pre_bwd(QKVg, P1, par, dU, dW, dKT, dQD, dQK, dGT, dm, "gdn_chunk_pre_bwd")
    dPx, dcw_x = gdn_prep_bwd(P1, dQKVg[:Tx], conv_w, dm, True, "gdn_prep_bwd_x")
    dPh, dcw_h = gdn_prep_bwd(P1, dQKVg[Tx:], conv_w, dm, False, "gdn_prep_bwd_h")
    d_conv = dcw_x + dcw_h
    dP1 = jnp.concatenate([jnp.concatenate([dPx, dPh], 0), jnp.concatenate([dZ, jnp.zeros((Th, HD), F32)], 0), dAB],
                          axis=1).astype(BF16)
    d_gin = atb(XN1, dP1, T, TM, "gdn_dwin")[:, :n_gdn]
    dX5_full = jnp.concatenate([dX5, jnp.zeros((Th, D), F32)], 0)
    dX4, dmod_11, dgain_11 = mixin_bwd(dP1, tr(W_GIN), X4, dX5_full, MOD[1], gains[1, 1], 3, dm, "gdn_in_proj_bwd")
    dX3, dmod_10, dgain_10, dwg10, dwu10, dwd10 = ffn_back(X3, dX4, Y4, 1, 0, 0, nt, "10")
    dMOD1 = jnp.concatenate([dmod_10, dmod_11, zrow(dgate_1), dmod_12], 1).reshape(G, 9 * D)

    dX2, dmod_02, dgain_02, dwg01, dwu01, dwd01 = ffn_back(X2, dX3, Y3, 0, 1, 6, nt, "01")
    DY0, dATT, dPOOL, dgate_0 = proj_res_bwd(dX2, YM0, MOD[0], [tr(W_ABOUT[:na]), tr(W_ABOUT[na:])], dm, nt, "ab_out_proj_bwd")
    d_about = jnp.concatenate([atb(ATT, DY0, T, TM, "ab_dwout_a"), atb(POOL, DY0, T, TM, "ab_dwout_p")], 0)
    dUx, dpw_x, dps_x = pool_bwd(P0, dPOOL[:Tx], pw, pwT, pool_scale, dm, True, "pool_bwd_x")
    dUh, dpw_h, dps_h = pool_bwd(P0, dPOOL[Tx:], pw, pwT, pool_scale, dm, False, "pool_bwd_h")
    dQh, dKh0, dVh0 = attn_bwd(QKV, dATT[Tx:], dm, False, None, "attn_bwd_h")
    dQx, dKh, dVh, dKx, dVx = attn_bwd(QKV, dATT[:Tx], dm, True, (dKh0, dVh0), "attn_bwd_x")
    dQKV = jnp.concatenate([jnp.concatenate([dQx, dQh], 0), jnp.concatenate([dKx, dKh], 0), jnp.concatenate([dVx, dVh], 0)], 1)
    dPqkv, d_qkg = qkv_prep_bwd(P0, dQKV, qkg, tabs, dm, "qkv_prep_bwd")
    dP0 = jnp.concatenate([dPqkv, jnp.concatenate([dUx, dUh], 0)], 1).astype(BF16)
    d_abin = atb(XN0, dP0, T, TM, "ab_dwin")
    dX1, dmod_01, dgain_01 = mixin_bwd(dP0, tr(W_ABIN), X1, dX2, MOD[0], gains[0, 1], 3, dm, "ab_in_proj_bwd")
    dX0, dmod_00, dgain_00, dwg00, dwu00, dwd00 = ffn_back(X0, dX1, Y1, 0, 0, 0, nt, "00")
    dMOD0 = jnp.concatenate([dmod_00, dmod_01, dgate_0, dmod_02], 1).reshape(G, 9 * D)
    grad_x = dX0[:Tx].reshape(Bl, N, D)

    d_ng = jnp.concatenate([dgain_00, dgain_01, dgain_02, dgain_10, dgain_11, dgain_12], 0)
    dwg = [[dwg00, dwg01], [dwg10, dwg11]]
    dwu = [[dwu00, dwu01], [dwu10, dwu11]]
    dwd = [[dwd00, dwd01], [dwd10, dwd11]]
    pieces = [jnp.concatenate([_cols_split(dwg[l][s]) for l in range(2) for s in range(2)], 1),
              jnp.concatenate([_cols_split(dwu[l][s]) for l in range(2) for s in range(2)], 1),
              jnp.concatenate([dwd[l][s].reshape(N_DEV, -1) for l in range(2) for s in range(2)], 1),
              _cols_split(d_abin), d_about.reshape(N_DEV, -1), _cols_split(d_gin), d_gout.reshape(N_DEV, -1),
              _cols_split(d_conv), _cols_split(d_ng)]
    tail = (-sum(p.shape[1] for p in pieces)) % LANE
    if tail:
        pieces.append(jnp.zeros((N_DEV, tail), F32))
    sizes = [p.shape[1] for p in pieces]
    packed = jnp.concatenate(pieces, 1)
    got = scatter_blocks(packed.reshape(N_DEV, -1, LANE), "scatter_grads").reshape(N_DEV, -1)
    offs = [sum(sizes[:i]) for i in range(len(sizes))]
    rs = [got[:, o:o + s] for o, s in zip(offs, sizes)]
    nf = ffn_wg.shape[3]
    gs_wg = rs[0].reshape(N_DEV, 4 * D, nf)
    gs_wu = rs[1].reshape(N_DEV, 4 * D, nf)
    gs_wd = rs[2].reshape(N_DEV, 4 * nf, D)
    gs_abin = rs[3].reshape(N_DEV, D, -1)
    gs_about = rs[4].reshape(N_DEV, -1, D)
    gs_gin = rs[5].reshape(N_DEV, D, -1)
    gs_gout = rs[6].reshape(N_DEV, -1, D)
    gs_conv = rs[7].reshape(N_DEV, 3, -1)
    gs_ng = rs[8].reshape(N_DEV, 6, -1)

    d_alog = dPAR[0, :32].reshape(2, 16)[:, :8].reshape(1, 16)
    d_dtb = dPAR[1, :32].reshape(2, 16)[:, :8].reshape(1, 16)
    small = [d_qkg[0], d_qkg[1], (dpw_x + dpw_h).reshape(-1, HEAD), dps_x + dps_h, d_alog, d_dtb, d_onorm,
             jnp.stack([dMOD0, dMOD1])]
    gs_qn, gs_kn, gs_pw, gs_ps, gs_alog, gs_dtb, gs_on, g_dm = _gather_flat(small, F32, "gather_small_grads")
    dmx = g_dm[:, :, :Bl].transpose(1, 0, 2, 3).reshape(2, nb, 9 * D)
    dmx = jnp.concatenate([dmx, jnp.zeros((2, R - nb, 9 * D), F32)], 1)
    dmh = g_dm[:, :, Bl].transpose(1, 0, 2)
    cols_of_me = lambda a: lax.dynamic_slice_in_dim(a, me * nl, nl, axis=2)
    d_wmod, dcc = adaln_bwd(c_ext, c_ctx[None], w_mod, cols_of_me(dmx), cols_of_me(dmh), nb, "adaln_bwd")
    d_bmod = bmod_grad(dmx, dmh, "bmod_grad")
    (gs_cc,) = _gather_flat([dcc], F32, "gather_cctx_grad")

    def upd(gs, w, m, v, shape2, name):
        outs = adamw(gs.reshape((gs.shape[0],) + shape2), w.reshape(shape2), m.reshape(shape2), v.reshape(shape2), "adamw_" + name)
        return [o.reshape(w.shape) for o in outs]

    res = [
        upd(gs_cc, c_ctx, m_c_ctx, v_c_ctx, (1, D), "c_ctx"),
        upd(d_wmod[None], w_mod, m_w_mod, v_w_mod, (2 * D, nl), "w_mod"),
        upd(d_bmod[None], b_mod, m_b_mod, v_b_mod, (2, 9 * D), "b_mod"),
        upd(gs_ng, norm_g, m_norm_g, v_norm_g, (6, D // N_DEV), "norm_g"),
        upd(gs_wg, ffn_wg, m_ffn_wg, v_ffn_wg, (4 * D, nf), "ffn_wg"),
        upd(gs_wu, ffn_wu, m_ffn_wu, v_ffn_wu, (4 * D, nf), "ffn_wu"),
        upd(gs_wd, ffn_wd, m_ffn_wd, v_ffn_wd, (4 * nf, D), "ffn_wd"),
        upd(gs_abin, ab_w_in, m_ab_w_in, v_ab_w_in, (D, ab_w_in.shape[2]), "ab_w_in"),
        upd(gs_qn, ab_q_norm, m_ab_q_norm, v_ab_q_norm, (1, HEAD), "ab_q_norm"),
        upd(gs_kn, ab_k_norm, m_ab_k_norm, v_ab_k_norm, (1, HEAD), "ab_k_norm"),
        upd(gs_pw, pool_w, m_pool_w, v_pool_w, (len(POOL_WINDOWS) * HEAD, HEAD), "pool_w"),
        upd(gs_ps, pool_scale, m_pool_scale, v_pool_scale, (1, len(POOL_WINDOWS) * HEAD), "pool_scale"),
        upd(gs_about, ab_w_out, m_ab_w_out, v_ab_w_out, (ab_w_out.shape[1], D), "ab_w_out"),
        upd(gs_gin, gdn_w_in, m_gdn_w_in, v_gdn_w_in, (D, gdn_w_in.shape[2]), "gdn_w_in"),
        upd(gs_conv, gdn_conv_w, m_gdn_conv_w, v_gdn_conv_w, (3, gdn_conv_w.shape[2]), "gdn_conv_w"),
        upd(gs_alog, gdn_a_log, m_gdn_a_log, v_gdn_a_log, (1, 16), "gdn_a_log"),
        upd(gs_dtb, gdn_dt_bias, m_gdn_dt_bias, v_gdn_dt_bias, (1, 16), "gdn_dt_bias"),
        upd(gs_on, gdn_o_norm, m_gdn_o_norm, v_gdn_o_norm, (1, HEAD), "gdn_o_norm"),
        upd(gs_gout, gdn_w_out, m_gdn_w_out, v_gdn_w_out, (gdn_w_out.shape[1], D), "gdn_w_out"),
    ]
    return (loss, grad_x, *[r[0] for r in res], *[r[1] for r in res], *[r[2] for r in res], *[r[3] for r in res])
```

```python
import functools
import math
from typing import NamedTuple

import jax
import jax.numpy as jnp
from jax import lax
from jax.experimental import pallas as pl
from jax.experimental.pallas import tpu as pltpu

F32, BF16 = jnp.float32, jnp.bfloat16
EPS = 1e-6
HEAD = 128
CHUNK = 64
GRID_W = 64
ROPE_THETA = 10000.0
POOL_WINDOWS = (2, 4, 8, 16)
A_HEADS, A_KV = 4, 2
C_HEADS = 8
N_DEV = 8
AXES = ("x", "y", "c")
ADAM_LR, ADAM_B1, ADAM_B2, ADAM_EPS, ADAM_WD, ADAM_STEP = 0.001, 0.9, 0.999, 1e-08, 0.01, 10
LANE = 128
VMEM_LIMIT = 56 * 1024 * 1024
HI = lax.Precision.HIGHEST
NEG = -1e30


def _cp():
    return pltpu.CompilerParams(vmem_limit_bytes=VMEM_LIMIT)


def _sds(shape, dtype):
    return jax.ShapeDtypeStruct(tuple(shape), dtype)


def _dot(a, b):
    return jnp.dot(a.astype(BF16), b.astype(BF16), preferred_element_type=F32)


def _dot_nt(a, b):
    return lax.dot_general(a.astype(BF16), b.astype(BF16), (((1,), (1,)), ((), ())), preferred_element_type=F32)


def _dot_tn(a, b):
    return lax.dot_general(a.astype(BF16), b.astype(BF16), (((0,), (0,)), ((), ())), preferred_element_type=F32)


def _dot3(a, b, dims):
    ah, bh = a.astype(BF16), b.astype(BF16)
    al, bl = (a - ah.astype(F32)).astype(BF16), (b - bh.astype(F32)).astype(BF16)
    f = lambda x, y: lax.dot_general(x, y, (dims, ((), ())), preferred_element_type=F32)
    return f(ah, bh) + (f(ah, bl) + f(al, bh))


def _hdot(a, b):
    return _dot3(a, b, ((1,), (0,)))


def _hdot_nt(a, b):
    return _dot3(a, b, ((1,), (1,)))


def _hdot_tn(a, b):
    return _dot3(a, b, ((0,), (0,)))


def _pick(n, cap):
    if n <= cap:
        return n
    best = None
    for t in range(LANE, cap + 1, LANE):
        if n % t == 0:
            best = t
    assert best is not None, (n, cap)
    return best


class Dims(NamedTuple):
    Bl: int
    N: int
    M: int
    D: int
    F: int

    @property
    def TM(self):
        return min(256, self.M)

    @property
    def Tx(self):
        return self.Bl * self.N

    @property
    def Th(self):
        return self.Bl * self.M

    @property
    def T(self):
        return self.Tx + self.Th

    @property
    def ntx(self):
        return self.Tx // self.TM

    @property
    def nt(self):
        return self.T // self.TM

    @property
    def tps(self):
        return self.N // self.TM

    @property
    def G(self):
        return self.Bl + 1


def _grp(i, dm, tm=None):
    tm = dm.TM if tm is None else tm
    return jnp.where(i < dm.Tx // tm, i // (dm.N // tm), dm.Bl)


def _first_of_group(i, dm, tm=None):
    tm = dm.TM if tm is None else tm
    return jnp.where(i < dm.Tx // tm, i % (dm.N // tm) == 0, i == dm.Tx // tm)


def _ffn_tile(dm):
    return max(t for t in (512, 256, 128) if dm.N % t == 0 and dm.Th % t == 0)


def _acc(ref, idx, val, first):
    @pl.when(first)
    def _():
        ref[idx] = val

    @pl.when(jnp.logical_not(first))
    def _():
        ref[idx] += val


def _modulate(x, gain, shift, scale):
    y = x * lax.rsqrt(jnp.mean(x * x, axis=-1, keepdims=True) + EPS)
    return (y * gain) * (1.0 + scale) + shift


def _silu(x):
    return x * jax.nn.sigmoid(x)


@functools.partial(jax.custom_vjp, nondiff_argnums=(1,))
def _shift_rows(a, k):
    n = a.shape[0]
    if k == 0:
        return a
    r = lax.broadcasted_iota(jnp.int32, a.shape, 0)
    rolled = pltpu.roll(a, (-k) % n, 0)
    ok = (r + k >= 0) & (r + k < n)
    return jnp.where(ok, rolled, 0.0)


def _shift_rows_fwd(a, k):
    return _shift_rows(a, k), None


def _shift_rows_bwd(k, _, d):
    return (_shift_rows(d, -k),)


_shift_rows.defvjp(_shift_rows_fwd, _shift_rows_bwd)


@functools.partial(jax.custom_vjp, nondiff_argnums=(1,))
def _roll_lanes(a, s):
    return pltpu.roll(a, s % LANE, 1)


def _roll_lanes_fwd(a, s):
    return _roll_lanes(a, s), None


def _roll_lanes_bwd(s, _, d):
    return (_roll_lanes(d, -s),)


_roll_lanes.defvjp(_roll_lanes_fwd, _roll_lanes_bwd)


def _rope(t, cs, sneg, spos):
    return t * cs + _roll_lanes(t, 96) * sneg + _roll_lanes(t, 32) * spos


def _rope_tables(dm):
    rows = dm.N // GRID_W
    row = jnp.repeat(jnp.arange(rows), GRID_W).astype(F32)
    col = jnp.tile(jnp.arange(GRID_W), rows).astype(F32)
    half = HEAD // 2
    inv_freq = jnp.power(ROPE_THETA, -jnp.arange(0, half, 2, dtype=F32) / half)
    ar, ac = row[:, None] * inv_freq, col[:, None] * inv_freq
    cs = jnp.concatenate([jnp.cos(ar), jnp.cos(ar), jnp.cos(ac), jnp.cos(ac)], axis=1)
    z = jnp.zeros_like(ar)
    sneg = jnp.concatenate([-jnp.sin(ar), z, -jnp.sin(ac), z], axis=1)
    spos = jnp.concatenate([z, jnp.sin(ar), z, jnp.sin(ac)], axis=1)
    pad1 = jnp.ones((dm.TM, HEAD), F32)
    pad0 = jnp.zeros((dm.TM, HEAD), F32)
    return (jnp.concatenate([cs, pad1], 0), jnp.concatenate([sneg, pad0], 0), jnp.concatenate([spos, pad0], 0))


def all_gather(xs, name):
    n = len(xs)

    def body(*refs):
        x_refs, out_refs = refs[:n], refs[n:2 * n]
        send_sems, recv_sems, local_sems = refs[2 * n:]
        x, y, c = lax.axis_index("x"), lax.axis_index("y"), lax.axis_index("c")
        me, sibling = (x, y, c), (x, y, 1 - c)
        chips = [(1 - x, y), (x, 1 - y), (1 - x, 1 - y)]

        def slot(a, px, py, pc):
            return out_refs[a].at[4 * px + 2 * py + pc]

        def copy(a, k, block, to, src=None):
            return pltpu.make_async_remote_copy(
                src_ref=slot(a, *block) if src is None else src, dst_ref=slot(a, *block),
                send_sem=send_sems.at[7 * a + k], recv_sem=recv_sems.at[7 * a + k],
                device_id=to, device_id_type=pl.DeviceIdType.MESH)

        mine = [pltpu.make_async_copy(x_refs[a], slot(a, *me), local_sems.at[a]) for a in range(n)]
        for cp in mine:
            cp.start()
        first = []
        for a in range(n):
            first.append(copy(a, 0, me, sibling, src=x_refs[a]))
            first += [copy(a, 1 + j, me, (*chip, c), src=x_refs[a]) for j, chip in enumerate(chips)]
        for cp in first:
            cp.start()
        passed = []
        for j, chip in enumerate(chips):
            for a in range(n):
                copy(a, 1 + j, (*chip, c), me).wait_recv()
                cp = copy(a, 4 + j, (*chip, c), sibling)
                cp.start()
                passed.append(cp)
        for a in range(n):
            copy(a, 0, sibling, me).wait_recv()
            for j, chip in enumerate(chips):
                copy(a, 4 + j, (*chip, 1 - c), me).wait_recv()
        for cp in first + passed:
            cp.wait_send()
        for cp in mine:
            cp.wait()

    anyspec = pl.BlockSpec(memory_space=pl.ANY)
    return pl.pallas_call(
        body, name=name, out_shape=[_sds((N_DEV,) + a.shape, a.dtype) for a in xs],
        in_specs=[anyspec] * n, out_specs=[anyspec] * n,
        scratch_shapes=[pltpu.SemaphoreType.DMA((7 * n,)), pltpu.SemaphoreType.DMA((7 * n,)),
                        pltpu.SemaphoreType.DMA((n,))],
    )(*xs)


def scatter_blocks(xs, name):
    n = len(xs)
    flips = [(0, 0, 1), (0, 1, 0), (0, 1, 1), (1, 0, 0), (1, 0, 1), (1, 1, 0), (1, 1, 1)]

    def body(*refs):
        x_refs, out_refs = refs[:n], refs[n:2 * n]
        send_sems, recv_sems, local_sems = refs[2 * n:]
        x, y, c = lax.axis_index("x"), lax.axis_index("y"), lax.axis_index("c")
        me = 4 * x + 2 * y + c

        def peer(f):
            return tuple(1 - v if d else v for v, d in zip((x, y, c), f))

        def lin(p):
            return 4 * p[0] + 2 * p[1] + p[2]

        mine = [pltpu.make_async_copy(x_refs[a].at[me], out_refs[a].at[me], local_sems.at[a]) for a in range(n)]
        for cp in mine:
            cp.start()
        copies = []
        for k, f in enumerate(flips):
            p = peer(f)
            for a in range(n):
                copies.append(pltpu.make_async_remote_copy(
                    src_ref=x_refs[a].at[lin(p)], dst_ref=out_refs[a].at[me],
                    send_sem=send_sems.at[7 * a + k], recv_sem=recv_sems.at[7 * a + k],
                    device_id=p, device_id_type=pl.DeviceIdType.MESH))
        for cp in copies:
            cp.start()
        for cp in copies:
            cp.wait_send()
            cp.wait_recv()
        for cp in mine:
            cp.wait()

    anyspec = pl.BlockSpec(memory_space=pl.ANY)
    return pl.pallas_call(
        body, name=name, out_shape=[_sds(a.shape, a.dtype) for a in xs],
        in_specs=[anyspec] * n, out_specs=[anyspec] * n,
        scratch_shapes=[pltpu.SemaphoreType.DMA((7 * n,)), pltpu.SemaphoreType.DMA((7 * n,)),
                        pltpu.SemaphoreType.DMA((n,))],
    )(*xs)


def _mod_spec(dm, nidx, tm=None):
    if nidx == 1:
        return pl.BlockSpec((1, 9, dm.D), lambda i: (_grp(i, dm, tm), 0, 0))
    return pl.BlockSpec((1, 9, dm.D), lambda i, k: (_grp(i, dm, tm), 0, 0))


def _wspec(shape5, l, s):
    return pl.BlockSpec((1, 1, 1) + tuple(shape5[3:]), lambda i, k: (k, l, s, 0, 0))


def ffn_fwd(X, MOD, gain, gwg, gwu, gwd, l, s, s0, dm, all_rows, name):
    D = dm.D
    tm = _ffn_tile(dm)
    rows = dm.T if all_rows else dm.Tx
    nk = N_DEV

    def body(x_ref, m_ref, g_ref, wg_ref, wu_ref, wd_ref, xo_ref, y_ref, xn_s, acc_s):
        k = pl.program_id(1)

        @pl.when(k == 0)
        def _():
            m = m_ref[0]
            xn = _modulate(x_ref[...], g_ref[...], m[s0:s0 + 1], m[s0 + 1:s0 + 2])
            xn_s[...] = xn.astype(BF16)
            acc_s[...] = jnp.zeros_like(acc_s)

        xn = xn_s[...]
        g = jnp.dot(xn, wg_ref[0, 0, 0], preferred_element_type=F32)
        u = jnp.dot(xn, wu_ref[0, 0, 0], preferred_element_type=F32)
        h = _silu(g) * u
        acc_s[...] += jnp.dot(h.astype(BF16), wd_ref[0, 0, 0], preferred_element_type=F32)

        @pl.when(k == nk - 1)
        def _():
            m = m_ref[0]
            y = acc_s[...]
            y_ref[...] = y
            xo_ref[...] = x_ref[...] + (0.5 * m[s0 + 2:s0 + 3]) * y

    row = pl.BlockSpec((tm, D), lambda i, k: (i, 0))
    return pl.pallas_call(
        body, name=name, grid=(rows // tm, nk),
        in_specs=[row, _mod_spec(dm, 2, tm), pl.BlockSpec((1, D), lambda i, k: (0, 0)),
                  _wspec(gwg.shape, l, s), _wspec(gwu.shape, l, s), _wspec(gwd.shape, l, s)],
        out_specs=[row, row],
        out_shape=[_sds((rows, D), F32), _sds((rows, D), F32)],
        scratch_shapes=[pltpu.VMEM((tm, D), BF16), pltpu.VMEM((tm, D), F32)],
        compiler_params=_cp(),
    )(X, MOD, gain, gwg, gwu, gwd)


def ffn_bwd(X, dXo, Y, MOD, gain, gwg, gwu, gwd, l, s, s0, dm, all_rows, name):
    D = dm.D
    tm = _ffn_tile(dm)
    rows = dm.T if all_rows else dm.Tx
    nk = N_DEV
    nf = gwg.shape[4]
    ngr = dm.G if all_rows else dm.Bl

    def body(x_ref, dxo_ref, y_ref, m_ref, g_ref, wg_ref, wu_ref, wd_ref,
             dxi_ref, xn_ref, do_ref, h_ref, dg_ref, du_ref, dm_ref, dgain_ref, xn_s, do_s, dxn_s):
        i, k = pl.program_id(0), pl.program_id(1)

        @pl.when(k == 0)
        def _():
            m = m_ref[0]
            xn = _modulate(x_ref[...], g_ref[...], m[s0:s0 + 1], m[s0 + 1:s0 + 2])
            xn_s[...] = xn.astype(BF16)
            do_s[...] = ((0.5 * m[s0 + 2:s0 + 3]) * dxo_ref[...]).astype(BF16)
            dxn_s[...] = jnp.zeros_like(dxn_s)

        xn = xn_s[...]
        wg, wu = wg_ref[0, 0, 0], wu_ref[0, 0, 0]
        g = jnp.dot(xn, wg, preferred_element_type=F32)
        u = jnp.dot(xn, wu, preferred_element_type=F32)
        sg = jax.nn.sigmoid(g)
        si = g * sg
        dh = _dot_nt(do_s[...], wd_ref[0, 0, 0])
        dg = (dh * u * (sg * (1.0 + g * (1.0 - sg)))).astype(BF16)
        du = (dh * si).astype(BF16)
        dxn_s[...] += _dot_nt(dg, wg) + _dot_nt(du, wu)
        h_ref[0] = (si * u).astype(BF16)
        dg_ref[0] = dg
        du_ref[0] = du

        @pl.when(k == nk - 1)
        def _():
            m = m_ref[0]
            _, vjp = jax.vjp(_modulate, x_ref[...], g_ref[...], m[s0:s0 + 1], m[s0 + 1:s0 + 2])
            dx, dgain, dshift, dscale = vjp(dxn_s[...])
            dxo = dxo_ref[...]
            dxi_ref[...] = dxo + dx
            xn_ref[...] = xn_s[...]
            do_ref[...] = do_s[...]
            dgate = jnp.sum(0.5 * dxo * y_ref[...], axis=0, keepdims=True)
            first = _first_of_group(i, dm, tm)
            _acc(dm_ref, (0, pl.ds(0, 1), slice(None)), dshift, first)
            _acc(dm_ref, (0, pl.ds(1, 1), slice(None)), dscale, first)
            _acc(dm_ref, (0, pl.ds(2, 1), slice(None)), dgate, first)
            _acc(dgain_ref, (slice(None), slice(None)), dgain, i == 0)

    row = pl.BlockSpec((tm, D), lambda i, k: (i, 0))
    slab = pl.BlockSpec((1, tm, nf), lambda i, k: (k, i, 0))
    return pl.pallas_call(
        body, name=name, grid=(rows // tm, nk),
        in_specs=[row, row, row, _mod_spec(dm, 2, tm), pl.BlockSpec((1, D), lambda i, k: (0, 0)),
                  _wspec(gwg.shape, l, s), _wspec(gwu.shape, l, s), _wspec(gwd.shape, l, s)],
        out_specs=[row, row, row, slab, slab, slab,
                   pl.BlockSpec((1, 3, D), lambda i, k: (_grp(i, dm, tm), 0, 0)),
                   pl.BlockSpec((1, D), lambda i, k: (0, 0))],
        out_shape=[_sds((rows, D), F32), _sds((rows, D), BF16), _sds((rows, D), BF16),
                   _sds((nk, rows, nf), BF16), _sds((nk, rows, nf), BF16), _sds((nk, rows, nf), BF16),
                   _sds((ngr, 3, D), F32), _sds((1, D), F32)],
        scratch_shapes=[pltpu.VMEM((tm, D), BF16), pltpu.VMEM((tm, D), BF16), pltpu.VMEM((tm, D), F32)],
        compiler_params=_cp(),
    )(X, dXo, Y, MOD, gain, gwg, gwu, gwd)


def ffn_dw(XN, DO, H, DG, DU, dm, name):
    rows, D = XN.shape
    nf = H.shape[2]
    tt = _ffn_tile(dm)
    nT = rows // tt

    def body(xn_ref, do_ref, h_ref, dg_ref, du_ref, dwg_ref, dwu_ref, dwd_ref, ag_s, au_s, ad_s):
        t = pl.program_id(1)

        @pl.when(t == 0)
        def _():
            ag_s[...] = jnp.zeros_like(ag_s)
            au_s[...] = jnp.zeros_like(au_s)
            ad_s[...] = jnp.zeros_like(ad_s)

        xn = xn_ref[...]
        ag_s[...] += _dot_tn(xn, dg_ref[0])
        au_s[...] += _dot_tn(xn, du_ref[0])
        ad_s[...] += _dot_tn(h_ref[0], do_ref[...])

        @pl.when(t == nT - 1)
        def _():
            dwg_ref[0] = ag_s[...].astype(BF16)
            dwu_ref[0] = au_s[...].astype(BF16)
            dwd_ref[0] = ad_s[...].astype(BF16)

    row = pl.BlockSpec((tt, D), lambda k, t: (t, 0))
    slab = pl.BlockSpec((1, tt, nf), lambda k, t: (k, t, 0))
    return pl.pallas_call(
        body, name=name, grid=(N_DEV, nT),
        in_specs=[row, row, slab, slab, slab],
        out_specs=[pl.BlockSpec((1, D, nf), lambda k, t: (k, 0, 0)), pl.BlockSpec((1, D, nf), lambda k, t: (k, 0, 0)),
                   pl.BlockSpec((1, nf, D), lambda k, t: (k, 0, 0))],
        out_shape=[_sds((N_DEV, D, nf), BF16), _sds((N_DEV, D, nf), BF16), _sds((N_DEV, nf, D), BF16)],
        scratch_shapes=[pltpu.VMEM((D, nf), F32), pltpu.VMEM((D, nf), F32), pltpu.VMEM((nf, D), F32)],
        compiler_params=_cp(),
    )(XN, DO, H, DG, DU)


def atb(A, B, rows, dm, name):
    Ka, Nb = A.shape[1], B.shape[1]
    tk, tn = _pick(Ka, 1024), _pick(Nb, 1536)
    tt = _ffn_tile(dm)
    nT = rows // tt

    def body(a_ref, b_ref, o_ref, acc_s):
        t = pl.program_id(2)

        @pl.when(t == 0)
        def _():
            acc_s[...] = jnp.zeros_like(acc_s)

        acc_s[...] += _dot_tn(a_ref[...], b_ref[...])

        @pl.when(t == nT - 1)
        def _():
            o_ref[...] = acc_s[...].astype(BF16)

    return pl.pallas_call(
        body, name=name, grid=(Ka // tk, Nb // tn, nT),
        in_specs=[pl.BlockSpec((tt, tk), lambda i, j, t: (t, i)), pl.BlockSpec((tt, tn), lambda i, j, t: (t, j))],
        out_specs=pl.BlockSpec((tk, tn), lambda i, j, t: (i, j)),
        out_shape=_sds((Ka, Nb), BF16), scratch_shapes=[pltpu.VMEM((tk, tn), F32)], compiler_params=_cp(),
    )(A, B)


def modmm(X, MOD, gain, W, s0, dm, name):
    TM, D = dm.TM, dm.D
    Nc = W.shape[1]
    tn = _pick(Nc, 1536)
    nj = Nc // tn

    def body(x_ref, m_ref, g_ref, w_ref, p_ref, xn_ref):
        @pl.when(pl.program_id(1) == 0)
        def _():
            m = m_ref[0]
            xn_ref[...] = _modulate(x_ref[...], g_ref[...], m[s0:s0 + 1], m[s0 + 1:s0 + 2]).astype(BF16)

        p_ref[...] = jnp.dot(xn_ref[...], w_ref[...], preferred_element_type=F32)

    row = pl.BlockSpec((TM, D), lambda i, j: (i, 0))
    return pl.pallas_call(
        body, name=name, grid=(dm.nt, nj),
        in_specs=[row, _mod_spec(dm, 2), pl.BlockSpec((1, D), lambda i, j: (0, 0)),
                  pl.BlockSpec((D, tn), lambda i, j: (0, j))],
        out_specs=[pl.BlockSpec((TM, tn), lambda i, j: (i, j)), row],
        out_shape=[_sds((dm.T, Nc), F32), _sds((dm.T, D), BF16)],
        compiler_params=_cp(),
    )(X, MOD, gain, W)


def mixin_bwd(dP, W, X, dXres, MOD, gain, s0, dm, name):
    TM, D = dm.TM, dm.D
    K = dP.shape[1]

    def body(dp_ref, w_ref, x_ref, dr_ref, m_ref, g_ref, dx_ref, dm_ref, dgain_ref):
        i = pl.program_id(0)
        dxn = _dot_nt(dp_ref[...], w_ref[...])
        m = m_ref[0]
        _, vjp = jax.vjp(_modulate, x_ref[...], g_ref[...], m[s0:s0 + 1], m[s0 + 1:s0 + 2])
        dx, dgain, dshift, dscale = vjp(dxn)
        dx_ref[...] = dr_ref[...] + dx
        first = _first_of_group(i, dm)
        _acc(dm_ref, (0, pl.ds(0, 1), slice(None)), dshift, first)
        _acc(dm_ref, (0, pl.ds(1, 1), slice(None)), dscale, first)
        _acc(dgain_ref, (slice(None), slice(None)), dgain, i == 0)

    row = pl.BlockSpec((TM, D), lambda i: (i, 0))
    return pl.pallas_call(
        body, name=name, grid=(dm.nt,),
        in_specs=[pl.BlockSpec((TM, K), lambda i: (i, 0)), pl.BlockSpec((D, K), lambda i: (0, 0)), row, row,
                  _mod_spec(dm, 1), pl.BlockSpec((1, D), lambda i: (0, 0))],
        out_specs=[row, pl.BlockSpec((1, 2, D), lambda i: (_grp(i, dm), 0, 0)), pl.BlockSpec((1, D), lambda i: (0, 0))],
        out_shape=[_sds((dm.T, D), F32), _sds((dm.G, 2, D), F32), _sds((1, D), F32)],
        compiler_params=_cp(),
    )(dP, W, X, dXres, MOD, gain)


def proj_res(As, Ws, X, MOD, dm, ntiles, name):
    TM, D = dm.TM, dm.D
    n = len(As)
    rows = ntiles * TM

    def body(*refs):
        a_refs, w_refs = refs[:n], refs[n:2 * n]
        x_ref, m_ref, xo_ref, y_ref = refs[2 * n:]
        y = jnp.dot(a_refs[0][...], w_refs[0][...], preferred_element_type=F32)
        for a, w in zip(a_refs[1:], w_refs[1:]):
            y += jnp.dot(a[...], w[...], preferred_element_type=F32)
        y_ref[...] = y
        xo_ref[...] = x_ref[...] + m_ref[0][5:6] * y

    row = pl.BlockSpec((TM, D), lambda i: (i, 0))
    return pl.pallas_call(
        body, name=name, grid=(ntiles,),
        in_specs=[pl.BlockSpec((TM, a.shape[1]), lambda i: (i, 0)) for a in As]
        + [pl.BlockSpec(w.shape, lambda i: (0, 0)) for w in Ws] + [row, _mod_spec(dm, 1)],
        out_specs=[row, row], out_shape=[_sds((rows, D), F32), _sds((rows, D), F32)],
        compiler_params=_cp(),
    )(*As, *Ws, X, MOD)


def proj_res_bwd(dXo, Y, MOD, Ws, dm, ntiles, name):
    TM, D = dm.TM, dm.D
    n = len(Ws)
    rows = ntiles * TM
    ngr = dm.G if ntiles == dm.nt else dm.Bl

    def body(*refs):
        dxo_ref, y_ref, m_ref = refs[:3]
        w_refs = refs[3:3 + n]
        dy_ref = refs[3 + n]
        da_refs = refs[4 + n:4 + 2 * n]
        dgate_ref = refs[4 + 2 * n]
        i = pl.program_id(0)
        dxo = dxo_ref[...]
        dy = (m_ref[0][5:6] * dxo).astype(BF16)
        dy_ref[...] = dy
        for w, da in zip(w_refs, da_refs):
            da[...] = _dot_nt(dy, w[...])
        dgate = jnp.sum(dxo * y_ref[...], axis=0, keepdims=True)
        _acc(dgate_ref, (0, slice(None), slice(None)), dgate, _first_of_group(i, dm))

    row = pl.BlockSpec((TM, D), lambda i: (i, 0))
    return pl.pallas_call(
        body, name=name, grid=(ntiles,),
        in_specs=[row, row, _mod_spec(dm, 1)] + [pl.BlockSpec(w.shape, lambda i: (0, 0)) for w in Ws],
        out_specs=[row] + [pl.BlockSpec((TM, w.shape[0]), lambda i: (i, 0)) for w in Ws]
        + [pl.BlockSpec((1, 1, D), lambda i: (_grp(i, dm), 0, 0))],
        out_shape=[_sds((rows, D), BF16)] + [_sds((rows, w.shape[0]), F32) for w in Ws] + [_sds((ngr, 1, D), F32)],
        compiler_params=_cp(),
    )(dXo, Y, MOD, *Ws)


def loss_head(Xf, target, dm, name):
    TM, D = dm.TM, dm.D

    def body(x_ref, t_ref, l_ref, dx_ref, acc_s):
        i = pl.program_id(0)
        e = x_ref[...] - t_ref[...]
        dx_ref[...] = e * (1.0 / D)

        @pl.when(i == 0)
        def _():
            acc_s[...] = jnp.zeros_like(acc_s)

        acc_s[...] += jnp.sum(e * e, axis=0, keepdims=True)

        @pl.when(i == dm.ntx - 1)
        def _():
            tot = jnp.sum(acc_s[...], axis=1, keepdims=True) * (0.5 / D)
            l_ref[...] = jnp.broadcast_to(tot, (1, LANE))

    row = pl.BlockSpec((TM, D), lambda i: (i, 0))
    return pl.pallas_call(
        body, name=name, grid=(dm.ntx,), in_specs=[row, row],
        out_specs=[pl.BlockSpec((1, LANE), lambda i: (0, 0)), row],
        out_shape=[_sds((1, LANE), F32), _sds((dm.Tx, D), F32)],
        scratch_shapes=[pltpu.VMEM((1, D), F32)], compiler_params=_cp(),
    )(Xf, target)


def _qk_fn(p, gain, cs, sneg, spos):
    y = p * lax.rsqrt(jnp.mean(p * p, axis=-1, keepdims=True) + EPS) * gain
    return _rope(y, cs, sneg, spos)


def _tab_specs(dm, swap):
    def idx(i):
        return jnp.where(i < dm.ntx, i % dm.tps, dm.tps)
    if swap:
        return [pl.BlockSpec((dm.TM, HEAD), lambda j, i: (idx(i), 0))] * 3
    return [pl.BlockSpec((dm.TM, HEAD), lambda i, j: (idx(i), 0))] * 3


def qkv_prep(P0, qkg, tabs, dm, name):
    TM = dm.TM

    def body(p_ref, g_ref, cs_ref, sn_ref, sp_ref, o_ref):
        j = pl.program_id(1)

        @pl.when(j < 6)
        def _():
            o_ref[...] = _qk_fn(p_ref[...], g_ref[0], cs_ref[...], sn_ref[...], sp_ref[...]).astype(BF16)

        @pl.when(j >= 6)
        def _():
            o_ref[...] = p_ref[...].astype(BF16)

    blk = pl.BlockSpec((TM, HEAD), lambda i, j: (i, j))
    return pl.pallas_call(
        body, name=name, grid=(dm.nt, 8),
        in_specs=[blk, pl.BlockSpec((1, 1, HEAD), lambda i, j: (jnp.minimum(j // 4, 1), 0, 0))] + _tab_specs(dm, False),
        out_specs=blk, out_shape=_sds((dm.T, 8 * HEAD), BF16), compiler_params=_cp(),
    )(P0, qkg, *tabs)


def qkv_prep_bwd(P0, dQKV, qkg, tabs, dm, name):
    TM = dm.TM

    def body(p_ref, d_ref, g_ref, cs_ref, sn_ref, sp_ref, dp_ref, dg_ref):
        j, i = pl.program_id(0), pl.program_id(1)
        first = (i == 0) & ((j == 0) | (j == 4))

        @pl.when(j < 6)
        def _():
            _, vjp = jax.vjp(_qk_fn, p_ref[...], g_ref[0], cs_ref[...], sn_ref[...], sp_ref[...])
            dp, dg = vjp(d_ref[...])[:2]
            dp_ref[...] = dp
            _acc(dg_ref, (0, slice(None), slice(None)), dg, first)

        @pl.when(j >= 6)
        def _():
            dp_ref[...] = d_ref[...]

    blk = pl.BlockSpec((TM, HEAD), lambda j, i: (i, j))
    return pl.pallas_call(
        body, name=name, grid=(8, dm.nt),
        in_specs=[blk, blk, pl.BlockSpec((1, 1, HEAD), lambda j, i: (jnp.minimum(j // 4, 1), 0, 0))] + _tab_specs(dm, True),
        out_specs=[blk, pl.BlockSpec((1, 1, HEAD), lambda j, i: (jnp.minimum(j // 4, 1), 0, 0))],
        out_shape=[_sds((dm.T, 8 * HEAD), F32), _sds((2, 1, HEAD), F32)], compiler_params=_cp(),
    )(P0, dQKV, qkg, *tabs)


def _softmax2(sx, sh):
    m = jnp.max(sh, axis=-1, keepdims=True)
    if sx is not None:
        m = jnp.maximum(m, jnp.max(sx, axis=-1, keepdims=True))
    eh = jnp.exp(sh - m)
    l = jnp.sum(eh, axis=-1, keepdims=True)
    ex = None
    if sx is not None:
        ex = jnp.exp(sx - m)
        l = l + jnp.sum(ex, axis=-1, keepdims=True)
    inv = 1.0 / l
    return (None if ex is None else ex * inv), eh * inv


def _attn_geometry(dm, with_x):
    TQ = dm.TM
    if with_x:
        nq, qoff = dm.N // TQ, 0
    else:
        nq, qoff = dm.M // TQ, dm.Tx // TQ
    hoff = dm.Tx // dm.M
    return TQ, nq, qoff, hoff


def attn_fwd(QKV, dm, with_x, name):
    TQ, nq, qoff, hoff = _attn_geometry(dm, with_x)
    scale = HEAD ** -0.5
    rows = dm.Tx if with_x else dm.Th

    def body(*refs):
        if with_x:
            q_ref, kh_ref, vh_ref, kx_ref, vx_ref, o_ref = refs
        else:
            q_ref, kh_ref, vh_ref, o_ref = refs
        q = q_ref[...]
        sh = _dot_nt(q, kh_ref[...]) * scale
        sx = _dot_nt(q, kx_ref[...]) * scale if with_x else None
        px, ph = _softmax2(sx, sh)
        o = _dot(ph, vh_ref[...])
        if with_x:
            o = o + _dot(px, vx_ref[...])
        o_ref[...] = o.astype(BF16)

    qs = pl.BlockSpec((TQ, HEAD), lambda b, kv, g, qi: (qoff + b * nq + qi, kv * 2 + g))
    in_specs = [qs, pl.BlockSpec((dm.M, HEAD), lambda b, kv, g, qi: (hoff + b, 4 + kv)),
                pl.BlockSpec((dm.M, HEAD), lambda b, kv, g, qi: (hoff + b, 6 + kv))]
    args = [QKV, QKV, QKV]
    if with_x:
        in_specs += [pl.BlockSpec((dm.N, HEAD), lambda b, kv, g, qi: (b, 4 + kv)),
                     pl.BlockSpec((dm.N, HEAD), lambda b, kv, g, qi: (b, 6 + kv))]
        args += [QKV, QKV]
    return pl.pallas_call(
        body, name=name, grid=(dm.Bl, A_KV, 2, nq), in_specs=in_specs,
        out_specs=pl.BlockSpec((TQ, HEAD), lambda b, kv, g, qi: (b * nq + qi, kv * 2 + g)),
        out_shape=_sds((rows, A_HEADS * HEAD), BF16), compiler_params=_cp(),
    )(*args)


def attn_bwd(QKV, dO, dm, with_x, init, name):
    TQ, nq, qoff, hoff = _attn_geometry(dm, with_x)
    scale = HEAD ** -0.5
    rows = dm.Tx if with_x else dm.Th

    def body(*refs):
        if with_x:
            (q_ref, kh_ref, vh_ref, kx_ref, vx_ref, do_ref, ikh_ref, ivh_ref,
             dq_ref, dkh_ref, dvh_ref, dkx_ref, dvx_ref) = refs
        else:
            q_ref, kh_ref, vh_ref, do_ref, dq_ref, dkh_ref, dvh_ref = refs
        g, qi = pl.program_id(2), pl.program_id(3)
        q = q_ref[...]
        kh, vh = kh_ref[...], vh_ref[...]
        sh = _dot_nt(q, kh) * scale
        sx = _dot_nt(q, kx_ref[...]) * scale if with_x else None
        px, ph = _softmax2(sx, sh)
        dob = do_ref[...].astype(BF16)
        dph = _dot_nt(dob, vh)
        delta = jnp.sum(dph * ph, axis=-1, keepdims=True)
        if with_x:
            dpx = _dot_nt(dob, vx_ref[...])
            delta = delta + jnp.sum(dpx * px, axis=-1, keepdims=True)
        dsh = (ph * (dph - delta) * scale).astype(BF16)
        dq = _dot(dsh, kh)
        first = (g == 0) & (qi == 0)

        @pl.when(first)
        def _():
            if with_x:
                dkh_ref[...] = ikh_ref[...]
                dvh_ref[...] = ivh_ref[...]
                dkx_ref[...] = jnp.zeros_like(dkx_ref)
                dvx_ref[...] = jnp.zeros_like(dvx_ref)
            else:
                dkh_ref[...] = jnp.zeros_like(dkh_ref)
                dvh_ref[...] = jnp.zeros_like(dvh_ref)

        dkh_ref[...] += _dot_tn(dsh, q)
        dvh_ref[...] += _dot_tn(ph, dob)
        if with_x:
            dsx = (px * (dpx - delta) * scale).astype(BF16)
            dq = dq + _dot(dsx, kx_ref[...])
            dkx_ref[...] += _dot_tn(dsx, q)
            dvx_ref[...] += _dot_tn(px, dob)
        dq_ref[...] = dq

    qs = pl.BlockSpec((TQ, HEAD), lambda b, kv, g, qi: (qoff + b * nq + qi, kv * 2 + g))
    hs = lambda c0: pl.BlockSpec((dm.M, HEAD), lambda b, kv, g, qi: (hoff + b, c0 + kv))
    xs = lambda c0: pl.BlockSpec((dm.N, HEAD), lambda b, kv, g, qi: (b, c0 + kv))
    dos = pl.BlockSpec((TQ, HEAD), lambda b, kv, g, qi: (b * nq + qi, kv * 2 + g))
    acc_h = pl.BlockSpec((dm.M, HEAD), lambda b, kv, g, qi: (b, kv))
    acc_x = pl.BlockSpec((dm.N, HEAD), lambda b, kv, g, qi: (b, kv))
    in_specs, args = [qs, hs(4), hs(6)], [QKV, QKV, QKV]
    out_specs = [dos, acc_h, acc_h]
    out_shape = [_sds((rows, A_HEADS * HEAD), F32), _sds((dm.Th, A_KV * HEAD), F32), _sds((dm.Th, A_KV * HEAD), F32)]
    if with_x:
        in_specs += [xs(4), xs(6), dos, acc_h, acc_h]
        args += [QKV, QKV, dO, init[0], init[1]]
        out_specs += [acc_x, acc_x]
        out_shape += [_sds((dm.Tx, A_KV * HEAD), F32), _sds((dm.Tx, A_KV * HEAD), F32)]
    else:
        in_specs += [dos]
        args += [dO]
    return pl.pallas_call(
        body, name=name, grid=(dm.Bl, A_KV, 2, nq), in_specs=in_specs, out_specs=out_specs,
        out_shape=out_shape, compiler_params=_cp(),
    )(*args)


def _pool_mean(u, w):
    n = u.shape[0]
    t = lax.broadcasted_iota(jnp.int32, (n, 1), 0)
    cnt = (jnp.clip(t + (w - w // 2), 0, n) - jnp.clip(t - w // 2, 0, n)).astype(F32)
    s = _shift_rows(u, -(w // 2))
    for j in range(-(w // 2) + 1, w - w // 2):
        s = s + _shift_rows(u, j)
    return s / cnt - u


def pool_fwd(P0, pw, pscale, dm, on_x, name):
    n, off, rows = (dm.N, 0, dm.Tx) if on_x else (dm.M, dm.Tx // dm.M, dm.Th)
    ng = len(POOL_WINDOWS)

    def body(u_ref, w_ref, s_ref, o_ref):
        for g, w in enumerate(POOL_WINDOWS):
            cols = pl.ds(g * HEAD, HEAD)
            pooled = _pool_mean(u_ref[:, cols], w)
            o_ref[:, cols] = (_dot(pooled, w_ref[g]) * s_ref[:, cols]).astype(BF16)

    return pl.pallas_call(
        body, name=name, grid=(dm.Bl,),
        in_specs=[pl.BlockSpec((n, ng * HEAD), lambda b: (off + b, 2)),
                  pl.BlockSpec((ng, HEAD, HEAD), lambda b: (0, 0, 0)), pl.BlockSpec((1, ng * HEAD), lambda b: (0, 0))],
        out_specs=pl.BlockSpec((n, ng * HEAD), lambda b: (b, 0)),
        out_shape=_sds((rows, ng * HEAD), BF16), compiler_params=_cp(),
    )(P0, pw, pscale)


def pool_bwd(P0, dY, pw, pwT, pscale, dm, on_x, name):
    n, off, rows = (dm.N, 0, dm.Tx) if on_x else (dm.M, dm.Tx // dm.M, dm.Th)
    ng = len(POOL_WINDOWS)

    def body(u_ref, dy_ref, w_ref, wt_ref, s_ref, du_ref, dw_ref, ds_ref):
        b = pl.program_id(0)
        for g, w in enumerate(POOL_WINDOWS):
            cols = pl.ds(g * HEAD, HEAD)
            pooled, vjp = jax.vjp(lambda u: _pool_mean(u, w), u_ref[:, cols])
            pre = _dot(pooled, w_ref[g])
            dy = dy_ref[:, cols]
            dpre = dy * s_ref[:, cols]
            du_ref[:, cols] = vjp(_dot(dpre, wt_ref[g]))[0]
            _acc(dw_ref, (g, slice(None), slice(None)), _dot_tn(pooled, dpre), b == 0)
            _acc(ds_ref, (slice(None), cols), jnp.sum(dy * pre, axis=0, keepdims=True), b == 0)

    full = pl.BlockSpec((ng, HEAD, HEAD), lambda b: (0, 0, 0))
    vec = pl.BlockSpec((1, ng * HEAD), lambda b: (0, 0))
    return pl.pallas_call(
        body, name=name, grid=(dm.Bl,),
        in_specs=[pl.BlockSpec((n, ng * HEAD), lambda b: (off + b, 2)), pl.BlockSpec((n, ng * HEAD), lambda b: (b, 0)),
                  full, full, vec],
        out_specs=[pl.BlockSpec((n, ng * HEAD), lambda b: (b, 0)), full, vec],
        out_shape=[_sds((rows, ng * HEAD), F32), _sds((ng, HEAD, HEAD), F32), _sds((1, ng * HEAD), F32)],
        compiler_params=_cp(),
    )(P0, dY, pw, pwT, pscale)


def _conv_fn(p, w0, w1, w2, kind):
    c = w0 * _shift_rows(p, -1) + w1 * p + w2 * _shift_rows(p, 1)
    a = _silu(c)
    if kind == 2:
        return a
    a = a * lax.rsqrt(jnp.sum(a * a, axis=-1, keepdims=True) + EPS)
    return a * (HEAD ** -0.5) if kind == 0 else a


def gdn_prep(P1, conv_w, dm, on_x, name):
    n, off, rows = (dm.N, 0, dm.Tx) if on_x else (dm.M, dm.Tx // dm.M, dm.Th)

    def body(p_ref, w_ref, o_ref):
        j = pl.program_id(1)
        p, w = p_ref[...], w_ref[...]
        for kind in range(3):
            @pl.when(j // C_HEADS == kind)
            def _():
                o_ref[...] = _conv_fn(p, w[0:1], w[1:2], w[2:3], kind)

    return pl.pallas_call(
        body, name=name, grid=(dm.Bl, 3 * C_HEADS),
        in_specs=[pl.BlockSpec((n, HEAD), lambda b, j: (off + b, j)), pl.BlockSpec((3, HEAD), lambda b, j: (0, j))],
        out_specs=pl.BlockSpec((n, HEAD), lambda b, j: (b, j)),
        out_shape=_sds((rows, 3 * C_HEADS * HEAD), F32), compiler_params=_cp(),
    )(P1, conv_w)


def gdn_prep_bwd(P1, dQ, conv_w, dm, on_x, name):
    n, off, rows = (dm.N, 0, dm.Tx) if on_x else (dm.M, dm.Tx // dm.M, dm.Th)

    def body(p_ref, d0_ref, d1_ref, w_ref, dp_ref, dw_ref):
        j, b = pl.program_id(0), pl.program_id(1)
        p, w = p_ref[...], w_ref[...]
        for kind in range(3):
            @pl.when(j // C_HEADS == kind)
            def _():
                _, vjp = jax.vjp(functools.partial(_conv_fn, kind=kind), p, w[0:1], w[1:2], w[2:3])
                dp, d0, d1, d2 = vjp(d0_ref[0] + d1_ref[0])
                dp_ref[...] = dp
                _acc(dw_ref, (pl.ds(0, 1), slice(None)), d0, b == 0)
                _acc(dw_ref, (pl.ds(1, 1), slice(None)), d1, b == 0)
                _acc(dw_ref, (pl.ds(2, 1), slice(None)), d2, b == 0)

    return pl.pallas_call(
        body, name=name, grid=(3 * C_HEADS, dm.Bl),
        in_specs=[pl.BlockSpec((n, HEAD), lambda j, b: (off + b, j)),
                  pl.BlockSpec((1, n, HEAD), lambda j, b: (0, off + b, j)), pl.BlockSpec((1, n, HEAD), lambda j, b: (1, off + b, j)),
                  pl.BlockSpec((3, HEAD), lambda j, b: (0, j))],
        out_specs=[pl.BlockSpec((n, HEAD), lambda j, b: (b, j)), pl.BlockSpec((3, HEAD), lambda j, b: (0, j))],
        out_shape=[_sds((rows, 3 * C_HEADS * HEAD), F32), _sds((3, 3 * C_HEADS * HEAD), F32)],
        compiler_params=_cp(),
    )(P1, dQ, dQ, conv_w)


def _gate_fn(ab, par):
    lane = lax.broadcasted_iota(jnp.int32, ab.shape, 1)
    is_a = (lane % 16) < C_HEADS
    g = -jnp.exp(par[0:1]) * jax.nn.softplus(ab + par[1:2])
    return jnp.where(lane < 4 * C_HEADS, jnp.where(is_a, g, jax.nn.sigmoid(ab)), 0.0)


def _col(blk, idx):
    lane = lax.broadcasted_iota(jnp.int32, blk.shape, 1)
    return jnp.sum(jnp.where(lane == idx, blk, 0.0), axis=1, keepdims=True)


def _chunk_masks(rev):
    ii = lax.broadcasted_iota(jnp.int32, (CHUNK, CHUNK), 0)
    jj = lax.broadcasted_iota(jnp.int32, (CHUNK, CHUNK), 1)
    ahead = jnp.where(rev, jj - ii, ii - jj)
    return ahead >= 0, ahead > 0, (ii == jj).astype(F32)


def _inv_unit_tri(nmat, eye):
    x = eye - nmat
    p = _hdot(nmat, nmat)
    step = 2
    while True:
        x = x + _hdot(x, p)
        step *= 2
        if step >= CHUNK:
            break
        p = _hdot(p, p)
    return x


def _cum_lanes(x, transpose=False):
    lane = lax.broadcasted_iota(jnp.int32, x.shape, 1)
    down, up = x, x
    s = 1
    while s < CHUNK:
        down = down + _shift_rows(down, -s)
        up = up + _shift_rows(up, s)
        s *= 2
    return jnp.where((lane >= 16) if transpose else (lane < 16), down, up)


def _chunk_common(q, k, v, gc, gcr, total, beta, rev):
    incl, strict, eye = _chunk_masks(rev)
    e = jnp.exp(jnp.where(incl, gc - gcr, NEG))
    eg, et, gt = jnp.exp(gc), jnp.exp(total - gc), jnp.exp(total)
    kb = k * beta
    kk = _dot_nt(kb, k)
    nmat = jnp.where(strict, kk * e, 0.0)
    ainv = _inv_unit_tri(nmat, eye)
    rhs = jnp.concatenate([v * beta, kb * eg], axis=1)
    sol = _hdot(ainv, rhs)
    qq = _dot_nt(q, k)
    return dict(incl=incl, strict=strict, e=e, eg=eg, et=et, gt=gt, kb=kb, kk=kk, ainv=ainv, sol=sol, qq=qq)


def _chunk_fwd(q, k, v, gc, gcr, total, beta, rev):
    c = _chunk_common(q, k, v, gc, gcr, total, beta, rev)
    u, w = c["sol"][:, :HEAD], c["sol"][:, HEAD:]
    qk = jnp.where(c["incl"], c["qq"] * c["e"], 0.0)
    return u, w, k * c["et"], q * c["eg"], qk, c["gt"]


def _chunk_bwd(q, k, v, gc, gcr, total, beta, rev, du, dw, dkt, dqd, dqk, dgt):
    c = _chunk_common(q, k, v, gc, gcr, total, beta, rev)
    incl, strict, e, eg, et, gt, kb = c["incl"], c["strict"], c["e"], c["eg"], c["et"], c["gt"], c["kb"]
    drhs = _hdot_tn(c["ainv"], jnp.concatenate([du, dw], axis=1))
    dn = jnp.where(strict, -_hdot_nt(drhs, c["sol"]), 0.0)
    dkk = dn * e
    dqq = jnp.where(incl, dqk, 0.0) * e
    de = dn * c["kk"] + jnp.where(incl, dqk, 0.0) * c["qq"]
    dq = _dot(dqq, k) + dqd * eg
    dk = _dot_tn(dqq, q) + _dot_tn(dkk, kb) + dkt * et
    dkb = _dot(dkk, k) + drhs[:, HEAD:] * eg
    dv = drhs[:, :HEAD] * beta
    dbeta = jnp.sum(drhs[:, :HEAD] * v + dkb * k, axis=1, keepdims=True)
    dk = dk + dkb * beta
    deg = jnp.sum(drhs[:, HEAD:] * kb + dqd * q, axis=1, keepdims=True)
    dd = de * e
    dtd = jnp.sum(dkt * k, axis=1, keepdims=True) * et
    dgc = deg * eg - dtd + jnp.sum(dd, axis=1, keepdims=True) - jnp.sum(dd.T, axis=1, keepdims=True)
    dtotal = jnp.sum(dtd, axis=0, keepdims=True) + dgt * gt
    return dq, dk, dv, dgc, dtotal, dbeta


def gdn_chunk_pre(QKVg, P1, par, dm, name):
    nch = dm.T // CHUNK
    HD = C_HEADS * HEAD
    abcol = (4 * HD) // LANE

    def body(x_ref, ab_ref, par_ref, u_ref, w_ref, kt_ref, qd_ref, qk_ref, gt_ref, gct_s):
        d = pl.program_id(1)
        rev = d == 1
        gb = _gate_fn(ab_ref[...], par_ref[...])
        gcl = _cum_lanes(gb)
        gct_s[...] = gcl.T
        tot = jnp.sum(gb, axis=0, keepdims=True)
        for h in range(C_HEADS):
            cols = pl.ds(h * HEAD, HEAD)
            idx = d * 16 + h
            q = x_ref[:, cols]
            k = x_ref[:, pl.ds((C_HEADS + h) * HEAD, HEAD)]
            v = x_ref[:, pl.ds((2 * C_HEADS + h) * HEAD, HEAD)]
            u, w, kt, qd, qk, gt = _chunk_fwd(q, k, v, _col(gcl, idx), gct_s[pl.ds(idx, 1), :], _col(tot, idx),
                                              _col(gb, idx + 8), rev)
            u_ref[0, :, cols] = u
            w_ref[0, :, cols] = w.astype(BF16)
            kt_ref[0, :, cols] = kt.astype(BF16)
            qd_ref[0, :, cols] = qd.astype(BF16)
            qk_ref[0, :, cols] = jnp.concatenate([qk, jnp.zeros_like(qk)], axis=1).astype(BF16)
            gt_ref[0, 0, pl.ds(h, 1), :] = jnp.broadcast_to(gt, (1, HEAD))

    big = pl.BlockSpec((1, CHUNK, HD), lambda i, d: (d, i, 0))
    return pl.pallas_call(
        body, name=name, grid=(nch, 2),
        in_specs=[pl.BlockSpec((CHUNK, 3 * HD), lambda i, d: (i, 0)), pl.BlockSpec((CHUNK, LANE), lambda i, d: (i, abcol)),
                  pl.BlockSpec((2, LANE), lambda i, d: (0, 0))],
        out_specs=[big, big, big, big, big, pl.BlockSpec((1, 1, C_HEADS, HEAD), lambda i, d: (d, i, 0, 0))],
        out_shape=[_sds((2, dm.T, HD), F32), _sds((2, dm.T, HD), BF16), _sds((2, dm.T, HD), BF16),
                   _sds((2, dm.T, HD), BF16), _sds((2, dm.T, HD), BF16), _sds((2, nch, C_HEADS, HEAD), F32)],
        scratch_shapes=[pltpu.VMEM((LANE, CHUNK), F32)], compiler_params=_cp(),
    )(QKVg, P1, par)


def gdn_chunk_pre_bwd(QKVg, P1, par, dU, dW, dKT, dQD, dQK, dGT, dm, name):
    nch = dm.T // CHUNK
    HD = C_HEADS * HEAD
    abcol = (4 * HD) // LANE

    def body(x_ref, ab_ref, par_ref, du_ref, dw_ref, dkt_ref, dqd_ref, dqk_ref, dgt_ref, dx_ref, dab_ref, dpar_ref, gct_s):
        i, d = pl.program_id(0), pl.program_id(1)
        rev = d == 1
        ab, par = ab_ref[...], par_ref[...]
        gb, gate_vjp = jax.vjp(_gate_fn, ab, par)
        gcl = _cum_lanes(gb)
        gct_s[...] = gcl.T
        tot = jnp.sum(gb, axis=0, keepdims=True)
        lane = lax.broadcasted_iota(jnp.int32, (CHUNK, LANE), 1)
        dgcl = jnp.zeros((CHUNK, LANE), F32)
        dgb = jnp.zeros((CHUNK, LANE), F32)
        first = d == 0
        for h in range(C_HEADS):
            cols = pl.ds(h * HEAD, HEAD)
            kcols = pl.ds((C_HEADS + h) * HEAD, HEAD)
            vcols = pl.ds((2 * C_HEADS + h) * HEAD, HEAD)
            idx = d * 16 + h
            dq, dk, dv, dgc, dtotal, dbeta = _chunk_bwd(
                x_ref[:, cols], x_ref[:, kcols], x_ref[:, vcols],
                _col(gcl, idx), gct_s[pl.ds(idx, 1), :], _col(tot, idx), _col(gb, idx + 8), rev,
                du_ref[0, :, cols], dw_ref[0, :, cols], dkt_ref[0, :, cols], dqd_ref[0, :, cols],
                dqk_ref[0, :, pl.ds(h * HEAD, CHUNK)], dgt_ref[0, 0, pl.ds(h, 1), pl.ds(0, 1)])
            dx_ref[0, :, cols] = dq
            dx_ref[0, :, kcols] = dk
            dx_ref[0, :, vcols] = dv
            dgcl = dgcl + jnp.where(lane == idx, dgc, 0.0)
            dgb = dgb + jnp.where(lane == idx + 8, dbeta, 0.0) + jnp.where(lane == idx, dtotal, 0.0)
        dab, dpar = gate_vjp(dgb + _cum_lanes(dgcl, transpose=True))
        dab_ref[0] = dab
        _acc(dpar_ref, (slice(None), slice(None)), dpar, (i == 0) & first)

    big = pl.BlockSpec((1, CHUNK, HD), lambda i, d: (d, i, 0))
    return pl.pallas_call(
        body, name=name, grid=(nch, 2),
        in_specs=[pl.BlockSpec((CHUNK, 3 * HD), lambda i, d: (i, 0)), pl.BlockSpec((CHUNK, LANE), lambda i, d: (i, abcol)),
                  pl.BlockSpec((2, LANE), lambda i, d: (0, 0)), big, big, big, big, big,
                  pl.BlockSpec((1, 1, C_HEADS, HEAD), lambda i, d: (d, i, 0, 0))],
        out_specs=[pl.BlockSpec((1, CHUNK, 3 * HD), lambda i, d: (d, i, 0)), pl.BlockSpec((1, CHUNK, LANE), lambda i, d: (d, i, 0)),
                   pl.BlockSpec((2, LANE), lambda i, d: (0, 0))],
        out_shape=[_sds((2, dm.T, 3 * HD), F32), _sds((2, dm.T, LANE), F32), _sds((2, LANE), F32)],
        scratch_shapes=[pltpu.VMEM((LANE, CHUNK), F32)], compiler_params=_cp(),
    )(QKVg, P1, par, dU, dW, dKT, dQD, dQK, dGT)


def _scan_chunk(b, d, c, dm):
    nh, nx = dm.M // CHUNK, dm.N // CHUNK
    in_h = c < nh
    pos_h = jnp.where(d == 0, c, nh - 1 - c)
    pos_x = jnp.where(d == 0, c - nh, nx - 1 - (c - nh))
    return jnp.where(in_h, dm.Tx // CHUNK + b * nh + pos_h, b * nx + pos_x)


def gdn_scan_fwd(U, W, KT, QD, QK, GT, dm, name):
    nch = dm.T // CHUNK
    HD = C_HEADS * HEAD
    nsc = (dm.M + dm.N) // CHUNK

    def body(u_ref, w_ref, kt_ref, qd_ref, qk_ref, gt_ref, o_ref, ss_ref, s_s):
        @pl.when(pl.program_id(2) == 0)
        def _():
            s_s[...] = jnp.zeros_like(s_s)

        for h in range(C_HEADS):
            rows, cols = pl.ds(h * HEAD, HEAD), pl.ds(h * HEAD, HEAD)
            s = s_s[rows, :]
            ss_ref[0, 0, rows, :] = s
            sb = s.astype(BF16)
            vn = u_ref[0, :, cols] - jnp.dot(w_ref[0, :, cols], sb, preferred_element_type=F32)
            vnb = vn.astype(BF16)
            qk = qk_ref[0, :, pl.ds(h * HEAD, CHUNK)]
            o_ref[0, :, cols] = (jnp.dot(qd_ref[0, :, cols], sb, preferred_element_type=F32)
                                 + jnp.dot(qk, vnb, preferred_element_type=F32))
            s_s[rows, :] = s * gt_ref[0, 0, pl.ds(h, 1), :] + _dot_tn(kt_ref[0, :, cols], vnb)

    big = pl.BlockSpec((1, CHUNK, HD), lambda b, d, c: (d, _scan_chunk(b, d, c, dm), 0))
    return pl.pallas_call(
        body, name=name, grid=(dm.Bl, 2, nsc),
        in_specs=[big, big, big, big, big,
                  pl.BlockSpec((1, 1, C_HEADS, HEAD), lambda b, d, c: (d, _scan_chunk(b, d, c, dm), 0, 0))],
        out_specs=[big, pl.BlockSpec((1, 1, HD, HEAD), lambda b, d, c: (d, _scan_chunk(b, d, c, dm), 0, 0))],
        out_shape=[_sds((2, dm.T, HD), F32), _sds((2, nch, HD, HEAD), F32)],
        scratch_shapes=[pltpu.VMEM((HD, HEAD), F32)], compiler_params=_cp(),
    )(U, W, KT, QD, QK, GT)


def gdn_scan_bwd(dO, SS, U, W, KT, QD, QK, GT, dm, name):
    nch = dm.T // CHUNK
    HD = C_HEADS * HEAD
    nsc = (dm.M + dm.N) // CHUNK

    def body(do_ref, ss_ref, u_ref, w_ref, kt_ref, qd_ref, qk_ref, gt_ref,
             du_ref, dw_ref, dkt_ref, dqd_ref, dqk_ref, dgt_ref, ds_s):
        @pl.when(pl.program_id(2) == 0)
        def _():
            ds_s[...] = jnp.zeros_like(ds_s)

        for h in range(C_HEADS):
            rows, cols = pl.ds(h * HEAD, HEAD), pl.ds(h * HEAD, HEAD)
            s = ss_ref[0, 0, rows, :]
            sb = s.astype(BF16)
            w, kt, qd = w_ref[0, :, cols], kt_ref[0, :, cols], qd_ref[0, :, cols]
            qk = qk_ref[0, :, pl.ds(h * HEAD, CHUNK)]
            gt = gt_ref[0, 0, pl.ds(h, 1), :]
            vnb = (u_ref[0, :, cols] - jnp.dot(w, sb, preferred_element_type=F32)).astype(BF16)
            dob = do_ref[:, cols].astype(BF16)
            dsn = ds_s[rows, :]
            dsnb = dsn.astype(BF16)
            dqd_ref[0, :, cols] = _dot_nt(dob, sb)
            dqk = _dot_nt(dob, vnb)
            dqk_ref[0, :, cols] = jnp.concatenate([dqk, jnp.zeros_like(dqk)], axis=1)
            dvn = _dot_tn(qk, dob) + jnp.dot(kt, dsnb, preferred_element_type=F32)
            dvnb = dvn.astype(BF16)
            dkt_ref[0, :, cols] = _dot_nt(vnb, dsnb)
            du_ref[0, :, cols] = dvn
            dw_ref[0, :, cols] = -_dot_nt(dvnb, sb)
            dgt_ref[0, 0, pl.ds(h, 1), :] = jnp.broadcast_to(jnp.sum(dsn * s, keepdims=True), (1, HEAD))
            ds_s[rows, :] = dsn * gt + _dot_tn(qd, dob) - _dot_tn(w, dvnb)

    def mem(b, d, c):
        return _scan_chunk(b, d, nsc - 1 - c, dm)

    big = pl.BlockSpec((1, CHUNK, HD), lambda b, d, c: (d, mem(b, d, c), 0))
    gts = pl.BlockSpec((1, 1, C_HEADS, HEAD), lambda b, d, c: (d, mem(b, d, c), 0, 0))
    return pl.pallas_call(
        body, name=name, grid=(dm.Bl, 2, nsc),
        in_specs=[pl.BlockSpec((CHUNK, HD), lambda b, d, c: (mem(b, d, c), 0)),
                  pl.BlockSpec((1, 1, HD, HEAD), lambda b, d, c: (d, mem(b, d, c), 0, 0)), big, big, big, big, big, gts],
        out_specs=[big, big, big, big, big, gts],
        out_shape=[_sds((2, dm.T, HD), F32)] * 5 + [_sds((2, nch, C_HEADS, HEAD), F32)],
        scratch_shapes=[pltpu.VMEM((HD, HEAD), F32)], compiler_params=_cp(),
    )(dO, SS, U, W, KT, QD, QK, GT)


def _finish_fn(o, z, gain):
    y = o * lax.rsqrt(jnp.mean(o * o, axis=-1, keepdims=True) + EPS) * gain
    return y * _silu(z)


def gdn_finish(O, P1, og, dm, name):
    TM = dm.TM
    HD = C_HEADS * HEAD
    zc = (3 * HD) // HEAD

    def body(o0_ref, o1_ref, z_ref, g_ref, y_ref):
        y_ref[...] = _finish_fn(o0_ref[0] + o1_ref[0], z_ref[...], g_ref[...]).astype(BF16)

    return pl.pallas_call(
        body, name=name, grid=(dm.ntx, C_HEADS),
        in_specs=[pl.BlockSpec((1, TM, HEAD), lambda i, j: (0, i, j)), pl.BlockSpec((1, TM, HEAD), lambda i, j: (1, i, j)),
                  pl.BlockSpec((TM, HEAD), lambda i, j: (i, zc + j)), pl.BlockSpec((1, HEAD), lambda i, j: (0, 0))],
        out_specs=pl.BlockSpec((TM, HEAD), lambda i, j: (i, j)),
        out_shape=_sds((dm.Tx, HD), BF16), compiler_params=_cp(),
    )(O, O, P1, og)


def gdn_finish_bwd(O, P1, og, dY, dm, name):
    TM = dm.TM
    HD = C_HEADS * HEAD
    zc = (3 * HD) // HEAD

    def body(o0_ref, o1_ref, z_ref, g_ref, dy_ref, do_ref, dz_ref, dg_ref):
        i, j = pl.program_id(0), pl.program_id(1)
        _, vjp = jax.vjp(_finish_fn, o0_ref[0] + o1_ref[0], z_ref[...], g_ref[...])
        do, dz, dg = vjp(dy_ref[...])
        do_ref[...] = do
        dz_ref[...] = dz
        _acc(dg_ref, (slice(None), slice(None)), dg, (i == 0) & (j == 0))

    blk = pl.BlockSpec((TM, HEAD), lambda i, j: (i, j))
    return pl.pallas_call(
        body, name=name, grid=(dm.ntx, C_HEADS),
        in_specs=[pl.BlockSpec((1, TM, HEAD), lambda i, j: (0, i, j)), pl.BlockSpec((1, TM, HEAD), lambda i, j: (1, i, j)),
                  pl.BlockSpec((TM, HEAD), lambda i, j: (i, zc + j)), pl.BlockSpec((1, HEAD), lambda i, j: (0, 0)), blk],
        out_specs=[blk, blk, pl.BlockSpec((1, HEAD), lambda i, j: (0, 0))],
        out_shape=[_sds((dm.Tx, HD), F32), _sds((dm.Tx, HD), F32), _sds((1, HEAD), F32)],
        compiler_params=_cp(),
    )(O, O, P1, og, dY)


def adaln_fwd(c_ext, w_mod, b_loc, name):
    R, D = c_ext.shape
    nl = w_mod.shape[2]
    tn = _pick(nl, 384)

    def body(c_ref, w_ref, b_ref, o_ref):
        o_ref[0] = _dot(_silu(c_ref[...]), w_ref[0]) + b_ref[0]

    return pl.pallas_call(
        body, name=name, grid=(2, nl // tn),
        in_specs=[pl.BlockSpec((R, D), lambda l, j: (0, 0)), pl.BlockSpec((1, D, tn), lambda l, j: (l, 0, j)),
                  pl.BlockSpec((1, 1, tn), lambda l, j: (l, 0, j))],
        out_specs=pl.BlockSpec((1, R, tn), lambda l, j: (l, 0, j)),
        out_shape=_sds((2, R, nl), F32), compiler_params=_cp(),
    )(c_ext, w_mod, b_loc)


def adaln_bwd(c_ext, c_ctx, w_mod, dmx, dmh, nb, name):
    R, D = c_ext.shape
    nl = w_mod.shape[2]
    tn = _pick(nl, 384)
    nj = nl // tn

    def body(c_ref, cc_ref, w_ref, dmx_ref, dmh_ref, gw_ref, dc_ref):
        l, j = pl.program_id(0), pl.program_id(1)
        dh = dmh_ref[0, 0:1, :]
        for k in range(1, N_DEV):
            dh = dh + dmh_ref[0, k:k + 1, :]
        row = lax.broadcasted_iota(jnp.int32, (R, tn), 0)
        dmat = dmx_ref[0] + jnp.where(row == nb, dh, 0.0)
        gw_ref[0] = _dot_tn(_silu(c_ref[...]), dmat)
        part = _dot_nt(jnp.broadcast_to(dh, (8, tn)), w_ref[0])[0:1]
        _acc(dc_ref, (slice(None), slice(None)), part, (l == 0) & (j == 0))

        @pl.when((l == 1) & (j == nj - 1))
        def _():
            cc = cc_ref[...]
            sg = jax.nn.sigmoid(cc)
            dc_ref[...] = dc_ref[...] * (sg * (1.0 + cc * (1.0 - sg)))

    return pl.pallas_call(
        body, name=name, grid=(2, nj),
        in_specs=[pl.BlockSpec((R, D), lambda l, j: (0, 0)), pl.BlockSpec((1, D), lambda l, j: (0, 0)),
                  pl.BlockSpec((1, D, tn), lambda l, j: (l, 0, j)), pl.BlockSpec((1, R, tn), lambda l, j: (l, 0, j)),
                  pl.BlockSpec((1, N_DEV, tn), lambda l, j: (l, 0, j))],
        out_specs=[pl.BlockSpec((1, D, tn), lambda l, j: (l, 0, j)), pl.BlockSpec((1, D), lambda l, j: (0, 0))],
        out_shape=[_sds((2, D, nl), F32), _sds((1, D), F32)], compiler_params=_cp(),
    )(c_ext, c_ctx, w_mod, dmx, dmh)


def bmod_grad(dmx, dmh, name):
    _, R, n9 = dmx.shape

    def body(dmx_ref, dmh_ref, o_ref):
        o_ref[0] = jnp.sum(dmx_ref[0], axis=0, keepdims=True) + jnp.sum(dmh_ref[0], axis=0, keepdims=True)

    return pl.pallas_call(
        body, name=name, grid=(2,),
        in_specs=[pl.BlockSpec((1, R, n9), lambda l: (l, 0, 0)), pl.BlockSpec((1, N_DEV, n9), lambda l: (l, 0, 0))],
        out_specs=pl.BlockSpec((1, 1, n9), lambda l: (l, 0, 0)), out_shape=_sds((2, 1, n9), F32),
        compiler_params=_cp(),
    )(dmx, dmh)


def adamw(gs, w, m, v, name):
    S, R, C = gs.shape
    cap = max(8, (1 << 20) // (S * C))
    tr = R
    if R > cap:
        tr = max(t for t in range(8, cap + 1, 8) if R % t == 0)

    def body(g_ref, w_ref, m_ref, v_ref, go_ref, d_ref, mo_ref, vo_ref):
        g = g_ref[0].astype(F32)
        for k in range(1, S):
            g = g + g_ref[k].astype(F32)
        mn = ADAM_B1 * m_ref[...] + (1.0 - ADAM_B1) * g
        vn = ADAM_B2 * v_ref[...] + (1.0 - ADAM_B2) * jnp.square(g)
        m_hat = mn / (1.0 - ADAM_B1 ** ADAM_STEP)
        v_hat = vn / (1.0 - ADAM_B2 ** ADAM_STEP)
        go_ref[...] = g
        d_ref[...] = -ADAM_LR * (m_hat / (jnp.sqrt(v_hat) + ADAM_EPS) + ADAM_WD * w_ref[...])
        mo_ref[...] = mn
        vo_ref[...] = vn

    blk = pl.BlockSpec((tr, C), lambda i: (i, 0))
    return pl.pallas_call(
        body, name=name, grid=(R // tr,),
        in_specs=[pl.BlockSpec((S, tr, C), lambda i: (0, i, 0)), blk, blk, blk],
        out_specs=[blk] * 4, out_shape=[_sds((R, C), F32)] * 4, compiler_params=_cp(),
    )(gs, w, m, v)


def _gather_flat(parts, dtype, name):
    flat = jnp.concatenate([p.astype(dtype).reshape(-1) for p in parts])
    n = flat.shape[0]
    pad = (-n) % LANE
    if pad:
        flat = jnp.concatenate([flat, jnp.zeros((pad,), dtype)])
    got = all_gather([flat.reshape(-1, LANE)], name)[0].reshape(N_DEV, -1)
    out, off = [], 0
    for p in parts:
        out.append(got[:, off:off + p.size].reshape((N_DEV,) + p.shape))
        off += p.size
    return out


def _cols_full(g):
    return g.transpose(1, 0, 2).reshape(g.shape[1], -1)


def _cols_split(full):
    K = full.shape[0]
    return full.reshape(K, N_DEV, -1).transpose(1, 0, 2)


def kernel(x, c, ctx, c_ctx, w_mod, b_mod, norm_g, ffn_wg, ffn_wu, ffn_wd, ab_w_in, ab_q_norm, ab_k_norm, pool_w, pool_scale, ab_w_out, gdn_w_in, gdn_conv_w, gdn_a_log, gdn_dt_bias, gdn_o_norm, gdn_w_out, loss_target, m_c_ctx, m_w_mod, m_b_mod, m_norm_g, m_ffn_wg, m_ffn_wu, m_ffn_wd, m_ab_w_in, m_ab_q_norm, m_ab_k_norm, m_pool_w, m_pool_scale, m_ab_w_out, m_gdn_w_in, m_gdn_conv_w, m_gdn_a_log, m_gdn_dt_bias, m_gdn_o_norm, m_gdn_w_out, v_c_ctx, v_w_mod, v_b_mod, v_norm_g, v_ffn_wg, v_ffn_wu, v_ffn_wd, v_ab_w_in, v_ab_q_norm, v_ab_k_norm, v_pool_w, v_pool_scale, v_ab_w_out, v_gdn_w_in, v_gdn_conv_w, v_gdn_a_log, v_gdn_dt_bias, v_gdn_o_norm, v_gdn_w_out):
    Bl, N, D = x.shape
    M = ctx.shape[1]
    F = ffn_wd.shape[2] * N_DEV
    dm = Dims(Bl, N, M, D, F)
    TM, Tx, Th, T, G = dm.TM, dm.Tx, dm.Th, dm.T, dm.G
    HD = C_HEADS * HEAD
    me = 4 * lax.axis_index("x") + 2 * lax.axis_index("y") + lax.axis_index("c")
    nb = N_DEV * Bl
    R = -(-(nb + 1) // 8) * 8
    nl = w_mod.shape[2]
    n_gdn = gdn_w_in.shape[2] * N_DEV
    n_gdn_pad = -(-n_gdn // LANE) * LANE

    big = [w.astype(BF16) for w in (ffn_wg, ffn_wu, ffn_wd, ab_w_in, ab_w_out, gdn_w_in, gdn_w_out)]
    g_wg, g_wu, g_wd, g_abin, g_about, g_gin, g_gout = all_gather(big, "gather_weights")
    g_c, g_ng, g_cw = _gather_flat([c, norm_g, gdn_conv_w], F32, "gather_small")
    W_ABIN = _cols_full(g_abin[:, 0])
    W_ABOUT = g_about[:, 0].reshape(-1, D)
    W_GIN = jnp.pad(_cols_full(g_gin[:, 0]), ((0, 0), (0, n_gdn_pad - n_gdn)))
    W_GOUT = g_gout[:, 0].reshape(-1, D)
    gains = g_ng.transpose(1, 2, 0, 3).reshape(2, 3, 1, D)
    conv_w = g_cw[:, 0].transpose(1, 0, 2).reshape(3, -1)

    c_all = g_c.reshape(nb, D)
    c_ext = jnp.concatenate([c_all, c_ctx[None], jnp.zeros((R - nb - 1, D), F32)], 0)
    b_loc = lax.dynamic_slice_in_dim(b_mod, me * nl, nl, axis=1).reshape(2, 1, nl)
    mod_loc = adaln_fwd(c_ext, w_mod, b_loc, "adaln_fwd")
    (g_mod,) = _gather_flat([mod_loc], F32, "gather_mod")
    mod_full = g_mod.transpose(1, 2, 0, 3).reshape(2, R, 9 * D)
    MOD = []
    for l in range(2):
        mine = lax.dynamic_slice_in_dim(mod_full[l], me * Bl, Bl, axis=0)
        MOD.append(jnp.concatenate([mine, mod_full[l, nb:nb + 1]], 0).reshape(G, 9, D))

    tabs = _rope_tables(dm)
    qkg = jnp.stack([ab_q_norm, ab_k_norm])
    pw = pool_w[0].astype(BF16)
    pwT = pool_w[0].transpose(0, 2, 1).astype(BF16)
    par = jnp.stack([jnp.pad(jnp.pad(p[0], ((0, 0), (0, 8))).reshape(-1), (0, LANE - 32))
                     for p in (gdn_a_log, gdn_dt_bias)])

    X0 = jnp.concatenate([x.reshape(Tx, D), ctx.reshape(Th, D)], 0)
    nt, ntx = dm.nt, dm.ntx
    def ffn(X, l, s, s0, all_rows, tag):
        return ffn_fwd(X, MOD[l], gains[l, 2 * s], g_wg, g_wu, g_wd, l, s, s0, dm, all_rows, "ffn_fwd_" + tag)

    X1, Y1 = ffn(X0, 0, 0, 0, True, "00")
    P0, XN0 = modmm(X1, MOD[0], gains[0, 1], W_ABIN, 3, dm, "ab_in_proj")
    QKV = qkv_prep(P0, qkg, tabs, dm, "qkv_prep")
    ATT = jnp.concatenate([attn_fwd(QKV, dm, True, "attn_fwd_x"), attn_fwd(QKV, dm, False, "attn_fwd_h")], 0)
    POOL = jnp.concatenate([pool_fwd(P0, pw, pool_scale, dm, True, "pool_fwd_x"),
                            pool_fwd(P0, pw, pool_scale, dm, False, "pool_fwd_h")], 0)
    na = A_HEADS * HEAD
    X2, YM0 = proj_res([ATT, POOL], [W_ABOUT[:na], W_ABOUT[na:]], X1, MOD[0], dm, nt, "ab_out_proj")
    X3, Y3 = ffn(X2, 0, 1, 6, True, "01")
    X4, Y4 = ffn(X3, 1, 0, 0, True, "10")
    P1, XN1 = modmm(X4, MOD[1], gains[1, 1], W_GIN, 3, dm, "gdn_in_proj")
    QKVg = jnp.concatenate([gdn_prep(P1, conv_w, dm, True, "gdn_prep_x"), gdn_prep(P1, conv_w, dm, False, "gdn_prep_h")], 0)
    U, W, KT, QD, QK, GT = gdn_chunk_pre(QKVg, P1, par, dm, "gdn_chunk_pre")
    O, SS = gdn_scan_fwd(U, W, KT, QD, QK, GT, dm, "gdn_scan_fwd")
    FIN = gdn_finish(O, P1, gdn_o_norm, dm, "gdn_finish")
    X5, YM1 = proj_res([FIN], [W_GOUT], X4[:Tx], MOD[1], dm, ntx, "gdn_out_proj")
    X6, Y6 = ffn(X5, 1, 1, 6, False, "11")
    lvec, dX6 = loss_head(X6, loss_target.reshape(Tx, D), dm, "loss_head")
    loss = lax.psum(lvec[0, 0], AXES)

    zrow = lambda a: jnp.concatenate([a, jnp.zeros((G - a.shape[0],) + a.shape[1:], F32)], 0) if a.shape[0] < G else a

    def ffn_back(Xin, dXo, Y, l, s, s0, all_rows, tag):
        dXi, XNb, DOb, Hb, DGb, DUb, dmod, dgain = ffn_bwd(
            Xin, dXo, Y, MOD[l], gains[l, 2 * s], g_wg, g_wu, g_wd, l, s, s0, dm, all_rows, "ffn_bwd_" + tag)
        dwg, dwu, dwd = ffn_dw(XNb, DOb, Hb, DGb, DUb, dm, "ffn_dw_" + tag)
        return dXi, zrow(dmod), dgain, dwg, dwu, dwd

    dX5, dmod_12, dgain_12, dwg11, dwu11, dwd11 = ffn_back(X5, dX6, Y6, 1, 1, 6, False, "11")
    DY1, dFIN, dgate_1 = proj_res_bwd(dX5, YM1, MOD[1], [W_GOUT], dm, ntx, "gdn_out_proj_bwd")
    d_gout = atb(FIN, DY1, Tx, dm, "gdn_dwout")
    dOsum, dZ, d_onorm = gdn_finish_bwd(O, P1, gdn_o_norm, dFIN, dm, "gdn_finish_bwd")
    dO_all = jnp.concatenate([dOsum, jnp.zeros((Th, HD), F32)], 0)
    dU, dW, dKT, dQD, dQK, dGT = gdn_scan_bwd(dO_all, SS, U, W, KT, QD, QK, GT, dm, "gdn_scan_bwd")
    dQKVg, dAB, dPAR = gdn_chunk_pre_bwd(QKVg, P1, par, dU, dW, dKT, dQD, dQK, dGT, dm, "gdn_chunk_pre_bwd")
    dPx, dcw_x = gdn_prep_bwd(P1, dQKVg, conv_w, dm, True, "gdn_prep_bwd_x")
    dPh, dcw_h = gdn_prep_bwd(P1, dQKVg, conv_w, dm, False, "gdn_prep_bwd_h")
    d_conv = dcw_x + dcw_h
    dP1 = jnp.concatenate([jnp.concatenate([dPx, dPh], 0), jnp.concatenate([dZ, jnp.zeros((Th, HD), F32)], 0),
                           dAB[0] + dAB[1]], axis=1).astype(BF16)
    d_gin = atb(XN1, dP1, T, dm, "gdn_dwin")[:, :n_gdn]
    dX5_full = jnp.concatenate([dX5, jnp.zeros((Th, D), F32)], 0)
    dX4, dmod_11, dgain_11 = mixin_bwd(dP1, W_GIN, X4, dX5_full, MOD[1], gains[1, 1], 3, dm, "gdn_in_proj_bwd")
    dX3, dmod_10, dgain_10, dwg10, dwu10, dwd10 = ffn_back(X3, dX4, Y4, 1, 0, 0, True, "10")
    dMOD1 = jnp.concatenate([dmod_10, dmod_11, zrow(dgate_1), dmod_12], 1).reshape(G, 9 * D)

    dX2, dmod_02, dgain_02, dwg01, dwu01, dwd01 = ffn_back(X2, dX3, Y3, 0, 1, 6, True, "01")
    DY0, dATT, dPOOL, dgate_0 = proj_res_bwd(dX2, YM0, MOD[0], [W_ABOUT[:na], W_ABOUT[na:]], dm, nt, "ab_out_proj_bwd")
    d_about = jnp.concatenate([atb(ATT, DY0, T, dm, "ab_dwout_a"), atb(POOL, DY0, T, dm, "ab_dwout_p")], 0)
    dUx, dpw_x, dps_x = pool_bwd(P0, dPOOL[:Tx], pw, pwT, pool_scale, dm, True, "pool_bwd_x")
    dUh, dpw_h, dps_h = pool_bwd(P0, dPOOL[Tx:], pw, pwT, pool_scale, dm, False, "pool_bwd_h")
    dQh, dKh0, dVh0 = attn_bwd(QKV, dATT[Tx:], dm, False, None, "attn_bwd_h")
    dQx, dKh, dVh, dKx, dVx = attn_bwd(QKV, dATT[:Tx], dm, True, (dKh0, dVh0), "attn_bwd_x")
    dQKV = jnp.concatenate([jnp.concatenate([dQx, dQh], 0), jnp.concatenate([dKx, dKh], 0), jnp.concatenate([dVx, dVh], 0)], 1)
    dPqkv, d_qkg = qkv_prep_bwd(P0, dQKV, qkg, tabs, dm, "qkv_prep_bwd")
    dP0 = jnp.concatenate([dPqkv, jnp.concatenate([dUx, dUh], 0)], 1).astype(BF16)
    d_abin = atb(XN0, dP0, T, dm, "ab_dwin")
    dX1, dmod_01, dgain_01 = mixin_bwd(dP0, W_ABIN, X1, dX2, MOD[0], gains[0, 1], 3, dm, "ab_in_proj_bwd")
    dX0, dmod_00, dgain_00, dwg00, dwu00, dwd00 = ffn_back(X0, dX1, Y1, 0, 0, 0, True, "00")
    dMOD0 = jnp.concatenate([dmod_00, dmod_01, dgate_0, dmod_02], 1).reshape(G, 9 * D)
    grad_x = dX0[:Tx].reshape(Bl, N, D)

    d_ng = jnp.concatenate([dgain_00, dgain_01, dgain_02, dgain_10, dgain_11, dgain_12], 0)
    nf = ffn_wg.shape[3]
    parts = [jnp.concatenate([dwg00, dwg01, dwg10, dwg11], 1), jnp.concatenate([dwu00, dwu01, dwu10, dwu11], 1),
             jnp.concatenate([dwd00, dwd01, dwd10, dwd11], 1),
             _cols_split(d_abin), d_about.reshape(N_DEV, -1, D), _cols_split(d_gin), d_gout.reshape(N_DEV, -1, D),
             _cols_split(d_conv), _cols_split(d_ng)]
    gs_wg, gs_wu, gs_wd, gs_abin, gs_about, gs_gin, gs_gout, gs_conv, gs_ng = scatter_blocks(parts, "scatter_grads")

    d_alog = dPAR[0, :32].reshape(2, 16)[:, :8].reshape(1, 16)
    d_dtb = dPAR[1, :32].reshape(2, 16)[:, :8].reshape(1, 16)
    small = [d_qkg[0], d_qkg[1], (dpw_x + dpw_h).reshape(-1, HEAD), dps_x + dps_h, d_alog, d_dtb, d_onorm,
             jnp.stack([dMOD0, dMOD1])]
    gs_qn, gs_kn, gs_pw, gs_ps, gs_alog, gs_dtb, gs_on, g_dm = _gather_flat(small, F32, "gather_small_grads")
    dmx = g_dm[:, :, :Bl].transpose(1, 0, 2, 3).reshape(2, nb, 9 * D)
    dmx = jnp.concatenate([dmx, jnp.zeros((2, R - nb, 9 * D), F32)], 1)
    dmh = g_dm[:, :, Bl].transpose(1, 0, 2)
    cols_of_me = lambda a: lax.dynamic_slice_in_dim(a, me * nl, nl, axis=2)
    d_wmod, dcc = adaln_bwd(c_ext, c_ctx[None], w_mod, cols_of_me(dmx), cols_of_me(dmh), nb, "adaln_bwd")
    d_bmod = bmod_grad(dmx, dmh, "bmod_grad")
    (gs_cc,) = _gather_flat([dcc], F32, "gather_cctx_grad")

    def upd(gs, w, m, v, shape2, name):
        outs = adamw(gs.reshape((gs.shape[0],) + shape2), w.reshape(shape2), m.reshape(shape2), v.reshape(shape2), "adamw_" + name)
        return [o.reshape(w.shape) for o in outs]

    res = [
        upd(gs_cc, c_ctx, m_c_ctx, v_c_ctx, (1, D), "c_ctx"),
        upd(d_wmod[None], w_mod, m_w_mod, v_w_mod, (2 * D, nl), "w_mod"),
        upd(d_bmod[None], b_mod, m_b_mod, v_b_mod, (2, 9 * D), "b_mod"),
        upd(gs_ng, norm_g, m_norm_g, v_norm_g, (6, D // N_DEV), "norm_g"),
        upd(gs_wg, ffn_wg, m_ffn_wg, v_ffn_wg, (4 * D, nf), "ffn_wg"),
        upd(gs_wu, ffn_wu, m_ffn_wu, v_ffn_wu, (4 * D, nf), "ffn_wu"),
        upd(gs_wd, ffn_wd, m_ffn_wd, v_ffn_wd, (4 * nf, D), "ffn_wd"),
        upd(gs_abin, ab_w_in, m_ab_w_in, v_ab_w_in, (D, ab_w_in.shape[2]), "ab_w_in"),
        upd(gs_qn, ab_q_norm, m_ab_q_norm, v_ab_q_norm, (1, HEAD), "ab_q_norm"),
        upd(gs_kn, ab_k_norm, m_ab_k_norm, v_ab_k_norm, (1, HEAD), "ab_k_norm"),
        upd(gs_pw, pool_w, m_pool_w, v_pool_w, (len(POOL_WINDOWS) * HEAD, HEAD), "pool_w"),
        upd(gs_ps, pool_scale, m_pool_scale, v_pool_scale, (1, len(POOL_WINDOWS) * HEAD), "pool_scale"),
        upd(gs_about, ab_w_out, m_ab_w_out, v_ab_w_out, (ab_w_out.shape[1], D), "ab_w_out"),
        upd(gs_gin, gdn_w_in, m_gdn_w_in, v_gdn_w_in, (D, gdn_w_in.shape[2]), "gdn_w_in"),
        upd(gs_conv, gdn_conv_w, m_gdn_conv_w, v_gdn_conv_w, (3, gdn_conv_w.shape[2]), "gdn_conv_w"),
        upd(gs_alog, gdn_a_log, m_gdn_a_log, v_gdn_a_log, (1, 16), "gdn_a_log"),
        upd(gs_dtb, gdn_dt_bias, m_gdn_dt_bias, v_gdn_dt_bias, (1, 16), "gdn_dt_bias"),
        upd(gs_on, gdn_o_norm, m_gdn_o_norm, v_gdn_o_norm, (1, HEAD), "gdn_o_norm"),
        upd(gs_gout, gdn_w_out, m_gdn_w_out, v_gdn_w_out, (gdn_w_out.shape[1], D), "gdn_w_out"),
    ]
    return (loss, grad_x, *[r[0] for r in res], *[r[1] for r in res], *[r[2] for r in res], *[r[3] for r in res])
```

```python
import functools
import math
from typing import NamedTuple

import jax
import jax.numpy as jnp
from jax import lax
from jax.experimental import pallas as pl
from jax.experimental.pallas import tpu as pltpu

F32, BF16 = jnp.float32, jnp.bfloat16
EPS = 1e-6
HEAD = 128
CHUNK = 64
GRID_W = 64
ROPE_THETA = 10000.0
POOL_WINDOWS = (2, 4, 8, 16)
A_HEADS, A_KV = 4, 2
C_HEADS = 8
N_DEV = 8
AXES = ("x", "y", "c")
ADAM_LR, ADAM_B1, ADAM_B2, ADAM_EPS, ADAM_WD, ADAM_STEP = 0.001, 0.9, 0.999, 1e-08, 0.01, 10
LANE = 128
VMEM_LIMIT = 56 * 1024 * 1024
HI = lax.Precision.HIGHEST
NEG = -1e30


def _cp():
    return pltpu.CompilerParams(vmem_limit_bytes=VMEM_LIMIT)


def _sds(shape, dtype):
    return jax.ShapeDtypeStruct(tuple(shape), dtype)


def _dot(a, b):
    return jnp.dot(a.astype(BF16), b.astype(BF16), preferred_element_type=F32)


def _dot_nt(a, b):
    return lax.dot_general(a.astype(BF16), b.astype(BF16), (((1,), (1,)), ((), ())), preferred_element_type=F32)


def _dot_tn(a, b):
    return lax.dot_general(a.astype(BF16), b.astype(BF16), (((0,), (0,)), ((), ())), preferred_element_type=F32)


def _dot3(a, b, dims):
    ah, bh = a.astype(BF16), b.astype(BF16)
    al, bl = (a - ah.astype(F32)).astype(BF16), (b - bh.astype(F32)).astype(BF16)
    f = lambda x, y: lax.dot_general(x, y, (dims, ((), ())), preferred_element_type=F32)
    return f(ah, bh) + (f(ah, bl) + f(al, bh))


def _hdot(a, b):
    return _dot3(a, b, ((1,), (0,)))


def _hdot_nt(a, b):
    return _dot3(a, b, ((1,), (1,)))


def _hdot_tn(a, b):
    return _dot3(a, b, ((0,), (0,)))


def _pick(n, cap):
    if n <= cap:
        return n
    best = None
    for t in range(LANE, cap + 1, LANE):
        if n % t == 0:
            best = t
    assert best is not None, (n, cap)
    return best


class Dims(NamedTuple):
    Bl: int
    N: int
    M: int
    D: int
    F: int
    tm: int = 0

    @property
    def TM(self):
        return self.tm if self.tm else min(256, self.M)

    def with_tile(self, cap):
        return self._replace(tm=max(t for t in (1024, 512, 256, 128) if t <= cap and self.N % t == 0 and self.Th % t == 0))

    @property
    def Tx(self):
        return self.Bl * self.N

    @property
    def Th(self):
        return self.Bl * self.M

    @property
    def T(self):
        return self.Tx + self.Th

    @property
    def ntx(self):
        return self.Tx // self.TM

    @property
    def nt(self):
        return self.T // self.TM

    @property
    def tps(self):
        return self.N // self.TM

    @property
    def G(self):
        return self.Bl + 1


def _grp(i, dm, tm=None):
    tm = dm.TM if tm is None else tm
    return jnp.where(i < dm.Tx // tm, i // (dm.N // tm), dm.Bl)


def _first_of_group(i, dm, tm=None):
    tm = dm.TM if tm is None else tm
    return jnp.where(i < dm.Tx // tm, i % (dm.N // tm) == 0, i == dm.Tx // tm)


def _contraction_tile(rows):
    return max(t for t in (1024, 512, 256, 128) if rows % t == 0)


def _ffn_tile(dm):
    return max(t for t in (512, 256, 128) if dm.N % t == 0 and dm.Th % t == 0)


def _acc(ref, idx, val, first):
    @pl.when(first)
    def _():
        ref[idx] = val

    @pl.when(jnp.logical_not(first))
    def _():
        ref[idx] += val


def _modulate(x, gain, shift, scale):
    y = x * lax.rsqrt(jnp.mean(x * x, axis=-1, keepdims=True) + EPS)
    return (y * gain) * (1.0 + scale) + shift


def _silu(x):
    return x * jax.nn.sigmoid(x)


@functools.partial(jax.custom_vjp, nondiff_argnums=(1,))
def _shift_rows(a, k):
    n = a.shape[0]
    if k == 0:
        return a
    r = lax.broadcasted_iota(jnp.int32, a.shape, 0)
    rolled = pltpu.roll(a, (-k) % n, 0)
    ok = (r + k >= 0) & (r + k < n)
    return jnp.where(ok, rolled, 0.0)


def _shift_rows_fwd(a, k):
    return _shift_rows(a, k), None


def _shift_rows_bwd(k, _, d):
    return (_shift_rows(d, -k),)


_shift_rows.defvjp(_shift_rows_fwd, _shift_rows_bwd)


@functools.partial(jax.custom_vjp, nondiff_argnums=(1,))
def _roll_lanes(a, s):
    return pltpu.roll(a, s % LANE, 1)


def _roll_lanes_fwd(a, s):
    return _roll_lanes(a, s), None


def _roll_lanes_bwd(s, _, d):
    return (_roll_lanes(d, -s),)


_roll_lanes.defvjp(_roll_lanes_fwd, _roll_lanes_bwd)


def _rope(t, cs, sneg, spos):
    return t * cs + _roll_lanes(t, 96) * sneg + _roll_lanes(t, 32) * spos


def _rope_tables(dm):
    rows = dm.N // GRID_W
    row = jnp.repeat(jnp.arange(rows), GRID_W).astype(F32)
    col = jnp.tile(jnp.arange(GRID_W), rows).astype(F32)
    half = HEAD // 2
    inv_freq = jnp.power(ROPE_THETA, -jnp.arange(0, half, 2, dtype=F32) / half)
    ar, ac = row[:, None] * inv_freq, col[:, None] * inv_freq
    cs = jnp.concatenate([jnp.cos(ar), jnp.cos(ar), jnp.cos(ac), jnp.cos(ac)], axis=1)
    z = jnp.zeros_like(ar)
    sneg = jnp.concatenate([-jnp.sin(ar), z, -jnp.sin(ac), z], axis=1)
    spos = jnp.concatenate([z, jnp.sin(ar), z, jnp.sin(ac)], axis=1)
    pad1 = jnp.ones((dm.TM, HEAD), F32)
    pad0 = jnp.zeros((dm.TM, HEAD), F32)
    return (jnp.concatenate([cs, pad1], 0), jnp.concatenate([sneg, pad0], 0), jnp.concatenate([spos, pad0], 0))


def all_gather(xs, name):
    n = len(xs)

    def body(*refs):
        x_refs, out_refs = refs[:n], refs[n:2 * n]
        send_sems, recv_sems, local_sems = refs[2 * n:]
        x, y, c = lax.axis_index("x"), lax.axis_index("y"), lax.axis_index("c")
        me, sibling = (x, y, c), (x, y, 1 - c)
        chips = [(1 - x, y), (x, 1 - y), (1 - x, 1 - y)]

        def slot(a, px, py, pc):
            return out_refs[a].at[4 * px + 2 * py + pc]

        def copy(a, k, block, to, src=None):
            return pltpu.make_async_remote_copy(
                src_ref=slot(a, *block) if src is None else src, dst_ref=slot(a, *block),
                send_sem=send_sems.at[7 * a + k], recv_sem=recv_sems.at[7 * a + k],
                device_id=to, device_id_type=pl.DeviceIdType.MESH)

        mine = [pltpu.make_async_copy(x_refs[a], slot(a, *me), local_sems.at[a]) for a in range(n)]
        for cp in mine:
            cp.start()
        first = []
        for a in range(n):
            first.append(copy(a, 0, me, sibling, src=x_refs[a]))
            first += [copy(a, 1 + j, me, (*chip, c), src=x_refs[a]) for j, chip in enumerate(chips)]
        for cp in first:
            cp.start()
        passed = []
        for j, chip in enumerate(chips):
            for a in range(n):
                copy(a, 1 + j, (*chip, c), me).wait_recv()
                cp = copy(a, 4 + j, (*chip, c), sibling)
                cp.start()
                passed.append(cp)
        for a in range(n):
            copy(a, 0, sibling, me).wait_recv()
            for j, chip in enumerate(chips):
                copy(a, 4 + j, (*chip, 1 - c), me).wait_recv()
        for cp in first + passed:
            cp.wait_send()
        for cp in mine:
            cp.wait()

    anyspec = pl.BlockSpec(memory_space=pl.ANY)
    return pl.pallas_call(
        body, name=name, out_shape=[_sds((N_DEV,) + a.shape, a.dtype) for a in xs],
        in_specs=[anyspec] * n, out_specs=[anyspec] * n,
        scratch_shapes=[pltpu.SemaphoreType.DMA((7 * n,)), pltpu.SemaphoreType.DMA((7 * n,)),
                        pltpu.SemaphoreType.DMA((n,))],
    )(*xs)


def scatter_blocks(xs, name):
    n = len(xs)
    flips = [(0, 0, 1), (0, 1, 0), (0, 1, 1), (1, 0, 0), (1, 0, 1), (1, 1, 0), (1, 1, 1)]

    def body(*refs):
        x_refs, out_refs = refs[:n], refs[n:2 * n]
        send_sems, recv_sems, local_sems = refs[2 * n:]
        x, y, c = lax.axis_index("x"), lax.axis_index("y"), lax.axis_index("c")
        me = 4 * x + 2 * y + c

        def peer(f):
            return tuple(1 - v if d else v for v, d in zip((x, y, c), f))

        def lin(p):
            return 4 * p[0] + 2 * p[1] + p[2]

        mine = [pltpu.make_async_copy(x_refs[a].at[me], out_refs[a].at[me], local_sems.at[a]) for a in range(n)]
        for cp in mine:
            cp.start()
        copies = []
        for k, f in enumerate(flips):
            p = peer(f)
            for a in range(n):
                copies.append(pltpu.make_async_remote_copy(
                    src_ref=x_refs[a].at[lin(p)], dst_ref=out_refs[a].at[me],
                    send_sem=send_sems.at[7 * a + k], recv_sem=recv_sems.at[7 * a + k],
                    device_id=p, device_id_type=pl.DeviceIdType.MESH))
        for cp in copies:
            cp.start()
        for cp in copies:
            cp.wait_send()
            cp.wait_recv()
        for cp in mine:
            cp.wait()

    anyspec = pl.BlockSpec(memory_space=pl.ANY)
    return pl.pallas_call(
        body, name=name, out_shape=[_sds(a.shape, a.dtype) for a in xs],
        in_specs=[anyspec] * n, out_specs=[anyspec] * n,
        scratch_shapes=[pltpu.SemaphoreType.DMA((7 * n,)), pltpu.SemaphoreType.DMA((7 * n,)),
                        pltpu.SemaphoreType.DMA((n,))],
    )(*xs)


def _mod_spec(dm, nidx, tm=None):
    if nidx == 1:
        return pl.BlockSpec((1, 9, dm.D), lambda i: (_grp(i, dm, tm), 0, 0))
    return pl.BlockSpec((1, 9, dm.D), lambda i, k: (_grp(i, dm, tm), 0, 0))


def _wspec(shape5, l, s, ks):
    return pl.BlockSpec((ks, 1, 1) + tuple(shape5[3:]), lambda i, k: (k, l, s, 0, 0))


FFN_FWD_SHARDS = 4
FFN_BWD_SHARDS = 2


def ffn_fwd(X, MOD, gain, gwg, gwu, gwd, l, s, s0, dm, all_rows, name):
    D = dm.D
    tm = _ffn_tile(dm)
    rows = dm.T if all_rows else dm.Tx
    ks = FFN_FWD_SHARDS
    nk = N_DEV // ks

    def body(x_ref, m_ref, g_ref, wg_ref, wu_ref, wd_ref, xo_ref, y_ref, xn_s, acc_s):
        k = pl.program_id(1)

        @pl.when(k == 0)
        def _():
            m = m_ref[0]
            xn = _modulate(x_ref[...], g_ref[...], m[s0:s0 + 1], m[s0 + 1:s0 + 2])
            xn_s[...] = xn.astype(BF16)
            acc_s[...] = jnp.zeros_like(acc_s)

        xn = xn_s[...]
        y = None
        for j in range(ks):
            g = jnp.dot(xn, wg_ref[j, 0, 0], preferred_element_type=F32)
            u = jnp.dot(xn, wu_ref[j, 0, 0], preferred_element_type=F32)
            yj = jnp.dot((_silu(g) * u).astype(BF16), wd_ref[j, 0, 0], preferred_element_type=F32)
            y = yj if y is None else y + yj
        acc_s[...] += y

        @pl.when(k == nk - 1)
        def _():
            m = m_ref[0]
            y = acc_s[...]
            y_ref[...] = y
            xo_ref[...] = x_ref[...] + (0.5 * m[s0 + 2:s0 + 3]) * y

    row = pl.BlockSpec((tm, D), lambda i, k: (i, 0))
    return pl.pallas_call(
        body, name=name, grid=(rows // tm, nk),
        in_specs=[row, _mod_spec(dm, 2, tm), pl.BlockSpec((1, D), lambda i, k: (0, 0)),
                  _wspec(gwg.shape, l, s, ks), _wspec(gwu.shape, l, s, ks), _wspec(gwd.shape, l, s, ks)],
        out_specs=[row, row],
        out_shape=[_sds((rows, D), F32), _sds((rows, D), F32)],
        scratch_shapes=[pltpu.VMEM((tm, D), BF16), pltpu.VMEM((tm, D), F32)],
        compiler_params=_cp(),
    )(X, MOD, gain, gwg, gwu, gwd)


def ffn_bwd(X, dXo, Y, MOD, gain, gwg, gwu, gwd, l, s, s0, dm, all_rows, name):
    D = dm.D
    tm = _ffn_tile(dm)
    rows = dm.T if all_rows else dm.Tx
    ks = FFN_BWD_SHARDS
    nk = N_DEV // ks
    nf = gwg.shape[4]
    ngr = dm.G if all_rows else dm.Bl

    def body(x_ref, dxo_ref, y_ref, m_ref, g_ref, wg_ref, wu_ref, wd_ref,
             dxi_ref, xn_ref, do_ref, h_ref, dg_ref, du_ref, dm_ref, dgain_ref, xn_s, do_s, dxn_s):
        i, k = pl.program_id(0), pl.program_id(1)

        @pl.when(k == 0)
        def _():
            m = m_ref[0]
            xn = _modulate(x_ref[...], g_ref[...], m[s0:s0 + 1], m[s0 + 1:s0 + 2])
            xn_s[...] = xn.astype(BF16)
            do_s[...] = ((0.5 * m[s0 + 2:s0 + 3]) * dxo_ref[...]).astype(BF16)
            dxn_s[...] = jnp.zeros_like(dxn_s)

        xn, do = xn_s[...], do_s[...]
        dxn = None
        for j in range(ks):
            wg, wu = wg_ref[j, 0, 0], wu_ref[j, 0, 0]
            g = jnp.dot(xn, wg, preferred_element_type=F32)
            u = jnp.dot(xn, wu, preferred_element_type=F32)
            sg = jax.nn.sigmoid(g)
            si = g * sg
            dh = _dot_nt(do, wd_ref[j, 0, 0])
            dg = (dh * u * (sg * (1.0 + g * (1.0 - sg)))).astype(BF16)
            du = (dh * si).astype(BF16)
            dj = _dot_nt(dg, wg) + _dot_nt(du, wu)
            dxn = dj if dxn is None else dxn + dj
            h_ref[j] = (si * u).astype(BF16)
            dg_ref[j] = dg
            du_ref[j] = du
        dxn_s[...] += dxn

        @pl.when(k == nk - 1)
        def _():
            m = m_ref[0]
            _, vjp = jax.vjp(_modulate, x_ref[...], g_ref[...], m[s0:s0 + 1], m[s0 + 1:s0 + 2])
            dx, dgain, dshift, dscale = vjp(dxn_s[...])
            dxo = dxo_ref[...]
            dxi_ref[...] = dxo + dx
            xn_ref[...] = xn_s[...]
            do_ref[...] = do_s[...]
            dgate = jnp.sum(0.5 * dxo * y_ref[...], axis=0, keepdims=True)
            first = _first_of_group(i, dm, tm)
            _acc(dm_ref, (0, pl.ds(0, 1), slice(None)), dshift, first)
            _acc(dm_ref, (0, pl.ds(1, 1), slice(None)), dscale, first)
            _acc(dm_ref, (0, pl.ds(2, 1), slice(None)), dgate, first)
            _acc(dgain_ref, (slice(None), slice(None)), dgain, i == 0)

    row = pl.BlockSpec((tm, D), lambda i, k: (i, 0))
    slab = pl.BlockSpec((ks, tm, nf), lambda i, k: (k, i, 0))
    return pl.pallas_call(
        body, name=name, grid=(rows // tm, nk),
        in_specs=[row, row, row, _mod_spec(dm, 2, tm), pl.BlockSpec((1, D), lambda i, k: (0, 0)),
                  _wspec(gwg.shape, l, s, ks), _wspec(gwu.shape, l, s, ks), _wspec(gwd.shape, l, s, ks)],
        out_specs=[row, row, row, slab, slab, slab,
                   pl.BlockSpec((1, 3, D), lambda i, k: (_grp(i, dm, tm), 0, 0)),
                   pl.BlockSpec((1, D), lambda i, k: (0, 0))],
        out_shape=[_sds((rows, D), F32), _sds((rows, D), BF16), _sds((rows, D), BF16),
                   _sds((N_DEV, rows, nf), BF16), _sds((N_DEV, rows, nf), BF16), _sds((N_DEV, rows, nf), BF16),
                   _sds((ngr, 3, D), F32), _sds((1, D), F32)],
        scratch_shapes=[pltpu.VMEM((tm, D), BF16), pltpu.VMEM((tm, D), BF16), pltpu.VMEM((tm, D), F32)],
        compiler_params=_cp(),
    )(X, dXo, Y, MOD, gain, gwg, gwu, gwd)


def ffn_dw(XN, DO, H, DG, DU, dm, name):
    rows, D = XN.shape
    nf = H.shape[2]
    tt = _contraction_tile(rows)
    nT = rows // tt

    def body(xn_ref, do_ref, h_ref, dg_ref, du_ref, dwg_ref, dwu_ref, dwd_ref, ag_s, au_s, ad_s):
        t = pl.program_id(1)

        @pl.when(t == 0)
        def _():
            ag_s[...] = jnp.zeros_like(ag_s)
            au_s[...] = jnp.zeros_like(au_s)
            ad_s[...] = jnp.zeros_like(ad_s)

        xn = xn_ref[...]
        ag_s[...] += _dot_tn(xn, dg_ref[0])
        au_s[...] += _dot_tn(xn, du_ref[0])
        ad_s[...] += _dot_tn(h_ref[0], do_ref[...])

        @pl.when(t == nT - 1)
        def _():
            dwg_ref[0] = ag_s[...].astype(BF16)
            dwu_ref[0] = au_s[...].astype(BF16)
            dwd_ref[0] = ad_s[...].astype(BF16)

    row = pl.BlockSpec((tt, D), lambda k, t: (t, 0))
    slab = pl.BlockSpec((1, tt, nf), lambda k, t: (k, t, 0))
    return pl.pallas_call(
        body, name=name, grid=(N_DEV, nT),
        in_specs=[row, row, slab, slab, slab],
        out_specs=[pl.BlockSpec((1, D, nf), lambda k, t: (k, 0, 0)), pl.BlockSpec((1, D, nf), lambda k, t: (k, 0, 0)),
                   pl.BlockSpec((1, nf, D), lambda k, t: (k, 0, 0))],
        out_shape=[_sds((N_DEV, D, nf), BF16), _sds((N_DEV, D, nf), BF16), _sds((N_DEV, nf, D), BF16)],
        scratch_shapes=[pltpu.VMEM((D, nf), F32), pltpu.VMEM((D, nf), F32), pltpu.VMEM((nf, D), F32)],
        compiler_params=_cp(),
    )(XN, DO, H, DG, DU)


def atb(A, B, rows, dm, name):
    Ka, Nb = A.shape[1], B.shape[1]
    tk, tn = _pick(Ka, 1024), _pick(Nb, 1536)
    tt = _contraction_tile(rows)
    nT = rows // tt

    def body(a_ref, b_ref, o_ref, acc_s):
        t = pl.program_id(2)

        @pl.when(t == 0)
        def _():
            acc_s[...] = jnp.zeros_like(acc_s)

        acc_s[...] += _dot_tn(a_ref[...], b_ref[...])

        @pl.when(t == nT - 1)
        def _():
            o_ref[...] = acc_s[...].astype(BF16)

    return pl.pallas_call(
        body, name=name, grid=(Ka // tk, Nb // tn, nT),
        in_specs=[pl.BlockSpec((tt, tk), lambda i, j, t: (t, i)), pl.BlockSpec((tt, tn), lambda i, j, t: (t, j))],
        out_specs=pl.BlockSpec((tk, tn), lambda i, j, t: (i, j)),
        out_shape=_sds((Ka, Nb), BF16), scratch_shapes=[pltpu.VMEM((tk, tn), F32)], compiler_params=_cp(),
    )(A, B)


def modmm(X, MOD, gain, W, s0, dm, name):
    TM, D = dm.TM, dm.D
    Nc = W.shape[1]
    tn = _pick(Nc, 1536)
    nj = Nc // tn

    def body(x_ref, m_ref, g_ref, w_ref, p_ref, xn_ref):
        @pl.when(pl.program_id(1) == 0)
        def _():
            m = m_ref[0]
            xn_ref[...] = _modulate(x_ref[...], g_ref[...], m[s0:s0 + 1], m[s0 + 1:s0 + 2]).astype(BF16)

        p_ref[...] = jnp.dot(xn_ref[...], w_ref[...], preferred_element_type=F32)

    row = pl.BlockSpec((TM, D), lambda i, j: (i, 0))
    return pl.pallas_call(
        body, name=name, grid=(dm.nt, nj),
        in_specs=[row, _mod_spec(dm, 2), pl.BlockSpec((1, D), lambda i, j: (0, 0)),
                  pl.BlockSpec((D, tn), lambda i, j: (0, j))],
        out_specs=[pl.BlockSpec((TM, tn), lambda i, j: (i, j)), row],
        out_shape=[_sds((dm.T, Nc), F32), _sds((dm.T, D), BF16)],
        compiler_params=_cp(),
    )(X, MOD, gain, W)


def mixin_bwd(dP, W, X, dXres, MOD, gain, s0, dm, name):
    TM, D = dm.TM, dm.D
    K = dP.shape[1]

    def body(dp_ref, w_ref, x_ref, dr_ref, m_ref, g_ref, dx_ref, dm_ref, dgain_ref):
        i = pl.program_id(0)
        dxn = _dot_nt(dp_ref[...], w_ref[...])
        m = m_ref[0]
        _, vjp = jax.vjp(_modulate, x_ref[...], g_ref[...], m[s0:s0 + 1], m[s0 + 1:s0 + 2])
        dx, dgain, dshift, dscale = vjp(dxn)
        dx_ref[...] = dr_ref[...] + dx
        first = _first_of_group(i, dm)
        _acc(dm_ref, (0, pl.ds(0, 1), slice(None)), dshift, first)
        _acc(dm_ref, (0, pl.ds(1, 1), slice(None)), dscale, first)
        _acc(dgain_ref, (slice(None), slice(None)), dgain, i == 0)

    row = pl.BlockSpec((TM, D), lambda i: (i, 0))
    return pl.pallas_call(
        body, name=name, grid=(dm.nt,),
        in_specs=[pl.BlockSpec((TM, K), lambda i: (i, 0)), pl.BlockSpec((D, K), lambda i: (0, 0)), row, row,
                  _mod_spec(dm, 1), pl.BlockSpec((1, D), lambda i: (0, 0))],
        out_specs=[row, pl.BlockSpec((1, 2, D), lambda i: (_grp(i, dm), 0, 0)), pl.BlockSpec((1, D), lambda i: (0, 0))],
        out_shape=[_sds((dm.T, D), F32), _sds((dm.G, 2, D), F32), _sds((1, D), F32)],
        compiler_params=_cp(),
    )(dP, W, X, dXres, MOD, gain)


def proj_res(As, Ws, X, MOD, dm, ntiles, name):
    TM, D = dm.TM, dm.D
    n = len(As)
    rows = ntiles * TM

    def body(*refs):
        a_refs, w_refs = refs[:n], refs[n:2 * n]
        x_ref, m_ref, xo_ref, y_ref = refs[2 * n:]
        y = jnp.dot(a_refs[0][...], w_refs[0][...], preferred_element_type=F32)
        for a, w in zip(a_refs[1:], w_refs[1:]):
            y += jnp.dot(a[...], w[...], preferred_element_type=F32)
        y_ref[...] = y
        xo_ref[...] = x_ref[...] + m_ref[0][5:6] * y

    row = pl.BlockSpec((TM, D), lambda i: (i, 0))
    return pl.pallas_call(
        body, name=name, grid=(ntiles,),
        in_specs=[pl.BlockSpec((TM, a.shape[1]), lambda i: (i, 0)) for a in As]
        + [pl.BlockSpec(w.shape, lambda i: (0, 0)) for w in Ws] + [row, _mod_spec(dm, 1)],
        out_specs=[row, row], out_shape=[_sds((rows, D), F32), _sds((rows, D), F32)],
        compiler_params=_cp(),
    )(*As, *Ws, X, MOD)


def proj_res_bwd(dXo, Y, MOD, Ws, dm, ntiles, name):
    TM, D = dm.TM, dm.D
    n = len(Ws)
    rows = ntiles * TM
    ngr = dm.G if ntiles == dm.nt else dm.Bl

    def body(*refs):
        dxo_ref, y_ref, m_ref = refs[:3]
        w_refs = refs[3:3 + n]
        dy_ref = refs[3 + n]
        da_refs = refs[4 + n:4 + 2 * n]
        dgate_ref = refs[4 + 2 * n]
        i = pl.program_id(0)
        dxo = dxo_ref[...]
        dy = (m_ref[0][5:6] * dxo).astype(BF16)
        dy_ref[...] = dy
        for w, da in zip(w_refs, da_refs):
            da[...] = _dot_nt(dy, w[...])
        dgate = jnp.sum(dxo * y_ref[...], axis=0, keepdims=True)
        _acc(dgate_ref, (0, slice(None), slice(None)), dgate, _first_of_group(i, dm))

    row = pl.BlockSpec((TM, D), lambda i: (i, 0))
    return pl.pallas_call(
        body, name=name, grid=(ntiles,),
        in_specs=[row, row, _mod_spec(dm, 1)] + [pl.BlockSpec(w.shape, lambda i: (0, 0)) for w in Ws],
        out_specs=[row] + [pl.BlockSpec((TM, w.shape[0]), lambda i: (i, 0)) for w in Ws]
        + [pl.BlockSpec((1, 1, D), lambda i: (_grp(i, dm), 0, 0))],
        out_shape=[_sds((rows, D), BF16)] + [_sds((rows, w.shape[0]), F32) for w in Ws] + [_sds((ngr, 1, D), F32)],
        compiler_params=_cp(),
    )(dXo, Y, MOD, *Ws)


def loss_head(Xf, target, dm, name):
    TM, D = dm.TM, dm.D

    def body(x_ref, t_ref, l_ref, dx_ref, acc_s):
        i = pl.program_id(0)
        e = x_ref[...] - t_ref[...]
        dx_ref[...] = e * (1.0 / D)

        @pl.when(i == 0)
        def _():
            acc_s[...] = jnp.zeros_like(acc_s)

        acc_s[...] += jnp.sum(e * e, axis=0, keepdims=True)

        @pl.when(i == dm.ntx - 1)
        def _():
            tot = jnp.sum(acc_s[...], axis=1, keepdims=True) * (0.5 / D)
            l_ref[...] = jnp.broadcast_to(tot, (1, LANE))

    row = pl.BlockSpec((TM, D), lambda i: (i, 0))
    return pl.pallas_call(
        body, name=name, grid=(dm.ntx,), in_specs=[row, row],
        out_specs=[pl.BlockSpec((1, LANE), lambda i: (0, 0)), row],
        out_shape=[_sds((1, LANE), F32), _sds((dm.Tx, D), F32)],
        scratch_shapes=[pltpu.VMEM((1, D), F32)], compiler_params=_cp(),
    )(Xf, target)


def _qk_fn(p, gain, cs, sneg, spos):
    y = p * lax.rsqrt(jnp.mean(p * p, axis=-1, keepdims=True) + EPS) * gain
    return _rope(y, cs, sneg, spos)


def _tab_specs(dm, swap):
    def idx(i):
        return jnp.where(i < dm.ntx, i % dm.tps, dm.tps)
    if swap:
        return [pl.BlockSpec((dm.TM, HEAD), lambda j, i: (idx(i), 0))] * 3
    return [pl.BlockSpec((dm.TM, HEAD), lambda i, j: (idx(i), 0))] * 3


def qkv_prep(P0, qkg, tabs, dm, name):
    TM = dm.TM

    def body(p_ref, g_ref, cs_ref, sn_ref, sp_ref, o_ref):
        j = pl.program_id(1)

        @pl.when(j < 6)
        def _():
            o_ref[...] = _qk_fn(p_ref[...], g_ref[0], cs_ref[...], sn_ref[...], sp_ref[...]).astype(BF16)

        @pl.when(j >= 6)
        def _():
            o_ref[...] = p_ref[...].astype(BF16)

    blk = pl.BlockSpec((TM, HEAD), lambda i, j: (i, j))
    return pl.pallas_call(
        body, name=name, grid=(dm.nt, 8),
        in_specs=[blk, pl.BlockSpec((1, 1, HEAD), lambda i, j: (jnp.minimum(j // 4, 1), 0, 0))] + _tab_specs(dm, False),
        out_specs=blk, out_shape=_sds((dm.T, 8 * HEAD), BF16), compiler_params=_cp(),
    )(P0, qkg, *tabs)


def qkv_prep_bwd(P0, dQKV, qkg, tabs, dm, name):
    TM = dm.TM

    def body(p_ref, d_ref, g_ref, cs_ref, sn_ref, sp_ref, dp_ref, dg_ref):
        j, i = pl.program_id(0), pl.program_id(1)
        first = (i == 0) & ((j == 0) | (j == 4))

        @pl.when(j < 6)
        def _():
            _, vjp = jax.vjp(_qk_fn, p_ref[...], g_ref[0], cs_ref[...], sn_ref[...], sp_ref[...])
            dp, dg = vjp(d_ref[...])[:2]
            dp_ref[...] = dp
            _acc(dg_ref, (0, slice(None), slice(None)), dg, first)

        @pl.when(j >= 6)
        def _():
            dp_ref[...] = d_ref[...]

    blk = pl.BlockSpec((TM, HEAD), lambda j, i: (i, j))
    return pl.pallas_call(
        body, name=name, grid=(8, dm.nt),
        in_specs=[blk, blk, pl.BlockSpec((1, 1, HEAD), lambda j, i: (jnp.minimum(j // 4, 1), 0, 0))] + _tab_specs(dm, True),
        out_specs=[blk, pl.BlockSpec((1, 1, HEAD), lambda j, i: (jnp.minimum(j // 4, 1), 0, 0))],
        out_shape=[_sds((dm.T, 8 * HEAD), F32), _sds((2, 1, HEAD), F32)], compiler_params=_cp(),
    )(P0, dQKV, qkg, *tabs)


def _softmax2(sx, sh):
    m = jnp.max(sh, axis=-1, keepdims=True)
    if sx is not None:
        m = jnp.maximum(m, jnp.max(sx, axis=-1, keepdims=True))
    eh = jnp.exp(sh - m)
    l = jnp.sum(eh, axis=-1, keepdims=True)
    ex = None
    if sx is not None:
        ex = jnp.exp(sx - m)
        l = l + jnp.sum(ex, axis=-1, keepdims=True)
    inv = 1.0 / l
    return (None if ex is None else ex * inv), eh * inv


def _attn_geometry(dm, with_x):
    TQ = dm.TM
    if with_x:
        nq, qoff = dm.N // TQ, 0
    else:
        nq, qoff = dm.M // TQ, dm.Tx // TQ
    hoff = dm.Tx // dm.M
    return TQ, nq, qoff, hoff


def attn_fwd(QKV, dm, with_x, name):
    TQ, nq, qoff, hoff = _attn_geometry(dm, with_x)
    scale = HEAD ** -0.5
    rows = dm.Tx if with_x else dm.Th

    def body(*refs):
        if with_x:
            q_ref, kh_ref, vh_ref, kx_ref, vx_ref, o_ref = refs
        else:
            q_ref, kh_ref, vh_ref, o_ref = refs
        q = q_ref[...]
        sh = _dot_nt(q, kh_ref[...]) * scale
        sx = _dot_nt(q, kx_ref[...]) * scale if with_x else None
        px, ph = _softmax2(sx, sh)
        o = _dot(ph, vh_ref[...])
        if with_x:
            o = o + _dot(px, vx_ref[...])
        o_ref[...] = o.astype(BF16)

    qs = pl.BlockSpec((TQ, HEAD), lambda b, kv, g, qi: (qoff + b * nq + qi, kv * 2 + g))
    in_specs = [qs, pl.BlockSpec((dm.M, HEAD), lambda b, kv, g, qi: (hoff + b, 4 + kv)),
                pl.BlockSpec((dm.M, HEAD), lambda b, kv, g, qi: (hoff + b, 6 + kv))]
    args = [QKV, QKV, QKV]
    if with_x:
        in_specs += [pl.BlockSpec((dm.N, HEAD), lambda b, kv, g, qi: (b, 4 + kv)),
                     pl.BlockSpec((dm.N, HEAD), lambda b, kv, g, qi: (b, 6 + kv))]
        args += [QKV, QKV]
    return pl.pallas_call(
        body, name=name, grid=(dm.Bl, A_KV, 2, nq), in_specs=in_specs,
        out_specs=pl.BlockSpec((TQ, HEAD), lambda b, kv, g, qi: (b * nq + qi, kv * 2 + g)),
        out_shape=_sds((rows, A_HEADS * HEAD), BF16), compiler_params=_cp(),
    )(*args)


def attn_bwd(QKV, dO, dm, with_x, init, name):
    TQ, nq, qoff, hoff = _attn_geometry(dm, with_x)
    scale = HEAD ** -0.5
    rows = dm.Tx if with_x else dm.Th

    def body(*refs):
        if with_x:
            (q_ref, kh_ref, vh_ref, kx_ref, vx_ref, do_ref, ikh_ref, ivh_ref,
             dq_ref, dkh_ref, dvh_ref, dkx_ref, dvx_ref) = refs
        else:
            q_ref, kh_ref, vh_ref, do_ref, dq_ref, dkh_ref, dvh_ref = refs
        g, qi = pl.program_id(2), pl.program_id(3)
        q = q_ref[...]
        kh, vh = kh_ref[...], vh_ref[...]
        sh = _dot_nt(q, kh) * scale
        sx = _dot_nt(q, kx_ref[...]) * scale if with_x else None
        px, ph = _softmax2(sx, sh)
        dob = do_ref[...].astype(BF16)
        dph = _dot_nt(dob, vh)
        delta = jnp.sum(dph * ph, axis=-1, keepdims=True)
        if with_x:
            dpx = _dot_nt(dob, vx_ref[...])
            delta = delta + jnp.sum(dpx * px, axis=-1, keepdims=True)
        dsh = (ph * (dph - delta) * scale).astype(BF16)
        dq = _dot(dsh, kh)
        first = (g == 0) & (qi == 0)

        @pl.when(first)
        def _():
            if with_x:
                dkh_ref[...] = ikh_ref[...]
                dvh_ref[...] = ivh_ref[...]
                dkx_ref[...] = jnp.zeros_like(dkx_ref)
                dvx_ref[...] = jnp.zeros_like(dvx_ref)
            else:
                dkh_ref[...] = jnp.zeros_like(dkh_ref)
                dvh_ref[...] = jnp.zeros_like(dvh_ref)

        dkh_ref[...] += _dot_tn(dsh, q)
        dvh_ref[...] += _dot_tn(ph, dob)
        if with_x:
            dsx = (px * (dpx - delta) * scale).astype(BF16)
            dq = dq + _dot(dsx, kx_ref[...])
            dkx_ref[...] += _dot_tn(dsx, q)
            dvx_ref[...] += _dot_tn(px, dob)
        dq_ref[...] = dq

    qs = pl.BlockSpec((TQ, HEAD), lambda b, kv, g, qi: (qoff + b * nq + qi, kv * 2 + g))
    hs = lambda c0: pl.BlockSpec((dm.M, HEAD), lambda b, kv, g, qi: (hoff + b, c0 + kv))
    xs = lambda c0: pl.BlockSpec((dm.N, HEAD), lambda b, kv, g, qi: (b, c0 + kv))
    dos = pl.BlockSpec((TQ, HEAD), lambda b, kv, g, qi: (b * nq + qi, kv * 2 + g))
    acc_h = pl.BlockSpec((dm.M, HEAD), lambda b, kv, g, qi: (b, kv))
    acc_x = pl.BlockSpec((dm.N, HEAD), lambda b, kv, g, qi: (b, kv))
    in_specs, args = [qs, hs(4), hs(6)], [QKV, QKV, QKV]
    out_specs = [dos, acc_h, acc_h]
    out_shape = [_sds((rows, A_HEADS * HEAD), F32), _sds((dm.Th, A_KV * HEAD), F32), _sds((dm.Th, A_KV * HEAD), F32)]
    if with_x:
        in_specs += [xs(4), xs(6), dos, acc_h, acc_h]
        args += [QKV, QKV, dO, init[0], init[1]]
        out_specs += [acc_x, acc_x]
        out_shape += [_sds((dm.Tx, A_KV * HEAD), F32), _sds((dm.Tx, A_KV * HEAD), F32)]
    else:
        in_specs += [dos]
        args += [dO]
    return pl.pallas_call(
        body, name=name, grid=(dm.Bl, A_KV, 2, nq), in_specs=in_specs, out_specs=out_specs,
        out_shape=out_shape, compiler_params=_cp(),
    )(*args)


def _pool_mean(u, w):
    n = u.shape[0]
    t = lax.broadcasted_iota(jnp.int32, (n, 1), 0)
    cnt = (jnp.clip(t + (w - w // 2), 0, n) - jnp.clip(t - w // 2, 0, n)).astype(F32)
    s = _shift_rows(u, -(w // 2))
    for j in range(-(w // 2) + 1, w - w // 2):
        s = s + _shift_rows(u, j)
    return s / cnt - u


def pool_fwd(P0, pw, pscale, dm, on_x, name):
    n, off, rows = (dm.N, 0, dm.Tx) if on_x else (dm.M, dm.Tx // dm.M, dm.Th)
    ng = len(POOL_WINDOWS)

    def body(u_ref, w_ref, s_ref, o_ref):
        for g, w in enumerate(POOL_WINDOWS):
            cols = pl.ds(g * HEAD, HEAD)
            pooled = _pool_mean(u_ref[:, cols], w)
            o_ref[:, cols] = (_dot(pooled, w_ref[g]) * s_ref[:, cols]).astype(BF16)

    return pl.pallas_call(
        body, name=name, grid=(dm.Bl,),
        in_specs=[pl.BlockSpec((n, ng * HEAD), lambda b: (off + b, 2)),
                  pl.BlockSpec((ng, HEAD, HEAD), lambda b: (0, 0, 0)), pl.BlockSpec((1, ng * HEAD), lambda b: (0, 0))],
        out_specs=pl.BlockSpec((n, ng * HEAD), lambda b: (b, 0)),
        out_shape=_sds((rows, ng * HEAD), BF16), compiler_params=_cp(),
    )(P0, pw, pscale)


def pool_bwd(P0, dY, pw, pwT, pscale, dm, on_x, name):
    n, off, rows = (dm.N, 0, dm.Tx) if on_x else (dm.M, dm.Tx // dm.M, dm.Th)
    ng = len(POOL_WINDOWS)

    def body(u_ref, dy_ref, w_ref, wt_ref, s_ref, du_ref, dw_ref, ds_ref):
        b = pl.program_id(0)
        for g, w in enumerate(POOL_WINDOWS):
            cols = pl.ds(g * HEAD, HEAD)
            pooled, vjp = jax.vjp(lambda u: _pool_mean(u, w), u_ref[:, cols])
            pre = _dot(pooled, w_ref[g])
            dy = dy_ref[:, cols]
            dpre = dy * s_ref[:, cols]
            du_ref[:, cols] = vjp(_dot(dpre, wt_ref[g]))[0]
            _acc(dw_ref, (g, slice(None), slice(None)), _dot_tn(pooled, dpre), b == 0)
            _acc(ds_ref, (slice(None), cols), jnp.sum(dy * pre, axis=0, keepdims=True), b == 0)

    full = pl.BlockSpec((ng, HEAD, HEAD), lambda b: (0, 0, 0))
    vec = pl.BlockSpec((1, ng * HEAD), lambda b: (0, 0))
    return pl.pallas_call(
        body, name=name, grid=(dm.Bl,),
        in_specs=[pl.BlockSpec((n, ng * HEAD), lambda b: (off + b, 2)), pl.BlockSpec((n, ng * HEAD), lambda b: (b, 0)),
                  full, full, vec],
        out_specs=[pl.BlockSpec((n, ng * HEAD), lambda b: (b, 0)), full, vec],
        out_shape=[_sds((rows, ng * HEAD), F32), _sds((ng, HEAD, HEAD), F32), _sds((1, ng * HEAD), F32)],
        compiler_params=_cp(),
    )(P0, dY, pw, pwT, pscale)


def _conv_fn(p, w0, w1, w2, kind):
    c = w0 * _shift_rows(p, -1) + w1 * p + w2 * _shift_rows(p, 1)
    a = _silu(c)
    if kind == 2:
        return a
    a = a * lax.rsqrt(jnp.sum(a * a, axis=-1, keepdims=True) + EPS)
    return a * (HEAD ** -0.5) if kind == 0 else a


def gdn_prep(P1, conv_w, dm, on_x, name):
    n, off, rows = (dm.N, 0, dm.Tx) if on_x else (dm.M, dm.Tx // dm.M, dm.Th)

    def body(p_ref, w_ref, o_ref):
        j = pl.program_id(1)
        p, w = p_ref[...], w_ref[...]
        for kind in range(3):
            @pl.when(j // C_HEADS == kind)
            def _():
                o_ref[...] = _conv_fn(p, w[0:1], w[1:2], w[2:3], kind)

    return pl.pallas_call(
        body, name=name, grid=(dm.Bl, 3 * C_HEADS),
        in_specs=[pl.BlockSpec((n, HEAD), lambda b, j: (off + b, j)), pl.BlockSpec((3, HEAD), lambda b, j: (0, j))],
        out_specs=pl.BlockSpec((n, HEAD), lambda b, j: (b, j)),
        out_shape=_sds((rows, 3 * C_HEADS * HEAD), F32), compiler_params=_cp(),
    )(P1, conv_w)


def gdn_prep_bwd(P1, dQ, conv_w, dm, on_x, name):
    n, off, rows = (dm.N, 0, dm.Tx) if on_x else (dm.M, dm.Tx // dm.M, dm.Th)

    def body(p_ref, d0_ref, d1_ref, w_ref, dp_ref, dw_ref):
        j, b = pl.program_id(0), pl.program_id(1)
        p, w = p_ref[...], w_ref[...]
        for kind in range(3):
            @pl.when(j // C_HEADS == kind)
            def _():
                _, vjp = jax.vjp(functools.partial(_conv_fn, kind=kind), p, w[0:1], w[1:2], w[2:3])
                dp, d0, d1, d2 = vjp(d0_ref[0] + d1_ref[0])
                dp_ref[...] = dp
                _acc(dw_ref, (pl.ds(0, 1), slice(None)), d0, b == 0)
                _acc(dw_ref, (pl.ds(1, 1), slice(None)), d1, b == 0)
                _acc(dw_ref, (pl.ds(2, 1), slice(None)), d2, b == 0)

    return pl.pallas_call(
        body, name=name, grid=(3 * C_HEADS, dm.Bl),
        in_specs=[pl.BlockSpec((n, HEAD), lambda j, b: (off + b, j)),
                  pl.BlockSpec((1, n, HEAD), lambda j, b: (0, off + b, j)), pl.BlockSpec((1, n, HEAD), lambda j, b: (1, off + b, j)),
                  pl.BlockSpec((3, HEAD), lambda j, b: (0, j))],
        out_specs=[pl.BlockSpec((n, HEAD), lambda j, b: (b, j)), pl.BlockSpec((3, HEAD), lambda j, b: (0, j))],
        out_shape=[_sds((rows, 3 * C_HEADS * HEAD), F32), _sds((3, 3 * C_HEADS * HEAD), F32)],
        compiler_params=_cp(),
    )(P1, dQ, dQ, conv_w)


def _gate_fn(ab, par):
    lane = lax.broadcasted_iota(jnp.int32, ab.shape, 1)
    is_a = (lane % 16) < C_HEADS
    g = -jnp.exp(par[0:1]) * jax.nn.softplus(ab + par[1:2])
    return jnp.where(lane < 4 * C_HEADS, jnp.where(is_a, g, jax.nn.sigmoid(ab)), 0.0)


def _col(blk, idx):
    lane = lax.broadcasted_iota(jnp.int32, blk.shape, 1)
    return jnp.sum(jnp.where(lane == idx, blk, 0.0), axis=1, keepdims=True)


def _chunk_masks(rev):
    ii = lax.broadcasted_iota(jnp.int32, (CHUNK, CHUNK), 0)
    jj = lax.broadcasted_iota(jnp.int32, (CHUNK, CHUNK), 1)
    ahead = jnp.where(rev, jj - ii, ii - jj)
    return ahead >= 0, ahead > 0, (ii == jj).astype(F32)


def _inv_unit_tri(nmats, eye):
    xs = [eye - n for n in nmats]
    ps = [_hdot(n, n) for n in nmats]
    step = 2
    while True:
        xs = [x + _hdot(x, p) for x, p in zip(xs, ps)]
        step *= 2
        if step >= CHUNK:
            break
        ps = [_hdot(p, p) for p in ps]
    return xs


def _cum_lanes(x, transpose=False):
    lane = lax.broadcasted_iota(jnp.int32, x.shape, 1)
    down, up = x, x
    s = 1
    while s < CHUNK:
        down = down + _shift_rows(down, -s)
        up = up + _shift_rows(up, s)
        s *= 2
    return jnp.where((lane >= 16) if transpose else (lane < 16), down, up)


def _each(f, *lists):
    return [f(*a) for a in zip(*lists)]


def _chunk_common(qs, ks, vs, gcs, gcrs, tots, betas, rev):
    incl, strict, eye = _chunk_masks(rev)
    es = _each(lambda gc, gcr: jnp.exp(jnp.where(incl, gc - gcr, NEG)), gcs, gcrs)
    egs = [jnp.exp(gc) for gc in gcs]
    ets = _each(lambda t, gc: jnp.exp(t - gc), tots, gcs)
    gts = [jnp.exp(t) for t in tots]
    kbs = _each(lambda k, b: k * b, ks, betas)
    kks = _each(_dot_nt, kbs, ks)
    qqs = _each(_dot_nt, qs, ks)
    nmats = _each(lambda kk, e: jnp.where(strict, kk * e, 0.0), kks, es)
    ainvs = _inv_unit_tri(nmats, eye)
    rhss = _each(lambda v, b, kb, eg: jnp.concatenate([v * b, kb * eg], axis=1), vs, betas, kbs, egs)
    sols = _each(_hdot, ainvs, rhss)
    return dict(incl=incl, strict=strict, e=es, eg=egs, et=ets, gt=gts, kb=kbs, kk=kks, ainv=ainvs, sol=sols, qq=qqs)


def _chunk_fwd(qs, ks, vs, gcs, gcrs, tots, betas, rev):
    c = _chunk_common(qs, ks, vs, gcs, gcrs, tots, betas, rev)
    incl = c["incl"]
    return _each(lambda q, k, sol, qq, e, et, eg, gt:
                 (sol[:, :HEAD], sol[:, HEAD:], k * et, q * eg, jnp.where(incl, qq * e, 0.0), gt),
                 qs, ks, c["sol"], c["qq"], c["e"], c["et"], c["eg"], c["gt"])


def _chunk_bwd(qs, ks, vs, gcs, gcrs, tots, betas, rev, dus, dws, dkts, dqds, dqks, dgts):
    c = _chunk_common(qs, ks, vs, gcs, gcrs, tots, betas, rev)
    incl, strict = c["incl"], c["strict"]
    drhss = _each(lambda a, du, dw: _hdot_tn(a, jnp.concatenate([du, dw], axis=1)), c["ainv"], dus, dws)
    dns = _each(lambda drhs, sol: jnp.where(strict, -_hdot_nt(drhs, sol), 0.0), drhss, c["sol"])
    dkks = _each(lambda dn, e: dn * e, dns, c["e"])
    dqms = [jnp.where(incl, dqk, 0.0) for dqk in dqks]
    dqqs = _each(lambda dqm, e: dqm * e, dqms, c["e"])
    m_q = _each(_dot, dqqs, ks)
    m_k1 = _each(_dot_tn, dqqs, qs)
    m_k2 = _each(_dot_tn, dkks, c["kb"])
    m_kb = _each(_dot, dkks, ks)

    def finish(q, k, v, beta, e, eg, et, gt, kb, kk, qq, drhs, dn, dqm, dkt, dqd, dgt, mq, mk1, mk2, mkb):
        de = dn * kk + dqm * qq
        dq = mq + dqd * eg
        dkb = mkb + drhs[:, HEAD:] * eg
        dk = mk1 + mk2 + dkt * et + dkb * beta
        dv = drhs[:, :HEAD] * beta
        dbeta = jnp.sum(drhs[:, :HEAD] * v + dkb * k, axis=1, keepdims=True)
        deg = jnp.sum(drhs[:, HEAD:] * kb + dqd * q, axis=1, keepdims=True)
        dd = de * e
        dtd = jnp.sum(dkt * k, axis=1, keepdims=True) * et
        dgc = deg * eg - dtd + jnp.sum(dd, axis=1, keepdims=True) - jnp.sum(dd.T, axis=1, keepdims=True)
        dtot = jnp.sum(dtd, axis=0, keepdims=True) + dgt * gt
        return dq, dk, dv, dgc, dtot, dbeta

    return _each(finish, qs, ks, vs, betas, c["e"], c["eg"], c["et"], c["gt"], c["kb"], c["kk"], c["qq"],
                 drhss, dns, dqms, dkts, dqds, dgts, m_q, m_k1, m_k2, m_kb)


def gdn_chunk_pre(QKVg, P1, par, dm, name):
    nch = dm.T // CHUNK
    HD = C_HEADS * HEAD
    abcol = (4 * HD) // LANE

    def body(x_ref, ab_ref, par_ref, u_ref, w_ref, kt_ref, qd_ref, qk_ref, gt_ref, gct_s):
        d = pl.program_id(1)
        rev = d == 1
        gb = _gate_fn(ab_ref[...], par_ref[...])
        gcl = _cum_lanes(gb)
        gct_s[...] = gcl.T
        tot = jnp.sum(gb, axis=0, keepdims=True)
        hs = range(C_HEADS)
        outs = _chunk_fwd(
            [x_ref[:, pl.ds(h * HEAD, HEAD)] for h in hs],
            [x_ref[:, pl.ds((C_HEADS + h) * HEAD, HEAD)] for h in hs],
            [x_ref[:, pl.ds((2 * C_HEADS + h) * HEAD, HEAD)] for h in hs],
            [_col(gcl, d * 16 + h) for h in hs], [gct_s[pl.ds(d * 16 + h, 1), :] for h in hs],
            [_col(tot, d * 16 + h) for h in hs], [_col(gb, d * 16 + 8 + h) for h in hs], rev)
        for h, (u, w, kt, qd, qk, gt) in enumerate(outs):
            cols = pl.ds(h * HEAD, HEAD)
            u_ref[0, :, cols] = u
            w_ref[0, :, cols] = w.astype(BF16)
            kt_ref[0, :, cols] = kt.astype(BF16)
            qd_ref[0, :, cols] = qd.astype(BF16)
            qk_ref[0, :, cols] = jnp.concatenate([qk, jnp.zeros_like(qk)], axis=1).astype(BF16)
            gt_ref[0, 0, pl.ds(h, 1), :] = jnp.broadcast_to(gt, (1, HEAD))

    big = pl.BlockSpec((1, CHUNK, HD), lambda i, d: (d, i, 0))
    return pl.pallas_call(
        body, name=name, grid=(nch, 2),
        in_specs=[pl.BlockSpec((CHUNK, 3 * HD), lambda i, d: (i, 0)), pl.BlockSpec((CHUNK, LANE), lambda i, d: (i, abcol)),
                  pl.BlockSpec((2, LANE), lambda i, d: (0, 0))],
        out_specs=[big, big, big, big, big, pl.BlockSpec((1, 1, C_HEADS, HEAD), lambda i, d: (d, i, 0, 0))],
        out_shape=[_sds((2, dm.T, HD), F32), _sds((2, dm.T, HD), BF16), _sds((2, dm.T, HD), BF16),
                   _sds((2, dm.T, HD), BF16), _sds((2, dm.T, HD), BF16), _sds((2, nch, C_HEADS, HEAD), F32)],
        scratch_shapes=[pltpu.VMEM((LANE, CHUNK), F32)], compiler_params=_cp(),
    )(QKVg, P1, par)


def gdn_chunk_pre_bwd(QKVg, P1, par, dU, dW, dKT, dQD, dQK, dGT, dm, name):
    nch = dm.T // CHUNK
    HD = C_HEADS * HEAD
    abcol = (4 * HD) // LANE

    def body(x_ref, ab_ref, par_ref, du_ref, dw_ref, dkt_ref, dqd_ref, dqk_ref, dgt_ref, dx_ref, dab_ref, dpar_ref, gct_s):
        i, d = pl.program_id(0), pl.program_id(1)
        rev = d == 1
        ab, par = ab_ref[...], par_ref[...]
        gb, gate_vjp = jax.vjp(_gate_fn, ab, par)
        gcl = _cum_lanes(gb)
        gct_s[...] = gcl.T
        tot = jnp.sum(gb, axis=0, keepdims=True)
        lane = lax.broadcasted_iota(jnp.int32, (CHUNK, LANE), 1)
        dgcl = jnp.zeros((CHUNK, LANE), F32)
        dgb = jnp.zeros((CHUNK, LANE), F32)
        first = d == 0
        hs = range(C_HEADS)
        hcols = [pl.ds(h * HEAD, HEAD) for h in hs]
        outs = _chunk_bwd(
            [x_ref[:, c] for c in hcols],
            [x_ref[:, pl.ds((C_HEADS + h) * HEAD, HEAD)] for h in hs],
            [x_ref[:, pl.ds((2 * C_HEADS + h) * HEAD, HEAD)] for h in hs],
            [_col(gcl, d * 16 + h) for h in hs], [gct_s[pl.ds(d * 16 + h, 1), :] for h in hs],
            [_col(tot, d * 16 + h) for h in hs], [_col(gb, d * 16 + 8 + h) for h in hs], rev,
            [du_ref[0, :, c] for c in hcols], [dw_ref[0, :, c] for c in hcols], [dkt_ref[0, :, c] for c in hcols],
            [dqd_ref[0, :, c] for c in hcols], [dqk_ref[0, :, pl.ds(h * HEAD, CHUNK)] for h in hs],
            [dgt_ref[0, 0, pl.ds(h, 1), pl.ds(0, 1)] for h in hs])
        for h, (dq, dk, dv, dgc, dtotal, dbeta) in enumerate(outs):
            idx = d * 16 + h
            dx_ref[0, :, hcols[h]] = dq
            dx_ref[0, :, pl.ds((C_HEADS + h) * HEAD, HEAD)] = dk
            dx_ref[0, :, pl.ds((2 * C_HEADS + h) * HEAD, HEAD)] = dv
            dgcl = dgcl + jnp.where(lane == idx, dgc, 0.0)
            dgb = dgb + jnp.where(lane == idx + 8, dbeta, 0.0) + jnp.where(lane == idx, dtotal, 0.0)
        dab, dpar = gate_vjp(dgb + _cum_lanes(dgcl, transpose=True))
        dab_ref[0] = dab
        _acc(dpar_ref, (slice(None), slice(None)), dpar, (i == 0) & first)

    big = pl.BlockSpec((1, CHUNK, HD), lambda i, d: (d, i, 0))
    return pl.pallas_call(
        body, name=name, grid=(nch, 2),
        in_specs=[pl.BlockSpec((CHUNK, 3 * HD), lambda i, d: (i, 0)), pl.BlockSpec((CHUNK, LANE), lambda i, d: (i, abcol)),
                  pl.BlockSpec((2, LANE), lambda i, d: (0, 0)), big, big, big, big, big,
                  pl.BlockSpec((1, 1, C_HEADS, HEAD), lambda i, d: (d, i, 0, 0))],
        out_specs=[pl.BlockSpec((1, CHUNK, 3 * HD), lambda i, d: (d, i, 0)), pl.BlockSpec((1, CHUNK, LANE), lambda i, d: (d, i, 0)),
                   pl.BlockSpec((2, LANE), lambda i, d: (0, 0))],
        out_shape=[_sds((2, dm.T, 3 * HD), F32), _sds((2, dm.T, LANE), F32), _sds((2, LANE), F32)],
        scratch_shapes=[pltpu.VMEM((LANE, CHUNK), F32)], compiler_params=_cp(),
    )(QKVg, P1, par, dU, dW, dKT, dQD, dQK, dGT)


def _scan_chunk(b, d, c, dm):
    nh, nx = dm.M // CHUNK, dm.N // CHUNK
    in_h = c < nh
    pos_h = jnp.where(d == 0, c, nh - 1 - c)
    pos_x = jnp.where(d == 0, c - nh, nx - 1 - (c - nh))
    return jnp.where(in_h, dm.Tx // CHUNK + b * nh + pos_h, b * nx + pos_x)


def gdn_scan_fwd(U, W, KT, QD, QK, GT, dm, name):
    nch = dm.T // CHUNK
    HD = C_HEADS * HEAD
    nsc = (dm.M + dm.N) // CHUNK

    def body(u_ref, w_ref, kt_ref, qd_ref, qk_ref, gt_ref, o_ref, ss_ref, s_s):
        @pl.when(pl.program_id(2) == 0)
        def _():
            s_s[...] = jnp.zeros_like(s_s)

        hs = range(C_HEADS)
        blk = [pl.ds(h * HEAD, HEAD) for h in hs]
        ss = [s_s[b, :] for b in blk]
        for b, s in zip(blk, ss):
            ss_ref[0, 0, b, :] = s
        sbs = [s.astype(BF16) for s in ss]
        ws = [jnp.dot(w_ref[0, :, b], sb, preferred_element_type=F32) for b, sb in zip(blk, sbs)]
        os1 = [jnp.dot(qd_ref[0, :, b], sb, preferred_element_type=F32) for b, sb in zip(blk, sbs)]
        vnbs = [(u_ref[0, :, b] - wv).astype(BF16) for b, wv in zip(blk, ws)]
        os2 = [jnp.dot(qk_ref[0, :, pl.ds(h * HEAD, CHUNK)], vnbs[h], preferred_element_type=F32) for h in hs]
        upd = [_dot_tn(kt_ref[0, :, b], vnb) for b, vnb in zip(blk, vnbs)]
        for h in hs:
            o_ref[0, :, blk[h]] = os1[h] + os2[h]
            s_s[blk[h], :] = ss[h] * gt_ref[0, 0, pl.ds(h, 1), :] + upd[h]

    big = pl.BlockSpec((1, CHUNK, HD), lambda b, d, c: (d, _scan_chunk(b, d, c, dm), 0))
    return pl.pallas_call(
        body, name=name, grid=(dm.Bl, 2, nsc),
        in_specs=[big, big, big, big, big,
                  pl.BlockSpec((1, 1, C_HEADS, HEAD), lambda b, d, c: (d, _scan_chunk(b, d, c, dm), 0, 0))],
        out_specs=[big, pl.BlockSpec((1, 1, HD, HEAD), lambda b, d, c: (d, _scan_chunk(b, d, c, dm), 0, 0))],
        out_shape=[_sds((2, dm.T, HD), F32), _sds((2, nch, HD, HEAD), F32)],
        scratch_shapes=[pltpu.VMEM((HD, HEAD), F32)], compiler_params=_cp(),
    )(U, W, KT, QD, QK, GT)


def gdn_scan_bwd(dO, SS, U, W, KT, QD, QK, GT, dm, name):
    nch = dm.T // CHUNK
    HD = C_HEADS * HEAD
    nsc = (dm.M + dm.N) // CHUNK

    def body(do_ref, ss_ref, u_ref, w_ref, kt_ref, qd_ref, qk_ref, gt_ref,
             du_ref, dw_ref, dkt_ref, dqd_ref, dqk_ref, dgt_ref, ds_s):
        @pl.when(pl.program_id(2) == 0)
        def _():
            ds_s[...] = jnp.zeros_like(ds_s)

        hs = range(C_HEADS)
        blk = [pl.ds(h * HEAD, HEAD) for h in hs]
        ss = [ss_ref[0, 0, b, :] for b in blk]
        sbs = [s.astype(BF16) for s in ss]
        dobs = [do_ref[:, b].astype(BF16) for b in blk]
        dsns = [ds_s[b, :] for b in blk]
        dsnbs = [t.astype(BF16) for t in dsns]
        wss = [jnp.dot(w_ref[0, :, b], sb, preferred_element_type=F32) for b, sb in zip(blk, sbs)]
        dqds = [_dot_nt(dob, sb) for dob, sb in zip(dobs, sbs)]
        dv1 = [_dot_tn(qk_ref[0, :, pl.ds(h * HEAD, CHUNK)], dobs[h]) for h in hs]
        dv2 = [jnp.dot(kt_ref[0, :, b], t, preferred_element_type=F32) for b, t in zip(blk, dsnbs)]
        ds1 = [_dot_tn(qd_ref[0, :, b], dob) for b, dob in zip(blk, dobs)]
        vnbs = [(u_ref[0, :, b] - wv).astype(BF16) for b, wv in zip(blk, wss)]
        dvns = [a + b for a, b in zip(dv1, dv2)]
        dvnbs = [t.astype(BF16) for t in dvns]
        dqks = [_dot_nt(dob, vnb) for dob, vnb in zip(dobs, vnbs)]
        dkts = [_dot_nt(vnb, t) for vnb, t in zip(vnbs, dsnbs)]
        dws = [_dot_nt(t, sb) for t, sb in zip(dvnbs, sbs)]
        ds2 = [_dot_tn(w_ref[0, :, b], t) for b, t in zip(blk, dvnbs)]
        for h in hs:
            b = blk[h]
            dqd_ref[0, :, b] = dqds[h]
            dqk_ref[0, :, b] = jnp.concatenate([dqks[h], jnp.zeros_like(dqks[h])], axis=1)
            dkt_ref[0, :, b] = dkts[h]
            du_ref[0, :, b] = dvns[h]
            dw_ref[0, :, b] = -dws[h]
            dgt_ref[0, 0, pl.ds(h, 1), :] = jnp.broadcast_to(jnp.sum(dsns[h] * ss[h], keepdims=True), (1, HEAD))
            ds_s[b, :] = dsns[h] * gt_ref[0, 0, pl.ds(h, 1), :] + ds1[h] - ds2[h]

    def mem(b, d, c):
        return _scan_chunk(b, d, nsc - 1 - c, dm)

    big = pl.BlockSpec((1, CHUNK, HD), lambda b, d, c: (d, mem(b, d, c), 0))
    gts = pl.BlockSpec((1, 1, C_HEADS, HEAD), lambda b, d, c: (d, mem(b, d, c), 0, 0))
    return pl.pallas_call(
        body, name=name, grid=(dm.Bl, 2, nsc),
        in_specs=[pl.BlockSpec((CHUNK, HD), lambda b, d, c: (mem(b, d, c), 0)),
                  pl.BlockSpec((1, 1, HD, HEAD), lambda b, d, c: (d, mem(b, d, c), 0, 0)), big, big, big, big, big, gts],
        out_specs=[big, big, big, big, big, gts],
        out_shape=[_sds((2, dm.T, HD), F32)] * 5 + [_sds((2, nch, C_HEADS, HEAD), F32)],
        scratch_shapes=[pltpu.VMEM((HD, HEAD), F32)], compiler_params=_cp(),
    )(dO, SS, U, W, KT, QD, QK, GT)


def _finish_fn(o, z, gain):
    y = o * lax.rsqrt(jnp.mean(o * o, axis=-1, keepdims=True) + EPS) * gain
    return y * _silu(z)


def gdn_finish(O, P1, og, dm, name):
    TM = dm.TM
    HD = C_HEADS * HEAD
    zc = (3 * HD) // HEAD

    def body(o0_ref, o1_ref, z_ref, g_ref, y_ref):
        y_ref[...] = _finish_fn(o0_ref[0] + o1_ref[0], z_ref[...], g_ref[...]).astype(BF16)

    return pl.pallas_call(
        body, name=name, grid=(dm.ntx, C_HEADS),
        in_specs=[pl.BlockSpec((1, TM, HEAD), lambda i, j: (0, i, j)), pl.BlockSpec((1, TM, HEAD), lambda i, j: (1, i, j)),
                  pl.BlockSpec((TM, HEAD), lambda i, j: (i, zc + j)), pl.BlockSpec((1, HEAD), lambda i, j: (0, 0))],
        out_specs=pl.BlockSpec((TM, HEAD), lambda i, j: (i, j)),
        out_shape=_sds((dm.Tx, HD), BF16), compiler_params=_cp(),
    )(O, O, P1, og)


def gdn_finish_bwd(O, P1, og, dY, dm, name):
    TM = dm.TM
    HD = C_HEADS * HEAD
    zc = (3 * HD) // HEAD

    def body(o0_ref, o1_ref, z_ref, g_ref, dy_ref, do_ref, dz_ref, dg_ref):
        i, j = pl.program_id(0), pl.program_id(1)
        _, vjp = jax.vjp(_finish_fn, o0_ref[0] + o1_ref[0], z_ref[...], g_ref[...])
        do, dz, dg = vjp(dy_ref[...])
        do_ref[...] = do
        dz_ref[...] = dz
        _acc(dg_ref, (slice(None), slice(None)), dg, (i == 0) & (j == 0))

    blk = pl.BlockSpec((TM, HEAD), lambda i, j: (i, j))
    return pl.pallas_call(
        body, name=name, grid=(dm.ntx, C_HEADS),
        in_specs=[pl.BlockSpec((1, TM, HEAD), lambda i, j: (0, i, j)), pl.BlockSpec((1, TM, HEAD), lambda i, j: (1, i, j)),
                  pl.BlockSpec((TM, HEAD), lambda i, j: (i, zc + j)), pl.BlockSpec((1, HEAD), lambda i, j: (0, 0)), blk],
        out_specs=[blk, blk, pl.BlockSpec((1, HEAD), lambda i, j: (0, 0))],
        out_shape=[_sds((dm.Tx, HD), F32), _sds((dm.Tx, HD), F32), _sds((1, HEAD), F32)],
        compiler_params=_cp(),
    )(O, O, P1, og, dY)


def adaln_fwd(c_ext, w_mod, b_loc, name):
    R, D = c_ext.shape
    nl = w_mod.shape[2]
    tn = _pick(nl, 384)

    def body(c_ref, w_ref, b_ref, o_ref):
        o_ref[0] = _dot(_silu(c_ref[...]), w_ref[0]) + b_ref[0]

    return pl.pallas_call(
        body, name=name, grid=(2, nl // tn),
        in_specs=[pl.BlockSpec((R, D), lambda l, j: (0, 0)), pl.BlockSpec((1, D, tn), lambda l, j: (l, 0, j)),
                  pl.BlockSpec((1, 1, tn), lambda l, j: (l, 0, j))],
        out_specs=pl.BlockSpec((1, R, tn), lambda l, j: (l, 0, j)),
        out_shape=_sds((2, R, nl), F32), compiler_params=_cp(),
    )(c_ext, w_mod, b_loc)


def adaln_bwd(c_ext, c_ctx, w_mod, dmx, dmh, nb, name):
    R, D = c_ext.shape
    nl = w_mod.shape[2]
    tn = _pick(nl, 384)
    nj = nl // tn

    def body(c_ref, cc_ref, w_ref, dmx_ref, dmh_ref, gw_ref, dc_ref):
        l, j = pl.program_id(0), pl.program_id(1)
        dh = dmh_ref[0, 0:1, :]
        for k in range(1, N_DEV):
            dh = dh + dmh_ref[0, k:k + 1, :]
        row = lax.broadcasted_iota(jnp.int32, (R, tn), 0)
        dmat = dmx_ref[0] + jnp.where(row == nb, dh, 0.0)
        gw_ref[0] = _dot_tn(_silu(c_ref[...]), dmat)
        part = _dot_nt(jnp.broadcast_to(dh, (8, tn)), w_ref[0])[0:1]
        _acc(dc_ref, (slice(None), slice(None)), part, (l == 0) & (j == 0))

        @pl.when((l == 1) & (j == nj - 1))
        def _():
            cc = cc_ref[...]
            sg = jax.nn.sigmoid(cc)
            dc_ref[...] = dc_ref[...] * (sg * (1.0 + cc * (1.0 - sg)))

    return pl.pallas_call(
        body, name=name, grid=(2, nj),
        in_specs=[pl.BlockSpec((R, D), lambda l, j: (0, 0)), pl.BlockSpec((1, D), lambda l, j: (0, 0)),
                  pl.BlockSpec((1, D, tn), lambda l, j: (l, 0, j)), pl.BlockSpec((1, R, tn), lambda l, j: (l, 0, j)),
                  pl.BlockSpec((1, N_DEV, tn), lambda l, j: (l, 0, j))],
        out_specs=[pl.BlockSpec((1, D, tn), lambda l, j: (l, 0, j)), pl.BlockSpec((1, D), lambda l, j: (0, 0))],
        out_shape=[_sds((2, D, nl), F32), _sds((1, D), F32)], compiler_params=_cp(),
    )(c_ext, c_ctx, w_mod, dmx, dmh)


def bmod_grad(dmx, dmh, name):
    _, R, n9 = dmx.shape

    def body(dmx_ref, dmh_ref, o_ref):
        o_ref[0] = jnp.sum(dmx_ref[0], axis=0, keepdims=True) + jnp.sum(dmh_ref[0], axis=0, keepdims=True)

    return pl.pallas_call(
        body, name=name, grid=(2,),
        in_specs=[pl.BlockSpec((1, R, n9), lambda l: (l, 0, 0)), pl.BlockSpec((1, N_DEV, n9), lambda l: (l, 0, 0))],
        out_specs=pl.BlockSpec((1, 1, n9), lambda l: (l, 0, 0)), out_shape=_sds((2, 1, n9), F32),
        compiler_params=_cp(),
    )(dmx, dmh)


def adamw(gs, w, m, v, name):
    S, R, C = gs.shape
    cap = max(8, (1 << 20) // (S * C))
    tr = R
    if R > cap:
        tr = max(t for t in range(8, cap + 1, 8) if R % t == 0)

    def body(g_ref, w_ref, m_ref, v_ref, go_ref, d_ref, mo_ref, vo_ref):
        g = g_ref[0].astype(F32)
        for k in range(1, S):
            g = g + g_ref[k].astype(F32)
        mn = ADAM_B1 * m_ref[...] + (1.0 - ADAM_B1) * g
        vn = ADAM_B2 * v_ref[...] + (1.0 - ADAM_B2) * jnp.square(g)
        m_hat = mn / (1.0 - ADAM_B1 ** ADAM_STEP)
        v_hat = vn / (1.0 - ADAM_B2 ** ADAM_STEP)
        go_ref[...] = g
        d_ref[...] = -ADAM_LR * (m_hat / (jnp.sqrt(v_hat) + ADAM_EPS) + ADAM_WD * w_ref[...])
        mo_ref[...] = mn
        vo_ref[...] = vn

    blk = pl.BlockSpec((tr, C), lambda i: (i, 0))
    return pl.pallas_call(
        body, name=name, grid=(R // tr,),
        in_specs=[pl.BlockSpec((S, tr, C), lambda i: (0, i, 0)), blk, blk, blk],
        out_specs=[blk] * 4, out_shape=[_sds((R, C), F32)] * 4, compiler_params=_cp(),
    )(gs, w, m, v)


def _gather_flat(parts, dtype, name):
    flat = jnp.concatenate([p.astype(dtype).reshape(-1) for p in parts])
    n = flat.shape[0]
    pad = (-n) % LANE
    if pad:
        flat = jnp.concatenate([flat, jnp.zeros((pad,), dtype)])
    got = all_gather([flat.reshape(-1, LANE)], name)[0].reshape(N_DEV, -1)
    out, off = [], 0
    for p in parts:
        out.append(got[:, off:off + p.size].reshape((N_DEV,) + p.shape))
        off += p.size
    return out


def _cols_full(g):
    return g.transpose(1, 0, 2).reshape(g.shape[1], -1)


def _cols_split(full):
    K = full.shape[0]
    return full.reshape(K, N_DEV, -1).transpose(1, 0, 2)


def kernel(x, c, ctx, c_ctx, w_mod, b_mod, norm_g, ffn_wg, ffn_wu, ffn_wd, ab_w_in, ab_q_norm, ab_k_norm, pool_w, pool_scale, ab_w_out, gdn_w_in, gdn_conv_w, gdn_a_log, gdn_dt_bias, gdn_o_norm, gdn_w_out, loss_target, m_c_ctx, m_w_mod, m_b_mod, m_norm_g, m_ffn_wg, m_ffn_wu, m_ffn_wd, m_ab_w_in, m_ab_q_norm, m_ab_k_norm, m_pool_w, m_pool_scale, m_ab_w_out, m_gdn_w_in, m_gdn_conv_w, m_gdn_a_log, m_gdn_dt_bias, m_gdn_o_norm, m_gdn_w_out, v_c_ctx, v_w_mod, v_b_mod, v_norm_g, v_ffn_wg, v_ffn_wu, v_ffn_wd, v_ab_w_in, v_ab_q_norm, v_ab_k_norm, v_pool_w, v_pool_scale, v_ab_w_out, v_gdn_w_in, v_gdn_conv_w, v_gdn_a_log, v_gdn_dt_bias, v_gdn_o_norm, v_gdn_w_out):
    Bl, N, D = x.shape
    M = ctx.shape[1]
    F = ffn_wd.shape[2] * N_DEV
    dm = Dims(Bl, N, M, D, F)
    TM, Tx, Th, T, G = dm.TM, dm.Tx, dm.Th, dm.T, dm.G
    HD = C_HEADS * HEAD
    me = 4 * lax.axis_index("x") + 2 * lax.axis_index("y") + lax.axis_index("c")
    nb = N_DEV * Bl
    R = -(-(nb + 1) // 8) * 8
    nl = w_mod.shape[2]
    n_gdn = gdn_w_in.shape[2] * N_DEV
    n_gdn_pad = -(-n_gdn // LANE) * LANE

    big = [w.astype(BF16) for w in (ffn_wg, ffn_wu, ffn_wd, ab_w_in, ab_w_out, gdn_w_in, gdn_w_out)]
    g_wg, g_wu, g_wd, g_abin, g_about, g_gin, g_gout = all_gather(big, "gather_weights")
    g_c, g_ng, g_cw = _gather_flat([c, norm_g, gdn_conv_w], F32, "gather_small")
    W_ABIN = _cols_full(g_abin[:, 0])
    W_ABOUT = g_about[:, 0].reshape(-1, D)
    W_GIN = jnp.pad(_cols_full(g_gin[:, 0]), ((0, 0), (0, n_gdn_pad - n_gdn)))
    W_GOUT = g_gout[:, 0].reshape(-1, D)
    gains = g_ng.transpose(1, 2, 0, 3).reshape(2, 3, 1, D)
    conv_w = g_cw[:, 0].transpose(1, 0, 2).reshape(3, -1)

    c_all = g_c.reshape(nb, D)
    c_ext = jnp.concatenate([c_all, c_ctx[None], jnp.zeros((R - nb - 1, D), F32)], 0)
    b_loc = lax.dynamic_slice_in_dim(b_mod, me * nl, nl, axis=1).reshape(2, 1, nl)
    mod_loc = adaln_fwd(c_ext, w_mod, b_loc, "adaln_fwd")
    (g_mod,) = _gather_flat([mod_loc], F32, "gather_mod")
    mod_full = g_mod.transpose(1, 2, 0, 3).reshape(2, R, 9 * D)
    MOD = []
    for l in range(2):
        mine = lax.dynamic_slice_in_dim(mod_full[l], me * Bl, Bl, axis=0)
        MOD.append(jnp.concatenate([mine, mod_full[l, nb:nb + 1]], 0).reshape(G, 9, D))

    dm5, dmr = dm.with_tile(512), dm.with_tile(1024)
    tabs = _rope_tables(dmr)
    qkg = jnp.stack([ab_q_norm, ab_k_norm])
    pw = pool_w[0].astype(BF16)
    pwT = pool_w[0].transpose(0, 2, 1).astype(BF16)
    par = jnp.stack([jnp.pad(jnp.pad(p[0], ((0, 0), (0, 8))).reshape(-1), (0, LANE - 32))
                     for p in (gdn_a_log, gdn_dt_bias)])

    X0 = jnp.concatenate([x.reshape(Tx, D), ctx.reshape(Th, D)], 0)
    nt, ntx = dm.nt, dm.ntx
    def ffn(X, l, s, s0, all_rows, tag):
        return ffn_fwd(X, MOD[l], gains[l, 2 * s], g_wg, g_wu, g_wd, l, s, s0, dm, all_rows, "ffn_fwd_" + tag)

    X1, Y1 = ffn(X0, 0, 0, 0, True, "00")
    P0, XN0 = modmm(X1, MOD[0], gains[0, 1], W_ABIN, 3, dm5, "ab_in_proj")
    QKV = qkv_prep(P0, qkg, tabs, dmr, "qkv_prep")
    ATT = jnp.concatenate([attn_fwd(QKV, dm, True, "attn_fwd_x"), attn_fwd(QKV, dm, False, "attn_fwd_h")], 0)
    POOL = jnp.concatenate([pool_fwd(P0, pw, pool_scale, dm, True, "pool_fwd_x"),
                            pool_fwd(P0, pw, pool_scale, dm, False, "pool_fwd_h")], 0)
    na = A_HEADS * HEAD
    X2, YM0 = proj_res([ATT, POOL], [W_ABOUT[:na], W_ABOUT[na:]], X1, MOD[0], dm5, dm5.nt, "ab_out_proj")
    X3, Y3 = ffn(X2, 0, 1, 6, True, "01")
    X4, Y4 = ffn(X3, 1, 0, 0, True, "10")
    P1, XN1 = modmm(X4, MOD[1], gains[1, 1], W_GIN, 3, dm5, "gdn_in_proj")
    QKVg = jnp.concatenate([gdn_prep(P1, conv_w, dm, True, "gdn_prep_x"), gdn_prep(P1, conv_w, dm, False, "gdn_prep_h")], 0)
    U, W, KT, QD, QK, GT = gdn_chunk_pre(QKVg, P1, par, dm, "gdn_chunk_pre")
    O, SS = gdn_scan_fwd(U, W, KT, QD, QK, GT, dm, "gdn_scan_fwd")
    FIN = gdn_finish(O, P1, gdn_o_norm, dmr, "gdn_finish")
    X5, YM1 = proj_res([FIN], [W_GOUT], X4[:Tx], MOD[1], dm5, dm5.ntx, "gdn_out_proj")
    X6, Y6 = ffn(X5, 1, 1, 6, False, "11")
    lvec, dX6 = loss_head(X6, loss_target.reshape(Tx, D), dmr, "loss_head")
    loss = lax.psum(lvec[0, 0], AXES)

    zrow = lambda a: jnp.concatenate([a, jnp.zeros((G - a.shape[0],) + a.shape[1:], F32)], 0) if a.shape[0] < G else a

    def ffn_back(Xin, dXo, Y, l, s, s0, all_rows, tag):
        dXi, XNb, DOb, Hb, DGb, DUb, dmod, dgain = ffn_bwd(
            Xin, dXo, Y, MOD[l], gains[l, 2 * s], g_wg, g_wu, g_wd, l, s, s0, dm, all_rows, "ffn_bwd_" + tag)
        dwg, dwu, dwd = ffn_dw(XNb, DOb, Hb, DGb, DUb, dm, "ffn_dw_" + tag)
        return dXi, zrow(dmod), dgain, dwg, dwu, dwd

    dX5, dmod_12, dgain_12, dwg11, dwu11, dwd11 = ffn_back(X5, dX6, Y6, 1, 1, 6, False, "11")
    DY1, dFIN, dgate_1 = proj_res_bwd(dX5, YM1, MOD[1], [W_GOUT], dm5, dm5.ntx, "gdn_out_proj_bwd")
    d_gout = atb(FIN, DY1, Tx, dm, "gdn_dwout")
    dOsum, dZ, d_onorm = gdn_finish_bwd(O, P1, gdn_o_norm, dFIN, dmr, "gdn_finish_bwd")
    dO_all = jnp.concatenate([dOsum, jnp.zeros((Th, HD), F32)], 0)
    dU, dW, dKT, dQD, dQK, dGT = gdn_scan_bwd(dO_all, SS, U, W, KT, QD, QK, GT, dm, "gdn_scan_bwd")
    dQKVg, dAB, dPAR = gdn_chunk_pre_bwd(QKVg, P1, par, dU, dW, dKT, dQD, dQK, dGT, dm, "gdn_chunk_pre_bwd")
    dPx, dcw_x = gdn_prep_bwd(P1, dQKVg, conv_w, dm, True, "gdn_prep_bwd_x")
    dPh, dcw_h = gdn_prep_bwd(P1, dQKVg, conv_w, dm, False, "gdn_prep_bwd_h")
    d_conv = dcw_x + dcw_h
    dP1 = jnp.concatenate([jnp.concatenate([dPx, dPh], 0), jnp.concatenate([dZ, jnp.zeros((Th, HD), F32)], 0),
                           dAB[0] + dAB[1]], axis=1).astype(BF16)
    d_gin = atb(XN1, dP1, T, dm, "gdn_dwin")[:, :n_gdn]
    dX5_full = jnp.concatenate([dX5, jnp.zeros((Th, D), F32)], 0)
    dX4, dmod_11, dgain_11 = mixin_bwd(dP1, W_GIN, X4, dX5_full, MOD[1], gains[1, 1], 3, dm, "gdn_in_proj_bwd")
    dX3, dmod_10, dgain_10, dwg10, dwu10, dwd10 = ffn_back(X3, dX4, Y4, 1, 0, 0, True, "10")
    dMOD1 = jnp.concatenate([dmod_10, dmod_11, zrow(dgate_1), dmod_12], 1).reshape(G, 9 * D)

    dX2, dmod_02, dgain_02, dwg01, dwu01, dwd01 = ffn_back(X2, dX3, Y3, 0, 1, 6, True, "01")
    DY0, dATT, dPOOL, dgate_0 = proj_res_bwd(dX2, YM0, MOD[0], [W_ABOUT[:na], W_ABOUT[na:]], dm5, dm5.nt, "ab_out_proj_bwd")
    d_about = jnp.concatenate([atb(ATT, DY0, T, dm, "ab_dwout_a"), atb(POOL, DY0, T, dm, "ab_dwout_p")], 0)
    dUx, dpw_x, dps_x = pool_bwd(P0, dPOOL[:Tx], pw, pwT, pool_scale, dm, True, "pool_bwd_x")
    dUh, dpw_h, dps_h = pool_bwd(P0, dPOOL[Tx:], pw, pwT, pool_scale, dm, False, "pool_bwd_h")
    dQh, dKh0, dVh0 = attn_bwd(QKV, dATT[Tx:], dm, False, None, "attn_bwd_h")
    dQx, dKh, dVh, dKx, dVx = attn_bwd(QKV, dATT[:Tx], dm, True, (dKh0, dVh0), "attn_bwd_x")
    dQKV = jnp.concatenate([jnp.concatenate([dQx, dQh], 0), jnp.concatenate([dKx, dKh], 0), jnp.concatenate([dVx, dVh], 0)], 1)
    dPqkv, d_qkg = qkv_prep_bwd(P0, dQKV, qkg, tabs, dmr, "qkv_prep_bwd")
    dP0 = jnp.concatenate([dPqkv, jnp.concatenate([dUx, dUh], 0)], 1).astype(BF16)
    d_abin = atb(XN0, dP0, T, dm, "ab_dwin")
    dX1, dmod_01, dgain_01 = mixin_bwd(dP0, W_ABIN, X1, dX2, MOD[0], gains[0, 1], 3, dm5, "ab_in_proj_bwd")
    dX0, dmod_00, dgain_00, dwg00, dwu00, dwd00 = ffn_back(X0, dX1, Y1, 0, 0, 0, True, "00")
    dMOD0 = jnp.concatenate([dmod_00, dmod_01, dgate_0, dmod_02], 1).reshape(G, 9 * D)
    grad_x = dX0[:Tx].reshape(Bl, N, D)

    d_ng = jnp.concatenate([dgain_00, dgain_01, dgain_02, dgain_10, dgain_11, dgain_12], 0)
    nf = ffn_wg.shape[3]
    parts = [jnp.concatenate([dwg00, dwg01, dwg10, dwg11], 1), jnp.concatenate([dwu00, dwu01, dwu10, dwu11], 1),
             jnp.concatenate([dwd00, dwd01, dwd10, dwd11], 1),
             _cols_split(d_abin), d_about.reshape(N_DEV, -1, D), _cols_split(d_gin), d_gout.reshape(N_DEV, -1, D),
             _cols_split(d_conv), _cols_split(d_ng)]
    gs_wg, gs_wu, gs_wd, gs_abin, gs_about, gs_gin, gs_gout, gs_conv, gs_ng = scatter_blocks(parts, "scatter_grads")

    d_alog = dPAR[0, :32].reshape(2, 16)[:, :8].reshape(1, 16)
    d_dtb = dPAR[1, :32].reshape(2, 16)[:, :8].reshape(1, 16)
    small = [d_qkg[0], d_qkg[1], (dpw_x + dpw_h).reshape(-1, HEAD), dps_x + dps_h, d_alog, d_dtb, d_onorm,
             jnp.stack([dMOD0, dMOD1])]
    gs_qn, gs_kn, gs_pw, gs_ps, gs_alog, gs_dtb, gs_on, g_dm = _gather_flat(small, F32, "gather_small_grads")
    dmx = g_dm[:, :, :Bl].transpose(1, 0, 2, 3).reshape(2, nb, 9 * D)
    dmx = jnp.concatenate([dmx, jnp.zeros((2, R - nb, 9 * D), F32)], 1)
    dmh = g_dm[:, :, Bl].transpose(1, 0, 2)
    cols_of_me = lambda a: lax.dynamic_slice_in_dim(a, me * nl, nl, axis=2)
    d_wmod, dcc = adaln_bwd(c_ext, c_ctx[None], w_mod, cols_of_me(dmx), cols_of_me(dmh), nb, "adaln_bwd")
    d_bmod = bmod_grad(dmx, dmh, "bmod_grad")
    (gs_cc,) = _gather_flat([dcc], F32, "gather_cctx_grad")

    def upd(gs, w, m, v, shape2, name):
        outs = adamw(gs.reshape((gs.shape[0],) + shape2), w.reshape(shape2), m.reshape(shape2), v.reshape(shape2), "adamw_" + name)
        return [o.reshape(w.shape) for o in outs]

    res = [
        upd(gs_cc, c_ctx, m_c_ctx, v_c_ctx, (1, D), "c_ctx"),
        upd(d_wmod[None], w_mod, m_w_mod, v_w_mod, (2 * D, nl), "w_mod"),
        upd(d_bmod[None], b_mod, m_b_mod, v_b_mod, (2, 9 * D), "b_mod"),
        upd(gs_ng, norm_g, m_norm_g, v_norm_g, (6, D // N_DEV), "norm_g"),
        upd(gs_wg, ffn_wg, m_ffn_wg, v_ffn_wg, (4 * D, nf), "ffn_wg"),
        upd(gs_wu, ffn_wu, m_ffn_wu, v_ffn_wu, (4 * D, nf), "ffn_wu"),
        upd(gs_wd, ffn_wd, m_ffn_wd, v_ffn_wd, (4 * nf, D), "ffn_wd"),
        upd(gs_abin, ab_w_in, m_ab_w_in, v_ab_w_in, (D, ab_w_in.shape[2]), "ab_w_in"),
        upd(gs_qn, ab_q_norm, m_ab_q_norm, v_ab_q_norm, (1, HEAD), "ab_q_norm"),
        upd(gs_kn, ab_k_norm, m_ab_k_norm, v_ab_k_norm, (1, HEAD), "ab_k_norm"),
        upd(gs_pw, pool_w, m_pool_w, v_pool_w, (len(POOL_WINDOWS) * HEAD, HEAD), "pool_w"),
        upd(gs_ps, pool_scale, m_pool_scale, v_pool_scale, (1, len(POOL_WINDOWS) * HEAD), "pool_scale"),
        upd(gs_about, ab_w_out, m_ab_w_out, v_ab_w_out, (ab_w_out.shape[1], D), "ab_w_out"),
        upd(gs_gin, gdn_w_in, m_gdn_w_in, v_gdn_w_in, (D, gdn_w_in.shape[2]), "gdn_w_in"),
        upd(gs_conv, gdn_conv_w, m_gdn_conv_w, v_gdn_conv_w, (3, gdn_conv_w.shape[2]), "gdn_conv_w"),
        upd(gs_alog, gdn_a_log, m_gdn_a_log, v_gdn_a_log, (1, 16), "gdn_a_log"),
        upd(gs_dtb, gdn_dt_bias, m_gdn_dt_bias, v_gdn_dt_bias, (1, 16), "gdn_dt_bias"),
        upd(gs_on, gdn_o_norm, m_gdn_o_norm, v_gdn_o_norm, (1, HEAD), "gdn_o_norm"),
        upd(gs_gout, gdn_w_out, m_gdn_w_out, v_gdn_w_out, (gdn_w_out.shape[1], D), "gdn_w_out"),
    ]
    return (loss, grad_x, *[r[0] for r in res], *[r[1] for r in res], *[r[2] for r in res], *[r[3] for r in res])
```

```python
import functools
import math
from typing import NamedTuple

import jax
import jax.numpy as jnp
from jax import lax
from jax.experimental import pallas as pl
from jax.experimental.pallas import tpu as pltpu

F32, BF16 = jnp.float32, jnp.bfloat16
EPS = 1e-6
HEAD = 128
CHUNK = 64
GRID_W = 64
ROPE_THETA = 10000.0
POOL_WINDOWS = (2, 4, 8, 16)
A_HEADS, A_KV = 4, 2
C_HEADS = 8
N_DEV = 8
AXES = ("x", "y", "c")
ADAM_LR, ADAM_B1, ADAM_B2, ADAM_EPS, ADAM_WD, ADAM_STEP = 0.001, 0.9, 0.999, 1e-08, 0.01, 10
LANE = 128
VMEM_LIMIT = 56 * 1024 * 1024
HI = lax.Precision.HIGHEST
NEG = -1e30


def _cp():
    return pltpu.CompilerParams(vmem_limit_bytes=VMEM_LIMIT)


def _sds(shape, dtype):
    return jax.ShapeDtypeStruct(tuple(shape), dtype)


def _dot(a, b):
    return jnp.dot(a.astype(BF16), b.astype(BF16), preferred_element_type=F32)


def _dot_nt(a, b):
    return lax.dot_general(a.astype(BF16), b.astype(BF16), (((1,), (1,)), ((), ())), preferred_element_type=F32)


def _dot_tn(a, b):
    return lax.dot_general(a.astype(BF16), b.astype(BF16), (((0,), (0,)), ((), ())), preferred_element_type=F32)


def _dot3(a, b, dims):
    ah, bh = a.astype(BF16), b.astype(BF16)
    al, bl = (a - ah.astype(F32)).astype(BF16), (b - bh.astype(F32)).astype(BF16)
    f = lambda x, y: lax.dot_general(x, y, (dims, ((), ())), preferred_element_type=F32)
    return f(ah, bh) + (f(ah, bl) + f(al, bh))


def _hdot(a, b):
    return _dot3(a, b, ((1,), (0,)))


def _hdot_nt(a, b):
    return _dot3(a, b, ((1,), (1,)))


def _hdot_tn(a, b):
    return _dot3(a, b, ((0,), (0,)))


def _pick(n, cap):
    if n <= cap:
        return n
    best = None
    for t in range(LANE, cap + 1, LANE):
        if n % t == 0:
            best = t
    assert best is not None, (n, cap)
    return best


class Dims(NamedTuple):
    Bl: int
    N: int
    M: int
    D: int
    F: int
    tm: int = 0

    @property
    def TM(self):
        return self.tm if self.tm else min(256, self.M)

    def with_tile(self, cap):
        return self._replace(tm=max(t for t in (1024, 512, 256, 128) if t <= cap and self.N % t == 0 and self.Th % t == 0))

    @property
    def Tx(self):
        return self.Bl * self.N

    @property
    def Th(self):
        return self.Bl * self.M

    @property
    def T(self):
        return self.Tx + self.Th

    @property
    def ntx(self):
        return self.Tx // self.TM

    @property
    def nt(self):
        return self.T // self.TM

    @property
    def tps(self):
        return self.N // self.TM

    @property
    def G(self):
        return self.Bl + 1


def _grp(i, dm, tm=None):
    tm = dm.TM if tm is None else tm
    return jnp.where(i < dm.Tx // tm, i // (dm.N // tm), dm.Bl)


def _first_of_group(i, dm, tm=None):
    tm = dm.TM if tm is None else tm
    return jnp.where(i < dm.Tx // tm, i % (dm.N // tm) == 0, i == dm.Tx // tm)


def _contraction_tile(rows):
    return max(t for t in (1024, 512, 256, 128) if rows % t == 0)


def _ffn_tile(dm):
    return max(t for t in (512, 256, 128) if dm.N % t == 0 and dm.Th % t == 0)


def _acc(ref, idx, val, first):
    @pl.when(first)
    def _():
        ref[idx] = val

    @pl.when(jnp.logical_not(first))
    def _():
        ref[idx] += val


def _modulate(x, gain, shift, scale):
    y = x * lax.rsqrt(jnp.mean(x * x, axis=-1, keepdims=True) + EPS)
    return (y * gain) * (1.0 + scale) + shift


def _silu(x):
    return x * jax.nn.sigmoid(x)


@functools.partial(jax.custom_vjp, nondiff_argnums=(1,))
def _shift_rows(a, k):
    n = a.shape[0]
    if k == 0:
        return a
    r = lax.broadcasted_iota(jnp.int32, a.shape, 0)
    rolled = pltpu.roll(a, (-k) % n, 0)
    ok = (r + k >= 0) & (r + k < n)
    return jnp.where(ok, rolled, 0.0)


def _shift_rows_fwd(a, k):
    return _shift_rows(a, k), None


def _shift_rows_bwd(k, _, d):
    return (_shift_rows(d, -k),)


_shift_rows.defvjp(_shift_rows_fwd, _shift_rows_bwd)


@functools.partial(jax.custom_vjp, nondiff_argnums=(1,))
def _roll_lanes(a, s):
    return pltpu.roll(a, s % LANE, 1)


def _roll_lanes_fwd(a, s):
    return _roll_lanes(a, s), None


def _roll_lanes_bwd(s, _, d):
    return (_roll_lanes(d, -s),)


_roll_lanes.defvjp(_roll_lanes_fwd, _roll_lanes_bwd)


def _rope(t, cs, sneg, spos):
    return t * cs + _roll_lanes(t, 96) * sneg + _roll_lanes(t, 32) * spos


def _rope_tables(dm):
    rows = dm.N // GRID_W
    row = jnp.repeat(jnp.arange(rows), GRID_W).astype(F32)
    col = jnp.tile(jnp.arange(GRID_W), rows).astype(F32)
    half = HEAD // 2
    inv_freq = jnp.power(ROPE_THETA, -jnp.arange(0, half, 2, dtype=F32) / half)
    ar, ac = row[:, None] * inv_freq, col[:, None] * inv_freq
    cs = jnp.concatenate([jnp.cos(ar), jnp.cos(ar), jnp.cos(ac), jnp.cos(ac)], axis=1)
    z = jnp.zeros_like(ar)
    sneg = jnp.concatenate([-jnp.sin(ar), z, -jnp.sin(ac), z], axis=1)
    spos = jnp.concatenate([z, jnp.sin(ar), z, jnp.sin(ac)], axis=1)
    pad1 = jnp.ones((dm.TM, HEAD), F32)
    pad0 = jnp.zeros((dm.TM, HEAD), F32)
    return (jnp.concatenate([cs, pad1], 0), jnp.concatenate([sneg, pad0], 0), jnp.concatenate([spos, pad0], 0))


def all_gather(xs, name):
    n = len(xs)

    def body(*refs):
        x_refs, out_refs = refs[:n], refs[n:2 * n]
        send_sems, recv_sems, local_sems = refs[2 * n:]
        x, y, c = lax.axis_index("x"), lax.axis_index("y"), lax.axis_index("c")
        me, sibling = (x, y, c), (x, y, 1 - c)
        chips = [(1 - x, y), (x, 1 - y), (1 - x, 1 - y)]

        def slot(a, px, py, pc):
            return out_refs[a].at[4 * px + 2 * py + pc]

        def copy(a, k, block, to, src=None):
            return pltpu.make_async_remote_copy(
                src_ref=slot(a, *block) if src is None else src, dst_ref=slot(a, *block),
                send_sem=send_sems.at[7 * a + k], recv_sem=recv_sems.at[7 * a + k],
                device_id=to, device_id_type=pl.DeviceIdType.MESH)

        mine = [pltpu.make_async_copy(x_refs[a], slot(a, *me), local_sems.at[a]) for a in range(n)]
        for cp in mine:
            cp.start()
        first = []
        for a in range(n):
            first.append(copy(a, 0, me, sibling, src=x_refs[a]))
            first += [copy(a, 1 + j, me, (*chip, c), src=x_refs[a]) for j, chip in enumerate(chips)]
        for cp in first:
            cp.start()
        passed = []
        for j, chip in enumerate(chips):
            for a in range(n):
                copy(a, 1 + j, (*chip, c), me).wait_recv()
                cp = copy(a, 4 + j, (*chip, c), sibling)
                cp.start()
                passed.append(cp)
        for a in range(n):
            copy(a, 0, sibling, me).wait_recv()
            for j, chip in enumerate(chips):
                copy(a, 4 + j, (*chip, 1 - c), me).wait_recv()
        for cp in first + passed:
            cp.wait_send()
        for cp in mine:
            cp.wait()

    anyspec = pl.BlockSpec(memory_space=pl.ANY)
    return pl.pallas_call(
        body, name=name, out_shape=[_sds((N_DEV,) + a.shape, a.dtype) for a in xs],
        in_specs=[anyspec] * n, out_specs=[anyspec] * n,
        scratch_shapes=[pltpu.SemaphoreType.DMA((7 * n,)), pltpu.SemaphoreType.DMA((7 * n,)),
                        pltpu.SemaphoreType.DMA((n,))],
    )(*xs)


N_CHIP = N_DEV // 2


def scatter_sibling(xs, name):
    n = len(xs)

    def body(*refs):
        x_refs, own_refs, got_refs = refs[:n], refs[n:2 * n], refs[2 * n:3 * n]
        send_sems, recv_sems, local_sems = refs[3 * n:]
        x, y, c = lax.axis_index("x"), lax.axis_index("y"), lax.axis_index("c")
        sibling = (x, y, 1 - c)
        local, remote = [], []
        for a in range(n):
            for q in range(N_CHIP):
                local.append(pltpu.make_async_copy(x_refs[a].at[2 * q + c], own_refs[a].at[q], local_sems.at[N_CHIP * a + q]))
                remote.append(pltpu.make_async_remote_copy(
                    src_ref=x_refs[a].at[2 * q + (1 - c)], dst_ref=got_refs[a].at[q],
                    send_sem=send_sems.at[N_CHIP * a + q], recv_sem=recv_sems.at[N_CHIP * a + q],
                    device_id=sibling, device_id_type=pl.DeviceIdType.MESH))
        for cp in remote + local:
            cp.start()
        for cp in remote:
            cp.wait_send()
            cp.wait_recv()
        for cp in local:
            cp.wait()

    anyspec = pl.BlockSpec(memory_space=pl.ANY)
    half = [_sds((N_CHIP,) + a.shape[1:], a.dtype) for a in xs]
    outs = pl.pallas_call(
        body, name=name, out_shape=half + half, in_specs=[anyspec] * n, out_specs=[anyspec] * (2 * n),
        scratch_shapes=[pltpu.SemaphoreType.DMA((N_CHIP * n,)), pltpu.SemaphoreType.DMA((N_CHIP * n,)),
                        pltpu.SemaphoreType.DMA((N_CHIP * n,))],
    )(*xs)
    return outs[:n], outs[n:]


def pair_add(a, b, name):
    Q, R, C = a.shape
    cap = max(8, (1 << 19) // C)
    tr = R if R <= cap else max(t for t in range(8, cap + 1, 8) if R % t == 0)

    def body(a_ref, b_ref, o_ref):
        o_ref[...] = (a_ref[...].astype(F32) + b_ref[...].astype(F32)).astype(o_ref.dtype)

    blk = pl.BlockSpec((1, tr, C), lambda q, i: (q, i, 0))
    return pl.pallas_call(body, name=name, grid=(Q, R // tr), in_specs=[blk, blk], out_specs=blk,
                          out_shape=_sds(a.shape, a.dtype), compiler_params=_cp())(a, b)


def scatter_chips(xs, name):
    n = len(xs)
    flips = [(0, 1), (1, 0), (1, 1)]

    def body(*refs):
        x_refs, out_refs = refs[:n], refs[n:2 * n]
        send_sems, recv_sems, local_sems = refs[2 * n:]
        x, y, c = lax.axis_index("x"), lax.axis_index("y"), lax.axis_index("c")
        chip = 2 * x + y
        mine = [pltpu.make_async_copy(x_refs[a].at[chip], out_refs[a].at[chip], local_sems.at[a]) for a in range(n)]
        copies = []
        for k, (fx, fy) in enumerate(flips):
            px, py = (1 - x if fx else x), (1 - y if fy else y)
            for a in range(n):
                copies.append(pltpu.make_async_remote_copy(
                    src_ref=x_refs[a].at[2 * px + py], dst_ref=out_refs[a].at[chip],
                    send_sem=send_sems.at[3 * a + k], recv_sem=recv_sems.at[3 * a + k],
                    device_id=(px, py, c), device_id_type=pl.DeviceIdType.MESH))
        for cp in copies + mine:
            cp.start()
        for cp in copies:
            cp.wait_send()
            cp.wait_recv()
        for cp in mine:
            cp.wait()

    anyspec = pl.BlockSpec(memory_space=pl.ANY)
    return pl.pallas_call(
        body, name=name, out_shape=[_sds(a.shape, a.dtype) for a in xs],
        in_specs=[anyspec] * n, out_specs=[anyspec] * n,
        scratch_shapes=[pltpu.SemaphoreType.DMA((3 * n,)), pltpu.SemaphoreType.DMA((3 * n,)),
                        pltpu.SemaphoreType.DMA((n,))],
    )(*xs)


def _mod_spec(dm, nidx, tm=None):
    if nidx == 1:
        return pl.BlockSpec((1, 9, dm.D), lambda i: (_grp(i, dm, tm), 0, 0))
    return pl.BlockSpec((1, 9, dm.D), lambda i, k: (_grp(i, dm, tm), 0, 0))


def _wspec(shape5, l, s, ks):
    return pl.BlockSpec((ks, 1, 1) + tuple(shape5[3:]), lambda i, k: (k, l, s, 0, 0))


FFN_FWD_SHARDS = 4
FFN_BWD_SHARDS = 2


def ffn_fwd(X, MOD, gain, gwg, gwu, gwd, l, s, s0, dm, all_rows, name):
    D = dm.D
    tm = _ffn_tile(dm)
    rows = dm.T if all_rows else dm.Tx
    ks = FFN_FWD_SHARDS
    nk = N_DEV // ks

    def body(x_ref, m_ref, g_ref, wg_ref, wu_ref, wd_ref, xo_ref, y_ref, xn_s, acc_s):
        k = pl.program_id(1)

        @pl.when(k == 0)
        def _():
            m = m_ref[0]
            xn = _modulate(x_ref[...], g_ref[...], m[s0:s0 + 1], m[s0 + 1:s0 + 2])
            xn_s[...] = xn.astype(BF16)
            acc_s[...] = jnp.zeros_like(acc_s)

        xn = xn_s[...]
        y = None
        for j in range(ks):
            g = jnp.dot(xn, wg_ref[j, 0, 0], preferred_element_type=F32)
            u = jnp.dot(xn, wu_ref[j, 0, 0], preferred_element_type=F32)
            yj = jnp.dot((_silu(g) * u).astype(BF16), wd_ref[j, 0, 0], preferred_element_type=F32)
            y = yj if y is None else y + yj
        acc_s[...] += y

        @pl.when(k == nk - 1)
        def _():
            m = m_ref[0]
            y = acc_s[...]
            y_ref[...] = y
            xo_ref[...] = x_ref[...] + (0.5 * m[s0 + 2:s0 + 3]) * y

    row = pl.BlockSpec((tm, D), lambda i, k: (i, 0))
    return pl.pallas_call(
        body, name=name, grid=(rows // tm, nk),
        in_specs=[row, _mod_spec(dm, 2, tm), pl.BlockSpec((1, D), lambda i, k: (0, 0)),
                  _wspec(gwg.shape, l, s, ks), _wspec(gwu.shape, l, s, ks), _wspec(gwd.shape, l, s, ks)],
        out_specs=[row, row],
        out_shape=[_sds((rows, D), F32), _sds((rows, D), F32)],
        scratch_shapes=[pltpu.VMEM((tm, D), BF16), pltpu.VMEM((tm, D), F32)],
        compiler_params=_cp(),
    )(X, MOD, gain, gwg, gwu, gwd)


def ffn_bwd(X, dXo, Y, MOD, gain, gwg, gwu, gwd, l, s, s0, dm, all_rows, name):
    D = dm.D
    tm = _ffn_tile(dm)
    rows = dm.T if all_rows else dm.Tx
    ks = FFN_BWD_SHARDS
    nk = N_DEV // ks
    nf = gwg.shape[4]
    ngr = dm.G if all_rows else dm.Bl

    def body(x_ref, dxo_ref, y_ref, m_ref, g_ref, wg_ref, wu_ref, wd_ref,
             dxi_ref, xn_ref, do_ref, h_ref, dg_ref, du_ref, dm_ref, dgain_ref, xn_s, do_s, dxn_s):
        i, k = pl.program_id(0), pl.program_id(1)

        @pl.when(k == 0)
        def _():
            m = m_ref[0]
            xn = _modulate(x_ref[...], g_ref[...], m[s0:s0 + 1], m[s0 + 1:s0 + 2])
            xn_s[...] = xn.astype(BF16)
            do_s[...] = ((0.5 * m[s0 + 2:s0 + 3]) * dxo_ref[...]).astype(BF16)
            dxn_s[...] = jnp.zeros_like(dxn_s)

        xn, do = xn_s[...], do_s[...]
        dxn = None
        for j in range(ks):
            wg, wu = wg_ref[j, 0, 0], wu_ref[j, 0, 0]
            g = jnp.dot(xn, wg, preferred_element_type=F32)
            u = jnp.dot(xn, wu, preferred_element_type=F32)
            sg = jax.nn.sigmoid(g)
            si = g * sg
            dh = _dot_nt(do, wd_ref[j, 0, 0])
            dg = (dh * u * (sg * (1.0 + g * (1.0 - sg)))).astype(BF16)
            du = (dh * si).astype(BF16)
            dj = _dot_nt(dg, wg) + _dot_nt(du, wu)
            dxn = dj if dxn is None else dxn + dj
            h_ref[j] = (si * u).astype(BF16)
            dg_ref[j] = dg
            du_ref[j] = du
        dxn_s[...] += dxn

        @pl.when(k == nk - 1)
        def _():
            m = m_ref[0]
            _, vjp = jax.vjp(_modulate, x_ref[...], g_ref[...], m[s0:s0 + 1], m[s0 + 1:s0 + 2])
            dx, dgain, dshift, dscale = vjp(dxn_s[...])
            dxo = dxo_ref[...]
            dxi_ref[...] = dxo + dx
            xn_ref[...] = xn_s[...]
            do_ref[...] = do_s[...]
            dgate = jnp.sum(0.5 * dxo * y_ref[...], axis=0, keepdims=True)
            first = _first_of_group(i, dm, tm)
            _acc(dm_ref, (0, pl.ds(0, 1), slice(None)), dshift, first)
            _acc(dm_ref, (0, pl.ds(1, 1), slice(None)), dscale, first)
            _acc(dm_ref, (0, pl.ds(2, 1), slice(None)), dgate, first)
            _acc(dgain_ref, (slice(None), slice(None)), dgain, i == 0)

    row = pl.BlockSpec((tm, D), lambda i, k: (i, 0))
    slab = pl.BlockSpec((ks, tm, nf), lambda i, k: (k, i, 0))
    return pl.pallas_call(
        body, name=name, grid=(rows // tm, nk),
        in_specs=[row, row, row, _mod_spec(dm, 2, tm), pl.BlockSpec((1, D), lambda i, k: (0, 0)),
                  _wspec(gwg.shape, l, s, ks), _wspec(gwu.shape, l, s, ks), _wspec(gwd.shape, l, s, ks)],
        out_specs=[row, row, row, slab, slab, slab,
                   pl.BlockSpec((1, 3, D), lambda i, k: (_grp(i, dm, tm), 0, 0)),
                   pl.BlockSpec((1, D), lambda i, k: (0, 0))],
        out_shape=[_sds((rows, D), F32), _sds((rows, D), BF16), _sds((rows, D), BF16),
                   _sds((N_DEV, rows, nf), BF16), _sds((N_DEV, rows, nf), BF16), _sds((N_DEV, rows, nf), BF16),
                   _sds((ngr, 3, D), F32), _sds((1, D), F32)],
        scratch_shapes=[pltpu.VMEM((tm, D), BF16), pltpu.VMEM((tm, D), BF16), pltpu.VMEM((tm, D), F32)],
        compiler_params=_cp(),
    )(X, dXo, Y, MOD, gain, gwg, gwu, gwd)


def ffn_dw(XN, DO, H, DG, DU, dm, name):
    rows, D = XN.shape
    nf = H.shape[2]
    tt = _contraction_tile(rows)
    nT = rows // tt

    def body(xn_ref, do_ref, h_ref, dg_ref, du_ref, dwg_ref, dwu_ref, dwd_ref, ag_s, au_s, ad_s):
        t = pl.program_id(1)

        @pl.when(t == 0)
        def _():
            ag_s[...] = jnp.zeros_like(ag_s)
            au_s[...] = jnp.zeros_like(au_s)
            ad_s[...] = jnp.zeros_like(ad_s)

        xn = xn_ref[...]
        ag_s[...] += _dot_tn(xn, dg_ref[0])
        au_s[...] += _dot_tn(xn, du_ref[0])
        ad_s[...] += _dot_tn(h_ref[0], do_ref[...])

        @pl.when(t == nT - 1)
        def _():
            dwg_ref[0] = ag_s[...].astype(BF16)
            dwu_ref[0] = au_s[...].astype(BF16)
            dwd_ref[0] = ad_s[...].astype(BF16)

    row = pl.BlockSpec((tt, D), lambda k, t: (t, 0))
    slab = pl.BlockSpec((1, tt, nf), lambda k, t: (k, t, 0))
    return pl.pallas_call(
        body, name=name, grid=(N_DEV, nT),
        in_specs=[row, row, slab, slab, slab],
        out_specs=[pl.BlockSpec((1, D, nf), lambda k, t: (k, 0, 0)), pl.BlockSpec((1, D, nf), lambda k, t: (k, 0, 0)),
                   pl.BlockSpec((1, nf, D), lambda k, t: (k, 0, 0))],
        out_shape=[_sds((N_DEV, D, nf), BF16), _sds((N_DEV, D, nf), BF16), _sds((N_DEV, nf, D), BF16)],
        scratch_shapes=[pltpu.VMEM((D, nf), F32), pltpu.VMEM((D, nf), F32), pltpu.VMEM((nf, D), F32)],
        compiler_params=_cp(),
    )(XN, DO, H, DG, DU)


def atb(A, B, rows, dm, name):
    Ka, Nb = A.shape[1], B.shape[1]
    tk, tn = _pick(Ka, 1024), _pick(Nb, 1536)
    tt = _contraction_tile(rows)
    nT = rows // tt

    def body(a_ref, b_ref, o_ref, acc_s):
        t = pl.program_id(2)

        @pl.when(t == 0)
        def _():
            acc_s[...] = jnp.zeros_like(acc_s)

        acc_s[...] += _dot_tn(a_ref[...], b_ref[...])

        @pl.when(t == nT - 1)
        def _():
            o_ref[...] = acc_s[...].astype(BF16)

    return pl.pallas_call(
        body, name=name, grid=(Ka // tk, Nb // tn, nT),
        in_specs=[pl.BlockSpec((tt, tk), lambda i, j, t: (t, i)), pl.BlockSpec((tt, tn), lambda i, j, t: (t, j))],
        out_specs=pl.BlockSpec((tk, tn), lambda i, j, t: (i, j)),
        out_shape=_sds((Ka, Nb), BF16), scratch_shapes=[pltpu.VMEM((tk, tn), F32)], compiler_params=_cp(),
    )(A, B)


def modmm(X, MOD, gain, W, s0, dm, name):
    TM, D = dm.TM, dm.D
    Nc = W.shape[1]
    tn = _pick(Nc, 1536)
    nj = Nc // tn

    def body(x_ref, m_ref, g_ref, w_ref, p_ref, xn_ref):
        @pl.when(pl.program_id(1) == 0)
        def _():
            m = m_ref[0]
            xn_ref[...] = _modulate(x_ref[...], g_ref[...], m[s0:s0 + 1], m[s0 + 1:s0 + 2]).astype(BF16)

        p_ref[...] = jnp.dot(xn_ref[...], w_ref[...], preferred_element_type=F32)

    row = pl.BlockSpec((TM, D), lambda i, j: (i, 0))
    return pl.pallas_call(
        body, name=name, grid=(dm.nt, nj),
        in_specs=[row, _mod_spec(dm, 2), pl.BlockSpec((1, D), lambda i, j: (0, 0)),
                  pl.BlockSpec((D, tn), lambda i, j: (0, j))],
        out_specs=[pl.BlockSpec((TM, tn), lambda i, j: (i, j)), row],
        out_shape=[_sds((dm.T, Nc), F32), _sds((dm.T, D), BF16)],
        compiler_params=_cp(),
    )(X, MOD, gain, W)


def mixin_bwd(dP, W, X, dXres, MOD, gain, s0, dm, name):
    TM, D = dm.TM, dm.D
    K = dP.shape[1]

    def body(dp_ref, w_ref, x_ref, dr_ref, m_ref, g_ref, dx_ref, dm_ref, dgain_ref):
        i = pl.program_id(0)
        dxn = _dot_nt(dp_ref[...], w_ref[...])
        m = m_ref[0]
        _, vjp = jax.vjp(_modulate, x_ref[...], g_ref[...], m[s0:s0 + 1], m[s0 + 1:s0 + 2])
        dx, dgain, dshift, dscale = vjp(dxn)
        dx_ref[...] = dr_ref[...] + dx
        first = _first_of_group(i, dm)
        _acc(dm_ref, (0, pl.ds(0, 1), slice(None)), dshift, first)
        _acc(dm_ref, (0, pl.ds(1, 1), slice(None)), dscale, first)
        _acc(dgain_ref, (slice(None), slice(None)), dgain, i == 0)

    row = pl.BlockSpec((TM, D), lambda i: (i, 0))
    return pl.pallas_call(
        body, name=name, grid=(dm.nt,),
        in_specs=[pl.BlockSpec((TM, K), lambda i: (i, 0)), pl.BlockSpec((D, K), lambda i: (0, 0)), row, row,
                  _mod_spec(dm, 1), pl.BlockSpec((1, D), lambda i: (0, 0))],
        out_specs=[row, pl.BlockSpec((1, 2, D), lambda i: (_grp(i, dm), 0, 0)), pl.BlockSpec((1, D), lambda i: (0, 0))],
        out_shape=[_sds((dm.T, D), F32), _sds((dm.G, 2, D), F32), _sds((1, D), F32)],
        compiler_params=_cp(),
    )(dP, W, X, dXres, MOD, gain)


def proj_res(As, Ws, X, MOD, dm, ntiles, name):
    TM, D = dm.TM, dm.D
    n = len(As)
    rows = ntiles * TM

    def body(*refs):
        a_refs, w_refs = refs[:n], refs[n:2 * n]
        x_ref, m_ref, xo_ref, y_ref = refs[2 * n:]
        y = jnp.dot(a_refs[0][...], w_refs[0][...], preferred_element_type=F32)
        for a, w in zip(a_refs[1:], w_refs[1:]):
            y += jnp.dot(a[...], w[...], preferred_element_type=F32)
        y_ref[...] = y
        xo_ref[...] = x_ref[...] + m_ref[0][5:6] * y

    row = pl.BlockSpec((TM, D), lambda i: (i, 0))
    return pl.pallas_call(
        body, name=name, grid=(ntiles,),
        in_specs=[pl.BlockSpec((TM, a.shape[1]), lambda i: (i, 0)) for a in As]
        + [pl.BlockSpec(w.shape, lambda i: (0, 0)) for w in Ws] + [row, _mod_spec(dm, 1)],
        out_specs=[row, row], out_shape=[_sds((rows, D), F32), _sds((rows, D), F32)],
        compiler_params=_cp(),
    )(*As, *Ws, X, MOD)


def proj_res_bwd(dXo, Y, MOD, Ws, dm, ntiles, name):
    TM, D = dm.TM, dm.D
    n = len(Ws)
    rows = ntiles * TM
    ngr = dm.G if ntiles == dm.nt else dm.Bl

    def body(*refs):
        dxo_ref, y_ref, m_ref = refs[:3]
        w_refs = refs[3:3 + n]
        dy_ref = refs[3 + n]
        da_refs = refs[4 + n:4 + 2 * n]
        dgate_ref = refs[4 + 2 * n]
        i = pl.program_id(0)
        dxo = dxo_ref[...]
        dy = (m_ref[0][5:6] * dxo).astype(BF16)
        dy_ref[...] = dy
        for w, da in zip(w_refs, da_refs):
            da[...] = _dot_nt(dy, w[...])
        dgate = jnp.sum(dxo * y_ref[...], axis=0, keepdims=True)
        _acc(dgate_ref, (0, slice(None), slice(None)), dgate, _first_of_group(i, dm))

    row = pl.BlockSpec((TM, D), lambda i: (i, 0))
    return pl.pallas_call(
        body, name=name, grid=(ntiles,),
        in_specs=[row, row, _mod_spec(dm, 1)] + [pl.BlockSpec(w.shape, lambda i: (0, 0)) for w in Ws],
        out_specs=[row] + [pl.BlockSpec((TM, w.shape[0]), lambda i: (i, 0)) for w in Ws]
        + [pl.BlockSpec((1, 1, D), lambda i: (_grp(i, dm), 0, 0))],
        out_shape=[_sds((rows, D), BF16)] + [_sds((rows, w.shape[0]), F32) for w in Ws] + [_sds((ngr, 1, D), F32)],
        compiler_params=_cp(),
    )(dXo, Y, MOD, *Ws)


def loss_head(Xf, target, dm, name):
    TM, D = dm.TM, dm.D

    def body(x_ref, t_ref, l_ref, dx_ref, acc_s):
        i = pl.program_id(0)
        e = x_ref[...] - t_ref[...]
        dx_ref[...] = e * (1.0 / D)

        @pl.when(i == 0)
        def _():
            acc_s[...] = jnp.zeros_like(acc_s)

        acc_s[...] += jnp.sum(e * e, axis=0, keepdims=True)

        @pl.when(i == dm.ntx - 1)
        def _():
            tot = jnp.sum(acc_s[...], axis=1, keepdims=True) * (0.5 / D)
            l_ref[...] = jnp.broadcast_to(tot, (1, LANE))

    row = pl.BlockSpec((TM, D), lambda i: (i, 0))
    return pl.pallas_call(
        body, name=name, grid=(dm.ntx,), in_specs=[row, row],
        out_specs=[pl.BlockSpec((1, LANE), lambda i: (0, 0)), row],
        out_shape=[_sds((1, LANE), F32), _sds((dm.Tx, D), F32)],
        scratch_shapes=[pltpu.VMEM((1, D), F32)], compiler_params=_cp(),
    )(Xf, target)


def _qk_fn(p, gain, cs, sneg, spos):
    y = p * lax.rsqrt(jnp.mean(p * p, axis=-1, keepdims=True) + EPS) * gain
    return _rope(y, cs, sneg, spos)


def _tab_specs(dm, swap):
    def idx(i):
        return jnp.where(i < dm.ntx, i % dm.tps, dm.tps)
    if swap:
        return [pl.BlockSpec((dm.TM, HEAD), lambda j, i: (idx(i), 0))] * 3
    return [pl.BlockSpec((dm.TM, HEAD), lambda i, j: (idx(i), 0))] * 3


def qkv_prep(P0, qkg, tabs, dm, name):
    TM = dm.TM

    def body(p_ref, g_ref, cs_ref, sn_ref, sp_ref, o_ref):
        j = pl.program_id(1)

        @pl.when(j < 6)
        def _():
            o_ref[...] = _qk_fn(p_ref[...], g_ref[0], cs_ref[...], sn_ref[...], sp_ref[...]).astype(BF16)

        @pl.when(j >= 6)
        def _():
            o_ref[...] = p_ref[...].astype(BF16)

    blk = pl.BlockSpec((TM, HEAD), lambda i, j: (i, j))
    return pl.pallas_call(
        body, name=name, grid=(dm.nt, 8),
        in_specs=[blk, pl.BlockSpec((1, 1, HEAD), lambda i, j: (jnp.minimum(j // 4, 1), 0, 0))] + _tab_specs(dm, False),
        out_specs=blk, out_shape=_sds((dm.T, 8 * HEAD), BF16), compiler_params=_cp(),
    )(P0, qkg, *tabs)


def qkv_prep_bwd(P0, dQKV, qkg, tabs, dm, name):
    TM = dm.TM

    def body(p_ref, d_ref, g_ref, cs_ref, sn_ref, sp_ref, dp_ref, dg_ref):
        j, i = pl.program_id(0), pl.program_id(1)
        first = (i == 0) & ((j == 0) | (j == 4))

        @pl.when(j < 6)
        def _():
            _, vjp = jax.vjp(_qk_fn, p_ref[...], g_ref[0], cs_ref[...], sn_ref[...], sp_ref[...])
            dp, dg = vjp(d_ref[...])[:2]
            dp_ref[...] = dp
            _acc(dg_ref, (0, slice(None), slice(None)), dg, first)

        @pl.when(j >= 6)
        def _():
            dp_ref[...] = d_ref[...]

    blk = pl.BlockSpec((TM, HEAD), lambda j, i: (i, j))
    return pl.pallas_call(
        body, name=name, grid=(8, dm.nt),
        in_specs=[blk, blk, pl.BlockSpec((1, 1, HEAD), lambda j, i: (jnp.minimum(j // 4, 1), 0, 0))] + _tab_specs(dm, True),
        out_specs=[blk, pl.BlockSpec((1, 1, HEAD), lambda j, i: (jnp.minimum(j // 4, 1), 0, 0))],
        out_shape=[_sds((dm.T, 8 * HEAD), F32), _sds((2, 1, HEAD), F32)], compiler_params=_cp(),
    )(P0, dQKV, qkg, *tabs)


def _softmax2(sx, sh):
    m = jnp.max(sh, axis=-1, keepdims=True)
    if sx is not None:
        m = jnp.maximum(m, jnp.max(sx, axis=-1, keepdims=True))
    eh = jnp.exp(sh - m)
    l = jnp.sum(eh, axis=-1, keepdims=True)
    ex = None
    if sx is not None:
        ex = jnp.exp(sx - m)
        l = l + jnp.sum(ex, axis=-1, keepdims=True)
    inv = 1.0 / l
    return (None if ex is None else ex * inv), eh * inv


def _attn_geometry(dm, with_x):
    TQ = dm.TM
    if with_x:
        nq, qoff = dm.N // TQ, 0
    else:
        nq, qoff = dm.M // TQ, dm.Tx // TQ
    hoff = dm.Tx // dm.M
    return TQ, nq, qoff, hoff


def attn_fwd(QKV, dm, with_x, name):
    TQ, nq, qoff, hoff = _attn_geometry(dm, with_x)
    scale = HEAD ** -0.5
    rows = dm.Tx if with_x else dm.Th

    def body(*refs):
        if with_x:
            q_ref, kh_ref, vh_ref, kx_ref, vx_ref, o_ref = refs
        else:
            q_ref, kh_ref, vh_ref, o_ref = refs
        q = q_ref[...]
        sh = _dot_nt(q, kh_ref[...]) * scale
        sx = _dot_nt(q, kx_ref[...]) * scale if with_x else None
        px, ph = _softmax2(sx, sh)
        o = _dot(ph, vh_ref[...])
        if with_x:
            o = o + _dot(px, vx_ref[...])
        o_ref[...] = o.astype(BF16)

    qs = pl.BlockSpec((TQ, HEAD), lambda b, kv, g, qi: (qoff + b * nq + qi, kv * 2 + g))
    in_specs = [qs, pl.BlockSpec((dm.M, HEAD), lambda b, kv, g, qi: (hoff + b, 4 + kv)),
                pl.BlockSpec((dm.M, HEAD), lambda b, kv, g, qi: (hoff + b, 6 + kv))]
    args = [QKV, QKV, QKV]
    if with_x:
        in_specs += [pl.BlockSpec((dm.N, HEAD), lambda b, kv, g, qi: (b, 4 + kv)),
                     pl.BlockSpec((dm.N, HEAD), lambda b, kv, g, qi: (b, 6 + kv))]
        args += [QKV, QKV]
    return pl.pallas_call(
        body, name=name, grid=(dm.Bl, A_KV, 2, nq), in_specs=in_specs,
        out_specs=pl.BlockSpec((TQ, HEAD), lambda b, kv, g, qi: (b * nq + qi, kv * 2 + g)),
        out_shape=_sds((rows, A_HEADS * HEAD), BF16), compiler_params=_cp(),
    )(*args)


def attn_bwd(QKV, dO, dm, with_x, init, name):
    TQ, nq, qoff, hoff = _attn_geometry(dm, with_x)
    scale = HEAD ** -0.5
    rows = dm.Tx if with_x else dm.Th

    def body(*refs):
        if with_x:
            (q_ref, kh_ref, vh_ref, kx_ref, vx_ref, do_ref, ikh_ref, ivh_ref,
             dq_ref, dkh_ref, dvh_ref, dkx_ref, dvx_ref) = refs
        else:
            q_ref, kh_ref, vh_ref, do_ref, dq_ref, dkh_ref, dvh_ref = refs
        g, qi = pl.program_id(2), pl.program_id(3)
        q = q_ref[...]
        kh, vh = kh_ref[...], vh_ref[...]
        sh = _dot_nt(q, kh) * scale
        sx = _dot_nt(q, kx_ref[...]) * scale if with_x else None
        px, ph = _softmax2(sx, sh)
        dob = do_ref[...].astype(BF16)
        dph = _dot_nt(dob, vh)
        delta = jnp.sum(dph * ph, axis=-1, keepdims=True)
        if with_x:
            dpx = _dot_nt(dob, vx_ref[...])
            delta = delta + jnp.sum(dpx * px, axis=-1, keepdims=True)
        dsh = (ph * (dph - delta) * scale).astype(BF16)
        dq = _dot(dsh, kh)
        first = (g == 0) & (qi == 0)

        @pl.when(first)
        def _():
            if with_x:
                dkh_ref[...] = ikh_ref[...]
                dvh_ref[...] = ivh_ref[...]
                dkx_ref[...] = jnp.zeros_like(dkx_ref)
                dvx_ref[...] = jnp.zeros_like(dvx_ref)
            else:
                dkh_ref[...] = jnp.zeros_like(dkh_ref)
                dvh_ref[...] = jnp.zeros_like(dvh_ref)

        dkh_ref[...] += _dot_tn(dsh, q)
        dvh_ref[...] += _dot_tn(ph, dob)
        if with_x:
            dsx = (px * (dpx - delta) * scale).astype(BF16)
            dq = dq + _dot(dsx, kx_ref[...])
            dkx_ref[...] += _dot_tn(dsx, q)
            dvx_ref[...] += _dot_tn(px, dob)
        dq_ref[...] = dq

    qs = pl.BlockSpec((TQ, HEAD), lambda b, kv, g, qi: (qoff + b * nq + qi, kv * 2 + g))
    hs = lambda c0: pl.BlockSpec((dm.M, HEAD), lambda b, kv, g, qi: (hoff + b, c0 + kv))
    xs = lambda c0: pl.BlockSpec((dm.N, HEAD), lambda b, kv, g, qi: (b, c0 + kv))
    dos = pl.BlockSpec((TQ, HEAD), lambda b, kv, g, qi: (b * nq + qi, kv * 2 + g))
    acc_h = pl.BlockSpec((dm.M, HEAD), lambda b, kv, g, qi: (b, kv))
    acc_x = pl.BlockSpec((dm.N, HEAD), lambda b, kv, g, qi: (b, kv))
    in_specs, args = [qs, hs(4), hs(6)], [QKV, QKV, QKV]
    out_specs = [dos, acc_h, acc_h]
    out_shape = [_sds((rows, A_HEADS * HEAD), F32), _sds((dm.Th, A_KV * HEAD), F32), _sds((dm.Th, A_KV * HEAD), F32)]
    if with_x:
        in_specs += [xs(4), xs(6), dos, acc_h, acc_h]
        args += [QKV, QKV, dO, init[0], init[1]]
        out_specs += [acc_x, acc_x]
        out_shape += [_sds((dm.Tx, A_KV * HEAD), F32), _sds((dm.Tx, A_KV * HEAD), F32)]
    else:
        in_specs += [dos]
        args += [dO]
    return pl.pallas_call(
        body, name=name, grid=(dm.Bl, A_KV, 2, nq), in_specs=in_specs, out_specs=out_specs,
        out_shape=out_shape, compiler_params=_cp(),
    )(*args)


def _pool_mean(u, w):
    n = u.shape[0]
    t = lax.broadcasted_iota(jnp.int32, (n, 1), 0)
    cnt = (jnp.clip(t + (w - w // 2), 0, n) - jnp.clip(t - w // 2, 0, n)).astype(F32)
    s = _shift_rows(u, -(w // 2))
    for j in range(-(w // 2) + 1, w - w // 2):
        s = s + _shift_rows(u, j)
    return s / cnt - u


def pool_fwd(P0, pw, pscale, dm, on_x, name):
    n, off, rows = (dm.N, 0, dm.Tx) if on_x else (dm.M, dm.Tx // dm.M, dm.Th)
    ng = len(POOL_WINDOWS)

    def body(u_ref, w_ref, s_ref, o_ref):
        for g, w in enumerate(POOL_WINDOWS):
            cols = pl.ds(g * HEAD, HEAD)
            pooled = _pool_mean(u_ref[:, cols], w)
            o_ref[:, cols] = (_dot(pooled, w_ref[g]) * s_ref[:, cols]).astype(BF16)

    return pl.pallas_call(
        body, name=name, grid=(dm.Bl,),
        in_specs=[pl.BlockSpec((n, ng * HEAD), lambda b: (off + b, 2)),
                  pl.BlockSpec((ng, HEAD, HEAD), lambda b: (0, 0, 0)), pl.BlockSpec((1, ng * HEAD), lambda b: (0, 0))],
        out_specs=pl.BlockSpec((n, ng * HEAD), lambda b: (b, 0)),
        out_shape=_sds((rows, ng * HEAD), BF16), compiler_params=_cp(),
    )(P0, pw, pscale)


def pool_bwd(P0, dY, pw, pwT, pscale, dm, on_x, name):
    n, off, rows = (dm.N, 0, dm.Tx) if on_x else (dm.M, dm.Tx // dm.M, dm.Th)
    ng = len(POOL_WINDOWS)

    def body(u_ref, dy_ref, w_ref, wt_ref, s_ref, du_ref, dw_ref, ds_ref):
        b = pl.program_id(0)
        for g, w in enumerate(POOL_WINDOWS):
            cols = pl.ds(g * HEAD, HEAD)
            pooled, vjp = jax.vjp(lambda u: _pool_mean(u, w), u_ref[:, cols])
            pre = _dot(pooled, w_ref[g])
            dy = dy_ref[:, cols]
            dpre = dy * s_ref[:, cols]
            du_ref[:, cols] = vjp(_dot(dpre, wt_ref[g]))[0]
            _acc(dw_ref, (g, slice(None), slice(None)), _dot_tn(pooled, dpre), b == 0)
            _acc(ds_ref, (slice(None), cols), jnp.sum(dy * pre, axis=0, keepdims=True), b == 0)

    full = pl.BlockSpec((ng, HEAD, HEAD), lambda b: (0, 0, 0))
    vec = pl.BlockSpec((1, ng * HEAD), lambda b: (0, 0))
    return pl.pallas_call(
        body, name=name, grid=(dm.Bl,),
        in_specs=[pl.BlockSpec((n, ng * HEAD), lambda b: (off + b, 2)), pl.BlockSpec((n, ng * HEAD), lambda b: (b, 0)),
                  full, full, vec],
        out_specs=[pl.BlockSpec((n, ng * HEAD), lambda b: (b, 0)), full, vec],
        out_shape=[_sds((rows, ng * HEAD), F32), _sds((ng, HEAD, HEAD), F32), _sds((1, ng * HEAD), F32)],
        compiler_params=_cp(),
    )(P0, dY, pw, pwT, pscale)


def _conv_fn(p, w0, w1, w2, kind):
    c = w0 * _shift_rows(p, -1) + w1 * p + w2 * _shift_rows(p, 1)
    a = _silu(c)
    if kind == 2:
        return a
    a = a * lax.rsqrt(jnp.sum(a * a, axis=-1, keepdims=True) + EPS)
    return a * (HEAD ** -0.5) if kind == 0 else a


def gdn_prep(P1, conv_w, dm, on_x, name):
    n, off, rows = (dm.N, 0, dm.Tx) if on_x else (dm.M, dm.Tx // dm.M, dm.Th)

    def body(p_ref, w_ref, o_ref):
        j = pl.program_id(1)
        p, w = p_ref[...], w_ref[...]
        for kind in range(3):
            @pl.when(j // C_HEADS == kind)
            def _():
                o_ref[...] = _conv_fn(p, w[0:1], w[1:2], w[2:3], kind)

    return pl.pallas_call(
        body, name=name, grid=(dm.Bl, 3 * C_HEADS),
        in_specs=[pl.BlockSpec((n, HEAD), lambda b, j: (off + b, j)), pl.BlockSpec((3, HEAD), lambda b, j: (0, j))],
        out_specs=pl.BlockSpec((n, HEAD), lambda b, j: (b, j)),
        out_shape=_sds((rows, 3 * C_HEADS * HEAD), F32), compiler_params=_cp(),
    )(P1, conv_w)


def gdn_prep_bwd(P1, dQ, conv_w, dm, on_x, name):
    n, off, rows = (dm.N, 0, dm.Tx) if on_x else (dm.M, dm.Tx // dm.M, dm.Th)

    def body(p_ref, d0_ref, d1_ref, w_ref, dp_ref, dw_ref):
        j, b = pl.program_id(0), pl.program_id(1)
        p, w = p_ref[...], w_ref[...]
        for kind in range(3):
            @pl.when(j // C_HEADS == kind)
            def _():
                _, vjp = jax.vjp(functools.partial(_conv_fn, kind=kind), p, w[0:1], w[1:2], w[2:3])
                dp, d0, d1, d2 = vjp(d0_ref[0] + d1_ref[0])
                dp_ref[...] = dp
                _acc(dw_ref, (pl.ds(0, 1), slice(None)), d0, b == 0)
                _acc(dw_ref, (pl.ds(1, 1), slice(None)), d1, b == 0)
                _acc(dw_ref, (pl.ds(2, 1), slice(None)), d2, b == 0)

    return pl.pallas_call(
        body, name=name, grid=(3 * C_HEADS, dm.Bl),
        in_specs=[pl.BlockSpec((n, HEAD), lambda j, b: (off + b, j)),
                  pl.BlockSpec((1, n, HEAD), lambda j, b: (0, off + b, j)), pl.BlockSpec((1, n, HEAD), lambda j, b: (1, off + b, j)),
                  pl.BlockSpec((3, HEAD), lambda j, b: (0, j))],
        out_specs=[pl.BlockSpec((n, HEAD), lambda j, b: (b, j)), pl.BlockSpec((3, HEAD), lambda j, b: (0, j))],
        out_shape=[_sds((rows, 3 * C_HEADS * HEAD), F32), _sds((3, 3 * C_HEADS * HEAD), F32)],
        compiler_params=_cp(),
    )(P1, dQ, dQ, conv_w)


def _gate_fn(ab, par):
    lane = lax.broadcasted_iota(jnp.int32, ab.shape, 1)
    is_a = (lane % 16) < C_HEADS
    g = -jnp.exp(par[0:1]) * jax.nn.softplus(ab + par[1:2])
    return jnp.where(lane < 4 * C_HEADS, jnp.where(is_a, g, jax.nn.sigmoid(ab)), 0.0)


def _col(blk, idx):
    lane = lax.broadcasted_iota(jnp.int32, blk.shape, 1)
    return jnp.sum(jnp.where(lane == idx, blk, 0.0), axis=1, keepdims=True)


def _chunk_masks(rev):
    ii = lax.broadcasted_iota(jnp.int32, (CHUNK, CHUNK), 0)
    jj = lax.broadcasted_iota(jnp.int32, (CHUNK, CHUNK), 1)
    ahead = jnp.where(rev, jj - ii, ii - jj)
    return ahead >= 0, ahead > 0, (ii == jj).astype(F32)


def _inv_unit_tri(nmats, eye):
    xs = [eye - n for n in nmats]
    ps = [_hdot(n, n) for n in nmats]
    step = 2
    while True:
        xs = [x + _hdot(x, p) for x, p in zip(xs, ps)]
        step *= 2
        if step >= CHUNK:
            break
        ps = [_hdot(p, p) for p in ps]
    return xs


def _cum_lanes(x, transpose=False):
    lane = lax.broadcasted_iota(jnp.int32, x.shape, 1)
    down, up = x, x
    s = 1
    while s < CHUNK:
        down = down + _shift_rows(down, -s)
        up = up + _shift_rows(up, s)
        s *= 2
    return jnp.where((lane >= 16) if transpose else (lane < 16), down, up)


def _each(f, *lists):
    return [f(*a) for a in zip(*lists)]


def _chunk_common(qs, ks, vs, gcs, gcrs, tots, betas, rev):
    incl, strict, eye = _chunk_masks(rev)
    es = _each(lambda gc, gcr: jnp.exp(jnp.where(incl, gc - gcr, NEG)), gcs, gcrs)
    egs = [jnp.exp(gc) for gc in gcs]
    ets = _each(lambda t, gc: jnp.exp(t - gc), tots, gcs)
    gts = [jnp.exp(t) for t in tots]
    kbs = _each(lambda k, b: k * b, ks, betas)
    kks = _each(_dot_nt, kbs, ks)
    qqs = _each(_dot_nt, qs, ks)
    nmats = _each(lambda kk, e: jnp.where(strict, kk * e, 0.0), kks, es)
    ainvs = _inv_unit_tri(nmats, eye)
    rhss = _each(lambda v, b, kb, eg: jnp.concatenate([v * b, kb * eg], axis=1), vs, betas, kbs, egs)
    sols = _each(_hdot, ainvs, rhss)
    return dict(incl=incl, strict=strict, e=es, eg=egs, et=ets, gt=gts, kb=kbs, kk=kks, ainv=ainvs, sol=sols, qq=qqs)


def _chunk_fwd(qs, ks, vs, gcs, gcrs, tots, betas, rev):
    c = _chunk_common(qs, ks, vs, gcs, gcrs, tots, betas, rev)
    incl = c["incl"]
    return _each(lambda q, k, sol, qq, e, et, eg, gt:
                 (sol[:, :HEAD], sol[:, HEAD:], k * et, q * eg, jnp.where(incl, qq * e, 0.0), gt),
                 qs, ks, c["sol"], c["qq"], c["e"], c["et"], c["eg"], c["gt"])


def _chunk_bwd(qs, ks, vs, gcs, gcrs, tots, betas, rev, dus, dws, dkts, dqds, dqks, dgts):
    c = _chunk_common(qs, ks, vs, gcs, gcrs, tots, betas, rev)
    incl, strict = c["incl"], c["strict"]
    drhss = _each(lambda a, du, dw: _hdot_tn(a, jnp.concatenate([du, dw], axis=1)), c["ainv"], dus, dws)
    dns = _each(lambda drhs, sol: jnp.where(strict, -_hdot_nt(drhs, sol), 0.0), drhss, c["sol"])
    dkks = _each(lambda dn, e: dn * e, dns, c["e"])
    dqms = [jnp.where(incl, dqk, 0.0) for dqk in dqks]
    dqqs = _each(lambda dqm, e: dqm * e, dqms, c["e"])
    m_q = _each(_dot, dqqs, ks)
    m_k1 = _each(_dot_tn, dqqs, qs)
    m_k2 = _each(_dot_tn, dkks, c["kb"])
    m_kb = _each(_dot, dkks, ks)

    def finish(q, k, v, beta, e, eg, et, gt, kb, kk, qq, drhs, dn, dqm, dkt, dqd, dgt, mq, mk1, mk2, mkb):
        de = dn * kk + dqm * qq
        dq = mq + dqd * eg
        dkb = mkb + drhs[:, HEAD:] * eg
        dk = mk1 + mk2 + dkt * et + dkb * beta
        dv = drhs[:, :HEAD] * beta
        dbeta = jnp.sum(drhs[:, :HEAD] * v + dkb * k, axis=1, keepdims=True)
        deg = jnp.sum(drhs[:, HEAD:] * kb + dqd * q, axis=1, keepdims=True)
        dd = de * e
        dtd = jnp.sum(dkt * k, axis=1, keepdims=True) * et
        dgc = deg * eg - dtd + jnp.sum(dd, axis=1, keepdims=True) - jnp.sum(dd.T, axis=1, keepdims=True)
        dtot = jnp.sum(dtd, axis=0, keepdims=True) + dgt * gt
        return dq, dk, dv, dgc, dtot, dbeta

    return _each(finish, qs, ks, vs, betas, c["e"], c["eg"], c["et"], c["gt"], c["kb"], c["kk"], c["qq"],
                 drhss, dns, dqms, dkts, dqds, dgts, m_q, m_k1, m_k2, m_kb)


def gdn_chunk_pre(QKVg, P1, par, dm, name):
    nch = dm.T // CHUNK
    HD = C_HEADS * HEAD
    abcol = (4 * HD) // LANE

    def body(x_ref, ab_ref, par_ref, u_ref, w_ref, kt_ref, qd_ref, qk_ref, gt_ref, gct_s):
        d = pl.program_id(1)
        rev = d == 1
        gb = _gate_fn(ab_ref[...], par_ref[...])
        gcl = _cum_lanes(gb)
        gct_s[...] = gcl.T
        tot = jnp.sum(gb, axis=0, keepdims=True)
        hs = range(C_HEADS)
        outs = _chunk_fwd(
            [x_ref[:, pl.ds(h * HEAD, HEAD)] for h in hs],
            [x_ref[:, pl.ds((C_HEADS + h) * HEAD, HEAD)] for h in hs],
            [x_ref[:, pl.ds((2 * C_HEADS + h) * HEAD, HEAD)] for h in hs],
            [_col(gcl, d * 16 + h) for h in hs], [gct_s[pl.ds(d * 16 + h, 1), :] for h in hs],
            [_col(tot, d * 16 + h) for h in hs], [_col(gb, d * 16 + 8 + h) for h in hs], rev)
        for h, (u, w, kt, qd, qk, gt) in enumerate(outs):
            cols = pl.ds(h * HEAD, HEAD)
            u_ref[0, :, cols] = u
            w_ref[0, :, cols] = w.astype(BF16)
            kt_ref[0, :, cols] = kt.astype(BF16)
            qd_ref[0, :, cols] = qd.astype(BF16)
            qk_ref[0, :, cols] = jnp.concatenate([qk, jnp.zeros_like(qk)], axis=1).astype(BF16)
            gt_ref[0, 0, pl.ds(h, 1), :] = jnp.broadcast_to(gt, (1, HEAD))

    big = pl.BlockSpec((1, CHUNK, HD), lambda i, d: (d, i, 0))
    return pl.pallas_call(
        body, name=name, grid=(nch, 2),
        in_specs=[pl.BlockSpec((CHUNK, 3 * HD), lambda i, d: (i, 0)), pl.BlockSpec((CHUNK, LANE), lambda i, d: (i, abcol)),
                  pl.BlockSpec((2, LANE), lambda i, d: (0, 0))],
        out_specs=[big, big, big, big, big, pl.BlockSpec((1, 1, C_HEADS, HEAD), lambda i, d: (d, i, 0, 0))],
        out_shape=[_sds((2, dm.T, HD), F32), _sds((2, dm.T, HD), BF16), _sds((2, dm.T, HD), BF16),
                   _sds((2, dm.T, HD), BF16), _sds((2, dm.T, HD), BF16), _sds((2, nch, C_HEADS, HEAD), F32)],
        scratch_shapes=[pltpu.VMEM((LANE, CHUNK), F32)], compiler_params=_cp(),
    )(QKVg, P1, par)


def gdn_chunk_pre_bwd(QKVg, P1, par, dU, dW, dKT, dQD, dQK, dGT, dm, name):
    nch = dm.T // CHUNK
    HD = C_HEADS * HEAD
    abcol = (4 * HD) // LANE

    def body(x_ref, ab_ref, par_ref, du_ref, dw_ref, dkt_ref, dqd_ref, dqk_ref, dgt_ref, dx_ref, dab_ref, dpar_ref, gct_s):
        i, d = pl.program_id(0), pl.program_id(1)
        rev = d == 1
        ab, par = ab_ref[...], par_ref[...]
        gb, gate_vjp = jax.vjp(_gate_fn, ab, par)
        gcl = _cum_lanes(gb)
        gct_s[...] = gcl.T
        tot = jnp.sum(gb, axis=0, keepdims=True)
        lane = lax.broadcasted_iota(jnp.int32, (CHUNK, LANE), 1)
        dgcl = jnp.zeros((CHUNK, LANE), F32)
        dgb = jnp.zeros((CHUNK, LANE), F32)
        first = d == 0
        hs = range(C_HEADS)
        hcols = [pl.ds(h * HEAD, HEAD) for h in hs]
        outs = _chunk_bwd(
            [x_ref[:, c] for c in hcols],
            [x_ref[:, pl.ds((C_HEADS + h) * HEAD, HEAD)] for h in hs],
            [x_ref[:, pl.ds((2 * C_HEADS + h) * HEAD, HEAD)] for h in hs],
            [_col(gcl, d * 16 + h) for h in hs], [gct_s[pl.ds(d * 16 + h, 1), :] for h in hs],
            [_col(tot, d * 16 + h) for h in hs], [_col(gb, d * 16 + 8 + h) for h in hs], rev,
            [du_ref[0, :, c] for c in hcols], [dw_ref[0, :, c] for c in hcols], [dkt_ref[0, :, c] for c in hcols],
            [dqd_ref[0, :, c] for c in hcols], [dqk_ref[0, :, pl.ds(h * HEAD, CHUNK)] for h in hs],
            [dgt_ref[0, 0, pl.ds(h, 1), pl.ds(0, 1)] for h in hs])
        for h, (dq, dk, dv, dgc, dtotal, dbeta) in enumerate(outs):
            idx = d * 16 + h
            dx_ref[0, :, hcols[h]] = dq
            dx_ref[0, :, pl.ds((C_HEADS + h) * HEAD, HEAD)] = dk
            dx_ref[0, :, pl.ds((2 * C_HEADS + h) * HEAD, HEAD)] = dv
            dgcl = dgcl + jnp.where(lane == idx, dgc, 0.0)
            dgb = dgb + jnp.where(lane == idx + 8, dbeta, 0.0) + jnp.where(lane == idx, dtotal, 0.0)
        dab, dpar = gate_vjp(dgb + _cum_lanes(dgcl, transpose=True))
        dab_ref[0] = dab
        _acc(dpar_ref, (slice(None), slice(None)), dpar, (i == 0) & first)

    big = pl.BlockSpec((1, CHUNK, HD), lambda i, d: (d, i, 0))
    return pl.pallas_call(
        body, name=name, grid=(nch, 2),
        in_specs=[pl.BlockSpec((CHUNK, 3 * HD), lambda i, d: (i, 0)), pl.BlockSpec((CHUNK, LANE), lambda i, d: (i, abcol)),
                  pl.BlockSpec((2, LANE), lambda i, d: (0, 0)), big, big, big, big, big,
                  pl.BlockSpec((1, 1, C_HEADS, HEAD), lambda i, d: (d, i, 0, 0))],
        out_specs=[pl.BlockSpec((1, CHUNK, 3 * HD), lambda i, d: (d, i, 0)), pl.BlockSpec((1, CHUNK, LANE), lambda i, d: (d, i, 0)),
                   pl.BlockSpec((2, LANE), lambda i, d: (0, 0))],
        out_shape=[_sds((2, dm.T, 3 * HD), F32), _sds((2, dm.T, LANE), F32), _sds((2, LANE), F32)],
        scratch_shapes=[pltpu.VMEM((LANE, CHUNK), F32)], compiler_params=_cp(),
    )(QKVg, P1, par, dU, dW, dKT, dQD, dQK, dGT)


def _scan_chunk(b, d, c, dm):
    nh, nx = dm.M // CHUNK, dm.N // CHUNK
    in_h = c < nh
    pos_h = jnp.where(d == 0, c, nh - 1 - c)
    pos_x = jnp.where(d == 0, c - nh, nx - 1 - (c - nh))
    return jnp.where(in_h, dm.Tx // CHUNK + b * nh + pos_h, b * nx + pos_x)


def gdn_scan_fwd(U, W, KT, QD, QK, GT, dm, name):
    nch = dm.T // CHUNK
    HD = C_HEADS * HEAD
    nsc = (dm.M + dm.N) // CHUNK

    def body(u_ref, w_ref, kt_ref, qd_ref, qk_ref, gt_ref, o_ref, ss_ref, s_s):
        @pl.when(pl.program_id(2) == 0)
        def _():
            s_s[...] = jnp.zeros_like(s_s)

        hs = range(C_HEADS)
        blk = [pl.ds(h * HEAD, HEAD) for h in hs]
        ss = [s_s[b, :] for b in blk]
        for b, s in zip(blk, ss):
            ss_ref[0, 0, b, :] = s
        sbs = [s.astype(BF16) for s in ss]
        ws = [jnp.dot(w_ref[0, :, b], sb, preferred_element_type=F32) for b, sb in zip(blk, sbs)]
        os1 = [jnp.dot(qd_ref[0, :, b], sb, preferred_element_type=F32) for b, sb in zip(blk, sbs)]
        vnbs = [(u_ref[0, :, b] - wv).astype(BF16) for b, wv in zip(blk, ws)]
        os2 = [jnp.dot(qk_ref[0, :, pl.ds(h * HEAD, CHUNK)], vnbs[h], preferred_element_type=F32) for h in hs]
        upd = [_dot_tn(kt_ref[0, :, b], vnb) for b, vnb in zip(blk, vnbs)]
        for h in hs:
            o_ref[0, :, blk[h]] = os1[h] + os2[h]
            s_s[blk[h], :] = ss[h] * gt_ref[0, 0, pl.ds(h, 1), :] + upd[h]

    big = pl.BlockSpec((1, CHUNK, HD), lambda b, d, c: (d, _scan_chunk(b, d, c, dm), 0))
    return pl.pallas_call(
        body, name=name, grid=(dm.Bl, 2, nsc),
        in_specs=[big, big, big, big, big,
                  pl.BlockSpec((1, 1, C_HEADS, HEAD), lambda b, d, c: (d, _scan_chunk(b, d, c, dm), 0, 0))],
        out_specs=[big, pl.BlockSpec((1, 1, HD, HEAD), lambda b, d, c: (d, _scan_chunk(b, d, c, dm), 0, 0))],
        out_shape=[_sds((2, dm.T, HD), F32), _sds((2, nch, HD, HEAD), F32)],
        scratch_shapes=[pltpu.VMEM((HD, HEAD), F32)], compiler_params=_cp(),
    )(U, W, KT, QD, QK, GT)


def gdn_scan_bwd(dO, SS, U, W, KT, QD, QK, GT, dm, name):
    nch = dm.T // CHUNK
    HD = C_HEADS * HEAD
    nsc = (dm.M + dm.N) // CHUNK

    def body(do_ref, ss_ref, u_ref, w_ref, kt_ref, qd_ref, qk_ref, gt_ref,
             du_ref, dw_ref, dkt_ref, dqd_ref, dqk_ref, dgt_ref, ds_s):
        @pl.when(pl.program_id(2) == 0)
        def _():
            ds_s[...] = jnp.zeros_like(ds_s)

        hs = range(C_HEADS)
        blk = [pl.ds(h * HEAD, HEAD) for h in hs]
        ss = [ss_ref[0, 0, b, :] for b in blk]
        sbs = [s.astype(BF16) for s in ss]
        dobs = [do_ref[:, b].astype(BF16) for b in blk]
        dsns = [ds_s[b, :] for b in blk]
        dsnbs = [t.astype(BF16) for t in dsns]
        wss = [jnp.dot(w_ref[0, :, b], sb, preferred_element_type=F32) for b, sb in zip(blk, sbs)]
        dqds = [_dot_nt(dob, sb) for dob, sb in zip(dobs, sbs)]
        dv1 = [_dot_tn(qk_ref[0, :, pl.ds(h * HEAD, CHUNK)], dobs[h]) for h in hs]
        dv2 = [jnp.dot(kt_ref[0, :, b], t, preferred_element_type=F32) for b, t in zip(blk, dsnbs)]
        ds1 = [_dot_tn(qd_ref[0, :, b], dob) for b, dob in zip(blk, dobs)]
        vnbs = [(u_ref[0, :, b] - wv).astype(BF16) for b, wv in zip(blk, wss)]
        dvns = [a + b for a, b in zip(dv1, dv2)]
        dvnbs = [t.astype(BF16) for t in dvns]
        dqks = [_dot_nt(dob, vnb) for dob, vnb in zip(dobs, vnbs)]
        dkts = [_dot_nt(vnb, t) for vnb, t in zip(vnbs, dsnbs)]
        dws = [_dot_nt(t, sb) for t, sb in zip(dvnbs, sbs)]
        ds2 = [_dot_tn(w_ref[0, :, b], t) for b, t in zip(blk, dvnbs)]
        for h in hs:
            b = blk[h]
            dqd_ref[0, :, b] = dqds[h]
            dqk_ref[0, :, b] = jnp.concatenate([dqks[h], jnp.zeros_like(dqks[h])], axis=1)
            dkt_ref[0, :, b] = dkts[h]
            du_ref[0, :, b] = dvns[h]
            dw_ref[0, :, b] = -dws[h]
            dgt_ref[0, 0, pl.ds(h, 1), :] = jnp.broadcast_to(jnp.sum(dsns[h] * ss[h], keepdims=True), (1, HEAD))
            ds_s[b, :] = dsns[h] * gt_ref[0, 0, pl.ds(h, 1), :] + ds1[h] - ds2[h]

    def mem(b, d, c):
        return _scan_chunk(b, d, nsc - 1 - c, dm)

    big = pl.BlockSpec((1, CHUNK, HD), lambda b, d, c: (d, mem(b, d, c), 0))
    gts = pl.BlockSpec((1, 1, C_HEADS, HEAD), lambda b, d, c: (d, mem(b, d, c), 0, 0))
    return pl.pallas_call(
        body, name=name, grid=(dm.Bl, 2, nsc),
        in_specs=[pl.BlockSpec((CHUNK, HD), lambda b, d, c: (mem(b, d, c), 0)),
                  pl.BlockSpec((1, 1, HD, HEAD), lambda b, d, c: (d, mem(b, d, c), 0, 0)), big, big, big, big, big, gts],
        out_specs=[big, big, big, big, big, gts],
        out_shape=[_sds((2, dm.T, HD), F32)] * 5 + [_sds((2, nch, C_HEADS, HEAD), F32)],
        scratch_shapes=[pltpu.VMEM((HD, HEAD), F32)], compiler_params=_cp(),
    )(dO, SS, U, W, KT, QD, QK, GT)


def _finish_fn(o, z, gain):
    y = o * lax.rsqrt(jnp.mean(o * o, axis=-1, keepdims=True) + EPS) * gain
    return y * _silu(z)


def gdn_finish(O, P1, og, dm, name):
    TM = dm.TM
    HD = C_HEADS * HEAD
    zc = (3 * HD) // HEAD

    def body(o0_ref, o1_ref, z_ref, g_ref, y_ref):
        y_ref[...] = _finish_fn(o0_ref[0] + o1_ref[0], z_ref[...], g_ref[...]).astype(BF16)

    return pl.pallas_call(
        body, name=name, grid=(dm.ntx, C_HEADS),
        in_specs=[pl.BlockSpec((1, TM, HEAD), lambda i, j: (0, i, j)), pl.BlockSpec((1, TM, HEAD), lambda i, j: (1, i, j)),
                  pl.BlockSpec((TM, HEAD), lambda i, j: (i, zc + j)), pl.BlockSpec((1, HEAD), lambda i, j: (0, 0))],
        out_specs=pl.BlockSpec((TM, HEAD), lambda i, j: (i, j)),
        out_shape=_sds((dm.Tx, HD), BF16), compiler_params=_cp(),
    )(O, O, P1, og)


def gdn_finish_bwd(O, P1, og, dY, dm, name):
    TM = dm.TM
    HD = C_HEADS * HEAD
    zc = (3 * HD) // HEAD

    def body(o0_ref, o1_ref, z_ref, g_ref, dy_ref, do_ref, dz_ref, dg_ref):
        i, j = pl.program_id(0), pl.program_id(1)
        _, vjp = jax.vjp(_finish_fn, o0_ref[0] + o1_ref[0], z_ref[...], g_ref[...])
        do, dz, dg = vjp(dy_ref[...])
        do_ref[...] = do
        dz_ref[...] = dz
        _acc(dg_ref, (slice(None), slice(None)), dg, (i == 0) & (j == 0))

    blk = pl.BlockSpec((TM, HEAD), lambda i, j: (i, j))
    return pl.pallas_call(
        body, name=name, grid=(dm.ntx, C_HEADS),
        in_specs=[pl.BlockSpec((1, TM, HEAD), lambda i, j: (0, i, j)), pl.BlockSpec((1, TM, HEAD), lambda i, j: (1, i, j)),
                  pl.BlockSpec((TM, HEAD), lambda i, j: (i, zc + j)), pl.BlockSpec((1, HEAD), lambda i, j: (0, 0)), blk],
        out_specs=[blk, blk, pl.BlockSpec((1, HEAD), lambda i, j: (0, 0))],
        out_shape=[_sds((dm.Tx, HD), F32), _sds((dm.Tx, HD), F32), _sds((1, HEAD), F32)],
        compiler_params=_cp(),
    )(O, O, P1, og, dY)


def adaln_fwd(c_ext, w_mod, b_loc, name):
    R, D = c_ext.shape
    nl = w_mod.shape[2]
    tn = _pick(nl, 384)

    def body(c_ref, w_ref, b_ref, o_ref):
        o_ref[0] = _dot(_silu(c_ref[...]), w_ref[0]) + b_ref[0]

    return pl.pallas_call(
        body, name=name, grid=(2, nl // tn),
        in_specs=[pl.BlockSpec((R, D), lambda l, j: (0, 0)), pl.BlockSpec((1, D, tn), lambda l, j: (l, 0, j)),
                  pl.BlockSpec((1, 1, tn), lambda l, j: (l, 0, j))],
        out_specs=pl.BlockSpec((1, R, tn), lambda l, j: (l, 0, j)),
        out_shape=_sds((2, R, nl), F32), compiler_params=_cp(),
    )(c_ext, w_mod, b_loc)


def adaln_bwd(c_ext, c_ctx, w_mod, dmx, dmh, nb, name):
    R, D = c_ext.shape
    nl = w_mod.shape[2]
    tn = _pick(nl, 384)
    nj = nl // tn

    def body(c_ref, cc_ref, w_ref, dmx_ref, dmh_ref, gw_ref, dc_ref):
        l, j = pl.program_id(0), pl.program_id(1)
        dh = dmh_ref[0, 0:1, :]
        for k in range(1, N_DEV):
            dh = dh + dmh_ref[0, k:k + 1, :]
        row = lax.broadcasted_iota(jnp.int32, (R, tn), 0)
        dmat = dmx_ref[0] + jnp.where(row == nb, dh, 0.0)
        gw_ref[0] = _dot_tn(_silu(c_ref[...]), dmat)
        part = _dot_nt(jnp.broadcast_to(dh, (8, tn)), w_ref[0])[0:1]
        _acc(dc_ref, (slice(None), slice(None)), part, (l == 0) & (j == 0))

        @pl.when((l == 1) & (j == nj - 1))
        def _():
            cc = cc_ref[...]
            sg = jax.nn.sigmoid(cc)
            dc_ref[...] = dc_ref[...] * (sg * (1.0 + cc * (1.0 - sg)))

    return pl.pallas_call(
        body, name=name, grid=(2, nj),
        in_specs=[pl.BlockSpec((R, D), lambda l, j: (0, 0)), pl.BlockSpec((1, D), lambda l, j: (0, 0)),
                  pl.BlockSpec((1, D, tn), lambda l, j: (l, 0, j)), pl.BlockSpec((1, R, tn), lambda l, j: (l, 0, j)),
                  pl.BlockSpec((1, N_DEV, tn), lambda l, j: (l, 0, j))],
        out_specs=[pl.BlockSpec((1, D, tn), lambda l, j: (l, 0, j)), pl.BlockSpec((1, D), lambda l, j: (0, 0))],
        out_shape=[_sds((2, D, nl), F32), _sds((1, D), F32)], compiler_params=_cp(),
    )(c_ext, c_ctx, w_mod, dmx, dmh)


def bmod_grad(dmx, dmh, name):
    _, R, n9 = dmx.shape

    def body(dmx_ref, dmh_ref, o_ref):
        o_ref[0] = jnp.sum(dmx_ref[0], axis=0, keepdims=True) + jnp.sum(dmh_ref[0], axis=0, keepdims=True)

    return pl.pallas_call(
        body, name=name, grid=(2,),
        in_specs=[pl.BlockSpec((1, R, n9), lambda l: (l, 0, 0)), pl.BlockSpec((1, N_DEV, n9), lambda l: (l, 0, 0))],
        out_specs=pl.BlockSpec((1, 1, n9), lambda l: (l, 0, 0)), out_shape=_sds((2, 1, n9), F32),
        compiler_params=_cp(),
    )(dmx, dmh)


def adamw(gs, w, m, v, name):
    S, R, C = gs.shape
    cap = max(8, (1 << 20) // (S * C))
    tr = R
    if R > cap:
        tr = max(t for t in range(8, cap + 1, 8) if R % t == 0)

    def body(g_ref, w_ref, m_ref, v_ref, go_ref, d_ref, mo_ref, vo_ref):
        g = g_ref[0].astype(F32)
        for k in range(1, S):
            g = g + g_ref[k].astype(F32)
        mn = ADAM_B1 * m_ref[...] + (1.0 - ADAM_B1) * g
        vn = ADAM_B2 * v_ref[...] + (1.0 - ADAM_B2) * jnp.square(g)
        m_hat = mn / (1.0 - ADAM_B1 ** ADAM_STEP)
        v_hat = vn / (1.0 - ADAM_B2 ** ADAM_STEP)
        go_ref[...] = g
        d_ref[...] = -ADAM_LR * (m_hat / (jnp.sqrt(v_hat) + ADAM_EPS) + ADAM_WD * w_ref[...])
        mo_ref[...] = mn
        vo_ref[...] = vn

    blk = pl.BlockSpec((tr, C), lambda i: (i, 0))
    return pl.pallas_call(
        body, name=name, grid=(R // tr,),
        in_specs=[pl.BlockSpec((S, tr, C), lambda i: (0, i, 0)), blk, blk, blk],
        out_specs=[blk] * 4, out_shape=[_sds((R, C), F32)] * 4, compiler_params=_cp(),
    )(gs, w, m, v)


def _gather_flat(parts, dtype, name):
    flat = jnp.concatenate([p.astype(dtype).reshape(-1) for p in parts])
    n = flat.shape[0]
    pad = (-n) % LANE
    if pad:
        flat = jnp.concatenate([flat, jnp.zeros((pad,), dtype)])
    got = all_gather([flat.reshape(-1, LANE)], name)[0].reshape(N_DEV, -1)
    out, off = [], 0
    for p in parts:
        out.append(got[:, off:off + p.size].reshape((N_DEV,) + p.shape))
        off += p.size
    return out


def _cols_full(g):
    return g.transpose(1, 0, 2).reshape(g.shape[1], -1)


def _cols_split(full):
    K = full.shape[0]
    return full.reshape(K, N_DEV, -1).transpose(1, 0, 2)


def kernel(x, c, ctx, c_ctx, w_mod, b_mod, norm_g, ffn_wg, ffn_wu, ffn_wd, ab_w_in, ab_q_norm, ab_k_norm, pool_w, pool_scale, ab_w_out, gdn_w_in, gdn_conv_w, gdn_a_log, gdn_dt_bias, gdn_o_norm, gdn_w_out, loss_target, m_c_ctx, m_w_mod, m_b_mod, m_norm_g, m_ffn_wg, m_ffn_wu, m_ffn_wd, m_ab_w_in, m_ab_q_norm, m_ab_k_norm, m_pool_w, m_pool_scale, m_ab_w_out, m_gdn_w_in, m_gdn_conv_w, m_gdn_a_log, m_gdn_dt_bias, m_gdn_o_norm, m_gdn_w_out, v_c_ctx, v_w_mod, v_b_mod, v_norm_g, v_ffn_wg, v_ffn_wu, v_ffn_wd, v_ab_w_in, v_ab_q_norm, v_ab_k_norm, v_pool_w, v_pool_scale, v_ab_w_out, v_gdn_w_in, v_gdn_conv_w, v_gdn_a_log, v_gdn_dt_bias, v_gdn_o_norm, v_gdn_w_out):
    Bl, N, D = x.shape
    M = ctx.shape[1]
    F = ffn_wd.shape[2] * N_DEV
    dm = Dims(Bl, N, M, D, F)
    TM, Tx, Th, T, G = dm.TM, dm.Tx, dm.Th, dm.T, dm.G
    HD = C_HEADS * HEAD
    me = 4 * lax.axis_index("x") + 2 * lax.axis_index("y") + lax.axis_index("c")
    nb = N_DEV * Bl
    R = -(-(nb + 1) // 8) * 8
    nl = w_mod.shape[2]
    n_gdn = gdn_w_in.shape[2] * N_DEV
    n_gdn_pad = -(-n_gdn // LANE) * LANE

    big = [w.astype(BF16) for w in (ffn_wg, ffn_wu, ffn_wd, ab_w_in, ab_w_out, gdn_w_in, gdn_w_out)]
    g_wg, g_wu, g_wd, g_abin, g_about, g_gin, g_gout = all_gather(big, "gather_weights")
    g_c, g_ng, g_cw = _gather_flat([c, norm_g, gdn_conv_w], F32, "gather_small")
    W_ABIN = _cols_full(g_abin[:, 0])
    W_ABOUT = g_about[:, 0].reshape(-1, D)
    W_GIN = jnp.pad(_cols_full(g_gin[:, 0]), ((0, 0), (0, n_gdn_pad - n_gdn)))
    W_GOUT = g_gout[:, 0].reshape(-1, D)
    gains = g_ng.transpose(1, 2, 0, 3).reshape(2, 3, 1, D)
    conv_w = g_cw[:, 0].transpose(1, 0, 2).reshape(3, -1)

    c_all = g_c.reshape(nb, D)
    c_ext = jnp.concatenate([c_all, c_ctx[None], jnp.zeros((R - nb - 1, D), F32)], 0)
    b_loc = lax.dynamic_slice_in_dim(b_mod, me * nl, nl, axis=1).reshape(2, 1, nl)
    mod_loc = adaln_fwd(c_ext, w_mod, b_loc, "adaln_fwd")
    (g_mod,) = _gather_flat([mod_loc], F32, "gather_mod")
    mod_full = g_mod.transpose(1, 2, 0, 3).reshape(2, R, 9 * D)
    MOD = []
    for l in range(2):
        mine = lax.dynamic_slice_in_dim(mod_full[l], me * Bl, Bl, axis=0)
        MOD.append(jnp.concatenate([mine, mod_full[l, nb:nb + 1]], 0).reshape(G, 9, D))

    dm5, dmr = dm.with_tile(512), dm.with_tile(1024)
    tabs = _rope_tables(dmr)
    qkg = jnp.stack([ab_q_norm, ab_k_norm])
    pw = pool_w[0].astype(BF16)
    pwT = pool_w[0].transpose(0, 2, 1).astype(BF16)
    par = jnp.stack([jnp.pad(jnp.pad(p[0], ((0, 0), (0, 8))).reshape(-1), (0, LANE - 32))
                     for p in (gdn_a_log, gdn_dt_bias)])

    X0 = jnp.concatenate([x.reshape(Tx, D), ctx.reshape(Th, D)], 0)
    nt, ntx = dm.nt, dm.ntx
    def ffn(X, l, s, s0, all_rows, tag):
        return ffn_fwd(X, MOD[l], gains[l, 2 * s], g_wg, g_wu, g_wd, l, s, s0, dm, all_rows, "ffn_fwd_" + tag)

    X1, Y1 = ffn(X0, 0, 0, 0, True, "00")
    P0, XN0 = modmm(X1, MOD[0], gains[0, 1], W_ABIN, 3, dm5, "ab_in_proj")
    QKV = qkv_prep(P0, qkg, tabs, dmr, "qkv_prep")
    ATT = jnp.concatenate([attn_fwd(QKV, dm, True, "attn_fwd_x"), attn_fwd(QKV, dm, False, "attn_fwd_h")], 0)
    POOL = jnp.concatenate([pool_fwd(P0, pw, pool_scale, dm, True, "pool_fwd_x"),
                            pool_fwd(P0, pw, pool_scale, dm, False, "pool_fwd_h")], 0)
    na = A_HEADS * HEAD
    X2, YM0 = proj_res([ATT, POOL], [W_ABOUT[:na], W_ABOUT[na:]], X1, MOD[0], dm5, dm5.nt, "ab_out_proj")
    X3, Y3 = ffn(X2, 0, 1, 6, True, "01")
    X4, Y4 = ffn(X3, 1, 0, 0, True, "10")
    P1, XN1 = modmm(X4, MOD[1], gains[1, 1], W_GIN, 3, dm5, "gdn_in_proj")
    QKVg = jnp.concatenate([gdn_prep(P1, conv_w, dm, True, "gdn_prep_x"), gdn_prep(P1, conv_w, dm, False, "gdn_prep_h")], 0)
    U, W, KT, QD, QK, GT = gdn_chunk_pre(QKVg, P1, par, dm, "gdn_chunk_pre")
    O, SS = gdn_scan_fwd(U, W, KT, QD, QK, GT, dm, "gdn_scan_fwd")
    FIN = gdn_finish(O, P1, gdn_o_norm, dmr, "gdn_finish")
    X5, YM1 = proj_res([FIN], [W_GOUT], X4, MOD[1], dm5, dm5.ntx, "gdn_out_proj")
    X6, Y6 = ffn(X5, 1, 1, 6, False, "11")
    lvec, dX6 = loss_head(X6, loss_target.reshape(Tx, D), dmr, "loss_head")
    loss = lax.psum(lvec[0, 0], AXES)

    zrow = lambda a: jnp.concatenate([a, jnp.zeros((G - a.shape[0],) + a.shape[1:], F32)], 0) if a.shape[0] < G else a

    def ffn_back(Xin, dXo, Y, l, s, s0, all_rows, tag):
        dXi, XNb, DOb, Hb, DGb, DUb, dmod, dgain = ffn_bwd(
            Xin, dXo, Y, MOD[l], gains[l, 2 * s], g_wg, g_wu, g_wd, l, s, s0, dm, all_rows, "ffn_bwd_" + tag)
        dwg, dwu, dwd = ffn_dw(XNb, DOb, Hb, DGb, DUb, dm, "ffn_dw_" + tag)
        return dXi, zrow(dmod), dgain, dwg, dwu, dwd

    dX5, dmod_12, dgain_12, dwg11, dwu11, dwd11 = ffn_back(X5, dX6, Y6, 1, 1, 6, False, "11")
    DY1, dFIN, dgate_1 = proj_res_bwd(dX5, YM1, MOD[1], [W_GOUT], dm5, dm5.ntx, "gdn_out_proj_bwd")
    d_gout = atb(FIN, DY1, Tx, dm, "gdn_dwout")
    dOsum, dZ, d_onorm = gdn_finish_bwd(O, P1, gdn_o_norm, dFIN, dmr, "gdn_finish_bwd")
    dO_all = jnp.concatenate([dOsum, jnp.zeros((Th, HD), F32)], 0)
    dU, dW, dKT, dQD, dQK, dGT = gdn_scan_bwd(dO_all, SS, U, W, KT, QD, QK, GT, dm, "gdn_scan_bwd")
    dQKVg, dAB, dPAR = gdn_chunk_pre_bwd(QKVg, P1, par, dU, dW, dKT, dQD, dQK, dGT, dm, "gdn_chunk_pre_bwd")
    dPx, dcw_x = gdn_prep_bwd(P1, dQKVg, conv_w, dm, True, "gdn_prep_bwd_x")
    dPh, dcw_h = gdn_prep_bwd(P1, dQKVg, conv_w, dm, False, "gdn_prep_bwd_h")
    d_conv = dcw_x + dcw_h
    dP1 = jnp.concatenate([jnp.concatenate([dPx, dPh], 0), jnp.concatenate([dZ, jnp.zeros((Th, HD), F32)], 0),
                           dAB[0] + dAB[1]], axis=1).astype(BF16)
    d_gin = atb(XN1, dP1, T, dm, "gdn_dwin")[:, :n_gdn]
    dX5_full = jnp.concatenate([dX5, jnp.zeros((Th, D), F32)], 0)
    dX4, dmod_11, dgain_11 = mixin_bwd(dP1, W_GIN, X4, dX5_full, MOD[1], gains[1, 1], 3, dm, "gdn_in_proj_bwd")
    dX3, dmod_10, dgain_10, dwg10, dwu10, dwd10 = ffn_back(X3, dX4, Y4, 1, 0, 0, True, "10")
    dMOD1 = jnp.concatenate([dmod_10, dmod_11, zrow(dgate_1), dmod_12], 1).reshape(G, 9 * D)

    dX2, dmod_02, dgain_02, dwg01, dwu01, dwd01 = ffn_back(X2, dX3, Y3, 0, 1, 6, True, "01")
    DY0, dATT, dPOOL, dgate_0 = proj_res_bwd(dX2, YM0, MOD[0], [W_ABOUT[:na], W_ABOUT[na:]], dm5, dm5.nt, "ab_out_proj_bwd")
    d_about = jnp.concatenate([atb(ATT, DY0, T, dm, "ab_dwout_a"), atb(POOL, DY0, T, dm, "ab_dwout_p")], 0)
    dUx, dpw_x, dps_x = pool_bwd(P0, dPOOL[:Tx], pw, pwT, pool_scale, dm, True, "pool_bwd_x")
    dUh, dpw_h, dps_h = pool_bwd(P0, dPOOL[Tx:], pw, pwT, pool_scale, dm, False, "pool_bwd_h")
    dQh, dKh0, dVh0 = attn_bwd(QKV, dATT[Tx:], dm, False, None, "attn_bwd_h")
    dQx, dKh, dVh, dKx, dVx = attn_bwd(QKV, dATT[:Tx], dm, True, (dKh0, dVh0), "attn_bwd_x")
    dQKV = jnp.concatenate([jnp.concatenate([dQx, dQh], 0), jnp.concatenate([dKx, dKh], 0), jnp.concatenate([dVx, dVh], 0)], 1)
    dPqkv, d_qkg = qkv_prep_bwd(P0, dQKV, qkg, tabs, dmr, "qkv_prep_bwd")
    dP0 = jnp.concatenate([dPqkv, jnp.concatenate([dUx, dUh], 0)], 1).astype(BF16)
    d_abin = atb(XN0, dP0, T, dm, "ab_dwin")
    dX1, dmod_01, dgain_01 = mixin_bwd(dP0, W_ABIN, X1, dX2, MOD[0], gains[0, 1], 3, dm5, "ab_in_proj_bwd")
    dX0, dmod_00, dgain_00, dwg00, dwu00, dwd00 = ffn_back(X0, dX1, Y1, 0, 0, 0, True, "00")
    dMOD0 = jnp.concatenate([dmod_00, dmod_01, dgate_0, dmod_02], 1).reshape(G, 9 * D)
    grad_x = dX0[:Tx].reshape(Bl, N, D)

    d_ng = jnp.concatenate([dgain_00, dgain_01, dgain_02, dgain_10, dgain_11, dgain_12], 0)
    nf = ffn_wg.shape[3]
    parts = [jnp.concatenate([dwg00, dwg01, dwg10, dwg11], 1), jnp.concatenate([dwu00, dwu01, dwu10, dwu11], 1),
             jnp.concatenate([dwd00, dwd01, dwd10, dwd11], 1),
             _cols_split(d_abin), d_about.reshape(N_DEV, -1, D), _cols_split(d_gin), d_gout.reshape(N_DEV, -1, D),
             _cols_split(d_conv), _cols_split(d_ng)]
    own, got = scatter_sibling(parts, "scatter_sibling")
    chip_sums = [pair_add(o, g, "pair_add_%d" % i) for i, (o, g) in enumerate(zip(own, got))]
    gs_wg, gs_wu, gs_wd, gs_abin, gs_about, gs_gin, gs_gout, gs_conv, gs_ng = scatter_chips(chip_sums, "scatter_chips")

    d_alog = dPAR[0, :32].reshape(2, 16)[:, :8].reshape(1, 16)
    d_dtb = dPAR[1, :32].reshape(2, 16)[:, :8].reshape(1, 16)
    small = [d_qkg[0], d_qkg[1], (dpw_x + dpw_h).reshape(-1, HEAD), dps_x + dps_h, d_alog, d_dtb, d_onorm,
             jnp.stack([dMOD0, dMOD1])]
    gs_qn, gs_kn, gs_pw, gs_ps, gs_alog, gs_dtb, gs_on, g_dm = _gather_flat(small, F32, "gather_small_grads")
    dmx = g_dm[:, :, :Bl].transpose(1, 0, 2, 3).reshape(2, nb, 9 * D)
    dmx = jnp.concatenate([dmx, jnp.zeros((2, R - nb, 9 * D), F32)], 1)
    dmh = g_dm[:, :, Bl].transpose(1, 0, 2)
    cols_of_me = lambda a: lax.dynamic_slice_in_dim(a, me * nl, nl, axis=2)
    d_wmod, dcc = adaln_bwd(c_ext, c_ctx[None], w_mod, cols_of_me(dmx), cols_of_me(dmh), nb, "adaln_bwd")
    d_bmod = bmod_grad(dmx, dmh, "bmod_grad")
    (gs_cc,) = _gather_flat([dcc], F32, "gather_cctx_grad")

    def upd(gs, w, m, v, shape2, name):
        outs = adamw(gs.reshape((gs.shape[0],) + shape2), w.reshape(shape2), m.reshape(shape2), v.reshape(shape2), "adamw_" + name)
        return [o.reshape(w.shape) for o in outs]

    res = [
        upd(gs_cc, c_ctx, m_c_ctx, v_c_ctx, (1, D), "c_ctx"),
        upd(d_wmod[None], w_mod, m_w_mod, v_w_mod, (2 * D, nl), "w_mod"),
        upd(d_bmod[None], b_mod, m_b_mod, v_b_mod, (2, 9 * D), "b_mod"),
        upd(gs_ng, norm_g, m_norm_g, v_norm_g, (6, D // N_DEV), "norm_g"),
        upd(gs_wg, ffn_wg, m_ffn_wg, v_ffn_wg, (4 * D, nf), "ffn_wg"),
        upd(gs_wu, ffn_wu, m_ffn_wu, v_ffn_wu, (4 * D, nf), "ffn_wu"),
        upd(gs_wd, ffn_wd, m_ffn_wd, v_ffn_wd, (4 * nf, D), "ffn_wd"),
        upd(gs_abin, ab_w_in, m_ab_w_in, v_ab_w_in, (D, ab_w_in.shape[2]), "ab_w_in"),
        upd(gs_qn, ab_q_norm, m_ab_q_norm, v_ab_q_norm, (1, HEAD), "ab_q_norm"),
        upd(gs_kn, ab_k_norm, m_ab_k_norm, v_ab_k_norm, (1, HEAD), "ab_k_norm"),
        upd(gs_pw, pool_w, m_pool_w, v_pool_w, (len(POOL_WINDOWS) * HEAD, HEAD), "pool_w"),
        upd(gs_ps, pool_scale, m_pool_scale, v_pool_scale, (1, len(POOL_WINDOWS) * HEAD), "pool_scale"),
        upd(gs_about, ab_w_out, m_ab_w_out, v_ab_w_out, (ab_w_out.shape[1], D), "ab_w_out"),
        upd(gs_gin, gdn_w_in, m_gdn_w_in, v_gdn_w_in, (D, gdn_w_in.shape[2]), "gdn_w_in"),
        upd(gs_conv, gdn_conv_w, m_gdn_conv_w, v_gdn_conv_w, (3, gdn_conv_w.shape[2]), "gdn_conv_w"),
        upd(gs_alog, gdn_a_log, m_gdn_a_log, v_gdn_a_log, (1, 16), "gdn_a_log"),
        upd(gs_dtb, gdn_dt_bias, m_gdn_dt_bias, v_gdn_dt_bias, (1, 16), "gdn_dt_bias"),
        upd(gs_on, gdn_o_norm, m_gdn_o_norm, v_gdn_o_norm, (1, HEAD), "gdn_o_norm"),
        upd(gs_gout, gdn_w_out, m_gdn_w_out, v_gdn_w_out, (gdn_w_out.shape[1], D), "gdn_w_out"),
    ]
    return (loss, grad_x, *[r[0] for r in res], *[r[1] for r in res], *[r[2] for r in res], *[r[3] for r in res])
```

```python
import functools
import math
from typing import NamedTuple

import jax
import jax.numpy as jnp
from jax import lax
from jax.experimental import pallas as pl
from jax.experimental.pallas import tpu as pltpu

F32, BF16 = jnp.float32, jnp.bfloat16
EPS = 1e-6
HEAD = 128
CHUNK = 64
GRID_W = 64
ROPE_THETA = 10000.0
POOL_WINDOWS = (2, 4, 8, 16)
A_HEADS, A_KV = 4, 2
C_HEADS = 8
N_DEV = 8
AXES = ("x", "y", "c")
ADAM_LR, ADAM_B1, ADAM_B2, ADAM_EPS, ADAM_WD, ADAM_STEP = 0.001, 0.9, 0.999, 1e-08, 0.01, 10
LANE = 128
VMEM_LIMIT = 56 * 1024 * 1024
HI = lax.Precision.HIGHEST
NEG = -1e30


def _cp():
    return pltpu.CompilerParams(vmem_limit_bytes=VMEM_LIMIT)


def _sds(shape, dtype):
    return jax.ShapeDtypeStruct(tuple(shape), dtype)


def _dot(a, b):
    return jnp.dot(a.astype(BF16), b.astype(BF16), preferred_element_type=F32)


def _dot_nt(a, b):
    return lax.dot_general(a.astype(BF16), b.astype(BF16), (((1,), (1,)), ((), ())), preferred_element_type=F32)


def _dot_tn(a, b):
    return lax.dot_general(a.astype(BF16), b.astype(BF16), (((0,), (0,)), ((), ())), preferred_element_type=F32)


def _dot3(a, b, dims):
    ah, bh = a.astype(BF16), b.astype(BF16)
    al, bl = (a - ah.astype(F32)).astype(BF16), (b - bh.astype(F32)).astype(BF16)
    f = lambda x, y: lax.dot_general(x, y, (dims, ((), ())), preferred_element_type=F32)
    return f(ah, bh) + (f(ah, bl) + f(al, bh))


def _hdot(a, b):
    return _dot3(a, b, ((1,), (0,)))


def _hdot_nt(a, b):
    return _dot3(a, b, ((1,), (1,)))


def _hdot_tn(a, b):
    return _dot3(a, b, ((0,), (0,)))


def _pick(n, cap):
    if n <= cap:
        return n
    best = None
    for t in range(LANE, cap + 1, LANE):
        if n % t == 0:
            best = t
    assert best is not None, (n, cap)
    return best


class Dims(NamedTuple):
    Bl: int
    N: int
    M: int
    D: int
    F: int
    tm: int = 0

    @property
    def TM(self):
        return self.tm if self.tm else min(256, self.M)

    def with_tile(self, cap):
        return self._replace(tm=max(t for t in (1024, 512, 256, 128) if t <= cap and self.N % t == 0 and self.Th % t == 0))

    @property
    def Tx(self):
        return self.Bl * self.N

    @property
    def Th(self):
        return self.Bl * self.M

    @property
    def T(self):
        return self.Tx + self.Th

    @property
    def ntx(self):
        return self.Tx // self.TM

    @property
    def nt(self):
        return self.T // self.TM

    @property
    def tps(self):
        return self.N // self.TM

    @property
    def G(self):
        return self.Bl + 1


def _grp(i, dm, tm=None):
    tm = dm.TM if tm is None else tm
    return jnp.where(i < dm.Tx // tm, i // (dm.N // tm), dm.Bl)


def _first_of_group(i, dm, tm=None):
    tm = dm.TM if tm is None else tm
    return jnp.where(i < dm.Tx // tm, i % (dm.N // tm) == 0, i == dm.Tx // tm)


def _contraction_tile(rows):
    return max(t for t in (1024, 512, 256, 128) if rows % t == 0)


def _ffn_tile(dm):
    return max(t for t in (512, 256, 128) if dm.N % t == 0 and dm.Th % t == 0)


def _acc(ref, idx, val, first):
    @pl.when(first)
    def _():
        ref[idx] = val

    @pl.when(jnp.logical_not(first))
    def _():
        ref[idx] += val


def _modulate(x, gain, shift, scale):
    y = x * lax.rsqrt(jnp.mean(x * x, axis=-1, keepdims=True) + EPS)
    return (y * gain) * (1.0 + scale) + shift


def _silu(x):
    return x * jax.nn.sigmoid(x)


@functools.partial(jax.custom_vjp, nondiff_argnums=(1,))
def _shift_rows(a, k):
    n = a.shape[0]
    if k == 0:
        return a
    r = lax.broadcasted_iota(jnp.int32, a.shape, 0)
    rolled = pltpu.roll(a, (-k) % n, 0)
    ok = (r + k >= 0) & (r + k < n)
    return jnp.where(ok, rolled, 0.0)


def _shift_rows_fwd(a, k):
    return _shift_rows(a, k), None


def _shift_rows_bwd(k, _, d):
    return (_shift_rows(d, -k),)


_shift_rows.defvjp(_shift_rows_fwd, _shift_rows_bwd)


@functools.partial(jax.custom_vjp, nondiff_argnums=(1,))
def _roll_lanes(a, s):
    return pltpu.roll(a, s % LANE, 1)


def _roll_lanes_fwd(a, s):
    return _roll_lanes(a, s), None


def _roll_lanes_bwd(s, _, d):
    return (_roll_lanes(d, -s),)


_roll_lanes.defvjp(_roll_lanes_fwd, _roll_lanes_bwd)


def _rope(t, cs, sneg, spos):
    return t * cs + _roll_lanes(t, 96) * sneg + _roll_lanes(t, 32) * spos


def _rope_tables(dm):
    rows = dm.N // GRID_W
    row = jnp.repeat(jnp.arange(rows), GRID_W).astype(F32)
    col = jnp.tile(jnp.arange(GRID_W), rows).astype(F32)
    half = HEAD // 2
    inv_freq = jnp.power(ROPE_THETA, -jnp.arange(0, half, 2, dtype=F32) / half)
    ar, ac = row[:, None] * inv_freq, col[:, None] * inv_freq
    cs = jnp.concatenate([jnp.cos(ar), jnp.cos(ar), jnp.cos(ac), jnp.cos(ac)], axis=1)
    z = jnp.zeros_like(ar)
    sneg = jnp.concatenate([-jnp.sin(ar), z, -jnp.sin(ac), z], axis=1)
    spos = jnp.concatenate([z, jnp.sin(ar), z, jnp.sin(ac)], axis=1)
    pad1 = jnp.ones((dm.TM, HEAD), F32)
    pad0 = jnp.zeros((dm.TM, HEAD), F32)
    return (jnp.concatenate([cs, pad1], 0), jnp.concatenate([sneg, pad0], 0), jnp.concatenate([spos, pad0], 0))


def all_gather(xs, name):
    n = len(xs)

    def body(*refs):
        x_refs, out_refs = refs[:n], refs[n:2 * n]
        send_sems, recv_sems, local_sems = refs[2 * n:]
        x, y, c = lax.axis_index("x"), lax.axis_index("y"), lax.axis_index("c")
        me, sibling = (x, y, c), (x, y, 1 - c)
        chips = [(1 - x, y), (x, 1 - y), (1 - x, 1 - y)]

        def slot(a, px, py, pc):
            return out_refs[a].at[4 * px + 2 * py + pc]

        def copy(a, k, block, to, src=None):
            return pltpu.make_async_remote_copy(
                src_ref=slot(a, *block) if src is None else src, dst_ref=slot(a, *block),
                send_sem=send_sems.at[7 * a + k], recv_sem=recv_sems.at[7 * a + k],
                device_id=to, device_id_type=pl.DeviceIdType.MESH)

        mine = [pltpu.make_async_copy(x_refs[a], slot(a, *me), local_sems.at[a]) for a in range(n)]
        for cp in mine:
            cp.start()
        first = []
        for a in range(n):
            first.append(copy(a, 0, me, sibling, src=x_refs[a]))
            first += [copy(a, 1 + j, me, (*chip, c), src=x_refs[a]) for j, chip in enumerate(chips)]
        for cp in first:
            cp.start()
        passed = []
        for j, chip in enumerate(chips):
            for a in range(n):
                copy(a, 1 + j, (*chip, c), me).wait_recv()
                cp = copy(a, 4 + j, (*chip, c), sibling)
                cp.start()
                passed.append(cp)
        for a in range(n):
            copy(a, 0, sibling, me).wait_recv()
            for j, chip in enumerate(chips):
                copy(a, 4 + j, (*chip, 1 - c), me).wait_recv()
        for cp in first + passed:
            cp.wait_send()
        for cp in mine:
            cp.wait()

    anyspec = pl.BlockSpec(memory_space=pl.ANY)
    return pl.pallas_call(
        body, name=name, out_shape=[_sds((N_DEV,) + a.shape, a.dtype) for a in xs],
        in_specs=[anyspec] * n, out_specs=[anyspec] * n,
        scratch_shapes=[pltpu.SemaphoreType.DMA((7 * n,)), pltpu.SemaphoreType.DMA((7 * n,)),
                        pltpu.SemaphoreType.DMA((n,))],
    )(*xs)


def scatter_blocks(xs, name):
    n = len(xs)
    flips = [(0, 0, 1), (0, 1, 0), (0, 1, 1), (1, 0, 0), (1, 0, 1), (1, 1, 0), (1, 1, 1)]

    def body(*refs):
        x_refs, out_refs = refs[:n], refs[n:2 * n]
        send_sems, recv_sems, local_sems = refs[2 * n:]
        x, y, c = lax.axis_index("x"), lax.axis_index("y"), lax.axis_index("c")
        me = 4 * x + 2 * y + c

        def peer(f):
            return tuple(1 - v if d else v for v, d in zip((x, y, c), f))

        def lin(p):
            return 4 * p[0] + 2 * p[1] + p[2]

        mine = [pltpu.make_async_copy(x_refs[a].at[me], out_refs[a].at[me], local_sems.at[a]) for a in range(n)]
        for cp in mine:
            cp.start()
        copies = []
        for k, f in enumerate(flips):
            p = peer(f)
            for a in range(n):
                copies.append(pltpu.make_async_remote_copy(
                    src_ref=x_refs[a].at[lin(p)], dst_ref=out_refs[a].at[me],
                    send_sem=send_sems.at[7 * a + k], recv_sem=recv_sems.at[7 * a + k],
                    device_id=p, device_id_type=pl.DeviceIdType.MESH))
        for cp in copies:
            cp.start()
        for cp in copies:
            cp.wait_send()
            cp.wait_recv()
        for cp in mine:
            cp.wait()

    anyspec = pl.BlockSpec(memory_space=pl.ANY)
    return pl.pallas_call(
        body, name=name, out_shape=[_sds(a.shape, a.dtype) for a in xs],
        in_specs=[anyspec] * n, out_specs=[anyspec] * n,
        scratch_shapes=[pltpu.SemaphoreType.DMA((7 * n,)), pltpu.SemaphoreType.DMA((7 * n,)),
                        pltpu.SemaphoreType.DMA((n,))],
    )(*xs)


def _mod_spec(dm, nidx, tm=None):
    if nidx == 1:
        return pl.BlockSpec((1, 9, dm.D), lambda i: (_grp(i, dm, tm), 0, 0))
    return pl.BlockSpec((1, 9, dm.D), lambda i, k: (_grp(i, dm, tm), 0, 0))


def _wspec(shape5, l, s, ks):
    return pl.BlockSpec((ks, 1, 1) + tuple(shape5[3:]), lambda i, k: (k, l, s, 0, 0))


FFN_FWD_SHARDS = 4
FFN_BWD_SHARDS = 2


def ffn_fwd(X, MOD, gain, gwg, gwu, gwd, l, s, s0, dm, all_rows, name):
    D = dm.D
    tm = _ffn_tile(dm)
    rows = dm.T if all_rows else dm.Tx
    ks = FFN_FWD_SHARDS
    nk = N_DEV // ks

    def body(x_ref, m_ref, g_ref, wg_ref, wu_ref, wd_ref, xo_ref, y_ref, xn_s, acc_s):
        k = pl.program_id(1)

        @pl.when(k == 0)
        def _():
            m = m_ref[0]
            xn = _modulate(x_ref[...], g_ref[...], m[s0:s0 + 1], m[s0 + 1:s0 + 2])
            xn_s[...] = xn.astype(BF16)
            acc_s[...] = jnp.zeros_like(acc_s)

        xn = xn_s[...]
        y = None
        for j in range(ks):
            g = jnp.dot(xn, wg_ref[j, 0, 0], preferred_element_type=F32)
            u = jnp.dot(xn, wu_ref[j, 0, 0], preferred_element_type=F32)
            yj = jnp.dot((_silu(g) * u).astype(BF16), wd_ref[j, 0, 0], preferred_element_type=F32)
            y = yj if y is None else y + yj
        acc_s[...] += y

        @pl.when(k == nk - 1)
        def _():
            m = m_ref[0]
            y = acc_s[...]
            y_ref[...] = y
            xo_ref[...] = x_ref[...] + (0.5 * m[s0 + 2:s0 + 3]) * y

    row = pl.BlockSpec((tm, D), lambda i, k: (i, 0))
    return pl.pallas_call(
        body, name=name, grid=(rows // tm, nk),
        in_specs=[row, _mod_spec(dm, 2, tm), pl.BlockSpec((1, D), lambda i, k: (0, 0)),
                  _wspec(gwg.shape, l, s, ks), _wspec(gwu.shape, l, s, ks), _wspec(gwd.shape, l, s, ks)],
        out_specs=[row, row],
        out_shape=[_sds((rows, D), F32), _sds((rows, D), F32)],
        scratch_shapes=[pltpu.VMEM((tm, D), BF16), pltpu.VMEM((tm, D), F32)],
        compiler_params=_cp(),
    )(X, MOD, gain, gwg, gwu, gwd)


def ffn_bwd(X, dXo, Y, MOD, gain, gwg, gwu, gwd, l, s, s0, dm, all_rows, name):
    D = dm.D
    tm = _ffn_tile(dm)
    rows = dm.T if all_rows else dm.Tx
    ks = FFN_BWD_SHARDS
    nk = N_DEV // ks
    nf = gwg.shape[4]
    ngr = dm.G if all_rows else dm.Bl

    def body(x_ref, dxo_ref, y_ref, m_ref, g_ref, wg_ref, wu_ref, wd_ref,
             dxi_ref, xn_ref, do_ref, h_ref, dg_ref, du_ref, dm_ref, dgain_ref, xn_s, do_s, dxn_s):
        i, k = pl.program_id(0), pl.program_id(1)

        @pl.when(k == 0)
        def _():
            m = m_ref[0]
            xn = _modulate(x_ref[...], g_ref[...], m[s0:s0 + 1], m[s0 + 1:s0 + 2])
            xn_s[...] = xn.astype(BF16)
            do_s[...] = ((0.5 * m[s0 + 2:s0 + 3]) * dxo_ref[...]).astype(BF16)
            dxn_s[...] = jnp.zeros_like(dxn_s)

        xn, do = xn_s[...], do_s[...]
        dxn = None
        for j in range(ks):
            wg, wu = wg_ref[j, 0, 0], wu_ref[j, 0, 0]
            g = jnp.dot(xn, wg, preferred_element_type=F32)
            u = jnp.dot(xn, wu, preferred_element_type=F32)
            sg = jax.nn.sigmoid(g)
            si = g * sg
            dh = _dot_nt(do, wd_ref[j, 0, 0])
            dg = (dh * u * (sg * (1.0 + g * (1.0 - sg)))).astype(BF16)
            du = (dh * si).astype(BF16)
            dj = _dot_nt(dg, wg) + _dot_nt(du, wu)
            dxn = dj if dxn is None else dxn + dj
            h_ref[j] = (si * u).astype(BF16)
            dg_ref[j] = dg
            du_ref[j] = du
        dxn_s[...] += dxn

        @pl.when(k == nk - 1)
        def _():
            m = m_ref[0]
            _, vjp = jax.vjp(_modulate, x_ref[...], g_ref[...], m[s0:s0 + 1], m[s0 + 1:s0 + 2])
            dx, dgain, dshift, dscale = vjp(dxn_s[...])
            dxo = dxo_ref[...]
            dxi_ref[...] = dxo + dx
            xn_ref[...] = xn_s[...]
            do_ref[...] = do_s[...]
            dgate = jnp.sum(0.5 * dxo * y_ref[...], axis=0, keepdims=True)
            first = _first_of_group(i, dm, tm)
            _acc(dm_ref, (0, pl.ds(0, 1), slice(None)), dshift, first)
            _acc(dm_ref, (0, pl.ds(1, 1), slice(None)), dscale, first)
            _acc(dm_ref, (0, pl.ds(2, 1), slice(None)), dgate, first)
            _acc(dgain_ref, (slice(None), slice(None)), dgain, i == 0)

    row = pl.BlockSpec((tm, D), lambda i, k: (i, 0))
    slab = pl.BlockSpec((ks, tm, nf), lambda i, k: (k, i, 0))
    return pl.pallas_call(
        body, name=name, grid=(rows // tm, nk),
        in_specs=[row, row, row, _mod_spec(dm, 2, tm), pl.BlockSpec((1, D), lambda i, k: (0, 0)),
                  _wspec(gwg.shape, l, s, ks), _wspec(gwu.shape, l, s, ks), _wspec(gwd.shape, l, s, ks)],
        out_specs=[row, row, row, slab, slab, slab,
                   pl.BlockSpec((1, 3, D), lambda i, k: (_grp(i, dm, tm), 0, 0)),
                   pl.BlockSpec((1, D), lambda i, k: (0, 0))],
        out_shape=[_sds((rows, D), F32), _sds((rows, D), BF16), _sds((rows, D), BF16),
                   _sds((N_DEV, rows, nf), BF16), _sds((N_DEV, rows, nf), BF16), _sds((N_DEV, rows, nf), BF16),
                   _sds((ngr, 3, D), F32), _sds((1, D), F32)],
        scratch_shapes=[pltpu.VMEM((tm, D), BF16), pltpu.VMEM((tm, D), BF16), pltpu.VMEM((tm, D), F32)],
        compiler_params=_cp(),
    )(X, dXo, Y, MOD, gain, gwg, gwu, gwd)


def ffn_dw(XN, DO, H, DG, DU, dm, name):
    rows, D = XN.shape
    nf = H.shape[2]
    tt = _contraction_tile(rows)
    nT = rows // tt

    def body(xn_ref, do_ref, h_ref, dg_ref, du_ref, dwg_ref, dwu_ref, dwd_ref, ag_s, au_s, ad_s):
        t = pl.program_id(1)

        @pl.when(t == 0)
        def _():
            ag_s[...] = jnp.zeros_like(ag_s)
            au_s[...] = jnp.zeros_like(au_s)
            ad_s[...] = jnp.zeros_like(ad_s)

        xn = xn_ref[...]
        ag_s[...] += _dot_tn(xn, dg_ref[0])
        au_s[...] += _dot_tn(xn, du_ref[0])
        ad_s[...] += _dot_tn(h_ref[0], do_ref[...])

        @pl.when(t == nT - 1)
        def _():
            dwg_ref[0] = ag_s[...].astype(BF16)
            dwu_ref[0] = au_s[...].astype(BF16)
            dwd_ref[0] = ad_s[...].astype(BF16)

    row = pl.BlockSpec((tt, D), lambda k, t: (t, 0))
    slab = pl.BlockSpec((1, tt, nf), lambda k, t: (k, t, 0))
    return pl.pallas_call(
        body, name=name, grid=(N_DEV, nT),
        in_specs=[row, row, slab, slab, slab],
        out_specs=[pl.BlockSpec((1, D, nf), lambda k, t: (k, 0, 0)), pl.BlockSpec((1, D, nf), lambda k, t: (k, 0, 0)),
                   pl.BlockSpec((1, nf, D), lambda k, t: (k, 0, 0))],
        out_shape=[_sds((N_DEV, D, nf), BF16), _sds((N_DEV, D, nf), BF16), _sds((N_DEV, nf, D), BF16)],
        scratch_shapes=[pltpu.VMEM((D, nf), F32), pltpu.VMEM((D, nf), F32), pltpu.VMEM((nf, D), F32)],
        compiler_params=_cp(),
    )(XN, DO, H, DG, DU)


def atb(A, B, rows, dm, name):
    Ka, Nb = A.shape[1], B.shape[1]
    tk, tn = _pick(Ka, 1024), _pick(Nb, 1536)
    tt = _contraction_tile(rows)
    nT = rows // tt

    def body(a_ref, b_ref, o_ref, acc_s):
        t = pl.program_id(2)

        @pl.when(t == 0)
        def _():
            acc_s[...] = jnp.zeros_like(acc_s)

        acc_s[...] += _dot_tn(a_ref[...], b_ref[...])

        @pl.when(t == nT - 1)
        def _():
            o_ref[...] = acc_s[...].astype(BF16)

    return pl.pallas_call(
        body, name=name, grid=(Ka // tk, Nb // tn, nT),
        in_specs=[pl.BlockSpec((tt, tk), lambda i, j, t: (t, i)), pl.BlockSpec((tt, tn), lambda i, j, t: (t, j))],
        out_specs=pl.BlockSpec((tk, tn), lambda i, j, t: (i, j)),
        out_shape=_sds((Ka, Nb), BF16), scratch_shapes=[pltpu.VMEM((tk, tn), F32)], compiler_params=_cp(),
    )(A, B)


def modmm(X, MOD, gain, W, s0, dm, name):
    TM, D = dm.TM, dm.D
    Nc = W.shape[1]
    tn = _pick(Nc, 1536)
    nj = Nc // tn

    def body(x_ref, m_ref, g_ref, w_ref, p_ref, xn_ref):
        @pl.when(pl.program_id(1) == 0)
        def _():
            m = m_ref[0]
            xn_ref[...] = _modulate(x_ref[...], g_ref[...], m[s0:s0 + 1], m[s0 + 1:s0 + 2]).astype(BF16)

        p_ref[...] = jnp.dot(xn_ref[...], w_ref[...], preferred_element_type=F32)

    row = pl.BlockSpec((TM, D), lambda i, j: (i, 0))
    return pl.pallas_call(
        body, name=name, grid=(dm.nt, nj),
        in_specs=[row, _mod_spec(dm, 2), pl.BlockSpec((1, D), lambda i, j: (0, 0)),
                  pl.BlockSpec((D, tn), lambda i, j: (0, j))],
        out_specs=[pl.BlockSpec((TM, tn), lambda i, j: (i, j)), row],
        out_shape=[_sds((dm.T, Nc), F32), _sds((dm.T, D), BF16)],
        compiler_params=_cp(),
    )(X, MOD, gain, W)


def mixin_bwd(dP, W, X, dXres, MOD, gain, s0, dm, name):
    TM, D = dm.TM, dm.D
    K = dP.shape[1]

    def body(dp_ref, w_ref, x_ref, dr_ref, m_ref, g_ref, dx_ref, dm_ref, dgain_ref):
        i = pl.program_id(0)
        dxn = _dot_nt(dp_ref[...], w_ref[...])
        m = m_ref[0]
        _, vjp = jax.vjp(_modulate, x_ref[...], g_ref[...], m[s0:s0 + 1], m[s0 + 1:s0 + 2])
        dx, dgain, dshift, dscale = vjp(dxn)
        dx_ref[...] = dr_ref[...] + dx
        first = _first_of_group(i, dm)
        _acc(dm_ref, (0, pl.ds(0, 1), slice(None)), dshift, first)
        _acc(dm_ref, (0, pl.ds(1, 1), slice(None)), dscale, first)
        _acc(dgain_ref, (slice(None), slice(None)), dgain, i == 0)

    row = pl.BlockSpec((TM, D), lambda i: (i, 0))
    return pl.pallas_call(
        body, name=name, grid=(dm.nt,),
        in_specs=[pl.BlockSpec((TM, K), lambda i: (i, 0)), pl.BlockSpec((D, K), lambda i: (0, 0)), row, row,
                  _mod_spec(dm, 1), pl.BlockSpec((1, D), lambda i: (0, 0))],
        out_specs=[row, pl.BlockSpec((1, 2, D), lambda i: (_grp(i, dm), 0, 0)), pl.BlockSpec((1, D), lambda i: (0, 0))],
        out_shape=[_sds((dm.T, D), F32), _sds((dm.G, 2, D), F32), _sds((1, D), F32)],
        compiler_params=_cp(),
    )(dP, W, X, dXres, MOD, gain)


def proj_res(As, Ws, X, MOD, dm, ntiles, name):
    TM, D = dm.TM, dm.D
    n = len(As)
    rows = ntiles * TM

    def body(*refs):
        a_refs, w_refs = refs[:n], refs[n:2 * n]
        x_ref, m_ref, xo_ref, y_ref = refs[2 * n:]
        y = jnp.dot(a_refs[0][...], w_refs[0][...], preferred_element_type=F32)
        for a, w in zip(a_refs[1:], w_refs[1:]):
            y += jnp.dot(a[...], w[...], preferred_element_type=F32)
        y_ref[...] = y
        xo_ref[...] = x_ref[...] + m_ref[0][5:6] * y

    row = pl.BlockSpec((TM, D), lambda i: (i, 0))
    return pl.pallas_call(
        body, name=name, grid=(ntiles,),
        in_specs=[pl.BlockSpec((TM, a.shape[1]), lambda i: (i, 0)) for a in As]
        + [pl.BlockSpec(w.shape, lambda i: (0, 0)) for w in Ws] + [row, _mod_spec(dm, 1)],
        out_specs=[row, row], out_shape=[_sds((rows, D), F32), _sds((rows, D), F32)],
        compiler_params=_cp(),
    )(*As, *Ws, X, MOD)


def proj_res_bwd(dXo, Y, MOD, Ws, dm, ntiles, name):
    TM, D = dm.TM, dm.D
    n = len(Ws)
    rows = ntiles * TM
    ngr = dm.G if ntiles == dm.nt else dm.Bl

    def body(*refs):
        dxo_ref, y_ref, m_ref = refs[:3]
        w_refs = refs[3:3 + n]
        dy_ref = refs[3 + n]
        da_refs = refs[4 + n:4 + 2 * n]
        dgate_ref = refs[4 + 2 * n]
        i = pl.program_id(0)
        dxo = dxo_ref[...]
        dy = (m_ref[0][5:6] * dxo).astype(BF16)
        dy_ref[...] = dy
        for w, da in zip(w_refs, da_refs):
            da[...] = _dot_nt(dy, w[...])
        dgate = jnp.sum(dxo * y_ref[...], axis=0, keepdims=True)
        _acc(dgate_ref, (0, slice(None), slice(None)), dgate, _first_of_group(i, dm))

    row = pl.BlockSpec((TM, D), lambda i: (i, 0))
    return pl.pallas_call(
        body, name=name, grid=(ntiles,),
        in_specs=[row, row, _mod_spec(dm, 1)] + [pl.BlockSpec(w.shape, lambda i: (0, 0)) for w in Ws],
        out_specs=[row] + [pl.BlockSpec((TM, w.shape[0]), lambda i: (i, 0)) for w in Ws]
        + [pl.BlockSpec((1, 1, D), lambda i: (_grp(i, dm), 0, 0))],
        out_shape=[_sds((rows, D), BF16)] + [_sds((rows, w.shape[0]), F32) for w in Ws] + [_sds((ngr, 1, D), F32)],
        compiler_params=_cp(),
    )(dXo, Y, MOD, *Ws)


def loss_head(Xf, target, dm, name):
    TM, D = dm.TM, dm.D

    def body(x_ref, t_ref, l_ref, dx_ref, acc_s):
        i = pl.program_id(0)
        e = x_ref[...] - t_ref[...]
        dx_ref[...] = e * (1.0 / D)

        @pl.when(i == 0)
        def _():
            acc_s[...] = jnp.zeros_like(acc_s)

        acc_s[...] += jnp.sum(e * e, axis=0, keepdims=True)

        @pl.when(i == dm.ntx - 1)
        def _():
            tot = jnp.sum(acc_s[...], axis=1, keepdims=True) * (0.5 / D)
            l_ref[...] = jnp.broadcast_to(tot, (1, LANE))

    row = pl.BlockSpec((TM, D), lambda i: (i, 0))
    return pl.pallas_call(
        body, name=name, grid=(dm.ntx,), in_specs=[row, row],
        out_specs=[pl.BlockSpec((1, LANE), lambda i: (0, 0)), row],
        out_shape=[_sds((1, LANE), F32), _sds((dm.Tx, D), F32)],
        scratch_shapes=[pltpu.VMEM((1, D), F32)], compiler_params=_cp(),
    )(Xf, target)


def _qk_fn(p, gain, cs, sneg, spos):
    y = p * lax.rsqrt(jnp.mean(p * p, axis=-1, keepdims=True) + EPS) * gain
    return _rope(y, cs, sneg, spos)


def _tab_specs(dm, swap):
    def idx(i):
        return jnp.where(i < dm.ntx, i % dm.tps, dm.tps)
    if swap:
        return [pl.BlockSpec((dm.TM, HEAD), lambda j, i: (idx(i), 0))] * 3
    return [pl.BlockSpec((dm.TM, HEAD), lambda i, j: (idx(i), 0))] * 3


def qkv_prep(P0, qkg, tabs, dm, name):
    TM = dm.TM

    def body(p_ref, g_ref, cs_ref, sn_ref, sp_ref, o_ref):
        j = pl.program_id(1)

        @pl.when(j < 6)
        def _():
            o_ref[...] = _qk_fn(p_ref[...], g_ref[0], cs_ref[...], sn_ref[...], sp_ref[...]).astype(BF16)

        @pl.when(j >= 6)
        def _():
            o_ref[...] = p_ref[...].astype(BF16)

    blk = pl.BlockSpec((TM, HEAD), lambda i, j: (i, j))
    return pl.pallas_call(
        body, name=name, grid=(dm.nt, 8),
        in_specs=[blk, pl.BlockSpec((1, 1, HEAD), lambda i, j: (jnp.minimum(j // 4, 1), 0, 0))] + _tab_specs(dm, False),
        out_specs=blk, out_shape=_sds((dm.T, 8 * HEAD), BF16), compiler_params=_cp(),
    )(P0, qkg, *tabs)


def qkv_prep_bwd(P0, dQKV, qkg, tabs, dm, name):
    TM = dm.TM

    def body(p_ref, d_ref, g_ref, cs_ref, sn_ref, sp_ref, dp_ref, dg_ref):
        j, i = pl.program_id(0), pl.program_id(1)
        first = (i == 0) & ((j == 0) | (j == 4))

        @pl.when(j < 6)
        def _():
            _, vjp = jax.vjp(_qk_fn, p_ref[...], g_ref[0], cs_ref[...], sn_ref[...], sp_ref[...])
            dp, dg = vjp(d_ref[...])[:2]
            dp_ref[...] = dp
            _acc(dg_ref, (0, slice(None), slice(None)), dg, first)

        @pl.when(j >= 6)
        def _():
            dp_ref[...] = d_ref[...]

    blk = pl.BlockSpec((TM, HEAD), lambda j, i: (i, j))
    return pl.pallas_call(
        body, name=name, grid=(8, dm.nt),
        in_specs=[blk, blk, pl.BlockSpec((1, 1, HEAD), lambda j, i: (jnp.minimum(j // 4, 1), 0, 0))] + _tab_specs(dm, True),
        out_specs=[blk, pl.BlockSpec((1, 1, HEAD), lambda j, i: (jnp.minimum(j // 4, 1), 0, 0))],
        out_shape=[_sds((dm.T, 8 * HEAD), F32), _sds((2, 1, HEAD), F32)], compiler_params=_cp(),
    )(P0, dQKV, qkg, *tabs)


def _softmax2(sx, sh):
    m = jnp.max(sh, axis=-1, keepdims=True)
    if sx is not None:
        m = jnp.maximum(m, jnp.max(sx, axis=-1, keepdims=True))
    eh = jnp.exp(sh - m)
    l = jnp.sum(eh, axis=-1, keepdims=True)
    ex = None
    if sx is not None:
        ex = jnp.exp(sx - m)
        l = l + jnp.sum(ex, axis=-1, keepdims=True)
    inv = 1.0 / l
    return (None if ex is None else ex * inv), eh * inv


def _attn_geometry(dm, with_x):
    TQ = dm.TM
    if with_x:
        nq, qoff = dm.N // TQ, 0
    else:
        nq, qoff = dm.M // TQ, dm.Tx // TQ
    hoff = dm.Tx // dm.M
    return TQ, nq, qoff, hoff


def attn_fwd(QKV, dm, with_x, name):
    TQ, nq, qoff, hoff = _attn_geometry(dm, with_x)
    scale = HEAD ** -0.5
    rows = dm.Tx if with_x else dm.Th

    def body(*refs):
        if with_x:
            q_ref, kh_ref, vh_ref, kx_ref, vx_ref, o_ref = refs
        else:
            q_ref, kh_ref, vh_ref, o_ref = refs
        q = q_ref[...]
        sh = _dot_nt(q, kh_ref[...]) * scale
        sx = _dot_nt(q, kx_ref[...]) * scale if with_x else None
        px, ph = _softmax2(sx, sh)
        o = _dot(ph, vh_ref[...])
        if with_x:
            o = o + _dot(px, vx_ref[...])
        o_ref[...] = o.astype(BF16)

    qs = pl.BlockSpec((TQ, HEAD), lambda b, kv, g, qi: (qoff + b * nq + qi, kv * 2 + g))
    in_specs = [qs, pl.BlockSpec((dm.M, HEAD), lambda b, kv, g, qi: (hoff + b, 4 + kv)),
                pl.BlockSpec((dm.M, HEAD), lambda b, kv, g, qi: (hoff + b, 6 + kv))]
    args = [QKV, QKV, QKV]
    if with_x:
        in_specs += [pl.BlockSpec((dm.N, HEAD), lambda b, kv, g, qi: (b, 4 + kv)),
                     pl.BlockSpec((dm.N, HEAD), lambda b, kv, g, qi: (b, 6 + kv))]
        args += [QKV, QKV]
    return pl.pallas_call(
        body, name=name, grid=(dm.Bl, A_KV, 2, nq), in_specs=in_specs,
        out_specs=pl.BlockSpec((TQ, HEAD), lambda b, kv, g, qi: (b * nq + qi, kv * 2 + g)),
        out_shape=_sds((rows, A_HEADS * HEAD), BF16), compiler_params=_cp(),
    )(*args)


def attn_bwd(QKV, dO, dm, with_x, init, name):
    TQ, nq, qoff, hoff = _attn_geometry(dm, with_x)
    scale = HEAD ** -0.5
    rows = dm.Tx if with_x else dm.Th

    def body(*refs):
        if with_x:
            (q_ref, kh_ref, vh_ref, kx_ref, vx_ref, do_ref, ikh_ref, ivh_ref,
             dq_ref, dkh_ref, dvh_ref, dkx_ref, dvx_ref) = refs
        else:
            q_ref, kh_ref, vh_ref, do_ref, dq_ref, dkh_ref, dvh_ref = refs
        g, qi = pl.program_id(2), pl.program_id(3)
        q = q_ref[...]
        kh, vh = kh_ref[...], vh_ref[...]
        sh = _dot_nt(q, kh) * scale
        sx = _dot_nt(q, kx_ref[...]) * scale if with_x else None
        px, ph = _softmax2(sx, sh)
        dob = do_ref[...].astype(BF16)
        dph = _dot_nt(dob, vh)
        delta = jnp.sum(dph * ph, axis=-1, keepdims=True)
        if with_x:
            dpx = _dot_nt(dob, vx_ref[...])
            delta = delta + jnp.sum(dpx * px, axis=-1, keepdims=True)
        dsh = (ph * (dph - delta) * scale).astype(BF16)
        dq = _dot(dsh, kh)
        first = (g == 0) & (qi == 0)

        @pl.when(first)
        def _():
            if with_x:
                dkh_ref[...] = ikh_ref[...]
                dvh_ref[...] = ivh_ref[...]
                dkx_ref[...] = jnp.zeros_like(dkx_ref)
                dvx_ref[...] = jnp.zeros_like(dvx_ref)
            else:
                dkh_ref[...] = jnp.zeros_like(dkh_ref)
                dvh_ref[...] = jnp.zeros_like(dvh_ref)

        dkh_ref[...] += _dot_tn(dsh, q)
        dvh_ref[...] += _dot_tn(ph, dob)
        if with_x:
            dsx = (px * (dpx - delta) * scale).astype(BF16)
            dq = dq + _dot(dsx, kx_ref[...])
            dkx_ref[...] += _dot_tn(dsx, q)
            dvx_ref[...] += _dot_tn(px, dob)
        dq_ref[...] = dq

    qs = pl.BlockSpec((TQ, HEAD), lambda b, kv, g, qi: (qoff + b * nq + qi, kv * 2 + g))
    hs = lambda c0: pl.BlockSpec((dm.M, HEAD), lambda b, kv, g, qi: (hoff + b, c0 + kv))
    xs = lambda c0: pl.BlockSpec((dm.N, HEAD), lambda b, kv, g, qi: (b, c0 + kv))
    dos = pl.BlockSpec((TQ, HEAD), lambda b, kv, g, qi: (b * nq + qi, kv * 2 + g))
    acc_h = pl.BlockSpec((dm.M, HEAD), lambda b, kv, g, qi: (b, kv))
    acc_x = pl.BlockSpec((dm.N, HEAD), lambda b, kv, g, qi: (b, kv))
    in_specs, args = [qs, hs(4), hs(6)], [QKV, QKV, QKV]
    out_specs = [dos, acc_h, acc_h]
    out_shape = [_sds((rows, A_HEADS * HEAD), F32), _sds((dm.Th, A_KV * HEAD), F32), _sds((dm.Th, A_KV * HEAD), F32)]
    if with_x:
        in_specs += [xs(4), xs(6), dos, acc_h, acc_h]
        args += [QKV, QKV, dO, init[0], init[1]]
        out_specs += [acc_x, acc_x]
        out_shape += [_sds((dm.Tx, A_KV * HEAD), F32), _sds((dm.Tx, A_KV * HEAD), F32)]
    else:
        in_specs += [dos]
        args += [dO]
    return pl.pallas_call(
        body, name=name, grid=(dm.Bl, A_KV, 2, nq), in_specs=in_specs, out_specs=out_specs,
        out_shape=out_shape, compiler_params=_cp(),
    )(*args)


def _pool_mean(u, w):
    n = u.shape[0]
    t = lax.broadcasted_iota(jnp.int32, (n, 1), 0)
    cnt = (jnp.clip(t + (w - w // 2), 0, n) - jnp.clip(t - w // 2, 0, n)).astype(F32)
    s = _shift_rows(u, -(w // 2))
    for j in range(-(w // 2) + 1, w - w // 2):
        s = s + _shift_rows(u, j)
    return s / cnt - u


def pool_fwd(P0, pw, pscale, dm, on_x, name):
    n, off, rows = (dm.N, 0, dm.Tx) if on_x else (dm.M, dm.Tx // dm.M, dm.Th)
    ng = len(POOL_WINDOWS)

    def body(u_ref, w_ref, s_ref, o_ref):
        for g, w in enumerate(POOL_WINDOWS):
            cols = pl.ds(g * HEAD, HEAD)
            pooled = _pool_mean(u_ref[:, cols], w)
            o_ref[:, cols] = (_dot(pooled, w_ref[g]) * s_ref[:, cols]).astype(BF16)

    return pl.pallas_call(
        body, name=name, grid=(dm.Bl,),
        in_specs=[pl.BlockSpec((n, ng * HEAD), lambda b: (off + b, 2)),
                  pl.BlockSpec((ng, HEAD, HEAD), lambda b: (0, 0, 0)), pl.BlockSpec((1, ng * HEAD), lambda b: (0, 0))],
        out_specs=pl.BlockSpec((n, ng * HEAD), lambda b: (b, 0)),
        out_shape=_sds((rows, ng * HEAD), BF16), compiler_params=_cp(),
    )(P0, pw, pscale)


def pool_bwd(P0, dY, pw, pwT, pscale, dm, on_x, name):
    n, off, rows = (dm.N, 0, dm.Tx) if on_x else (dm.M, dm.Tx // dm.M, dm.Th)
    ng = len(POOL_WINDOWS)

    def body(u_ref, dy_ref, w_ref, wt_ref, s_ref, du_ref, dw_ref, ds_ref):
        b = pl.program_id(0)
        for g, w in enumerate(POOL_WINDOWS):
            cols = pl.ds(g * HEAD, HEAD)
            pooled, vjp = jax.vjp(lambda u: _pool_mean(u, w), u_ref[:, cols])
            pre = _dot(pooled, w_ref[g])
            dy = dy_ref[:, cols]
            dpre = dy * s_ref[:, cols]
            du_ref[:, cols] = vjp(_dot(dpre, wt_ref[g]))[0]
            _acc(dw_ref, (g, slice(None), slice(None)), _dot_tn(pooled, dpre), b == 0)
            _acc(ds_ref, (slice(None), cols), jnp.sum(dy * pre, axis=0, keepdims=True), b == 0)

    full = pl.BlockSpec((ng, HEAD, HEAD), lambda b: (0, 0, 0))
    vec = pl.BlockSpec((1, ng * HEAD), lambda b: (0, 0))
    return pl.pallas_call(
        body, name=name, grid=(dm.Bl,),
        in_specs=[pl.BlockSpec((n, ng * HEAD), lambda b: (off + b, 2)), pl.BlockSpec((n, ng * HEAD), lambda b: (b, 0)),
                  full, full, vec],
        out_specs=[pl.BlockSpec((n, ng * HEAD), lambda b: (b, 0)), full, vec],
        out_shape=[_sds((rows, ng * HEAD), F32), _sds((ng, HEAD, HEAD), F32), _sds((1, ng * HEAD), F32)],
        compiler_params=_cp(),
    )(P0, dY, pw, pwT, pscale)


def _conv_fn(p, w0, w1, w2, kind):
    c = w0 * _shift_rows(p, -1) + w1 * p + w2 * _shift_rows(p, 1)
    a = _silu(c)
    if kind == 2:
        return a
    a = a * lax.rsqrt(jnp.sum(a * a, axis=-1, keepdims=True) + EPS)
    return a * (HEAD ** -0.5) if kind == 0 else a


def gdn_prep(P1, conv_w, dm, on_x, name):
    n, off, rows = (dm.N, 0, dm.Tx) if on_x else (dm.M, dm.Tx // dm.M, dm.Th)

    def body(p_ref, w_ref, o_ref):
        j = pl.program_id(1)
        p, w = p_ref[...], w_ref[...]
        for kind in range(3):
            @pl.when(j // C_HEADS == kind)
            def _():
                o_ref[...] = _conv_fn(p, w[0:1], w[1:2], w[2:3], kind)

    return pl.pallas_call(
        body, name=name, grid=(dm.Bl, 3 * C_HEADS),
        in_specs=[pl.BlockSpec((n, HEAD), lambda b, j: (off + b, j)), pl.BlockSpec((3, HEAD), lambda b, j: (0, j))],
        out_specs=pl.BlockSpec((n, HEAD), lambda b, j: (b, j)),
        out_shape=_sds((rows, 3 * C_HEADS * HEAD), F32), compiler_params=_cp(),
    )(P1, conv_w)


def gdn_prep_bwd(P1, dQ, conv_w, dm, on_x, name):
    n, off, rows = (dm.N, 0, dm.Tx) if on_x else (dm.M, dm.Tx // dm.M, dm.Th)

    def body(p_ref, d0_ref, d1_ref, w_ref, dp_ref, dw_ref):
        j, b = pl.program_id(0), pl.program_id(1)
        p, w = p_ref[...], w_ref[...]
        for kind in range(3):
            @pl.when(j // C_HEADS == kind)
            def _():
                _, vjp = jax.vjp(functools.partial(_conv_fn, kind=kind), p, w[0:1], w[1:2], w[2:3])
                dp, d0, d1, d2 = vjp(d0_ref[0] + d1_ref[0])
                dp_ref[...] = dp
                _acc(dw_ref, (pl.ds(0, 1), slice(None)), d0, b == 0)
                _acc(dw_ref, (pl.ds(1, 1), slice(None)), d1, b == 0)
                _acc(dw_ref, (pl.ds(2, 1), slice(None)), d2, b == 0)

    return pl.pallas_call(
        body, name=name, grid=(3 * C_HEADS, dm.Bl),
        in_specs=[pl.BlockSpec((n, HEAD), lambda j, b: (off + b, j)),
                  pl.BlockSpec((1, n, HEAD), lambda j, b: (0, off + b, j)), pl.BlockSpec((1, n, HEAD), lambda j, b: (1, off + b, j)),
                  pl.BlockSpec((3, HEAD), lambda j, b: (0, j))],
        out_specs=[pl.BlockSpec((n, HEAD), lambda j, b: (b, j)), pl.BlockSpec((3, HEAD), lambda j, b: (0, j))],
        out_shape=[_sds((rows, 3 * C_HEADS * HEAD), F32), _sds((3, 3 * C_HEADS * HEAD), F32)],
        compiler_params=_cp(),
    )(P1, dQ, dQ, conv_w)


def _gate_fn(ab, par):
    lane = lax.broadcasted_iota(jnp.int32, ab.shape, 1)
    is_a = (lane % 16) < C_HEADS
    g = -jnp.exp(par[0:1]) * jax.nn.softplus(ab + par[1:2])
    return jnp.where(lane < 4 * C_HEADS, jnp.where(is_a, g, jax.nn.sigmoid(ab)), 0.0)


def _col(blk, idx):
    lane = lax.broadcasted_iota(jnp.int32, blk.shape, 1)
    return jnp.sum(jnp.where(lane == idx, blk, 0.0), axis=1, keepdims=True)


def _chunk_masks(rev):
    ii = lax.broadcasted_iota(jnp.int32, (CHUNK, CHUNK), 0)
    jj = lax.broadcasted_iota(jnp.int32, (CHUNK, CHUNK), 1)
    ahead = jnp.where(rev, jj - ii, ii - jj)
    return ahead >= 0, ahead > 0, (ii == jj).astype(F32)


def _inv_unit_tri(nmats, eye):
    xs = [eye - n for n in nmats]
    ps = [_hdot(n, n) for n in nmats]
    step = 2
    while True:
        xs = [x + _hdot(x, p) for x, p in zip(xs, ps)]
        step *= 2
        if step >= CHUNK:
            break
        ps = [_hdot(p, p) for p in ps]
    return xs


def _cum_lanes(x, transpose=False):
    lane = lax.broadcasted_iota(jnp.int32, x.shape, 1)
    down, up = x, x
    s = 1
    while s < CHUNK:
        down = down + _shift_rows(down, -s)
        up = up + _shift_rows(up, s)
        s *= 2
    return jnp.where((lane >= 16) if transpose else (lane < 16), down, up)


def _each(f, *lists):
    return [f(*a) for a in zip(*lists)]


def _chunk_common(qs, ks, vs, gcs, gcrs, tots, betas, rev, saved=None):
    incl, strict, eye = _chunk_masks(rev)
    es = _each(lambda gc, gcr: jnp.exp(jnp.where(incl, gc - gcr, NEG)), gcs, gcrs)
    egs = [jnp.exp(gc) for gc in gcs]
    ets = _each(lambda t, gc: jnp.exp(t - gc), tots, gcs)
    gts = [jnp.exp(t) for t in tots]
    kbs = _each(lambda k, b: k * b, ks, betas)
    kks = _each(_dot_nt, kbs, ks)
    qqs = _each(_dot_nt, qs, ks)
    if saved is None:
        nmats = _each(lambda kk, e: jnp.where(strict, kk * e, 0.0), kks, es)
        ainvs = _inv_unit_tri(nmats, eye)
        rhss = _each(lambda v, b, kb, eg: jnp.concatenate([v * b, kb * eg], axis=1), vs, betas, kbs, egs)
        sols = _each(_hdot, ainvs, rhss)
    else:
        ainvs, sols = saved
    return dict(incl=incl, strict=strict, e=es, eg=egs, et=ets, gt=gts, kb=kbs, kk=kks, ainv=ainvs, sol=sols, qq=qqs)


def _chunk_fwd(qs, ks, vs, gcs, gcrs, tots, betas, rev):
    c = _chunk_common(qs, ks, vs, gcs, gcrs, tots, betas, rev)
    incl = c["incl"]
    return _each(lambda q, k, sol, qq, e, et, eg, gt, ainv:
                 (sol[:, :HEAD], sol[:, HEAD:], k * et, q * eg, jnp.where(incl, qq * e, 0.0), gt, ainv),
                 qs, ks, c["sol"], c["qq"], c["e"], c["et"], c["eg"], c["gt"], c["ainv"])


def _chunk_bwd(qs, ks, vs, gcs, gcrs, tots, betas, rev, ainvs, sols, dus, dws, dkts, dqds, dqks, dgts):
    c = _chunk_common(qs, ks, vs, gcs, gcrs, tots, betas, rev, saved=(ainvs, sols))
    incl, strict = c["incl"], c["strict"]
    drhss = _each(lambda a, du, dw: _hdot_tn(a, jnp.concatenate([du, dw], axis=1)), c["ainv"], dus, dws)
    dns = _each(lambda drhs, sol: jnp.where(strict, -_hdot_nt(drhs, sol), 0.0), drhss, c["sol"])
    dkks = _each(lambda dn, e: dn * e, dns, c["e"])
    dqms = [jnp.where(incl, dqk, 0.0) for dqk in dqks]
    dqqs = _each(lambda dqm, e: dqm * e, dqms, c["e"])
    m_q = _each(_dot, dqqs, ks)
    m_k1 = _each(_dot_tn, dqqs, qs)
    m_k2 = _each(_dot_tn, dkks, c["kb"])
    m_kb = _each(_dot, dkks, ks)

    def finish(q, k, v, beta, e, eg, et, gt, kb, kk, qq, drhs, dn, dqm, dkt, dqd, dgt, mq, mk1, mk2, mkb):
        de = dn * kk + dqm * qq
        dq = mq + dqd * eg
        dkb = mkb + drhs[:, HEAD:] * eg
        dk = mk1 + mk2 + dkt * et + dkb * beta
        dv = drhs[:, :HEAD] * beta
        dbeta = jnp.sum(drhs[:, :HEAD] * v + dkb * k, axis=1, keepdims=True)
        deg = jnp.sum(drhs[:, HEAD:] * kb + dqd * q, axis=1, keepdims=True)
        dd = de * e
        dtd = jnp.sum(dkt * k, axis=1, keepdims=True) * et
        dgc = deg * eg - dtd + jnp.sum(dd, axis=1, keepdims=True) - jnp.sum(dd.T, axis=1, keepdims=True)
        dtot = jnp.sum(dtd, axis=0, keepdims=True) + dgt * gt
        return dq, dk, dv, dgc, dtot, dbeta

    return _each(finish, qs, ks, vs, betas, c["e"], c["eg"], c["et"], c["gt"], c["kb"], c["kk"], c["qq"],
                 drhss, dns, dqms, dkts, dqds, dgts, m_q, m_k1, m_k2, m_kb)


def gdn_chunk_pre(QKVg, P1, par, dm, name):
    nch = dm.T // CHUNK
    HD = C_HEADS * HEAD
    abcol = (4 * HD) // LANE

    def body(x_ref, ab_ref, par_ref, u_ref, w_ref, kt_ref, qd_ref, qk_ref, gt_ref, wf_ref, ai_ref, gct_s):
        d = pl.program_id(1)
        rev = d == 1
        gb = _gate_fn(ab_ref[...], par_ref[...])
        gcl = _cum_lanes(gb)
        gct_s[...] = gcl.T
        tot = jnp.sum(gb, axis=0, keepdims=True)
        hs = range(C_HEADS)
        outs = _chunk_fwd(
            [x_ref[:, pl.ds(h * HEAD, HEAD)] for h in hs],
            [x_ref[:, pl.ds((C_HEADS + h) * HEAD, HEAD)] for h in hs],
            [x_ref[:, pl.ds((2 * C_HEADS + h) * HEAD, HEAD)] for h in hs],
            [_col(gcl, d * 16 + h) for h in hs], [gct_s[pl.ds(d * 16 + h, 1), :] for h in hs],
            [_col(tot, d * 16 + h) for h in hs], [_col(gb, d * 16 + 8 + h) for h in hs], rev)
        for h, (u, w, kt, qd, qk, gt, ainv) in enumerate(outs):
            cols = pl.ds(h * HEAD, HEAD)
            u_ref[0, :, cols] = u
            w_ref[0, :, cols] = w.astype(BF16)
            kt_ref[0, :, cols] = kt.astype(BF16)
            qd_ref[0, :, cols] = qd.astype(BF16)
            qk_ref[0, :, cols] = jnp.concatenate([qk, jnp.zeros_like(qk)], axis=1).astype(BF16)
            gt_ref[0, 0, pl.ds(h, 1), :] = jnp.broadcast_to(gt, (1, HEAD))
            wf_ref[0, :, cols] = w
            ai_ref[0, :, cols] = jnp.concatenate([ainv, jnp.zeros_like(ainv)], axis=1)

    big = pl.BlockSpec((1, CHUNK, HD), lambda i, d: (d, i, 0))
    return pl.pallas_call(
        body, name=name, grid=(nch, 2),
        in_specs=[pl.BlockSpec((CHUNK, 3 * HD), lambda i, d: (i, 0)), pl.BlockSpec((CHUNK, LANE), lambda i, d: (i, abcol)),
                  pl.BlockSpec((2, LANE), lambda i, d: (0, 0))],
        out_specs=[big, big, big, big, big, pl.BlockSpec((1, 1, C_HEADS, HEAD), lambda i, d: (d, i, 0, 0)), big, big],
        out_shape=[_sds((2, dm.T, HD), F32), _sds((2, dm.T, HD), BF16), _sds((2, dm.T, HD), BF16),
                   _sds((2, dm.T, HD), BF16), _sds((2, dm.T, HD), BF16), _sds((2, nch, C_HEADS, HEAD), F32),
                   _sds((2, dm.T, HD), F32), _sds((2, dm.T, HD), F32)],
        scratch_shapes=[pltpu.VMEM((LANE, CHUNK), F32)], compiler_params=_cp(),
    )(QKVg, P1, par)


def gdn_chunk_pre_bwd(QKVg, P1, par, U, WF, AI, dU, dW, dKT, dQD, dQK, dGT, dm, name):
    nch = dm.T // CHUNK
    HD = C_HEADS * HEAD
    abcol = (4 * HD) // LANE

    def body(x_ref, ab_ref, par_ref, u_ref, wf_ref, ai_ref, du_ref, dw_ref, dkt_ref, dqd_ref, dqk_ref, dgt_ref,
             dx_ref, dab_ref, dpar_ref, gct_s):
        i, d = pl.program_id(0), pl.program_id(1)
        rev = d == 1
        ab, par = ab_ref[...], par_ref[...]
        gb, gate_vjp = jax.vjp(_gate_fn, ab, par)
        gcl = _cum_lanes(gb)
        gct_s[...] = gcl.T
        tot = jnp.sum(gb, axis=0, keepdims=True)
        lane = lax.broadcasted_iota(jnp.int32, (CHUNK, LANE), 1)
        dgcl = jnp.zeros((CHUNK, LANE), F32)
        dgb = jnp.zeros((CHUNK, LANE), F32)
        first = d == 0
        hs = range(C_HEADS)
        hcols = [pl.ds(h * HEAD, HEAD) for h in hs]
        outs = _chunk_bwd(
            [x_ref[:, c] for c in hcols],
            [x_ref[:, pl.ds((C_HEADS + h) * HEAD, HEAD)] for h in hs],
            [x_ref[:, pl.ds((2 * C_HEADS + h) * HEAD, HEAD)] for h in hs],
            [_col(gcl, d * 16 + h) for h in hs], [gct_s[pl.ds(d * 16 + h, 1), :] for h in hs],
            [_col(tot, d * 16 + h) for h in hs], [_col(gb, d * 16 + 8 + h) for h in hs], rev,
            [ai_ref[0, :, pl.ds(h * HEAD, CHUNK)] for h in hs],
            [jnp.concatenate([u_ref[0, :, c], wf_ref[0, :, c]], axis=1) for c in hcols],
            [du_ref[0, :, c] for c in hcols], [dw_ref[0, :, c] for c in hcols], [dkt_ref[0, :, c] for c in hcols],
            [dqd_ref[0, :, c] for c in hcols], [dqk_ref[0, :, pl.ds(h * HEAD, CHUNK)] for h in hs],
            [dgt_ref[0, 0, pl.ds(h, 1), pl.ds(0, 1)] for h in hs])
        for h, (dq, dk, dv, dgc, dtotal, dbeta) in enumerate(outs):
            idx = d * 16 + h
            dx_ref[0, :, hcols[h]] = dq
            dx_ref[0, :, pl.ds((C_HEADS + h) * HEAD, HEAD)] = dk
            dx_ref[0, :, pl.ds((2 * C_HEADS + h) * HEAD, HEAD)] = dv
            dgcl = dgcl + jnp.where(lane == idx, dgc, 0.0)
            dgb = dgb + jnp.where(lane == idx + 8, dbeta, 0.0) + jnp.where(lane == idx, dtotal, 0.0)
        dab, dpar = gate_vjp(dgb + _cum_lanes(dgcl, transpose=True))
        dab_ref[0] = dab
        _acc(dpar_ref, (slice(None), slice(None)), dpar, (i == 0) & first)

    big = pl.BlockSpec((1, CHUNK, HD), lambda i, d: (d, i, 0))
    return pl.pallas_call(
        body, name=name, grid=(nch, 2),
        in_specs=[pl.BlockSpec((CHUNK, 3 * HD), lambda i, d: (i, 0)), pl.BlockSpec((CHUNK, LANE), lambda i, d: (i, abcol)),
                  pl.BlockSpec((2, LANE), lambda i, d: (0, 0)), big, big, big, big, big, big, big, big,
                  pl.BlockSpec((1, 1, C_HEADS, HEAD), lambda i, d: (d, i, 0, 0))],
        out_specs=[pl.BlockSpec((1, CHUNK, 3 * HD), lambda i, d: (d, i, 0)), pl.BlockSpec((1, CHUNK, LANE), lambda i, d: (d, i, 0)),
                   pl.BlockSpec((2, LANE), lambda i, d: (0, 0))],
        out_shape=[_sds((2, dm.T, 3 * HD), F32), _sds((2, dm.T, LANE), F32), _sds((2, LANE), F32)],
        scratch_shapes=[pltpu.VMEM((LANE, CHUNK), F32)], compiler_params=_cp(),
    )(QKVg, P1, par, U, WF, AI, dU, dW, dKT, dQD, dQK, dGT)


def _scan_chunk(b, d, c, dm):
    nh, nx = dm.M // CHUNK, dm.N // CHUNK
    in_h = c < nh
    pos_h = jnp.where(d == 0, c, nh - 1 - c)
    pos_x = jnp.where(d == 0, c - nh, nx - 1 - (c - nh))
    return jnp.where(in_h, dm.Tx // CHUNK + b * nh + pos_h, b * nx + pos_x)


def gdn_scan_fwd(U, W, KT, QD, QK, GT, dm, name):
    nch = dm.T // CHUNK
    HD = C_HEADS * HEAD
    nsc = (dm.M + dm.N) // CHUNK

    def body(u_ref, w_ref, kt_ref, qd_ref, qk_ref, gt_ref, o_ref, ss_ref, s_s):
        @pl.when(pl.program_id(2) == 0)
        def _():
            s_s[...] = jnp.zeros_like(s_s)

        hs = range(C_HEADS)
        blk = [pl.ds(h * HEAD, HEAD) for h in hs]
        ss = [s_s[b, :] for b in blk]
        for b, s in zip(blk, ss):
            ss_ref[0, 0, b, :] = s
        sbs = [s.astype(BF16) for s in ss]
        ws = [jnp.dot(w_ref[0, :, b], sb, preferred_element_type=F32) for b, sb in zip(blk, sbs)]
        os1 = [jnp.dot(qd_ref[0, :, b], sb, preferred_element_type=F32) for b, sb in zip(blk, sbs)]
        vnbs = [(u_ref[0, :, b] - wv).astype(BF16) for b, wv in zip(blk, ws)]
        os2 = [jnp.dot(qk_ref[0, :, pl.ds(h * HEAD, CHUNK)], vnbs[h], preferred_element_type=F32) for h in hs]
        upd = [_dot_tn(kt_ref[0, :, b], vnb) for b, vnb in zip(blk, vnbs)]
        for h in hs:
            o_ref[0, :, blk[h]] = os1[h] + os2[h]
            s_s[blk[h], :] = ss[h] * gt_ref[0, 0, pl.ds(h, 1), :] + upd[h]

    big = pl.BlockSpec((1, CHUNK, HD), lambda b, d, c: (d, _scan_chunk(b, d, c, dm), 0))
    return pl.pallas_call(
        body, name=name, grid=(dm.Bl, 2, nsc),
        in_specs=[big, big, big, big, big,
                  pl.BlockSpec((1, 1, C_HEADS, HEAD), lambda b, d, c: (d, _scan_chunk(b, d, c, dm), 0, 0))],
        out_specs=[big, pl.BlockSpec((1, 1, HD, HEAD), lambda b, d, c: (d, _scan_chunk(b, d, c, dm), 0, 0))],
        out_shape=[_sds((2, dm.T, HD), F32), _sds((2, nch, HD, HEAD), F32)],
        scratch_shapes=[pltpu.VMEM((HD, HEAD), F32)], compiler_params=_cp(),
    )(U, W, KT, QD, QK, GT)


def gdn_scan_bwd(dO, SS, U, W, KT, QD, QK, GT, dm, name):
    nch = dm.T // CHUNK
    HD = C_HEADS * HEAD
    nsc = (dm.M + dm.N) // CHUNK

    def body(do_ref, ss_ref, u_ref, w_ref, kt_ref, qd_ref, qk_ref, gt_ref,
             du_ref, dw_ref, dkt_ref, dqd_ref, dqk_ref, dgt_ref, ds_s):
        @pl.when(pl.program_id(2) == 0)
        def _():
            ds_s[...] = jnp.zeros_like(ds_s)

        hs = range(C_HEADS)
        blk = [pl.ds(h * HEAD, HEAD) for h in hs]
        ss = [ss_ref[0, 0, b, :] for b in blk]
        sbs = [s.astype(BF16) for s in ss]
        dobs = [do_ref[:, b].astype(BF16) for b in blk]
        dsns = [ds_s[b, :] for b in blk]
        dsnbs = [t.astype(BF16) for t in dsns]
        wss = [jnp.dot(w_ref[0, :, b], sb, preferred_element_type=F32) for b, sb in zip(blk, sbs)]
        dqds = [_dot_nt(dob, sb) for dob, sb in zip(dobs, sbs)]
        dv1 = [_dot_tn(qk_ref[0, :, pl.ds(h * HEAD, CHUNK)], dobs[h]) for h in hs]
        dv2 = [jnp.dot(kt_ref[0, :, b], t, preferred_element_type=F32) for b, t in zip(blk, dsnbs)]
        ds1 = [_dot_tn(qd_ref[0, :, b], dob) for b, dob in zip(blk, dobs)]
        vnbs = [(u_ref[0, :, b] - wv).astype(BF16) for b, wv in zip(blk, wss)]
        dvns = [a + b for a, b in zip(dv1, dv2)]
        dvnbs = [t.astype(BF16) for t in dvns]
        dqks = [_dot_nt(dob, vnb) for dob, vnb in zip(dobs, vnbs)]
        dkts = [_dot_nt(vnb, t) for vnb, t in zip(vnbs, dsnbs)]
        dws = [_dot_nt(t, sb) for t, sb in zip(dvnbs, sbs)]
        ds2 = [_dot_tn(w_ref[0, :, b], t) for b, t in zip(blk, dvnbs)]
        for h in hs:
            b = blk[h]
            dqd_ref[0, :, b] = dqds[h]
            dqk_ref[0, :, b] = jnp.concatenate([dqks[h], jnp.zeros_like(dqks[h])], axis=1)
            dkt_ref[0, :, b] = dkts[h]
            du_ref[0, :, b] = dvns[h]
            dw_ref[0, :, b] = -dws[h]
            dgt_ref[0, 0, pl.ds(h, 1), :] = jnp.broadcast_to(jnp.sum(dsns[h] * ss[h], keepdims=True), (1, HEAD))
            ds_s[b, :] = dsns[h] * gt_ref[0, 0, pl.ds(h, 1), :] + ds1[h] - ds2[h]

    def mem(b, d, c):
        return _scan_chunk(b, d, nsc - 1 - c, dm)

    big = pl.BlockSpec((1, CHUNK, HD), lambda b, d, c: (d, mem(b, d, c), 0))
    gts = pl.BlockSpec((1, 1, C_HEADS, HEAD), lambda b, d, c: (d, mem(b, d, c), 0, 0))
    return pl.pallas_call(
        body, name=name, grid=(dm.Bl, 2, nsc),
        in_specs=[pl.BlockSpec((CHUNK, HD), lambda b, d, c: (mem(b, d, c), 0)),
                  pl.BlockSpec((1, 1, HD, HEAD), lambda b, d, c: (d, mem(b, d, c), 0, 0)), big, big, big, big, big, gts],
        out_specs=[big, big, big, big, big, gts],
        out_shape=[_sds((2, dm.T, HD), F32)] * 5 + [_sds((2, nch, C_HEADS, HEAD), F32)],
        scratch_shapes=[pltpu.VMEM((HD, HEAD), F32)], compiler_params=_cp(),
    )(dO, SS, U, W, KT, QD, QK, GT)


def _finish_fn(o, z, gain):
    y = o * lax.rsqrt(jnp.mean(o * o, axis=-1, keepdims=True) + EPS) * gain
    return y * _silu(z)


def gdn_finish(O, P1, og, dm, name):
    TM = dm.TM
    HD = C_HEADS * HEAD
    zc = (3 * HD) // HEAD

    def body(o0_ref, o1_ref, z_ref, g_ref, y_ref):
        y_ref[...] = _finish_fn(o0_ref[0] + o1_ref[0], z_ref[...], g_ref[...]).astype(BF16)

    return pl.pallas_call(
        body, name=name, grid=(dm.ntx, C_HEADS),
        in_specs=[pl.BlockSpec((1, TM, HEAD), lambda i, j: (0, i, j)), pl.BlockSpec((1, TM, HEAD), lambda i, j: (1, i, j)),
                  pl.BlockSpec((TM, HEAD), lambda i, j: (i, zc + j)), pl.BlockSpec((1, HEAD), lambda i, j: (0, 0))],
        out_specs=pl.BlockSpec((TM, HEAD), lambda i, j: (i, j)),
        out_shape=_sds((dm.Tx, HD), BF16), compiler_params=_cp(),
    )(O, O, P1, og)


def gdn_finish_bwd(O, P1, og, dY, dm, name):
    TM = dm.TM
    HD = C_HEADS * HEAD
    zc = (3 * HD) // HEAD

    def body(o0_ref, o1_ref, z_ref, g_ref, dy_ref, do_ref, dz_ref, dg_ref):
        i, j = pl.program_id(0), pl.program_id(1)
        _, vjp = jax.vjp(_finish_fn, o0_ref[0] + o1_ref[0], z_ref[...], g_ref[...])
        do, dz, dg = vjp(dy_ref[...])
        do_ref[...] = do
        dz_ref[...] = dz
        _acc(dg_ref, (slice(None), slice(None)), dg, (i == 0) & (j == 0))

    blk = pl.BlockSpec((TM, HEAD), lambda i, j: (i, j))
    return pl.pallas_call(
        body, name=name, grid=(dm.ntx, C_HEADS),
        in_specs=[pl.BlockSpec((1, TM, HEAD), lambda i, j: (0, i, j)), pl.BlockSpec((1, TM, HEAD), lambda i, j: (1, i, j)),
                  pl.BlockSpec((TM, HEAD), lambda i, j: (i, zc + j)), pl.BlockSpec((1, HEAD), lambda i, j: (0, 0)), blk],
        out_specs=[blk, blk, pl.BlockSpec((1, HEAD), lambda i, j: (0, 0))],
        out_shape=[_sds((dm.Tx, HD), F32), _sds((dm.Tx, HD), F32), _sds((1, HEAD), F32)],
        compiler_params=_cp(),
    )(O, O, P1, og, dY)


def adaln_fwd(c_ext, w_mod, b_loc, name):
    R, D = c_ext.shape
    nl = w_mod.shape[2]
    tn = _pick(nl, 384)

    def body(c_ref, w_ref, b_ref, o_ref):
        o_ref[0] = _dot(_silu(c_ref[...]), w_ref[0]) + b_ref[0]

    return pl.pallas_call(
        body, name=name, grid=(2, nl // tn),
        in_specs=[pl.BlockSpec((R, D), lambda l, j: (0, 0)), pl.BlockSpec((1, D, tn), lambda l, j: (l, 0, j)),
                  pl.BlockSpec((1, 1, tn), lambda l, j: (l, 0, j))],
        out_specs=pl.BlockSpec((1, R, tn), lambda l, j: (l, 0, j)),
        out_shape=_sds((2, R, nl), F32), compiler_params=_cp(),
    )(c_ext, w_mod, b_loc)


def adaln_bwd(c_ext, c_ctx, w_mod, dmx, dmh, nb, name):
    R, D = c_ext.shape
    nl = w_mod.shape[2]
    tn = _pick(nl, 384)
    nj = nl // tn

    def body(c_ref, cc_ref, w_ref, dmx_ref, dmh_ref, gw_ref, dc_ref):
        l, j = pl.program_id(0), pl.program_id(1)
        dh = dmh_ref[0, 0:1, :]
        for k in range(1, N_DEV):
            dh = dh + dmh_ref[0, k:k + 1, :]
        row = lax.broadcasted_iota(jnp.int32, (R, tn), 0)
        dmat = dmx_ref[0] + jnp.where(row == nb, dh, 0.0)
        gw_ref[0] = _dot_tn(_silu(c_ref[...]), dmat)
        part = _dot_nt(jnp.broadcast_to(dh, (8, tn)), w_ref[0])[0:1]
        _acc(dc_ref, (slice(None), slice(None)), part, (l == 0) & (j == 0))

        @pl.when((l == 1) & (j == nj - 1))
        def _():
            cc = cc_ref[...]
            sg = jax.nn.sigmoid(cc)
            dc_ref[...] = dc_ref[...] * (sg * (1.0 + cc * (1.0 - sg)))

    return pl.pallas_call(
        body, name=name, grid=(2, nj),
        in_specs=[pl.BlockSpec((R, D), lambda l, j: (0, 0)), pl.BlockSpec((1, D), lambda l, j: (0, 0)),
                  pl.BlockSpec((1, D, tn), lambda l, j: (l, 0, j)), pl.BlockSpec((1, R, tn), lambda l, j: (l, 0, j)),
                  pl.BlockSpec((1, N_DEV, tn), lambda l, j: (l, 0, j))],
        out_specs=[pl.BlockSpec((1, D, tn), lambda l, j: (l, 0, j)), pl.BlockSpec((1, D), lambda l, j: (0, 0))],
        out_shape=[_sds((2, D, nl), F32), _sds((1, D), F32)], compiler_params=_cp(),
    )(c_ext, c_ctx, w_mod, dmx, dmh)


def bmod_grad(dmx, dmh, name):
    _, R, n9 = dmx.shape

    def body(dmx_ref, dmh_ref, o_ref):
        o_ref[0] = jnp.sum(dmx_ref[0], axis=0, keepdims=True) + jnp.sum(dmh_ref[0], axis=0, keepdims=True)

    return pl.pallas_call(
        body, name=name, grid=(2,),
        in_specs=[pl.BlockSpec((1, R, n9), lambda l: (l, 0, 0)), pl.BlockSpec((1, N_DEV, n9), lambda l: (l, 0, 0))],
        out_specs=pl.BlockSpec((1, 1, n9), lambda l: (l, 0, 0)), out_shape=_sds((2, 1, n9), F32),
        compiler_params=_cp(),
    )(dmx, dmh)


def adamw(gs, w, m, v, name):
    S, R, C = gs.shape
    cap = max(8, (1 << 20) // (S * C))
    tr = R
    if R > cap:
        tr = max(t for t in range(8, cap + 1, 8) if R % t == 0)

    def body(g_ref, w_ref, m_ref, v_ref, go_ref, d_ref, mo_ref, vo_ref):
        g = g_ref[0].astype(F32)
        for k in range(1, S):
            g = g + g_ref[k].astype(F32)
        mn = ADAM_B1 * m_ref[...] + (1.0 - ADAM_B1) * g
        vn = ADAM_B2 * v_ref[...] + (1.0 - ADAM_B2) * jnp.square(g)
        m_hat = mn / (1.0 - ADAM_B1 ** ADAM_STEP)
        v_hat = vn / (1.0 - ADAM_B2 ** ADAM_STEP)
        go_ref[...] = g
        d_ref[...] = -ADAM_LR * (m_hat / (jnp.sqrt(v_hat) + ADAM_EPS) + ADAM_WD * w_ref[...])
        mo_ref[...] = mn
        vo_ref[...] = vn

    blk = pl.BlockSpec((tr, C), lambda i: (i, 0))
    return pl.pallas_call(
        body, name=name, grid=(R // tr,),
        in_specs=[pl.BlockSpec((S, tr, C), lambda i: (0, i, 0)), blk, blk, blk],
        out_specs=[blk] * 4, out_shape=[_sds((R, C), F32)] * 4, compiler_params=_cp(),
    )(gs, w, m, v)


def _gather_flat(parts, dtype, name):
    flat = jnp.concatenate([p.astype(dtype).reshape(-1) for p in parts])
    n = flat.shape[0]
    pad = (-n) % LANE
    if pad:
        flat = jnp.concatenate([flat, jnp.zeros((pad,), dtype)])
    got = all_gather([flat.reshape(-1, LANE)], name)[0].reshape(N_DEV, -1)
    out, off = [], 0
    for p in parts:
        out.append(got[:, off:off + p.size].reshape((N_DEV,) + p.shape))
        off += p.size
    return out


def _cols_full(g):
    return g.transpose(1, 0, 2).reshape(g.shape[1], -1)


def _cols_split(full):
    K = full.shape[0]
    return full.reshape(K, N_DEV, -1).transpose(1, 0, 2)


def kernel(x, c, ctx, c_ctx, w_mod, b_mod, norm_g, ffn_wg, ffn_wu, ffn_wd, ab_w_in, ab_q_norm, ab_k_norm, pool_w, pool_scale, ab_w_out, gdn_w_in, gdn_conv_w, gdn_a_log, gdn_dt_bias, gdn_o_norm, gdn_w_out, loss_target, m_c_ctx, m_w_mod, m_b_mod, m_norm_g, m_ffn_wg, m_ffn_wu, m_ffn_wd, m_ab_w_in, m_ab_q_norm, m_ab_k_norm, m_pool_w, m_pool_scale, m_ab_w_out, m_gdn_w_in, m_gdn_conv_w, m_gdn_a_log, m_gdn_dt_bias, m_gdn_o_norm, m_gdn_w_out, v_c_ctx, v_w_mod, v_b_mod, v_norm_g, v_ffn_wg, v_ffn_wu, v_ffn_wd, v_ab_w_in, v_ab_q_norm, v_ab_k_norm, v_pool_w, v_pool_scale, v_ab_w_out, v_gdn_w_in, v_gdn_conv_w, v_gdn_a_log, v_gdn_dt_bias, v_gdn_o_norm, v_gdn_w_out):
    Bl, N, D = x.shape
    M = ctx.shape[1]
    F = ffn_wd.shape[2] * N_DEV
    dm = Dims(Bl, N, M, D, F)
    TM, Tx, Th, T, G = dm.TM, dm.Tx, dm.Th, dm.T, dm.G
    HD = C_HEADS * HEAD
    me = 4 * lax.axis_index("x") + 2 * lax.axis_index("y") + lax.axis_index("c")
    nb = N_DEV * Bl
    R = -(-(nb + 1) // 8) * 8
    nl = w_mod.shape[2]
    n_gdn = gdn_w_in.shape[2] * N_DEV
    n_gdn_pad = -(-n_gdn // LANE) * LANE

    big = [w.astype(BF16) for w in (ffn_wg, ffn_wu, ffn_wd, ab_w_in, ab_w_out, gdn_w_in, gdn_w_out)]
    g_wg, g_wu, g_wd, g_abin, g_about, g_gin, g_gout = all_gather(big, "gather_weights")
    g_c, g_ng, g_cw = _gather_flat([c, norm_g, gdn_conv_w], F32, "gather_small")
    W_ABIN = _cols_full(g_abin[:, 0])
    W_ABOUT = g_about[:, 0].reshape(-1, D)
    W_GIN = jnp.pad(_cols_full(g_gin[:, 0]), ((0, 0), (0, n_gdn_pad - n_gdn)))
    W_GOUT = g_gout[:, 0].reshape(-1, D)
    gains = g_ng.transpose(1, 2, 0, 3).reshape(2, 3, 1, D)
    conv_w = g_cw[:, 0].transpose(1, 0, 2).reshape(3, -1)

    c_all = g_c.reshape(nb, D)
    c_ext = jnp.concatenate([c_all, c_ctx[None], jnp.zeros((R - nb - 1, D), F32)], 0)
    b_loc = lax.dynamic_slice_in_dim(b_mod, me * nl, nl, axis=1).reshape(2, 1, nl)
    mod_loc = adaln_fwd(c_ext, w_mod, b_loc, "adaln_fwd")
    (g_mod,) = _gather_flat([mod_loc], F32, "gather_mod")
    mod_full = g_mod.transpose(1, 2, 0, 3).reshape(2, R, 9 * D)
    MOD = []
    for l in range(2):
        mine = lax.dynamic_slice_in_dim(mod_full[l], me * Bl, Bl, axis=0)
        MOD.append(jnp.concatenate([mine, mod_full[l, nb:nb + 1]], 0).reshape(G, 9, D))

    dm5, dmr = dm.with_tile(512), dm.with_tile(1024)
    tabs = _rope_tables(dmr)
    qkg = jnp.stack([ab_q_norm, ab_k_norm])
    pw = pool_w[0].astype(BF16)
    pwT = pool_w[0].transpose(0, 2, 1).astype(BF16)
    par = jnp.stack([jnp.pad(jnp.pad(p[0], ((0, 0), (0, 8))).reshape(-1), (0, LANE - 32))
                     for p in (gdn_a_log, gdn_dt_bias)])

    X0 = jnp.concatenate([x.reshape(Tx, D), ctx.reshape(Th, D)], 0)
    nt, ntx = dm.nt, dm.ntx
    def ffn(X, l, s, s0, all_rows, tag):
        return ffn_fwd(X, MOD[l], gains[l, 2 * s], g_wg, g_wu, g_wd, l, s, s0, dm, all_rows, "ffn_fwd_" + tag)

    X1, Y1 = ffn(X0, 0, 0, 0, True, "00")
    P0, XN0 = modmm(X1, MOD[0], gains[0, 1], W_ABIN, 3, dm5, "ab_in_proj")
    QKV = qkv_prep(P0, qkg, tabs, dmr, "qkv_prep")
    ATT = jnp.concatenate([attn_fwd(QKV, dm, True, "attn_fwd_x"), attn_fwd(QKV, dm, False, "attn_fwd_h")], 0)
    POOL = jnp.concatenate([pool_fwd(P0, pw, pool_scale, dm, True, "pool_fwd_x"),
                            pool_fwd(P0, pw, pool_scale, dm, False, "pool_fwd_h")], 0)
    na = A_HEADS * HEAD
    X2, YM0 = proj_res([ATT, POOL], [W_ABOUT[:na], W_ABOUT[na:]], X1, MOD[0], dm5, dm5.nt, "ab_out_proj")
    X3, Y3 = ffn(X2, 0, 1, 6, True, "01")
    X4, Y4 = ffn(X3, 1, 0, 0, True, "10")
    P1, XN1 = modmm(X4, MOD[1], gains[1, 1], W_GIN, 3, dm5, "gdn_in_proj")
    QKVg = jnp.concatenate([gdn_prep(P1, conv_w, dm, True, "gdn_prep_x"), gdn_prep(P1, conv_w, dm, False, "gdn_prep_h")], 0)
    U, W, KT, QD, QK, GT, WF, AI = gdn_chunk_pre(QKVg, P1, par, dm, "gdn_chunk_pre")
    O, SS = gdn_scan_fwd(U, W, KT, QD, QK, GT, dm, "gdn_scan_fwd")
    FIN = gdn_finish(O, P1, gdn_o_norm, dmr, "gdn_finish")
    X5, YM1 = proj_res([FIN], [W_GOUT], X4, MOD[1], dm5, dm5.ntx, "gdn_out_proj")
    X6, Y6 = ffn(X5, 1, 1, 6, False, "11")
    lvec, dX6 = loss_head(X6, loss_target.reshape(Tx, D), dmr, "loss_head")
    loss = lax.psum(lvec[0, 0], AXES)

    zrow = lambda a: jnp.concatenate([a, jnp.zeros((G - a.shape[0],) + a.shape[1:], F32)], 0) if a.shape[0] < G else a

    def ffn_back(Xin, dXo, Y, l, s, s0, all_rows, tag):
        dXi, XNb, DOb, Hb, DGb, DUb, dmod, dgain = ffn_bwd(
            Xin, dXo, Y, MOD[l], gains[l, 2 * s], g_wg, g_wu, g_wd, l, s, s0, dm, all_rows, "ffn_bwd_" + tag)
        dwg, dwu, dwd = ffn_dw(XNb, DOb, Hb, DGb, DUb, dm, "ffn_dw_" + tag)
        return dXi, zrow(dmod), dgain, dwg, dwu, dwd

    dX5, dmod_12, dgain_12, dwg11, dwu11, dwd11 = ffn_back(X5, dX6, Y6, 1, 1, 6, False, "11")
    DY1, dFIN, dgate_1 = proj_res_bwd(dX5, YM1, MOD[1], [W_GOUT], dm5, dm5.ntx, "gdn_out_proj_bwd")
    d_gout = atb(FIN, DY1, Tx, dm, "gdn_dwout")
    dOsum, dZ, d_onorm = gdn_finish_bwd(O, P1, gdn_o_norm, dFIN, dmr, "gdn_finish_bwd")
    dO_all = jnp.concatenate([dOsum, jnp.zeros((Th, HD), F32)], 0)
    dU, dW, dKT, dQD, dQK, dGT = gdn_scan_bwd(dO_all, SS, U, W, KT, QD, QK, GT, dm, "gdn_scan_bwd")
    dQKVg, dAB, dPAR = gdn_chunk_pre_bwd(QKVg, P1, par, U, WF, AI, dU, dW, dKT, dQD, dQK, dGT, dm, "gdn_chunk_pre_bwd")
    dPx, dcw_x = gdn_prep_bwd(P1, dQKVg, conv_w, dm, True, "gdn_prep_bwd_x")
    dPh, dcw_h = gdn_prep_bwd(P1, dQKVg, conv_w, dm, False, "gdn_prep_bwd_h")
    d_conv = dcw_x + dcw_h
    dP1 = jnp.concatenate([jnp.concatenate([dPx, dPh], 0), jnp.concatenate([dZ, jnp.zeros((Th, HD), F32)], 0),
                           dAB[0] + dAB[1]], axis=1).astype(BF16)
    d_gin = atb(XN1, dP1, T, dm, "gdn_dwin")[:, :n_gdn]
    dX5_full = jnp.concatenate([dX5, jnp.zeros((Th, D), F32)], 0)
    dX4, dmod_11, dgain_11 = mixin_bwd(dP1, W_GIN, X4, dX5_full, MOD[1], gains[1, 1], 3, dm, "gdn_in_proj_bwd")
    dX3, dmod_10, dgain_10, dwg10, dwu10, dwd10 = ffn_back(X3, dX4, Y4, 1, 0, 0, True, "10")
    dMOD1 = jnp.concatenate([dmod_10, dmod_11, zrow(dgate_1), dmod_12], 1).reshape(G, 9 * D)

    dX2, dmod_02, dgain_02, dwg01, dwu01, dwd01 = ffn_back(X2, dX3, Y3, 0, 1, 6, True, "01")
    DY0, dATT, dPOOL, dgate_0 = proj_res_bwd(dX2, YM0, MOD[0], [W_ABOUT[:na], W_ABOUT[na:]], dm5, dm5.nt, "ab_out_proj_bwd")
    d_about = jnp.concatenate([atb(ATT, DY0, T, dm, "ab_dwout_a"), atb(POOL, DY0, T, dm, "ab_dwout_p")], 0)
    dUx, dpw_x, dps_x = pool_bwd(P0, dPOOL[:Tx], pw, pwT, pool_scale, dm, True, "pool_bwd_x")
    dUh, dpw_h, dps_h = pool_bwd(P0, dPOOL[Tx:], pw, pwT, pool_scale, dm, False, "pool_bwd_h")
    dQh, dKh0, dVh0 = attn_bwd(QKV, dATT[Tx:], dm, False, None, "attn_bwd_h")
    dQx, dKh, dVh, dKx, dVx = attn_bwd(QKV, dATT[:Tx], dm, True, (dKh0, dVh0), "attn_bwd_x")
    dQKV = jnp.concatenate([jnp.concatenate([dQx, dQh], 0), jnp.concatenate([dKx, dKh], 0), jnp.concatenate([dVx, dVh], 0)], 1)
    dPqkv, d_qkg = qkv_prep_bwd(P0, dQKV, qkg, tabs, dmr, "qkv_prep_bwd")
    dP0 = jnp.concatenate([dPqkv, jnp.concatenate([dUx, dUh], 0)], 1).astype(BF16)
    d_abin = atb(XN0, dP0, T, dm, "ab_dwin")
    dX1, dmod_01, dgain_01 = mixin_bwd(dP0, W_ABIN, X1, dX2, MOD[0], gains[0, 1], 3, dm5, "ab_in_proj_bwd")
    dX0, dmod_00, dgain_00, dwg00, dwu00, dwd00 = ffn_back(X0, dX1, Y1, 0, 0, 0, True, "00")
    dMOD0 = jnp.concatenate([dmod_00, dmod_01, dgate_0, dmod_02], 1).reshape(G, 9 * D)
    grad_x = dX0[:Tx].reshape(Bl, N, D)

    d_ng = jnp.concatenate([dgain_00, dgain_01, dgain_02, dgain_10, dgain_11, dgain_12], 0)
    nf = ffn_wg.shape[3]
    parts = [jnp.concatenate([dwg00, dwg01, dwg10, dwg11], 1), jnp.concatenate([dwu00, dwu01, dwu10, dwu11], 1),
             jnp.concatenate([dwd00, dwd01, dwd10, dwd11], 1),
             _cols_split(d_abin), d_about.reshape(N_DEV, -1, D), _cols_split(d_gin), d_gout.reshape(N_DEV, -1, D),
             _cols_split(d_conv), _cols_split(d_ng)]
    gs_wg, gs_wu, gs_wd, gs_abin, gs_about, gs_gin, gs_gout, gs_conv, gs_ng = scatter_blocks(parts, "scatter_grads")

    d_alog = dPAR[0, :32].reshape(2, 16)[:, :8].reshape(1, 16)
    d_dtb = dPAR[1, :32].reshape(2, 16)[:, :8].reshape(1, 16)
    small = [d_qkg[0], d_qkg[1], (dpw_x + dpw_h).reshape(-1, HEAD), dps_x + dps_h, d_alog, d_dtb, d_onorm,
             jnp.stack([dMOD0, dMOD1])]
    gs_qn, gs_kn, gs_pw, gs_ps, gs_alog, gs_dtb, gs_on, g_dm = _gather_flat(small, F32, "gather_small_grads")
    dmx = g_dm[:, :, :Bl].transpose(1, 0, 2, 3).reshape(2, nb, 9 * D)
    dmx = jnp.concatenate([dmx, jnp.zeros((2, R - nb, 9 * D), F32)], 1)
    dmh = g_dm[:, :, Bl].transpose(1, 0, 2)
    cols_of_me = lambda a: lax.dynamic_slice_in_dim(a, me * nl, nl, axis=2)
    d_wmod, dcc = adaln_bwd(c_ext, c_ctx[None], w_mod, cols_of_me(dmx), cols_of_me(dmh), nb, "adaln_bwd")
    d_bmod = bmod_grad(dmx, dmh, "bmod_grad")
    (gs_cc,) = _gather_flat([dcc], F32, "gather_cctx_grad")

    def upd(gs, w, m, v, shape2, name):
        outs = adamw(gs.reshape((gs.shape[0],) + shape2), w.reshape(shape2), m.reshape(shape2), v.reshape(shape2), "adamw_" + name)
        return [o.reshape(w.shape) for o in outs]

    res = [
        upd(gs_cc, c_ctx, m_c_ctx, v_c_ctx, (1, D), "c_ctx"),
        upd(d_wmod[None], w_mod, m_w_mod, v_w_mod, (2 * D, nl), "w_mod"),
        upd(d_bmod[None], b_mod, m_b_mod, v_b_mod, (2, 9 * D), "b_mod"),
        upd(gs_ng, norm_g, m_norm_g, v_norm_g, (6, D // N_DEV), "norm_g"),
        upd(gs_wg, ffn_wg, m_ffn_wg, v_ffn_wg, (4 * D, nf), "ffn_wg"),
        upd(gs_wu, ffn_wu, m_ffn_wu, v_ffn_wu, (4 * D, nf), "ffn_wu"),
        upd(gs_wd, ffn_wd, m_ffn_wd, v_ffn_wd, (4 * nf, D), "ffn_wd"),
        upd(gs_abin, ab_w_in, m_ab_w_in, v_ab_w_in, (D, ab_w_in.shape[2]), "ab_w_in"),
        upd(gs_qn, ab_q_norm, m_ab_q_norm, v_ab_q_norm, (1, HEAD), "ab_q_norm"),
        upd(gs_kn, ab_k_norm, m_ab_k_norm, v_ab_k_norm, (1, HEAD), "ab_k_norm"),
        upd(gs_pw, pool_w, m_pool_w, v_pool_w, (len(POOL_WINDOWS) * HEAD, HEAD), "pool_w"),
        upd(gs_ps, pool_scale, m_pool_scale, v_pool_scale, (1, len(POOL_WINDOWS) * HEAD), "pool_scale"),
        upd(gs_about, ab_w_out, m_ab_w_out, v_ab_w_out, (ab_w_out.shape[1], D), "ab_w_out"),
        upd(gs_gin, gdn_w_in, m_gdn_w_in, v_gdn_w_in, (D, gdn_w_in.shape[2]), "gdn_w_in"),
        upd(gs_conv, gdn_conv_w, m_gdn_conv_w, v_gdn_conv_w, (3, gdn_conv_w.shape[2]), "gdn_conv_w"),
        upd(gs_alog, gdn_a_log, m_gdn_a_log, v_gdn_a_log, (1, 16), "gdn_a_log"),
        upd(gs_dtb, gdn_dt_bias, m_gdn_dt_bias, v_gdn_dt_bias, (1, 16), "gdn_dt_bias"),
        upd(gs_on, gdn_o_norm, m_gdn_o_norm, v_gdn_o_norm, (1, HEAD), "gdn_o_norm"),
        upd(gs_gout, gdn_w_out, m_gdn_w_out, v_gdn_w_out, (gdn_w_out.shape[1], D), "gdn_w_out"),
    ]
    return (loss, grad_x, *[r[0] for r in res], *[r[1] for r in res], *[r[2] for r in res], *[r[3] for r in res])
```

```python
import functools
import math
from typing import NamedTuple

import jax
import jax.numpy as jnp
from jax import lax
from jax.experimental import pallas as pl
from jax.experimental.pallas import tpu as pltpu

F32, BF16 = jnp.float32, jnp.bfloat16
EPS = 1e-6
HEAD = 128
CHUNK = 64
GRID_W = 64
ROPE_THETA = 10000.0
POOL_WINDOWS = (2, 4, 8, 16)
A_HEADS, A_KV = 4, 2
C_HEADS = 8
N_DEV = 8
AXES = ("x", "y", "c")
ADAM_LR, ADAM_B1, ADAM_B2, ADAM_EPS, ADAM_WD, ADAM_STEP = 0.001, 0.9, 0.999, 1e-08, 0.01, 10
LANE = 128
VMEM_LIMIT = 56 * 1024 * 1024
HI = lax.Precision.HIGHEST
NEG = -1e30


def _cp():
    return pltpu.CompilerParams(vmem_limit_bytes=VMEM_LIMIT)


def _sds(shape, dtype):
    return jax.ShapeDtypeStruct(tuple(shape), dtype)


def _dot(a, b):
    return jnp.dot(a.astype(BF16), b.astype(BF16), preferred_element_type=F32)


def _dot_nt(a, b):
    return lax.dot_general(a.astype(BF16), b.astype(BF16), (((1,), (1,)), ((), ())), preferred_element_type=F32)


def _dot_tn(a, b):
    return lax.dot_general(a.astype(BF16), b.astype(BF16), (((0,), (0,)), ((), ())), preferred_element_type=F32)


def _dot3(a, b, dims):
    ah, bh = a.astype(BF16), b.astype(BF16)
    al, bl = (a - ah.astype(F32)).astype(BF16), (b - bh.astype(F32)).astype(BF16)
    f = lambda x, y: lax.dot_general(x, y, (dims, ((), ())), preferred_element_type=F32)
    return f(ah, bh) + (f(ah, bl) + f(al, bh))


def _hdot(a, b):
    return _dot3(a, b, ((1,), (0,)))


def _hdot_nt(a, b):
    return _dot3(a, b, ((1,), (1,)))


def _hdot_tn(a, b):
    return _dot3(a, b, ((0,), (0,)))


def _pick(n, cap):
    if n <= cap:
        return n
    best = None
    for t in range(LANE, cap + 1, LANE):
        if n % t == 0:
            best = t
    assert best is not None, (n, cap)
    return best


class Dims(NamedTuple):
    Bl: int
    N: int
    M: int
    D: int
    F: int
    tm: int = 0

    @property
    def TM(self):
        return self.tm if self.tm else min(256, self.M)

    def with_tile(self, cap):
        return self._replace(tm=max(t for t in (1024, 512, 256, 128) if t <= cap and self.N % t == 0 and self.Th % t == 0))

    @property
    def Tx(self):
        return self.Bl * self.N

    @property
    def Th(self):
        return self.Bl * self.M

    @property
    def T(self):
        return self.Tx + self.Th

    @property
    def ntx(self):
        return self.Tx // self.TM

    @property
    def nt(self):
        return self.T // self.TM

    @property
    def tps(self):
        return self.N // self.TM

    @property
    def G(self):
        return self.Bl + 1


def _grp(i, dm, tm=None):
    tm = dm.TM if tm is None else tm
    return jnp.where(i < dm.Tx // tm, i // (dm.N // tm), dm.Bl)


def _first_of_group(i, dm, tm=None):
    tm = dm.TM if tm is None else tm
    return jnp.where(i < dm.Tx // tm, i % (dm.N // tm) == 0, i == dm.Tx // tm)


def _contraction_tile(rows):
    return max(t for t in (1024, 512, 256, 128) if rows % t == 0)


def _ffn_tile(dm):
    return max(t for t in (512, 256, 128) if dm.N % t == 0 and dm.Th % t == 0)


def _acc(ref, idx, val, first):
    @pl.when(first)
    def _():
        ref[idx] = val

    @pl.when(jnp.logical_not(first))
    def _():
        ref[idx] += val


def _modulate(x, gain, shift, scale):
    y = x * lax.rsqrt(jnp.mean(x * x, axis=-1, keepdims=True) + EPS)
    return (y * gain) * (1.0 + scale) + shift


def _silu(x):
    return x * jax.nn.sigmoid(x)


@functools.partial(jax.custom_vjp, nondiff_argnums=(1,))
def _shift_rows(a, k):
    n = a.shape[0]
    if k == 0:
        return a
    r = lax.broadcasted_iota(jnp.int32, a.shape, 0)
    rolled = pltpu.roll(a, (-k) % n, 0)
    ok = (r + k >= 0) & (r + k < n)
    return jnp.where(ok, rolled, 0.0)


def _shift_rows_fwd(a, k):
    return _shift_rows(a, k), None


def _shift_rows_bwd(k, _, d):
    return (_shift_rows(d, -k),)


_shift_rows.defvjp(_shift_rows_fwd, _shift_rows_bwd)


@functools.partial(jax.custom_vjp, nondiff_argnums=(1,))
def _roll_lanes(a, s):
    return pltpu.roll(a, s % LANE, 1)


def _roll_lanes_fwd(a, s):
    return _roll_lanes(a, s), None


def _roll_lanes_bwd(s, _, d):
    return (_roll_lanes(d, -s),)


_roll_lanes.defvjp(_roll_lanes_fwd, _roll_lanes_bwd)


def _rope(t, cs, sneg, spos):
    return t * cs + _roll_lanes(t, 96) * sneg + _roll_lanes(t, 32) * spos


def _rope_tables(dm):
    rows = dm.N // GRID_W
    row = jnp.repeat(jnp.arange(rows), GRID_W).astype(F32)
    col = jnp.tile(jnp.arange(GRID_W), rows).astype(F32)
    half = HEAD // 2
    inv_freq = jnp.power(ROPE_THETA, -jnp.arange(0, half, 2, dtype=F32) / half)
    ar, ac = row[:, None] * inv_freq, col[:, None] * inv_freq
    cs = jnp.concatenate([jnp.cos(ar), jnp.cos(ar), jnp.cos(ac), jnp.cos(ac)], axis=1)
    z = jnp.zeros_like(ar)
    sneg = jnp.concatenate([-jnp.sin(ar), z, -jnp.sin(ac), z], axis=1)
    spos = jnp.concatenate([z, jnp.sin(ar), z, jnp.sin(ac)], axis=1)
    pad1 = jnp.ones((dm.TM, HEAD), F32)
    pad0 = jnp.zeros((dm.TM, HEAD), F32)
    return (jnp.concatenate([cs, pad1], 0), jnp.concatenate([sneg, pad0], 0), jnp.concatenate([spos, pad0], 0))


def all_gather(xs, name):
    n = len(xs)

    def body(*refs):
        x_refs, out_refs = refs[:n], refs[n:2 * n]
        send_sems, recv_sems, local_sems = refs[2 * n:]
        x, y, c = lax.axis_index("x"), lax.axis_index("y"), lax.axis_index("c")
        me, sibling = (x, y, c), (x, y, 1 - c)
        chips = [(1 - x, y), (x, 1 - y), (1 - x, 1 - y)]

        def slot(a, px, py, pc):
            return out_refs[a].at[4 * px + 2 * py + pc]

        def copy(a, k, block, to, src=None):
            return pltpu.make_async_remote_copy(
                src_ref=slot(a, *block) if src is None else src, dst_ref=slot(a, *block),
                send_sem=send_sems.at[7 * a + k], recv_sem=recv_sems.at[7 * a + k],
                device_id=to, device_id_type=pl.DeviceIdType.MESH)

        mine = [pltpu.make_async_copy(x_refs[a], slot(a, *me), local_sems.at[a]) for a in range(n)]
        for cp in mine:
            cp.start()
        first = []
        for a in range(n):
            first.append(copy(a, 0, me, sibling, src=x_refs[a]))
            first += [copy(a, 1 + j, me, (*chip, c), src=x_refs[a]) for j, chip in enumerate(chips)]
        for cp in first:
            cp.start()
        passed = []
        for j, chip in enumerate(chips):
            for a in range(n):
                copy(a, 1 + j, (*chip, c), me).wait_recv()
                cp = copy(a, 4 + j, (*chip, c), sibling)
                cp.start()
                passed.append(cp)
        for a in range(n):
            copy(a, 0, sibling, me).wait_recv()
            for j, chip in enumerate(chips):
                copy(a, 4 + j, (*chip, 1 - c), me).wait_recv()
        for cp in first + passed:
            cp.wait_send()
        for cp in mine:
            cp.wait()

    anyspec = pl.BlockSpec(memory_space=pl.ANY)
    return pl.pallas_call(
        body, name=name, out_shape=[_sds((N_DEV,) + a.shape, a.dtype) for a in xs],
        in_specs=[anyspec] * n, out_specs=[anyspec] * n,
        scratch_shapes=[pltpu.SemaphoreType.DMA((7 * n,)), pltpu.SemaphoreType.DMA((7 * n,)),
                        pltpu.SemaphoreType.DMA((n,))],
    )(*xs)


def scatter_blocks(xs, name):
    n = len(xs)
    flips = [(0, 0, 1), (0, 1, 0), (0, 1, 1), (1, 0, 0), (1, 0, 1), (1, 1, 0), (1, 1, 1)]

    def body(*refs):
        x_refs, out_refs = refs[:n], refs[n:2 * n]
        send_sems, recv_sems, local_sems = refs[2 * n:]
        x, y, c = lax.axis_index("x"), lax.axis_index("y"), lax.axis_index("c")
        me = 4 * x + 2 * y + c

        def peer(f):
            return tuple(1 - v if d else v for v, d in zip((x, y, c), f))

        def lin(p):
            return 4 * p[0] + 2 * p[1] + p[2]

        mine = [pltpu.make_async_copy(x_refs[a].at[me], out_refs[a].at[me], local_sems.at[a]) for a in range(n)]
        for cp in mine:
            cp.start()
        copies = []
        for k, f in enumerate(flips):
            p = peer(f)
            for a in range(n):
                copies.append(pltpu.make_async_remote_copy(
                    src_ref=x_refs[a].at[lin(p)], dst_ref=out_refs[a].at[me],
                    send_sem=send_sems.at[7 * a + k], recv_sem=recv_sems.at[7 * a + k],
                    device_id=p, device_id_type=pl.DeviceIdType.MESH))
        for cp in copies:
            cp.start()
        for cp in copies:
            cp.wait_send()
            cp.wait_recv()
        for cp in mine:
            cp.wait()

    anyspec = pl.BlockSpec(memory_space=pl.ANY)
    return pl.pallas_call(
        body, name=name, out_shape=[_sds(a.shape, a.dtype) for a in xs],
        in_specs=[anyspec] * n, out_specs=[anyspec] * n,
        scratch_shapes=[pltpu.SemaphoreType.DMA((7 * n,)), pltpu.SemaphoreType.DMA((7 * n,)),
                        pltpu.SemaphoreType.DMA((n,))],
    )(*xs)


def _mod_spec(dm, nidx, tm=None):
    if nidx == 1:
        return pl.BlockSpec((1, 9, dm.D), lambda i: (_grp(i, dm, tm), 0, 0))
    return pl.BlockSpec((1, 9, dm.D), lambda i, k: (_grp(i, dm, tm), 0, 0))


def _wspec(shape5, l, s, ks):
    return pl.BlockSpec((ks, 1, 1) + tuple(shape5[3:]), lambda i, k: (k, l, s, 0, 0))


FFN_FWD_SHARDS = 4
FFN_BWD_SHARDS = 2


def ffn_fwd(X, MOD, gain, gwg, gwu, gwd, l, s, s0, dm, all_rows, name):
    D = dm.D
    tm = _ffn_tile(dm)
    rows = dm.T if all_rows else dm.Tx
    ks = FFN_FWD_SHARDS
    nk = N_DEV // ks

    def body(x_ref, m_ref, g_ref, wg_ref, wu_ref, wd_ref, xo_ref, y_ref, xn_s, acc_s):
        k = pl.program_id(1)

        @pl.when(k == 0)
        def _():
            m = m_ref[0]
            xn = _modulate(x_ref[...], g_ref[...], m[s0:s0 + 1], m[s0 + 1:s0 + 2])
            xn_s[...] = xn.astype(BF16)
            acc_s[...] = jnp.zeros_like(acc_s)

        xn = xn_s[...]
        y = None
        for j in range(0, ks, 2):
            wg2 = jnp.concatenate([wg_ref[j, 0, 0], wg_ref[j + 1, 0, 0]], axis=1)
            wu2 = jnp.concatenate([wu_ref[j, 0, 0], wu_ref[j + 1, 0, 0]], axis=1)
            wd2 = jnp.concatenate([wd_ref[j, 0, 0], wd_ref[j + 1, 0, 0]], axis=0)
            g = jnp.dot(xn, wg2, preferred_element_type=F32)
            u = jnp.dot(xn, wu2, preferred_element_type=F32)
            yj = jnp.dot((_silu(g) * u).astype(BF16), wd2, preferred_element_type=F32)
            y = yj if y is None else y + yj
        acc_s[...] += y

        @pl.when(k == nk - 1)
        def _():
            m = m_ref[0]
            y = acc_s[...]
            y_ref[...] = y
            xo_ref[...] = x_ref[...] + (0.5 * m[s0 + 2:s0 + 3]) * y

    row = pl.BlockSpec((tm, D), lambda i, k: (i, 0))
    return pl.pallas_call(
        body, name=name, grid=(rows // tm, nk),
        in_specs=[row, _mod_spec(dm, 2, tm), pl.BlockSpec((1, D), lambda i, k: (0, 0)),
                  _wspec(gwg.shape, l, s, ks), _wspec(gwu.shape, l, s, ks), _wspec(gwd.shape, l, s, ks)],
        out_specs=[row, row],
        out_shape=[_sds((rows, D), F32), _sds((rows, D), F32)],
        scratch_shapes=[pltpu.VMEM((tm, D), BF16), pltpu.VMEM((tm, D), F32)],
        compiler_params=_cp(),
    )(X, MOD, gain, gwg, gwu, gwd)


def ffn_bwd(X, dXo, Y, MOD, gain, gwg, gwu, gwd, l, s, s0, dm, all_rows, name):
    D = dm.D
    tm = _ffn_tile(dm)
    rows = dm.T if all_rows else dm.Tx
    ks = FFN_BWD_SHARDS
    nk = N_DEV // ks
    nf = gwg.shape[4]
    ngr = dm.G if all_rows else dm.Bl

    def body(x_ref, dxo_ref, y_ref, m_ref, g_ref, wg_ref, wu_ref, wd_ref,
             dxi_ref, xn_ref, do_ref, h_ref, dg_ref, du_ref, dm_ref, dgain_ref, xn_s, do_s, dxn_s):
        i, k = pl.program_id(0), pl.program_id(1)

        @pl.when(k == 0)
        def _():
            m = m_ref[0]
            xn = _modulate(x_ref[...], g_ref[...], m[s0:s0 + 1], m[s0 + 1:s0 + 2])
            xn_s[...] = xn.astype(BF16)
            do_s[...] = ((0.5 * m[s0 + 2:s0 + 3]) * dxo_ref[...]).astype(BF16)
            dxn_s[...] = jnp.zeros_like(dxn_s)

        xn, do = xn_s[...], do_s[...]
        dxn = None
        for p in range(ks // 2):
            j = 2 * p
            wg = jnp.concatenate([wg_ref[j, 0, 0], wg_ref[j + 1, 0, 0]], axis=1)
            wu = jnp.concatenate([wu_ref[j, 0, 0], wu_ref[j + 1, 0, 0]], axis=1)
            wd = jnp.concatenate([wd_ref[j, 0, 0], wd_ref[j + 1, 0, 0]], axis=0)
            g = jnp.dot(xn, wg, preferred_element_type=F32)
            u = jnp.dot(xn, wu, preferred_element_type=F32)
            sg = jax.nn.sigmoid(g)
            si = g * sg
            dh = _dot_nt(do, wd)
            dg = (dh * u * (sg * (1.0 + g * (1.0 - sg)))).astype(BF16)
            du = (dh * si).astype(BF16)
            dj = _dot_nt(dg, wg) + _dot_nt(du, wu)
            dxn = dj if dxn is None else dxn + dj
            h_ref[p] = (si * u).astype(BF16)
            dg_ref[p] = dg
            du_ref[p] = du
        dxn_s[...] += dxn

        @pl.when(k == nk - 1)
        def _():
            m = m_ref[0]
            _, vjp = jax.vjp(_modulate, x_ref[...], g_ref[...], m[s0:s0 + 1], m[s0 + 1:s0 + 2])
            dx, dgain, dshift, dscale = vjp(dxn_s[...])
            dxo = dxo_ref[...]
            dxi_ref[...] = dxo + dx
            xn_ref[...] = xn_s[...]
            do_ref[...] = do_s[...]
            dgate = jnp.sum(0.5 * dxo * y_ref[...], axis=0, keepdims=True)
            first = _first_of_group(i, dm, tm)
            _acc(dm_ref, (0, pl.ds(0, 1), slice(None)), dshift, first)
            _acc(dm_ref, (0, pl.ds(1, 1), slice(None)), dscale, first)
            _acc(dm_ref, (0, pl.ds(2, 1), slice(None)), dgate, first)
            _acc(dgain_ref, (slice(None), slice(None)), dgain, i == 0)

    row = pl.BlockSpec((tm, D), lambda i, k: (i, 0))
    slab = pl.BlockSpec((ks // 2, tm, 2 * nf), lambda i, k: (k, i, 0))
    pairs = _sds((N_DEV // 2, rows, 2 * nf), BF16)
    return pl.pallas_call(
        body, name=name, grid=(rows // tm, nk),
        in_specs=[row, row, row, _mod_spec(dm, 2, tm), pl.BlockSpec((1, D), lambda i, k: (0, 0)),
                  _wspec(gwg.shape, l, s, ks), _wspec(gwu.shape, l, s, ks), _wspec(gwd.shape, l, s, ks)],
        out_specs=[row, row, row, slab, slab, slab,
                   pl.BlockSpec((1, 3, D), lambda i, k: (_grp(i, dm, tm), 0, 0)),
                   pl.BlockSpec((1, D), lambda i, k: (0, 0))],
        out_shape=[_sds((rows, D), F32), _sds((rows, D), BF16), _sds((rows, D), BF16),
                   pairs, pairs, pairs,
                   _sds((ngr, 3, D), F32), _sds((1, D), F32)],
        scratch_shapes=[pltpu.VMEM((tm, D), BF16), pltpu.VMEM((tm, D), BF16), pltpu.VMEM((tm, D), F32)],
        compiler_params=_cp(),
    )(X, dXo, Y, MOD, gain, gwg, gwu, gwd)


def ffn_dw(XN, DO, H, DG, DU, dm, name):
    rows, D = XN.shape
    nf = H.shape[2] // 2
    tt = _contraction_tile(rows)
    nT = rows // tt

    def body(xn_ref, do_ref, h_ref, dg_ref, du_ref, dwg_ref, dwu_ref, dwd_ref, ag_s, au_s, ad_s):
        t = pl.program_id(1)

        @pl.when(t == 0)
        def _():
            ag_s[...] = jnp.zeros_like(ag_s)
            au_s[...] = jnp.zeros_like(au_s)
            ad_s[...] = jnp.zeros_like(ad_s)

        xn = xn_ref[...]
        ag_s[...] += _dot_tn(xn, dg_ref[0])
        au_s[...] += _dot_tn(xn, du_ref[0])
        ad_s[...] += _dot_tn(h_ref[0], do_ref[...])

        @pl.when(t == nT - 1)
        def _():
            for j in range(2):
                dwg_ref[j] = ag_s[:, pl.ds(j * nf, nf)].astype(BF16)
                dwu_ref[j] = au_s[:, pl.ds(j * nf, nf)].astype(BF16)
                dwd_ref[j] = ad_s[pl.ds(j * nf, nf), :].astype(BF16)

    row = pl.BlockSpec((tt, D), lambda k, t: (t, 0))
    slab = pl.BlockSpec((1, tt, 2 * nf), lambda k, t: (k, t, 0))
    return pl.pallas_call(
        body, name=name, grid=(N_DEV // 2, nT),
        in_specs=[row, row, slab, slab, slab],
        out_specs=[pl.BlockSpec((2, D, nf), lambda k, t: (k, 0, 0)), pl.BlockSpec((2, D, nf), lambda k, t: (k, 0, 0)),
                   pl.BlockSpec((2, nf, D), lambda k, t: (k, 0, 0))],
        out_shape=[_sds((N_DEV, D, nf), BF16), _sds((N_DEV, D, nf), BF16), _sds((N_DEV, nf, D), BF16)],
        scratch_shapes=[pltpu.VMEM((D, 2 * nf), F32), pltpu.VMEM((D, 2 * nf), F32), pltpu.VMEM((2 * nf, D), F32)],
        compiler_params=_cp(),
    )(XN, DO, H, DG, DU)


def atb(A, B, rows, dm, name):
    Ka, Nb = A.shape[1], B.shape[1]
    tk, tn = _pick(Ka, 1024), _pick(Nb, 1536)
    tt = _contraction_tile(rows)
    nT = rows // tt

    def body(a_ref, b_ref, o_ref, acc_s):
        t = pl.program_id(2)

        @pl.when(t == 0)
        def _():
            acc_s[...] = jnp.zeros_like(acc_s)

        acc_s[...] += _dot_tn(a_ref[...], b_ref[...])

        @pl.when(t == nT - 1)
        def _():
            o_ref[...] = acc_s[...].astype(BF16)

    return pl.pallas_call(
        body, name=name, grid=(Ka // tk, Nb // tn, nT),
        in_specs=[pl.BlockSpec((tt, tk), lambda i, j, t: (t, i)), pl.BlockSpec((tt, tn), lambda i, j, t: (t, j))],
        out_specs=pl.BlockSpec((tk, tn), lambda i, j, t: (i, j)),
        out_shape=_sds((Ka, Nb), BF16), scratch_shapes=[pltpu.VMEM((tk, tn), F32)], compiler_params=_cp(),
    )(A, B)


def modmm(X, MOD, gain, W, s0, dm, name):
    TM, D = dm.TM, dm.D
    Nc = W.shape[1]
    tn = _pick(Nc, 1536)
    nj = Nc // tn

    def body(x_ref, m_ref, g_ref, w_ref, p_ref, xn_ref):
        @pl.when(pl.program_id(1) == 0)
        def _():
            m = m_ref[0]
            xn_ref[...] = _modulate(x_ref[...], g_ref[...], m[s0:s0 + 1], m[s0 + 1:s0 + 2]).astype(BF16)

        p_ref[...] = jnp.dot(xn_ref[...], w_ref[...], preferred_element_type=F32)

    row = pl.BlockSpec((TM, D), lambda i, j: (i, 0))
    return pl.pallas_call(
        body, name=name, grid=(dm.nt, nj),
        in_specs=[row, _mod_spec(dm, 2), pl.BlockSpec((1, D), lambda i, j: (0, 0)),
                  pl.BlockSpec((D, tn), lambda i, j: (0, j))],
        out_specs=[pl.BlockSpec((TM, tn), lambda i, j: (i, j)), row],
        out_shape=[_sds((dm.T, Nc), F32), _sds((dm.T, D), BF16)],
        compiler_params=_cp(),
    )(X, MOD, gain, W)


def mixin_bwd(dP, W, X, dXres, MOD, gain, s0, dm, name):
    TM, D = dm.TM, dm.D
    K = dP.shape[1]

    def body(dp_ref, w_ref, x_ref, dr_ref, m_ref, g_ref, dx_ref, dm_ref, dgain_ref):
        i = pl.program_id(0)
        dxn = _dot_nt(dp_ref[...], w_ref[...])
        m = m_ref[0]
        _, vjp = jax.vjp(_modulate, x_ref[...], g_ref[...], m[s0:s0 + 1], m[s0 + 1:s0 + 2])
        dx, dgain, dshift, dscale = vjp(dxn)
        dx_ref[...] = dr_ref[...] + dx
        first = _first_of_group(i, dm)
        _acc(dm_ref, (0, pl.ds(0, 1), slice(None)), dshift, first)
        _acc(dm_ref, (0, pl.ds(1, 1), slice(None)), dscale, first)
        _acc(dgain_ref, (slice(None), slice(None)), dgain, i == 0)

    row = pl.BlockSpec((TM, D), lambda i: (i, 0))
    return pl.pallas_call(
        body, name=name, grid=(dm.nt,),
        in_specs=[pl.BlockSpec((TM, K), lambda i: (i, 0)), pl.BlockSpec((D, K), lambda i: (0, 0)), row, row,
                  _mod_spec(dm, 1), pl.BlockSpec((1, D), lambda i: (0, 0))],
        out_specs=[row, pl.BlockSpec((1, 2, D), lambda i: (_grp(i, dm), 0, 0)), pl.BlockSpec((1, D), lambda i: (0, 0))],
        out_shape=[_sds((dm.T, D), F32), _sds((dm.G, 2, D), F32), _sds((1, D), F32)],
        compiler_params=_cp(),
    )(dP, W, X, dXres, MOD, gain)


def proj_res(As, Ws, X, MOD, dm, ntiles, name):
    TM, D = dm.TM, dm.D
    n = len(As)
    rows = ntiles * TM

    def body(*refs):
        a_refs, w_refs = refs[:n], refs[n:2 * n]
        x_ref, m_ref, xo_ref, y_ref = refs[2 * n:]
        y = jnp.dot(a_refs[0][...], w_refs[0][...], preferred_element_type=F32)
        for a, w in zip(a_refs[1:], w_refs[1:]):
            y += jnp.dot(a[...], w[...], preferred_element_type=F32)
        y_ref[...] = y
        xo_ref[...] = x_ref[...] + m_ref[0][5:6] * y

    row = pl.BlockSpec((TM, D), lambda i: (i, 0))
    return pl.pallas_call(
        body, name=name, grid=(ntiles,),
        in_specs=[pl.BlockSpec((TM, a.shape[1]), lambda i: (i, 0)) for a in As]
        + [pl.BlockSpec(w.shape, lambda i: (0, 0)) for w in Ws] + [row, _mod_spec(dm, 1)],
        out_specs=[row, row], out_shape=[_sds((rows, D), F32), _sds((rows, D), F32)],
        compiler_params=_cp(),
    )(*As, *Ws, X, MOD)


def proj_res_bwd(dXo, Y, MOD, Ws, dm, ntiles, name):
    TM, D = dm.TM, dm.D
    n = len(Ws)
    rows = ntiles * TM
    ngr = dm.G if ntiles == dm.nt else dm.Bl

    def body(*refs):
        dxo_ref, y_ref, m_ref = refs[:3]
        w_refs = refs[3:3 + n]
        dy_ref = refs[3 + n]
        da_refs = refs[4 + n:4 + 2 * n]
        dgate_ref = refs[4 + 2 * n]
        i = pl.program_id(0)
        dxo = dxo_ref[...]
        dy = (m_ref[0][5:6] * dxo).astype(BF16)
        dy_ref[...] = dy
        for w, da in zip(w_refs, da_refs):
            da[...] = _dot_nt(dy, w[...])
        dgate = jnp.sum(dxo * y_ref[...], axis=0, keepdims=True)
        _acc(dgate_ref, (0, slice(None), slice(None)), dgate, _first_of_group(i, dm))

    row = pl.BlockSpec((TM, D), lambda i: (i, 0))
    return pl.pallas_call(
        body, name=name, grid=(ntiles,),
        in_specs=[row, row, _mod_spec(dm, 1)] + [pl.BlockSpec(w.shape, lambda i: (0, 0)) for w in Ws],
        out_specs=[row] + [pl.BlockSpec((TM, w.shape[0]), lambda i: (i, 0)) for w in Ws]
        + [pl.BlockSpec((1, 1, D), lambda i: (_grp(i, dm), 0, 0))],
        out_shape=[_sds((rows, D), BF16)] + [_sds((rows, w.shape[0]), F32) for w in Ws] + [_sds((ngr, 1, D), F32)],
        compiler_params=_cp(),
    )(dXo, Y, MOD, *Ws)


def loss_head(Xf, target, dm, name):
    TM, D = dm.TM, dm.D

    def body(x_ref, t_ref, l_ref, dx_ref, acc_s):
        i = pl.program_id(0)
        e = x_ref[...] - t_ref[...]
        dx_ref[...] = e * (1.0 / D)

        @pl.when(i == 0)
        def _():
            acc_s[...] = jnp.zeros_like(acc_s)

        acc_s[...] += jnp.sum(e * e, axis=0, keepdims=True)

        @pl.when(i == dm.ntx - 1)
        def _():
            tot = jnp.sum(acc_s[...], axis=1, keepdims=True) * (0.5 / D)
            l_ref[...] = jnp.broadcast_to(tot, (1, LANE))

    row = pl.BlockSpec((TM, D), lambda i: (i, 0))
    return pl.pallas_call(
        body, name=name, grid=(dm.ntx,), in_specs=[row, row],
        out_specs=[pl.BlockSpec((1, LANE), lambda i: (0, 0)), row],
        out_shape=[_sds((1, LANE), F32), _sds((dm.Tx, D), F32)],
        scratch_shapes=[pltpu.VMEM((1, D), F32)], compiler_params=_cp(),
    )(Xf, target)


def _qk_fn(p, gain, cs, sneg, spos):
    y = p * lax.rsqrt(jnp.mean(p * p, axis=-1, keepdims=True) + EPS) * gain
    return _rope(y, cs, sneg, spos)


def _tab_specs(dm, swap):
    def idx(i):
        return jnp.where(i < dm.ntx, i % dm.tps, dm.tps)
    if swap:
        return [pl.BlockSpec((dm.TM, HEAD), lambda j, i: (idx(i), 0))] * 3
    return [pl.BlockSpec((dm.TM, HEAD), lambda i, j: (idx(i), 0))] * 3


def qkv_prep(P0, qkg, tabs, dm, name):
    TM = dm.TM

    def body(p_ref, g_ref, cs_ref, sn_ref, sp_ref, o_ref):
        j = pl.program_id(1)

        @pl.when(j < 6)
        def _():
            o_ref[...] = _qk_fn(p_ref[...], g_ref[0], cs_ref[...], sn_ref[...], sp_ref[...]).astype(BF16)

        @pl.when(j >= 6)
        def _():
            o_ref[...] = p_ref[...].astype(BF16)

    blk = pl.BlockSpec((TM, HEAD), lambda i, j: (i, j))
    return pl.pallas_call(
        body, name=name, grid=(dm.nt, 8),
        in_specs=[blk, pl.BlockSpec((1, 1, HEAD), lambda i, j: (jnp.minimum(j // 4, 1), 0, 0))] + _tab_specs(dm, False),
        out_specs=blk, out_shape=_sds((dm.T, 8 * HEAD), BF16), compiler_params=_cp(),
    )(P0, qkg, *tabs)


def qkv_prep_bwd(P0, dQKV, qkg, tabs, dm, name):
    TM = dm.TM

    def body(p_ref, d_ref, g_ref, cs_ref, sn_ref, sp_ref, dp_ref, dg_ref):
        j, i = pl.program_id(0), pl.program_id(1)
        first = (i == 0) & ((j == 0) | (j == 4))

        @pl.when(j < 6)
        def _():
            _, vjp = jax.vjp(_qk_fn, p_ref[...], g_ref[0], cs_ref[...], sn_ref[...], sp_ref[...])
            dp, dg = vjp(d_ref[...])[:2]
            dp_ref[...] = dp
            _acc(dg_ref, (0, slice(None), slice(None)), dg, first)

        @pl.when(j >= 6)
        def _():
            dp_ref[...] = d_ref[...]

    blk = pl.BlockSpec((TM, HEAD), lambda j, i: (i, j))
    return pl.pallas_call(
        body, name=name, grid=(8, dm.nt),
        in_specs=[blk, blk, pl.BlockSpec((1, 1, HEAD), lambda j, i: (jnp.minimum(j // 4, 1), 0, 0))] + _tab_specs(dm, True),
        out_specs=[blk, pl.BlockSpec((1, 1, HEAD), lambda j, i: (jnp.minimum(j // 4, 1), 0, 0))],
        out_shape=[_sds((dm.T, 8 * HEAD), F32), _sds((2, 1, HEAD), F32)], compiler_params=_cp(),
    )(P0, dQKV, qkg, *tabs)


def _softmax2(sx, sh):
    m = jnp.max(sh, axis=-1, keepdims=True)
    if sx is not None:
        m = jnp.maximum(m, jnp.max(sx, axis=-1, keepdims=True))
    eh = jnp.exp(sh - m)
    l = jnp.sum(eh, axis=-1, keepdims=True)
    ex = None
    if sx is not None:
        ex = jnp.exp(sx - m)
        l = l + jnp.sum(ex, axis=-1, keepdims=True)
    inv = 1.0 / l
    return (None if ex is None else ex * inv), eh * inv


def _attn_geometry(dm, with_x):
    TQ = dm.TM
    if with_x:
        nq, qoff = dm.N // TQ, 0
    else:
        nq, qoff = dm.M // TQ, dm.Tx // TQ
    hoff = dm.Tx // dm.M
    return TQ, nq, qoff, hoff


def attn_fwd(QKV, dm, with_x, name):
    TQ, nq, qoff, hoff = _attn_geometry(dm, with_x)
    scale = HEAD ** -0.5
    rows = dm.Tx if with_x else dm.Th

    def body(*refs):
        if with_x:
            q_ref, kh_ref, vh_ref, kx_ref, vx_ref, o_ref = refs
        else:
            q_ref, kh_ref, vh_ref, o_ref = refs
        q = q_ref[...]
        sh = _dot_nt(q, kh_ref[...]) * scale
        sx = _dot_nt(q, kx_ref[...]) * scale if with_x else None
        px, ph = _softmax2(sx, sh)
        o = _dot(ph, vh_ref[...])
        if with_x:
            o = o + _dot(px, vx_ref[...])
        o_ref[...] = o.astype(BF16)

    qs = pl.BlockSpec((TQ, HEAD), lambda b, kv, g, qi: (qoff + b * nq + qi, kv * 2 + g))
    in_specs = [qs, pl.BlockSpec((dm.M, HEAD), lambda b, kv, g, qi: (hoff + b, 4 + kv)),
                pl.BlockSpec((dm.M, HEAD), lambda b, kv, g, qi: (hoff + b, 6 + kv))]
    args = [QKV, QKV, QKV]
    if with_x:
        in_specs += [pl.BlockSpec((dm.N, HEAD), lambda b, kv, g, qi: (b, 4 + kv)),
                     pl.BlockSpec((dm.N, HEAD), lambda b, kv, g, qi: (b, 6 + kv))]
        args += [QKV, QKV]
    return pl.pallas_call(
        body, name=name, grid=(dm.Bl, A_KV, 2, nq), in_specs=in_specs,
        out_specs=pl.BlockSpec((TQ, HEAD), lambda b, kv, g, qi: (b * nq + qi, kv * 2 + g)),
        out_shape=_sds((rows, A_HEADS * HEAD), BF16), compiler_params=_cp(),
    )(*args)


def attn_bwd(QKV, dO, dm, with_x, init, name):
    TQ, nq, qoff, hoff = _attn_geometry(dm, with_x)
    scale = HEAD ** -0.5
    rows = dm.Tx if with_x else dm.Th

    def body(*refs):
        if with_x:
            (q_ref, kh_ref, vh_ref, kx_ref, vx_ref, do_ref, ikh_ref, ivh_ref,
             dq_ref, dkh_ref, dvh_ref, dkx_ref, dvx_ref) = refs
        else:
            q_ref, kh_ref, vh_ref, do_ref, dq_ref, dkh_ref, dvh_ref = refs
        g, qi = pl.program_id(2), pl.program_id(3)
        q = q_ref[...]
        kh, vh = kh_ref[...], vh_ref[...]
        sh = _dot_nt(q, kh) * scale
        sx = _dot_nt(q, kx_ref[...]) * scale if with_x else None
        px, ph = _softmax2(sx, sh)
        dob = do_ref[...].astype(BF16)
        dph = _dot_nt(dob, vh)
        delta = jnp.sum(dph * ph, axis=-1, keepdims=True)
        if with_x:
            dpx = _dot_nt(dob, vx_ref[...])
            delta = delta + jnp.sum(dpx * px, axis=-1, keepdims=True)
        dsh = (ph * (dph - delta) * scale).astype(BF16)
        dq = _dot(dsh, kh)
        first = (g == 0) & (qi == 0)

        @pl.when(first)
        def _():
            if with_x:
                dkh_ref[...] = ikh_ref[...]
                dvh_ref[...] = ivh_ref[...]
                dkx_ref[...] = jnp.zeros_like(dkx_ref)
                dvx_ref[...] = jnp.zeros_like(dvx_ref)
            else:
                dkh_ref[...] = jnp.zeros_like(dkh_ref)
                dvh_ref[...] = jnp.zeros_like(dvh_ref)

        dkh_ref[...] += _dot_tn(dsh, q)
        dvh_ref[...] += _dot_tn(ph, dob)
        if with_x:
            dsx = (px * (dpx - delta) * scale).astype(BF16)
            dq = dq + _dot(dsx, kx_ref[...])
            dkx_ref[...] += _dot_tn(dsx, q)
            dvx_ref[...] += _dot_tn(px, dob)
        dq_ref[...] = dq

    qs = pl.BlockSpec((TQ, HEAD), lambda b, kv, g, qi: (qoff + b * nq + qi, kv * 2 + g))
    hs = lambda c0: pl.BlockSpec((dm.M, HEAD), lambda b, kv, g, qi: (hoff + b, c0 + kv))
    xs = lambda c0: pl.BlockSpec((dm.N, HEAD), lambda b, kv, g, qi: (b, c0 + kv))
    dos = pl.BlockSpec((TQ, HEAD), lambda b, kv, g, qi: (b * nq + qi, kv * 2 + g))
    acc_h = pl.BlockSpec((dm.M, HEAD), lambda b, kv, g, qi: (b, kv))
    acc_x = pl.BlockSpec((dm.N, HEAD), lambda b, kv, g, qi: (b, kv))
    in_specs, args = [qs, hs(4), hs(6)], [QKV, QKV, QKV]
    out_specs = [dos, acc_h, acc_h]
    out_shape = [_sds((rows, A_HEADS * HEAD), F32), _sds((dm.Th, A_KV * HEAD), F32), _sds((dm.Th, A_KV * HEAD), F32)]
    if with_x:
        in_specs += [xs(4), xs(6), dos, acc_h, acc_h]
        args += [QKV, QKV, dO, init[0], init[1]]
        out_specs += [acc_x, acc_x]
        out_shape += [_sds((dm.Tx, A_KV * HEAD), F32), _sds((dm.Tx, A_KV * HEAD), F32)]
    else:
        in_specs += [dos]
        args += [dO]
    return pl.pallas_call(
        body, name=name, grid=(dm.Bl, A_KV, 2, nq), in_specs=in_specs, out_specs=out_specs,
        out_shape=out_shape, compiler_params=_cp(),
    )(*args)


def _pool_mean(u, w):
    n = u.shape[0]
    t = lax.broadcasted_iota(jnp.int32, (n, 1), 0)
    cnt = (jnp.clip(t + (w - w // 2), 0, n) - jnp.clip(t - w // 2, 0, n)).astype(F32)
    s = _shift_rows(u, -(w // 2))
    for j in range(-(w // 2) + 1, w - w // 2):
        s = s + _shift_rows(u, j)
    return s / cnt - u


def pool_fwd(P0, pw, pscale, dm, on_x, name):
    n, off, rows = (dm.N, 0, dm.Tx) if on_x else (dm.M, dm.Tx // dm.M, dm.Th)
    ng = len(POOL_WINDOWS)

    def body(u_ref, w_ref, s_ref, o_ref):
        for g, w in enumerate(POOL_WINDOWS):
            cols = pl.ds(g * HEAD, HEAD)
            pooled = _pool_mean(u_ref[:, cols], w)
            o_ref[:, cols] = (_dot(pooled, w_ref[g]) * s_ref[:, cols]).astype(BF16)

    return pl.pallas_call(
        body, name=name, grid=(dm.Bl,),
        in_specs=[pl.BlockSpec((n, ng * HEAD), lambda b: (off + b, 2)),
                  pl.BlockSpec((ng, HEAD, HEAD), lambda b: (0, 0, 0)), pl.BlockSpec((1, ng * HEAD), lambda b: (0, 0))],
        out_specs=pl.BlockSpec((n, ng * HEAD), lambda b: (b, 0)),
        out_shape=_sds((rows, ng * HEAD), BF16), compiler_params=_cp(),
    )(P0, pw, pscale)


def pool_bwd(P0, dY, pw, pwT, pscale, dm, on_x, name):
    n, off, rows = (dm.N, 0, dm.Tx) if on_x else (dm.M, dm.Tx // dm.M, dm.Th)
    ng = len(POOL_WINDOWS)

    def body(u_ref, dy_ref, w_ref, wt_ref, s_ref, du_ref, dw_ref, ds_ref):
        b = pl.program_id(0)
        for g, w in enumerate(POOL_WINDOWS):
            cols = pl.ds(g * HEAD, HEAD)
            pooled, vjp = jax.vjp(lambda u: _pool_mean(u, w), u_ref[:, cols])
            pre = _dot(pooled, w_ref[g])
            dy = dy_ref[:, cols]
            dpre = dy * s_ref[:, cols]
            du_ref[:, cols] = vjp(_dot(dpre, wt_ref[g]))[0]
            _acc(dw_ref, (g, slice(None), slice(None)), _dot_tn(pooled, dpre), b == 0)
            _acc(ds_ref, (slice(None), cols), jnp.sum(dy * pre, axis=0, keepdims=True), b == 0)

    full = pl.BlockSpec((ng, HEAD, HEAD), lambda b: (0, 0, 0))
    vec = pl.BlockSpec((1, ng * HEAD), lambda b: (0, 0))
    return pl.pallas_call(
        body, name=name, grid=(dm.Bl,),
        in_specs=[pl.BlockSpec((n, ng * HEAD), lambda b: (off + b, 2)), pl.BlockSpec((n, ng * HEAD), lambda b: (b, 0)),
                  full, full, vec],
        out_specs=[pl.BlockSpec((n, ng * HEAD), lambda b: (b, 0)), full, vec],
        out_shape=[_sds((rows, ng * HEAD), F32), _sds((ng, HEAD, HEAD), F32), _sds((1, ng * HEAD), F32)],
        compiler_params=_cp(),
    )(P0, dY, pw, pwT, pscale)


def _conv_fn(p, w0, w1, w2, kind):
    c = w0 * _shift_rows(p, -1) + w1 * p + w2 * _shift_rows(p, 1)
    a = _silu(c)
    if kind == 2:
        return a
    a = a * lax.rsqrt(jnp.sum(a * a, axis=-1, keepdims=True) + EPS)
    return a * (HEAD ** -0.5) if kind == 0 else a


def gdn_prep(P1, conv_w, dm, on_x, name):
    n, off, rows = (dm.N, 0, dm.Tx) if on_x else (dm.M, dm.Tx // dm.M, dm.Th)

    def body(p_ref, w_ref, o_ref):
        j = pl.program_id(1)
        p, w = p_ref[...], w_ref[...]
        for kind in range(3):
            @pl.when(j // C_HEADS == kind)
            def _():
                o_ref[...] = _conv_fn(p, w[0:1], w[1:2], w[2:3], kind)

    return pl.pallas_call(
        body, name=name, grid=(dm.Bl, 3 * C_HEADS),
        in_specs=[pl.BlockSpec((n, HEAD), lambda b, j: (off + b, j)), pl.BlockSpec((3, HEAD), lambda b, j: (0, j))],
        out_specs=pl.BlockSpec((n, HEAD), lambda b, j: (b, j)),
        out_shape=_sds((rows, 3 * C_HEADS * HEAD), F32), compiler_params=_cp(),
    )(P1, conv_w)


def gdn_prep_bwd(P1, dQ, conv_w, dm, on_x, name):
    n, off, rows = (dm.N, 0, dm.Tx) if on_x else (dm.M, dm.Tx // dm.M, dm.Th)

    def body(p_ref, d0_ref, d1_ref, w_ref, dp_ref, dw_ref):
        j, b = pl.program_id(0), pl.program_id(1)
        p, w = p_ref[...], w_ref[...]
        for kind in range(3):
            @pl.when(j // C_HEADS == kind)
            def _():
                _, vjp = jax.vjp(functools.partial(_conv_fn, kind=kind), p, w[0:1], w[1:2], w[2:3])
                dp, d0, d1, d2 = vjp(d0_ref[0] + d1_ref[0])
                dp_ref[...] = dp
                _acc(dw_ref, (pl.ds(0, 1), slice(None)), d0, b == 0)
                _acc(dw_ref, (pl.ds(1, 1), slice(None)), d1, b == 0)
                _acc(dw_ref, (pl.ds(2, 1), slice(None)), d2, b == 0)

    return pl.pallas_call(
        body, name=name, grid=(3 * C_HEADS, dm.Bl),
        in_specs=[pl.BlockSpec((n, HEAD), lambda j, b: (off + b, j)),
                  pl.BlockSpec((1, n, HEAD), lambda j, b: (0, off + b, j)), pl.BlockSpec((1, n, HEAD), lambda j, b: (1, off + b, j)),
                  pl.BlockSpec((3, HEAD), lambda j, b: (0, j))],
        out_specs=[pl.BlockSpec((n, HEAD), lambda j, b: (b, j)), pl.BlockSpec((3, HEAD), lambda j, b: (0, j))],
        out_shape=[_sds((rows, 3 * C_HEADS * HEAD), F32), _sds((3, 3 * C_HEADS * HEAD), F32)],
        compiler_params=_cp(),
    )(P1, dQ, dQ, conv_w)


def _gate_fn(ab, par):
    lane = lax.broadcasted_iota(jnp.int32, ab.shape, 1)
    is_a = (lane % 16) < C_HEADS
    g = -jnp.exp(par[0:1]) * jax.nn.softplus(ab + par[1:2])
    return jnp.where(lane < 4 * C_HEADS, jnp.where(is_a, g, jax.nn.sigmoid(ab)), 0.0)


def _col(blk, idx):
    lane = lax.broadcasted_iota(jnp.int32, blk.shape, 1)
    return jnp.sum(jnp.where(lane == idx, blk, 0.0), axis=1, keepdims=True)


def _chunk_masks(rev):
    ii = lax.broadcasted_iota(jnp.int32, (CHUNK, CHUNK), 0)
    jj = lax.broadcasted_iota(jnp.int32, (CHUNK, CHUNK), 1)
    ahead = jnp.where(rev, jj - ii, ii - jj)
    return ahead >= 0, ahead > 0, (ii == jj).astype(F32)


def _inv_unit_tri(nmats, eye):
    xs = [eye - n for n in nmats]
    ps = [_hdot(n, n) for n in nmats]
    step = 2
    while True:
        xs = [x + _hdot(x, p) for x, p in zip(xs, ps)]
        step *= 2
        if step >= CHUNK:
            break
        ps = [_hdot(p, p) for p in ps]
    return xs


def _cum_lanes(x, transpose=False):
    lane = lax.broadcasted_iota(jnp.int32, x.shape, 1)
    down, up = x, x
    s = 1
    while s < CHUNK:
        down = down + _shift_rows(down, -s)
        up = up + _shift_rows(up, s)
        s *= 2
    return jnp.where((lane >= 16) if transpose else (lane < 16), down, up)


def _each(f, *lists):
    return [f(*a) for a in zip(*lists)]


def _chunk_common(qs, ks, vs, gcs, gcrs, tots, betas, rev, saved=None):
    incl, strict, eye = _chunk_masks(rev)
    es = _each(lambda gc, gcr: jnp.exp(jnp.where(incl, gc - gcr, NEG)), gcs, gcrs)
    egs = [jnp.exp(gc) for gc in gcs]
    ets = _each(lambda t, gc: jnp.exp(t - gc), tots, gcs)
    gts = [jnp.exp(t) for t in tots]
    kbs = _each(lambda k, b: k * b, ks, betas)
    kks = _each(_dot_nt, kbs, ks)
    qqs = _each(_dot_nt, qs, ks)
    if saved is None:
        nmats = _each(lambda kk, e: jnp.where(strict, kk * e, 0.0), kks, es)
        ainvs = _inv_unit_tri(nmats, eye)
        rhss = _each(lambda v, b, kb, eg: jnp.concatenate([v * b, kb * eg], axis=1), vs, betas, kbs, egs)
        sols = _each(_hdot, ainvs, rhss)
    else:
        ainvs, sols = saved
    return dict(incl=incl, strict=strict, e=es, eg=egs, et=ets, gt=gts, kb=kbs, kk=kks, ainv=ainvs, sol=sols, qq=qqs)


def _chunk_fwd(qs, ks, vs, gcs, gcrs, tots, betas, rev):
    c = _chunk_common(qs, ks, vs, gcs, gcrs, tots, betas, rev)
    incl = c["incl"]
    return _each(lambda q, k, sol, qq, e, et, eg, gt, ainv:
                 (sol[:, :HEAD], sol[:, HEAD:], k * et, q * eg, jnp.where(incl, qq * e, 0.0), gt, ainv),
                 qs, ks, c["sol"], c["qq"], c["e"], c["et"], c["eg"], c["gt"], c["ainv"])


def _chunk_bwd(qs, ks, vs, gcs, gcrs, tots, betas, rev, ainvs, sols, dus, dws, dkts, dqds, dqks, dgts):
    c = _chunk_common(qs, ks, vs, gcs, gcrs, tots, betas, rev, saved=(ainvs, sols))
    incl, strict = c["incl"], c["strict"]
    drhss = _each(lambda a, du, dw: _hdot_tn(a, jnp.concatenate([du, dw], axis=1)), c["ainv"], dus, dws)
    dns = _each(lambda drhs, sol: jnp.where(strict, -_hdot_nt(drhs, sol), 0.0), drhss, c["sol"])
    dkks = _each(lambda dn, e: dn * e, dns, c["e"])
    dqms = [jnp.where(incl, dqk, 0.0) for dqk in dqks]
    dqqs = _each(lambda dqm, e: dqm * e, dqms, c["e"])
    m_q = _each(_dot, dqqs, ks)
    m_k1 = _each(_dot_tn, dqqs, qs)
    m_k2 = _each(_dot_tn, dkks, c["kb"])
    m_kb = _each(_dot, dkks, ks)

    def finish(q, k, v, beta, e, eg, et, gt, kb, kk, qq, drhs, dn, dqm, dkt, dqd, dgt, mq, mk1, mk2, mkb):
        de = dn * kk + dqm * qq
        dq = mq + dqd * eg
        dkb = mkb + drhs[:, HEAD:] * eg
        dk = mk1 + mk2 + dkt * et + dkb * beta
        dv = drhs[:, :HEAD] * beta
        dbeta = jnp.sum(drhs[:, :HEAD] * v + dkb * k, axis=1, keepdims=True)
        deg = jnp.sum(drhs[:, HEAD:] * kb + dqd * q, axis=1, keepdims=True)
        dd = de * e
        dtd = jnp.sum(dkt * k, axis=1, keepdims=True) * et
        dgc = deg * eg - dtd + jnp.sum(dd, axis=1, keepdims=True) - jnp.sum(dd.T, axis=1, keepdims=True)
        dtot = jnp.sum(dtd, axis=0, keepdims=True) + dgt * gt
        return dq, dk, dv, dgc, dtot, dbeta

    return _each(finish, qs, ks, vs, betas, c["e"], c["eg"], c["et"], c["gt"], c["kb"], c["kk"], c["qq"],
                 drhss, dns, dqms, dkts, dqds, dgts, m_q, m_k1, m_k2, m_kb)


def gdn_chunk_pre(QKVg, P1, par, dm, name):
    nch = dm.T // CHUNK
    HD = C_HEADS * HEAD
    abcol = (4 * HD) // LANE

    def body(x_ref, ab_ref, par_ref, u_ref, w_ref, kt_ref, qd_ref, qk_ref, gt_ref, wf_ref, ai_ref, gct_s):
        d = pl.program_id(1)
        rev = d == 1
        gb = _gate_fn(ab_ref[...], par_ref[...])
        gcl = _cum_lanes(gb)
        gct_s[...] = gcl.T
        tot = jnp.sum(gb, axis=0, keepdims=True)
        hs = range(C_HEADS)
        outs = _chunk_fwd(
            [x_ref[:, pl.ds(h * HEAD, HEAD)] for h in hs],
            [x_ref[:, pl.ds((C_HEADS + h) * HEAD, HEAD)] for h in hs],
            [x_ref[:, pl.ds((2 * C_HEADS + h) * HEAD, HEAD)] for h in hs],
            [_col(gcl, d * 16 + h) for h in hs], [gct_s[pl.ds(d * 16 + h, 1), :] for h in hs],
            [_col(tot, d * 16 + h) for h in hs], [_col(gb, d * 16 + 8 + h) for h in hs], rev)
        for h, (u, w, kt, qd, qk, gt, ainv) in enumerate(outs):
            cols = pl.ds(h * HEAD, HEAD)
            u_ref[0, :, cols] = u
            w_ref[0, :, cols] = w.astype(BF16)
            kt_ref[0, :, cols] = kt.astype(BF16)
            qd_ref[0, :, cols] = qd.astype(BF16)
            qk_ref[0, :, cols] = jnp.concatenate([qk, jnp.zeros_like(qk)], axis=1).astype(BF16)
            gt_ref[0, 0, pl.ds(h, 1), :] = jnp.broadcast_to(gt, (1, HEAD))
            wf_ref[0, :, cols] = w
            ai_ref[0, :, cols] = jnp.concatenate([ainv, jnp.zeros_like(ainv)], axis=1)

    big = pl.BlockSpec((1, CHUNK, HD), lambda i, d: (d, i, 0))
    return pl.pallas_call(
        body, name=name, grid=(nch, 2),
        in_specs=[pl.BlockSpec((CHUNK, 3 * HD), lambda i, d: (i, 0)), pl.BlockSpec((CHUNK, LANE), lambda i, d: (i, abcol)),
                  pl.BlockSpec((2, LANE), lambda i, d: (0, 0))],
        out_specs=[big, big, big, big, big, pl.BlockSpec((1, 1, C_HEADS, HEAD), lambda i, d: (d, i, 0, 0)), big, big],
        out_shape=[_sds((2, dm.T, HD), F32), _sds((2, dm.T, HD), BF16), _sds((2, dm.T, HD), BF16),
                   _sds((2, dm.T, HD), BF16), _sds((2, dm.T, HD), BF16), _sds((2, nch, C_HEADS, HEAD), F32),
                   _sds((2, dm.T, HD), F32), _sds((2, dm.T, HD), F32)],
        scratch_shapes=[pltpu.VMEM((LANE, CHUNK), F32)], compiler_params=_cp(),
    )(QKVg, P1, par)


def gdn_chunk_pre_bwd(QKVg, P1, par, U, WF, AI, dU, dW, dKT, dQD, dQK, dGT, dm, name):
    nch = dm.T // CHUNK
    HD = C_HEADS * HEAD
    abcol = (4 * HD) // LANE

    def body(x_ref, ab_ref, par_ref, u_ref, wf_ref, ai_ref, du_ref, dw_ref, dkt_ref, dqd_ref, dqk_ref, dgt_ref,
             dx_ref, dab_ref, dpar_ref, gct_s):
        i, d = pl.program_id(0), pl.program_id(1)
        rev = d == 1
        ab, par = ab_ref[...], par_ref[...]
        gb, gate_vjp = jax.vjp(_gate_fn, ab, par)
        gcl = _cum_lanes(gb)
        gct_s[...] = gcl.T
        tot = jnp.sum(gb, axis=0, keepdims=True)
        lane = lax.broadcasted_iota(jnp.int32, (CHUNK, LANE), 1)
        dgcl = jnp.zeros((CHUNK, LANE), F32)
        dgb = jnp.zeros((CHUNK, LANE), F32)
        first = d == 0
        hs = range(C_HEADS)
        hcols = [pl.ds(h * HEAD, HEAD) for h in hs]
        outs = _chunk_bwd(
            [x_ref[:, c] for c in hcols],
            [x_ref[:, pl.ds((C_HEADS + h) * HEAD, HEAD)] for h in hs],
            [x_ref[:, pl.ds((2 * C_HEADS + h) * HEAD, HEAD)] for h in hs],
            [_col(gcl, d * 16 + h) for h in hs], [gct_s[pl.ds(d * 16 + h, 1), :] for h in hs],
            [_col(tot, d * 16 + h) for h in hs], [_col(gb, d * 16 + 8 + h) for h in hs], rev,
            [ai_ref[0, :, pl.ds(h * HEAD, CHUNK)] for h in hs],
            [jnp.concatenate([u_ref[0, :, c], wf_ref[0, :, c]], axis=1) for c in hcols],
            [du_ref[0, :, c] for c in hcols], [dw_ref[0, :, c] for c in hcols], [dkt_ref[0, :, c] for c in hcols],
            [dqd_ref[0, :, c] for c in hcols], [dqk_ref[0, :, pl.ds(h * HEAD, CHUNK)] for h in hs],
            [dgt_ref[0, 0, pl.ds(h, 1), pl.ds(0, 1)] for h in hs])
        for h, (dq, dk, dv, dgc, dtotal, dbeta) in enumerate(outs):
            idx = d * 16 + h
            dx_ref[0, :, hcols[h]] = dq
            dx_ref[0, :, pl.ds((C_HEADS + h) * HEAD, HEAD)] = dk
            dx_ref[0, :, pl.ds((2 * C_HEADS + h) * HEAD, HEAD)] = dv
            dgcl = dgcl + jnp.where(lane == idx, dgc, 0.0)
            dgb = dgb + jnp.where(lane == idx + 8, dbeta, 0.0) + jnp.where(lane == idx, dtotal, 0.0)
        dab, dpar = gate_vjp(dgb + _cum_lanes(dgcl, transpose=True))
        dab_ref[0] = dab
        _acc(dpar_ref, (slice(None), slice(None)), dpar, (i == 0) & first)

    big = pl.BlockSpec((1, CHUNK, HD), lambda i, d: (d, i, 0))
    return pl.pallas_call(
        body, name=name, grid=(nch, 2),
        in_specs=[pl.BlockSpec((CHUNK, 3 * HD), lambda i, d: (i, 0)), pl.BlockSpec((CHUNK, LANE), lambda i, d: (i, abcol)),
                  pl.BlockSpec((2, LANE), lambda i, d: (0, 0)), big, big, big, big, big, big, big, big,
                  pl.BlockSpec((1, 1, C_HEADS, HEAD), lambda i, d: (d, i, 0, 0))],
        out_specs=[pl.BlockSpec((1, CHUNK, 3 * HD), lambda i, d: (d, i, 0)), pl.BlockSpec((1, CHUNK, LANE), lambda i, d: (d, i, 0)),
                   pl.BlockSpec((2, LANE), lambda i, d: (0, 0))],
        out_shape=[_sds((2, dm.T, 3 * HD), F32), _sds((2, dm.T, LANE), F32), _sds((2, LANE), F32)],
        scratch_shapes=[pltpu.VMEM((LANE, CHUNK), F32)], compiler_params=_cp(),
    )(QKVg, P1, par, U, WF, AI, dU, dW, dKT, dQD, dQK, dGT)


def _scan_chunk(b, d, c, dm):
    nh, nx = dm.M // CHUNK, dm.N // CHUNK
    in_h = c < nh
    pos_h = jnp.where(d == 0, c, nh - 1 - c)
    pos_x = jnp.where(d == 0, c - nh, nx - 1 - (c - nh))
    return jnp.where(in_h, dm.Tx // CHUNK + b * nh + pos_h, b * nx + pos_x)


def gdn_scan_fwd(U, W, KT, QD, QK, GT, dm, name):
    nch = dm.T // CHUNK
    HD = C_HEADS * HEAD
    nsc = (dm.M + dm.N) // CHUNK

    def body(u_ref, w_ref, kt_ref, qd_ref, qk_ref, gt_ref, o_ref, ss_ref, s_s):
        @pl.when(pl.program_id(2) == 0)
        def _():
            s_s[...] = jnp.zeros_like(s_s)

        hs = range(C_HEADS)
        blk = [pl.ds(h * HEAD, HEAD) for h in hs]
        ss = [s_s[b, :] for b in blk]
        for b, s in zip(blk, ss):
            ss_ref[0, 0, b, :] = s
        sbs = [s.astype(BF16) for s in ss]
        ws = [jnp.dot(w_ref[0, :, b], sb, preferred_element_type=F32) for b, sb in zip(blk, sbs)]
        os1 = [jnp.dot(qd_ref[0, :, b], sb, preferred_element_type=F32) for b, sb in zip(blk, sbs)]
        vnbs = [(u_ref[0, :, b] - wv).astype(BF16) for b, wv in zip(blk, ws)]
        os2 = [jnp.dot(qk_ref[0, :, pl.ds(h * HEAD, CHUNK)], vnbs[h], preferred_element_type=F32) for h in hs]
        upd = [_dot_tn(kt_ref[0, :, b], vnb) for b, vnb in zip(blk, vnbs)]
        for h in hs:
            o_ref[0, :, blk[h]] = os1[h] + os2[h]
            s_s[blk[h], :] = ss[h] * gt_ref[0, 0, pl.ds(h, 1), :] + upd[h]

    big = pl.BlockSpec((1, CHUNK, HD), lambda b, d, c: (d, _scan_chunk(b, d, c, dm), 0))
    return pl.pallas_call(
        body, name=name, grid=(dm.Bl, 2, nsc),
        in_specs=[big, big, big, big, big,
                  pl.BlockSpec((1, 1, C_HEADS, HEAD), lambda b, d, c: (d, _scan_chunk(b, d, c, dm), 0, 0))],
        out_specs=[big, pl.BlockSpec((1, 1, HD, HEAD), lambda b, d, c: (d, _scan_chunk(b, d, c, dm), 0, 0))],
        out_shape=[_sds((2, dm.T, HD), F32), _sds((2, nch, HD, HEAD), F32)],
        scratch_shapes=[pltpu.VMEM((HD, HEAD), F32)], compiler_params=_cp(),
    )(U, W, KT, QD, QK, GT)


def gdn_scan_bwd(dO, SS, U, W, KT, QD, QK, GT, dm, name):
    nch = dm.T // CHUNK
    HD = C_HEADS * HEAD
    nsc = (dm.M + dm.N) // CHUNK

    def body(do_ref, ss_ref, u_ref, w_ref, kt_ref, qd_ref, qk_ref, gt_ref,
             du_ref, dw_ref, dkt_ref, dqd_ref, dqk_ref, dgt_ref, ds_s):
        @pl.when(pl.program_id(2) == 0)
        def _():
            ds_s[...] = jnp.zeros_like(ds_s)

        hs = range(C_HEADS)
        blk = [pl.ds(h * HEAD, HEAD) for h in hs]
        ss = [ss_ref[0, 0, b, :] for b in blk]
        sbs = [s.astype(BF16) for s in ss]
        dobs = [do_ref[:, b].astype(BF16) for b in blk]
        dsns = [ds_s[b, :] for b in blk]
        dsnbs = [t.astype(BF16) for t in dsns]
        wss = [jnp.dot(w_ref[0, :, b], sb, preferred_element_type=F32) for b, sb in zip(blk, sbs)]
        dqds = [_dot_nt(dob, sb) for dob, sb in zip(dobs, sbs)]
        dv1 = [_dot_tn(qk_ref[0, :, pl.ds(h * HEAD, CHUNK)], dobs[h]) for h in hs]
        dv2 = [jnp.dot(kt_ref[0, :, b], t, preferred_element_type=F32) for b, t in zip(blk, dsnbs)]
        ds1 = [_dot_tn(qd_ref[0, :, b], dob) for b, dob in zip(blk, dobs)]
        vnbs = [(u_ref[0, :, b] - wv).astype(BF16) for b, wv in zip(blk, wss)]
        dvns = [a + b for a, b in zip(dv1, dv2)]
        dvnbs = [t.astype(BF16) for t in dvns]
        dqks = [_dot_nt(dob, vnb) for dob, vnb in zip(dobs, vnbs)]
        dkts = [_dot_nt(vnb, t) for vnb, t in zip(vnbs, dsnbs)]
        dws = [_dot_nt(t, sb) for t, sb in zip(dvnbs, sbs)]
        ds2 = [_dot_tn(w_ref[0, :, b], t) for b, t in zip(blk, dvnbs)]
        for h in hs:
            b = blk[h]
            dqd_ref[0, :, b] = dqds[h]
            dqk_ref[0, :, b] = jnp.concatenate([dqks[h], jnp.zeros_like(dqks[h])], axis=1)
            dkt_ref[0, :, b] = dkts[h]
            du_ref[0, :, b] = dvns[h]
            dw_ref[0, :, b] = -dws[h]
            dgt_ref[0, 0, pl.ds(h, 1), :] = jnp.broadcast_to(jnp.sum(dsns[h] * ss[h], keepdims=True), (1, HEAD))
            ds_s[b, :] = dsns[h] * gt_ref[0, 0, pl.ds(h, 1), :] + ds1[h] - ds2[h]

    def mem(b, d, c):
        return _scan_chunk(b, d, nsc - 1 - c, dm)

    big = pl.BlockSpec((1, CHUNK, HD), lambda b, d, c: (d, mem(b, d, c), 0))
    gts = pl.BlockSpec((1, 1, C_HEADS, HEAD), lambda b, d, c: (d, mem(b, d, c), 0, 0))
    return pl.pallas_call(
        body, name=name, grid=(dm.Bl, 2, nsc),
        in_specs=[pl.BlockSpec((CHUNK, HD), lambda b, d, c: (mem(b, d, c), 0)),
                  pl.BlockSpec((1, 1, HD, HEAD), lambda b, d, c: (d, mem(b, d, c), 0, 0)), big, big, big, big, big, gts],
        out_specs=[big, big, big, big, big, gts],
        out_shape=[_sds((2, dm.T, HD), F32)] * 5 + [_sds((2, nch, C_HEADS, HEAD), F32)],
        scratch_shapes=[pltpu.VMEM((HD, HEAD), F32)], compiler_params=_cp(),
    )(dO, SS, U, W, KT, QD, QK, GT)


def _finish_fn(o, z, gain):
    y = o * lax.rsqrt(jnp.mean(o * o, axis=-1, keepdims=True) + EPS) * gain
    return y * _silu(z)


def gdn_finish(O, P1, og, dm, name):
    TM = dm.TM
    HD = C_HEADS * HEAD
    zc = (3 * HD) // HEAD

    def body(o0_ref, o1_ref, z_ref, g_ref, y_ref):
        y_ref[...] = _finish_fn(o0_ref[0] + o1_ref[0], z_ref[...], g_ref[...]).astype(BF16)

    return pl.pallas_call(
        body, name=name, grid=(dm.ntx, C_HEADS),
        in_specs=[pl.BlockSpec((1, TM, HEAD), lambda i, j: (0, i, j)), pl.BlockSpec((1, TM, HEAD), lambda i, j: (1, i, j)),
                  pl.BlockSpec((TM, HEAD), lambda i, j: (i, zc + j)), pl.BlockSpec((1, HEAD), lambda i, j: (0, 0))],
        out_specs=pl.BlockSpec((TM, HEAD), lambda i, j: (i, j)),
        out_shape=_sds((dm.Tx, HD), BF16), compiler_params=_cp(),
    )(O, O, P1, og)


def gdn_finish_bwd(O, P1, og, dY, dm, name):
    TM = dm.TM
    HD = C_HEADS * HEAD
    zc = (3 * HD) // HEAD

    def body(o0_ref, o1_ref, z_ref, g_ref, dy_ref, do_ref, dz_ref, dg_ref):
        i, j = pl.program_id(0), pl.program_id(1)
        _, vjp = jax.vjp(_finish_fn, o0_ref[0] + o1_ref[0], z_ref[...], g_ref[...])
        do, dz, dg = vjp(dy_ref[...])
        do_ref[...] = do
        dz_ref[...] = dz
        _acc(dg_ref, (slice(None), slice(None)), dg, (i == 0) & (j == 0))

    blk = pl.BlockSpec((TM, HEAD), lambda i, j: (i, j))
    return pl.pallas_call(
        body, name=name, grid=(dm.ntx, C_HEADS),
        in_specs=[pl.BlockSpec((1, TM, HEAD), lambda i, j: (0, i, j)), pl.BlockSpec((1, TM, HEAD), lambda i, j: (1, i, j)),
                  pl.BlockSpec((TM, HEAD), lambda i, j: (i, zc + j)), pl.BlockSpec((1, HEAD), lambda i, j: (0, 0)), blk],
        out_specs=[blk, blk, pl.BlockSpec((1, HEAD), lambda i, j: (0, 0))],
        out_shape=[_sds((dm.Tx, HD), F32), _sds((dm.Tx, HD), F32), _sds((1, HEAD), F32)],
        compiler_params=_cp(),
    )(O, O, P1, og, dY)


def adaln_fwd(c_ext, w_mod, b_loc, name):
    R, D = c_ext.shape
    nl = w_mod.shape[2]
    tn = _pick(nl, 384)

    def body(c_ref, w_ref, b_ref, o_ref):
        o_ref[0] = _dot(_silu(c_ref[...]), w_ref[0]) + b_ref[0]

    return pl.pallas_call(
        body, name=name, grid=(2, nl // tn),
        in_specs=[pl.BlockSpec((R, D), lambda l, j: (0, 0)), pl.BlockSpec((1, D, tn), lambda l, j: (l, 0, j)),
                  pl.BlockSpec((1, 1, tn), lambda l, j: (l, 0, j))],
        out_specs=pl.BlockSpec((1, R, tn), lambda l, j: (l, 0, j)),
        out_shape=_sds((2, R, nl), F32), compiler_params=_cp(),
    )(c_ext, w_mod, b_loc)


def adaln_bwd(c_ext, c_ctx, w_mod, dmx, dmh, nb, name):
    R, D = c_ext.shape
    nl = w_mod.shape[2]
    tn = _pick(nl, 384)
    nj = nl // tn

    def body(c_ref, cc_ref, w_ref, dmx_ref, dmh_ref, gw_ref, dc_ref):
        l, j = pl.program_id(0), pl.program_id(1)
        dh = dmh_ref[0, 0:1, :]
        for k in range(1, N_DEV):
            dh = dh + dmh_ref[0, k:k + 1, :]
        row = lax.broadcasted_iota(jnp.int32, (R, tn), 0)
        dmat = dmx_ref[0] + jnp.where(row == nb, dh, 0.0)
        gw_ref[0] = _dot_tn(_silu(c_ref[...]), dmat)
        part = _dot_nt(jnp.broadcast_to(dh, (8, tn)), w_ref[0])[0:1]
        _acc(dc_ref, (slice(None), slice(None)), part, (l == 0) & (j == 0))

        @pl.when((l == 1) & (j == nj - 1))
        def _():
            cc = cc_ref[...]
            sg = jax.nn.sigmoid(cc)
            dc_ref[...] = dc_ref[...] * (sg * (1.0 + cc * (1.0 - sg)))

    return pl.pallas_call(
        body, name=name, grid=(2, nj),
        in_specs=[pl.BlockSpec((R, D), lambda l, j: (0, 0)), pl.BlockSpec((1, D), lambda l, j: (0, 0)),
                  pl.BlockSpec((1, D, tn), lambda l, j: (l, 0, j)), pl.BlockSpec((1, R, tn), lambda l, j: (l, 0, j)),
                  pl.BlockSpec((1, N_DEV, tn), lambda l, j: (l, 0, j))],
        out_specs=[pl.BlockSpec((1, D, tn), lambda l, j: (l, 0, j)), pl.BlockSpec((1, D), lambda l, j: (0, 0))],
        out_shape=[_sds((2, D, nl), F32), _sds((1, D), F32)], compiler_params=_cp(),
    )(c_ext, c_ctx, w_mod, dmx, dmh)


def bmod_grad(dmx, dmh, name):
    _, R, n9 = dmx.shape

    def body(dmx_ref, dmh_ref, o_ref):
        o_ref[0] = jnp.sum(dmx_ref[0], axis=0, keepdims=True) + jnp.sum(dmh_ref[0], axis=0, keepdims=True)

    return pl.pallas_call(
        body, name=name, grid=(2,),
        in_specs=[pl.BlockSpec((1, R, n9), lambda l: (l, 0, 0)), pl.BlockSpec((1, N_DEV, n9), lambda l: (l, 0, 0))],
        out_specs=pl.BlockSpec((1, 1, n9), lambda l: (l, 0, 0)), out_shape=_sds((2, 1, n9), F32),
        compiler_params=_cp(),
    )(dmx, dmh)


def adamw(gs, w, m, v, name):
    S, R, C = gs.shape
    cap = max(8, (1 << 20) // (S * C))
    tr = R
    if R > cap:
        tr = max(t for t in range(8, cap + 1, 8) if R % t == 0)

    def body(g_ref, w_ref, m_ref, v_ref, go_ref, d_ref, mo_ref, vo_ref):
        g = g_ref[0].astype(F32)
        for k in range(1, S):
            g = g + g_ref[k].astype(F32)
        mn = ADAM_B1 * m_ref[...] + (1.0 - ADAM_B1) * g
        vn = ADAM_B2 * v_ref[...] + (1.0 - ADAM_B2) * jnp.square(g)
        m_hat = mn / (1.0 - ADAM_B1 ** ADAM_STEP)
        v_hat = vn / (1.0 - ADAM_B2 ** ADAM_STEP)
        go_ref[...] = g
        d_ref[...] = -ADAM_LR * (m_hat / (jnp.sqrt(v_hat) + ADAM_EPS) + ADAM_WD * w_ref[...])
        mo_ref[...] = mn
        vo_ref[...] = vn

    blk = pl.BlockSpec((tr, C), lambda i: (i, 0))
    return pl.pallas_call(
        body, name=name, grid=(R // tr,),
        in_specs=[pl.BlockSpec((S, tr, C), lambda i: (0, i, 0)), blk, blk, blk],
        out_specs=[blk] * 4, out_shape=[_sds((R, C), F32)] * 4, compiler_params=_cp(),
    )(gs, w, m, v)


def _gather_flat(parts, dtype, name):
    flat = jnp.concatenate([p.astype(dtype).reshape(-1) for p in parts])
    n = flat.shape[0]
    pad = (-n) % LANE
    if pad:
        flat = jnp.concatenate([flat, jnp.zeros((pad,), dtype)])
    got = all_gather([flat.reshape(-1, LANE)], name)[0].reshape(N_DEV, -1)
    out, off = [], 0
    for p in parts:
        out.append(got[:, off:off + p.size].reshape((N_DEV,) + p.shape))
        off += p.size
    return out


def _cols_full(g):
    return g.transpose(1, 0, 2).reshape(g.shape[1], -1)


def _cols_split(full):
    K = full.shape[0]
    return full.reshape(K, N_DEV, -1).transpose(1, 0, 2)


def kernel(x, c, ctx, c_ctx, w_mod, b_mod, norm_g, ffn_wg, ffn_wu, ffn_wd, ab_w_in, ab_q_norm, ab_k_norm, pool_w, pool_scale, ab_w_out, gdn_w_in, gdn_conv_w, gdn_a_log, gdn_dt_bias, gdn_o_norm, gdn_w_out, loss_target, m_c_ctx, m_w_mod, m_b_mod, m_norm_g, m_ffn_wg, m_ffn_wu, m_ffn_wd, m_ab_w_in, m_ab_q_norm, m_ab_k_norm, m_pool_w, m_pool_scale, m_ab_w_out, m_gdn_w_in, m_gdn_conv_w, m_gdn_a_log, m_gdn_dt_bias, m_gdn_o_norm, m_gdn_w_out, v_c_ctx, v_w_mod, v_b_mod, v_norm_g, v_ffn_wg, v_ffn_wu, v_ffn_wd, v_ab_w_in, v_ab_q_norm, v_ab_k_norm, v_pool_w, v_pool_scale, v_ab_w_out, v_gdn_w_in, v_gdn_conv_w, v_gdn_a_log, v_gdn_dt_bias, v_gdn_o_norm, v_gdn_w_out):
    Bl, N, D = x.shape
    M = ctx.shape[1]
    F = ffn_wd.shape[2] * N_DEV
    dm = Dims(Bl, N, M, D, F)
    TM, Tx, Th, T, G = dm.TM, dm.Tx, dm.Th, dm.T, dm.G
    HD = C_HEADS * HEAD
    me = 4 * lax.axis_index("x") + 2 * lax.axis_index("y") + lax.axis_index("c")
    nb = N_DEV * Bl
    R = -(-(nb + 1) // 8) * 8
    nl = w_mod.shape[2]
    n_gdn = gdn_w_in.shape[2] * N_DEV
    n_gdn_pad = -(-n_gdn // LANE) * LANE

    big = [w.astype(BF16) for w in (ffn_wg, ffn_wu, ffn_wd, ab_w_in, ab_w_out, gdn_w_in, gdn_w_out)]
    g_wg, g_wu, g_wd, g_abin, g_about, g_gin, g_gout = all_gather(big, "gather_weights")
    g_c, g_ng, g_cw = _gather_flat([c, norm_g, gdn_conv_w], F32, "gather_small")
    W_ABIN = _cols_full(g_abin[:, 0])
    W_ABOUT = g_about[:, 0].reshape(-1, D)
    W_GIN = jnp.pad(_cols_full(g_gin[:, 0]), ((0, 0), (0, n_gdn_pad - n_gdn)))
    W_GOUT = g_gout[:, 0].reshape(-1, D)
    gains = g_ng.transpose(1, 2, 0, 3).reshape(2, 3, 1, D)
    conv_w = g_cw[:, 0].transpose(1, 0, 2).reshape(3, -1)

    c_all = g_c.reshape(nb, D)
    c_ext = jnp.concatenate([c_all, c_ctx[None], jnp.zeros((R - nb - 1, D), F32)], 0)
    b_loc = lax.dynamic_slice_in_dim(b_mod, me * nl, nl, axis=1).reshape(2, 1, nl)
    mod_loc = adaln_fwd(c_ext, w_mod, b_loc, "adaln_fwd")
    (g_mod,) = _gather_flat([mod_loc], F32, "gather_mod")
    mod_full = g_mod.transpose(1, 2, 0, 3).reshape(2, R, 9 * D)
    MOD = []
    for l in range(2):
        mine = lax.dynamic_slice_in_dim(mod_full[l], me * Bl, Bl, axis=0)
        MOD.append(jnp.concatenate([mine, mod_full[l, nb:nb + 1]], 0).reshape(G, 9, D))

    dm5, dmr = dm.with_tile(512), dm.with_tile(1024)
    tabs = _rope_tables(dmr)
    qkg = jnp.stack([ab_q_norm, ab_k_norm])
    pw = pool_w[0].astype(BF16)
    pwT = pool_w[0].transpose(0, 2, 1).astype(BF16)
    par = jnp.stack([jnp.pad(jnp.pad(p[0], ((0, 0), (0, 8))).reshape(-1), (0, LANE - 32))
                     for p in (gdn_a_log, gdn_dt_bias)])

    X0 = jnp.concatenate([x.reshape(Tx, D), ctx.reshape(Th, D)], 0)
    nt, ntx = dm.nt, dm.ntx
    def ffn(X, l, s, s0, all_rows, tag):
        return ffn_fwd(X, MOD[l], gains[l, 2 * s], g_wg, g_wu, g_wd, l, s, s0, dm, all_rows, "ffn_fwd_" + tag)

    X1, Y1 = ffn(X0, 0, 0, 0, True, "00")
    P0, XN0 = modmm(X1, MOD[0], gains[0, 1], W_ABIN, 3, dm5, "ab_in_proj")
    QKV = qkv_prep(P0, qkg, tabs, dmr, "qkv_prep")
    ATT = jnp.concatenate([attn_fwd(QKV, dm, True, "attn_fwd_x"), attn_fwd(QKV, dm, False, "attn_fwd_h")], 0)
    POOL = jnp.concatenate([pool_fwd(P0, pw, pool_scale, dm, True, "pool_fwd_x"),
                            pool_fwd(P0, pw, pool_scale, dm, False, "pool_fwd_h")], 0)
    na = A_HEADS * HEAD
    X2, YM0 = proj_res([ATT, POOL], [W_ABOUT[:na], W_ABOUT[na:]], X1, MOD[0], dm5, dm5.nt, "ab_out_proj")
    X3, Y3 = ffn(X2, 0, 1, 6, True, "01")
    X4, Y4 = ffn(X3, 1, 0, 0, True, "10")
    P1, XN1 = modmm(X4, MOD[1], gains[1, 1], W_GIN, 3, dm5, "gdn_in_proj")
    QKVg = jnp.concatenate([gdn_prep(P1, conv_w, dm, True, "gdn_prep_x"), gdn_prep(P1, conv_w, dm, False, "gdn_prep_h")], 0)
    U, W, KT, QD, QK, GT, WF, AI = gdn_chunk_pre(QKVg, P1, par, dm, "gdn_chunk_pre")
    O, SS = gdn_scan_fwd(U, W, KT, QD, QK, GT, dm, "gdn_scan_fwd")
    FIN = gdn_finish(O, P1, gdn_o_norm, dmr, "gdn_finish")
    X5, YM1 = proj_res([FIN], [W_GOUT], X4, MOD[1], dm5, dm5.ntx, "gdn_out_proj")
    X6, Y6 = ffn(X5, 1, 1, 6, False, "11")
    lvec, dX6 = loss_head(X6, loss_target.reshape(Tx, D), dmr, "loss_head")
    loss = lax.psum(lvec[0, 0], AXES)

    zrow = lambda a: jnp.concatenate([a, jnp.zeros((G - a.shape[0],) + a.shape[1:], F32)], 0) if a.shape[0] < G else a

    def ffn_back(Xin, dXo, Y, l, s, s0, all_rows, tag):
        dXi, XNb, DOb, Hb, DGb, DUb, dmod, dgain = ffn_bwd(
            Xin, dXo, Y, MOD[l], gains[l, 2 * s], g_wg, g_wu, g_wd, l, s, s0, dm, all_rows, "ffn_bwd_" + tag)
        dwg, dwu, dwd = ffn_dw(XNb, DOb, Hb, DGb, DUb, dm, "ffn_dw_" + tag)
        return dXi, zrow(dmod), dgain, dwg, dwu, dwd

    dX5, dmod_12, dgain_12, dwg11, dwu11, dwd11 = ffn_back(X5, dX6, Y6, 1, 1, 6, False, "11")
    DY1, dFIN, dgate_1 = proj_res_bwd(dX5, YM1, MOD[1], [W_GOUT], dm5, dm5.ntx, "gdn_out_proj_bwd")
    d_gout = atb(FIN, DY1, Tx, dm, "gdn_dwout")
    dOsum, dZ, d_onorm = gdn_finish_bwd(O, P1, gdn_o_norm, dFIN, dmr, "gdn_finish_bwd")
    dO_all = jnp.concatenate([dOsum, jnp.zeros((Th, HD), F32)], 0)
    dU, dW, dKT, dQD, dQK, dGT = gdn_scan_bwd(dO_all, SS, U, W, KT, QD, QK, GT, dm, "gdn_scan_bwd")
    dQKVg, dAB, dPAR = gdn_chunk_pre_bwd(QKVg, P1, par, U, WF, AI, dU, dW, dKT, dQD, dQK, dGT, dm, "gdn_chunk_pre_bwd")
    dPx, dcw_x = gdn_prep_bwd(P1, dQKVg, conv_w, dm, True, "gdn_prep_bwd_x")
    dPh, dcw_h = gdn_prep_bwd(P1, dQKVg, conv_w, dm, False, "gdn_prep_bwd_h")
    d_conv = dcw_x + dcw_h
    dP1 = jnp.concatenate([jnp.concatenate([dPx, dPh], 0), jnp.concatenate([dZ, jnp.zeros((Th, HD), F32)], 0),
                           dAB[0] + dAB[1]], axis=1).astype(BF16)
    d_gin = atb(XN1, dP1, T, dm, "gdn_dwin")[:, :n_gdn]
    dX5_full = jnp.concatenate([dX5, jnp.zeros((Th, D), F32)], 0)
    dX4, dmod_11, dgain_11 = mixin_bwd(dP1, W_GIN, X4, dX5_full, MOD[1], gains[1, 1], 3, dm, "gdn_in_proj_bwd")
    dX3, dmod_10, dgain_10, dwg10, dwu10, dwd10 = ffn_back(X3, dX4, Y4, 1, 0, 0, True, "10")
    dMOD1 = jnp.concatenate([dmod_10, dmod_11, zrow(dgate_1), dmod_12], 1).reshape(G, 9 * D)

    dX2, dmod_02, dgain_02, dwg01, dwu01, dwd01 = ffn_back(X2, dX3, Y3, 0, 1, 6, True, "01")
    DY0, dATT, dPOOL, dgate_0 = proj_res_bwd(dX2, YM0, MOD[0], [W_ABOUT[:na], W_ABOUT[na:]], dm5, dm5.nt, "ab_out_proj_bwd")
    d_about = jnp.concatenate([atb(ATT, DY0, T, dm, "ab_dwout_a"), atb(POOL, DY0, T, dm, "ab_dwout_p")], 0)
    dUx, dpw_x, dps_x = pool_bwd(P0, dPOOL[:Tx], pw, pwT, pool_scale, dm, True, "pool_bwd_x")
    dUh, dpw_h, dps_h = pool_bwd(P0, dPOOL[Tx:], pw, pwT, pool_scale, dm, False, "pool_bwd_h")
    dQh, dKh0, dVh0 = attn_bwd(QKV, dATT[Tx:], dm, False, None, "attn_bwd_h")
    dQx, dKh, dVh, dKx, dVx = attn_bwd(QKV, dATT[:Tx], dm, True, (dKh0, dVh0), "attn_bwd_x")
    dQKV = jnp.concatenate([jnp.concatenate([dQx, dQh], 0), jnp.concatenate([dKx, dKh], 0), jnp.concatenate([dVx, dVh], 0)], 1)
    dPqkv, d_qkg = qkv_prep_bwd(P0, dQKV, qkg, tabs, dmr, "qkv_prep_bwd")
    dP0 = jnp.concatenate([dPqkv, jnp.concatenate([dUx, dUh], 0)], 1).astype(BF16)
    d_abin = atb(XN0, dP0, T, dm, "ab_dwin")
    dX1, dmod_01, dgain_01 = mixin_bwd(dP0, W_ABIN, X1, dX2, MOD[0], gains[0, 1], 3, dm5, "ab_in_proj_bwd")
    dX0, dmod_00, dgain_00, dwg00, dwu00, dwd00 = ffn_back(X0, dX1, Y1, 0, 0, 0, True, "00")
    dMOD0 = jnp.concatenate([dmod_00, dmod_01, dgate_0, dmod_02], 1).reshape(G, 9 * D)
    grad_x = dX0[:Tx].reshape(Bl, N, D)

    d_ng = jnp.concatenate([dgain_00, dgain_01, dgain_02, dgain_10, dgain_11, dgain_12], 0)
    nf = ffn_wg.shape[3]
    parts = [jnp.concatenate([dwg00, dwg01, dwg10, dwg11], 1), jnp.concatenate([dwu00, dwu01, dwu10, dwu11], 1),
             jnp.concatenate([dwd00, dwd01, dwd10, dwd11], 1),
             _cols_split(d_abin), d_about.reshape(N_DEV, -1, D), _cols_split(d_gin), d_gout.reshape(N_DEV, -1, D),
             _cols_split(d_conv), _cols_split(d_ng)]
    gs_wg, gs_wu, gs_wd, gs_abin, gs_about, gs_gin, gs_gout, gs_conv, gs_ng = scatter_blocks(parts, "scatter_grads")

    d_alog = dPAR[0, :32].reshape(2, 16)[:, :8].reshape(1, 16)
    d_dtb = dPAR[1, :32].reshape(2, 16)[:, :8].reshape(1, 16)
    small = [d_qkg[0], d_qkg[1], (dpw_x + dpw_h).reshape(-1, HEAD), dps_x + dps_h, d_alog, d_dtb, d_onorm,
             jnp.stack([dMOD0, dMOD1])]
    gs_qn, gs_kn, gs_pw, gs_ps, gs_alog, gs_dtb, gs_on, g_dm = _gather_flat(small, F32, "gather_small_grads")
    dmx = g_dm[:, :, :Bl].transpose(1, 0, 2, 3).reshape(2, nb, 9 * D)
    dmx = jnp.concatenate([dmx, jnp.zeros((2, R - nb, 9 * D), F32)], 1)
    dmh = g_dm[:, :, Bl].transpose(1, 0, 2)
    cols_of_me = lambda a: lax.dynamic_slice_in_dim(a, me * nl, nl, axis=2)
    d_wmod, dcc = adaln_bwd(c_ext, c_ctx[None], w_mod, cols_of_me(dmx), cols_of_me(dmh), nb, "adaln_bwd")
    d_bmod = bmod_grad(dmx, dmh, "bmod_grad")
    (gs_cc,) = _gather_flat([dcc], F32, "gather_cctx_grad")

    def upd(gs, w, m, v, shape2, name):
        outs = adamw(gs.reshape((gs.shape[0],) + shape2), w.reshape(shape2), m.reshape(shape2), v.reshape(shape2), "adamw_" + name)
        return [o.reshape(w.shape) for o in outs]

    res = [
        upd(gs_cc, c_ctx, m_c_ctx, v_c_ctx, (1, D), "c_ctx"),
        upd(d_wmod[None], w_mod, m_w_mod, v_w_mod, (2 * D, nl), "w_mod"),
        upd(d_bmod[None], b_mod, m_b_mod, v_b_mod, (2, 9 * D), "b_mod"),
        upd(gs_ng, norm_g, m_norm_g, v_norm_g, (6, D // N_DEV), "norm_g"),
        upd(gs_wg, ffn_wg, m_ffn_wg, v_ffn_wg, (4 * D, nf), "ffn_wg"),
        upd(gs_wu, ffn_wu, m_ffn_wu, v_ffn_wu, (4 * D, nf), "ffn_wu"),
        upd(gs_wd, ffn_wd, m_ffn_wd, v_ffn_wd, (4 * nf, D), "ffn_wd"),
        upd(gs_abin, ab_w_in, m_ab_w_in, v_ab_w_in, (D, ab_w_in.shape[2]), "ab_w_in"),
        upd(gs_qn, ab_q_norm, m_ab_q_norm, v_ab_q_norm, (1, HEAD), "ab_q_norm"),
        upd(gs_kn, ab_k_norm, m_ab_k_norm, v_ab_k_norm, (1, HEAD), "ab_k_norm"),
        upd(gs_pw, pool_w, m_pool_w, v_pool_w, (len(POOL_WINDOWS) * HEAD, HEAD), "pool_w"),
        upd(gs_ps, pool_scale, m_pool_scale, v_pool_scale, (1, len(POOL_WINDOWS) * HEAD), "pool_scale"),
        upd(gs_about, ab_w_out, m_ab_w_out, v_ab_w_out, (ab_w_out.shape[1], D), "ab_w_out"),
        upd(gs_gin, gdn_w_in, m_gdn_w_in, v_gdn_w_in, (D, gdn_w_in.shape[2]), "gdn_w_in"),
        upd(gs_conv, gdn_conv_w, m_gdn_conv_w, v_gdn_conv_w, (3, gdn_conv_w.shape[2]), "gdn_conv_w"),
        upd(gs_alog, gdn_a_log, m_gdn_a_log, v_gdn_a_log, (1, 16), "gdn_a_log"),
        upd(gs_dtb, gdn_dt_bias, m_gdn_dt_bias, v_gdn_dt_bias, (1, 16), "gdn_dt_bias"),
        upd(gs_on, gdn_o_norm, m_gdn_o_norm, v_gdn_o_norm, (1, HEAD), "gdn_o_norm"),
        upd(gs_gout, gdn_w_out, m_gdn_w_out, v_gdn_w_out, (gdn_w_out.shape[1], D), "gdn_w_out"),
    ]
    return (loss, grad_x, *[r[0] for r in res], *[r[1] for r in res], *[r[2] for r in res], *[r[3] for r in res])
```

```python
import functools
from typing import NamedTuple

import jax
import jax.numpy as jnp
from jax import lax
from jax.experimental import pallas as pl
from jax.experimental.pallas import tpu as pltpu

F32, BF16 = jnp.float32, jnp.bfloat16
EPS = 1e-6
HEAD = 128
CHUNK = 64
GRID_W = 64
ROPE_THETA = 10000.0
POOL_WINDOWS = (2, 4, 8, 16)
A_HEADS, A_KV = 4, 2
C_HEADS = 8
N_DEV = 8
AXES = ("x", "y", "c")
ADAM_LR, ADAM_B1, ADAM_B2, ADAM_EPS, ADAM_WD, ADAM_STEP = 0.001, 0.9, 0.999, 1e-08, 0.01, 10
LANE = 128
VMEM_LIMIT = 56 * 1024 * 1024
NEG = -1e30


def _cp():
    return pltpu.CompilerParams(vmem_limit_bytes=VMEM_LIMIT)


def _sds(shape, dtype):
    return jax.ShapeDtypeStruct(tuple(shape), dtype)


def _dot(a, b):
    return jnp.dot(a.astype(BF16), b.astype(BF16), preferred_element_type=F32)


def _dot_nt(a, b):
    return lax.dot_general(a.astype(BF16), b.astype(BF16), (((1,), (1,)), ((), ())), preferred_element_type=F32)


def _dot_tn(a, b):
    return lax.dot_general(a.astype(BF16), b.astype(BF16), (((0,), (0,)), ((), ())), preferred_element_type=F32)


def _dot3(a, b, dims):
    ah, bh = a.astype(BF16), b.astype(BF16)
    al, bl = (a - ah.astype(F32)).astype(BF16), (b - bh.astype(F32)).astype(BF16)
    f = lambda x, y: lax.dot_general(x, y, (dims, ((), ())), preferred_element_type=F32)
    return f(ah, bh) + (f(ah, bl) + f(al, bh))


def _hdot(a, b):
    return _dot3(a, b, ((1,), (0,)))


def _hdot_nt(a, b):
    return _dot3(a, b, ((1,), (1,)))


def _hdot_tn(a, b):
    return _dot3(a, b, ((0,), (0,)))


def _pick(n, cap):
    if n <= cap:
        return n
    best = None
    for t in range(LANE, cap + 1, LANE):
        if n % t == 0:
            best = t
    assert best is not None, (n, cap)
    return best


class Dims(NamedTuple):
    Bl: int
    N: int
    M: int
    D: int
    F: int
    tm: int = 0

    @property
    def TM(self):
        return self.tm if self.tm else min(256, self.M)

    def with_tile(self, cap):
        return self._replace(tm=max(t for t in (1024, 512, 256, 128) if t <= cap and self.N % t == 0 and self.Th % t == 0))

    @property
    def Tx(self):
        return self.Bl * self.N

    @property
    def Th(self):
        return self.Bl * self.M

    @property
    def T(self):
        return self.Tx + self.Th

    @property
    def ntx(self):
        return self.Tx // self.TM

    @property
    def nt(self):
        return self.T // self.TM

    @property
    def tps(self):
        return self.N // self.TM

    @property
    def G(self):
        return self.Bl + 1


def _grp(i, dm, tm=None):
    tm = dm.TM if tm is None else tm
    return jnp.where(i < dm.Tx // tm, i // (dm.N // tm), dm.Bl)


def _first_of_group(i, dm, tm=None):
    tm = dm.TM if tm is None else tm
    return jnp.where(i < dm.Tx // tm, i % (dm.N // tm) == 0, i == dm.Tx // tm)


def _contraction_tile(rows):
    return max(t for t in (1024, 512, 256, 128) if rows % t == 0)


def _ffn_tile(dm):
    return max(t for t in (512, 256, 128) if dm.N % t == 0 and dm.Th % t == 0)


def _acc(ref, idx, val, first):
    @pl.when(first)
    def _():
        ref[idx] = val

    @pl.when(jnp.logical_not(first))
    def _():
        ref[idx] += val


def _modulate(x, gain, shift, scale):
    y = x * lax.rsqrt(jnp.mean(x * x, axis=-1, keepdims=True) + EPS)
    return (y * gain) * (1.0 + scale) + shift


def _silu(x):
    return x * jax.nn.sigmoid(x)


@functools.partial(jax.custom_vjp, nondiff_argnums=(1,))
def _shift_rows(a, k):
    n = a.shape[0]
    if k == 0:
        return a
    r = lax.broadcasted_iota(jnp.int32, a.shape, 0)
    rolled = pltpu.roll(a, (-k) % n, 0)
    ok = (r + k >= 0) & (r + k < n)
    return jnp.where(ok, rolled, 0.0)


def _shift_rows_fwd(a, k):
    return _shift_rows(a, k), None


def _shift_rows_bwd(k, _, d):
    return (_shift_rows(d, -k),)


_shift_rows.defvjp(_shift_rows_fwd, _shift_rows_bwd)


@functools.partial(jax.custom_vjp, nondiff_argnums=(1,))
def _roll_lanes(a, s):
    return pltpu.roll(a, s % LANE, 1)


def _roll_lanes_fwd(a, s):
    return _roll_lanes(a, s), None


def _roll_lanes_bwd(s, _, d):
    return (_roll_lanes(d, -s),)


_roll_lanes.defvjp(_roll_lanes_fwd, _roll_lanes_bwd)


def _rope(t, cs, sneg, spos):
    return t * cs + _roll_lanes(t, 96) * sneg + _roll_lanes(t, 32) * spos


def _rope_tables(dm):
    rows = dm.N // GRID_W
    row = jnp.repeat(jnp.arange(rows), GRID_W).astype(F32)
    col = jnp.tile(jnp.arange(GRID_W), rows).astype(F32)
    half = HEAD // 2
    inv_freq = jnp.power(ROPE_THETA, -jnp.arange(0, half, 2, dtype=F32) / half)
    ar, ac = row[:, None] * inv_freq, col[:, None] * inv_freq
    cs = jnp.concatenate([jnp.cos(ar), jnp.cos(ar), jnp.cos(ac), jnp.cos(ac)], axis=1)
    z = jnp.zeros_like(ar)
    sneg = jnp.concatenate([-jnp.sin(ar), z, -jnp.sin(ac), z], axis=1)
    spos = jnp.concatenate([z, jnp.sin(ar), z, jnp.sin(ac)], axis=1)
    pad1 = jnp.ones((dm.TM, HEAD), F32)
    pad0 = jnp.zeros((dm.TM, HEAD), F32)
    return (jnp.concatenate([cs, pad1], 0), jnp.concatenate([sneg, pad0], 0), jnp.concatenate([spos, pad0], 0))


def all_gather(xs, name):
    n = len(xs)

    def body(*refs):
        x_refs, out_refs = refs[:n], refs[n:2 * n]
        send_sems, recv_sems, local_sems = refs[2 * n:]
        x, y, c = lax.axis_index("x"), lax.axis_index("y"), lax.axis_index("c")
        me, sibling = (x, y, c), (x, y, 1 - c)
        chips = [(1 - x, y), (x, 1 - y), (1 - x, 1 - y)]

        def slot(a, px, py, pc):
            return out_refs[a].at[4 * px + 2 * py + pc]

        def copy(a, k, block, to, src=None):
            return pltpu.make_async_remote_copy(
                src_ref=slot(a, *block) if src is None else src, dst_ref=slot(a, *block),
                send_sem=send_sems.at[7 * a + k], recv_sem=recv_sems.at[7 * a + k],
                device_id=to, device_id_type=pl.DeviceIdType.MESH)

        mine = [pltpu.make_async_copy(x_refs[a], slot(a, *me), local_sems.at[a]) for a in range(n)]
        for cp in mine:
            cp.start()
        first = []
        for a in range(n):
            first.append(copy(a, 0, me, sibling, src=x_refs[a]))
            first += [copy(a, 1 + j, me, (*chip, c), src=x_refs[a]) for j, chip in enumerate(chips)]
        for cp in first:
            cp.start()
        passed = []
        for j, chip in enumerate(chips):
            for a in range(n):
                copy(a, 1 + j, (*chip, c), me).wait_recv()
                cp = copy(a, 4 + j, (*chip, c), sibling)
                cp.start()
                passed.append(cp)
        for a in range(n):
            copy(a, 0, sibling, me).wait_recv()
            for j, chip in enumerate(chips):
                copy(a, 4 + j, (*chip, 1 - c), me).wait_recv()
        for cp in first + passed:
            cp.wait_send()
        for cp in mine:
            cp.wait()

    anyspec = pl.BlockSpec(memory_space=pl.ANY)
    return pl.pallas_call(
        body, name=name, out_shape=[_sds((N_DEV,) + a.shape, a.dtype) for a in xs],
        in_specs=[anyspec] * n, out_specs=[anyspec] * n,
        scratch_shapes=[pltpu.SemaphoreType.DMA((7 * n,)), pltpu.SemaphoreType.DMA((7 * n,)),
                        pltpu.SemaphoreType.DMA((n,))],
    )(*xs)


def scatter_blocks(xs, name):
    n = len(xs)
    flips = [(0, 0, 1), (0, 1, 0), (0, 1, 1), (1, 0, 0), (1, 0, 1), (1, 1, 0), (1, 1, 1)]

    def body(*refs):
        x_refs, out_refs = refs[:n], refs[n:2 * n]
        send_sems, recv_sems, local_sems = refs[2 * n:]
        x, y, c = lax.axis_index("x"), lax.axis_index("y"), lax.axis_index("c")
        me = 4 * x + 2 * y + c

        def peer(f):
            return tuple(1 - v if d else v for v, d in zip((x, y, c), f))

        def lin(p):
            return 4 * p[0] + 2 * p[1] + p[2]

        mine = [pltpu.make_async_copy(x_refs[a].at[me], out_refs[a].at[me], local_sems.at[a]) for a in range(n)]
        for cp in mine:
            cp.start()
        copies = []
        for k, f in enumerate(flips):
            p = peer(f)
            for a in range(n):
                copies.append(pltpu.make_async_remote_copy(
                    src_ref=x_refs[a].at[lin(p)], dst_ref=out_refs[a].at[me],
                    send_sem=send_sems.at[7 * a + k], recv_sem=recv_sems.at[7 * a + k],
                    device_id=p, device_id_type=pl.DeviceIdType.MESH))
        for cp in copies:
            cp.start()
        for cp in copies:
            cp.wait_send()
            cp.wait_recv()
        for cp in mine:
            cp.wait()

    anyspec = pl.BlockSpec(memory_space=pl.ANY)
    return pl.pallas_call(
        body, name=name, out_shape=[_sds(a.shape, a.dtype) for a in xs],
        in_specs=[anyspec] * n, out_specs=[anyspec] * n,
        scratch_shapes=[pltpu.SemaphoreType.DMA((7 * n,)), pltpu.SemaphoreType.DMA((7 * n,)),
                        pltpu.SemaphoreType.DMA((n,))],
    )(*xs)


def _mod_spec(dm, nidx, tm=None):
    if nidx == 1:
        return pl.BlockSpec((1, 9, dm.D), lambda i: (_grp(i, dm, tm), 0, 0))
    return pl.BlockSpec((1, 9, dm.D), lambda i, k: (_grp(i, dm, tm), 0, 0))


def _wspec(shape5, l, s, ks):
    return pl.BlockSpec((ks, 1, 1) + tuple(shape5[3:]), lambda i, k: (k, l, s, 0, 0))


FFN_FWD_SHARDS = 4
FFN_BWD_SHARDS = 2


def ffn_fwd(X, MOD, gain, gwg, gwu, gwd, l, s, s0, dm, all_rows, name):
    D = dm.D
    tm = _ffn_tile(dm)
    rows = dm.T if all_rows else dm.Tx
    ks = FFN_FWD_SHARDS
    nk = N_DEV // ks

    def body(x_ref, m_ref, g_ref, wg_ref, wu_ref, wd_ref, xo_ref, y_ref, xn_s, acc_s):
        k = pl.program_id(1)

        @pl.when(k == 0)
        def _():
            m = m_ref[0]
            xn = _modulate(x_ref[...], g_ref[...], m[s0:s0 + 1], m[s0 + 1:s0 + 2])
            xn_s[...] = xn.astype(BF16)
            acc_s[...] = jnp.zeros_like(acc_s)

        xn = xn_s[...]
        y = None
        for j in range(0, ks, 2):
            wg2 = jnp.concatenate([wg_ref[j, 0, 0], wg_ref[j + 1, 0, 0]], axis=1)
            wu2 = jnp.concatenate([wu_ref[j, 0, 0], wu_ref[j + 1, 0, 0]], axis=1)
            wd2 = jnp.concatenate([wd_ref[j, 0, 0], wd_ref[j + 1, 0, 0]], axis=0)
            g = jnp.dot(xn, wg2, preferred_element_type=F32)
            u = jnp.dot(xn, wu2, preferred_element_type=F32)
            yj = jnp.dot((_silu(g) * u).astype(BF16), wd2, preferred_element_type=F32)
            y = yj if y is None else y + yj
        acc_s[...] += y

        @pl.when(k == nk - 1)
        def _():
            m = m_ref[0]
            y = acc_s[...]
            y_ref[...] = y
            xo_ref[...] = x_ref[...] + (0.5 * m[s0 + 2:s0 + 3]) * y

    row = pl.BlockSpec((tm, D), lambda i, k: (i, 0))
    return pl.pallas_call(
        body, name=name, grid=(rows // tm, nk),
        in_specs=[row, _mod_spec(dm, 2, tm), pl.BlockSpec((1, D), lambda i, k: (0, 0)),
                  _wspec(gwg.shape, l, s, ks), _wspec(gwu.shape, l, s, ks), _wspec(gwd.shape, l, s, ks)],
        out_specs=[row, row],
        out_shape=[_sds((rows, D), F32), _sds((rows, D), F32)],
        scratch_shapes=[pltpu.VMEM((tm, D), BF16), pltpu.VMEM((tm, D), F32)],
        compiler_params=_cp(),
    )(X, MOD, gain, gwg, gwu, gwd)


def ffn_bwd(X, dXo, Y, MOD, gain, gwg, gwu, gwd, l, s, s0, dm, all_rows, name):
    D = dm.D
    tm = _ffn_tile(dm)
    rows = dm.T if all_rows else dm.Tx
    ks = FFN_BWD_SHARDS
    nk = N_DEV // ks
    nf = gwg.shape[4]
    ngr = dm.G if all_rows else dm.Bl

    def body(x_ref, dxo_ref, y_ref, m_ref, g_ref, wg_ref, wu_ref, wd_ref,
             dxi_ref, xn_ref, do_ref, h_ref, dg_ref, du_ref, dm_ref, dgain_ref, xn_s, do_s, dxn_s):
        i, k = pl.program_id(0), pl.program_id(1)

        @pl.when(k == 0)
        def _():
            m = m_ref[0]
            xn = _modulate(x_ref[...], g_ref[...], m[s0:s0 + 1], m[s0 + 1:s0 + 2])
            xn_s[...] = xn.astype(BF16)
            do_s[...] = ((0.5 * m[s0 + 2:s0 + 3]) * dxo_ref[...]).astype(BF16)
            dxn_s[...] = jnp.zeros_like(dxn_s)

        xn, do = xn_s[...], do_s[...]
        dxn = None
        for p in range(ks // 2):
            j = 2 * p
            wg = jnp.concatenate([wg_ref[j, 0, 0], wg_ref[j + 1, 0, 0]], axis=1)
            wu = jnp.concatenate([wu_ref[j, 0, 0], wu_ref[j + 1, 0, 0]], axis=1)
            wd = jnp.concatenate([wd_ref[j, 0, 0], wd_ref[j + 1, 0, 0]], axis=0)
            g = jnp.dot(xn, wg, preferred_element_type=F32)
            u = jnp.dot(xn, wu, preferred_element_type=F32)
            sg = jax.nn.sigmoid(g)
            si = g * sg
            dh = _dot_nt(do, wd)
            dg = (dh * u * (sg * (1.0 + g * (1.0 - sg)))).astype(BF16)
            du = (dh * si).astype(BF16)
            dj = _dot_nt(dg, wg) + _dot_nt(du, wu)
            dxn = dj if dxn is None else dxn + dj
            h_ref[p] = (si * u).astype(BF16)
            dg_ref[p] = dg
            du_ref[p] = du
        dxn_s[...] += dxn

        @pl.when(k == nk - 1)
        def _():
            m = m_ref[0]
            _, vjp = jax.vjp(_modulate, x_ref[...], g_ref[...], m[s0:s0 + 1], m[s0 + 1:s0 + 2])
            dx, dgain, dshift, dscale = vjp(dxn_s[...])
            dxo = dxo_ref[...]
            dxi_ref[...] = dxo + dx
            xn_ref[...] = xn_s[...]
            do_ref[...] = do_s[...]
            dgate = jnp.sum(0.5 * dxo * y_ref[...], axis=0, keepdims=True)
            first = _first_of_group(i, dm, tm)
            _acc(dm_ref, (0, pl.ds(0, 1), slice(None)), dshift, first)
            _acc(dm_ref, (0, pl.ds(1, 1), slice(None)), dscale, first)
            _acc(dm_ref, (0, pl.ds(2, 1), slice(None)), dgate, first)
            _acc(dgain_ref, (slice(None), slice(None)), dgain, i == 0)

    row = pl.BlockSpec((tm, D), lambda i, k: (i, 0))
    slab = pl.BlockSpec((ks // 2, tm, 2 * nf), lambda i, k: (k, i, 0))
    pairs = _sds((N_DEV // 2, rows, 2 * nf), BF16)
    return pl.pallas_call(
        body, name=name, grid=(rows // tm, nk),
        in_specs=[row, row, row, _mod_spec(dm, 2, tm), pl.BlockSpec((1, D), lambda i, k: (0, 0)),
                  _wspec(gwg.shape, l, s, ks), _wspec(gwu.shape, l, s, ks), _wspec(gwd.shape, l, s, ks)],
        out_specs=[row, row, row, slab, slab, slab,
                   pl.BlockSpec((1, 3, D), lambda i, k: (_grp(i, dm, tm), 0, 0)),
                   pl.BlockSpec((1, D), lambda i, k: (0, 0))],
        out_shape=[_sds((rows, D), F32), _sds((rows, D), BF16), _sds((rows, D), BF16),
                   pairs, pairs, pairs,
                   _sds((ngr, 3, D), F32), _sds((1, D), F32)],
        scratch_shapes=[pltpu.VMEM((tm, D), BF16), pltpu.VMEM((tm, D), BF16), pltpu.VMEM((tm, D), F32)],
        compiler_params=_cp(),
    )(X, dXo, Y, MOD, gain, gwg, gwu, gwd)


def ffn_dw(XN, DO, H, DG, DU, dm, name):
    rows, D = XN.shape
    nf = H.shape[2] // 2
    tt = _contraction_tile(rows)
    nT = rows // tt

    def body(xn_ref, do_ref, h_ref, dg_ref, du_ref, dwg_ref, dwu_ref, dwd_ref, ag_s, au_s, ad_s):
        t = pl.program_id(1)

        @pl.when(t == 0)
        def _():
            ag_s[...] = jnp.zeros_like(ag_s)
            au_s[...] = jnp.zeros_like(au_s)
            ad_s[...] = jnp.zeros_like(ad_s)

        xn = xn_ref[...]
        ag_s[...] += _dot_tn(xn, dg_ref[0])
        au_s[...] += _dot_tn(xn, du_ref[0])
        ad_s[...] += _dot_tn(h_ref[0], do_ref[...])

        @pl.when(t == nT - 1)
        def _():
            for j in range(2):
                dwg_ref[j] = ag_s[:, pl.ds(j * nf, nf)].astype(BF16)
                dwu_ref[j] = au_s[:, pl.ds(j * nf, nf)].astype(BF16)
                dwd_ref[j] = ad_s[pl.ds(j * nf, nf), :].astype(BF16)

    row = pl.BlockSpec((tt, D), lambda k, t: (t, 0))
    slab = pl.BlockSpec((1, tt, 2 * nf), lambda k, t: (k, t, 0))
    return pl.pallas_call(
        body, name=name, grid=(N_DEV // 2, nT),
        in_specs=[row, row, slab, slab, slab],
        out_specs=[pl.BlockSpec((2, D, nf), lambda k, t: (k, 0, 0)), pl.BlockSpec((2, D, nf), lambda k, t: (k, 0, 0)),
                   pl.BlockSpec((2, nf, D), lambda k, t: (k, 0, 0))],
        out_shape=[_sds((N_DEV, D, nf), BF16), _sds((N_DEV, D, nf), BF16), _sds((N_DEV, nf, D), BF16)],
        scratch_shapes=[pltpu.VMEM((D, 2 * nf), F32), pltpu.VMEM((D, 2 * nf), F32), pltpu.VMEM((2 * nf, D), F32)],
        compiler_params=_cp(),
    )(XN, DO, H, DG, DU)


def atb(A, B, rows, dm, name):
    Ka, Nb = A.shape[1], B.shape[1]
    tk, tn = _pick(Ka, 1024), _pick(Nb, 1536)
    tt = _contraction_tile(rows)
    nT = rows // tt

    def body(a_ref, b_ref, o_ref, acc_s):
        t = pl.program_id(2)

        @pl.when(t == 0)
        def _():
            acc_s[...] = jnp.zeros_like(acc_s)

        acc_s[...] += _dot_tn(a_ref[...], b_ref[...])

        @pl.when(t == nT - 1)
        def _():
            o_ref[...] = acc_s[...].astype(BF16)

    return pl.pallas_call(
        body, name=name, grid=(Ka // tk, Nb // tn, nT),
        in_specs=[pl.BlockSpec((tt, tk), lambda i, j, t: (t, i)), pl.BlockSpec((tt, tn), lambda i, j, t: (t, j))],
        out_specs=pl.BlockSpec((tk, tn), lambda i, j, t: (i, j)),
        out_shape=_sds((Ka, Nb), BF16), scratch_shapes=[pltpu.VMEM((tk, tn), F32)], compiler_params=_cp(),
    )(A, B)


def modmm(X, MOD, gain, W, s0, dm, name):
    TM, D = dm.TM, dm.D
    Nc = W.shape[1]
    tn = _pick(Nc, 1536)
    nj = Nc // tn

    def body(x_ref, m_ref, g_ref, w_ref, p_ref, xn_ref):
        @pl.when(pl.program_id(1) == 0)
        def _():
            m = m_ref[0]
            xn_ref[...] = _modulate(x_ref[...], g_ref[...], m[s0:s0 + 1], m[s0 + 1:s0 + 2]).astype(BF16)

        p_ref[...] = jnp.dot(xn_ref[...], w_ref[...], preferred_element_type=F32)

    row = pl.BlockSpec((TM, D), lambda i, j: (i, 0))
    return pl.pallas_call(
        body, name=name, grid=(dm.nt, nj),
        in_specs=[row, _mod_spec(dm, 2), pl.BlockSpec((1, D), lambda i, j: (0, 0)),
                  pl.BlockSpec((D, tn), lambda i, j: (0, j))],
        out_specs=[pl.BlockSpec((TM, tn), lambda i, j: (i, j)), row],
        out_shape=[_sds((dm.T, Nc), F32), _sds((dm.T, D), BF16)],
        compiler_params=_cp(),
    )(X, MOD, gain, W)


def mixin_bwd(dP, W, X, dXres, MOD, gain, s0, dm, name):
    TM, D = dm.TM, dm.D
    K = dP.shape[1]

    def body(dp_ref, w_ref, x_ref, dr_ref, m_ref, g_ref, dx_ref, dm_ref, dgain_ref):
        i = pl.program_id(0)
        dxn = _dot_nt(dp_ref[...], w_ref[...])
        m = m_ref[0]
        _, vjp = jax.vjp(_modulate, x_ref[...], g_ref[...], m[s0:s0 + 1], m[s0 + 1:s0 + 2])
        dx, dgain, dshift, dscale = vjp(dxn)
        dx_ref[...] = dr_ref[...] + dx
        first = _first_of_group(i, dm)
        _acc(dm_ref, (0, pl.ds(0, 1), slice(None)), dshift, first)
        _acc(dm_ref, (0, pl.ds(1, 1), slice(None)), dscale, first)
        _acc(dgain_ref, (slice(None), slice(None)), dgain, i == 0)

    row = pl.BlockSpec((TM, D), lambda i: (i, 0))
    return pl.pallas_call(
        body, name=name, grid=(dm.nt,),
        in_specs=[pl.BlockSpec((TM, K), lambda i: (i, 0)), pl.BlockSpec((D, K), lambda i: (0, 0)), row, row,
                  _mod_spec(dm, 1), pl.BlockSpec((1, D), lambda i: (0, 0))],
        out_specs=[row, pl.BlockSpec((1, 2, D), lambda i: (_grp(i, dm), 0, 0)), pl.BlockSpec((1, D), lambda i: (0, 0))],
        out_shape=[_sds((dm.T, D), F32), _sds((dm.G, 2, D), F32), _sds((1, D), F32)],
        compiler_params=_cp(),
    )(dP, W, X, dXres, MOD, gain)


def proj_res(As, Ws, X, MOD, dm, ntiles, name):
    TM, D = dm.TM, dm.D
    n = len(As)
    rows = ntiles * TM

    def body(*refs):
        a_refs, w_refs = refs[:n], refs[n:2 * n]
        x_ref, m_ref, xo_ref, y_ref = refs[2 * n:]
        y = jnp.dot(a_refs[0][...], w_refs[0][...], preferred_element_type=F32)
        for a, w in zip(a_refs[1:], w_refs[1:]):
            y += jnp.dot(a[...], w[...], preferred_element_type=F32)
        y_ref[...] = y
        xo_ref[...] = x_ref[...] + m_ref[0][5:6] * y

    row = pl.BlockSpec((TM, D), lambda i: (i, 0))
    return pl.pallas_call(
        body, name=name, grid=(ntiles,),
        in_specs=[pl.BlockSpec((TM, a.shape[1]), lambda i: (i, 0)) for a in As]
        + [pl.BlockSpec(w.shape, lambda i: (0, 0)) for w in Ws] + [row, _mod_spec(dm, 1)],
        out_specs=[row, row], out_shape=[_sds((rows, D), F32), _sds((rows, D), F32)],
        compiler_params=_cp(),
    )(*As, *Ws, X, MOD)


def proj_res_bwd(dXo, Y, MOD, Ws, dm, ntiles, name):
    TM, D = dm.TM, dm.D
    n = len(Ws)
    rows = ntiles * TM
    ngr = dm.G if ntiles == dm.nt else dm.Bl

    def body(*refs):
        dxo_ref, y_ref, m_ref = refs[:3]
        w_refs = refs[3:3 + n]
        dy_ref = refs[3 + n]
        da_refs = refs[4 + n:4 + 2 * n]
        dgate_ref = refs[4 + 2 * n]
        i = pl.program_id(0)
        dxo = dxo_ref[...]
        dy = (m_ref[0][5:6] * dxo).astype(BF16)
        dy_ref[...] = dy
        for w, da in zip(w_refs, da_refs):
            da[...] = _dot_nt(dy, w[...])
        dgate = jnp.sum(dxo * y_ref[...], axis=0, keepdims=True)
        _acc(dgate_ref, (0, slice(None), slice(None)), dgate, _first_of_group(i, dm))

    row = pl.BlockSpec((TM, D), lambda i: (i, 0))
    return pl.pallas_call(
        body, name=name, grid=(ntiles,),
        in_specs=[row, row, _mod_spec(dm, 1)] + [pl.BlockSpec(w.shape, lambda i: (0, 0)) for w in Ws],
        out_specs=[row] + [pl.BlockSpec((TM, w.shape[0]), lambda i: (i, 0)) for w in Ws]
        + [pl.BlockSpec((1, 1, D), lambda i: (_grp(i, dm), 0, 0))],
        out_shape=[_sds((rows, D), BF16)] + [_sds((rows, w.shape[0]), F32) for w in Ws] + [_sds((ngr, 1, D), F32)],
        compiler_params=_cp(),
    )(dXo, Y, MOD, *Ws)


def loss_head(Xf, target, dm, name):
    TM, D = dm.TM, dm.D

    def body(x_ref, t_ref, l_ref, dx_ref, acc_s):
        i = pl.program_id(0)
        e = x_ref[...] - t_ref[...]
        dx_ref[...] = e * (1.0 / D)

        @pl.when(i == 0)
        def _():
            acc_s[...] = jnp.zeros_like(acc_s)

        acc_s[...] += jnp.sum(e * e, axis=0, keepdims=True)

        @pl.when(i == dm.ntx - 1)
        def _():
            tot = jnp.sum(acc_s[...], axis=1, keepdims=True) * (0.5 / D)
            l_ref[...] = jnp.broadcast_to(tot, (1, LANE))

    row = pl.BlockSpec((TM, D), lambda i: (i, 0))
    return pl.pallas_call(
        body, name=name, grid=(dm.ntx,), in_specs=[row, row],
        out_specs=[pl.BlockSpec((1, LANE), lambda i: (0, 0)), row],
        out_shape=[_sds((1, LANE), F32), _sds((dm.Tx, D), F32)],
        scratch_shapes=[pltpu.VMEM((1, D), F32)], compiler_params=_cp(),
    )(Xf, target)


def _qk_fn(p, gain, cs, sneg, spos):
    y = p * lax.rsqrt(jnp.mean(p * p, axis=-1, keepdims=True) + EPS) * gain
    return _rope(y, cs, sneg, spos)


def _tab_specs(dm, swap):
    def idx(i):
        return jnp.where(i < dm.ntx, i % dm.tps, dm.tps)
    if swap:
        return [pl.BlockSpec((dm.TM, HEAD), lambda j, i: (idx(i), 0))] * 3
    return [pl.BlockSpec((dm.TM, HEAD), lambda i, j: (idx(i), 0))] * 3


def qkv_prep(P0, qkg, tabs, dm, name):
    TM = dm.TM

    def body(p_ref, g_ref, cs_ref, sn_ref, sp_ref, o_ref):
        j = pl.program_id(1)

        @pl.when(j < 6)
        def _():
            o_ref[...] = _qk_fn(p_ref[...], g_ref[0], cs_ref[...], sn_ref[...], sp_ref[...]).astype(BF16)

        @pl.when(j >= 6)
        def _():
            o_ref[...] = p_ref[...].astype(BF16)

    blk = pl.BlockSpec((TM, HEAD), lambda i, j: (i, j))
    return pl.pallas_call(
        body, name=name, grid=(dm.nt, 8),
        in_specs=[blk, pl.BlockSpec((1, 1, HEAD), lambda i, j: (jnp.minimum(j // 4, 1), 0, 0))] + _tab_specs(dm, False),
        out_specs=blk, out_shape=_sds((dm.T, 8 * HEAD), BF16), compiler_params=_cp(),
    )(P0, qkg, *tabs)


def qkv_prep_bwd(P0, dQKV, qkg, tabs, dm, name):
    TM = dm.TM

    def body(p_ref, d_ref, g_ref, cs_ref, sn_ref, sp_ref, dp_ref, dg_ref):
        j, i = pl.program_id(0), pl.program_id(1)
        first = (i == 0) & ((j == 0) | (j == 4))

        @pl.when(j < 6)
        def _():
            _, vjp = jax.vjp(_qk_fn, p_ref[...], g_ref[0], cs_ref[...], sn_ref[...], sp_ref[...])
            dp, dg = vjp(d_ref[...])[:2]
            dp_ref[...] = dp
            _acc(dg_ref, (0, slice(None), slice(None)), dg, first)

        @pl.when(j >= 6)
        def _():
            dp_ref[...] = d_ref[...]

    blk = pl.BlockSpec((TM, HEAD), lambda j, i: (i, j))
    return pl.pallas_call(
        body, name=name, grid=(8, dm.nt),
        in_specs=[blk, blk, pl.BlockSpec((1, 1, HEAD), lambda j, i: (jnp.minimum(j // 4, 1), 0, 0))] + _tab_specs(dm, True),
        out_specs=[blk, pl.BlockSpec((1, 1, HEAD), lambda j, i: (jnp.minimum(j // 4, 1), 0, 0))],
        out_shape=[_sds((dm.T, 8 * HEAD), F32), _sds((2, 1, HEAD), F32)], compiler_params=_cp(),
    )(P0, dQKV, qkg, *tabs)


def _softmax2(sx, sh):
    m = jnp.max(sh, axis=-1, keepdims=True)
    if sx is not None:
        m = jnp.maximum(m, jnp.max(sx, axis=-1, keepdims=True))
    eh = jnp.exp(sh - m)
    l = jnp.sum(eh, axis=-1, keepdims=True)
    ex = None
    if sx is not None:
        ex = jnp.exp(sx - m)
        l = l + jnp.sum(ex, axis=-1, keepdims=True)
    inv = 1.0 / l
    return (None if ex is None else ex * inv), eh * inv


def _attn_geometry(dm, with_x):
    TQ = dm.TM
    if with_x:
        nq, qoff = dm.N // TQ, 0
    else:
        nq, qoff = dm.M // TQ, dm.Tx // TQ
    hoff = dm.Tx // dm.M
    return TQ, nq, qoff, hoff


def attn_fwd(QKV, dm, with_x, name):
    TQ, nq, qoff, hoff = _attn_geometry(dm, with_x)
    scale = HEAD ** -0.5
    rows = dm.Tx if with_x else dm.Th

    def body(*refs):
        if with_x:
            q_ref, kh_ref, vh_ref, kx_ref, vx_ref, o_ref = refs
        else:
            q_ref, kh_ref, vh_ref, o_ref = refs
        q = q_ref[...]
        sh = _dot_nt(q, kh_ref[...]) * scale
        sx = _dot_nt(q, kx_ref[...]) * scale if with_x else None
        px, ph = _softmax2(sx, sh)
        o = _dot(ph, vh_ref[...])
        if with_x:
            o = o + _dot(px, vx_ref[...])
        o_ref[...] = o.astype(BF16)

    qs = pl.BlockSpec((TQ, HEAD), lambda b, kv, g, qi: (qoff + b * nq + qi, kv * 2 + g))
    in_specs = [qs, pl.BlockSpec((dm.M, HEAD), lambda b, kv, g, qi: (hoff + b, 4 + kv)),
                pl.BlockSpec((dm.M, HEAD), lambda b, kv, g, qi: (hoff + b, 6 + kv))]
    args = [QKV, QKV, QKV]
    if with_x:
        in_specs += [pl.BlockSpec((dm.N, HEAD), lambda b, kv, g, qi: (b, 4 + kv)),
                     pl.BlockSpec((dm.N, HEAD), lambda b, kv, g, qi: (b, 6 + kv))]
        args += [QKV, QKV]
    return pl.pallas_call(
        body, name=name, grid=(dm.Bl, A_KV, 2, nq), in_specs=in_specs,
        out_specs=pl.BlockSpec((TQ, HEAD), lambda b, kv, g, qi: (b * nq + qi, kv * 2 + g)),
        out_shape=_sds((rows, A_HEADS * HEAD), BF16), compiler_params=_cp(),
    )(*args)


def attn_bwd(QKV, dO, dm, with_x, init, name):
    TQ, nq, qoff, hoff = _attn_geometry(dm, with_x)
    scale = HEAD ** -0.5
    rows = dm.Tx if with_x else dm.Th

    def body(*refs):
        if with_x:
            (q_ref, kh_ref, vh_ref, kx_ref, vx_ref, do_ref, ikh_ref, ivh_ref,
             dq_ref, dkh_ref, dvh_ref, dkx_ref, dvx_ref) = refs
        else:
            q_ref, kh_ref, vh_ref, do_ref, dq_ref, dkh_ref, dvh_ref = refs
        g, qi = pl.program_id(2), pl.program_id(3)
        q = q_ref[...]
        kh, vh = kh_ref[...], vh_ref[...]
        sh = _dot_nt(q, kh) * scale
        sx = _dot_nt(q, kx_ref[...]) * scale if with_x else None
        px, ph = _softmax2(sx, sh)
        dob = do_ref[...].astype(BF16)
        dph = _dot_nt(dob, vh)
        delta = jnp.sum(dph * ph, axis=-1, keepdims=True)
        if with_x:
            dpx = _dot_nt(dob, vx_ref[...])
            delta = delta + jnp.sum(dpx * px, axis=-1, keepdims=True)
        dsh = (ph * (dph - delta) * scale).astype(BF16)
        dq = _dot(dsh, kh)
        first = (g == 0) & (qi == 0)

        @pl.when(first)
        def _():
            if with_x:
                dkh_ref[...] = ikh_ref[...]
                dvh_ref[...] = ivh_ref[...]
                dkx_ref[...] = jnp.zeros_like(dkx_ref)
                dvx_ref[...] = jnp.zeros_like(dvx_ref)
            else:
                dkh_ref[...] = jnp.zeros_like(dkh_ref)
                dvh_ref[...] = jnp.zeros_like(dvh_ref)

        dkh_ref[...] += _dot_tn(dsh, q)
        dvh_ref[...] += _dot_tn(ph, dob)
        if with_x:
            dsx = (px * (dpx - delta) * scale).astype(BF16)
            dq = dq + _dot(dsx, kx_ref[...])
            dkx_ref[...] += _dot_tn(dsx, q)
            dvx_ref[...] += _dot_tn(px, dob)
        dq_ref[...] = dq

    qs = pl.BlockSpec((TQ, HEAD), lambda b, kv, g, qi: (qoff + b * nq + qi, kv * 2 + g))
    hs = lambda c0: pl.BlockSpec((dm.M, HEAD), lambda b, kv, g, qi: (hoff + b, c0 + kv))
    xs = lambda c0: pl.BlockSpec((dm.N, HEAD), lambda b, kv, g, qi: (b, c0 + kv))
    dos = pl.BlockSpec((TQ, HEAD), lambda b, kv, g, qi: (b * nq + qi, kv * 2 + g))
    acc_h = pl.BlockSpec((dm.M, HEAD), lambda b, kv, g, qi: (b, kv))
    acc_x = pl.BlockSpec((dm.N, HEAD), lambda b, kv, g, qi: (b, kv))
    in_specs, args = [qs, hs(4), hs(6)], [QKV, QKV, QKV]
    out_specs = [dos, acc_h, acc_h]
    out_shape = [_sds((rows, A_HEADS * HEAD), F32), _sds((dm.Th, A_KV * HEAD), F32), _sds((dm.Th, A_KV * HEAD), F32)]
    if with_x:
        in_specs += [xs(4), xs(6), dos, acc_h, acc_h]
        args += [QKV, QKV, dO, init[0], init[1]]
        out_specs += [acc_x, acc_x]
        out_shape += [_sds((dm.Tx, A_KV * HEAD), F32), _sds((dm.Tx, A_KV * HEAD), F32)]
    else:
        in_specs += [dos]
        args += [dO]
    return pl.pallas_call(
        body, name=name, grid=(dm.Bl, A_KV, 2, nq), in_specs=in_specs, out_specs=out_specs,
        out_shape=out_shape, compiler_params=_cp(),
    )(*args)


def _pool_mean(u, w):
    n = u.shape[0]
    t = lax.broadcasted_iota(jnp.int32, (n, 1), 0)
    cnt = (jnp.clip(t + (w - w // 2), 0, n) - jnp.clip(t - w // 2, 0, n)).astype(F32)
    s = _shift_rows(u, -(w // 2))
    for j in range(-(w // 2) + 1, w - w // 2):
        s = s + _shift_rows(u, j)
    return s / cnt - u


def pool_fwd(P0, pw, pscale, dm, on_x, name):
    n, off, rows = (dm.N, 0, dm.Tx) if on_x else (dm.M, dm.Tx // dm.M, dm.Th)
    ng = len(POOL_WINDOWS)

    def body(u_ref, w_ref, s_ref, o_ref):
        for g, w in enumerate(POOL_WINDOWS):
            cols = pl.ds(g * HEAD, HEAD)
            pooled = _pool_mean(u_ref[:, cols], w)
            o_ref[:, cols] = (_dot(pooled, w_ref[g]) * s_ref[:, cols]).astype(BF16)

    return pl.pallas_call(
        body, name=name, grid=(dm.Bl,),
        in_specs=[pl.BlockSpec((n, ng * HEAD), lambda b: (off + b, 2)),
                  pl.BlockSpec((ng, HEAD, HEAD), lambda b: (0, 0, 0)), pl.BlockSpec((1, ng * HEAD), lambda b: (0, 0))],
        out_specs=pl.BlockSpec((n, ng * HEAD), lambda b: (b, 0)),
        out_shape=_sds((rows, ng * HEAD), BF16), compiler_params=_cp(),
    )(P0, pw, pscale)


def pool_bwd(P0, dY, pw, pwT, pscale, dm, on_x, name):
    n, off, rows = (dm.N, 0, dm.Tx) if on_x else (dm.M, dm.Tx // dm.M, dm.Th)
    ng = len(POOL_WINDOWS)

    def body(u_ref, dy_ref, w_ref, wt_ref, s_ref, du_ref, dw_ref, ds_ref):
        b = pl.program_id(0)
        for g, w in enumerate(POOL_WINDOWS):
            cols = pl.ds(g * HEAD, HEAD)
            pooled, vjp = jax.vjp(lambda u: _pool_mean(u, w), u_ref[:, cols])
            pre = _dot(pooled, w_ref[g])
            dy = dy_ref[:, cols]
            dpre = dy * s_ref[:, cols]
            du_ref[:, cols] = vjp(_dot(dpre, wt_ref[g]))[0]
            _acc(dw_ref, (g, slice(None), slice(None)), _dot_tn(pooled, dpre), b == 0)
            _acc(ds_ref, (slice(None), cols), jnp.sum(dy * pre, axis=0, keepdims=True), b == 0)

    full = pl.BlockSpec((ng, HEAD, HEAD), lambda b: (0, 0, 0))
    vec = pl.BlockSpec((1, ng * HEAD), lambda b: (0, 0))
    return pl.pallas_call(
        body, name=name, grid=(dm.Bl,),
        in_specs=[pl.BlockSpec((n, ng * HEAD), lambda b: (off + b, 2)), pl.BlockSpec((n, ng * HEAD), lambda b: (b, 0)),
                  full, full, vec],
        out_specs=[pl.BlockSpec((n, ng * HEAD), lambda b: (b, 0)), full, vec],
        out_shape=[_sds((rows, ng * HEAD), F32), _sds((ng, HEAD, HEAD), F32), _sds((1, ng * HEAD), F32)],
        compiler_params=_cp(),
    )(P0, dY, pw, pwT, pscale)


def _conv_fn(p, w0, w1, w2, kind):
    c = w0 * _shift_rows(p, -1) + w1 * p + w2 * _shift_rows(p, 1)
    a = _silu(c)
    if kind == 2:
        return a
    a = a * lax.rsqrt(jnp.sum(a * a, axis=-1, keepdims=True) + EPS)
    return a * (HEAD ** -0.5) if kind == 0 else a


def gdn_prep(P1, conv_w, dm, on_x, name):
    n, off, rows = (dm.N, 0, dm.Tx) if on_x else (dm.M, dm.Tx // dm.M, dm.Th)

    def body(p_ref, w_ref, o_ref):
        j = pl.program_id(1)
        p, w = p_ref[...], w_ref[...]
        for kind in range(3):
            @pl.when(j // C_HEADS == kind)
            def _():
                o_ref[...] = _conv_fn(p, w[0:1], w[1:2], w[2:3], kind)

    return pl.pallas_call(
        body, name=name, grid=(dm.Bl, 3 * C_HEADS),
        in_specs=[pl.BlockSpec((n, HEAD), lambda b, j: (off + b, j)), pl.BlockSpec((3, HEAD), lambda b, j: (0, j))],
        out_specs=pl.BlockSpec((n, HEAD), lambda b, j: (b, j)),
        out_shape=_sds((rows, 3 * C_HEADS * HEAD), F32), compiler_params=_cp(),
    )(P1, conv_w)


def gdn_prep_bwd(P1, dQ, conv_w, dm, on_x, name):
    n, off, rows = (dm.N, 0, dm.Tx) if on_x else (dm.M, dm.Tx // dm.M, dm.Th)

    def body(p_ref, d0_ref, d1_ref, w_ref, dp_ref, dw_ref):
        j, b = pl.program_id(0), pl.program_id(1)
        p, w = p_ref[...], w_ref[...]
        for kind in range(3):
            @pl.when(j // C_HEADS == kind)
            def _():
                _, vjp = jax.vjp(functools.partial(_conv_fn, kind=kind), p, w[0:1], w[1:2], w[2:3])
                dp, d0, d1, d2 = vjp(d0_ref[0] + d1_ref[0])
                dp_ref[...] = dp
                _acc(dw_ref, (pl.ds(0, 1), slice(None)), d0, b == 0)
                _acc(dw_ref, (pl.ds(1, 1), slice(None)), d1, b == 0)
                _acc(dw_ref, (pl.ds(2, 1), slice(None)), d2, b == 0)

    return pl.pallas_call(
        body, name=name, grid=(3 * C_HEADS, dm.Bl),
        in_specs=[pl.BlockSpec((n, HEAD), lambda j, b: (off + b, j)),
                  pl.BlockSpec((1, n, HEAD), lambda j, b: (0, off + b, j)), pl.BlockSpec((1, n, HEAD), lambda j, b: (1, off + b, j)),
                  pl.BlockSpec((3, HEAD), lambda j, b: (0, j))],
        out_specs=[pl.BlockSpec((n, HEAD), lambda j, b: (b, j)), pl.BlockSpec((3, HEAD), lambda j, b: (0, j))],
        out_shape=[_sds((rows, 3 * C_HEADS * HEAD), F32), _sds((3, 3 * C_HEADS * HEAD), F32)],
        compiler_params=_cp(),
    )(P1, dQ, dQ, conv_w)


def _gate_fn(ab, par):
    lane = lax.broadcasted_iota(jnp.int32, ab.shape, 1)
    is_a = (lane % 16) < C_HEADS
    g = -jnp.exp(par[0:1]) * jax.nn.softplus(ab + par[1:2])
    return jnp.where(lane < 4 * C_HEADS, jnp.where(is_a, g, jax.nn.sigmoid(ab)), 0.0)


def _col(blk, idx):
    lane = lax.broadcasted_iota(jnp.int32, blk.shape, 1)
    return jnp.sum(jnp.where(lane == idx, blk, 0.0), axis=1, keepdims=True)


def _chunk_masks(rev):
    ii = lax.broadcasted_iota(jnp.int32, (CHUNK, CHUNK), 0)
    jj = lax.broadcasted_iota(jnp.int32, (CHUNK, CHUNK), 1)
    ahead = jnp.where(rev, jj - ii, ii - jj)
    return ahead >= 0, ahead > 0, (ii == jj).astype(F32)


def _inv_unit_tri(nmats, eye):
    xs = [eye - n for n in nmats]
    ps = [_hdot(n, n) for n in nmats]
    step = 2
    while True:
        xs = [x + _hdot(x, p) for x, p in zip(xs, ps)]
        step *= 2
        if step >= CHUNK:
            break
        ps = [_hdot(p, p) for p in ps]
    return xs


def _cum_lanes(x, transpose=False):
    lane = lax.broadcasted_iota(jnp.int32, x.shape, 1)
    down, up = x, x
    s = 1
    while s < CHUNK:
        down = down + _shift_rows(down, -s)
        up = up + _shift_rows(up, s)
        s *= 2
    return jnp.where((lane >= 16) if transpose else (lane < 16), down, up)


def _each(f, *lists):
    return [f(*a) for a in zip(*lists)]


def _chunk_common(qs, ks, vs, gcs, gcrs, tots, betas, rev, saved=None):
    incl, strict, eye = _chunk_masks(rev)
    es = _each(lambda gc, gcr: jnp.exp(jnp.where(incl, gc - gcr, NEG)), gcs, gcrs)
    egs = [jnp.exp(gc) for gc in gcs]
    ets = _each(lambda t, gc: jnp.exp(t - gc), tots, gcs)
    gts = [jnp.exp(t) for t in tots]
    kbs = _each(lambda k, b: k * b, ks, betas)
    kks = _each(_dot_nt, kbs, ks)
    qqs = _each(_dot_nt, qs, ks)
    if saved is None:
        nmats = _each(lambda kk, e: jnp.where(strict, kk * e, 0.0), kks, es)
        ainvs = _inv_unit_tri(nmats, eye)
        rhss = _each(lambda v, b, kb, eg: jnp.concatenate([v * b, kb * eg], axis=1), vs, betas, kbs, egs)
        sols = _each(_hdot, ainvs, rhss)
    else:
        ainvs, sols = saved
    return dict(incl=incl, strict=strict, e=es, eg=egs, et=ets, gt=gts, kb=kbs, kk=kks, ainv=ainvs, sol=sols, qq=qqs)


def _chunk_fwd(qs, ks, vs, gcs, gcrs, tots, betas, rev):
    c = _chunk_common(qs, ks, vs, gcs, gcrs, tots, betas, rev)
    incl = c["incl"]
    return _each(lambda q, k, sol, qq, e, et, eg, gt, ainv:
                 (sol[:, :HEAD], sol[:, HEAD:], k * et, q * eg, jnp.where(incl, qq * e, 0.0), gt, ainv),
                 qs, ks, c["sol"], c["qq"], c["e"], c["et"], c["eg"], c["gt"], c["ainv"])


def _chunk_bwd(qs, ks, vs, gcs, gcrs, tots, betas, rev, ainvs, sols, dus, dws, dkts, dqds, dqks, dgts):
    c = _chunk_common(qs, ks, vs, gcs, gcrs, tots, betas, rev, saved=(ainvs, sols))
    incl, strict = c["incl"], c["strict"]
    drhss = _each(lambda a, du, dw: _hdot_tn(a, jnp.concatenate([du, dw], axis=1)), c["ainv"], dus, dws)
    dns = _each(lambda drhs, sol: jnp.where(strict, -_hdot_nt(drhs, sol), 0.0), drhss, c["sol"])
    dkks = _each(lambda dn, e: dn * e, dns, c["e"])
    dqms = [jnp.where(incl, dqk, 0.0) for dqk in dqks]
    dqqs = _each(lambda dqm, e: dqm * e, dqms, c["e"])
    m_q = _each(_dot, dqqs, ks)
    m_k1 = _each(_dot_tn, dqqs, qs)
    m_k2 = _each(_dot_tn, dkks, c["kb"])
    m_kb = _each(_dot, dkks, ks)

    def finish(q, k, v, beta, e, eg, et, gt, kb, kk, qq, drhs, dn, dqm, dkt, dqd, dgt, mq, mk1, mk2, mkb):
        de = dn * kk + dqm * qq
        dq = mq + dqd * eg
        dkb = mkb + drhs[:, HEAD:] * eg
        dk = mk1 + mk2 + dkt * et + dkb * beta
        dv = drhs[:, :HEAD] * beta
        dbeta = jnp.sum(drhs[:, :HEAD] * v + dkb * k, axis=1, keepdims=True)
        deg = jnp.sum(drhs[:, HEAD:] * kb + dqd * q, axis=1, keepdims=True)
        dd = de * e
        dtd = jnp.sum(dkt * k, axis=1, keepdims=True) * et
        dgc = deg * eg - dtd + jnp.sum(dd, axis=1, keepdims=True) - jnp.sum(dd.T, axis=1, keepdims=True)
        dtot = jnp.sum(dtd, axis=0, keepdims=True) + dgt * gt
        return dq, dk, dv, dgc, dtot, dbeta

    return _each(finish, qs, ks, vs, betas, c["e"], c["eg"], c["et"], c["gt"], c["kb"], c["kk"], c["qq"],
                 drhss, dns, dqms, dkts, dqds, dgts, m_q, m_k1, m_k2, m_kb)


def gdn_chunk_pre(QKVg, P1, par, dm, name):
    nch = dm.T // CHUNK
    HD = C_HEADS * HEAD
    abcol = (4 * HD) // LANE

    def body(x_ref, ab_ref, par_ref, u_ref, w_ref, kt_ref, qd_ref, qk_ref, gt_ref, wf_ref, ai_ref, gct_s):
        d = pl.program_id(1)
        rev = d == 1
        gb = _gate_fn(ab_ref[...], par_ref[...])
        gcl = _cum_lanes(gb)
        gct_s[...] = gcl.T
        tot = jnp.sum(gb, axis=0, keepdims=True)
        hs = range(C_HEADS)
        outs = _chunk_fwd(
            [x_ref[:, pl.ds(h * HEAD, HEAD)] for h in hs],
            [x_ref[:, pl.ds((C_HEADS + h) * HEAD, HEAD)] for h in hs],
            [x_ref[:, pl.ds((2 * C_HEADS + h) * HEAD, HEAD)] for h in hs],
            [_col(gcl, d * 16 + h) for h in hs], [gct_s[pl.ds(d * 16 + h, 1), :] for h in hs],
            [_col(tot, d * 16 + h) for h in hs], [_col(gb, d * 16 + 8 + h) for h in hs], rev)
        for h, (u, w, kt, qd, qk, gt, ainv) in enumerate(outs):
            cols = pl.ds(h * HEAD, HEAD)
            u_ref[0, :, cols] = u
            w_ref[0, :, cols] = w.astype(BF16)
            kt_ref[0, :, cols] = kt.astype(BF16)
            qd_ref[0, :, cols] = qd.astype(BF16)
            qk_ref[0, :, cols] = jnp.concatenate([qk, jnp.zeros_like(qk)], axis=1).astype(BF16)
            gt_ref[0, 0, pl.ds(h, 1), :] = jnp.broadcast_to(gt, (1, HEAD))
            wf_ref[0, :, cols] = w
            ai_ref[0, :, cols] = jnp.concatenate([ainv, jnp.zeros_like(ainv)], axis=1)

    big = pl.BlockSpec((1, CHUNK, HD), lambda i, d: (d, i, 0))
    return pl.pallas_call(
        body, name=name, grid=(nch, 2),
        in_specs=[pl.BlockSpec((CHUNK, 3 * HD), lambda i, d: (i, 0)), pl.BlockSpec((CHUNK, LANE), lambda i, d: (i, abcol)),
                  pl.BlockSpec((2, LANE), lambda i, d: (0, 0))],
        out_specs=[big, big, big, big, big, pl.BlockSpec((1, 1, C_HEADS, HEAD), lambda i, d: (d, i, 0, 0)), big, big],
        out_shape=[_sds((2, dm.T, HD), F32), _sds((2, dm.T, HD), BF16), _sds((2, dm.T, HD), BF16),
                   _sds((2, dm.T, HD), BF16), _sds((2, dm.T, HD), BF16), _sds((2, nch, C_HEADS, HEAD), F32),
                   _sds((2, dm.T, HD), F32), _sds((2, dm.T, HD), F32)],
        scratch_shapes=[pltpu.VMEM((LANE, CHUNK), F32)], compiler_params=_cp(),
    )(QKVg, P1, par)


def gdn_chunk_pre_bwd(QKVg, P1, par, U, WF, AI, dU, dW, dKT, dQD, dQK, dGT, dm, name):
    nch = dm.T // CHUNK
    HD = C_HEADS * HEAD
    abcol = (4 * HD) // LANE

    def body(x_ref, ab_ref, par_ref, u_ref, wf_ref, ai_ref, du_ref, dw_ref, dkt_ref, dqd_ref, dqk_ref, dgt_ref,
             dx_ref, dab_ref, dpar_ref, gct_s):
        i, d = pl.program_id(0), pl.program_id(1)
        rev = d == 1
        ab, par = ab_ref[...], par_ref[...]
        gb, gate_vjp = jax.vjp(_gate_fn, ab, par)
        gcl = _cum_lanes(gb)
        gct_s[...] = gcl.T
        tot = jnp.sum(gb, axis=0, keepdims=True)
        lane = lax.broadcasted_iota(jnp.int32, (CHUNK, LANE), 1)
        dgcl = jnp.zeros((CHUNK, LANE), F32)
        dgb = jnp.zeros((CHUNK, LANE), F32)
        first = d == 0
        hs = range(C_HEADS)
        hcols = [pl.ds(h * HEAD, HEAD) for h in hs]
        outs = _chunk_bwd(
            [x_ref[:, c] for c in hcols],
            [x_ref[:, pl.ds((C_HEADS + h) * HEAD, HEAD)] for h in hs],
            [x_ref[:, pl.ds((2 * C_HEADS + h) * HEAD, HEAD)] for h in hs],
            [_col(gcl, d * 16 + h) for h in hs], [gct_s[pl.ds(d * 16 + h, 1), :] for h in hs],
            [_col(tot, d * 16 + h) for h in hs], [_col(gb, d * 16 + 8 + h) for h in hs], rev,
            [ai_ref[0, :, pl.ds(h * HEAD, CHUNK)] for h in hs],
            [jnp.concatenate([u_ref[0, :, c], wf_ref[0, :, c]], axis=1) for c in hcols],
            [du_ref[0, :, c] for c in hcols], [dw_ref[0, :, c] for c in hcols], [dkt_ref[0, :, c] for c in hcols],
            [dqd_ref[0, :, c] for c in hcols], [dqk_ref[0, :, pl.ds(h * HEAD, CHUNK)] for h in hs],
            [dgt_ref[0, 0, pl.ds(h, 1), pl.ds(0, 1)] for h in hs])
        for h, (dq, dk, dv, dgc, dtotal, dbeta) in enumerate(outs):
            idx = d * 16 + h
            dx_ref[0, :, hcols[h]] = dq
            dx_ref[0, :, pl.ds((C_HEADS + h) * HEAD, HEAD)] = dk
            dx_ref[0, :, pl.ds((2 * C_HEADS + h) * HEAD, HEAD)] = dv
            dgcl = dgcl + jnp.where(lane == idx, dgc, 0.0)
            dgb = dgb + jnp.where(lane == idx + 8, dbeta, 0.0) + jnp.where(lane == idx, dtotal, 0.0)
        dab, dpar = gate_vjp(dgb + _cum_lanes(dgcl, transpose=True))
        dab_ref[0] = dab
        _acc(dpar_ref, (slice(None), slice(None)), dpar, (i == 0) & first)

    big = pl.BlockSpec((1, CHUNK, HD), lambda i, d: (d, i, 0))
    return pl.pallas_call(
        body, name=name, grid=(nch, 2),
        in_specs=[pl.BlockSpec((CHUNK, 3 * HD), lambda i, d: (i, 0)), pl.BlockSpec((CHUNK, LANE), lambda i, d: (i, abcol)),
                  pl.BlockSpec((2, LANE), lambda i, d: (0, 0)), big, big, big, big, big, big, big, big,
                  pl.BlockSpec((1, 1, C_HEADS, HEAD), lambda i, d: (d, i, 0, 0))],
        out_specs=[pl.BlockSpec((1, CHUNK, 3 * HD), lambda i, d: (d, i, 0)), pl.BlockSpec((1, CHUNK, LANE), lambda i, d: (d, i, 0)),
                   pl.BlockSpec((2, LANE), lambda i, d: (0, 0))],
        out_shape=[_sds((2, dm.T, 3 * HD), F32), _sds((2, dm.T, LANE), F32), _sds((2, LANE), F32)],
        scratch_shapes=[pltpu.VMEM((LANE, CHUNK), F32)], compiler_params=_cp(),
    )(QKVg, P1, par, U, WF, AI, dU, dW, dKT, dQD, dQK, dGT)


def _scan_chunk(b, d, c, dm):
    nh, nx = dm.M // CHUNK, dm.N // CHUNK
    in_h = c < nh
    pos_h = jnp.where(d == 0, c, nh - 1 - c)
    pos_x = jnp.where(d == 0, c - nh, nx - 1 - (c - nh))
    return jnp.where(in_h, dm.Tx // CHUNK + b * nh + pos_h, b * nx + pos_x)


def gdn_scan_fwd(U, W, KT, QD, QK, GT, dm, name):
    nch = dm.T // CHUNK
    HD = C_HEADS * HEAD
    nsc = (dm.M + dm.N) // CHUNK

    def body(u_ref, w_ref, kt_ref, qd_ref, qk_ref, gt_ref, o_ref, ss_ref, s_s):
        @pl.when(pl.program_id(2) == 0)
        def _():
            s_s[...] = jnp.zeros_like(s_s)

        hs = range(C_HEADS)
        blk = [pl.ds(h * HEAD, HEAD) for h in hs]
        ss = [s_s[b, :] for b in blk]
        sbs = [s.astype(BF16) for s in ss]
        for b, sb in zip(blk, sbs):
            ss_ref[0, 0, b, :] = sb
        ws = [jnp.dot(w_ref[0, :, b], sb, preferred_element_type=F32) for b, sb in zip(blk, sbs)]
        os1 = [jnp.dot(qd_ref[0, :, b], sb, preferred_element_type=F32) for b, sb in zip(blk, sbs)]
        vnbs = [(u_ref[0, :, b] - wv).astype(BF16) for b, wv in zip(blk, ws)]
        os2 = [jnp.dot(qk_ref[0, :, pl.ds(h * HEAD, CHUNK)], vnbs[h], preferred_element_type=F32) for h in hs]
        upd = [_dot_tn(kt_ref[0, :, b], vnb) for b, vnb in zip(blk, vnbs)]
        for h in hs:
            o_ref[0, :, blk[h]] = os1[h] + os2[h]
            s_s[blk[h], :] = ss[h] * gt_ref[0, 0, pl.ds(h, 1), :] + upd[h]

    big = pl.BlockSpec((1, CHUNK, HD), lambda b, d, c: (d, _scan_chunk(b, d, c, dm), 0))
    return pl.pallas_call(
        body, name=name, grid=(dm.Bl, 2, nsc),
        in_specs=[big, big, big, big, big,
                  pl.BlockSpec((1, 1, C_HEADS, HEAD), lambda b, d, c: (d, _scan_chunk(b, d, c, dm), 0, 0))],
        out_specs=[big, pl.BlockSpec((1, 1, HD, HEAD), lambda b, d, c: (d, _scan_chunk(b, d, c, dm), 0, 0))],
        out_shape=[_sds((2, dm.T, HD), F32), _sds((2, nch, HD, HEAD), BF16)],
        scratch_shapes=[pltpu.VMEM((HD, HEAD), F32)], compiler_params=_cp(),
    )(U, W, KT, QD, QK, GT)


def gdn_scan_bwd(dO, SS, U, W, KT, QD, QK, GT, dm, name):
    nch = dm.T // CHUNK
    HD = C_HEADS * HEAD
    nsc = (dm.M + dm.N) // CHUNK

    def body(do_ref, ss_ref, u_ref, w_ref, kt_ref, qd_ref, qk_ref, gt_ref,
             du_ref, dw_ref, dkt_ref, dqd_ref, dqk_ref, dgt_ref, ds_s):
        @pl.when(pl.program_id(2) == 0)
        def _():
            ds_s[...] = jnp.zeros_like(ds_s)

        hs = range(C_HEADS)
        blk = [pl.ds(h * HEAD, HEAD) for h in hs]
        sbs = [ss_ref[0, 0, b, :] for b in blk]
        ss = [s.astype(F32) for s in sbs]
        dobs = [do_ref[:, b].astype(BF16) for b in blk]
        dsns = [ds_s[b, :] for b in blk]
        dsnbs = [t.astype(BF16) for t in dsns]
        wss = [jnp.dot(w_ref[0, :, b], sb, preferred_element_type=F32) for b, sb in zip(blk, sbs)]
        dqds = [_dot_nt(dob, sb) for dob, sb in zip(dobs, sbs)]
        dv1 = [_dot_tn(qk_ref[0, :, pl.ds(h * HEAD, CHUNK)], dobs[h]) for h in hs]
        dv2 = [jnp.dot(kt_ref[0, :, b], t, preferred_element_type=F32) for b, t in zip(blk, dsnbs)]
        ds1 = [_dot_tn(qd_ref[0, :, b], dob) for b, dob in zip(blk, dobs)]
        vnbs = [(u_ref[0, :, b] - wv).astype(BF16) for b, wv in zip(blk, wss)]
        dvns = [a + b for a, b in zip(dv1, dv2)]
        dvnbs = [t.astype(BF16) for t in dvns]
        dqks = [_dot_nt(dob, vnb) for dob, vnb in zip(dobs, vnbs)]
        dkts = [_dot_nt(vnb, t) for vnb, t in zip(vnbs, dsnbs)]
        dws = [_dot_nt(t, sb) for t, sb in zip(dvnbs, sbs)]
        ds2 = [_dot_tn(w_ref[0, :, b], t) for b, t in zip(blk, dvnbs)]
        for h in hs:
            b = blk[h]
            dqd_ref[0, :, b] = dqds[h]
            dqk_ref[0, :, b] = jnp.concatenate([dqks[h], jnp.zeros_like(dqks[h])], axis=1)
            dkt_ref[0, :, b] = dkts[h]
            du_ref[0, :, b] = dvns[h]
            dw_ref[0, :, b] = -dws[h]
            dgt_ref[0, 0, pl.ds(h, 1), :] = jnp.broadcast_to(jnp.sum(dsns[h] * ss[h], keepdims=True), (1, HEAD))
            ds_s[b, :] = dsns[h] * gt_ref[0, 0, pl.ds(h, 1), :] + ds1[h] - ds2[h]

    def mem(b, d, c):
        return _scan_chunk(b, d, nsc - 1 - c, dm)

    big = pl.BlockSpec((1, CHUNK, HD), lambda b, d, c: (d, mem(b, d, c), 0))
    gts = pl.BlockSpec((1, 1, C_HEADS, HEAD), lambda b, d, c: (d, mem(b, d, c), 0, 0))
    return pl.pallas_call(
        body, name=name, grid=(dm.Bl, 2, nsc),
        in_specs=[pl.BlockSpec((CHUNK, HD), lambda b, d, c: (mem(b, d, c), 0)),
                  pl.BlockSpec((1, 1, HD, HEAD), lambda b, d, c: (d, mem(b, d, c), 0, 0)), big, big, big, big, big, gts],
        out_specs=[big, big, big, big, big, gts],
        out_shape=[_sds((2, dm.T, HD), F32)] * 5 + [_sds((2, nch, C_HEADS, HEAD), F32)],
        scratch_shapes=[pltpu.VMEM((HD, HEAD), F32)], compiler_params=_cp(),
    )(dO, SS, U, W, KT, QD, QK, GT)


def _finish_fn(o, z, gain):
    y = o * lax.rsqrt(jnp.mean(o * o, axis=-1, keepdims=True) + EPS) * gain
    return y * _silu(z)


def gdn_finish(O, P1, og, dm, name):
    TM = dm.TM
    HD = C_HEADS * HEAD
    zc = (3 * HD) // HEAD

    def body(o0_ref, o1_ref, z_ref, g_ref, y_ref):
        y_ref[...] = _finish_fn(o0_ref[0] + o1_ref[0], z_ref[...], g_ref[...]).astype(BF16)

    return pl.pallas_call(
        body, name=name, grid=(dm.ntx, C_HEADS),
        in_specs=[pl.BlockSpec((1, TM, HEAD), lambda i, j: (0, i, j)), pl.BlockSpec((1, TM, HEAD), lambda i, j: (1, i, j)),
                  pl.BlockSpec((TM, HEAD), lambda i, j: (i, zc + j)), pl.BlockSpec((1, HEAD), lambda i, j: (0, 0))],
        out_specs=pl.BlockSpec((TM, HEAD), lambda i, j: (i, j)),
        out_shape=_sds((dm.Tx, HD), BF16), compiler_params=_cp(),
    )(O, O, P1, og)


def gdn_finish_bwd(O, P1, og, dY, dm, name):
    TM = dm.TM
    HD = C_HEADS * HEAD
    zc = (3 * HD) // HEAD

    def body(o0_ref, o1_ref, z_ref, g_ref, dy_ref, do_ref, dz_ref, dg_ref):
        i, j = pl.program_id(0), pl.program_id(1)
        _, vjp = jax.vjp(_finish_fn, o0_ref[0] + o1_ref[0], z_ref[...], g_ref[...])
        do, dz, dg = vjp(dy_ref[...])
        do_ref[...] = do
        dz_ref[...] = dz
        _acc(dg_ref, (slice(None), slice(None)), dg, (i == 0) & (j == 0))

    blk = pl.BlockSpec((TM, HEAD), lambda i, j: (i, j))
    return pl.pallas_call(
        body, name=name, grid=(dm.ntx, C_HEADS),
        in_specs=[pl.BlockSpec((1, TM, HEAD), lambda i, j: (0, i, j)), pl.BlockSpec((1, TM, HEAD), lambda i, j: (1, i, j)),
                  pl.BlockSpec((TM, HEAD), lambda i, j: (i, zc + j)), pl.BlockSpec((1, HEAD), lambda i, j: (0, 0)), blk],
        out_specs=[blk, blk, pl.BlockSpec((1, HEAD), lambda i, j: (0, 0))],
        out_shape=[_sds((dm.Tx, HD), F32), _sds((dm.Tx, HD), F32), _sds((1, HEAD), F32)],
        compiler_params=_cp(),
    )(O, O, P1, og, dY)


def adaln_fwd(c_ext, w_mod, b_loc, name):
    R, D = c_ext.shape
    nl = w_mod.shape[2]
    tn = _pick(nl, 384)

    def body(c_ref, w_ref, b_ref, o_ref):
        o_ref[0] = _dot(_silu(c_ref[...]), w_ref[0]) + b_ref[0]

    return pl.pallas_call(
        body, name=name, grid=(2, nl // tn),
        in_specs=[pl.BlockSpec((R, D), lambda l, j: (0, 0)), pl.BlockSpec((1, D, tn), lambda l, j: (l, 0, j)),
                  pl.BlockSpec((1, 1, tn), lambda l, j: (l, 0, j))],
        out_specs=pl.BlockSpec((1, R, tn), lambda l, j: (l, 0, j)),
        out_shape=_sds((2, R, nl), F32), compiler_params=_cp(),
    )(c_ext, w_mod, b_loc)


def adaln_bwd(c_ext, c_ctx, w_mod, dmx, dmh, nb, name):
    R, D = c_ext.shape
    nl = w_mod.shape[2]
    tn = _pick(nl, 384)
    nj = nl // tn

    def body(c_ref, cc_ref, w_ref, dmx_ref, dmh_ref, gw_ref, dc_ref):
        l, j = pl.program_id(0), pl.program_id(1)
        dh = dmh_ref[0, 0:1, :]
        for k in range(1, N_DEV):
            dh = dh + dmh_ref[0, k:k + 1, :]
        row = lax.broadcasted_iota(jnp.int32, (R, tn), 0)
        dmat = dmx_ref[0] + jnp.where(row == nb, dh, 0.0)
        gw_ref[0] = _dot_tn(_silu(c_ref[...]), dmat)
        part = _dot_nt(jnp.broadcast_to(dh, (8, tn)), w_ref[0])[0:1]
        _acc(dc_ref, (slice(None), slice(None)), part, (l == 0) & (j == 0))

        @pl.when((l == 1) & (j == nj - 1))
        def _():
            cc = cc_ref[...]
            sg = jax.nn.sigmoid(cc)
            dc_ref[...] = dc_ref[...] * (sg * (1.0 + cc * (1.0 - sg)))

    return pl.pallas_call(
        body, name=name, grid=(2, nj),
        in_specs=[pl.BlockSpec((R, D), lambda l, j: (0, 0)), pl.BlockSpec((1, D), lambda l, j: (0, 0)),
                  pl.BlockSpec((1, D, tn), lambda l, j: (l, 0, j)), pl.BlockSpec((1, R, tn), lambda l, j: (l, 0, j)),
                  pl.BlockSpec((1, N_DEV, tn), lambda l, j: (l, 0, j))],
        out_specs=[pl.BlockSpec((1, D, tn), lambda l, j: (l, 0, j)), pl.BlockSpec((1, D), lambda l, j: (0, 0))],
        out_shape=[_sds((2, D, nl), F32), _sds((1, D), F32)], compiler_params=_cp(),
    )(c_ext, c_ctx, w_mod, dmx, dmh)


def bmod_grad(dmx, dmh, name):
    _, R, n9 = dmx.shape

    def body(dmx_ref, dmh_ref, o_ref):
        o_ref[0] = jnp.sum(dmx_ref[0], axis=0, keepdims=True) + jnp.sum(dmh_ref[0], axis=0, keepdims=True)

    return pl.pallas_call(
        body, name=name, grid=(2,),
        in_specs=[pl.BlockSpec((1, R, n9), lambda l: (l, 0, 0)), pl.BlockSpec((1, N_DEV, n9), lambda l: (l, 0, 0))],
        out_specs=pl.BlockSpec((1, 1, n9), lambda l: (l, 0, 0)), out_shape=_sds((2, 1, n9), F32),
        compiler_params=_cp(),
    )(dmx, dmh)


def adamw(gs, w, m, v, name):
    S, R, C = gs.shape
    cap = max(8, (1 << 20) // (S * C))
    tr = R
    if R > cap:
        tr = max(t for t in range(8, cap + 1, 8) if R % t == 0)

    def body(g_ref, w_ref, m_ref, v_ref, go_ref, d_ref, mo_ref, vo_ref):
        g = g_ref[0].astype(F32)
        for k in range(1, S):
            g = g + g_ref[k].astype(F32)
        mn = ADAM_B1 * m_ref[...] + (1.0 - ADAM_B1) * g
        vn = ADAM_B2 * v_ref[...] + (1.0 - ADAM_B2) * jnp.square(g)
        m_hat = mn / (1.0 - ADAM_B1 ** ADAM_STEP)
        v_hat = vn / (1.0 - ADAM_B2 ** ADAM_STEP)
        go_ref[...] = g
        d_ref[...] = -ADAM_LR * (m_hat / (jnp.sqrt(v_hat) + ADAM_EPS) + ADAM_WD * w_ref[...])
        mo_ref[...] = mn
        vo_ref[...] = vn

    blk = pl.BlockSpec((tr, C), lambda i: (i, 0))
    return pl.pallas_call(
        body, name=name, grid=(R // tr,),
        in_specs=[pl.BlockSpec((S, tr, C), lambda i: (0, i, 0)), blk, blk, blk],
        out_specs=[blk] * 4, out_shape=[_sds((R, C), F32)] * 4, compiler_params=_cp(),
    )(gs, w, m, v)


def _gather_flat(parts, dtype, name):
    flat = jnp.concatenate([p.astype(dtype).reshape(-1) for p in parts])
    n = flat.shape[0]
    pad = (-n) % LANE
    if pad:
        flat = jnp.concatenate([flat, jnp.zeros((pad,), dtype)])
    got = all_gather([flat.reshape(-1, LANE)], name)[0].reshape(N_DEV, -1)
    out, off = [], 0
    for p in parts:
        out.append(got[:, off:off + p.size].reshape((N_DEV,) + p.shape))
        off += p.size
    return out


def _cols_full(g):
    return g.transpose(1, 0, 2).reshape(g.shape[1], -1)


def _cols_split(full):
    K = full.shape[0]
    return full.reshape(K, N_DEV, -1).transpose(1, 0, 2)


def kernel(x, c, ctx, c_ctx, w_mod, b_mod, norm_g, ffn_wg, ffn_wu, ffn_wd, ab_w_in, ab_q_norm, ab_k_norm, pool_w, pool_scale, ab_w_out, gdn_w_in, gdn_conv_w, gdn_a_log, gdn_dt_bias, gdn_o_norm, gdn_w_out, loss_target, m_c_ctx, m_w_mod, m_b_mod, m_norm_g, m_ffn_wg, m_ffn_wu, m_ffn_wd, m_ab_w_in, m_ab_q_norm, m_ab_k_norm, m_pool_w, m_pool_scale, m_ab_w_out, m_gdn_w_in, m_gdn_conv_w, m_gdn_a_log, m_gdn_dt_bias, m_gdn_o_norm, m_gdn_w_out, v_c_ctx, v_w_mod, v_b_mod, v_norm_g, v_ffn_wg, v_ffn_wu, v_ffn_wd, v_ab_w_in, v_ab_q_norm, v_ab_k_norm, v_pool_w, v_pool_scale, v_ab_w_out, v_gdn_w_in, v_gdn_conv_w, v_gdn_a_log, v_gdn_dt_bias, v_gdn_o_norm, v_gdn_w_out):
    Bl, N, D = x.shape
    M = ctx.shape[1]
    F = ffn_wd.shape[2] * N_DEV
    dm = Dims(Bl, N, M, D, F)
    TM, Tx, Th, T, G = dm.TM, dm.Tx, dm.Th, dm.T, dm.G
    HD = C_HEADS * HEAD
    me = 4 * lax.axis_index("x") + 2 * lax.axis_index("y") + lax.axis_index("c")
    nb = N_DEV * Bl
    R = -(-(nb + 1) // 8) * 8
    nl = w_mod.shape[2]
    n_gdn = gdn_w_in.shape[2] * N_DEV
    n_gdn_pad = -(-n_gdn // LANE) * LANE

    big = [w.astype(BF16) for w in (ffn_wg, ffn_wu, ffn_wd, ab_w_in, ab_w_out, gdn_w_in, gdn_w_out)]
    g_wg, g_wu, g_wd, g_abin, g_about, g_gin, g_gout = all_gather(big, "gather_weights")
    g_c, g_ng, g_cw = _gather_flat([c, norm_g, gdn_conv_w], F32, "gather_small")
    W_ABIN = _cols_full(g_abin[:, 0])
    W_ABOUT = g_about[:, 0].reshape(-1, D)
    W_GIN = jnp.pad(_cols_full(g_gin[:, 0]), ((0, 0), (0, n_gdn_pad - n_gdn)))
    W_GOUT = g_gout[:, 0].reshape(-1, D)
    gains = g_ng.transpose(1, 2, 0, 3).reshape(2, 3, 1, D)
    conv_w = g_cw[:, 0].transpose(1, 0, 2).reshape(3, -1)

    c_all = g_c.reshape(nb, D)
    c_ext = jnp.concatenate([c_all, c_ctx[None], jnp.zeros((R - nb - 1, D), F32)], 0)
    b_loc = lax.dynamic_slice_in_dim(b_mod, me * nl, nl, axis=1).reshape(2, 1, nl)
    mod_loc = adaln_fwd(c_ext, w_mod, b_loc, "adaln_fwd")
    (g_mod,) = _gather_flat([mod_loc], F32, "gather_mod")
    mod_full = g_mod.transpose(1, 2, 0, 3).reshape(2, R, 9 * D)
    MOD = []
    for l in range(2):
        mine = lax.dynamic_slice_in_dim(mod_full[l], me * Bl, Bl, axis=0)
        MOD.append(jnp.concatenate([mine, mod_full[l, nb:nb + 1]], 0).reshape(G, 9, D))

    dm5, dmr = dm.with_tile(512), dm.with_tile(1024)
    tabs = _rope_tables(dmr)
    qkg = jnp.stack([ab_q_norm, ab_k_norm])
    pw = pool_w[0].astype(BF16)
    pwT = pool_w[0].transpose(0, 2, 1).astype(BF16)
    par = jnp.stack([jnp.pad(jnp.pad(p[0], ((0, 0), (0, 8))).reshape(-1), (0, LANE - 32))
                     for p in (gdn_a_log, gdn_dt_bias)])

    X0 = jnp.concatenate([x.reshape(Tx, D), ctx.reshape(Th, D)], 0)
    def ffn(X, l, s, s0, all_rows, tag):
        return ffn_fwd(X, MOD[l], gains[l, 2 * s], g_wg, g_wu, g_wd, l, s, s0, dm, all_rows, "ffn_fwd_" + tag)

    X1, Y1 = ffn(X0, 0, 0, 0, True, "00")
    P0, XN0 = modmm(X1, MOD[0], gains[0, 1], W_ABIN, 3, dm5, "ab_in_proj")
    QKV = qkv_prep(P0, qkg, tabs, dmr, "qkv_prep")
    ATT = jnp.concatenate([attn_fwd(QKV, dm, True, "attn_fwd_x"), attn_fwd(QKV, dm, False, "attn_fwd_h")], 0)
    POOL = jnp.concatenate([pool_fwd(P0, pw, pool_scale, dm, True, "pool_fwd_x"),
                            pool_fwd(P0, pw, pool_scale, dm, False, "pool_fwd_h")], 0)
    na = A_HEADS * HEAD
    X2, YM0 = proj_res([ATT, POOL], [W_ABOUT[:na], W_ABOUT[na:]], X1, MOD[0], dm5, dm5.nt, "ab_out_proj")
    X3, Y3 = ffn(X2, 0, 1, 6, True, "01")
    X4, Y4 = ffn(X3, 1, 0, 0, True, "10")
    P1, XN1 = modmm(X4, MOD[1], gains[1, 1], W_GIN, 3, dm5, "gdn_in_proj")
    QKVg = jnp.concatenate([gdn_prep(P1, conv_w, dm, True, "gdn_prep_x"), gdn_prep(P1, conv_w, dm, False, "gdn_prep_h")], 0)
    U, W, KT, QD, QK, GT, WF, AI = gdn_chunk_pre(QKVg, P1, par, dm, "gdn_chunk_pre")
    O, SS = gdn_scan_fwd(U, W, KT, QD, QK, GT, dm, "gdn_scan_fwd")
    FIN = gdn_finish(O, P1, gdn_o_norm, dmr, "gdn_finish")
    X5, YM1 = proj_res([FIN], [W_GOUT], X4, MOD[1], dm5, dm5.ntx, "gdn_out_proj")
    X6, Y6 = ffn(X5, 1, 1, 6, False, "11")
    lvec, dX6 = loss_head(X6, loss_target.reshape(Tx, D), dmr, "loss_head")
    loss = lax.psum(lvec[0, 0], AXES)

    zrow = lambda a: jnp.concatenate([a, jnp.zeros((G - a.shape[0],) + a.shape[1:], F32)], 0) if a.shape[0] < G else a

    def ffn_back(Xin, dXo, Y, l, s, s0, all_rows, tag):
        dXi, XNb, DOb, Hb, DGb, DUb, dmod, dgain = ffn_bwd(
            Xin, dXo, Y, MOD[l], gains[l, 2 * s], g_wg, g_wu, g_wd, l, s, s0, dm, all_rows, "ffn_bwd_" + tag)
        dwg, dwu, dwd = ffn_dw(XNb, DOb, Hb, DGb, DUb, dm, "ffn_dw_" + tag)
        return dXi, zrow(dmod), dgain, dwg, dwu, dwd

    dX5, dmod_12, dgain_12, dwg11, dwu11, dwd11 = ffn_back(X5, dX6, Y6, 1, 1, 6, False, "11")
    DY1, dFIN, dgate_1 = proj_res_bwd(dX5, YM1, MOD[1], [W_GOUT], dm5, dm5.ntx, "gdn_out_proj_bwd")
    d_gout = atb(FIN, DY1, Tx, dm, "gdn_dwout")
    dOsum, dZ, d_onorm = gdn_finish_bwd(O, P1, gdn_o_norm, dFIN, dmr, "gdn_finish_bwd")
    dO_all = jnp.concatenate([dOsum, jnp.zeros((Th, HD), F32)], 0)
    dU, dW, dKT, dQD, dQK, dGT = gdn_scan_bwd(dO_all, SS, U, W, KT, QD, QK, GT, dm, "gdn_scan_bwd")
    dQKVg, dAB, dPAR = gdn_chunk_pre_bwd(QKVg, P1, par, U, WF, AI, dU, dW, dKT, dQD, dQK, dGT, dm, "gdn_chunk_pre_bwd")
    dPx, dcw_x = gdn_prep_bwd(P1, dQKVg, conv_w, dm, True, "gdn_prep_bwd_x")
    dPh, dcw_h = gdn_prep_bwd(P1, dQKVg, conv_w, dm, False, "gdn_prep_bwd_h")
    d_conv = dcw_x + dcw_h
    dP1 = jnp.concatenate([jnp.concatenate([dPx, dPh], 0), jnp.concatenate([dZ, jnp.zeros((Th, HD), F32)], 0),
                           dAB[0] + dAB[1]], axis=1).astype(BF16)
    d_gin = atb(XN1, dP1, T, dm, "gdn_dwin")[:, :n_gdn]
    dX5_full = jnp.concatenate([dX5, jnp.zeros((Th, D), F32)], 0)
    dX4, dmod_11, dgain_11 = mixin_bwd(dP1, W_GIN, X4, dX5_full, MOD[1], gains[1, 1], 3, dm, "gdn_in_proj_bwd")
    dX3, dmod_10, dgain_10, dwg10, dwu10, dwd10 = ffn_back(X3, dX4, Y4, 1, 0, 0, True, "10")
    dMOD1 = jnp.concatenate([dmod_10, dmod_11, zrow(dgate_1), dmod_12], 1).reshape(G, 9 * D)

    dX2, dmod_02, dgain_02, dwg01, dwu01, dwd01 = ffn_back(X2, dX3, Y3, 0, 1, 6, True, "01")
    DY0, dATT, dPOOL, dgate_0 = proj_res_bwd(dX2, YM0, MOD[0], [W_ABOUT[:na], W_ABOUT[na:]], dm5, dm5.nt, "ab_out_proj_bwd")
    d_about = jnp.concatenate([atb(ATT, DY0, T, dm, "ab_dwout_a"), atb(POOL, DY0, T, dm, "ab_dwout_p")], 0)
    dUx, dpw_x, dps_x = pool_bwd(P0, dPOOL[:Tx], pw, pwT, pool_scale, dm, True, "pool_bwd_x")
    dUh, dpw_h, dps_h = pool_bwd(P0, dPOOL[Tx:], pw, pwT, pool_scale, dm, False, "pool_bwd_h")
    dQh, dKh0, dVh0 = attn_bwd(QKV, dATT[Tx:], dm, False, None, "attn_bwd_h")
    dQx, dKh, dVh, dKx, dVx = attn_bwd(QKV, dATT[:Tx], dm, True, (dKh0, dVh0), "attn_bwd_x")
    dQKV = jnp.concatenate([jnp.concatenate([dQx, dQh], 0), jnp.concatenate([dKx, dKh], 0), jnp.concatenate([dVx, dVh], 0)], 1)
    dPqkv, d_qkg = qkv_prep_bwd(P0, dQKV, qkg, tabs, dmr, "qkv_prep_bwd")
    dP0 = jnp.concatenate([dPqkv, jnp.concatenate([dUx, dUh], 0)], 1).astype(BF16)
    d_abin = atb(XN0, dP0, T, dm, "ab_dwin")
    dX1, dmod_01, dgain_01 = mixin_bwd(dP0, W_ABIN, X1, dX2, MOD[0], gains[0, 1], 3, dm5, "ab_in_proj_bwd")
    dX0, dmod_00, dgain_00, dwg00, dwu00, dwd00 = ffn_back(X0, dX1, Y1, 0, 0, 0, True, "00")
    dMOD0 = jnp.concatenate([dmod_00, dmod_01, dgate_0, dmod_02], 1).reshape(G, 9 * D)
    grad_x = dX0[:Tx].reshape(Bl, N, D)

    d_ng = jnp.concatenate([dgain_00, dgain_01, dgain_02, dgain_10, dgain_11, dgain_12], 0)
    nf = ffn_wg.shape[3]
    parts = [jnp.concatenate([dwg00, dwg01, dwg10, dwg11], 1), jnp.concatenate([dwu00, dwu01, dwu10, dwu11], 1),
             jnp.concatenate([dwd00, dwd01, dwd10, dwd11], 1),
             _cols_split(d_abin), d_about.reshape(N_DEV, -1, D), _cols_split(d_gin), d_gout.reshape(N_DEV, -1, D),
             _cols_split(d_conv), _cols_split(d_ng)]
    gs_wg, gs_wu, gs_wd, gs_abin, gs_about, gs_gin, gs_gout, gs_conv, gs_ng = scatter_blocks(parts, "scatter_grads")

    d_alog = dPAR[0, :32].reshape(2, 16)[:, :8].reshape(1, 16)
    d_dtb = dPAR[1, :32].reshape(2, 16)[:, :8].reshape(1, 16)
    small = [d_qkg[0], d_qkg[1], (dpw_x + dpw_h).reshape(-1, HEAD), dps_x + dps_h, d_alog, d_dtb, d_onorm,
             jnp.stack([dMOD0, dMOD1])]
    gs_qn, gs_kn, gs_pw, gs_ps, gs_alog, gs_dtb, gs_on, g_dm = _gather_flat(small, F32, "gather_small_grads")
    dmx = g_dm[:, :, :Bl].transpose(1, 0, 2, 3).reshape(2, nb, 9 * D)
    dmx = jnp.concatenate([dmx, jnp.zeros((2, R - nb, 9 * D), F32)], 1)
    dmh = g_dm[:, :, Bl].transpose(1, 0, 2)
    cols_of_me = lambda a: lax.dynamic_slice_in_dim(a, me * nl, nl, axis=2)
    d_wmod, dcc = adaln_bwd(c_ext, c_ctx[None], w_mod, cols_of_me(dmx), cols_of_me(dmh), nb, "adaln_bwd")
    d_bmod = bmod_grad(dmx, dmh, "bmod_grad")
    (gs_cc,) = _gather_flat([dcc], F32, "gather_cctx_grad")

    def upd(gs, w, m, v, shape2, name):
        outs = adamw(gs.reshape((gs.shape[0],) + shape2), w.reshape(shape2), m.reshape(shape2), v.reshape(shape2), "adamw_" + name)
        return [o.reshape(w.shape) for o in outs]

    res = [
        upd(gs_cc, c_ctx, m_c_ctx, v_c_ctx, (1, D), "c_ctx"),
        upd(d_wmod[None], w_mod, m_w_mod, v_w_mod, (2 * D, nl), "w_mod"),
        upd(d_bmod[None], b_mod, m_b_mod, v_b_mod, (2, 9 * D), "b_mod"),
        upd(gs_ng, norm_g, m_norm_g, v_norm_g, (6, D // N_DEV), "norm_g"),
        upd(gs_wg, ffn_wg, m_ffn_wg, v_ffn_wg, (4 * D, nf), "ffn_wg"),
        upd(gs_wu, ffn_wu, m_ffn_wu, v_ffn_wu, (4 * D, nf), "ffn_wu"),
        upd(gs_wd, ffn_wd, m_ffn_wd, v_ffn_wd, (4 * nf, D), "ffn_wd"),
        upd(gs_abin, ab_w_in, m_ab_w_in, v_ab_w_in, (D, ab_w_in.shape[2]), "ab_w_in"),
        upd(gs_qn, ab_q_norm, m_ab_q_norm, v_ab_q_norm, (1, HEAD), "ab_q_norm"),
        upd(gs_kn, ab_k_norm, m_ab_k_norm, v_ab_k_norm, (1, HEAD), "ab_k_norm"),
        upd(gs_pw, pool_w, m_pool_w, v_pool_w, (len(POOL_WINDOWS) * HEAD, HEAD), "pool_w"),
        upd(gs_ps, pool_scale, m_pool_scale, v_pool_scale, (1, len(POOL_WINDOWS) * HEAD), "pool_scale"),
        upd(gs_about, ab_w_out, m_ab_w_out, v_ab_w_out, (ab_w_out.shape[1], D), "ab_w_out"),
        upd(gs_gin, gdn_w_in, m_gdn_w_in, v_gdn_w_in, (D, gdn_w_in.shape[2]), "gdn_w_in"),
        upd(gs_conv, gdn_conv_w, m_gdn_conv_w, v_gdn_conv_w, (3, gdn_conv_w.shape[2]), "gdn_conv_w"),
        upd(gs_alog, gdn_a_log, m_gdn_a_log, v_gdn_a_log, (1, 16), "gdn_a_log"),
        upd(gs_dtb, gdn_dt_bias, m_gdn_dt_bias, v_gdn_dt_bias, (1, 16), "gdn_dt_bias"),
        upd(gs_on, gdn_o_norm, m_gdn_o_norm, v_gdn_o_norm, (1, HEAD), "gdn_o_norm"),
        upd(gs_gout, gdn_w_out, m_gdn_w_out, v_gdn_w_out, (gdn_w_out.shape[1], D), "gdn_w_out"),
    ]
    return (loss, grad_x, *[r[0] for r in res], *[r[1] for r in res], *[r[2] for r in res], *[r[3] for r in res])
```

```python
import functools
from typing import NamedTuple

import jax
import jax.numpy as jnp
from jax import lax
from jax.experimental import pallas as pl
from jax.experimental.pallas import tpu as pltpu

F32, BF16 = jnp.float32, jnp.bfloat16
EPS = 1e-6
HEAD = 128
CHUNK = 64
GRID_W = 64
ROPE_THETA = 10000.0
POOL_WINDOWS = (2, 4, 8, 16)
A_HEADS, A_KV = 4, 2
C_HEADS = 8
N_DEV = 8
AXES = ("x", "y", "c")
ADAM_LR, ADAM_B1, ADAM_B2, ADAM_EPS, ADAM_WD, ADAM_STEP = 0.001, 0.9, 0.999, 1e-08, 0.01, 10
LANE = 128
VMEM_LIMIT = 56 * 1024 * 1024
NEG = -1e30


def _cp():
    return pltpu.CompilerParams(vmem_limit_bytes=VMEM_LIMIT)


def _sds(shape, dtype):
    return jax.ShapeDtypeStruct(tuple(shape), dtype)


def _dot(a, b):
    return jnp.dot(a.astype(BF16), b.astype(BF16), preferred_element_type=F32)


def _dot_nt(a, b):
    return lax.dot_general(a.astype(BF16), b.astype(BF16), (((1,), (1,)), ((), ())), preferred_element_type=F32)


def _dot_tn(a, b):
    return lax.dot_general(a.astype(BF16), b.astype(BF16), (((0,), (0,)), ((), ())), preferred_element_type=F32)


def _dot3(a, b, dims):
    ah, bh = a.astype(BF16), b.astype(BF16)
    al, bl = (a - ah.astype(F32)).astype(BF16), (b - bh.astype(F32)).astype(BF16)
    f = lambda x, y: lax.dot_general(x, y, (dims, ((), ())), preferred_element_type=F32)
    return f(ah, bh) + (f(ah, bl) + f(al, bh))


def _hdot(a, b):
    return _dot3(a, b, ((1,), (0,)))


def _hdot_nt(a, b):
    return _dot3(a, b, ((1,), (1,)))


def _hdot_tn(a, b):
    return _dot3(a, b, ((0,), (0,)))


def _pick(n, cap):
    if n <= cap:
        return n
    best = None
    for t in range(LANE, cap + 1, LANE):
        if n % t == 0:
            best = t
    assert best is not None, (n, cap)
    return best


class Dims(NamedTuple):
    Bl: int
    N: int
    M: int
    D: int
    F: int
    tm: int = 0

    @property
    def TM(self):
        return self.tm if self.tm else min(256, self.M)

    def with_tile(self, cap):
        return self._replace(tm=max(t for t in (1024, 512, 256, 128) if t <= cap and self.N % t == 0 and self.Th % t == 0))

    @property
    def Tx(self):
        return self.Bl * self.N

    @property
    def Th(self):
        return self.Bl * self.M

    @property
    def T(self):
        return self.Tx + self.Th

    @property
    def ntx(self):
        return self.Tx // self.TM

    @property
    def nt(self):
        return self.T // self.TM

    @property
    def tps(self):
        return self.N // self.TM

    @property
    def G(self):
        return self.Bl + 1


def _grp(i, dm, tm=None):
    tm = dm.TM if tm is None else tm
    return jnp.where(i < dm.Tx // tm, i // (dm.N // tm), dm.Bl)


def _first_of_group(i, dm, tm=None):
    tm = dm.TM if tm is None else tm
    return jnp.where(i < dm.Tx // tm, i % (dm.N // tm) == 0, i == dm.Tx // tm)


def _contraction_tile(rows):
    return max(t for t in (1024, 512, 256, 128) if rows % t == 0)


def _ffn_tile(dm):
    return max(t for t in (512, 256, 128) if dm.N % t == 0 and dm.Th % t == 0)


def _acc(ref, idx, val, first):
    @pl.when(first)
    def _():
        ref[idx] = val

    @pl.when(jnp.logical_not(first))
    def _():
        ref[idx] += val


def _modulate(x, gain, shift, scale):
    y = x * lax.rsqrt(jnp.mean(x * x, axis=-1, keepdims=True) + EPS)
    return (y * gain) * (1.0 + scale) + shift


def _silu(x):
    return x * jax.nn.sigmoid(x)


@functools.partial(jax.custom_vjp, nondiff_argnums=(1,))
def _shift_rows(a, k):
    n = a.shape[0]
    if k == 0:
        return a
    r = lax.broadcasted_iota(jnp.int32, a.shape, 0)
    rolled = pltpu.roll(a, (-k) % n, 0)
    ok = (r + k >= 0) & (r + k < n)
    return jnp.where(ok, rolled, 0.0)


def _shift_rows_fwd(a, k):
    return _shift_rows(a, k), None


def _shift_rows_bwd(k, _, d):
    return (_shift_rows(d, -k),)


_shift_rows.defvjp(_shift_rows_fwd, _shift_rows_bwd)


@functools.partial(jax.custom_vjp, nondiff_argnums=(1,))
def _roll_lanes(a, s):
    return pltpu.roll(a, s % LANE, 1)


def _roll_lanes_fwd(a, s):
    return _roll_lanes(a, s), None


def _roll_lanes_bwd(s, _, d):
    return (_roll_lanes(d, -s),)


_roll_lanes.defvjp(_roll_lanes_fwd, _roll_lanes_bwd)


def _rope(t, cs, sneg, spos):
    return t * cs + _roll_lanes(t, 96) * sneg + _roll_lanes(t, 32) * spos


def _rope_tables(dm):
    rows = dm.N // GRID_W
    row = jnp.repeat(jnp.arange(rows), GRID_W).astype(F32)
    col = jnp.tile(jnp.arange(GRID_W), rows).astype(F32)
    half = HEAD // 2
    inv_freq = jnp.power(ROPE_THETA, -jnp.arange(0, half, 2, dtype=F32) / half)
    ar, ac = row[:, None] * inv_freq, col[:, None] * inv_freq
    cs = jnp.concatenate([jnp.cos(ar), jnp.cos(ar), jnp.cos(ac), jnp.cos(ac)], axis=1)
    z = jnp.zeros_like(ar)
    sneg = jnp.concatenate([-jnp.sin(ar), z, -jnp.sin(ac), z], axis=1)
    spos = jnp.concatenate([z, jnp.sin(ar), z, jnp.sin(ac)], axis=1)
    pad1 = jnp.ones((dm.TM, HEAD), F32)
    pad0 = jnp.zeros((dm.TM, HEAD), F32)
    return (jnp.concatenate([cs, pad1], 0), jnp.concatenate([sneg, pad0], 0), jnp.concatenate([spos, pad0], 0))


def all_gather(xs, name):
    n = len(xs)

    def body(*refs):
        x_refs, out_refs = refs[:n], refs[n:2 * n]
        send_sems, recv_sems, local_sems = refs[2 * n:]
        x, y, c = lax.axis_index("x"), lax.axis_index("y"), lax.axis_index("c")
        me, sibling = (x, y, c), (x, y, 1 - c)
        chips = [(1 - x, y), (x, 1 - y), (1 - x, 1 - y)]

        def slot(a, px, py, pc):
            return out_refs[a].at[4 * px + 2 * py + pc]

        def copy(a, k, block, to, src=None):
            return pltpu.make_async_remote_copy(
                src_ref=slot(a, *block) if src is None else src, dst_ref=slot(a, *block),
                send_sem=send_sems.at[7 * a + k], recv_sem=recv_sems.at[7 * a + k],
                device_id=to, device_id_type=pl.DeviceIdType.MESH)

        mine = [pltpu.make_async_copy(x_refs[a], slot(a, *me), local_sems.at[a]) for a in range(n)]
        for cp in mine:
            cp.start()
        first = []
        for a in range(n):
            first.append(copy(a, 0, me, sibling, src=x_refs[a]))
            first += [copy(a, 1 + j, me, (*chip, c), src=x_refs[a]) for j, chip in enumerate(chips)]
        for cp in first:
            cp.start()
        passed = []
        for j, chip in enumerate(chips):
            for a in range(n):
                copy(a, 1 + j, (*chip, c), me).wait_recv()
                cp = copy(a, 4 + j, (*chip, c), sibling)
                cp.start()
                passed.append(cp)
        for a in range(n):
            copy(a, 0, sibling, me).wait_recv()
            for j, chip in enumerate(chips):
                copy(a, 4 + j, (*chip, 1 - c), me).wait_recv()
        for cp in first + passed:
            cp.wait_send()
        for cp in mine:
            cp.wait()

    anyspec = pl.BlockSpec(memory_space=pl.ANY)
    return pl.pallas_call(
        body, name=name, out_shape=[_sds((N_DEV,) + a.shape, a.dtype) for a in xs],
        in_specs=[anyspec] * n, out_specs=[anyspec] * n,
        scratch_shapes=[pltpu.SemaphoreType.DMA((7 * n,)), pltpu.SemaphoreType.DMA((7 * n,)),
                        pltpu.SemaphoreType.DMA((n,))],
    )(*xs)


def scatter_blocks(xs, name):
    n = len(xs)
    flips = [(0, 0, 1), (0, 1, 0), (0, 1, 1), (1, 0, 0), (1, 0, 1), (1, 1, 0), (1, 1, 1)]

    def body(*refs):
        x_refs, out_refs = refs[:n], refs[n:2 * n]
        send_sems, recv_sems, local_sems = refs[2 * n:]
        x, y, c = lax.axis_index("x"), lax.axis_index("y"), lax.axis_index("c")
        me = 4 * x + 2 * y + c

        def peer(f):
            return tuple(1 - v if d else v for v, d in zip((x, y, c), f))

        def lin(p):
            return 4 * p[0] + 2 * p[1] + p[2]

        mine = [pltpu.make_async_copy(x_refs[a].at[me], out_refs[a].at[me], local_sems.at[a]) for a in range(n)]
        for cp in mine:
            cp.start()
        copies = []
        for k, f in enumerate(flips):
            p = peer(f)
            for a in range(n):
                copies.append(pltpu.make_async_remote_copy(
                    src_ref=x_refs[a].at[lin(p)], dst_ref=out_refs[a].at[me],
                    send_sem=send_sems.at[7 * a + k], recv_sem=recv_sems.at[7 * a + k],
                    device_id=p, device_id_type=pl.DeviceIdType.MESH))
        for cp in copies:
            cp.start()
        for cp in copies:
            cp.wait_send()
            cp.wait_recv()
        for cp in mine:
            cp.wait()

    anyspec = pl.BlockSpec(memory_space=pl.ANY)
    return pl.pallas_call(
        body, name=name, out_shape=[_sds(a.shape, a.dtype) for a in xs],
        in_specs=[anyspec] * n, out_specs=[anyspec] * n,
        scratch_shapes=[pltpu.SemaphoreType.DMA((7 * n,)), pltpu.SemaphoreType.DMA((7 * n,)),
                        pltpu.SemaphoreType.DMA((n,))],
    )(*xs)


def _mod_spec(dm, nidx, tm=None):
    if nidx == 1:
        return pl.BlockSpec((1, 9, dm.D), lambda i: (_grp(i, dm, tm), 0, 0))
    return pl.BlockSpec((1, 9, dm.D), lambda i, k: (_grp(i, dm, tm), 0, 0))


def _wspec(shape5, l, s, ks):
    return pl.BlockSpec((ks, 1, 1) + tuple(shape5[3:]), lambda i, k: (k, l, s, 0, 0))


FFN_FWD_SHARDS = 4
FFN_BWD_SHARDS = 2


def ffn_fwd(X, MOD, gain, gwg, gwu, gwd, l, s, s0, dm, all_rows, name):
    D = dm.D
    tm = _ffn_tile(dm)
    rows = dm.T if all_rows else dm.Tx
    ks = FFN_FWD_SHARDS
    nk = N_DEV // ks

    def body(x_ref, m_ref, g_ref, wg_ref, wu_ref, wd_ref, xo_ref, y_ref, xn_s, acc_s):
        k = pl.program_id(1)

        @pl.when(k == 0)
        def _():
            m = m_ref[0]
            xn = _modulate(x_ref[...], g_ref[...], m[s0:s0 + 1], m[s0 + 1:s0 + 2])
            xn_s[...] = xn.astype(BF16)
            acc_s[...] = jnp.zeros_like(acc_s)

        xn = xn_s[...]
        y = None
        for j in range(0, ks, 2):
            wg2 = jnp.concatenate([wg_ref[j, 0, 0], wg_ref[j + 1, 0, 0]], axis=1)
            wu2 = jnp.concatenate([wu_ref[j, 0, 0], wu_ref[j + 1, 0, 0]], axis=1)
            wd2 = jnp.concatenate([wd_ref[j, 0, 0], wd_ref[j + 1, 0, 0]], axis=0)
            g = jnp.dot(xn, wg2, preferred_element_type=F32)
            u = jnp.dot(xn, wu2, preferred_element_type=F32)
            yj = jnp.dot((_silu(g) * u).astype(BF16), wd2, preferred_element_type=F32)
            y = yj if y is None else y + yj
        acc_s[...] += y

        @pl.when(k == nk - 1)
        def _():
            m = m_ref[0]
            y = acc_s[...]
            y_ref[...] = y
            xo_ref[...] = x_ref[...] + (0.5 * m[s0 + 2:s0 + 3]) * y

    row = pl.BlockSpec((tm, D), lambda i, k: (i, 0))
    return pl.pallas_call(
        body, name=name, grid=(rows // tm, nk),
        in_specs=[row, _mod_spec(dm, 2, tm), pl.BlockSpec((1, D), lambda i, k: (0, 0)),
                  _wspec(gwg.shape, l, s, ks), _wspec(gwu.shape, l, s, ks), _wspec(gwd.shape, l, s, ks)],
        out_specs=[row, row],
        out_shape=[_sds((rows, D), F32), _sds((rows, D), F32)],
        scratch_shapes=[pltpu.VMEM((tm, D), BF16), pltpu.VMEM((tm, D), F32)],
        compiler_params=_cp(),
    )(X, MOD, gain, gwg, gwu, gwd)


def ffn_bwd(X, dXo, Y, MOD, gain, gwg, gwu, gwd, l, s, s0, dm, all_rows, name):
    D = dm.D
    tm = _ffn_tile(dm)
    rows = dm.T if all_rows else dm.Tx
    ks = FFN_BWD_SHARDS
    nk = N_DEV // ks
    nf = gwg.shape[4]
    ngr = dm.G if all_rows else dm.Bl

    def body(x_ref, dxo_ref, y_ref, m_ref, g_ref, wg_ref, wu_ref, wd_ref,
             dxi_ref, xn_ref, do_ref, h_ref, dg_ref, du_ref, dm_ref, dgain_ref, xn_s, do_s, dxn_s):
        i, k = pl.program_id(0), pl.program_id(1)

        @pl.when(k == 0)
        def _():
            m = m_ref[0]
            xn = _modulate(x_ref[...], g_ref[...], m[s0:s0 + 1], m[s0 + 1:s0 + 2])
            xn_s[...] = xn.astype(BF16)
            do_s[...] = ((0.5 * m[s0 + 2:s0 + 3]) * dxo_ref[...]).astype(BF16)
            dxn_s[...] = jnp.zeros_like(dxn_s)

        xn, do = xn_s[...], do_s[...]
        dxn = None
        for p in range(ks // 2):
            j = 2 * p
            wg = jnp.concatenate([wg_ref[j, 0, 0], wg_ref[j + 1, 0, 0]], axis=1)
            wu = jnp.concatenate([wu_ref[j, 0, 0], wu_ref[j + 1, 0, 0]], axis=1)
            wd = jnp.concatenate([wd_ref[j, 0, 0], wd_ref[j + 1, 0, 0]], axis=0)
            g = jnp.dot(xn, wg, preferred_element_type=F32)
            u = jnp.dot(xn, wu, preferred_element_type=F32)
            sg = jax.nn.sigmoid(g)
            si = g * sg
            dh = _dot_nt(do, wd)
            dg = (dh * u * (sg * (1.0 + g * (1.0 - sg)))).astype(BF16)
            du = (dh * si).astype(BF16)
            dj = _dot_nt(dg, wg) + _dot_nt(du, wu)
            dxn = dj if dxn is None else dxn + dj
            h_ref[p] = (si * u).astype(BF16)
            dg_ref[p] = dg
            du_ref[p] = du
        dxn_s[...] += dxn

        @pl.when(k == nk - 1)
        def _():
            m = m_ref[0]
            _, vjp = jax.vjp(_modulate, x_ref[...], g_ref[...], m[s0:s0 + 1], m[s0 + 1:s0 + 2])
            dx, dgain, dshift, dscale = vjp(dxn_s[...])
            dxo = dxo_ref[...]
            dxi_ref[...] = dxo + dx
            xn_ref[...] = xn_s[...]
            do_ref[...] = do_s[...]
            dgate = jnp.sum(0.5 * dxo * y_ref[...], axis=0, keepdims=True)
            first = _first_of_group(i, dm, tm)
            _acc(dm_ref, (0, pl.ds(0, 1), slice(None)), dshift, first)
            _acc(dm_ref, (0, pl.ds(1, 1), slice(None)), dscale, first)
            _acc(dm_ref, (0, pl.ds(2, 1), slice(None)), dgate, first)
            _acc(dgain_ref, (slice(None), slice(None)), dgain, i == 0)

    row = pl.BlockSpec((tm, D), lambda i, k: (i, 0))
    slab = pl.BlockSpec((ks // 2, tm, 2 * nf), lambda i, k: (k, i, 0))
    pairs = _sds((N_DEV // 2, rows, 2 * nf), BF16)
    return pl.pallas_call(
        body, name=name, grid=(rows // tm, nk),
        in_specs=[row, row, row, _mod_spec(dm, 2, tm), pl.BlockSpec((1, D), lambda i, k: (0, 0)),
                  _wspec(gwg.shape, l, s, ks), _wspec(gwu.shape, l, s, ks), _wspec(gwd.shape, l, s, ks)],
        out_specs=[row, row, row, slab, slab, slab,
                   pl.BlockSpec((1, 3, D), lambda i, k: (_grp(i, dm, tm), 0, 0)),
                   pl.BlockSpec((1, D), lambda i, k: (0, 0))],
        out_shape=[_sds((rows, D), F32), _sds((rows, D), BF16), _sds((rows, D), BF16),
                   pairs, pairs, pairs,
                   _sds((ngr, 3, D), F32), _sds((1, D), F32)],
        scratch_shapes=[pltpu.VMEM((tm, D), BF16), pltpu.VMEM((tm, D), BF16), pltpu.VMEM((tm, D), F32)],
        compiler_params=_cp(),
    )(X, dXo, Y, MOD, gain, gwg, gwu, gwd)


def ffn_dw(XN, DO, H, DG, DU, dm, name):
    rows, D = XN.shape
    nf = H.shape[2] // 2
    tt = _contraction_tile(rows)
    nT = rows // tt

    def body(xn_ref, do_ref, h_ref, dg_ref, du_ref, dwg_ref, dwu_ref, dwd_ref, ag_s, au_s, ad_s):
        t = pl.program_id(1)

        @pl.when(t == 0)
        def _():
            ag_s[...] = jnp.zeros_like(ag_s)
            au_s[...] = jnp.zeros_like(au_s)
            ad_s[...] = jnp.zeros_like(ad_s)

        xn = xn_ref[...]
        ag_s[...] += _dot_tn(xn, dg_ref[0])
        au_s[...] += _dot_tn(xn, du_ref[0])
        ad_s[...] += _dot_tn(h_ref[0], do_ref[...])

        @pl.when(t == nT - 1)
        def _():
            for j in range(2):
                dwg_ref[j] = ag_s[:, pl.ds(j * nf, nf)].astype(BF16)
                dwu_ref[j] = au_s[:, pl.ds(j * nf, nf)].astype(BF16)
                dwd_ref[j] = ad_s[pl.ds(j * nf, nf), :].astype(BF16)

    row = pl.BlockSpec((tt, D), lambda k, t: (t, 0))
    slab = pl.BlockSpec((1, tt, 2 * nf), lambda k, t: (k, t, 0))
    return pl.pallas_call(
        body, name=name, grid=(N_DEV // 2, nT),
        in_specs=[row, row, slab, slab, slab],
        out_specs=[pl.BlockSpec((2, D, nf), lambda k, t: (k, 0, 0)), pl.BlockSpec((2, D, nf), lambda k, t: (k, 0, 0)),
                   pl.BlockSpec((2, nf, D), lambda k, t: (k, 0, 0))],
        out_shape=[_sds((N_DEV, D, nf), BF16), _sds((N_DEV, D, nf), BF16), _sds((N_DEV, nf, D), BF16)],
        scratch_shapes=[pltpu.VMEM((D, 2 * nf), F32), pltpu.VMEM((D, 2 * nf), F32), pltpu.VMEM((2 * nf, D), F32)],
        compiler_params=_cp(),
    )(XN, DO, H, DG, DU)


def atb(A, B, rows, dm, name):
    Ka, Nb = A.shape[1], B.shape[1]
    tk, tn = _pick(Ka, 1024), _pick(Nb, 1536)
    tt = _contraction_tile(rows)
    nT = rows // tt

    def body(a_ref, b_ref, o_ref, acc_s):
        t = pl.program_id(2)

        @pl.when(t == 0)
        def _():
            acc_s[...] = jnp.zeros_like(acc_s)

        acc_s[...] += _dot_tn(a_ref[...], b_ref[...])

        @pl.when(t == nT - 1)
        def _():
            o_ref[...] = acc_s[...].astype(BF16)

    return pl.pallas_call(
        body, name=name, grid=(Ka // tk, Nb // tn, nT),
        in_specs=[pl.BlockSpec((tt, tk), lambda i, j, t: (t, i)), pl.BlockSpec((tt, tn), lambda i, j, t: (t, j))],
        out_specs=pl.BlockSpec((tk, tn), lambda i, j, t: (i, j)),
        out_shape=_sds((Ka, Nb), BF16), scratch_shapes=[pltpu.VMEM((tk, tn), F32)], compiler_params=_cp(),
    )(A, B)


def modmm(X, MOD, gain, W, s0, dm, name):
    TM, D = dm.TM, dm.D
    Nc = W.shape[1]
    tn = _pick(Nc, 1536)
    nj = Nc // tn

    def body(x_ref, m_ref, g_ref, w_ref, p_ref, xn_ref):
        @pl.when(pl.program_id(1) == 0)
        def _():
            m = m_ref[0]
            xn_ref[...] = _modulate(x_ref[...], g_ref[...], m[s0:s0 + 1], m[s0 + 1:s0 + 2]).astype(BF16)

        p_ref[...] = jnp.dot(xn_ref[...], w_ref[...], preferred_element_type=F32)

    row = pl.BlockSpec((TM, D), lambda i, j: (i, 0))
    return pl.pallas_call(
        body, name=name, grid=(dm.nt, nj),
        in_specs=[row, _mod_spec(dm, 2), pl.BlockSpec((1, D), lambda i, j: (0, 0)),
                  pl.BlockSpec((D, tn), lambda i, j: (0, j))],
        out_specs=[pl.BlockSpec((TM, tn), lambda i, j: (i, j)), row],
        out_shape=[_sds((dm.T, Nc), F32), _sds((dm.T, D), BF16)],
        compiler_params=_cp(),
    )(X, MOD, gain, W)


def mixin_bwd(dP, W, X, dXres, MOD, gain, s0, dm, name):
    TM, D = dm.TM, dm.D
    K = dP.shape[1]

    def body(dp_ref, w_ref, x_ref, dr_ref, m_ref, g_ref, dx_ref, dm_ref, dgain_ref):
        i = pl.program_id(0)
        dxn = _dot_nt(dp_ref[...], w_ref[...])
        m = m_ref[0]
        _, vjp = jax.vjp(_modulate, x_ref[...], g_ref[...], m[s0:s0 + 1], m[s0 + 1:s0 + 2])
        dx, dgain, dshift, dscale = vjp(dxn)
        dx_ref[...] = dr_ref[...] + dx
        first = _first_of_group(i, dm)
        _acc(dm_ref, (0, pl.ds(0, 1), slice(None)), dshift, first)
        _acc(dm_ref, (0, pl.ds(1, 1), slice(None)), dscale, first)
        _acc(dgain_ref, (slice(None), slice(None)), dgain, i == 0)

    row = pl.BlockSpec((TM, D), lambda i: (i, 0))
    return pl.pallas_call(
        body, name=name, grid=(dm.nt,),
        in_specs=[pl.BlockSpec((TM, K), lambda i: (i, 0)), pl.BlockSpec((D, K), lambda i: (0, 0)), row, row,
                  _mod_spec(dm, 1), pl.BlockSpec((1, D), lambda i: (0, 0))],
        out_specs=[row, pl.BlockSpec((1, 2, D), lambda i: (_grp(i, dm), 0, 0)), pl.BlockSpec((1, D), lambda i: (0, 0))],
        out_shape=[_sds((dm.T, D), F32), _sds((dm.G, 2, D), F32), _sds((1, D), F32)],
        compiler_params=_cp(),
    )(dP, W, X, dXres, MOD, gain)


def proj_res(As, Ws, X, MOD, dm, ntiles, name):
    TM, D = dm.TM, dm.D
    n = len(As)
    rows = ntiles * TM

    def body(*refs):
        a_refs, w_refs = refs[:n], refs[n:2 * n]
        x_ref, m_ref, xo_ref, y_ref = refs[2 * n:]
        y = jnp.dot(a_refs[0][...], w_refs[0][...], preferred_element_type=F32)
        for a, w in zip(a_refs[1:], w_refs[1:]):
            y += jnp.dot(a[...], w[...], preferred_element_type=F32)
        y_ref[...] = y
        xo_ref[...] = x_ref[...] + m_ref[0][5:6] * y

    row = pl.BlockSpec((TM, D), lambda i: (i, 0))
    return pl.pallas_call(
        body, name=name, grid=(ntiles,),
        in_specs=[pl.BlockSpec((TM, a.shape[1]), lambda i: (i, 0)) for a in As]
        + [pl.BlockSpec(w.shape, lambda i: (0, 0)) for w in Ws] + [row, _mod_spec(dm, 1)],
        out_specs=[row, row], out_shape=[_sds((rows, D), F32), _sds((rows, D), F32)],
        compiler_params=_cp(),
    )(*As, *Ws, X, MOD)


def proj_res_bwd(dXo, Y, MOD, Ws, dm, ntiles, name):
    TM, D = dm.TM, dm.D
    n = len(Ws)
    rows = ntiles * TM
    ngr = dm.G if ntiles == dm.nt else dm.Bl

    def body(*refs):
        dxo_ref, y_ref, m_ref = refs[:3]
        w_refs = refs[3:3 + n]
        dy_ref = refs[3 + n]
        da_refs = refs[4 + n:4 + 2 * n]
        dgate_ref = refs[4 + 2 * n]
        i = pl.program_id(0)
        dxo = dxo_ref[...]
        dy = (m_ref[0][5:6] * dxo).astype(BF16)
        dy_ref[...] = dy
        for w, da in zip(w_refs, da_refs):
            da[...] = _dot_nt(dy, w[...])
        dgate = jnp.sum(dxo * y_ref[...], axis=0, keepdims=True)
        _acc(dgate_ref, (0, slice(None), slice(None)), dgate, _first_of_group(i, dm))

    row = pl.BlockSpec((TM, D), lambda i: (i, 0))
    return pl.pallas_call(
        body, name=name, grid=(ntiles,),
        in_specs=[row, row, _mod_spec(dm, 1)] + [pl.BlockSpec(w.shape, lambda i: (0, 0)) for w in Ws],
        out_specs=[row] + [pl.BlockSpec((TM, w.shape[0]), lambda i: (i, 0)) for w in Ws]
        + [pl.BlockSpec((1, 1, D), lambda i: (_grp(i, dm), 0, 0))],
        out_shape=[_sds((rows, D), BF16)] + [_sds((rows, w.shape[0]), F32) for w in Ws] + [_sds((ngr, 1, D), F32)],
        compiler_params=_cp(),
    )(dXo, Y, MOD, *Ws)


def loss_head(Xf, target, dm, name):
    TM, D = dm.TM, dm.D

    def body(x_ref, t_ref, l_ref, dx_ref, acc_s):
        i = pl.program_id(0)
        e = x_ref[...] - t_ref[...]
        dx_ref[...] = e * (1.0 / D)

        @pl.when(i == 0)
        def _():
            acc_s[...] = jnp.zeros_like(acc_s)

        acc_s[...] += jnp.sum(e * e, axis=0, keepdims=True)

        @pl.when(i == dm.ntx - 1)
        def _():
            tot = jnp.sum(acc_s[...], axis=1, keepdims=True) * (0.5 / D)
            l_ref[...] = jnp.broadcast_to(tot, (1, LANE))

    row = pl.BlockSpec((TM, D), lambda i: (i, 0))
    return pl.pallas_call(
        body, name=name, grid=(dm.ntx,), in_specs=[row, row],
        out_specs=[pl.BlockSpec((1, LANE), lambda i: (0, 0)), row],
        out_shape=[_sds((1, LANE), F32), _sds((dm.Tx, D), F32)],
        scratch_shapes=[pltpu.VMEM((1, D), F32)], compiler_params=_cp(),
    )(Xf, target)


def _qk_fn(p, gain, cs, sneg, spos):
    y = p * lax.rsqrt(jnp.mean(p * p, axis=-1, keepdims=True) + EPS) * gain
    return _rope(y, cs, sneg, spos)


def _tab_specs(dm, swap):
    def idx(i):
        return jnp.where(i < dm.ntx, i % dm.tps, dm.tps)
    if swap:
        return [pl.BlockSpec((dm.TM, HEAD), lambda j, i: (idx(i), 0))] * 3
    return [pl.BlockSpec((dm.TM, HEAD), lambda i, j: (idx(i), 0))] * 3


def qkv_prep(P0, qkg, tabs, dm, name):
    TM = dm.TM

    def body(p_ref, g_ref, cs_ref, sn_ref, sp_ref, o_ref):
        j = pl.program_id(1)

        @pl.when(j < 6)
        def _():
            o_ref[...] = _qk_fn(p_ref[...], g_ref[0], cs_ref[...], sn_ref[...], sp_ref[...]).astype(BF16)

        @pl.when(j >= 6)
        def _():
            o_ref[...] = p_ref[...].astype(BF16)

    blk = pl.BlockSpec((TM, HEAD), lambda i, j: (i, j))
    return pl.pallas_call(
        body, name=name, grid=(dm.nt, 8),
        in_specs=[blk, pl.BlockSpec((1, 1, HEAD), lambda i, j: (jnp.minimum(j // 4, 1), 0, 0))] + _tab_specs(dm, False),
        out_specs=blk, out_shape=_sds((dm.T, 8 * HEAD), BF16), compiler_params=_cp(),
    )(P0, qkg, *tabs)


def qkv_prep_bwd(P0, dQKV, qkg, tabs, dm, name):
    TM = dm.TM

    def body(p_ref, d_ref, g_ref, cs_ref, sn_ref, sp_ref, dp_ref, dg_ref):
        j, i = pl.program_id(0), pl.program_id(1)
        first = (i == 0) & ((j == 0) | (j == 4))

        @pl.when(j < 6)
        def _():
            _, vjp = jax.vjp(_qk_fn, p_ref[...], g_ref[0], cs_ref[...], sn_ref[...], sp_ref[...])
            dp, dg = vjp(d_ref[...])[:2]
            dp_ref[...] = dp
            _acc(dg_ref, (0, slice(None), slice(None)), dg, first)

        @pl.when(j >= 6)
        def _():
            dp_ref[...] = d_ref[...]

    blk = pl.BlockSpec((TM, HEAD), lambda j, i: (i, j))
    return pl.pallas_call(
        body, name=name, grid=(8, dm.nt),
        in_specs=[blk, blk, pl.BlockSpec((1, 1, HEAD), lambda j, i: (jnp.minimum(j // 4, 1), 0, 0))] + _tab_specs(dm, True),
        out_specs=[blk, pl.BlockSpec((1, 1, HEAD), lambda j, i: (jnp.minimum(j // 4, 1), 0, 0))],
        out_shape=[_sds((dm.T, 8 * HEAD), F32), _sds((2, 1, HEAD), F32)], compiler_params=_cp(),
    )(P0, dQKV, qkg, *tabs)


def _softmax2(sx, sh):
    m = jnp.max(sh, axis=-1, keepdims=True)
    if sx is not None:
        m = jnp.maximum(m, jnp.max(sx, axis=-1, keepdims=True))
    eh = jnp.exp(sh - m)
    l = jnp.sum(eh, axis=-1, keepdims=True)
    ex = None
    if sx is not None:
        ex = jnp.exp(sx - m)
        l = l + jnp.sum(ex, axis=-1, keepdims=True)
    inv = 1.0 / l
    return (None if ex is None else ex * inv), eh * inv


def _attn_geometry(dm, with_x):
    TQ = dm.TM
    if with_x:
        nq, qoff = dm.N // TQ, 0
    else:
        nq, qoff = dm.M // TQ, dm.Tx // TQ
    hoff = dm.Tx // dm.M
    return TQ, nq, qoff, hoff


def attn_fwd(QKV, dm, with_x, name):
    TQ, nq, qoff, hoff = _attn_geometry(dm, with_x)
    scale = HEAD ** -0.5
    rows = dm.Tx if with_x else dm.Th

    def body(*refs):
        if with_x:
            q_ref, kh_ref, vh_ref, kx_ref, vx_ref, o_ref = refs
        else:
            q_ref, kh_ref, vh_ref, o_ref = refs
        q = q_ref[...]
        sh = _dot_nt(q, kh_ref[...]) * scale
        sx = _dot_nt(q, kx_ref[...]) * scale if with_x else None
        px, ph = _softmax2(sx, sh)
        o = _dot(ph, vh_ref[...])
        if with_x:
            o = o + _dot(px, vx_ref[...])
        o_ref[...] = o.astype(BF16)

    qs = pl.BlockSpec((TQ, HEAD), lambda b, kv, g, qi: (qoff + b * nq + qi, kv * 2 + g))
    in_specs = [qs, pl.BlockSpec((dm.M, HEAD), lambda b, kv, g, qi: (hoff + b, 4 + kv)),
                pl.BlockSpec((dm.M, HEAD), lambda b, kv, g, qi: (hoff + b, 6 + kv))]
    args = [QKV, QKV, QKV]
    if with_x:
        in_specs += [pl.BlockSpec((dm.N, HEAD), lambda b, kv, g, qi: (b, 4 + kv)),
                     pl.BlockSpec((dm.N, HEAD), lambda b, kv, g, qi: (b, 6 + kv))]
        args += [QKV, QKV]
    return pl.pallas_call(
        body, name=name, grid=(dm.Bl, A_KV, 2, nq), in_specs=in_specs,
        out_specs=pl.BlockSpec((TQ, HEAD), lambda b, kv, g, qi: (b * nq + qi, kv * 2 + g)),
        out_shape=_sds((rows, A_HEADS * HEAD), BF16), compiler_params=_cp(),
    )(*args)


def attn_bwd(QKV, dO, dm, with_x, init, name):
    TQ, nq, qoff, hoff = _attn_geometry(dm, with_x)
    scale = HEAD ** -0.5
    rows = dm.Tx if with_x else dm.Th

    def body(*refs):
        if with_x:
            (q_ref, kh_ref, vh_ref, kx_ref, vx_ref, do_ref, ikh_ref, ivh_ref,
             dq_ref, dkh_ref, dvh_ref, dkx_ref, dvx_ref) = refs
        else:
            q_ref, kh_ref, vh_ref, do_ref, dq_ref, dkh_ref, dvh_ref = refs
        g, qi = pl.program_id(2), pl.program_id(3)
        q = q_ref[...]
        kh, vh = kh_ref[...], vh_ref[...]
        sh = _dot_nt(q, kh) * scale
        sx = _dot_nt(q, kx_ref[...]) * scale if with_x else None
        px, ph = _softmax2(sx, sh)
        dob = do_ref[...].astype(BF16)
        dph = _dot_nt(dob, vh)
        delta = jnp.sum(dph * ph, axis=-1, keepdims=True)
        if with_x:
            dpx = _dot_nt(dob, vx_ref[...])
            delta = delta + jnp.sum(dpx * px, axis=-1, keepdims=True)
        dsh = (ph * (dph - delta) * scale).astype(BF16)
        dq = _dot(dsh, kh)
        first = (g == 0) & (qi == 0)

        @pl.when(first)
        def _():
            if with_x:
                dkh_ref[...] = ikh_ref[...]
                dvh_ref[...] = ivh_ref[...]
                dkx_ref[...] = jnp.zeros_like(dkx_ref)
                dvx_ref[...] = jnp.zeros_like(dvx_ref)
            else:
                dkh_ref[...] = jnp.zeros_like(dkh_ref)
                dvh_ref[...] = jnp.zeros_like(dvh_ref)

        dkh_ref[...] += _dot_tn(dsh, q)
        dvh_ref[...] += _dot_tn(ph, dob)
        if with_x:
            dsx = (px * (dpx - delta) * scale).astype(BF16)
            dq = dq + _dot(dsx, kx_ref[...])
            dkx_ref[...] += _dot_tn(dsx, q)
            dvx_ref[...] += _dot_tn(px, dob)
        dq_ref[...] = dq

    qs = pl.BlockSpec((TQ, HEAD), lambda b, kv, g, qi: (qoff + b * nq + qi, kv * 2 + g))
    hs = lambda c0: pl.BlockSpec((dm.M, HEAD), lambda b, kv, g, qi: (hoff + b, c0 + kv))
    xs = lambda c0: pl.BlockSpec((dm.N, HEAD), lambda b, kv, g, qi: (b, c0 + kv))
    dos = pl.BlockSpec((TQ, HEAD), lambda b, kv, g, qi: (b * nq + qi, kv * 2 + g))
    acc_h = pl.BlockSpec((dm.M, HEAD), lambda b, kv, g, qi: (b, kv))
    acc_x = pl.BlockSpec((dm.N, HEAD), lambda b, kv, g, qi: (b, kv))
    in_specs, args = [qs, hs(4), hs(6)], [QKV, QKV, QKV]
    out_specs = [dos, acc_h, acc_h]
    out_shape = [_sds((rows, A_HEADS * HEAD), F32), _sds((dm.Th, A_KV * HEAD), F32), _sds((dm.Th, A_KV * HEAD), F32)]
    if with_x:
        in_specs += [xs(4), xs(6), dos, acc_h, acc_h]
        args += [QKV, QKV, dO, init[0], init[1]]
        out_specs += [acc_x, acc_x]
        out_shape += [_sds((dm.Tx, A_KV * HEAD), F32), _sds((dm.Tx, A_KV * HEAD), F32)]
    else:
        in_specs += [dos]
        args += [dO]
    return pl.pallas_call(
        body, name=name, grid=(dm.Bl, A_KV, 2, nq), in_specs=in_specs, out_specs=out_specs,
        out_shape=out_shape, compiler_params=_cp(),
    )(*args)


def _pool_mean(u, w):
    n = u.shape[0]
    t = lax.broadcasted_iota(jnp.int32, (n, 1), 0)
    cnt = (jnp.clip(t + (w - w // 2), 0, n) - jnp.clip(t - w // 2, 0, n)).astype(F32)
    s = _shift_rows(u, -(w // 2))
    for j in range(-(w // 2) + 1, w - w // 2):
        s = s + _shift_rows(u, j)
    return s / cnt - u


def pool_fwd(P0, pw, pscale, dm, on_x, name):
    n, off, rows = (dm.N, 0, dm.Tx) if on_x else (dm.M, dm.Tx // dm.M, dm.Th)
    ng = len(POOL_WINDOWS)

    def body(u_ref, w_ref, s_ref, o_ref):
        for g, w in enumerate(POOL_WINDOWS):
            cols = pl.ds(g * HEAD, HEAD)
            pooled = _pool_mean(u_ref[:, cols], w)
            o_ref[:, cols] = (_dot(pooled, w_ref[g]) * s_ref[:, cols]).astype(BF16)

    return pl.pallas_call(
        body, name=name, grid=(dm.Bl,),
        in_specs=[pl.BlockSpec((n, ng * HEAD), lambda b: (off + b, 2)),
                  pl.BlockSpec((ng, HEAD, HEAD), lambda b: (0, 0, 0)), pl.BlockSpec((1, ng * HEAD), lambda b: (0, 0))],
        out_specs=pl.BlockSpec((n, ng * HEAD), lambda b: (b, 0)),
        out_shape=_sds((rows, ng * HEAD), BF16), compiler_params=_cp(),
    )(P0, pw, pscale)


def pool_bwd(P0, dY, pw, pwT, pscale, dm, on_x, name):
    n, off, rows = (dm.N, 0, dm.Tx) if on_x else (dm.M, dm.Tx // dm.M, dm.Th)
    ng = len(POOL_WINDOWS)

    def body(u_ref, dy_ref, w_ref, wt_ref, s_ref, du_ref, dw_ref, ds_ref):
        b = pl.program_id(0)
        for g, w in enumerate(POOL_WINDOWS):
            cols = pl.ds(g * HEAD, HEAD)
            pooled, vjp = jax.vjp(lambda u: _pool_mean(u, w), u_ref[:, cols])
            pre = _dot(pooled, w_ref[g])
            dy = dy_ref[:, cols]
            dpre = dy * s_ref[:, cols]
            du_ref[:, cols] = vjp(_dot(dpre, wt_ref[g]))[0]
            _acc(dw_ref, (g, slice(None), slice(None)), _dot_tn(pooled, dpre), b == 0)
            _acc(ds_ref, (slice(None), cols), jnp.sum(dy * pre, axis=0, keepdims=True), b == 0)

    full = pl.BlockSpec((ng, HEAD, HEAD), lambda b: (0, 0, 0))
    vec = pl.BlockSpec((1, ng * HEAD), lambda b: (0, 0))
    return pl.pallas_call(
        body, name=name, grid=(dm.Bl,),
        in_specs=[pl.BlockSpec((n, ng * HEAD), lambda b: (off + b, 2)), pl.BlockSpec((n, ng * HEAD), lambda b: (b, 0)),
                  full, full, vec],
        out_specs=[pl.BlockSpec((n, ng * HEAD), lambda b: (b, 0)), full, vec],
        out_shape=[_sds((rows, ng * HEAD), F32), _sds((ng, HEAD, HEAD), F32), _sds((1, ng * HEAD), F32)],
        compiler_params=_cp(),
    )(P0, dY, pw, pwT, pscale)


def _conv_fn(p, w0, w1, w2, kind):
    c = w0 * _shift_rows(p, -1) + w1 * p + w2 * _shift_rows(p, 1)
    a = _silu(c)
    if kind == 2:
        return a
    a = a * lax.rsqrt(jnp.sum(a * a, axis=-1, keepdims=True) + EPS)
    return a * (HEAD ** -0.5) if kind == 0 else a


def gdn_prep(P1, conv_w, dm, on_x, name):
    n, off, rows = (dm.N, 0, dm.Tx) if on_x else (dm.M, dm.Tx // dm.M, dm.Th)

    def body(p_ref, w_ref, o_ref):
        j = pl.program_id(1)
        p, w = p_ref[...], w_ref[...]
        for kind in range(3):
            @pl.when(j // C_HEADS == kind)
            def _():
                o_ref[...] = _conv_fn(p, w[0:1], w[1:2], w[2:3], kind)

    return pl.pallas_call(
        body, name=name, grid=(dm.Bl, 3 * C_HEADS),
        in_specs=[pl.BlockSpec((n, HEAD), lambda b, j: (off + b, j)), pl.BlockSpec((3, HEAD), lambda b, j: (0, j))],
        out_specs=pl.BlockSpec((n, HEAD), lambda b, j: (b, j)),
        out_shape=_sds((rows, 3 * C_HEADS * HEAD), F32), compiler_params=_cp(),
    )(P1, conv_w)


def gdn_prep_bwd(P1, dQ, conv_w, dm, on_x, name):
    n, off, rows = (dm.N, 0, dm.Tx) if on_x else (dm.M, dm.Tx // dm.M, dm.Th)

    def body(p_ref, d0_ref, d1_ref, w_ref, dp_ref, dw_ref):
        j, b = pl.program_id(0), pl.program_id(1)
        p, w = p_ref[...], w_ref[...]
        for kind in range(3):
            @pl.when(j // C_HEADS == kind)
            def _():
                _, vjp = jax.vjp(functools.partial(_conv_fn, kind=kind), p, w[0:1], w[1:2], w[2:3])
                dp, d0, d1, d2 = vjp(d0_ref[0] + d1_ref[0])
                dp_ref[...] = dp
                _acc(dw_ref, (pl.ds(0, 1), slice(None)), d0, b == 0)
                _acc(dw_ref, (pl.ds(1, 1), slice(None)), d1, b == 0)
                _acc(dw_ref, (pl.ds(2, 1), slice(None)), d2, b == 0)

    return pl.pallas_call(
        body, name=name, grid=(3 * C_HEADS, dm.Bl),
        in_specs=[pl.BlockSpec((n, HEAD), lambda j, b: (off + b, j)),
                  pl.BlockSpec((1, n, HEAD), lambda j, b: (0, off + b, j)), pl.BlockSpec((1, n, HEAD), lambda j, b: (1, off + b, j)),
                  pl.BlockSpec((3, HEAD), lambda j, b: (0, j))],
        out_specs=[pl.BlockSpec((n, HEAD), lambda j, b: (b, j)), pl.BlockSpec((3, HEAD), lambda j, b: (0, j))],
        out_shape=[_sds((rows, 3 * C_HEADS * HEAD), F32), _sds((3, 3 * C_HEADS * HEAD), F32)],
        compiler_params=_cp(),
    )(P1, dQ, dQ, conv_w)


def _gate_fn(ab, par):
    lane = lax.broadcasted_iota(jnp.int32, ab.shape, 1)
    is_a = (lane % 16) < C_HEADS
    g = -jnp.exp(par[0:1]) * jax.nn.softplus(ab + par[1:2])
    return jnp.where(lane < 4 * C_HEADS, jnp.where(is_a, g, jax.nn.sigmoid(ab)), 0.0)


def _col(blk, idx):
    lane = lax.broadcasted_iota(jnp.int32, blk.shape, 1)
    return jnp.sum(jnp.where(lane == idx, blk, 0.0), axis=1, keepdims=True)


def _chunk_masks(rev):
    ii = lax.broadcasted_iota(jnp.int32, (CHUNK, CHUNK), 0)
    jj = lax.broadcasted_iota(jnp.int32, (CHUNK, CHUNK), 1)
    ahead = jnp.where(rev, jj - ii, ii - jj)
    return ahead >= 0, ahead > 0, (ii == jj).astype(F32)


def _inv_unit_tri(nmats, eye):
    xs = [eye - n for n in nmats]
    ps = [_hdot(n, n) for n in nmats]
    step = 2
    while True:
        xs = [x + _hdot(x, p) for x, p in zip(xs, ps)]
        step *= 2
        if step >= CHUNK:
            break
        ps = [_hdot(p, p) for p in ps]
    return xs


def _cum_lanes(x, transpose=False):
    lane = lax.broadcasted_iota(jnp.int32, x.shape, 1)
    down, up = x, x
    s = 1
    while s < CHUNK:
        down = down + _shift_rows(down, -s)
        up = up + _shift_rows(up, s)
        s *= 2
    return jnp.where((lane >= 16) if transpose else (lane < 16), down, up)


def _each(f, *lists):
    return [f(*a) for a in zip(*lists)]


def _chunk_common(qs, ks, vs, gcs, gcrs, tots, betas, rev, saved=None):
    incl, strict, eye = _chunk_masks(rev)
    es = _each(lambda gc, gcr: jnp.exp(jnp.where(incl, gc - gcr, NEG)), gcs, gcrs)
    egs = [jnp.exp(gc) for gc in gcs]
    ets = _each(lambda t, gc: jnp.exp(t - gc), tots, gcs)
    gts = [jnp.exp(t) for t in tots]
    kbs = _each(lambda k, b: k * b, ks, betas)
    kks = _each(_dot_nt, kbs, ks)
    qqs = _each(_dot_nt, qs, ks)
    if saved is None:
        nmats = _each(lambda kk, e: jnp.where(strict, kk * e, 0.0), kks, es)
        ainvs = _inv_unit_tri(nmats, eye)
        rhss = _each(lambda v, b, kb, eg: jnp.concatenate([v * b, kb * eg], axis=1), vs, betas, kbs, egs)
        sols = _each(_hdot, ainvs, rhss)
    else:
        ainvs, sols = saved
    return dict(incl=incl, strict=strict, e=es, eg=egs, et=ets, gt=gts, kb=kbs, kk=kks, ainv=ainvs, sol=sols, qq=qqs)


def _chunk_fwd(qs, ks, vs, gcs, gcrs, tots, betas, rev):
    c = _chunk_common(qs, ks, vs, gcs, gcrs, tots, betas, rev)
    incl = c["incl"]
    return _each(lambda q, k, sol, qq, e, et, eg, gt, ainv:
                 (sol[:, :HEAD], sol[:, HEAD:], k * et, q * eg, jnp.where(incl, qq * e, 0.0), gt, ainv),
                 qs, ks, c["sol"], c["qq"], c["e"], c["et"], c["eg"], c["gt"], c["ainv"])


def _chunk_bwd(qs, ks, vs, gcs, gcrs, tots, betas, rev, ainvs, sols, dus, dws, dkts, dqds, dqks, dgts):
    c = _chunk_common(qs, ks, vs, gcs, gcrs, tots, betas, rev, saved=(ainvs, sols))
    incl, strict = c["incl"], c["strict"]
    drhss = _each(lambda a, du, dw: _hdot_tn(a, jnp.concatenate([du, dw], axis=1)), c["ainv"], dus, dws)
    dns = _each(lambda drhs, sol: jnp.where(strict, -_hdot_nt(drhs, sol), 0.0), drhss, c["sol"])
    dkks = _each(lambda dn, e: dn * e, dns, c["e"])
    dqms = [jnp.where(incl, dqk, 0.0) for dqk in dqks]
    dqqs = _each(lambda dqm, e: dqm * e, dqms, c["e"])
    m_q = _each(_dot, dqqs, ks)
    m_k1 = _each(_dot_tn, dqqs, qs)
    m_k2 = _each(_dot_tn, dkks, c["kb"])
    m_kb = _each(_dot, dkks, ks)

    def finish(q, k, v, beta, e, eg, et, gt, kb, kk, qq, drhs, dn, dqm, dkt, dqd, dgt, mq, mk1, mk2, mkb):
        de = dn * kk + dqm * qq
        dq = mq + dqd * eg
        dkb = mkb + drhs[:, HEAD:] * eg
        dk = mk1 + mk2 + dkt * et + dkb * beta
        dv = drhs[:, :HEAD] * beta
        dbeta = jnp.sum(drhs[:, :HEAD] * v + dkb * k, axis=1, keepdims=True)
        deg = jnp.sum(drhs[:, HEAD:] * kb + dqd * q, axis=1, keepdims=True)
        dd = de * e
        dtd = jnp.sum(dkt * k, axis=1, keepdims=True) * et
        dgc = deg * eg - dtd + jnp.sum(dd, axis=1, keepdims=True) - jnp.sum(dd.T, axis=1, keepdims=True)
        dtot = jnp.sum(dtd, axis=0, keepdims=True) + dgt * gt
        return dq, dk, dv, dgc, dtot, dbeta

    return _each(finish, qs, ks, vs, betas, c["e"], c["eg"], c["et"], c["gt"], c["kb"], c["kk"], c["qq"],
                 drhss, dns, dqms, dkts, dqds, dgts, m_q, m_k1, m_k2, m_kb)


def gdn_chunk_pre(QKVg, P1, par, dm, name):
    nch = dm.T // CHUNK
    HD = C_HEADS * HEAD
    abcol = (4 * HD) // LANE

    def body(x_ref, ab_ref, par_ref, u_ref, w_ref, kt_ref, qd_ref, qk_ref, gt_ref, wf_ref, ai_ref, gct_s):
        d = pl.program_id(1)
        rev = d == 1
        gb = _gate_fn(ab_ref[...], par_ref[...])
        gcl = _cum_lanes(gb)
        gct_s[...] = gcl.T
        tot = jnp.sum(gb, axis=0, keepdims=True)
        hs = range(C_HEADS)
        outs = _chunk_fwd(
            [x_ref[:, pl.ds(h * HEAD, HEAD)] for h in hs],
            [x_ref[:, pl.ds((C_HEADS + h) * HEAD, HEAD)] for h in hs],
            [x_ref[:, pl.ds((2 * C_HEADS + h) * HEAD, HEAD)] for h in hs],
            [_col(gcl, d * 16 + h) for h in hs], [gct_s[pl.ds(d * 16 + h, 1), :] for h in hs],
            [_col(tot, d * 16 + h) for h in hs], [_col(gb, d * 16 + 8 + h) for h in hs], rev)
        for h, (u, w, kt, qd, qk, gt, ainv) in enumerate(outs):
            cols = pl.ds(h * HEAD, HEAD)
            u_ref[0, :, cols] = u
            w_ref[0, :, cols] = w.astype(BF16)
            kt_ref[0, :, cols] = kt.astype(BF16)
            qd_ref[0, :, cols] = qd.astype(BF16)
            qk_ref[0, :, cols] = jnp.concatenate([qk, jnp.zeros_like(qk)], axis=1).astype(BF16)
            gt_ref[0, 0, pl.ds(h, 1), :] = jnp.broadcast_to(gt, (1, HEAD))
            wf_ref[0, :, cols] = w
            ai_ref[0, :, cols] = jnp.concatenate([ainv, jnp.zeros_like(ainv)], axis=1)

    big = pl.BlockSpec((1, CHUNK, HD), lambda i, d: (d, i, 0))
    return pl.pallas_call(
        body, name=name, grid=(nch, 2),
        in_specs=[pl.BlockSpec((CHUNK, 3 * HD), lambda i, d: (i, 0)), pl.BlockSpec((CHUNK, LANE), lambda i, d: (i, abcol)),
                  pl.BlockSpec((2, LANE), lambda i, d: (0, 0))],
        out_specs=[big, big, big, big, big, pl.BlockSpec((1, 1, C_HEADS, HEAD), lambda i, d: (d, i, 0, 0)), big, big],
        out_shape=[_sds((2, dm.T, HD), F32), _sds((2, dm.T, HD), BF16), _sds((2, dm.T, HD), BF16),
                   _sds((2, dm.T, HD), BF16), _sds((2, dm.T, HD), BF16), _sds((2, nch, C_HEADS, HEAD), F32),
                   _sds((2, dm.T, HD), F32), _sds((2, dm.T, HD), F32)],
        scratch_shapes=[pltpu.VMEM((LANE, CHUNK), F32)], compiler_params=_cp(),
    )(QKVg, P1, par)


def gdn_chunk_pre_bwd(QKVg, P1, par, U, WF, AI, dU, dW, dKT, dQD, dQK, dGT, dm, name):
    nch = dm.T // CHUNK
    HD = C_HEADS * HEAD
    abcol = (4 * HD) // LANE

    def body(x_ref, ab_ref, par_ref, u_ref, wf_ref, ai_ref, du_ref, dw_ref, dkt_ref, dqd_ref, dqk_ref, dgt_ref,
             dx_ref, dab_ref, dpar_ref, gct_s):
        i, d = pl.program_id(0), pl.program_id(1)
        rev = d == 1
        ab, par = ab_ref[...], par_ref[...]
        gb, gate_vjp = jax.vjp(_gate_fn, ab, par)
        gcl = _cum_lanes(gb)
        gct_s[...] = gcl.T
        tot = jnp.sum(gb, axis=0, keepdims=True)
        lane = lax.broadcasted_iota(jnp.int32, (CHUNK, LANE), 1)
        dgcl = jnp.zeros((CHUNK, LANE), F32)
        dgb = jnp.zeros((CHUNK, LANE), F32)
        first = d == 0
        hs = range(C_HEADS)
        hcols = [pl.ds(h * HEAD, HEAD) for h in hs]
        outs = _chunk_bwd(
            [x_ref[:, c] for c in hcols],
            [x_ref[:, pl.ds((C_HEADS + h) * HEAD, HEAD)] for h in hs],
            [x_ref[:, pl.ds((2 * C_HEADS + h) * HEAD, HEAD)] for h in hs],
            [_col(gcl, d * 16 + h) for h in hs], [gct_s[pl.ds(d * 16 + h, 1), :] for h in hs],
            [_col(tot, d * 16 + h) for h in hs], [_col(gb, d * 16 + 8 + h) for h in hs], rev,
            [ai_ref[0, :, pl.ds(h * HEAD, CHUNK)] for h in hs],
            [jnp.concatenate([u_ref[0, :, c], wf_ref[0, :, c]], axis=1) for c in hcols],
            [du_ref[0, :, c] for c in hcols], [dw_ref[0, :, c] for c in hcols], [dkt_ref[0, :, c] for c in hcols],
            [dqd_ref[0, :, c] for c in hcols], [dqk_ref[0, :, pl.ds(h * HEAD, CHUNK)] for h in hs],
            [dgt_ref[0, 0, pl.ds(h, 1), pl.ds(0, 1)] for h in hs])
        for h, (dq, dk, dv, dgc, dtotal, dbeta) in enumerate(outs):
            idx = d * 16 + h
            dx_ref[0, :, hcols[h]] = dq
            dx_ref[0, :, pl.ds((C_HEADS + h) * HEAD, HEAD)] = dk
            dx_ref[0, :, pl.ds((2 * C_HEADS + h) * HEAD, HEAD)] = dv
            dgcl = dgcl + jnp.where(lane == idx, dgc, 0.0)
            dgb = dgb + jnp.where(lane == idx + 8, dbeta, 0.0) + jnp.where(lane == idx, dtotal, 0.0)
        dab, dpar = gate_vjp(dgb + _cum_lanes(dgcl, transpose=True))
        dab_ref[0] = dab
        _acc(dpar_ref, (slice(None), slice(None)), dpar, (i == 0) & first)

    big = pl.BlockSpec((1, CHUNK, HD), lambda i, d: (d, i, 0))
    return pl.pallas_call(
        body, name=name, grid=(nch, 2),
        in_specs=[pl.BlockSpec((CHUNK, 3 * HD), lambda i, d: (i, 0)), pl.BlockSpec((CHUNK, LANE), lambda i, d: (i, abcol)),
                  pl.BlockSpec((2, LANE), lambda i, d: (0, 0)), big, big, big, big, big, big, big, big,
                  pl.BlockSpec((1, 1, C_HEADS, HEAD), lambda i, d: (d, i, 0, 0))],
        out_specs=[pl.BlockSpec((1, CHUNK, 3 * HD), lambda i, d: (d, i, 0)), pl.BlockSpec((1, CHUNK, LANE), lambda i, d: (d, i, 0)),
                   pl.BlockSpec((2, LANE), lambda i, d: (0, 0))],
        out_shape=[_sds((2, dm.T, 3 * HD), F32), _sds((2, dm.T, LANE), F32), _sds((2, LANE), F32)],
        scratch_shapes=[pltpu.VMEM((LANE, CHUNK), F32)], compiler_params=_cp(),
    )(QKVg, P1, par, U, WF, AI, dU, dW, dKT, dQD, dQK, dGT)


def _scan_chunk(b, d, c, dm):
    nh, nx = dm.M // CHUNK, dm.N // CHUNK
    in_h = c < nh
    pos_h = jnp.where(d == 0, c, nh - 1 - c)
    pos_x = jnp.where(d == 0, c - nh, nx - 1 - (c - nh))
    return jnp.where(in_h, dm.Tx // CHUNK + b * nh + pos_h, b * nx + pos_x)


def gdn_scan_fwd(U, W, KT, QD, QK, GT, dm, name):
    nch = dm.T // CHUNK
    HD = C_HEADS * HEAD
    nsc = (dm.M + dm.N) // CHUNK

    def body(u_ref, w_ref, kt_ref, qd_ref, qk_ref, gt_ref, o_ref, ss_ref, s_s):
        @pl.when(pl.program_id(2) == 0)
        def _():
            s_s[...] = jnp.zeros_like(s_s)

        hs = range(C_HEADS)
        blk = [pl.ds(h * HEAD, HEAD) for h in hs]
        ss = [s_s[b, :] for b in blk]
        sbs = [s.astype(BF16) for s in ss]
        for b, sb in zip(blk, sbs):
            ss_ref[0, 0, b, :] = sb
        ws = [jnp.dot(w_ref[0, :, b], sb, preferred_element_type=F32) for b, sb in zip(blk, sbs)]
        os1 = [jnp.dot(qd_ref[0, :, b], sb, preferred_element_type=F32) for b, sb in zip(blk, sbs)]
        vnbs = [(u_ref[0, :, b] - wv).astype(BF16) for b, wv in zip(blk, ws)]
        os2 = [jnp.dot(qk_ref[0, :, pl.ds(h * HEAD, CHUNK)], vnbs[h], preferred_element_type=F32) for h in hs]
        upd = [_dot_tn(kt_ref[0, :, b], vnb) for b, vnb in zip(blk, vnbs)]
        for h in hs:
            o_ref[0, :, blk[h]] = os1[h] + os2[h]
            s_s[blk[h], :] = ss[h] * gt_ref[0, 0, pl.ds(h, 1), :] + upd[h]

    big = pl.BlockSpec((1, CHUNK, HD), lambda b, d, c: (d, _scan_chunk(b, d, c, dm), 0))
    return pl.pallas_call(
        body, name=name, grid=(dm.Bl, 2, nsc),
        in_specs=[big, big, big, big, big,
                  pl.BlockSpec((1, 1, C_HEADS, HEAD), lambda b, d, c: (d, _scan_chunk(b, d, c, dm), 0, 0))],
        out_specs=[big, pl.BlockSpec((1, 1, HD, HEAD), lambda b, d, c: (d, _scan_chunk(b, d, c, dm), 0, 0))],
        out_shape=[_sds((2, dm.T, HD), F32), _sds((2, nch, HD, HEAD), BF16)],
        scratch_shapes=[pltpu.VMEM((HD, HEAD), F32)], compiler_params=_cp(),
    )(U, W, KT, QD, QK, GT)


def gdn_scan_bwd(dO, SS, U, W, KT, QD, QK, GT, dm, name):
    nch = dm.T // CHUNK
    HD = C_HEADS * HEAD
    nsc = (dm.M + dm.N) // CHUNK

    def body(do_ref, ss_ref, u_ref, w_ref, kt_ref, qd_ref, qk_ref, gt_ref,
             du_ref, dw_ref, dkt_ref, dqd_ref, dqk_ref, dgt_ref, ds_s):
        @pl.when(pl.program_id(2) == 0)
        def _():
            ds_s[...] = jnp.zeros_like(ds_s)

        hs = range(C_HEADS)
        blk = [pl.ds(h * HEAD, HEAD) for h in hs]
        sbs = [ss_ref[0, 0, b, :] for b in blk]
        ss = [s.astype(F32) for s in sbs]
        dobs = [do_ref[:, b].astype(BF16) for b in blk]
        dsns = [ds_s[b, :] for b in blk]
        dsnbs = [t.astype(BF16) for t in dsns]
        wss = [jnp.dot(w_ref[0, :, b], sb, preferred_element_type=F32) for b, sb in zip(blk, sbs)]
        dqds = [_dot_nt(dob, sb) for dob, sb in zip(dobs, sbs)]
        dv1 = [_dot_tn(qk_ref[0, :, pl.ds(h * HEAD, CHUNK)], dobs[h]) for h in hs]
        dv2 = [jnp.dot(kt_ref[0, :, b], t, preferred_element_type=F32) for b, t in zip(blk, dsnbs)]
        ds1 = [_dot_tn(qd_ref[0, :, b], dob) for b, dob in zip(blk, dobs)]
        vnbs = [(u_ref[0, :, b] - wv).astype(BF16) for b, wv in zip(blk, wss)]
        dvns = [a + b for a, b in zip(dv1, dv2)]
        dvnbs = [t.astype(BF16) for t in dvns]
        dqks = [_dot_nt(dob, vnb) for dob, vnb in zip(dobs, vnbs)]
        dkts = [_dot_nt(vnb, t) for vnb, t in zip(vnbs, dsnbs)]
        dws = [_dot_nt(t, sb) for t, sb in zip(dvnbs, sbs)]
        ds2 = [_dot_tn(w_ref[0, :, b], t) for b, t in zip(blk, dvnbs)]
        for h in hs:
            b = blk[h]
            dqd_ref[0, :, b] = dqds[h]
            dqk_ref[0, :, b] = jnp.concatenate([dqks[h], jnp.zeros_like(dqks[h])], axis=1)
            dkt_ref[0, :, b] = dkts[h]
            du_ref[0, :, b] = dvns[h]
            dw_ref[0, :, b] = -dws[h]
            dgt_ref[0, 0, pl.ds(h, 1), :] = jnp.broadcast_to(jnp.sum(dsns[h] * ss[h], keepdims=True), (1, HEAD))
            ds_s[b, :] = dsns[h] * gt_ref[0, 0, pl.ds(h, 1), :] + ds1[h] - ds2[h]

    def mem(b, d, c):
        return _scan_chunk(b, d, nsc - 1 - c, dm)

    big = pl.BlockSpec((1, CHUNK, HD), lambda b, d, c: (d, mem(b, d, c), 0))
    gts = pl.BlockSpec((1, 1, C_HEADS, HEAD), lambda b, d, c: (d, mem(b, d, c), 0, 0))
    return pl.pallas_call(
        body, name=name, grid=(dm.Bl, 2, nsc),
        in_specs=[pl.BlockSpec((CHUNK, HD), lambda b, d, c: (mem(b, d, c), 0)),
                  pl.BlockSpec((1, 1, HD, HEAD), lambda b, d, c: (d, mem(b, d, c), 0, 0)), big, big, big, big, big, gts],
        out_specs=[big, big, big, big, big, gts],
        out_shape=[_sds((2, dm.T, HD), F32)] * 5 + [_sds((2, nch, C_HEADS, HEAD), F32)],
        scratch_shapes=[pltpu.VMEM((HD, HEAD), F32)], compiler_params=_cp(),
    )(dO, SS, U, W, KT, QD, QK, GT)


def _finish_fn(o, z, gain):
    y = o * lax.rsqrt(jnp.mean(o * o, axis=-1, keepdims=True) + EPS) * gain
    return y * _silu(z)


def gdn_finish(O, P1, og, dm, name):
    TM = dm.TM
    HD = C_HEADS * HEAD
    zc = (3 * HD) // HEAD

    def body(o0_ref, o1_ref, z_ref, g_ref, y_ref):
        y_ref[...] = _finish_fn(o0_ref[0] + o1_ref[0], z_ref[...], g_ref[...]).astype(BF16)

    return pl.pallas_call(
        body, name=name, grid=(dm.ntx, C_HEADS),
        in_specs=[pl.BlockSpec((1, TM, HEAD), lambda i, j: (0, i, j)), pl.BlockSpec((1, TM, HEAD), lambda i, j: (1, i, j)),
                  pl.BlockSpec((TM, HEAD), lambda i, j: (i, zc + j)), pl.BlockSpec((1, HEAD), lambda i, j: (0, 0))],
        out_specs=pl.BlockSpec((TM, HEAD), lambda i, j: (i, j)),
        out_shape=_sds((dm.Tx, HD), BF16), compiler_params=_cp(),
    )(O, O, P1, og)


def gdn_finish_bwd(O, P1, og, dY, dm, name):
    TM = dm.TM
    HD = C_HEADS * HEAD
    zc = (3 * HD) // HEAD

    def body(o0_ref, o1_ref, z_ref, g_ref, dy_ref, do_ref, dz_ref, dg_ref):
        i, j = pl.program_id(0), pl.program_id(1)
        _, vjp = jax.vjp(_finish_fn, o0_ref[0] + o1_ref[0], z_ref[...], g_ref[...])
        do, dz, dg = vjp(dy_ref[...])
        do_ref[...] = do
        dz_ref[...] = dz
        _acc(dg_ref, (slice(None), slice(None)), dg, (i == 0) & (j == 0))

    blk = pl.BlockSpec((TM, HEAD), lambda i, j: (i, j))
    return pl.pallas_call(
        body, name=name, grid=(dm.ntx, C_HEADS),
        in_specs=[pl.BlockSpec((1, TM, HEAD), lambda i, j: (0, i, j)), pl.BlockSpec((1, TM, HEAD), lambda i, j: (1, i, j)),
                  pl.BlockSpec((TM, HEAD), lambda i, j: (i, zc + j)), pl.BlockSpec((1, HEAD), lambda i, j: (0, 0)), blk],
        out_specs=[blk, blk, pl.BlockSpec((1, HEAD), lambda i, j: (0, 0))],
        out_shape=[_sds((dm.Tx, HD), F32), _sds((dm.Tx, HD), F32), _sds((1, HEAD), F32)],
        compiler_params=_cp(),
    )(O, O, P1, og, dY)


def adaln_fwd(c_ext, w_mod, b_loc, name):
    R, D = c_ext.shape
    nl = w_mod.shape[2]
    tn = _pick(nl, 384)

    def body(c_ref, w_ref, b_ref, o_ref):
        o_ref[0] = _dot(_silu(c_ref[...]), w_ref[0]) + b_ref[0]

    return pl.pallas_call(
        body, name=name, grid=(2, nl // tn),
        in_specs=[pl.BlockSpec((R, D), lambda l, j: (0, 0)), pl.BlockSpec((1, D, tn), lambda l, j: (l, 0, j)),
                  pl.BlockSpec((1, 1, tn), lambda l, j: (l, 0, j))],
        out_specs=pl.BlockSpec((1, R, tn), lambda l, j: (l, 0, j)),
        out_shape=_sds((2, R, nl), F32), compiler_params=_cp(),
    )(c_ext, w_mod, b_loc)


def adaln_bwd(c_ext, c_ctx, w_mod, dmx, dmh, nb, name):
    R, D = c_ext.shape
    nl = w_mod.shape[2]
    tn = _pick(nl, 384)
    nj = nl // tn

    def body(c_ref, cc_ref, w_ref, dmx_ref, dmh_ref, gw_ref, dc_ref):
        l, j = pl.program_id(0), pl.program_id(1)
        dh = dmh_ref[0, 0:1, :]
        for k in range(1, N_DEV):
            dh = dh + dmh_ref[0, k:k + 1, :]
        row = lax.broadcasted_iota(jnp.int32, (R, tn), 0)
        dmat = dmx_ref[0] + jnp.where(row == nb, dh, 0.0)
        gw_ref[0] = _dot_tn(_silu(c_ref[...]), dmat)
        part = _dot_nt(jnp.broadcast_to(dh, (8, tn)), w_ref[0])[0:1]
        _acc(dc_ref, (slice(None), slice(None)), part, (l == 0) & (j == 0))

        @pl.when((l == 1) & (j == nj - 1))
        def _():
            cc = cc_ref[...]
            sg = jax.nn.sigmoid(cc)
            dc_ref[...] = dc_ref[...] * (sg * (1.0 + cc * (1.0 - sg)))

    return pl.pallas_call(
        body, name=name, grid=(2, nj),
        in_specs=[pl.BlockSpec((R, D), lambda l, j: (0, 0)), pl.BlockSpec((1, D), lambda l, j: (0, 0)),
                  pl.BlockSpec((1, D, tn), lambda l, j: (l, 0, j)), pl.BlockSpec((1, R, tn), lambda l, j: (l, 0, j)),
                  pl.BlockSpec((1, N_DEV, tn), lambda l, j: (l, 0, j))],
        out_specs=[pl.BlockSpec((1, D, tn), lambda l, j: (l, 0, j)), pl.BlockSpec((1, D), lambda l, j: (0, 0))],
        out_shape=[_sds((2, D, nl), F32), _sds((1, D), F32)], compiler_params=_cp(),
    )(c_ext, c_ctx, w_mod, dmx, dmh)


def bmod_grad(dmx, dmh, name):
    _, R, n9 = dmx.shape

    def body(dmx_ref, dmh_ref, o_ref):
        o_ref[0] = jnp.sum(dmx_ref[0], axis=0, keepdims=True) + jnp.sum(dmh_ref[0], axis=0, keepdims=True)

    return pl.pallas_call(
        body, name=name, grid=(2,),
        in_specs=[pl.BlockSpec((1, R, n9), lambda l: (l, 0, 0)), pl.BlockSpec((1, N_DEV, n9), lambda l: (l, 0, 0))],
        out_specs=pl.BlockSpec((1, 1, n9), lambda l: (l, 0, 0)), out_shape=_sds((2, 1, n9), F32),
        compiler_params=_cp(),
    )(dmx, dmh)


def adamw(gs, w, m, v, name):
    S, R, C = gs.shape
    cap = max(8, (1 << 20) // (S * C))
    tr = R
    if R > cap:
        tr = max(t for t in range(8, cap + 1, 8) if R % t == 0)

    def body(g_ref, w_ref, m_ref, v_ref, go_ref, d_ref, mo_ref, vo_ref):
        g = g_ref[0].astype(F32)
        for k in range(1, S):
            g = g + g_ref[k].astype(F32)
        mn = ADAM_B1 * m_ref[...] + (1.0 - ADAM_B1) * g
        vn = ADAM_B2 * v_ref[...] + (1.0 - ADAM_B2) * jnp.square(g)
        m_hat = mn / (1.0 - ADAM_B1 ** ADAM_STEP)
        v_hat = vn / (1.0 - ADAM_B2 ** ADAM_STEP)
        go_ref[...] = g
        d_ref[...] = -ADAM_LR * (m_hat / (jnp.sqrt(v_hat) + ADAM_EPS) + ADAM_WD * w_ref[...])
        mo_ref[...] = mn
        vo_ref[...] = vn

    blk = pl.BlockSpec((tr, C), lambda i: (i, 0))
    return pl.pallas_call(
        body, name=name, grid=(R // tr,),
        in_specs=[pl.BlockSpec((S, tr, C), lambda i: (0, i, 0)), blk, blk, blk],
        out_specs=[blk] * 4, out_shape=[_sds((R, C), F32)] * 4, compiler_params=_cp(),
    )(gs, w, m, v)


def _gather_flat(parts, dtype, name):
    flat = jnp.concatenate([p.astype(dtype).reshape(-1) for p in parts])
    n = flat.shape[0]
    pad = (-n) % LANE
    if pad:
        flat = jnp.concatenate([flat, jnp.zeros((pad,), dtype)])
    got = all_gather([flat.reshape(-1, LANE)], name)[0].reshape(N_DEV, -1)
    out, off = [], 0
    for p in parts:
        out.append(got[:, off:off + p.size].reshape((N_DEV,) + p.shape))
        off += p.size
    return out


def _cols_full(g):
    return g.transpose(1, 0, 2).reshape(g.shape[1], -1)


def _cols_split(full):
    K = full.shape[0]
    return full.reshape(K, N_DEV, -1).transpose(1, 0, 2)


def kernel(x, c, ctx, c_ctx, w_mod, b_mod, norm_g, ffn_wg, ffn_wu, ffn_wd, ab_w_in, ab_q_norm, ab_k_norm, pool_w, pool_scale, ab_w_out, gdn_w_in, gdn_conv_w, gdn_a_log, gdn_dt_bias, gdn_o_norm, gdn_w_out, loss_target, m_c_ctx, m_w_mod, m_b_mod, m_norm_g, m_ffn_wg, m_ffn_wu, m_ffn_wd, m_ab_w_in, m_ab_q_norm, m_ab_k_norm, m_pool_w, m_pool_scale, m_ab_w_out, m_gdn_w_in, m_gdn_conv_w, m_gdn_a_log, m_gdn_dt_bias, m_gdn_o_norm, m_gdn_w_out, v_c_ctx, v_w_mod, v_b_mod, v_norm_g, v_ffn_wg, v_ffn_wu, v_ffn_wd, v_ab_w_in, v_ab_q_norm, v_ab_k_norm, v_pool_w, v_pool_scale, v_ab_w_out, v_gdn_w_in, v_gdn_conv_w, v_gdn_a_log, v_gdn_dt_bias, v_gdn_o_norm, v_gdn_w_out):
    Bl, N, D = x.shape
    M = ctx.shape[1]
    F = ffn_wd.shape[2] * N_DEV
    dm = Dims(Bl, N, M, D, F)
    TM, Tx, Th, T, G = dm.TM, dm.Tx, dm.Th, dm.T, dm.G
    HD = C_HEADS * HEAD
    me = 4 * lax.axis_index("x") + 2 * lax.axis_index("y") + lax.axis_index("c")
    nb = N_DEV * Bl
    R = -(-(nb + 1) // 8) * 8
    nl = w_mod.shape[2]
    n_gdn = gdn_w_in.shape[2] * N_DEV
    n_gdn_pad = -(-n_gdn // LANE) * LANE

    big = [w.astype(BF16) for w in (ffn_wg, ffn_wu, ffn_wd, ab_w_in, ab_w_out, gdn_w_in, gdn_w_out)]
    g_wg, g_wu, g_wd, g_abin, g_about, g_gin, g_gout = all_gather(big, "gather_weights")
    g_c, g_ng, g_cw = _gather_flat([c, norm_g, gdn_conv_w], F32, "gather_small")
    W_ABIN = _cols_full(g_abin[:, 0])
    W_ABOUT = g_about[:, 0].reshape(-1, D)
    W_GIN = jnp.pad(_cols_full(g_gin[:, 0]), ((0, 0), (0, n_gdn_pad - n_gdn)))
    W_GOUT = g_gout[:, 0].reshape(-1, D)
    gains = g_ng.transpose(1, 2, 0, 3).reshape(2, 3, 1, D)
    conv_w = g_cw[:, 0].transpose(1, 0, 2).reshape(3, -1)

    c_all = g_c.reshape(nb, D)
    c_ext = jnp.concatenate([c_all, c_ctx[None], jnp.zeros((R - nb - 1, D), F32)], 0)
    b_loc = lax.dynamic_slice_in_dim(b_mod, me * nl, nl, axis=1).reshape(2, 1, nl)
    mod_loc = adaln_fwd(c_ext, w_mod, b_loc, "adaln_fwd")
    (g_mod,) = _gather_flat([mod_loc], F32, "gather_mod")
    mod_full = g_mod.transpose(1, 2, 0, 3).reshape(2, R, 9 * D)
    MOD = []
    for l in range(2):
        mine = lax.dynamic_slice_in_dim(mod_full[l], me * Bl, Bl, axis=0)
        MOD.append(jnp.concatenate([mine, mod_full[l, nb:nb + 1]], 0).reshape(G, 9, D))

    dm5, dmr = dm.with_tile(512), dm.with_tile(1024)
    tabs = _rope_tables(dmr)
    qkg = jnp.stack([ab_q_norm, ab_k_norm])
    pw = pool_w[0].astype(BF16)
    pwT = pool_w[0].transpose(0, 2, 1).astype(BF16)
    par = jnp.stack([jnp.pad(jnp.pad(p[0], ((0, 0), (0, 8))).reshape(-1), (0, LANE - 32))
                     for p in (gdn_a_log, gdn_dt_bias)])

    X0 = jnp.concatenate([x.reshape(Tx, D), ctx.reshape(Th, D)], 0)
    def ffn(X, l, s, s0, all_rows, tag):
        return ffn_fwd(X, MOD[l], gains[l, 2 * s], g_wg, g_wu, g_wd, l, s, s0, dm, all_rows, "ffn_fwd_" + tag)

    X1, Y1 = ffn(X0, 0, 0, 0, True, "00")
    P0, XN0 = modmm(X1, MOD[0], gains[0, 1], W_ABIN, 3, dm5, "ab_in_proj")
    QKV = qkv_prep(P0, qkg, tabs, dmr, "qkv_prep")
    ATT = jnp.concatenate([attn_fwd(QKV, dm5, True, "attn_fwd_x"), attn_fwd(QKV, dm, False, "attn_fwd_h")], 0)
    POOL = jnp.concatenate([pool_fwd(P0, pw, pool_scale, dm, True, "pool_fwd_x"),
                            pool_fwd(P0, pw, pool_scale, dm, False, "pool_fwd_h")], 0)
    na = A_HEADS * HEAD
    X2, YM0 = proj_res([ATT, POOL], [W_ABOUT[:na], W_ABOUT[na:]], X1, MOD[0], dm5, dm5.nt, "ab_out_proj")
    X3, Y3 = ffn(X2, 0, 1, 6, True, "01")
    X4, Y4 = ffn(X3, 1, 0, 0, True, "10")
    P1, XN1 = modmm(X4, MOD[1], gains[1, 1], W_GIN, 3, dm5, "gdn_in_proj")
    QKVg = jnp.concatenate([gdn_prep(P1, conv_w, dm, True, "gdn_prep_x"), gdn_prep(P1, conv_w, dm, False, "gdn_prep_h")], 0)
    U, W, KT, QD, QK, GT, WF, AI = gdn_chunk_pre(QKVg, P1, par, dm, "gdn_chunk_pre")
    O, SS = gdn_scan_fwd(U, W, KT, QD, QK, GT, dm, "gdn_scan_fwd")
    FIN = gdn_finish(O, P1, gdn_o_norm, dmr, "gdn_finish")
    X5, YM1 = proj_res([FIN], [W_GOUT], X4, MOD[1], dm5, dm5.ntx, "gdn_out_proj")
    X6, Y6 = ffn(X5, 1, 1, 6, False, "11")
    lvec, dX6 = loss_head(X6, loss_target.reshape(Tx, D), dmr, "loss_head")
    loss = lax.psum(lvec[0, 0], AXES)

    zrow = lambda a: jnp.concatenate([a, jnp.zeros((G - a.shape[0],) + a.shape[1:], F32)], 0) if a.shape[0] < G else a

    def ffn_back(Xin, dXo, Y, l, s, s0, all_rows, tag):
        dXi, XNb, DOb, Hb, DGb, DUb, dmod, dgain = ffn_bwd(
            Xin, dXo, Y, MOD[l], gains[l, 2 * s], g_wg, g_wu, g_wd, l, s, s0, dm, all_rows, "ffn_bwd_" + tag)
        dwg, dwu, dwd = ffn_dw(XNb, DOb, Hb, DGb, DUb, dm, "ffn_dw_" + tag)
        return dXi, zrow(dmod), dgain, dwg, dwu, dwd

    dX5, dmod_12, dgain_12, dwg11, dwu11, dwd11 = ffn_back(X5, dX6, Y6, 1, 1, 6, False, "11")
    DY1, dFIN, dgate_1 = proj_res_bwd(dX5, YM1, MOD[1], [W_GOUT], dm5, dm5.ntx, "gdn_out_proj_bwd")
    d_gout = atb(FIN, DY1, Tx, dm, "gdn_dwout")
    dOsum, dZ, d_onorm = gdn_finish_bwd(O, P1, gdn_o_norm, dFIN, dmr, "gdn_finish_bwd")
    dO_all = jnp.concatenate([dOsum, jnp.zeros((Th, HD), F32)], 0)
    dU, dW, dKT, dQD, dQK, dGT = gdn_scan_bwd(dO_all, SS, U, W, KT, QD, QK, GT, dm, "gdn_scan_bwd")
    dQKVg, dAB, dPAR = gdn_chunk_pre_bwd(QKVg, P1, par, U, WF, AI, dU, dW, dKT, dQD, dQK, dGT, dm, "gdn_chunk_pre_bwd")
    dPx, dcw_x = gdn_prep_bwd(P1, dQKVg, conv_w, dm, True, "gdn_prep_bwd_x")
    dPh, dcw_h = gdn_prep_bwd(P1, dQKVg, conv_w, dm, False, "gdn_prep_bwd_h")
    d_conv = dcw_x + dcw_h
    dP1 = jnp.concatenate([jnp.concatenate([dPx, dPh], 0), jnp.concatenate([dZ, jnp.zeros((Th, HD), F32)], 0),
                           dAB[0] + dAB[1]], axis=1).astype(BF16)
    d_gin = atb(XN1, dP1, T, dm, "gdn_dwin")[:, :n_gdn]
    dX5_full = jnp.concatenate([dX5, jnp.zeros((Th, D), F32)], 0)
    dX4, dmod_11, dgain_11 = mixin_bwd(dP1, W_GIN, X4, dX5_full, MOD[1], gains[1, 1], 3, dm, "gdn_in_proj_bwd")
    dX3, dmod_10, dgain_10, dwg10, dwu10, dwd10 = ffn_back(X3, dX4, Y4, 1, 0, 0, True, "10")
    dMOD1 = jnp.concatenate([dmod_10, dmod_11, zrow(dgate_1), dmod_12], 1).reshape(G, 9 * D)

    dX2, dmod_02, dgain_02, dwg01, dwu01, dwd01 = ffn_back(X2, dX3, Y3, 0, 1, 6, True, "01")
    DY0, dATT, dPOOL, dgate_0 = proj_res_bwd(dX2, YM0, MOD[0], [W_ABOUT[:na], W_ABOUT[na:]], dm5, dm5.nt, "ab_out_proj_bwd")
    d_about = jnp.concatenate([atb(ATT, DY0, T, dm, "ab_dwout_a"), atb(POOL, DY0, T, dm, "ab_dwout_p")], 0)
    dUx, dpw_x, dps_x = pool_bwd(P0, dPOOL[:Tx], pw, pwT, pool_scale, dm, True, "pool_bwd_x")
    dUh, dpw_h, dps_h = pool_bwd(P0, dPOOL[Tx:], pw, pwT, pool_scale, dm, False, "pool_bwd_h")
    dQh, dKh0, dVh0 = attn_bwd(QKV, dATT[Tx:], dm, False, None, "attn_bwd_h")
    dQx, dKh, dVh, dKx, dVx = attn_bwd(QKV, dATT[:Tx], dm5, True, (dKh0, dVh0), "attn_bwd_x")
    dQKV = jnp.concatenate([jnp.concatenate([dQx, dQh], 0), jnp.concatenate([dKx, dKh], 0), jnp.concatenate([dVx, dVh], 0)], 1)
    dPqkv, d_qkg = qkv_prep_bwd(P0, dQKV, qkg, tabs, dmr, "qkv_prep_bwd")
    dP0 = jnp.concatenate([dPqkv, jnp.concatenate([dUx, dUh], 0)], 1).astype(BF16)
    d_abin = atb(XN0, dP0, T, dm, "ab_dwin")
    dX1, dmod_01, dgain_01 = mixin_bwd(dP0, W_ABIN, X1, dX2, MOD[0], gains[0, 1], 3, dm5, "ab_in_proj_bwd")
    dX0, dmod_00, dgain_00, dwg00, dwu00, dwd00 = ffn_back(X0, dX1, Y1, 0, 0, 0, True, "00")
    dMOD0 = jnp.concatenate([dmod_00, dmod_01, dgate_0, dmod_02], 1).reshape(G, 9 * D)
    grad_x = dX0[:Tx].reshape(Bl, N, D)

    d_ng = jnp.concatenate([dgain_00, dgain_01, dgain_02, dgain_10, dgain_11, dgain_12], 0)
    nf = ffn_wg.shape[3]
    parts = [jnp.concatenate([dwg00, dwg01, dwg10, dwg11], 1), jnp.concatenate([dwu00, dwu01, dwu10, dwu11], 1),
             jnp.concatenate([dwd00, dwd01, dwd10, dwd11], 1),
             _cols_split(d_abin), d_about.reshape(N_DEV, -1, D), _cols_split(d_gin), d_gout.reshape(N_DEV, -1, D),
             _cols_split(d_conv), _cols_split(d_ng)]
    gs_wg, gs_wu, gs_wd, gs_abin, gs_about, gs_gin, gs_gout, gs_conv, gs_ng = scatter_blocks(parts, "scatter_grads")

    d_alog = dPAR[0, :32].reshape(2, 16)[:, :8].reshape(1, 16)
    d_dtb = dPAR[1, :32].reshape(2, 16)[:, :8].reshape(1, 16)
    small = [d_qkg[0], d_qkg[1], (dpw_x + dpw_h).reshape(-1, HEAD), dps_x + dps_h, d_alog, d_dtb, d_onorm,
             jnp.stack([dMOD0, dMOD1])]
    gs_qn, gs_kn, gs_pw, gs_ps, gs_alog, gs_dtb, gs_on, g_dm = _gather_flat(small, F32, "gather_small_grads")
    dmx = g_dm[:, :, :Bl].transpose(1, 0, 2, 3).reshape(2, nb, 9 * D)
    dmx = jnp.concatenate([dmx, jnp.zeros((2, R - nb, 9 * D), F32)], 1)
    dmh = g_dm[:, :, Bl].transpose(1, 0, 2)
    cols_of_me = lambda a: lax.dynamic_slice_in_dim(a, me * nl, nl, axis=2)
    d_wmod, dcc = adaln_bwd(c_ext, c_ctx[None], w_mod, cols_of_me(dmx), cols_of_me(dmh), nb, "adaln_bwd")
    d_bmod = bmod_grad(dmx, dmh, "bmod_grad")
    (gs_cc,) = _gather_flat([dcc], F32, "gather_cctx_grad")

    def upd(gs, w, m, v, shape2, name):
        outs = adamw(gs.reshape((gs.shape[0],) + shape2), w.reshape(shape2), m.reshape(shape2), v.reshape(shape2), "adamw_" + name)
        return [o.reshape(w.shape) for o in outs]

    res = [
        upd(gs_cc, c_ctx, m_c_ctx, v_c_ctx, (1, D), "c_ctx"),
        upd(d_wmod[None], w_mod, m_w_mod, v_w_mod, (2 * D, nl), "w_mod"),
        upd(d_bmod[None], b_mod, m_b_mod, v_b_mod, (2, 9 * D), "b_mod"),
        upd(gs_ng, norm_g, m_norm_g, v_norm_g, (6, D // N_DEV), "norm_g"),
        upd(gs_wg, ffn_wg, m_ffn_wg, v_ffn_wg, (4 * D, nf), "ffn_wg"),
        upd(gs_wu, ffn_wu, m_ffn_wu, v_ffn_wu, (4 * D, nf), "ffn_wu"),
        upd(gs_wd, ffn_wd, m_ffn_wd, v_ffn_wd, (4 * nf, D), "ffn_wd"),
        upd(gs_abin, ab_w_in, m_ab_w_in, v_ab_w_in, (D, ab_w_in.shape[2]), "ab_w_in"),
        upd(gs_qn, ab_q_norm, m_ab_q_norm, v_ab_q_norm, (1, HEAD), "ab_q_norm"),
        upd(gs_kn, ab_k_norm, m_ab_k_norm, v_ab_k_norm, (1, HEAD), "ab_k_norm"),
        upd(gs_pw, pool_w, m_pool_w, v_pool_w, (len(POOL_WINDOWS) * HEAD, HEAD), "pool_w"),
        upd(gs_ps, pool_scale, m_pool_scale, v_pool_scale, (1, len(POOL_WINDOWS) * HEAD), "pool_scale"),
        upd(gs_about, ab_w_out, m_ab_w_out, v_ab_w_out, (ab_w_out.shape[1], D), "ab_w_out"),
        upd(gs_gin, gdn_w_in, m_gdn_w_in, v_gdn_w_in, (D, gdn_w_in.shape[2]), "gdn_w_in"),
        upd(gs_conv, gdn_conv_w, m_gdn_conv_w, v_gdn_conv_w, (3, gdn_conv_w.shape[2]), "gdn_conv_w"),
        upd(gs_alog, gdn_a_log, m_gdn_a_log, v_gdn_a_log, (1, 16), "gdn_a_log"),
        upd(gs_dtb, gdn_dt_bias, m_gdn_dt_bias, v_gdn_dt_bias, (1, 16), "gdn_dt_bias"),
        upd(gs_on, gdn_o_norm, m_gdn_o_norm, v_gdn_o_norm, (1, HEAD), "gdn_o_norm"),
        upd(gs_gout, gdn_w_out, m_gdn_w_out, v_gdn_w_out, (gdn_w_out.shape[1], D), "gdn_w_out"),
    ]
    return (loss, grad_x, *[r[0] for r in res], *[r[1] for r in res], *[r[2] for r in res], *[r[3] for r in res])
```

```python
import functools
from typing import NamedTuple

import jax
import jax.numpy as jnp
from jax import lax
from jax.experimental import pallas as pl
from jax.experimental.pallas import tpu as pltpu

F32, BF16 = jnp.float32, jnp.bfloat16
EPS = 1e-6
HEAD = 128
CHUNK = 64
GRID_W = 64
ROPE_THETA = 10000.0
POOL_WINDOWS = (2, 4, 8, 16)
A_HEADS, A_KV = 4, 2
C_HEADS = 8
N_DEV = 8
AXES = ("x", "y", "c")
ADAM_LR, ADAM_B1, ADAM_B2, ADAM_EPS, ADAM_WD, ADAM_STEP = 0.001, 0.9, 0.999, 1e-08, 0.01, 10
LANE = 128
VMEM_LIMIT = 56 * 1024 * 1024
NEG = -1e30


def _cp():
    return pltpu.CompilerParams(vmem_limit_bytes=VMEM_LIMIT)


def _sds(shape, dtype):
    return jax.ShapeDtypeStruct(tuple(shape), dtype)


def _dot(a, b):
    return jnp.dot(a.astype(BF16), b.astype(BF16), preferred_element_type=F32)


def _dot_nt(a, b):
    return lax.dot_general(a.astype(BF16), b.astype(BF16), (((1,), (1,)), ((), ())), preferred_element_type=F32)


def _dot_tn(a, b):
    return lax.dot_general(a.astype(BF16), b.astype(BF16), (((0,), (0,)), ((), ())), preferred_element_type=F32)


def _dot3(a, b, dims):
    ah, bh = a.astype(BF16), b.astype(BF16)
    al, bl = (a - ah.astype(F32)).astype(BF16), (b - bh.astype(F32)).astype(BF16)
    f = lambda x, y: lax.dot_general(x, y, (dims, ((), ())), preferred_element_type=F32)
    return f(ah, bh) + (f(ah, bl) + f(al, bh))


def _hdot(a, b):
    return _dot3(a, b, ((1,), (0,)))


def _hdot_nt(a, b):
    return _dot3(a, b, ((1,), (1,)))


def _hdot_tn(a, b):
    return _dot3(a, b, ((0,), (0,)))


def _pick(n, cap):
    if n <= cap:
        return n
    best = None
    for t in range(LANE, cap + 1, LANE):
        if n % t == 0:
            best = t
    assert best is not None, (n, cap)
    return best


class Dims(NamedTuple):
    Bl: int
    N: int
    M: int
    D: int
    F: int
    tm: int = 0

    @property
    def TM(self):
        return self.tm if self.tm else min(256, self.M)

    def with_tile(self, cap):
        return self._replace(tm=max(t for t in (1024, 512, 256, 128) if t <= cap and self.N % t == 0 and self.Th % t == 0))

    @property
    def Tx(self):
        return self.Bl * self.N

    @property
    def Th(self):
        return self.Bl * self.M

    @property
    def T(self):
        return self.Tx + self.Th

    @property
    def ntx(self):
        return self.Tx // self.TM

    @property
    def nt(self):
        return self.T // self.TM

    @property
    def tps(self):
        return self.N // self.TM

    @property
    def G(self):
        return self.Bl + 1


def _grp(i, dm, tm=None):
    tm = dm.TM if tm is None else tm
    return jnp.where(i < dm.Tx // tm, i // (dm.N // tm), dm.Bl)


def _first_of_group(i, dm, tm=None):
    tm = dm.TM if tm is None else tm
    return jnp.where(i < dm.Tx // tm, i % (dm.N // tm) == 0, i == dm.Tx // tm)


def _contraction_tile(rows):
    return max(t for t in (1024, 512, 256, 128) if rows % t == 0)


def _ffn_tile(dm):
    return max(t for t in (512, 256, 128) if dm.N % t == 0 and dm.Th % t == 0)


def _acc(ref, idx, val, first):
    @pl.when(first)
    def _():
        ref[idx] = val

    @pl.when(jnp.logical_not(first))
    def _():
        ref[idx] += val


def _modulate(x, gain, shift, scale):
    y = x * lax.rsqrt(jnp.mean(x * x, axis=-1, keepdims=True) + EPS)
    return (y * gain) * (1.0 + scale) + shift


def _silu(x):
    return x * jax.nn.sigmoid(x)


@functools.partial(jax.custom_vjp, nondiff_argnums=(1,))
def _shift_rows(a, k):
    n = a.shape[0]
    if k == 0:
        return a
    r = lax.broadcasted_iota(jnp.int32, a.shape, 0)
    rolled = pltpu.roll(a, (-k) % n, 0)
    ok = (r + k >= 0) & (r + k < n)
    return jnp.where(ok, rolled, 0.0)


def _shift_rows_fwd(a, k):
    return _shift_rows(a, k), None


def _shift_rows_bwd(k, _, d):
    return (_shift_rows(d, -k),)


_shift_rows.defvjp(_shift_rows_fwd, _shift_rows_bwd)


@functools.partial(jax.custom_vjp, nondiff_argnums=(1,))
def _roll_lanes(a, s):
    return pltpu.roll(a, s % LANE, 1)


def _roll_lanes_fwd(a, s):
    return _roll_lanes(a, s), None


def _roll_lanes_bwd(s, _, d):
    return (_roll_lanes(d, -s),)


_roll_lanes.defvjp(_roll_lanes_fwd, _roll_lanes_bwd)


def _rope(t, cs, sneg, spos):
    return t * cs + _roll_lanes(t, 96) * sneg + _roll_lanes(t, 32) * spos


def _rope_tables(dm):
    rows = dm.N // GRID_W
    row = jnp.repeat(jnp.arange(rows), GRID_W).astype(F32)
    col = jnp.tile(jnp.arange(GRID_W), rows).astype(F32)
    half = HEAD // 2
    inv_freq = jnp.power(ROPE_THETA, -jnp.arange(0, half, 2, dtype=F32) / half)
    ar, ac = row[:, None] * inv_freq, col[:, None] * inv_freq
    cs = jnp.concatenate([jnp.cos(ar), jnp.cos(ar), jnp.cos(ac), jnp.cos(ac)], axis=1)
    z = jnp.zeros_like(ar)
    sneg = jnp.concatenate([-jnp.sin(ar), z, -jnp.sin(ac), z], axis=1)
    spos = jnp.concatenate([z, jnp.sin(ar), z, jnp.sin(ac)], axis=1)
    pad1 = jnp.ones((dm.TM, HEAD), F32)
    pad0 = jnp.zeros((dm.TM, HEAD), F32)
    return (jnp.concatenate([cs, pad1], 0), jnp.concatenate([sneg, pad0], 0), jnp.concatenate([spos, pad0], 0))


def all_gather(xs, name):
    n = len(xs)

    def body(*refs):
        x_refs, out_refs = refs[:n], refs[n:2 * n]
        send_sems, recv_sems, local_sems = refs[2 * n:]
        x, y, c = lax.axis_index("x"), lax.axis_index("y"), lax.axis_index("c")
        me, sibling = (x, y, c), (x, y, 1 - c)
        chips = [(1 - x, y), (x, 1 - y), (1 - x, 1 - y)]

        def slot(a, px, py, pc):
            return out_refs[a].at[4 * px + 2 * py + pc]

        def copy(a, k, block, to, src=None):
            return pltpu.make_async_remote_copy(
                src_ref=slot(a, *block) if src is None else src, dst_ref=slot(a, *block),
                send_sem=send_sems.at[7 * a + k], recv_sem=recv_sems.at[7 * a + k],
                device_id=to, device_id_type=pl.DeviceIdType.MESH)

        mine = [pltpu.make_async_copy(x_refs[a], slot(a, *me), local_sems.at[a]) for a in range(n)]
        for cp in mine:
            cp.start()
        first = []
        for a in range(n):
            first.append(copy(a, 0, me, sibling, src=x_refs[a]))
            first += [copy(a, 1 + j, me, (*chip, c), src=x_refs[a]) for j, chip in enumerate(chips)]
        for cp in first:
            cp.start()
        passed = []
        for j, chip in enumerate(chips):
            for a in range(n):
                copy(a, 1 + j, (*chip, c), me).wait_recv()
                cp = copy(a, 4 + j, (*chip, c), sibling)
                cp.start()
                passed.append(cp)
        for a in range(n):
            copy(a, 0, sibling, me).wait_recv()
            for j, chip in enumerate(chips):
                copy(a, 4 + j, (*chip, 1 - c), me).wait_recv()
        for cp in first + passed:
            cp.wait_send()
        for cp in mine:
            cp.wait()

    anyspec = pl.BlockSpec(memory_space=pl.ANY)
    return pl.pallas_call(
        body, name=name, out_shape=[_sds((N_DEV,) + a.shape, a.dtype) for a in xs],
        in_specs=[anyspec] * n, out_specs=[anyspec] * n,
        scratch_shapes=[pltpu.SemaphoreType.DMA((7 * n,)), pltpu.SemaphoreType.DMA((7 * n,)),
                        pltpu.SemaphoreType.DMA((n,))],
    )(*xs)


def scatter_blocks(xs, name):
    n = len(xs)
    flips = [(0, 0, 1), (0, 1, 0), (0, 1, 1), (1, 0, 0), (1, 0, 1), (1, 1, 0), (1, 1, 1)]

    def body(*refs):
        x_refs, out_refs = refs[:n], refs[n:2 * n]
        send_sems, recv_sems, local_sems = refs[2 * n:]
        x, y, c = lax.axis_index("x"), lax.axis_index("y"), lax.axis_index("c")
        me = 4 * x + 2 * y + c

        def peer(f):
            return tuple(1 - v if d else v for v, d in zip((x, y, c), f))

        def lin(p):
            return 4 * p[0] + 2 * p[1] + p[2]

        mine = [pltpu.make_async_copy(x_refs[a].at[me], out_refs[a].at[me], local_sems.at[a]) for a in range(n)]
        for cp in mine:
            cp.start()
        copies = []
        for k, f in enumerate(flips):
            p = peer(f)
            for a in range(n):
                copies.append(pltpu.make_async_remote_copy(
                    src_ref=x_refs[a].at[lin(p)], dst_ref=out_refs[a].at[me],
                    send_sem=send_sems.at[7 * a + k], recv_sem=recv_sems.at[7 * a + k],
                    device_id=p, device_id_type=pl.DeviceIdType.MESH))
        for cp in copies:
            cp.start()
        for cp in copies:
            cp.wait_send()
            cp.wait_recv()
        for cp in mine:
            cp.wait()

    anyspec = pl.BlockSpec(memory_space=pl.ANY)
    return pl.pallas_call(
        body, name=name, out_shape=[_sds(a.shape, a.dtype) for a in xs],
        in_specs=[anyspec] * n, out_specs=[anyspec] * n,
        scratch_shapes=[pltpu.SemaphoreType.DMA((7 * n,)), pltpu.SemaphoreType.DMA((7 * n,)),
                        pltpu.SemaphoreType.DMA((n,))],
    )(*xs)


_FLIPS = [(0, 0, 1), (0, 1, 0), (0, 1, 1), (1, 0, 0), (1, 0, 1), (1, 1, 0), (1, 1, 1)]
_HBM = pl.BlockSpec(memory_space=pltpu.HBM)
_SEM = pl.BlockSpec(memory_space=pltpu.SEMAPHORE)
_EFFECT = pltpu.SideEffectType.DATAFLOW_SIDE_EFFECTING


def _scatter_copies(x_refs, land_refs, send_sems, recv_sems):
    x, y, c = lax.axis_index("x"), lax.axis_index("y"), lax.axis_index("c")
    me = 4 * x + 2 * y + c
    copies = []
    for k, f in enumerate(_FLIPS):
        p = tuple(1 - v if d else v for v, d in zip((x, y, c), f))
        for a in range(len(x_refs)):
            copies.append(pltpu.make_async_remote_copy(
                src_ref=x_refs[a].at[4 * p[0] + 2 * p[1] + p[2]], dst_ref=land_refs[a].at[me],
                send_sem=send_sems.at[7 * a + k], recv_sem=recv_sems.at[7 * a + k],
                device_id=p, device_id_type=pl.DeviceIdType.MESH))
    return copies


def scatter_start(xs, lands, name):
    n = len(xs)

    def body(*refs):
        for cp in _scatter_copies(refs[:n], refs[n:2 * n], refs[2 * n], refs[2 * n + 1]):
            cp.start()
        refs[-1][...] = jnp.zeros_like(refs[-1])

    hbm = lambda a: pltpu.HBM(a.shape, a.dtype)
    outs = pl.pallas_call(
        body, name=name,
        out_shape=(pltpu.SemaphoreType.DMA((7 * n,)), pltpu.SemaphoreType.DMA((7 * n,)),
                   *[hbm(a) for a in xs], *[hbm(a) for a in lands], _sds((8, LANE), F32)),
        in_specs=[_HBM] * (2 * n),
        out_specs=(_SEM, _SEM, *([_HBM] * (2 * n)), pl.BlockSpec(memory_space=pltpu.VMEM)),
        input_output_aliases={i: 2 + i for i in range(2 * n)},
        compiler_params=pltpu.CompilerParams(has_side_effects=_EFFECT),
    )(*[pltpu.with_memory_space_constraint(a, pltpu.HBM) for a in list(xs) + list(lands)])
    return outs[0], outs[1], outs[2:2 + n], outs[2 + n:2 + 2 * n], outs[-1]


def scatter_wait(send_sems, recv_sems, xs, lands, after, name):
    n = len(xs)

    def body(*refs):
        for cp in _scatter_copies(refs[:n], refs[n:2 * n], refs[2 * n], refs[2 * n + 1]):
            cp.wait_send()
            cp.wait_recv()

    hbm = lambda a: pltpu.HBM(a.shape, a.dtype)
    outs = pl.pallas_call(
        body, name=name, out_shape=tuple(hbm(a) for a in list(xs) + list(lands)),
        in_specs=[_HBM] * (2 * n) + [_SEM, _SEM, pl.BlockSpec(memory_space=pl.ANY)],
        out_specs=tuple([_HBM] * (2 * n)), input_output_aliases={i: i for i in range(2 * n)},
        compiler_params=pltpu.CompilerParams(has_side_effects=_EFFECT),
    )(*xs, *lands, send_sems, recv_sems, after)
    return outs[n:]


def _mod_spec(dm, nidx, tm=None):
    if nidx == 1:
        return pl.BlockSpec((1, 9, dm.D), lambda i: (_grp(i, dm, tm), 0, 0))
    return pl.BlockSpec((1, 9, dm.D), lambda i, k: (_grp(i, dm, tm), 0, 0))


def _wspec(shape5, l, s, ks):
    return pl.BlockSpec((ks, 1, 1) + tuple(shape5[3:]), lambda i, k: (k, l, s, 0, 0))


FFN_FWD_SHARDS = 4
FFN_BWD_SHARDS = 2


def ffn_fwd(X, MOD, gain, gwg, gwu, gwd, l, s, s0, dm, all_rows, name):
    D = dm.D
    tm = _ffn_tile(dm)
    rows = dm.T if all_rows else dm.Tx
    ks = FFN_FWD_SHARDS
    nk = N_DEV // ks

    def body(x_ref, m_ref, g_ref, wg_ref, wu_ref, wd_ref, xo_ref, y_ref, xn_s, acc_s):
        k = pl.program_id(1)

        @pl.when(k == 0)
        def _():
            m = m_ref[0]
            xn = _modulate(x_ref[...], g_ref[...], m[s0:s0 + 1], m[s0 + 1:s0 + 2])
            xn_s[...] = xn.astype(BF16)
            acc_s[...] = jnp.zeros_like(acc_s)

        xn = xn_s[...]
        y = None
        for j in range(0, ks, 2):
            wg2 = jnp.concatenate([wg_ref[j, 0, 0], wg_ref[j + 1, 0, 0]], axis=1)
            wu2 = jnp.concatenate([wu_ref[j, 0, 0], wu_ref[j + 1, 0, 0]], axis=1)
            wd2 = jnp.concatenate([wd_ref[j, 0, 0], wd_ref[j + 1, 0, 0]], axis=0)
            g = jnp.dot(xn, wg2, preferred_element_type=F32)
            u = jnp.dot(xn, wu2, preferred_element_type=F32)
            yj = jnp.dot((_silu(g) * u).astype(BF16), wd2, preferred_element_type=F32)
            y = yj if y is None else y + yj
        acc_s[...] += y

        @pl.when(k == nk - 1)
        def _():
            m = m_ref[0]
            y = acc_s[...]
            y_ref[...] = y
            xo_ref[...] = x_ref[...] + (0.5 * m[s0 + 2:s0 + 3]) * y

    row = pl.BlockSpec((tm, D), lambda i, k: (i, 0))
    return pl.pallas_call(
        body, name=name, grid=(rows // tm, nk),
        in_specs=[row, _mod_spec(dm, 2, tm), pl.BlockSpec((1, D), lambda i, k: (0, 0)),
                  _wspec(gwg.shape, l, s, ks), _wspec(gwu.shape, l, s, ks), _wspec(gwd.shape, l, s, ks)],
        out_specs=[row, row],
        out_shape=[_sds((rows, D), F32), _sds((rows, D), F32)],
        scratch_shapes=[pltpu.VMEM((tm, D), BF16), pltpu.VMEM((tm, D), F32)],
        compiler_params=_cp(),
    )(X, MOD, gain, gwg, gwu, gwd)


def ffn_bwd(X, dXo, Y, MOD, gain, gwg, gwu, gwd, l, s, s0, dm, all_rows, name):
    D = dm.D
    tm = _ffn_tile(dm)
    rows = dm.T if all_rows else dm.Tx
    ks = FFN_BWD_SHARDS
    nk = N_DEV // ks
    nf = gwg.shape[4]
    ngr = dm.G if all_rows else dm.Bl

    def body(x_ref, dxo_ref, y_ref, m_ref, g_ref, wg_ref, wu_ref, wd_ref,
             dxi_ref, xn_ref, do_ref, h_ref, dg_ref, du_ref, dm_ref, dgain_ref, xn_s, do_s, dxn_s):
        i, k = pl.program_id(0), pl.program_id(1)

        @pl.when(k == 0)
        def _():
            m = m_ref[0]
            xn = _modulate(x_ref[...], g_ref[...], m[s0:s0 + 1], m[s0 + 1:s0 + 2])
            xn_s[...] = xn.astype(BF16)
            do_s[...] = ((0.5 * m[s0 + 2:s0 + 3]) * dxo_ref[...]).astype(BF16)
            dxn_s[...] = jnp.zeros_like(dxn_s)

        xn, do = xn_s[...], do_s[...]
        dxn = None
        for p in range(ks // 2):
            j = 2 * p
            wg = jnp.concatenate([wg_ref[j, 0, 0], wg_ref[j + 1, 0, 0]], axis=1)
            wu = jnp.concatenate([wu_ref[j, 0, 0], wu_ref[j + 1, 0, 0]], axis=1)
            wd = jnp.concatenate([wd_ref[j, 0, 0], wd_ref[j + 1, 0, 0]], axis=0)
            g = jnp.dot(xn, wg, preferred_element_type=F32)
            u = jnp.dot(xn, wu, preferred_element_type=F32)
            sg = jax.nn.sigmoid(g)
            si = g * sg
            dh = _dot_nt(do, wd)
            dg = (dh * u * (sg * (1.0 + g * (1.0 - sg)))).astype(BF16)
            du = (dh * si).astype(BF16)
            dj = _dot_nt(dg, wg) + _dot_nt(du, wu)
            dxn = dj if dxn is None else dxn + dj
            h_ref[p] = (si * u).astype(BF16)
            dg_ref[p] = dg
            du_ref[p] = du
        dxn_s[...] += dxn

        @pl.when(k == nk - 1)
        def _():
            m = m_ref[0]
            _, vjp = jax.vjp(_modulate, x_ref[...], g_ref[...], m[s0:s0 + 1], m[s0 + 1:s0 + 2])
            dx, dgain, dshift, dscale = vjp(dxn_s[...])
            dxo = dxo_ref[...]
            dxi_ref[...] = dxo + dx
            xn_ref[...] = xn_s[...]
            do_ref[...] = do_s[...]
            dgate = jnp.sum(0.5 * dxo * y_ref[...], axis=0, keepdims=True)
            first = _first_of_group(i, dm, tm)
            _acc(dm_ref, (0, pl.ds(0, 1), slice(None)), dshift, first)
            _acc(dm_ref, (0, pl.ds(1, 1), slice(None)), dscale, first)
            _acc(dm_ref, (0, pl.ds(2, 1), slice(None)), dgate, first)
            _acc(dgain_ref, (slice(None), slice(None)), dgain, i == 0)

    row = pl.BlockSpec((tm, D), lambda i, k: (i, 0))
    slab = pl.BlockSpec((ks // 2, tm, 2 * nf), lambda i, k: (k, i, 0))
    pairs = _sds((N_DEV // 2, rows, 2 * nf), BF16)
    return pl.pallas_call(
        body, name=name, grid=(rows // tm, nk),
        in_specs=[row, row, row, _mod_spec(dm, 2, tm), pl.BlockSpec((1, D), lambda i, k: (0, 0)),
                  _wspec(gwg.shape, l, s, ks), _wspec(gwu.shape, l, s, ks), _wspec(gwd.shape, l, s, ks)],
        out_specs=[row, row, row, slab, slab, slab,
                   pl.BlockSpec((1, 3, D), lambda i, k: (_grp(i, dm, tm), 0, 0)),
                   pl.BlockSpec((1, D), lambda i, k: (0, 0))],
        out_shape=[_sds((rows, D), F32), _sds((rows, D), BF16), _sds((rows, D), BF16),
                   pairs, pairs, pairs,
                   _sds((ngr, 3, D), F32), _sds((1, D), F32)],
        scratch_shapes=[pltpu.VMEM((tm, D), BF16), pltpu.VMEM((tm, D), BF16), pltpu.VMEM((tm, D), F32)],
        compiler_params=_cp(),
    )(X, dXo, Y, MOD, gain, gwg, gwu, gwd)


def ffn_dw(XN, DO, H, DG, DU, dm, name):
    rows, D = XN.shape
    nf = H.shape[2] // 2
    tt = _contraction_tile(rows)
    nT = rows // tt

    def body(xn_ref, do_ref, h_ref, dg_ref, du_ref, dwg_ref, dwu_ref, dwd_ref, ag_s, au_s, ad_s):
        t = pl.program_id(1)

        @pl.when(t == 0)
        def _():
            ag_s[...] = jnp.zeros_like(ag_s)
            au_s[...] = jnp.zeros_like(au_s)
            ad_s[...] = jnp.zeros_like(ad_s)

        xn = xn_ref[...]
        ag_s[...] += _dot_tn(xn, dg_ref[0])
        au_s[...] += _dot_tn(xn, du_ref[0])
        ad_s[...] += _dot_tn(h_ref[0], do_ref[...])

        @pl.when(t == nT - 1)
        def _():
            for j in range(2):
                dwg_ref[j] = ag_s[:, pl.ds(j * nf, nf)].astype(BF16)
                dwu_ref[j] = au_s[:, pl.ds(j * nf, nf)].astype(BF16)
                dwd_ref[j] = ad_s[pl.ds(j * nf, nf), :].astype(BF16)

    row = pl.BlockSpec((tt, D), lambda k, t: (t, 0))
    slab = pl.BlockSpec((1, tt, 2 * nf), lambda k, t: (k, t, 0))
    return pl.pallas_call(
        body, name=name, grid=(N_DEV // 2, nT),
        in_specs=[row, row, slab, slab, slab],
        out_specs=[pl.BlockSpec((2, D, nf), lambda k, t: (k, 0, 0)), pl.BlockSpec((2, D, nf), lambda k, t: (k, 0, 0)),
                   pl.BlockSpec((2, nf, D), lambda k, t: (k, 0, 0))],
        out_shape=[_sds((N_DEV, D, nf), BF16), _sds((N_DEV, D, nf), BF16), _sds((N_DEV, nf, D), BF16)],
        scratch_shapes=[pltpu.VMEM((D, 2 * nf), F32), pltpu.VMEM((D, 2 * nf), F32), pltpu.VMEM((2 * nf, D), F32)],
        compiler_params=_cp(),
    )(XN, DO, H, DG, DU)


def atb(A, B, rows, dm, name):
    Ka, Nb = A.shape[1], B.shape[1]
    tk, tn = _pick(Ka, 1024), _pick(Nb, 1536)
    tt = _contraction_tile(rows)
    nT = rows // tt

    def body(a_ref, b_ref, o_ref, acc_s):
        t = pl.program_id(2)

        @pl.when(t == 0)
        def _():
            acc_s[...] = jnp.zeros_like(acc_s)

        acc_s[...] += _dot_tn(a_ref[...], b_ref[...])

        @pl.when(t == nT - 1)
        def _():
            o_ref[...] = acc_s[...].astype(BF16)

    return pl.pallas_call(
        body, name=name, grid=(Ka // tk, Nb // tn, nT),
        in_specs=[pl.BlockSpec((tt, tk), lambda i, j, t: (t, i)), pl.BlockSpec((tt, tn), lambda i, j, t: (t, j))],
        out_specs=pl.BlockSpec((tk, tn), lambda i, j, t: (i, j)),
        out_shape=_sds((Ka, Nb), BF16), scratch_shapes=[pltpu.VMEM((tk, tn), F32)], compiler_params=_cp(),
    )(A, B)


def modmm(X, MOD, gain, W, s0, dm, name):
    TM, D = dm.TM, dm.D
    Nc = W.shape[1]
    tn = _pick(Nc, 1536)
    nj = Nc // tn

    def body(x_ref, m_ref, g_ref, w_ref, p_ref, xn_ref):
        @pl.when(pl.program_id(1) == 0)
        def _():
            m = m_ref[0]
            xn_ref[...] = _modulate(x_ref[...], g_ref[...], m[s0:s0 + 1], m[s0 + 1:s0 + 2]).astype(BF16)

        p_ref[...] = jnp.dot(xn_ref[...], w_ref[...], preferred_element_type=F32)

    row = pl.BlockSpec((TM, D), lambda i, j: (i, 0))
    return pl.pallas_call(
        body, name=name, grid=(dm.nt, nj),
        in_specs=[row, _mod_spec(dm, 2), pl.BlockSpec((1, D), lambda i, j: (0, 0)),
                  pl.BlockSpec((D, tn), lambda i, j: (0, j))],
        out_specs=[pl.BlockSpec((TM, tn), lambda i, j: (i, j)), row],
        out_shape=[_sds((dm.T, Nc), F32), _sds((dm.T, D), BF16)],
        compiler_params=_cp(),
    )(X, MOD, gain, W)


def mixin_bwd(dP, W, X, dXres, MOD, gain, s0, dm, name):
    TM, D = dm.TM, dm.D
    K = dP.shape[1]

    def body(dp_ref, w_ref, x_ref, dr_ref, m_ref, g_ref, dx_ref, dm_ref, dgain_ref):
        i = pl.program_id(0)
        dxn = _dot_nt(dp_ref[...], w_ref[...])
        m = m_ref[0]
        _, vjp = jax.vjp(_modulate, x_ref[...], g_ref[...], m[s0:s0 + 1], m[s0 + 1:s0 + 2])
        dx, dgain, dshift, dscale = vjp(dxn)
        dx_ref[...] = dr_ref[...] + dx
        first = _first_of_group(i, dm)
        _acc(dm_ref, (0, pl.ds(0, 1), slice(None)), dshift, first)
        _acc(dm_ref, (0, pl.ds(1, 1), slice(None)), dscale, first)
        _acc(dgain_ref, (slice(None), slice(None)), dgain, i == 0)

    row = pl.BlockSpec((TM, D), lambda i: (i, 0))
    return pl.pallas_call(
        body, name=name, grid=(dm.nt,),
        in_specs=[pl.BlockSpec((TM, K), lambda i: (i, 0)), pl.BlockSpec((D, K), lambda i: (0, 0)), row, row,
                  _mod_spec(dm, 1), pl.BlockSpec((1, D), lambda i: (0, 0))],
        out_specs=[row, pl.BlockSpec((1, 2, D), lambda i: (_grp(i, dm), 0, 0)), pl.BlockSpec((1, D), lambda i: (0, 0))],
        out_shape=[_sds((dm.T, D), F32), _sds((dm.G, 2, D), F32), _sds((1, D), F32)],
        compiler_params=_cp(),
    )(dP, W, X, dXres, MOD, gain)


def proj_res(As, Ws, X, MOD, dm, ntiles, name):
    TM, D = dm.TM, dm.D
    n = len(As)
    rows = ntiles * TM

    def body(*refs):
        a_refs, w_refs = refs[:n], refs[n:2 * n]
        x_ref, m_ref, xo_ref, y_ref = refs[2 * n:]
        y = jnp.dot(a_refs[0][...], w_refs[0][...], preferred_element_type=F32)
        for a, w in zip(a_refs[1:], w_refs[1:]):
            y += jnp.dot(a[...], w[...], preferred_element_type=F32)
        y_ref[...] = y
        xo_ref[...] = x_ref[...] + m_ref[0][5:6] * y

    row = pl.BlockSpec((TM, D), lambda i: (i, 0))
    return pl.pallas_call(
        body, name=name, grid=(ntiles,),
        in_specs=[pl.BlockSpec((TM, a.shape[1]), lambda i: (i, 0)) for a in As]
        + [pl.BlockSpec(w.shape, lambda i: (0, 0)) for w in Ws] + [row, _mod_spec(dm, 1)],
        out_specs=[row, row], out_shape=[_sds((rows, D), F32), _sds((rows, D), F32)],
        compiler_params=_cp(),
    )(*As, *Ws, X, MOD)


def proj_res_bwd(dXo, Y, MOD, Ws, dm, ntiles, name):
    TM, D = dm.TM, dm.D
    n = len(Ws)
    rows = ntiles * TM
    ngr = dm.G if ntiles == dm.nt else dm.Bl

    def body(*refs):
        dxo_ref, y_ref, m_ref = refs[:3]
        w_refs = refs[3:3 + n]
        dy_ref = refs[3 + n]
        da_refs = refs[4 + n:4 + 2 * n]
        dgate_ref = refs[4 + 2 * n]
        i = pl.program_id(0)
        dxo = dxo_ref[...]
        dy = (m_ref[0][5:6] * dxo).astype(BF16)
        dy_ref[...] = dy
        for w, da in zip(w_refs, da_refs):
            da[...] = _dot_nt(dy, w[...])
        dgate = jnp.sum(dxo * y_ref[...], axis=0, keepdims=True)
        _acc(dgate_ref, (0, slice(None), slice(None)), dgate, _first_of_group(i, dm))

    row = pl.BlockSpec((TM, D), lambda i: (i, 0))
    return pl.pallas_call(
        body, name=name, grid=(ntiles,),
        in_specs=[row, row, _mod_spec(dm, 1)] + [pl.BlockSpec(w.shape, lambda i: (0, 0)) for w in Ws],
        out_specs=[row] + [pl.BlockSpec((TM, w.shape[0]), lambda i: (i, 0)) for w in Ws]
        + [pl.BlockSpec((1, 1, D), lambda i: (_grp(i, dm), 0, 0))],
        out_shape=[_sds((rows, D), BF16)] + [_sds((rows, w.shape[0]), F32) for w in Ws] + [_sds((ngr, 1, D), F32)],
        compiler_params=_cp(),
    )(dXo, Y, MOD, *Ws)


def loss_head(Xf, target, dm, name):
    TM, D = dm.TM, dm.D

    def body(x_ref, t_ref, l_ref, dx_ref, acc_s):
        i = pl.program_id(0)
        e = x_ref[...] - t_ref[...]
        dx_ref[...] = e * (1.0 / D)

        @pl.when(i == 0)
        def _():
            acc_s[...] = jnp.zeros_like(acc_s)

        acc_s[...] += jnp.sum(e * e, axis=0, keepdims=True)

        @pl.when(i == dm.ntx - 1)
        def _():
            tot = jnp.sum(acc_s[...], axis=1, keepdims=True) * (0.5 / D)
            l_ref[...] = jnp.broadcast_to(tot, (1, LANE))

    row = pl.BlockSpec((TM, D), lambda i: (i, 0))
    return pl.pallas_call(
        body, name=name, grid=(dm.ntx,), in_specs=[row, row],
        out_specs=[pl.BlockSpec((1, LANE), lambda i: (0, 0)), row],
        out_shape=[_sds((1, LANE), F32), _sds((dm.Tx, D), F32)],
        scratch_shapes=[pltpu.VMEM((1, D), F32)], compiler_params=_cp(),
    )(Xf, target)


def _qk_fn(p, gain, cs, sneg, spos):
    y = p * lax.rsqrt(jnp.mean(p * p, axis=-1, keepdims=True) + EPS) * gain
    return _rope(y, cs, sneg, spos)


def _tab_specs(dm, swap):
    def idx(i):
        return jnp.where(i < dm.ntx, i % dm.tps, dm.tps)
    if swap:
        return [pl.BlockSpec((dm.TM, HEAD), lambda j, i: (idx(i), 0))] * 3
    return [pl.BlockSpec((dm.TM, HEAD), lambda i, j: (idx(i), 0))] * 3


def qkv_prep(P0, qkg, tabs, dm, name):
    TM = dm.TM

    def body(p_ref, g_ref, cs_ref, sn_ref, sp_ref, o_ref):
        j = pl.program_id(1)

        @pl.when(j < 6)
        def _():
            o_ref[...] = _qk_fn(p_ref[...], g_ref[0], cs_ref[...], sn_ref[...], sp_ref[...]).astype(BF16)

        @pl.when(j >= 6)
        def _():
            o_ref[...] = p_ref[...].astype(BF16)

    blk = pl.BlockSpec((TM, HEAD), lambda i, j: (i, j))
    return pl.pallas_call(
        body, name=name, grid=(dm.nt, 8),
        in_specs=[blk, pl.BlockSpec((1, 1, HEAD), lambda i, j: (jnp.minimum(j // 4, 1), 0, 0))] + _tab_specs(dm, False),
        out_specs=blk, out_shape=_sds((dm.T, 8 * HEAD), BF16), compiler_params=_cp(),
    )(P0, qkg, *tabs)


def qkv_prep_bwd(P0, dQKV, qkg, tabs, dm, name):
    TM = dm.TM

    def body(p_ref, d_ref, g_ref, cs_ref, sn_ref, sp_ref, dp_ref, dg_ref):
        j, i = pl.program_id(0), pl.program_id(1)
        first = (i == 0) & ((j == 0) | (j == 4))

        @pl.when(j < 6)
        def _():
            _, vjp = jax.vjp(_qk_fn, p_ref[...], g_ref[0], cs_ref[...], sn_ref[...], sp_ref[...])
            dp, dg = vjp(d_ref[...])[:2]
            dp_ref[...] = dp
            _acc(dg_ref, (0, slice(None), slice(None)), dg, first)

        @pl.when(j >= 6)
        def _():
            dp_ref[...] = d_ref[...]

    blk = pl.BlockSpec((TM, HEAD), lambda j, i: (i, j))
    return pl.pallas_call(
        body, name=name, grid=(8, dm.nt),
        in_specs=[blk, blk, pl.BlockSpec((1, 1, HEAD), lambda j, i: (jnp.minimum(j // 4, 1), 0, 0))] + _tab_specs(dm, True),
        out_specs=[blk, pl.BlockSpec((1, 1, HEAD), lambda j, i: (jnp.minimum(j // 4, 1), 0, 0))],
        out_shape=[_sds((dm.T, 8 * HEAD), F32), _sds((2, 1, HEAD), F32)], compiler_params=_cp(),
    )(P0, dQKV, qkg, *tabs)


def _softmax2(sx, sh):
    m = jnp.max(sh, axis=-1, keepdims=True)
    if sx is not None:
        m = jnp.maximum(m, jnp.max(sx, axis=-1, keepdims=True))
    eh = jnp.exp(sh - m)
    l = jnp.sum(eh, axis=-1, keepdims=True)
    ex = None
    if sx is not None:
        ex = jnp.exp(sx - m)
        l = l + jnp.sum(ex, axis=-1, keepdims=True)
    inv = 1.0 / l
    return (None if ex is None else ex * inv), eh * inv


def _attn_geometry(dm, with_x):
    TQ = dm.TM
    if with_x:
        nq, qoff = dm.N // TQ, 0
    else:
        nq, qoff = dm.M // TQ, dm.Tx // TQ
    hoff = dm.Tx // dm.M
    return TQ, nq, qoff, hoff


def attn_fwd(QKV, dm, with_x, name):
    TQ, nq, qoff, hoff = _attn_geometry(dm, with_x)
    scale = HEAD ** -0.5
    rows = dm.Tx if with_x else dm.Th

    def body(*refs):
        if with_x:
            q_ref, kh_ref, vh_ref, kx_ref, vx_ref, o_ref = refs
        else:
            q_ref, kh_ref, vh_ref, o_ref = refs
        q = q_ref[...]
        sh = _dot_nt(q, kh_ref[...]) * scale
        sx = _dot_nt(q, kx_ref[...]) * scale if with_x else None
        px, ph = _softmax2(sx, sh)
        o = _dot(ph, vh_ref[...])
        if with_x:
            o = o + _dot(px, vx_ref[...])
        o_ref[...] = o.astype(BF16)

    qs = pl.BlockSpec((TQ, HEAD), lambda b, kv, g, qi: (qoff + b * nq + qi, kv * 2 + g))
    in_specs = [qs, pl.BlockSpec((dm.M, HEAD), lambda b, kv, g, qi: (hoff + b, 4 + kv)),
                pl.BlockSpec((dm.M, HEAD), lambda b, kv, g, qi: (hoff + b, 6 + kv))]
    args = [QKV, QKV, QKV]
    if with_x:
        in_specs += [pl.BlockSpec((dm.N, HEAD), lambda b, kv, g, qi: (b, 4 + kv)),
                     pl.BlockSpec((dm.N, HEAD), lambda b, kv, g, qi: (b, 6 + kv))]
        args += [QKV, QKV]
    return pl.pallas_call(
        body, name=name, grid=(dm.Bl, A_KV, 2, nq), in_specs=in_specs,
        out_specs=pl.BlockSpec((TQ, HEAD), lambda b, kv, g, qi: (b * nq + qi, kv * 2 + g)),
        out_shape=_sds((rows, A_HEADS * HEAD), BF16), compiler_params=_cp(),
    )(*args)


def attn_bwd(QKV, dO, dm, with_x, init, name):
    TQ, nq, qoff, hoff = _attn_geometry(dm, with_x)
    scale = HEAD ** -0.5
    rows = dm.Tx if with_x else dm.Th

    def body(*refs):
        if with_x:
            (q_ref, kh_ref, vh_ref, kx_ref, vx_ref, do_ref, ikh_ref, ivh_ref,
             dq_ref, dkh_ref, dvh_ref, dkx_ref, dvx_ref) = refs
        else:
            q_ref, kh_ref, vh_ref, do_ref, dq_ref, dkh_ref, dvh_ref = refs
        g, qi = pl.program_id(2), pl.program_id(3)
        q = q_ref[...]
        kh, vh = kh_ref[...], vh_ref[...]
        sh = _dot_nt(q, kh) * scale
        sx = _dot_nt(q, kx_ref[...]) * scale if with_x else None
        px, ph = _softmax2(sx, sh)
        dob = do_ref[...].astype(BF16)
        dph = _dot_nt(dob, vh)
        delta = jnp.sum(dph * ph, axis=-1, keepdims=True)
        if with_x:
            dpx = _dot_nt(dob, vx_ref[...])
            delta = delta + jnp.sum(dpx * px, axis=-1, keepdims=True)
        dsh = (ph * (dph - delta) * scale).astype(BF16)
        dq = _dot(dsh, kh)
        first = (g == 0) & (qi == 0)

        @pl.when(first)
        def _():
            if with_x:
                dkh_ref[...] = ikh_ref[...]
                dvh_ref[...] = ivh_ref[...]
                dkx_ref[...] = jnp.zeros_like(dkx_ref)
                dvx_ref[...] = jnp.zeros_like(dvx_ref)
            else:
                dkh_ref[...] = jnp.zeros_like(dkh_ref)
                dvh_ref[...] = jnp.zeros_like(dvh_ref)

        dkh_ref[...] += _dot_tn(dsh, q)
        dvh_ref[...] += _dot_tn(ph, dob)
        if with_x:
            dsx = (px * (dpx - delta) * scale).astype(BF16)
            dq = dq + _dot(dsx, kx_ref[...])
            dkx_ref[...] += _dot_tn(dsx, q)
            dvx_ref[...] += _dot_tn(px, dob)
        dq_ref[...] = dq

    qs = pl.BlockSpec((TQ, HEAD), lambda b, kv, g, qi: (qoff + b * nq + qi, kv * 2 + g))
    hs = lambda c0: pl.BlockSpec((dm.M, HEAD), lambda b, kv, g, qi: (hoff + b, c0 + kv))
    xs = lambda c0: pl.BlockSpec((dm.N, HEAD), lambda b, kv, g, qi: (b, c0 + kv))
    dos = pl.BlockSpec((TQ, HEAD), lambda b, kv, g, qi: (b * nq + qi, kv * 2 + g))
    acc_h = pl.BlockSpec((dm.M, HEAD), lambda b, kv, g, qi: (b, kv))
    acc_x = pl.BlockSpec((dm.N, HEAD), lambda b, kv, g, qi: (b, kv))
    in_specs, args = [qs, hs(4), hs(6)], [QKV, QKV, QKV]
    out_specs = [dos, acc_h, acc_h]
    out_shape = [_sds((rows, A_HEADS * HEAD), F32), _sds((dm.Th, A_KV * HEAD), F32), _sds((dm.Th, A_KV * HEAD), F32)]
    if with_x:
        in_specs += [xs(4), xs(6), dos, acc_h, acc_h]
        args += [QKV, QKV, dO, init[0], init[1]]
        out_specs += [acc_x, acc_x]
        out_shape += [_sds((dm.Tx, A_KV * HEAD), F32), _sds((dm.Tx, A_KV * HEAD), F32)]
    else:
        in_specs += [dos]
        args += [dO]
    return pl.pallas_call(
        body, name=name, grid=(dm.Bl, A_KV, 2, nq), in_specs=in_specs, out_specs=out_specs,
        out_shape=out_shape, compiler_params=_cp(),
    )(*args)


def _pool_mean(u, w):
    n = u.shape[0]
    t = lax.broadcasted_iota(jnp.int32, (n, 1), 0)
    cnt = (jnp.clip(t + (w - w // 2), 0, n) - jnp.clip(t - w // 2, 0, n)).astype(F32)
    s = _shift_rows(u, -(w // 2))
    for j in range(-(w // 2) + 1, w - w // 2):
        s = s + _shift_rows(u, j)
    return s / cnt - u


def pool_fwd(P0, pw, pscale, dm, on_x, name):
    n, off, rows = (dm.N, 0, dm.Tx) if on_x else (dm.M, dm.Tx // dm.M, dm.Th)
    ng = len(POOL_WINDOWS)

    def body(u_ref, w_ref, s_ref, o_ref):
        for g, w in enumerate(POOL_WINDOWS):
            cols = pl.ds(g * HEAD, HEAD)
            pooled = _pool_mean(u_ref[:, cols], w)
            o_ref[:, cols] = (_dot(pooled, w_ref[g]) * s_ref[:, cols]).astype(BF16)

    return pl.pallas_call(
        body, name=name, grid=(dm.Bl,),
        in_specs=[pl.BlockSpec((n, ng * HEAD), lambda b: (off + b, 2)),
                  pl.BlockSpec((ng, HEAD, HEAD), lambda b: (0, 0, 0)), pl.BlockSpec((1, ng * HEAD), lambda b: (0, 0))],
        out_specs=pl.BlockSpec((n, ng * HEAD), lambda b: (b, 0)),
        out_shape=_sds((rows, ng * HEAD), BF16), compiler_params=_cp(),
    )(P0, pw, pscale)


def pool_bwd(P0, dY, pw, pwT, pscale, dm, on_x, name):
    n, off, rows = (dm.N, 0, dm.Tx) if on_x else (dm.M, dm.Tx // dm.M, dm.Th)
    ng = len(POOL_WINDOWS)

    def body(u_ref, dy_ref, w_ref, wt_ref, s_ref, du_ref, dw_ref, ds_ref):
        b = pl.program_id(0)
        for g, w in enumerate(POOL_WINDOWS):
            cols = pl.ds(g * HEAD, HEAD)
            pooled, vjp = jax.vjp(lambda u: _pool_mean(u, w), u_ref[:, cols])
            pre = _dot(pooled, w_ref[g])
            dy = dy_ref[:, cols]
            dpre = dy * s_ref[:, cols]
            du_ref[:, cols] = vjp(_dot(dpre, wt_ref[g]))[0]
            _acc(dw_ref, (g, slice(None), slice(None)), _dot_tn(pooled, dpre), b == 0)
            _acc(ds_ref, (slice(None), cols), jnp.sum(dy * pre, axis=0, keepdims=True), b == 0)

    full = pl.BlockSpec((ng, HEAD, HEAD), lambda b: (0, 0, 0))
    vec = pl.BlockSpec((1, ng * HEAD), lambda b: (0, 0))
    return pl.pallas_call(
        body, name=name, grid=(dm.Bl,),
        in_specs=[pl.BlockSpec((n, ng * HEAD), lambda b: (off + b, 2)), pl.BlockSpec((n, ng * HEAD), lambda b: (b, 0)),
                  full, full, vec],
        out_specs=[pl.BlockSpec((n, ng * HEAD), lambda b: (b, 0)), full, vec],
        out_shape=[_sds((rows, ng * HEAD), F32), _sds((ng, HEAD, HEAD), F32), _sds((1, ng * HEAD), F32)],
        compiler_params=_cp(),
    )(P0, dY, pw, pwT, pscale)


def _conv_fn(p, w0, w1, w2, kind):
    c = w0 * _shift_rows(p, -1) + w1 * p + w2 * _shift_rows(p, 1)
    a = _silu(c)
    if kind == 2:
        return a
    a = a * lax.rsqrt(jnp.sum(a * a, axis=-1, keepdims=True) + EPS)
    return a * (HEAD ** -0.5) if kind == 0 else a


def gdn_prep(P1, conv_w, dm, on_x, name):
    n, off, rows = (dm.N, 0, dm.Tx) if on_x else (dm.M, dm.Tx // dm.M, dm.Th)

    def body(p_ref, w_ref, o_ref):
        j = pl.program_id(1)
        p, w = p_ref[...], w_ref[...]
        for kind in range(3):
            @pl.when(j // C_HEADS == kind)
            def _():
                o_ref[...] = _conv_fn(p, w[0:1], w[1:2], w[2:3], kind)

    return pl.pallas_call(
        body, name=name, grid=(dm.Bl, 3 * C_HEADS),
        in_specs=[pl.BlockSpec((n, HEAD), lambda b, j: (off + b, j)), pl.BlockSpec((3, HEAD), lambda b, j: (0, j))],
        out_specs=pl.BlockSpec((n, HEAD), lambda b, j: (b, j)),
        out_shape=_sds((rows, 3 * C_HEADS * HEAD), F32), compiler_params=_cp(),
    )(P1, conv_w)


def gdn_prep_bwd(P1, dQ, conv_w, dm, on_x, name):
    n, off, rows = (dm.N, 0, dm.Tx) if on_x else (dm.M, dm.Tx // dm.M, dm.Th)

    def body(p_ref, d0_ref, d1_ref, w_ref, dp_ref, dw_ref):
        j, b = pl.program_id(0), pl.program_id(1)
        p, w = p_ref[...], w_ref[...]
        for kind in range(3):
            @pl.when(j // C_HEADS == kind)
            def _():
                _, vjp = jax.vjp(functools.partial(_conv_fn, kind=kind), p, w[0:1], w[1:2], w[2:3])
                dp, d0, d1, d2 = vjp(d0_ref[0] + d1_ref[0])
                dp_ref[...] = dp
                _acc(dw_ref, (pl.ds(0, 1), slice(None)), d0, b == 0)
                _acc(dw_ref, (pl.ds(1, 1), slice(None)), d1, b == 0)
                _acc(dw_ref, (pl.ds(2, 1), slice(None)), d2, b == 0)

    return pl.pallas_call(
        body, name=name, grid=(3 * C_HEADS, dm.Bl),
        in_specs=[pl.BlockSpec((n, HEAD), lambda j, b: (off + b, j)),
                  pl.BlockSpec((1, n, HEAD), lambda j, b: (0, off + b, j)), pl.BlockSpec((1, n, HEAD), lambda j, b: (1, off + b, j)),
                  pl.BlockSpec((3, HEAD), lambda j, b: (0, j))],
        out_specs=[pl.BlockSpec((n, HEAD), lambda j, b: (b, j)), pl.BlockSpec((3, HEAD), lambda j, b: (0, j))],
        out_shape=[_sds((rows, 3 * C_HEADS * HEAD), F32), _sds((3, 3 * C_HEADS * HEAD), F32)],
        compiler_params=_cp(),
    )(P1, dQ, dQ, conv_w)


def _gate_fn(ab, par):
    lane = lax.broadcasted_iota(jnp.int32, ab.shape, 1)
    is_a = (lane % 16) < C_HEADS
    g = -jnp.exp(par[0:1]) * jax.nn.softplus(ab + par[1:2])
    return jnp.where(lane < 4 * C_HEADS, jnp.where(is_a, g, jax.nn.sigmoid(ab)), 0.0)


def _col(blk, idx):
    lane = lax.broadcasted_iota(jnp.int32, blk.shape, 1)
    return jnp.sum(jnp.where(lane == idx, blk, 0.0), axis=1, keepdims=True)


def _chunk_masks(rev):
    ii = lax.broadcasted_iota(jnp.int32, (CHUNK, CHUNK), 0)
    jj = lax.broadcasted_iota(jnp.int32, (CHUNK, CHUNK), 1)
    ahead = jnp.where(rev, jj - ii, ii - jj)
    return ahead >= 0, ahead > 0, (ii == jj).astype(F32)


def _inv_unit_tri(nmats, eye):
    xs = [eye - n for n in nmats]
    ps = [_hdot(n, n) for n in nmats]
    step = 2
    while True:
        xs = [x + _hdot(x, p) for x, p in zip(xs, ps)]
        step *= 2
        if step >= CHUNK:
            break
        ps = [_hdot(p, p) for p in ps]
    return xs


def _cum_lanes(x, transpose=False):
    lane = lax.broadcasted_iota(jnp.int32, x.shape, 1)
    down, up = x, x
    s = 1
    while s < CHUNK:
        down = down + _shift_rows(down, -s)
        up = up + _shift_rows(up, s)
        s *= 2
    return jnp.where((lane >= 16) if transpose else (lane < 16), down, up)


def _each(f, *lists):
    return [f(*a) for a in zip(*lists)]


def _chunk_common(qs, ks, vs, gcs, gcrs, tots, betas, rev, saved=None):
    incl, strict, eye = _chunk_masks(rev)
    es = _each(lambda gc, gcr: jnp.exp(jnp.where(incl, gc - gcr, NEG)), gcs, gcrs)
    egs = [jnp.exp(gc) for gc in gcs]
    ets = _each(lambda t, gc: jnp.exp(t - gc), tots, gcs)
    gts = [jnp.exp(t) for t in tots]
    kbs = _each(lambda k, b: k * b, ks, betas)
    kks = _each(_dot_nt, kbs, ks)
    qqs = _each(_dot_nt, qs, ks)
    if saved is None:
        nmats = _each(lambda kk, e: jnp.where(strict, kk * e, 0.0), kks, es)
        ainvs = _inv_unit_tri(nmats, eye)
        rhss = _each(lambda v, b, kb, eg: jnp.concatenate([v * b, kb * eg], axis=1), vs, betas, kbs, egs)
        sols = _each(_hdot, ainvs, rhss)
    else:
        ainvs, sols = saved
    return dict(incl=incl, strict=strict, e=es, eg=egs, et=ets, gt=gts, kb=kbs, kk=kks, ainv=ainvs, sol=sols, qq=qqs)


def _chunk_fwd(qs, ks, vs, gcs, gcrs, tots, betas, rev):
    c = _chunk_common(qs, ks, vs, gcs, gcrs, tots, betas, rev)
    incl = c["incl"]
    return _each(lambda q, k, sol, qq, e, et, eg, gt, ainv:
                 (sol[:, :HEAD], sol[:, HEAD:], k * et, q * eg, jnp.where(incl, qq * e, 0.0), gt, ainv),
                 qs, ks, c["sol"], c["qq"], c["e"], c["et"], c["eg"], c["gt"], c["ainv"])


def _chunk_bwd(qs, ks, vs, gcs, gcrs, tots, betas, rev, ainvs, sols, dus, dws, dkts, dqds, dqks, dgts):
    c = _chunk_common(qs, ks, vs, gcs, gcrs, tots, betas, rev, saved=(ainvs, sols))
    incl, strict = c["incl"], c["strict"]
    drhss = _each(lambda a, du, dw: _hdot_tn(a, jnp.concatenate([du, dw], axis=1)), c["ainv"], dus, dws)
    dns = _each(lambda drhs, sol: jnp.where(strict, -_hdot_nt(drhs, sol), 0.0), drhss, c["sol"])
    dkks = _each(lambda dn, e: dn * e, dns, c["e"])
    dqms = [jnp.where(incl, dqk, 0.0) for dqk in dqks]
    dqqs = _each(lambda dqm, e: dqm * e, dqms, c["e"])
    m_q = _each(_dot, dqqs, ks)
    m_k1 = _each(_dot_tn, dqqs, qs)
    m_k2 = _each(_dot_tn, dkks, c["kb"])
    m_kb = _each(_dot, dkks, ks)

    def finish(q, k, v, beta, e, eg, et, gt, kb, kk, qq, drhs, dn, dqm, dkt, dqd, dgt, mq, mk1, mk2, mkb):
        de = dn * kk + dqm * qq
        dq = mq + dqd * eg
        dkb = mkb + drhs[:, HEAD:] * eg
        dk = mk1 + mk2 + dkt * et + dkb * beta
        dv = drhs[:, :HEAD] * beta
        dbeta = jnp.sum(drhs[:, :HEAD] * v + dkb * k, axis=1, keepdims=True)
        deg = jnp.sum(drhs[:, HEAD:] * kb + dqd * q, axis=1, keepdims=True)
        dd = de * e
        dtd = jnp.sum(dkt * k, axis=1, keepdims=True) * et
        dgc = deg * eg - dtd + jnp.sum(dd, axis=1, keepdims=True) - jnp.sum(dd.T, axis=1, keepdims=True)
        dtot = jnp.sum(dtd, axis=0, keepdims=True) + dgt * gt
        return dq, dk, dv, dgc, dtot, dbeta

    return _each(finish, qs, ks, vs, betas, c["e"], c["eg"], c["et"], c["gt"], c["kb"], c["kk"], c["qq"],
                 drhss, dns, dqms, dkts, dqds, dgts, m_q, m_k1, m_k2, m_kb)


def gdn_chunk_pre(QKVg, P1, par, dm, name):
    nch = dm.T // CHUNK
    HD = C_HEADS * HEAD
    abcol = (4 * HD) // LANE

    def body(x_ref, ab_ref, par_ref, u_ref, w_ref, kt_ref, qd_ref, qk_ref, gt_ref, wf_ref, ai_ref, gct_s):
        d = pl.program_id(1)
        rev = d == 1
        gb = _gate_fn(ab_ref[...], par_ref[...])
        gcl = _cum_lanes(gb)
        gct_s[...] = gcl.T
        tot = jnp.sum(gb, axis=0, keepdims=True)
        hs = range(C_HEADS)
        outs = _chunk_fwd(
            [x_ref[:, pl.ds(h * HEAD, HEAD)] for h in hs],
            [x_ref[:, pl.ds((C_HEADS + h) * HEAD, HEAD)] for h in hs],
            [x_ref[:, pl.ds((2 * C_HEADS + h) * HEAD, HEAD)] for h in hs],
            [_col(gcl, d * 16 + h) for h in hs], [gct_s[pl.ds(d * 16 + h, 1), :] for h in hs],
            [_col(tot, d * 16 + h) for h in hs], [_col(gb, d * 16 + 8 + h) for h in hs], rev)
        for h, (u, w, kt, qd, qk, gt, ainv) in enumerate(outs):
            cols = pl.ds(h * HEAD, HEAD)
            u_ref[0, :, cols] = u
            w_ref[0, :, cols] = w.astype(BF16)
            kt_ref[0, :, cols] = kt.astype(BF16)
            qd_ref[0, :, cols] = qd.astype(BF16)
            qk_ref[0, :, cols] = jnp.concatenate([qk, jnp.zeros_like(qk)], axis=1).astype(BF16)
            gt_ref[0, 0, pl.ds(h, 1), :] = jnp.broadcast_to(gt, (1, HEAD))
            wf_ref[0, :, cols] = w
            ai_ref[0, :, cols] = jnp.concatenate([ainv, jnp.zeros_like(ainv)], axis=1)

    big = pl.BlockSpec((1, CHUNK, HD), lambda i, d: (d, i, 0))
    return pl.pallas_call(
        body, name=name, grid=(nch, 2),
        in_specs=[pl.BlockSpec((CHUNK, 3 * HD), lambda i, d: (i, 0)), pl.BlockSpec((CHUNK, LANE), lambda i, d: (i, abcol)),
                  pl.BlockSpec((2, LANE), lambda i, d: (0, 0))],
        out_specs=[big, big, big, big, big, pl.BlockSpec((1, 1, C_HEADS, HEAD), lambda i, d: (d, i, 0, 0)), big, big],
        out_shape=[_sds((2, dm.T, HD), F32), _sds((2, dm.T, HD), BF16), _sds((2, dm.T, HD), BF16),
                   _sds((2, dm.T, HD), BF16), _sds((2, dm.T, HD), BF16), _sds((2, nch, C_HEADS, HEAD), F32),
                   _sds((2, dm.T, HD), F32), _sds((2, dm.T, HD), F32)],
        scratch_shapes=[pltpu.VMEM((LANE, CHUNK), F32)], compiler_params=_cp(),
    )(QKVg, P1, par)


def gdn_chunk_pre_bwd(QKVg, P1, par, U, WF, AI, dU, dW, dKT, dQD, dQK, dGT, dm, name):
    nch = dm.T // CHUNK
    HD = C_HEADS * HEAD
    abcol = (4 * HD) // LANE

    def body(x_ref, ab_ref, par_ref, u_ref, wf_ref, ai_ref, du_ref, dw_ref, dkt_ref, dqd_ref, dqk_ref, dgt_ref,
             dx_ref, dab_ref, dpar_ref, gct_s):
        i, d = pl.program_id(0), pl.program_id(1)
        rev = d == 1
        ab, par = ab_ref[...], par_ref[...]
        gb, gate_vjp = jax.vjp(_gate_fn, ab, par)
        gcl = _cum_lanes(gb)
        gct_s[...] = gcl.T
        tot = jnp.sum(gb, axis=0, keepdims=True)
        lane = lax.broadcasted_iota(jnp.int32, (CHUNK, LANE), 1)
        dgcl = jnp.zeros((CHUNK, LANE), F32)
        dgb = jnp.zeros((CHUNK, LANE), F32)
        first = d == 0
        hs = range(C_HEADS)
        hcols = [pl.ds(h * HEAD, HEAD) for h in hs]
        outs = _chunk_bwd(
            [x_ref[:, c] for c in hcols],
            [x_ref[:, pl.ds((C_HEADS + h) * HEAD, HEAD)] for h in hs],
            [x_ref[:, pl.ds((2 * C_HEADS + h) * HEAD, HEAD)] for h in hs],
            [_col(gcl, d * 16 + h) for h in hs], [gct_s[pl.ds(d * 16 + h, 1), :] for h in hs],
            [_col(tot, d * 16 + h) for h in hs], [_col(gb, d * 16 + 8 + h) for h in hs], rev,
            [ai_ref[0, :, pl.ds(h * HEAD, CHUNK)] for h in hs],
            [jnp.concatenate([u_ref[0, :, c], wf_ref[0, :, c]], axis=1) for c in hcols],
            [du_ref[0, :, c] for c in hcols], [dw_ref[0, :, c] for c in hcols], [dkt_ref[0, :, c] for c in hcols],
            [dqd_ref[0, :, c] for c in hcols], [dqk_ref[0, :, pl.ds(h * HEAD, CHUNK)] for h in hs],
            [dgt_ref[0, 0, pl.ds(h, 1), pl.ds(0, 1)] for h in hs])
        for h, (dq, dk, dv, dgc, dtotal, dbeta) in enumerate(outs):
            idx = d * 16 + h
            dx_ref[0, :, hcols[h]] = dq
            dx_ref[0, :, pl.ds((C_HEADS + h) * HEAD, HEAD)] = dk
            dx_ref[0, :, pl.ds((2 * C_HEADS + h) * HEAD, HEAD)] = dv
            dgcl = dgcl + jnp.where(lane == idx, dgc, 0.0)
            dgb = dgb + jnp.where(lane == idx + 8, dbeta, 0.0) + jnp.where(lane == idx, dtotal, 0.0)
        dab, dpar = gate_vjp(dgb + _cum_lanes(dgcl, transpose=True))
        dab_ref[0] = dab
        _acc(dpar_ref, (slice(None), slice(None)), dpar, (i == 0) & first)

    big = pl.BlockSpec((1, CHUNK, HD), lambda i, d: (d, i, 0))
    return pl.pallas_call(
        body, name=name, grid=(nch, 2),
        in_specs=[pl.BlockSpec((CHUNK, 3 * HD), lambda i, d: (i, 0)), pl.BlockSpec((CHUNK, LANE), lambda i, d: (i, abcol)),
                  pl.BlockSpec((2, LANE), lambda i, d: (0, 0)), big, big, big, big, big, big, big, big,
                  pl.BlockSpec((1, 1, C_HEADS, HEAD), lambda i, d: (d, i, 0, 0))],
        out_specs=[pl.BlockSpec((1, CHUNK, 3 * HD), lambda i, d: (d, i, 0)), pl.BlockSpec((1, CHUNK, LANE), lambda i, d: (d, i, 0)),
                   pl.BlockSpec((2, LANE), lambda i, d: (0, 0))],
        out_shape=[_sds((2, dm.T, 3 * HD), F32), _sds((2, dm.T, LANE), F32), _sds((2, LANE), F32)],
        scratch_shapes=[pltpu.VMEM((LANE, CHUNK), F32)], compiler_params=_cp(),
    )(QKVg, P1, par, U, WF, AI, dU, dW, dKT, dQD, dQK, dGT)


def _scan_chunk(b, d, c, dm):
    nh, nx = dm.M // CHUNK, dm.N // CHUNK
    in_h = c < nh
    pos_h = jnp.where(d == 0, c, nh - 1 - c)
    pos_x = jnp.where(d == 0, c - nh, nx - 1 - (c - nh))
    return jnp.where(in_h, dm.Tx // CHUNK + b * nh + pos_h, b * nx + pos_x)


def gdn_scan_fwd(U, W, KT, QD, QK, GT, dm, name):
    nch = dm.T // CHUNK
    HD = C_HEADS * HEAD
    nsc = (dm.M + dm.N) // CHUNK

    def body(u_ref, w_ref, kt_ref, qd_ref, qk_ref, gt_ref, o_ref, ss_ref, s_s):
        @pl.when(pl.program_id(2) == 0)
        def _():
            s_s[...] = jnp.zeros_like(s_s)

        hs = range(C_HEADS)
        blk = [pl.ds(h * HEAD, HEAD) for h in hs]
        ss = [s_s[b, :] for b in blk]
        sbs = [s.astype(BF16) for s in ss]
        for b, sb in zip(blk, sbs):
            ss_ref[0, 0, b, :] = sb
        ws = [jnp.dot(w_ref[0, :, b], sb, preferred_element_type=F32) for b, sb in zip(blk, sbs)]
        os1 = [jnp.dot(qd_ref[0, :, b], sb, preferred_element_type=F32) for b, sb in zip(blk, sbs)]
        vnbs = [(u_ref[0, :, b] - wv).astype(BF16) for b, wv in zip(blk, ws)]
        os2 = [jnp.dot(qk_ref[0, :, pl.ds(h * HEAD, CHUNK)], vnbs[h], preferred_element_type=F32) for h in hs]
        upd = [_dot_tn(kt_ref[0, :, b], vnb) for b, vnb in zip(blk, vnbs)]
        for h in hs:
            o_ref[0, :, blk[h]] = os1[h] + os2[h]
            s_s[blk[h], :] = ss[h] * gt_ref[0, 0, pl.ds(h, 1), :] + upd[h]

    big = pl.BlockSpec((1, CHUNK, HD), lambda b, d, c: (d, _scan_chunk(b, d, c, dm), 0))
    return pl.pallas_call(
        body, name=name, grid=(dm.Bl, 2, nsc),
        in_specs=[big, big, big, big, big,
                  pl.BlockSpec((1, 1, C_HEADS, HEAD), lambda b, d, c: (d, _scan_chunk(b, d, c, dm), 0, 0))],
        out_specs=[big, pl.BlockSpec((1, 1, HD, HEAD), lambda b, d, c: (d, _scan_chunk(b, d, c, dm), 0, 0))],
        out_shape=[_sds((2, dm.T, HD), F32), _sds((2, nch, HD, HEAD), BF16)],
        scratch_shapes=[pltpu.VMEM((HD, HEAD), F32)], compiler_params=_cp(),
    )(U, W, KT, QD, QK, GT)


def gdn_scan_bwd(dO, SS, U, W, KT, QD, QK, GT, dm, name):
    nch = dm.T // CHUNK
    HD = C_HEADS * HEAD
    nsc = (dm.M + dm.N) // CHUNK

    def body(do_ref, ss_ref, u_ref, w_ref, kt_ref, qd_ref, qk_ref, gt_ref,
             du_ref, dw_ref, dkt_ref, dqd_ref, dqk_ref, dgt_ref, ds_s):
        @pl.when(pl.program_id(2) == 0)
        def _():
            ds_s[...] = jnp.zeros_like(ds_s)

        hs = range(C_HEADS)
        blk = [pl.ds(h * HEAD, HEAD) for h in hs]
        sbs = [ss_ref[0, 0, b, :] for b in blk]
        ss = [s.astype(F32) for s in sbs]
        dobs = [do_ref[:, b].astype(BF16) for b in blk]
        dsns = [ds_s[b, :] for b in blk]
        dsnbs = [t.astype(BF16) for t in dsns]
        wss = [jnp.dot(w_ref[0, :, b], sb, preferred_element_type=F32) for b, sb in zip(blk, sbs)]
        dqds = [_dot_nt(dob, sb) for dob, sb in zip(dobs, sbs)]
        dv1 = [_dot_tn(qk_ref[0, :, pl.ds(h * HEAD, CHUNK)], dobs[h]) for h in hs]
        dv2 = [jnp.dot(kt_ref[0, :, b], t, preferred_element_type=F32) for b, t in zip(blk, dsnbs)]
        ds1 = [_dot_tn(qd_ref[0, :, b], dob) for b, dob in zip(blk, dobs)]
        vnbs = [(u_ref[0, :, b] - wv).astype(BF16) for b, wv in zip(blk, wss)]
        dvns = [a + b for a, b in zip(dv1, dv2)]
        dvnbs = [t.astype(BF16) for t in dvns]
        dqks = [_dot_nt(dob, vnb) for dob, vnb in zip(dobs, vnbs)]
        dkts = [_dot_nt(vnb, t) for vnb, t in zip(vnbs, dsnbs)]
        dws = [_dot_nt(t, sb) for t, sb in zip(dvnbs, sbs)]
        ds2 = [_dot_tn(w_ref[0, :, b], t) for b, t in zip(blk, dvnbs)]
        for h in hs:
            b = blk[h]
            dqd_ref[0, :, b] = dqds[h]
            dqk_ref[0, :, b] = jnp.concatenate([dqks[h], jnp.zeros_like(dqks[h])], axis=1)
            dkt_ref[0, :, b] = dkts[h]
            du_ref[0, :, b] = dvns[h]
            dw_ref[0, :, b] = -dws[h]
            dgt_ref[0, 0, pl.ds(h, 1), :] = jnp.broadcast_to(jnp.sum(dsns[h] * ss[h], keepdims=True), (1, HEAD))
            ds_s[b, :] = dsns[h] * gt_ref[0, 0, pl.ds(h, 1), :] + ds1[h] - ds2[h]

    def mem(b, d, c):
        return _scan_chunk(b, d, nsc - 1 - c, dm)

    big = pl.BlockSpec((1, CHUNK, HD), lambda b, d, c: (d, mem(b, d, c), 0))
    gts = pl.BlockSpec((1, 1, C_HEADS, HEAD), lambda b, d, c: (d, mem(b, d, c), 0, 0))
    return pl.pallas_call(
        body, name=name, grid=(dm.Bl, 2, nsc),
        in_specs=[pl.BlockSpec((CHUNK, HD), lambda b, d, c: (mem(b, d, c), 0)),
                  pl.BlockSpec((1, 1, HD, HEAD), lambda b, d, c: (d, mem(b, d, c), 0, 0)), big, big, big, big, big, gts],
        out_specs=[big, big, big, big, big, gts],
        out_shape=[_sds((2, dm.T, HD), F32)] * 5 + [_sds((2, nch, C_HEADS, HEAD), F32)],
        scratch_shapes=[pltpu.VMEM((HD, HEAD), F32)], compiler_params=_cp(),
    )(dO, SS, U, W, KT, QD, QK, GT)


def _finish_fn(o, z, gain):
    y = o * lax.rsqrt(jnp.mean(o * o, axis=-1, keepdims=True) + EPS) * gain
    return y * _silu(z)


def gdn_finish(O, P1, og, dm, name):
    TM = dm.TM
    HD = C_HEADS * HEAD
    zc = (3 * HD) // HEAD

    def body(o0_ref, o1_ref, z_ref, g_ref, y_ref):
        y_ref[...] = _finish_fn(o0_ref[0] + o1_ref[0], z_ref[...], g_ref[...]).astype(BF16)

    return pl.pallas_call(
        body, name=name, grid=(dm.ntx, C_HEADS),
        in_specs=[pl.BlockSpec((1, TM, HEAD), lambda i, j: (0, i, j)), pl.BlockSpec((1, TM, HEAD), lambda i, j: (1, i, j)),
                  pl.BlockSpec((TM, HEAD), lambda i, j: (i, zc + j)), pl.BlockSpec((1, HEAD), lambda i, j: (0, 0))],
        out_specs=pl.BlockSpec((TM, HEAD), lambda i, j: (i, j)),
        out_shape=_sds((dm.Tx, HD), BF16), compiler_params=_cp(),
    )(O, O, P1, og)


def gdn_finish_bwd(O, P1, og, dY, dm, name):
    TM = dm.TM
    HD = C_HEADS * HEAD
    zc = (3 * HD) // HEAD

    def body(o0_ref, o1_ref, z_ref, g_ref, dy_ref, do_ref, dz_ref, dg_ref):
        i, j = pl.program_id(0), pl.program_id(1)
        _, vjp = jax.vjp(_finish_fn, o0_ref[0] + o1_ref[0], z_ref[...], g_ref[...])
        do, dz, dg = vjp(dy_ref[...])
        do_ref[...] = do
        dz_ref[...] = dz
        _acc(dg_ref, (slice(None), slice(None)), dg, (i == 0) & (j == 0))

    blk = pl.BlockSpec((TM, HEAD), lambda i, j: (i, j))
    return pl.pallas_call(
        body, name=name, grid=(dm.ntx, C_HEADS),
        in_specs=[pl.BlockSpec((1, TM, HEAD), lambda i, j: (0, i, j)), pl.BlockSpec((1, TM, HEAD), lambda i, j: (1, i, j)),
                  pl.BlockSpec((TM, HEAD), lambda i, j: (i, zc + j)), pl.BlockSpec((1, HEAD), lambda i, j: (0, 0)), blk],
        out_specs=[blk, blk, pl.BlockSpec((1, HEAD), lambda i, j: (0, 0))],
        out_shape=[_sds((dm.Tx, HD), F32), _sds((dm.Tx, HD), F32), _sds((1, HEAD), F32)],
        compiler_params=_cp(),
    )(O, O, P1, og, dY)


def adaln_fwd(c_ext, w_mod, b_loc, name):
    R, D = c_ext.shape
    nl = w_mod.shape[2]
    tn = _pick(nl, 384)

    def body(c_ref, w_ref, b_ref, o_ref):
        o_ref[0] = _dot(_silu(c_ref[...]), w_ref[0]) + b_ref[0]

    return pl.pallas_call(
        body, name=name, grid=(2, nl // tn),
        in_specs=[pl.BlockSpec((R, D), lambda l, j: (0, 0)), pl.BlockSpec((1, D, tn), lambda l, j: (l, 0, j)),
                  pl.BlockSpec((1, 1, tn), lambda l, j: (l, 0, j))],
        out_specs=pl.BlockSpec((1, R, tn), lambda l, j: (l, 0, j)),
        out_shape=_sds((2, R, nl), F32), compiler_params=_cp(),
    )(c_ext, w_mod, b_loc)


def adaln_bwd(c_ext, c_ctx, w_mod, dmx, dmh, nb, name):
    R, D = c_ext.shape
    nl = w_mod.shape[2]
    tn = _pick(nl, 384)
    nj = nl // tn

    def body(c_ref, cc_ref, w_ref, dmx_ref, dmh_ref, gw_ref, dc_ref):
        l, j = pl.program_id(0), pl.program_id(1)
        dh = dmh_ref[0, 0:1, :]
        for k in range(1, N_DEV):
            dh = dh + dmh_ref[0, k:k + 1, :]
        row = lax.broadcasted_iota(jnp.int32, (R, tn), 0)
        dmat = dmx_ref[0] + jnp.where(row == nb, dh, 0.0)
        gw_ref[0] = _dot_tn(_silu(c_ref[...]), dmat)
        part = _dot_nt(jnp.broadcast_to(dh, (8, tn)), w_ref[0])[0:1]
        _acc(dc_ref, (slice(None), slice(None)), part, (l == 0) & (j == 0))

        @pl.when((l == 1) & (j == nj - 1))
        def _():
            cc = cc_ref[...]
            sg = jax.nn.sigmoid(cc)
            dc_ref[...] = dc_ref[...] * (sg * (1.0 + cc * (1.0 - sg)))

    return pl.pallas_call(
        body, name=name, grid=(2, nj),
        in_specs=[pl.BlockSpec((R, D), lambda l, j: (0, 0)), pl.BlockSpec((1, D), lambda l, j: (0, 0)),
                  pl.BlockSpec((1, D, tn), lambda l, j: (l, 0, j)), pl.BlockSpec((1, R, tn), lambda l, j: (l, 0, j)),
                  pl.BlockSpec((1, N_DEV, tn), lambda l, j: (l, 0, j))],
        out_specs=[pl.BlockSpec((1, D, tn), lambda l, j: (l, 0, j)), pl.BlockSpec((1, D), lambda l, j: (0, 0))],
        out_shape=[_sds((2, D, nl), F32), _sds((1, D), F32)], compiler_params=_cp(),
    )(c_ext, c_ctx, w_mod, dmx, dmh)


def bmod_grad(dmx, dmh, name):
    _, R, n9 = dmx.shape

    def body(dmx_ref, dmh_ref, o_ref):
        o_ref[0] = jnp.sum(dmx_ref[0], axis=0, keepdims=True) + jnp.sum(dmh_ref[0], axis=0, keepdims=True)

    return pl.pallas_call(
        body, name=name, grid=(2,),
        in_specs=[pl.BlockSpec((1, R, n9), lambda l: (l, 0, 0)), pl.BlockSpec((1, N_DEV, n9), lambda l: (l, 0, 0))],
        out_specs=pl.BlockSpec((1, 1, n9), lambda l: (l, 0, 0)), out_shape=_sds((2, 1, n9), F32),
        compiler_params=_cp(),
    )(dmx, dmh)


def adamw(gs, w, m, v, name):
    S, R, C = gs.shape
    cap = max(8, (1 << 20) // (S * C))
    tr = R
    if R > cap:
        tr = max(t for t in range(8, cap + 1, 8) if R % t == 0)

    def body(g_ref, w_ref, m_ref, v_ref, go_ref, d_ref, mo_ref, vo_ref):
        g = g_ref[0].astype(F32)
        for k in range(1, S):
            g = g + g_ref[k].astype(F32)
        mn = ADAM_B1 * m_ref[...] + (1.0 - ADAM_B1) * g
        vn = ADAM_B2 * v_ref[...] + (1.0 - ADAM_B2) * jnp.square(g)
        m_hat = mn / (1.0 - ADAM_B1 ** ADAM_STEP)
        v_hat = vn / (1.0 - ADAM_B2 ** ADAM_STEP)
        go_ref[...] = g
        d_ref[...] = -ADAM_LR * (m_hat / (jnp.sqrt(v_hat) + ADAM_EPS) + ADAM_WD * w_ref[...])
        mo_ref[...] = mn
        vo_ref[...] = vn

    blk = pl.BlockSpec((tr, C), lambda i: (i, 0))
    return pl.pallas_call(
        body, name=name, grid=(R // tr,),
        in_specs=[pl.BlockSpec((S, tr, C), lambda i: (0, i, 0)), blk, blk, blk],
        out_specs=[blk] * 4, out_shape=[_sds((R, C), F32)] * 4, compiler_params=_cp(),
    )(gs, w, m, v)


def _gather_flat(parts, dtype, name):
    flat = jnp.concatenate([p.astype(dtype).reshape(-1) for p in parts])
    n = flat.shape[0]
    pad = (-n) % LANE
    if pad:
        flat = jnp.concatenate([flat, jnp.zeros((pad,), dtype)])
    got = all_gather([flat.reshape(-1, LANE)], name)[0].reshape(N_DEV, -1)
    out, off = [], 0
    for p in parts:
        out.append(got[:, off:off + p.size].reshape((N_DEV,) + p.shape))
        off += p.size
    return out


def _cols_full(g):
    return g.transpose(1, 0, 2).reshape(g.shape[1], -1)


def _cols_split(full):
    K = full.shape[0]
    return full.reshape(K, N_DEV, -1).transpose(1, 0, 2)


def kernel(x, c, ctx, c_ctx, w_mod, b_mod, norm_g, ffn_wg, ffn_wu, ffn_wd, ab_w_in, ab_q_norm, ab_k_norm, pool_w, pool_scale, ab_w_out, gdn_w_in, gdn_conv_w, gdn_a_log, gdn_dt_bias, gdn_o_norm, gdn_w_out, loss_target, m_c_ctx, m_w_mod, m_b_mod, m_norm_g, m_ffn_wg, m_ffn_wu, m_ffn_wd, m_ab_w_in, m_ab_q_norm, m_ab_k_norm, m_pool_w, m_pool_scale, m_ab_w_out, m_gdn_w_in, m_gdn_conv_w, m_gdn_a_log, m_gdn_dt_bias, m_gdn_o_norm, m_gdn_w_out, v_c_ctx, v_w_mod, v_b_mod, v_norm_g, v_ffn_wg, v_ffn_wu, v_ffn_wd, v_ab_w_in, v_ab_q_norm, v_ab_k_norm, v_pool_w, v_pool_scale, v_ab_w_out, v_gdn_w_in, v_gdn_conv_w, v_gdn_a_log, v_gdn_dt_bias, v_gdn_o_norm, v_gdn_w_out):
    Bl, N, D = x.shape
    M = ctx.shape[1]
    F = ffn_wd.shape[2] * N_DEV
    dm = Dims(Bl, N, M, D, F)
    TM, Tx, Th, T, G = dm.TM, dm.Tx, dm.Th, dm.T, dm.G
    HD = C_HEADS * HEAD
    me = 4 * lax.axis_index("x") + 2 * lax.axis_index("y") + lax.axis_index("c")
    nb = N_DEV * Bl
    R = -(-(nb + 1) // 8) * 8
    nl = w_mod.shape[2]
    n_gdn = gdn_w_in.shape[2] * N_DEV
    n_gdn_pad = -(-n_gdn // LANE) * LANE

    big = [w.astype(BF16) for w in (ffn_wg, ffn_wu, ffn_wd, ab_w_in, ab_w_out, gdn_w_in, gdn_w_out)]
    g_wg, g_wu, g_wd, g_abin, g_about, g_gin, g_gout = all_gather(big, "gather_weights")
    g_c, g_ng, g_cw = _gather_flat([c, norm_g, gdn_conv_w], F32, "gather_small")
    W_ABIN = _cols_full(g_abin[:, 0])
    W_ABOUT = g_about[:, 0].reshape(-1, D)
    W_GIN = jnp.pad(_cols_full(g_gin[:, 0]), ((0, 0), (0, n_gdn_pad - n_gdn)))
    W_GOUT = g_gout[:, 0].reshape(-1, D)
    gains = g_ng.transpose(1, 2, 0, 3).reshape(2, 3, 1, D)
    conv_w = g_cw[:, 0].transpose(1, 0, 2).reshape(3, -1)

    c_all = g_c.reshape(nb, D)
    c_ext = jnp.concatenate([c_all, c_ctx[None], jnp.zeros((R - nb - 1, D), F32)], 0)
    b_loc = lax.dynamic_slice_in_dim(b_mod, me * nl, nl, axis=1).reshape(2, 1, nl)
    mod_loc = adaln_fwd(c_ext, w_mod, b_loc, "adaln_fwd")
    (g_mod,) = _gather_flat([mod_loc], F32, "gather_mod")
    mod_full = g_mod.transpose(1, 2, 0, 3).reshape(2, R, 9 * D)
    MOD = []
    for l in range(2):
        mine = lax.dynamic_slice_in_dim(mod_full[l], me * Bl, Bl, axis=0)
        MOD.append(jnp.concatenate([mine, mod_full[l, nb:nb + 1]], 0).reshape(G, 9, D))

    dm5, dmr = dm.with_tile(512), dm.with_tile(1024)
    tabs = _rope_tables(dmr)
    qkg = jnp.stack([ab_q_norm, ab_k_norm])
    pw = pool_w[0].astype(BF16)
    pwT = pool_w[0].transpose(0, 2, 1).astype(BF16)
    par = jnp.stack([jnp.pad(jnp.pad(p[0], ((0, 0), (0, 8))).reshape(-1), (0, LANE - 32))
                     for p in (gdn_a_log, gdn_dt_bias)])

    X0 = jnp.concatenate([x.reshape(Tx, D), ctx.reshape(Th, D)], 0)
    def ffn(X, l, s, s0, all_rows, tag):
        return ffn_fwd(X, MOD[l], gains[l, 2 * s], g_wg, g_wu, g_wd, l, s, s0, dm, all_rows, "ffn_fwd_" + tag)

    X1, Y1 = ffn(X0, 0, 0, 0, True, "00")
    P0, XN0 = modmm(X1, MOD[0], gains[0, 1], W_ABIN, 3, dm5, "ab_in_proj")
    QKV = qkv_prep(P0, qkg, tabs, dmr, "qkv_prep")
    ATT = jnp.concatenate([attn_fwd(QKV, dm5, True, "attn_fwd_x"), attn_fwd(QKV, dm, False, "attn_fwd_h")], 0)
    POOL = jnp.concatenate([pool_fwd(P0, pw, pool_scale, dm, True, "pool_fwd_x"),
                            pool_fwd(P0, pw, pool_scale, dm, False, "pool_fwd_h")], 0)
    na = A_HEADS * HEAD
    X2, YM0 = proj_res([ATT, POOL], [W_ABOUT[:na], W_ABOUT[na:]], X1, MOD[0], dm5, dm5.nt, "ab_out_proj")
    X3, Y3 = ffn(X2, 0, 1, 6, True, "01")
    X4, Y4 = ffn(X3, 1, 0, 0, True, "10")
    P1, XN1 = modmm(X4, MOD[1], gains[1, 1], W_GIN, 3, dm5, "gdn_in_proj")
    QKVg = jnp.concatenate([gdn_prep(P1, conv_w, dm, True, "gdn_prep_x"), gdn_prep(P1, conv_w, dm, False, "gdn_prep_h")], 0)
    U, W, KT, QD, QK, GT, WF, AI = gdn_chunk_pre(QKVg, P1, par, dm, "gdn_chunk_pre")
    O, SS = gdn_scan_fwd(U, W, KT, QD, QK, GT, dm, "gdn_scan_fwd")
    FIN = gdn_finish(O, P1, gdn_o_norm, dmr, "gdn_finish")
    X5, YM1 = proj_res([FIN], [W_GOUT], X4, MOD[1], dm5, dm5.ntx, "gdn_out_proj")
    X6, Y6 = ffn(X5, 1, 1, 6, False, "11")
    lvec, dX6 = loss_head(X6, loss_target.reshape(Tx, D), dmr, "loss_head")
    loss = lax.psum(lvec[0, 0], AXES)

    zrow = lambda a: jnp.concatenate([a, jnp.zeros((G - a.shape[0],) + a.shape[1:], F32)], 0) if a.shape[0] < G else a

    def ffn_back(Xin, dXo, Y, l, s, s0, all_rows, tag):
        dXi, XNb, DOb, Hb, DGb, DUb, dmod, dgain = ffn_bwd(
            Xin, dXo, Y, MOD[l], gains[l, 2 * s], g_wg, g_wu, g_wd, l, s, s0, dm, all_rows, "ffn_bwd_" + tag)
        dwg, dwu, dwd = ffn_dw(XNb, DOb, Hb, DGb, DUb, dm, "ffn_dw_" + tag)
        return dXi, zrow(dmod), dgain, dwg, dwu, dwd

    dX5, dmod_12, dgain_12, dwg11, dwu11, dwd11 = ffn_back(X5, dX6, Y6, 1, 1, 6, False, "11")
    DY1, dFIN, dgate_1 = proj_res_bwd(dX5, YM1, MOD[1], [W_GOUT], dm5, dm5.ntx, "gdn_out_proj_bwd")
    d_gout = atb(FIN, DY1, Tx, dm, "gdn_dwout")
    dOsum, dZ, d_onorm = gdn_finish_bwd(O, P1, gdn_o_norm, dFIN, dmr, "gdn_finish_bwd")
    dO_all = jnp.concatenate([dOsum, jnp.zeros((Th, HD), F32)], 0)
    dU, dW, dKT, dQD, dQK, dGT = gdn_scan_bwd(dO_all, SS, U, W, KT, QD, QK, GT, dm, "gdn_scan_bwd")
    dQKVg, dAB, dPAR = gdn_chunk_pre_bwd(QKVg, P1, par, U, WF, AI, dU, dW, dKT, dQD, dQK, dGT, dm, "gdn_chunk_pre_bwd")
    dPx, dcw_x = gdn_prep_bwd(P1, dQKVg, conv_w, dm, True, "gdn_prep_bwd_x")
    dPh, dcw_h = gdn_prep_bwd(P1, dQKVg, conv_w, dm, False, "gdn_prep_bwd_h")
    d_conv = dcw_x + dcw_h
    dP1 = jnp.concatenate([jnp.concatenate([dPx, dPh], 0), jnp.concatenate([dZ, jnp.zeros((Th, HD), F32)], 0),
                           dAB[0] + dAB[1]], axis=1).astype(BF16)
    d_gin = atb(XN1, dP1, T, dm, "gdn_dwin")[:, :n_gdn]
    dX5_full = jnp.concatenate([dX5, jnp.zeros((Th, D), F32)], 0)
    dX4, dmod_11, dgain_11 = mixin_bwd(dP1, W_GIN, X4, dX5_full, MOD[1], gains[1, 1], 3, dm, "gdn_in_proj_bwd")
    early = [_cols_split(d_gin), d_gout.reshape(N_DEV, -1, D), _cols_split(d_conv)]
    lands = [lax.dynamic_update_index_in_dim(jnp.zeros_like(a), lax.dynamic_index_in_dim(a, me, 0, keepdims=False), me, 0)
             for a in early]
    e_send, e_recv, e_src, e_land, e_token = scatter_start(early, lands, "scatter_start_gdn")
    MOD[0] = MOD[0] + e_token[0, 0]
    dX3, dmod_10, dgain_10, dwg10, dwu10, dwd10 = ffn_back(X3, dX4, Y4, 1, 0, 0, True, "10")
    dMOD1 = jnp.concatenate([dmod_10, dmod_11, zrow(dgate_1), dmod_12], 1).reshape(G, 9 * D)

    dX2, dmod_02, dgain_02, dwg01, dwu01, dwd01 = ffn_back(X2, dX3, Y3, 0, 1, 6, True, "01")
    DY0, dATT, dPOOL, dgate_0 = proj_res_bwd(dX2, YM0, MOD[0], [W_ABOUT[:na], W_ABOUT[na:]], dm5, dm5.nt, "ab_out_proj_bwd")
    d_about = jnp.concatenate([atb(ATT, DY0, T, dm, "ab_dwout_a"), atb(POOL, DY0, T, dm, "ab_dwout_p")], 0)
    dUx, dpw_x, dps_x = pool_bwd(P0, dPOOL[:Tx], pw, pwT, pool_scale, dm, True, "pool_bwd_x")
    dUh, dpw_h, dps_h = pool_bwd(P0, dPOOL[Tx:], pw, pwT, pool_scale, dm, False, "pool_bwd_h")
    dQh, dKh0, dVh0 = attn_bwd(QKV, dATT[Tx:], dm, False, None, "attn_bwd_h")
    dQx, dKh, dVh, dKx, dVx = attn_bwd(QKV, dATT[:Tx], dm5, True, (dKh0, dVh0), "attn_bwd_x")
    dQKV = jnp.concatenate([jnp.concatenate([dQx, dQh], 0), jnp.concatenate([dKx, dKh], 0), jnp.concatenate([dVx, dVh], 0)], 1)
    dPqkv, d_qkg = qkv_prep_bwd(P0, dQKV, qkg, tabs, dmr, "qkv_prep_bwd")
    dP0 = jnp.concatenate([dPqkv, jnp.concatenate([dUx, dUh], 0)], 1).astype(BF16)
    d_abin = atb(XN0, dP0, T, dm, "ab_dwin")
    dX1, dmod_01, dgain_01 = mixin_bwd(dP0, W_ABIN, X1, dX2, MOD[0], gains[0, 1], 3, dm5, "ab_in_proj_bwd")
    dX0, dmod_00, dgain_00, dwg00, dwu00, dwd00 = ffn_back(X0, dX1, Y1, 0, 0, 0, True, "00")
    dMOD0 = jnp.concatenate([dmod_00, dmod_01, dgate_0, dmod_02], 1).reshape(G, 9 * D)
    grad_x = dX0[:Tx].reshape(Bl, N, D)

    d_ng = jnp.concatenate([dgain_00, dgain_01, dgain_02, dgain_10, dgain_11, dgain_12], 0)
    nf = ffn_wg.shape[3]
    parts = [jnp.concatenate([dwg00, dwg01, dwg10, dwg11], 1), jnp.concatenate([dwu00, dwu01, dwu10, dwu11], 1),
             jnp.concatenate([dwd00, dwd01, dwd10, dwd11], 1),
             _cols_split(d_abin), d_about.reshape(N_DEV, -1, D), _cols_split(d_ng)]
    gs_wg, gs_wu, gs_wd, gs_abin, gs_about, gs_ng = scatter_blocks(parts, "scatter_grads")
    gs_gin, gs_gout, gs_conv = scatter_wait(e_send, e_recv, e_src, e_land, dX0, "scatter_wait_gdn")

    d_alog = dPAR[0, :32].reshape(2, 16)[:, :8].reshape(1, 16)
    d_dtb = dPAR[1, :32].reshape(2, 16)[:, :8].reshape(1, 16)
    small = [d_qkg[0], d_qkg[1], (dpw_x + dpw_h).reshape(-1, HEAD), dps_x + dps_h, d_alog, d_dtb, d_onorm,
             jnp.stack([dMOD0, dMOD1])]
    gs_qn, gs_kn, gs_pw, gs_ps, gs_alog, gs_dtb, gs_on, g_dm = _gather_flat(small, F32, "gather_small_grads")
    dmx = g_dm[:, :, :Bl].transpose(1, 0, 2, 3).reshape(2, nb, 9 * D)
    dmx = jnp.concatenate([dmx, jnp.zeros((2, R - nb, 9 * D), F32)], 1)
    dmh = g_dm[:, :, Bl].transpose(1, 0, 2)
    cols_of_me = lambda a: lax.dynamic_slice_in_dim(a, me * nl, nl, axis=2)
    d_wmod, dcc = adaln_bwd(c_ext, c_ctx[None], w_mod, cols_of_me(dmx), cols_of_me(dmh), nb, "adaln_bwd")
    d_bmod = bmod_grad(dmx, dmh, "bmod_grad")
    (gs_cc,) = _gather_flat([dcc], F32, "gather_cctx_grad")

    def upd(gs, w, m, v, shape2, name):
        outs = adamw(gs.reshape((gs.shape[0],) + shape2), w.reshape(shape2), m.reshape(shape2), v.reshape(shape2), "adamw_" + name)
        return [o.reshape(w.shape) for o in outs]

    res = [
        upd(gs_cc, c_ctx, m_c_ctx, v_c_ctx, (1, D), "c_ctx"),
        upd(d_wmod[None], w_mod, m_w_mod, v_w_mod, (2 * D, nl), "w_mod"),
        upd(d_bmod[None], b_mod, m_b_mod, v_b_mod, (2, 9 * D), "b_mod"),
        upd(gs_ng, norm_g, m_norm_g, v_norm_g, (6, D // N_DEV), "norm_g"),
        upd(gs_wg, ffn_wg, m_ffn_wg, v_ffn_wg, (4 * D, nf), "ffn_wg"),
        upd(gs_wu, ffn_wu, m_ffn_wu, v_ffn_wu, (4 * D, nf), "ffn_wu"),
        upd(gs_wd, ffn_wd, m_ffn_wd, v_ffn_wd, (4 * nf, D), "ffn_wd"),
        upd(gs_abin, ab_w_in, m_ab_w_in, v_ab_w_in, (D, ab_w_in.shape[2]), "ab_w_in"),
        upd(gs_qn, ab_q_norm, m_ab_q_norm, v_ab_q_norm, (1, HEAD), "ab_q_norm"),
        upd(gs_kn, ab_k_norm, m_ab_k_norm, v_ab_k_norm, (1, HEAD), "ab_k_norm"),
        upd(gs_pw, pool_w, m_pool_w, v_pool_w, (len(POOL_WINDOWS) * HEAD, HEAD), "pool_w"),
        upd(gs_ps, pool_scale, m_pool_scale, v_pool_scale, (1, len(POOL_WINDOWS) * HEAD), "pool_scale"),
        upd(gs_about, ab_w_out, m_ab_w_out, v_ab_w_out, (ab_w_out.shape[1], D), "ab_w_out"),
        upd(gs_gin, gdn_w_in, m_gdn_w_in, v_gdn_w_in, (D, gdn_w_in.shape[2]), "gdn_w_in"),
        upd(gs_conv, gdn_conv_w, m_gdn_conv_w, v_gdn_conv_w, (3, gdn_conv_w.shape[2]), "gdn_conv_w"),
        upd(gs_alog, gdn_a_log, m_gdn_a_log, v_gdn_a_log, (1, 16), "gdn_a_log"),
        upd(gs_dtb, gdn_dt_bias, m_gdn_dt_bias, v_gdn_dt_bias, (1, 16), "gdn_dt_bias"),
        upd(gs_on, gdn_o_norm, m_gdn_o_norm, v_gdn_o_norm, (1, HEAD), "gdn_o_norm"),
        upd(gs_gout, gdn_w_out, m_gdn_w_out, v_gdn_w_out, (gdn_w_out.shape[1], D), "gdn_w_out"),
    ]
    return (loss, grad_x, *[r[0] for r in res], *[r[1] for r in res], *[r[2] for r in res], *[r[3] for r in res])
```

```python
import functools
from typing import NamedTuple

import jax
import jax.numpy as jnp
from jax import lax
from jax.experimental import pallas as pl
from jax.experimental.pallas import tpu as pltpu

F32, BF16 = jnp.float32, jnp.bfloat16
EPS = 1e-6
HEAD = 128
CHUNK = 64
GRID_W = 64
ROPE_THETA = 10000.0
POOL_WINDOWS = (2, 4, 8, 16)
A_HEADS, A_KV = 4, 2
C_HEADS = 8
N_DEV = 8
AXES = ("x", "y", "c")
ADAM_LR, ADAM_B1, ADAM_B2, ADAM_EPS, ADAM_WD, ADAM_STEP = 0.001, 0.9, 0.999, 1e-08, 0.01, 10
LANE = 128
VMEM_LIMIT = 56 * 1024 * 1024
NEG = -1e30


def _cp():
    return pltpu.CompilerParams(vmem_limit_bytes=VMEM_LIMIT)


def _sds(shape, dtype):
    return jax.ShapeDtypeStruct(tuple(shape), dtype)


def _dot(a, b):
    return jnp.dot(a.astype(BF16), b.astype(BF16), preferred_element_type=F32)


def _dot_nt(a, b):
    return lax.dot_general(a.astype(BF16), b.astype(BF16), (((1,), (1,)), ((), ())), preferred_element_type=F32)


def _dot_tn(a, b):
    return lax.dot_general(a.astype(BF16), b.astype(BF16), (((0,), (0,)), ((), ())), preferred_element_type=F32)


def _dot3(a, b, dims):
    ah, bh = a.astype(BF16), b.astype(BF16)
    al, bl = (a - ah.astype(F32)).astype(BF16), (b - bh.astype(F32)).astype(BF16)
    f = lambda x, y: lax.dot_general(x, y, (dims, ((), ())), preferred_element_type=F32)
    return f(ah, bh) + (f(ah, bl) + f(al, bh))


def _hdot(a, b):
    return _dot3(a, b, ((1,), (0,)))


def _hdot_nt(a, b):
    return _dot3(a, b, ((1,), (1,)))


def _hdot_tn(a, b):
    return _dot3(a, b, ((0,), (0,)))


def _pick(n, cap):
    if n <= cap:
        return n
    best = None
    for t in range(LANE, cap + 1, LANE):
        if n % t == 0:
            best = t
    assert best is not None, (n, cap)
    return best


class Dims(NamedTuple):
    Bl: int
    N: int
    M: int
    D: int
    F: int
    tm: int = 0

    @property
    def TM(self):
        return self.tm if self.tm else min(256, self.M)

    def with_tile(self, cap):
        return self._replace(tm=max(t for t in (1024, 512, 256, 128) if t <= cap and self.N % t == 0 and self.Th % t == 0))

    @property
    def Tx(self):
        return self.Bl * self.N

    @property
    def Th(self):
        return self.Bl * self.M

    @property
    def T(self):
        return self.Tx + self.Th

    @property
    def ntx(self):
        return self.Tx // self.TM

    @property
    def nt(self):
        return self.T // self.TM

    @property
    def tps(self):
        return self.N // self.TM

    @property
    def G(self):
        return self.Bl + 1


def _grp(i, dm, tm=None):
    tm = dm.TM if tm is None else tm
    return jnp.where(i < dm.Tx // tm, i // (dm.N // tm), dm.Bl)


def _first_of_group(i, dm, tm=None):
    tm = dm.TM if tm is None else tm
    return jnp.where(i < dm.Tx // tm, i % (dm.N // tm) == 0, i == dm.Tx // tm)


def _contraction_tile(rows):
    return max(t for t in (1024, 512, 256, 128) if rows % t == 0)


def _ffn_tile(dm):
    return max(t for t in (512, 256, 128) if dm.N % t == 0 and dm.Th % t == 0)


def _acc(ref, idx, val, first):
    @pl.when(first)
    def _():
        ref[idx] = val

    @pl.when(jnp.logical_not(first))
    def _():
        ref[idx] += val


def _modulate(x, gain, shift, scale):
    y = x * lax.rsqrt(jnp.mean(x * x, axis=-1, keepdims=True) + EPS)
    return (y * gain) * (1.0 + scale) + shift


def _silu(x):
    return x * jax.nn.sigmoid(x)


@functools.partial(jax.custom_vjp, nondiff_argnums=(1,))
def _shift_rows(a, k):
    n = a.shape[0]
    if k == 0:
        return a
    r = lax.broadcasted_iota(jnp.int32, a.shape, 0)
    rolled = pltpu.roll(a, (-k) % n, 0)
    ok = (r + k >= 0) & (r + k < n)
    return jnp.where(ok, rolled, 0.0)


def _shift_rows_fwd(a, k):
    return _shift_rows(a, k), None


def _shift_rows_bwd(k, _, d):
    return (_shift_rows(d, -k),)


_shift_rows.defvjp(_shift_rows_fwd, _shift_rows_bwd)


@functools.partial(jax.custom_vjp, nondiff_argnums=(1,))
def _roll_lanes(a, s):
    return pltpu.roll(a, s % LANE, 1)


def _roll_lanes_fwd(a, s):
    return _roll_lanes(a, s), None


def _roll_lanes_bwd(s, _, d):
    return (_roll_lanes(d, -s),)


_roll_lanes.defvjp(_roll_lanes_fwd, _roll_lanes_bwd)


def _rope(t, cs, sneg, spos):
    return t * cs + _roll_lanes(t, 96) * sneg + _roll_lanes(t, 32) * spos


def _rope_tables(dm):
    rows = dm.N // GRID_W
    row = jnp.repeat(jnp.arange(rows), GRID_W).astype(F32)
    col = jnp.tile(jnp.arange(GRID_W), rows).astype(F32)
    half = HEAD // 2
    inv_freq = jnp.power(ROPE_THETA, -jnp.arange(0, half, 2, dtype=F32) / half)
    ar, ac = row[:, None] * inv_freq, col[:, None] * inv_freq
    cs = jnp.concatenate([jnp.cos(ar), jnp.cos(ar), jnp.cos(ac), jnp.cos(ac)], axis=1)
    z = jnp.zeros_like(ar)
    sneg = jnp.concatenate([-jnp.sin(ar), z, -jnp.sin(ac), z], axis=1)
    spos = jnp.concatenate([z, jnp.sin(ar), z, jnp.sin(ac)], axis=1)
    pad1 = jnp.ones((dm.TM, HEAD), F32)
    pad0 = jnp.zeros((dm.TM, HEAD), F32)
    return (jnp.concatenate([cs, pad1], 0), jnp.concatenate([sneg, pad0], 0), jnp.concatenate([spos, pad0], 0))


def all_gather(xs, name):
    n = len(xs)

    def body(*refs):
        x_refs, out_refs = refs[:n], refs[n:2 * n]
        send_sems, recv_sems, local_sems = refs[2 * n:]
        x, y, c = lax.axis_index("x"), lax.axis_index("y"), lax.axis_index("c")
        me, sibling = (x, y, c), (x, y, 1 - c)
        chips = [(1 - x, y), (x, 1 - y), (1 - x, 1 - y)]

        def slot(a, px, py, pc):
            return out_refs[a].at[4 * px + 2 * py + pc]

        def copy(a, k, block, to, src=None):
            return pltpu.make_async_remote_copy(
                src_ref=slot(a, *block) if src is None else src, dst_ref=slot(a, *block),
                send_sem=send_sems.at[7 * a + k], recv_sem=recv_sems.at[7 * a + k],
                device_id=to, device_id_type=pl.DeviceIdType.MESH)

        mine = [pltpu.make_async_copy(x_refs[a], slot(a, *me), local_sems.at[a]) for a in range(n)]
        for cp in mine:
            cp.start()
        first = []
        for a in range(n):
            first.append(copy(a, 0, me, sibling, src=x_refs[a]))
            first += [copy(a, 1 + j, me, (*chip, c), src=x_refs[a]) for j, chip in enumerate(chips)]
        for cp in first:
            cp.start()
        passed = []
        for j, chip in enumerate(chips):
            for a in range(n):
                copy(a, 1 + j, (*chip, c), me).wait_recv()
                cp = copy(a, 4 + j, (*chip, c), sibling)
                cp.start()
                passed.append(cp)
        for a in range(n):
            copy(a, 0, sibling, me).wait_recv()
            for j, chip in enumerate(chips):
                copy(a, 4 + j, (*chip, 1 - c), me).wait_recv()
        for cp in first + passed:
            cp.wait_send()
        for cp in mine:
            cp.wait()

    anyspec = pl.BlockSpec(memory_space=pl.ANY)
    return pl.pallas_call(
        body, name=name, out_shape=[_sds((N_DEV,) + a.shape, a.dtype) for a in xs],
        in_specs=[anyspec] * n, out_specs=[anyspec] * n,
        scratch_shapes=[pltpu.SemaphoreType.DMA((7 * n,)), pltpu.SemaphoreType.DMA((7 * n,)),
                        pltpu.SemaphoreType.DMA((n,))],
    )(*xs)


def scatter_blocks(xs, name):
    n = len(xs)
    flips = [(0, 0, 1), (0, 1, 0), (0, 1, 1), (1, 0, 0), (1, 0, 1), (1, 1, 0), (1, 1, 1)]

    def body(*refs):
        x_refs, out_refs = refs[:n], refs[n:2 * n]
        send_sems, recv_sems, local_sems = refs[2 * n:]
        x, y, c = lax.axis_index("x"), lax.axis_index("y"), lax.axis_index("c")
        me = 4 * x + 2 * y + c

        def peer(f):
            return tuple(1 - v if d else v for v, d in zip((x, y, c), f))

        def lin(p):
            return 4 * p[0] + 2 * p[1] + p[2]

        mine = [pltpu.make_async_copy(x_refs[a].at[me], out_refs[a].at[me], local_sems.at[a]) for a in range(n)]
        for cp in mine:
            cp.start()
        copies = []
        for k, f in enumerate(flips):
            p = peer(f)
            for a in range(n):
                copies.append(pltpu.make_async_remote_copy(
                    src_ref=x_refs[a].at[lin(p)], dst_ref=out_refs[a].at[me],
                    send_sem=send_sems.at[7 * a + k], recv_sem=recv_sems.at[7 * a + k],
                    device_id=p, device_id_type=pl.DeviceIdType.MESH))
        for cp in copies:
            cp.start()
        for cp in copies:
            cp.wait_send()
            cp.wait_recv()
        for cp in mine:
            cp.wait()

    anyspec = pl.BlockSpec(memory_space=pl.ANY)
    return pl.pallas_call(
        body, name=name, out_shape=[_sds(a.shape, a.dtype) for a in xs],
        in_specs=[anyspec] * n, out_specs=[anyspec] * n,
        scratch_shapes=[pltpu.SemaphoreType.DMA((7 * n,)), pltpu.SemaphoreType.DMA((7 * n,)),
                        pltpu.SemaphoreType.DMA((n,))],
    )(*xs)


_FLIPS = [(0, 0, 1), (0, 1, 0), (0, 1, 1), (1, 0, 0), (1, 0, 1), (1, 1, 0), (1, 1, 1)]
_HBM = pl.BlockSpec(memory_space=pltpu.HBM)
_SEM = pl.BlockSpec(memory_space=pltpu.SEMAPHORE)
_EFFECT = pltpu.SideEffectType.DATAFLOW_SIDE_EFFECTING


def _scatter_copies(x_refs, land_refs, send_sems, recv_sems):
    x, y, c = lax.axis_index("x"), lax.axis_index("y"), lax.axis_index("c")
    me = 4 * x + 2 * y + c
    copies = []
    for k, f in enumerate(_FLIPS):
        p = tuple(1 - v if d else v for v, d in zip((x, y, c), f))
        for a in range(len(x_refs)):
            copies.append(pltpu.make_async_remote_copy(
                src_ref=x_refs[a].at[4 * p[0] + 2 * p[1] + p[2]], dst_ref=land_refs[a].at[me],
                send_sem=send_sems.at[7 * a + k], recv_sem=recv_sems.at[7 * a + k],
                device_id=p, device_id_type=pl.DeviceIdType.MESH))
    return copies


def scatter_start(xs, lands, name):
    n = len(xs)

    def body(*refs):
        for cp in _scatter_copies(refs[:n], refs[n:2 * n], refs[2 * n], refs[2 * n + 1]):
            cp.start()
        refs[-1][...] = jnp.zeros_like(refs[-1])

    hbm = lambda a: pltpu.HBM(a.shape, a.dtype)
    outs = pl.pallas_call(
        body, name=name,
        out_shape=(pltpu.SemaphoreType.DMA((7 * n,)), pltpu.SemaphoreType.DMA((7 * n,)),
                   *[hbm(a) for a in xs], *[hbm(a) for a in lands], _sds((8, LANE), F32)),
        in_specs=[_HBM] * (2 * n),
        out_specs=(_SEM, _SEM, *([_HBM] * (2 * n)), pl.BlockSpec(memory_space=pltpu.VMEM)),
        input_output_aliases={i: 2 + i for i in range(2 * n)},
        compiler_params=pltpu.CompilerParams(has_side_effects=_EFFECT),
    )(*[pltpu.with_memory_space_constraint(a, pltpu.HBM) for a in list(xs) + list(lands)])
    return outs[0], outs[1], outs[2:2 + n], outs[2 + n:2 + 2 * n], outs[-1]


def scatter_wait(send_sems, recv_sems, xs, lands, after, name):
    n = len(xs)

    def body(*refs):
        for cp in _scatter_copies(refs[:n], refs[n:2 * n], refs[2 * n], refs[2 * n + 1]):
            cp.wait_send()
            cp.wait_recv()

    hbm = lambda a: pltpu.HBM(a.shape, a.dtype)
    outs = pl.pallas_call(
        body, name=name, out_shape=tuple(hbm(a) for a in list(xs) + list(lands)),
        in_specs=[_HBM] * (2 * n) + [_SEM, _SEM, pl.BlockSpec(memory_space=pl.ANY)],
        out_specs=tuple([_HBM] * (2 * n)), input_output_aliases={i: i for i in range(2 * n)},
        compiler_params=pltpu.CompilerParams(has_side_effects=_EFFECT),
    )(*xs, *lands, send_sems, recv_sems, after)
    return outs[n:]


def _mod_spec(dm, nidx, tm=None):
    if nidx == 1:
        return pl.BlockSpec((1, 9, dm.D), lambda i: (_grp(i, dm, tm), 0, 0))
    return pl.BlockSpec((1, 9, dm.D), lambda i, k: (_grp(i, dm, tm), 0, 0))


def _wspec(shape5, l, s, ks):
    return pl.BlockSpec((ks, 1, 1) + tuple(shape5[3:]), lambda i, k: (k, l, s, 0, 0))


FFN_FWD_SHARDS = 4
FFN_BWD_SHARDS = 2


def ffn_fwd(X, MOD, gain, gwg, gwu, gwd, l, s, s0, dm, all_rows, name):
    D = dm.D
    tm = _ffn_tile(dm)
    rows = dm.T if all_rows else dm.Tx
    ks = FFN_FWD_SHARDS
    nk = N_DEV // ks

    def body(x_ref, m_ref, g_ref, wg_ref, wu_ref, wd_ref, xo_ref, y_ref, xn_s, acc_s):
        k = pl.program_id(1)

        @pl.when(k == 0)
        def _():
            m = m_ref[0]
            xn = _modulate(x_ref[...], g_ref[...], m[s0:s0 + 1], m[s0 + 1:s0 + 2])
            xn_s[...] = xn.astype(BF16)
            acc_s[...] = jnp.zeros_like(acc_s)

        xn = xn_s[...]
        y = None
        for j in range(0, ks, 2):
            wg2 = jnp.concatenate([wg_ref[j, 0, 0], wg_ref[j + 1, 0, 0]], axis=1)
            wu2 = jnp.concatenate([wu_ref[j, 0, 0], wu_ref[j + 1, 0, 0]], axis=1)
            wd2 = jnp.concatenate([wd_ref[j, 0, 0], wd_ref[j + 1, 0, 0]], axis=0)
            g = jnp.dot(xn, wg2, preferred_element_type=F32)
            u = jnp.dot(xn, wu2, preferred_element_type=F32)
            yj = jnp.dot((_silu(g) * u).astype(BF16), wd2, preferred_element_type=F32)
            y = yj if y is None else y + yj
        acc_s[...] += y

        @pl.when(k == nk - 1)
        def _():
            m = m_ref[0]
            y = acc_s[...]
            y_ref[...] = y
            xo_ref[...] = x_ref[...] + (0.5 * m[s0 + 2:s0 + 3]) * y

    row = pl.BlockSpec((tm, D), lambda i, k: (i, 0))
    return pl.pallas_call(
        body, name=name, grid=(rows // tm, nk),
        in_specs=[row, _mod_spec(dm, 2, tm), pl.BlockSpec((1, D), lambda i, k: (0, 0)),
                  _wspec(gwg.shape, l, s, ks), _wspec(gwu.shape, l, s, ks), _wspec(gwd.shape, l, s, ks)],
        out_specs=[row, row],
        out_shape=[_sds((rows, D), F32), _sds((rows, D), F32)],
        scratch_shapes=[pltpu.VMEM((tm, D), BF16), pltpu.VMEM((tm, D), F32)],
        compiler_params=_cp(),
    )(X, MOD, gain, gwg, gwu, gwd)


def ffn_bwd(X, dXo, Y, MOD, gain, gwg, gwu, gwd, l, s, s0, dm, all_rows, name):
    D = dm.D
    tm = _ffn_tile(dm)
    rows = dm.T if all_rows else dm.Tx
    ks = FFN_BWD_SHARDS
    nk = N_DEV // ks
    nf = gwg.shape[4]
    ngr = dm.G if all_rows else dm.Bl

    def body(x_ref, dxo_ref, y_ref, m_ref, g_ref, wg_ref, wu_ref, wd_ref,
             dxi_ref, xn_ref, do_ref, h_ref, dg_ref, du_ref, dm_ref, dgain_ref, xn_s, do_s, dxn_s):
        i, k = pl.program_id(0), pl.program_id(1)

        @pl.when(k == 0)
        def _():
            m = m_ref[0]
            xn = _modulate(x_ref[...], g_ref[...], m[s0:s0 + 1], m[s0 + 1:s0 + 2])
            xn_s[...] = xn.astype(BF16)
            do_s[...] = ((0.5 * m[s0 + 2:s0 + 3]) * dxo_ref[...]).astype(BF16)
            dxn_s[...] = jnp.zeros_like(dxn_s)

        xn, do = xn_s[...], do_s[...]
        dxn = None
        for p in range(ks // 2):
            j = 2 * p
            wg = jnp.concatenate([wg_ref[j, 0, 0], wg_ref[j + 1, 0, 0]], axis=1)
            wu = jnp.concatenate([wu_ref[j, 0, 0], wu_ref[j + 1, 0, 0]], axis=1)
            wd = jnp.concatenate([wd_ref[j, 0, 0], wd_ref[j + 1, 0, 0]], axis=0)
            g = jnp.dot(xn, wg, preferred_element_type=F32)
            u = jnp.dot(xn, wu, preferred_element_type=F32)
            sg = jax.nn.sigmoid(g)
            si = g * sg
            dh = _dot_nt(do, wd)
            dg = (dh * u * (sg * (1.0 + g * (1.0 - sg)))).astype(BF16)
            du = (dh * si).astype(BF16)
            dj = _dot_nt(dg, wg) + _dot_nt(du, wu)
            dxn = dj if dxn is None else dxn + dj
            h_ref[p] = (si * u).astype(BF16)
            dg_ref[p] = dg
            du_ref[p] = du
        dxn_s[...] += dxn

        @pl.when(k == nk - 1)
        def _():
            m = m_ref[0]
            _, vjp = jax.vjp(_modulate, x_ref[...], g_ref[...], m[s0:s0 + 1], m[s0 + 1:s0 + 2])
            dx, dgain, dshift, dscale = vjp(dxn_s[...])
            dxo = dxo_ref[...]
            dxi_ref[...] = dxo + dx
            xn_ref[...] = xn_s[...]
            do_ref[...] = do_s[...]
            dgate = jnp.sum(0.5 * dxo * y_ref[...], axis=0, keepdims=True)
            first = _first_of_group(i, dm, tm)
            _acc(dm_ref, (0, pl.ds(0, 1), slice(None)), dshift, first)
            _acc(dm_ref, (0, pl.ds(1, 1), slice(None)), dscale, first)
            _acc(dm_ref, (0, pl.ds(2, 1), slice(None)), dgate, first)
            _acc(dgain_ref, (slice(None), slice(None)), dgain, i == 0)

    row = pl.BlockSpec((tm, D), lambda i, k: (i, 0))
    slab = pl.BlockSpec((ks // 2, tm, 2 * nf), lambda i, k: (k, i, 0))
    pairs = _sds((N_DEV // 2, rows, 2 * nf), BF16)
    return pl.pallas_call(
        body, name=name, grid=(rows // tm, nk),
        in_specs=[row, row, row, _mod_spec(dm, 2, tm), pl.BlockSpec((1, D), lambda i, k: (0, 0)),
                  _wspec(gwg.shape, l, s, ks), _wspec(gwu.shape, l, s, ks), _wspec(gwd.shape, l, s, ks)],
        out_specs=[row, row, row, slab, slab, slab,
                   pl.BlockSpec((1, 3, D), lambda i, k: (_grp(i, dm, tm), 0, 0)),
                   pl.BlockSpec((1, D), lambda i, k: (0, 0))],
        out_shape=[_sds((rows, D), F32), _sds((rows, D), BF16), _sds((rows, D), BF16),
                   pairs, pairs, pairs,
                   _sds((ngr, 3, D), F32), _sds((1, D), F32)],
        scratch_shapes=[pltpu.VMEM((tm, D), BF16), pltpu.VMEM((tm, D), BF16), pltpu.VMEM((tm, D), F32)],
        compiler_params=_cp(),
    )(X, dXo, Y, MOD, gain, gwg, gwu, gwd)


def ffn_dw(XN, DO, H, DG, DU, dm, name):
    rows, D = XN.shape
    nf = H.shape[2] // 2
    tt = _contraction_tile(rows)
    nT = rows // tt

    def body(xn_ref, do_ref, h_ref, dg_ref, du_ref, dwg_ref, dwu_ref, dwd_ref, ag_s, au_s, ad_s):
        t = pl.program_id(1)

        @pl.when(t == 0)
        def _():
            ag_s[...] = jnp.zeros_like(ag_s)
            au_s[...] = jnp.zeros_like(au_s)
            ad_s[...] = jnp.zeros_like(ad_s)

        xn = xn_ref[...]
        ag_s[...] += _dot_tn(xn, dg_ref[0])
        au_s[...] += _dot_tn(xn, du_ref[0])
        ad_s[...] += _dot_tn(h_ref[0], do_ref[...])

        @pl.when(t == nT - 1)
        def _():
            for j in range(2):
                dwg_ref[j] = ag_s[:, pl.ds(j * nf, nf)].astype(BF16)
                dwu_ref[j] = au_s[:, pl.ds(j * nf, nf)].astype(BF16)
                dwd_ref[j] = ad_s[pl.ds(j * nf, nf), :].astype(BF16)

    row = pl.BlockSpec((tt, D), lambda k, t: (t, 0))
    slab = pl.BlockSpec((1, tt, 2 * nf), lambda k, t: (k, t, 0))
    return pl.pallas_call(
        body, name=name, grid=(N_DEV // 2, nT),
        in_specs=[row, row, slab, slab, slab],
        out_specs=[pl.BlockSpec((2, D, nf), lambda k, t: (k, 0, 0)), pl.BlockSpec((2, D, nf), lambda k, t: (k, 0, 0)),
                   pl.BlockSpec((2, nf, D), lambda k, t: (k, 0, 0))],
        out_shape=[_sds((N_DEV, D, nf), BF16), _sds((N_DEV, D, nf), BF16), _sds((N_DEV, nf, D), BF16)],
        scratch_shapes=[pltpu.VMEM((D, 2 * nf), F32), pltpu.VMEM((D, 2 * nf), F32), pltpu.VMEM((2 * nf, D), F32)],
        compiler_params=_cp(),
    )(XN, DO, H, DG, DU)


def atb(A, B, rows, dm, name):
    Ka, Nb = A.shape[1], B.shape[1]
    tk, tn = _pick(Ka, 1024), _pick(Nb, 1536)
    tt = _contraction_tile(rows)
    nT = rows // tt

    def body(a_ref, b_ref, o_ref, acc_s):
        t = pl.program_id(2)

        @pl.when(t == 0)
        def _():
            acc_s[...] = jnp.zeros_like(acc_s)

        acc_s[...] += _dot_tn(a_ref[...], b_ref[...])

        @pl.when(t == nT - 1)
        def _():
            o_ref[...] = acc_s[...].astype(BF16)

    return pl.pallas_call(
        body, name=name, grid=(Ka // tk, Nb // tn, nT),
        in_specs=[pl.BlockSpec((tt, tk), lambda i, j, t: (t, i)), pl.BlockSpec((tt, tn), lambda i, j, t: (t, j))],
        out_specs=pl.BlockSpec((tk, tn), lambda i, j, t: (i, j)),
        out_shape=_sds((Ka, Nb), BF16), scratch_shapes=[pltpu.VMEM((tk, tn), F32)], compiler_params=_cp(),
    )(A, B)


def modmm(X, MOD, gain, W, s0, dm, name):
    TM, D = dm.TM, dm.D
    Nc = W.shape[1]
    tn = _pick(Nc, 1536)
    nj = Nc // tn

    def body(x_ref, m_ref, g_ref, w_ref, p_ref, xn_ref):
        @pl.when(pl.program_id(1) == 0)
        def _():
            m = m_ref[0]
            xn_ref[...] = _modulate(x_ref[...], g_ref[...], m[s0:s0 + 1], m[s0 + 1:s0 + 2]).astype(BF16)

        p_ref[...] = jnp.dot(xn_ref[...], w_ref[...], preferred_element_type=F32)

    row = pl.BlockSpec((TM, D), lambda i, j: (i, 0))
    return pl.pallas_call(
        body, name=name, grid=(dm.nt, nj),
        in_specs=[row, _mod_spec(dm, 2), pl.BlockSpec((1, D), lambda i, j: (0, 0)),
                  pl.BlockSpec((D, tn), lambda i, j: (0, j))],
        out_specs=[pl.BlockSpec((TM, tn), lambda i, j: (i, j)), row],
        out_shape=[_sds((dm.T, Nc), F32), _sds((dm.T, D), BF16)],
        compiler_params=_cp(),
    )(X, MOD, gain, W)


def mixin_bwd(dP, W, X, dXres, MOD, gain, s0, dm, name):
    TM, D = dm.TM, dm.D
    K = dP.shape[1]

    def body(dp_ref, w_ref, x_ref, dr_ref, m_ref, g_ref, dx_ref, dm_ref, dgain_ref):
        i = pl.program_id(0)
        dxn = _dot_nt(dp_ref[...], w_ref[...])
        m = m_ref[0]
        _, vjp = jax.vjp(_modulate, x_ref[...], g_ref[...], m[s0:s0 + 1], m[s0 + 1:s0 + 2])
        dx, dgain, dshift, dscale = vjp(dxn)
        dx_ref[...] = dr_ref[...] + dx
        first = _first_of_group(i, dm)
        _acc(dm_ref, (0, pl.ds(0, 1), slice(None)), dshift, first)
        _acc(dm_ref, (0, pl.ds(1, 1), slice(None)), dscale, first)
        _acc(dgain_ref, (slice(None), slice(None)), dgain, i == 0)

    row = pl.BlockSpec((TM, D), lambda i: (i, 0))
    return pl.pallas_call(
        body, name=name, grid=(dm.nt,),
        in_specs=[pl.BlockSpec((TM, K), lambda i: (i, 0)), pl.BlockSpec((D, K), lambda i: (0, 0)), row, row,
                  _mod_spec(dm, 1), pl.BlockSpec((1, D), lambda i: (0, 0))],
        out_specs=[row, pl.BlockSpec((1, 2, D), lambda i: (_grp(i, dm), 0, 0)), pl.BlockSpec((1, D), lambda i: (0, 0))],
        out_shape=[_sds((dm.T, D), F32), _sds((dm.G, 2, D), F32), _sds((1, D), F32)],
        compiler_params=_cp(),
    )(dP, W, X, dXres, MOD, gain)


def proj_res(As, Ws, X, MOD, dm, ntiles, name):
    TM, D = dm.TM, dm.D
    n = len(As)
    rows = ntiles * TM

    def body(*refs):
        a_refs, w_refs = refs[:n], refs[n:2 * n]
        x_ref, m_ref, xo_ref, y_ref = refs[2 * n:]
        y = jnp.dot(a_refs[0][...], w_refs[0][...], preferred_element_type=F32)
        for a, w in zip(a_refs[1:], w_refs[1:]):
            y += jnp.dot(a[...], w[...], preferred_element_type=F32)
        y_ref[...] = y
        xo_ref[...] = x_ref[...] + m_ref[0][5:6] * y

    row = pl.BlockSpec((TM, D), lambda i: (i, 0))
    return pl.pallas_call(
        body, name=name, grid=(ntiles,),
        in_specs=[pl.BlockSpec((TM, a.shape[1]), lambda i: (i, 0)) for a in As]
        + [pl.BlockSpec(w.shape, lambda i: (0, 0)) for w in Ws] + [row, _mod_spec(dm, 1)],
        out_specs=[row, row], out_shape=[_sds((rows, D), F32), _sds((rows, D), F32)],
        compiler_params=_cp(),
    )(*As, *Ws, X, MOD)


def proj_res_bwd(dXo, Y, MOD, Ws, dm, ntiles, name):
    TM, D = dm.TM, dm.D
    n = len(Ws)
    rows = ntiles * TM
    ngr = dm.G if ntiles == dm.nt else dm.Bl

    def body(*refs):
        dxo_ref, y_ref, m_ref = refs[:3]
        w_refs = refs[3:3 + n]
        dy_ref = refs[3 + n]
        da_refs = refs[4 + n:4 + 2 * n]
        dgate_ref = refs[4 + 2 * n]
        i = pl.program_id(0)
        dxo = dxo_ref[...]
        dy = (m_ref[0][5:6] * dxo).astype(BF16)
        dy_ref[...] = dy
        for w, da in zip(w_refs, da_refs):
            da[...] = _dot_nt(dy, w[...])
        dgate = jnp.sum(dxo * y_ref[...], axis=0, keepdims=True)
        _acc(dgate_ref, (0, slice(None), slice(None)), dgate, _first_of_group(i, dm))

    row = pl.BlockSpec((TM, D), lambda i: (i, 0))
    return pl.pallas_call(
        body, name=name, grid=(ntiles,),
        in_specs=[row, row, _mod_spec(dm, 1)] + [pl.BlockSpec(w.shape, lambda i: (0, 0)) for w in Ws],
        out_specs=[row] + [pl.BlockSpec((TM, w.shape[0]), lambda i: (i, 0)) for w in Ws]
        + [pl.BlockSpec((1, 1, D), lambda i: (_grp(i, dm), 0, 0))],
        out_shape=[_sds((rows, D), BF16)] + [_sds((rows, w.shape[0]), F32) for w in Ws] + [_sds((ngr, 1, D), F32)],
        compiler_params=_cp(),
    )(dXo, Y, MOD, *Ws)


def loss_head(Xf, target, dm, name):
    TM, D = dm.TM, dm.D

    def body(x_ref, t_ref, l_ref, dx_ref, acc_s):
        i = pl.program_id(0)
        e = x_ref[...] - t_ref[...]
        dx_ref[...] = e * (1.0 / D)

        @pl.when(i == 0)
        def _():
            acc_s[...] = jnp.zeros_like(acc_s)

        acc_s[...] += jnp.sum(e * e, axis=0, keepdims=True)

        @pl.when(i == dm.ntx - 1)
        def _():
            tot = jnp.sum(acc_s[...], axis=1, keepdims=True) * (0.5 / D)
            l_ref[...] = jnp.broadcast_to(tot, (1, LANE))

    row = pl.BlockSpec((TM, D), lambda i: (i, 0))
    return pl.pallas_call(
        body, name=name, grid=(dm.ntx,), in_specs=[row, row],
        out_specs=[pl.BlockSpec((1, LANE), lambda i: (0, 0)), row],
        out_shape=[_sds((1, LANE), F32), _sds((dm.Tx, D), F32)],
        scratch_shapes=[pltpu.VMEM((1, D), F32)], compiler_params=_cp(),
    )(Xf, target)


def _qk_fn(p, gain, cs, sneg, spos):
    y = p * lax.rsqrt(jnp.mean(p * p, axis=-1, keepdims=True) + EPS) * gain
    return _rope(y, cs, sneg, spos)


def _tab_specs(dm, swap):
    def idx(i):
        return jnp.where(i < dm.ntx, i % dm.tps, dm.tps)
    if swap:
        return [pl.BlockSpec((dm.TM, HEAD), lambda j, i: (idx(i), 0))] * 3
    return [pl.BlockSpec((dm.TM, HEAD), lambda i, j: (idx(i), 0))] * 3


def qkv_prep(P0, qkg, tabs, dm, name):
    TM = dm.TM

    def body(p_ref, g_ref, cs_ref, sn_ref, sp_ref, o_ref):
        j = pl.program_id(1)

        @pl.when(j < 6)
        def _():
            o_ref[...] = _qk_fn(p_ref[...], g_ref[0], cs_ref[...], sn_ref[...], sp_ref[...]).astype(BF16)

        @pl.when(j >= 6)
        def _():
            o_ref[...] = p_ref[...].astype(BF16)

    blk = pl.BlockSpec((TM, HEAD), lambda i, j: (i, j))
    return pl.pallas_call(
        body, name=name, grid=(dm.nt, 8),
        in_specs=[blk, pl.BlockSpec((1, 1, HEAD), lambda i, j: (jnp.minimum(j // 4, 1), 0, 0))] + _tab_specs(dm, False),
        out_specs=blk, out_shape=_sds((dm.T, 8 * HEAD), BF16), compiler_params=_cp(),
    )(P0, qkg, *tabs)


def qkv_prep_bwd(P0, dQKV, qkg, tabs, dm, name):
    TM = dm.TM

    def body(p_ref, d_ref, g_ref, cs_ref, sn_ref, sp_ref, dp_ref, dg_ref):
        j, i = pl.program_id(0), pl.program_id(1)
        first = (i == 0) & ((j == 0) | (j == 4))

        @pl.when(j < 6)
        def _():
            _, vjp = jax.vjp(_qk_fn, p_ref[...], g_ref[0], cs_ref[...], sn_ref[...], sp_ref[...])
            dp, dg = vjp(d_ref[...])[:2]
            dp_ref[...] = dp
            _acc(dg_ref, (0, slice(None), slice(None)), dg, first)

        @pl.when(j >= 6)
        def _():
            dp_ref[...] = d_ref[...]

    blk = pl.BlockSpec((TM, HEAD), lambda j, i: (i, j))
    return pl.pallas_call(
        body, name=name, grid=(8, dm.nt),
        in_specs=[blk, blk, pl.BlockSpec((1, 1, HEAD), lambda j, i: (jnp.minimum(j // 4, 1), 0, 0))] + _tab_specs(dm, True),
        out_specs=[blk, pl.BlockSpec((1, 1, HEAD), lambda j, i: (jnp.minimum(j // 4, 1), 0, 0))],
        out_shape=[_sds((dm.T, 8 * HEAD), F32), _sds((2, 1, HEAD), F32)], compiler_params=_cp(),
    )(P0, dQKV, qkg, *tabs)


def _softmax2(sx, sh):
    m = jnp.max(sh, axis=-1, keepdims=True)
    if sx is not None:
        m = jnp.maximum(m, jnp.max(sx, axis=-1, keepdims=True))
    eh = jnp.exp(sh - m)
    l = jnp.sum(eh, axis=-1, keepdims=True)
    ex = None
    if sx is not None:
        ex = jnp.exp(sx - m)
        l = l + jnp.sum(ex, axis=-1, keepdims=True)
    inv = 1.0 / l
    return (None if ex is None else ex * inv), eh * inv


def _attn_geometry(dm, with_x):
    TQ = dm.TM
    if with_x:
        nq, qoff = dm.N // TQ, 0
    else:
        nq, qoff = dm.M // TQ, dm.Tx // TQ
    hoff = dm.Tx // dm.M
    return TQ, nq, qoff, hoff


def attn_fwd(QKV, dm, with_x, name):
    TQ, nq, qoff, hoff = _attn_geometry(dm, with_x)
    scale = HEAD ** -0.5
    rows = dm.Tx if with_x else dm.Th

    def body(*refs):
        if with_x:
            q_ref, kh_ref, vh_ref, kx_ref, vx_ref, o_ref = refs
        else:
            q_ref, kh_ref, vh_ref, o_ref = refs
        q = q_ref[...]
        sh = _dot_nt(q, kh_ref[...]) * scale
        sx = _dot_nt(q, kx_ref[...]) * scale if with_x else None
        px, ph = _softmax2(sx, sh)
        o = _dot(ph, vh_ref[...])
        if with_x:
            o = o + _dot(px, vx_ref[...])
        o_ref[...] = o.astype(BF16)

    qs = pl.BlockSpec((TQ, HEAD), lambda b, kv, g, qi: (qoff + b * nq + qi, kv * 2 + g))
    in_specs = [qs, pl.BlockSpec((dm.M, HEAD), lambda b, kv, g, qi: (hoff + b, 4 + kv)),
                pl.BlockSpec((dm.M, HEAD), lambda b, kv, g, qi: (hoff + b, 6 + kv))]
    args = [QKV, QKV, QKV]
    if with_x:
        in_specs += [pl.BlockSpec((dm.N, HEAD), lambda b, kv, g, qi: (b, 4 + kv)),
                     pl.BlockSpec((dm.N, HEAD), lambda b, kv, g, qi: (b, 6 + kv))]
        args += [QKV, QKV]
    return pl.pallas_call(
        body, name=name, grid=(dm.Bl, A_KV, 2, nq), in_specs=in_specs,
        out_specs=pl.BlockSpec((TQ, HEAD), lambda b, kv, g, qi: (b * nq + qi, kv * 2 + g)),
        out_shape=_sds((rows, A_HEADS * HEAD), BF16), compiler_params=_cp(),
    )(*args)


def attn_bwd(QKV, dO, dm, with_x, init, name):
    TQ, nq, qoff, hoff = _attn_geometry(dm, with_x)
    scale = HEAD ** -0.5
    rows = dm.Tx if with_x else dm.Th

    def body(*refs):
        if with_x:
            (q_ref, kh_ref, vh_ref, kx_ref, vx_ref, do_ref, ikh_ref, ivh_ref,
             dq_ref, dkh_ref, dvh_ref, dkx_ref, dvx_ref) = refs
        else:
            q_ref, kh_ref, vh_ref, do_ref, dq_ref, dkh_ref, dvh_ref = refs
        g, qi = pl.program_id(2), pl.program_id(3)
        q = q_ref[...]
        kh, vh = kh_ref[...], vh_ref[...]
        sh = _dot_nt(q, kh) * scale
        sx = _dot_nt(q, kx_ref[...]) * scale if with_x else None
        px, ph = _softmax2(sx, sh)
        dob = do_ref[...].astype(BF16)
        dph = _dot_nt(dob, vh)
        delta = jnp.sum(dph * ph, axis=-1, keepdims=True)
        if with_x:
            dpx = _dot_nt(dob, vx_ref[...])
            delta = delta + jnp.sum(dpx * px, axis=-1, keepdims=True)
        dsh = (ph * (dph - delta) * scale).astype(BF16)
        dq = _dot(dsh, kh)
        first = (g == 0) & (qi == 0)

        @pl.when(first)
        def _():
            if with_x:
                dkh_ref[...] = ikh_ref[...]
                dvh_ref[...] = ivh_ref[...]
                dkx_ref[...] = jnp.zeros_like(dkx_ref)
                dvx_ref[...] = jnp.zeros_like(dvx_ref)
            else:
                dkh_ref[...] = jnp.zeros_like(dkh_ref)
                dvh_ref[...] = jnp.zeros_like(dvh_ref)

        dkh_ref[...] += _dot_tn(dsh, q)
        dvh_ref[...] += _dot_tn(ph, dob)
        if with_x:
            dsx = (px * (dpx - delta) * scale).astype(BF16)
            dq = dq + _dot(dsx, kx_ref[...])
            dkx_ref[...] += _dot_tn(dsx, q)
            dvx_ref[...] += _dot_tn(px, dob)
        dq_ref[...] = dq

    qs = pl.BlockSpec((TQ, HEAD), lambda b, kv, g, qi: (qoff + b * nq + qi, kv * 2 + g))
    hs = lambda c0: pl.BlockSpec((dm.M, HEAD), lambda b, kv, g, qi: (hoff + b, c0 + kv))
    xs = lambda c0: pl.BlockSpec((dm.N, HEAD), lambda b, kv, g, qi: (b, c0 + kv))
    dos = pl.BlockSpec((TQ, HEAD), lambda b, kv, g, qi: (b * nq + qi, kv * 2 + g))
    acc_h = pl.BlockSpec((dm.M, HEAD), lambda b, kv, g, qi: (b, kv))
    acc_x = pl.BlockSpec((dm.N, HEAD), lambda b, kv, g, qi: (b, kv))
    in_specs, args = [qs, hs(4), hs(6)], [QKV, QKV, QKV]
    out_specs = [dos, acc_h, acc_h]
    out_shape = [_sds((rows, A_HEADS * HEAD), F32), _sds((dm.Th, A_KV * HEAD), F32), _sds((dm.Th, A_KV * HEAD), F32)]
    if with_x:
        in_specs += [xs(4), xs(6), dos, acc_h, acc_h]
        args += [QKV, QKV, dO, init[0], init[1]]
        out_specs += [acc_x, acc_x]
        out_shape += [_sds((dm.Tx, A_KV * HEAD), F32), _sds((dm.Tx, A_KV * HEAD), F32)]
    else:
        in_specs += [dos]
        args += [dO]
    return pl.pallas_call(
        body, name=name, grid=(dm.Bl, A_KV, 2, nq), in_specs=in_specs, out_specs=out_specs,
        out_shape=out_shape, compiler_params=_cp(),
    )(*args)


def _pool_mean(u, w):
    n = u.shape[0]
    t = lax.broadcasted_iota(jnp.int32, (n, 1), 0)
    cnt = (jnp.clip(t + (w - w // 2), 0, n) - jnp.clip(t - w // 2, 0, n)).astype(F32)
    s = _shift_rows(u, -(w // 2))
    for j in range(-(w // 2) + 1, w - w // 2):
        s = s + _shift_rows(u, j)
    return s / cnt - u


def pool_fwd(P0, pw, pscale, dm, on_x, name):
    n, off, rows = (dm.N, 0, dm.Tx) if on_x else (dm.M, dm.Tx // dm.M, dm.Th)
    ng = len(POOL_WINDOWS)

    def body(u_ref, w_ref, s_ref, o_ref):
        for g, w in enumerate(POOL_WINDOWS):
            cols = pl.ds(g * HEAD, HEAD)
            pooled = _pool_mean(u_ref[:, cols], w)
            o_ref[:, cols] = (_dot(pooled, w_ref[g]) * s_ref[:, cols]).astype(BF16)

    return pl.pallas_call(
        body, name=name, grid=(dm.Bl,),
        in_specs=[pl.BlockSpec((n, ng * HEAD), lambda b: (off + b, 2)),
                  pl.BlockSpec((ng, HEAD, HEAD), lambda b: (0, 0, 0)), pl.BlockSpec((1, ng * HEAD), lambda b: (0, 0))],
        out_specs=pl.BlockSpec((n, ng * HEAD), lambda b: (b, 0)),
        out_shape=_sds((rows, ng * HEAD), BF16), compiler_params=_cp(),
    )(P0, pw, pscale)


def pool_bwd(P0, dY, pw, pwT, pscale, dm, on_x, name):
    n, off, rows = (dm.N, 0, dm.Tx) if on_x else (dm.M, dm.Tx // dm.M, dm.Th)
    ng = len(POOL_WINDOWS)

    def body(u_ref, dy_ref, w_ref, wt_ref, s_ref, du_ref, dw_ref, ds_ref):
        b = pl.program_id(0)
        for g, w in enumerate(POOL_WINDOWS):
            cols = pl.ds(g * HEAD, HEAD)
            pooled, vjp = jax.vjp(lambda u: _pool_mean(u, w), u_ref[:, cols])
            pre = _dot(pooled, w_ref[g])
            dy = dy_ref[:, cols]
            dpre = dy * s_ref[:, cols]
            du_ref[:, cols] = vjp(_dot(dpre, wt_ref[g]))[0]
            _acc(dw_ref, (g, slice(None), slice(None)), _dot_tn(pooled, dpre), b == 0)
            _acc(ds_ref, (slice(None), cols), jnp.sum(dy * pre, axis=0, keepdims=True), b == 0)

    full = pl.BlockSpec((ng, HEAD, HEAD), lambda b: (0, 0, 0))
    vec = pl.BlockSpec((1, ng * HEAD), lambda b: (0, 0))
    return pl.pallas_call(
        body, name=name, grid=(dm.Bl,),
        in_specs=[pl.BlockSpec((n, ng * HEAD), lambda b: (off + b, 2)), pl.BlockSpec((n, ng * HEAD), lambda b: (b, 0)),
                  full, full, vec],
        out_specs=[pl.BlockSpec((n, ng * HEAD), lambda b: (b, 0)), full, vec],
        out_shape=[_sds((rows, ng * HEAD), F32), _sds((ng, HEAD, HEAD), F32), _sds((1, ng * HEAD), F32)],
        compiler_params=_cp(),
    )(P0, dY, pw, pwT, pscale)


def _conv_fn(p, w0, w1, w2, kind):
    c = w0 * _shift_rows(p, -1) + w1 * p + w2 * _shift_rows(p, 1)
    a = _silu(c)
    if kind == 2:
        return a
    a = a * lax.rsqrt(jnp.sum(a * a, axis=-1, keepdims=True) + EPS)
    return a * (HEAD ** -0.5) if kind == 0 else a


def gdn_prep(P1, conv_w, dm, on_x, name):
    n, off, rows = (dm.N, 0, dm.Tx) if on_x else (dm.M, dm.Tx // dm.M, dm.Th)

    def body(p_ref, w_ref, o_ref):
        j = pl.program_id(1)
        p, w = p_ref[...], w_ref[...]
        for kind in range(3):
            @pl.when(j // C_HEADS == kind)
            def _():
                o_ref[...] = _conv_fn(p, w[0:1], w[1:2], w[2:3], kind)

    return pl.pallas_call(
        body, name=name, grid=(dm.Bl, 3 * C_HEADS),
        in_specs=[pl.BlockSpec((n, HEAD), lambda b, j: (off + b, j)), pl.BlockSpec((3, HEAD), lambda b, j: (0, j))],
        out_specs=pl.BlockSpec((n, HEAD), lambda b, j: (b, j)),
        out_shape=_sds((rows, 3 * C_HEADS * HEAD), F32), compiler_params=_cp(),
    )(P1, conv_w)


def gdn_prep_bwd(P1, dQ, conv_w, dm, on_x, name):
    n, off, rows = (dm.N, 0, dm.Tx) if on_x else (dm.M, dm.Tx // dm.M, dm.Th)

    def body(p_ref, d0_ref, d1_ref, w_ref, dp_ref, dw_ref):
        j, b = pl.program_id(0), pl.program_id(1)
        p, w = p_ref[...], w_ref[...]
        for kind in range(3):
            @pl.when(j // C_HEADS == kind)
            def _():
                _, vjp = jax.vjp(functools.partial(_conv_fn, kind=kind), p, w[0:1], w[1:2], w[2:3])
                dp, d0, d1, d2 = vjp(d0_ref[0] + d1_ref[0])
                dp_ref[...] = dp
                _acc(dw_ref, (pl.ds(0, 1), slice(None)), d0, b == 0)
                _acc(dw_ref, (pl.ds(1, 1), slice(None)), d1, b == 0)
                _acc(dw_ref, (pl.ds(2, 1), slice(None)), d2, b == 0)

    return pl.pallas_call(
        body, name=name, grid=(3 * C_HEADS, dm.Bl),
        in_specs=[pl.BlockSpec((n, HEAD), lambda j, b: (off + b, j)),
                  pl.BlockSpec((1, n, HEAD), lambda j, b: (0, off + b, j)), pl.BlockSpec((1, n, HEAD), lambda j, b: (1, off + b, j)),
                  pl.BlockSpec((3, HEAD), lambda j, b: (0, j))],
        out_specs=[pl.BlockSpec((n, HEAD), lambda j, b: (b, j)), pl.BlockSpec((3, HEAD), lambda j, b: (0, j))],
        out_shape=[_sds((rows, 3 * C_HEADS * HEAD), F32), _sds((3, 3 * C_HEADS * HEAD), F32)],
        compiler_params=_cp(),
    )(P1, dQ, dQ, conv_w)


def _gate_fn(ab, par):
    lane = lax.broadcasted_iota(jnp.int32, ab.shape, 1)
    is_a = (lane % 16) < C_HEADS
    g = -jnp.exp(par[0:1]) * jax.nn.softplus(ab + par[1:2])
    return jnp.where(lane < 4 * C_HEADS, jnp.where(is_a, g, jax.nn.sigmoid(ab)), 0.0)


def _col(blk, idx):
    lane = lax.broadcasted_iota(jnp.int32, blk.shape, 1)
    return jnp.sum(jnp.where(lane == idx, blk, 0.0), axis=1, keepdims=True)


def _chunk_masks(rev):
    ii = lax.broadcasted_iota(jnp.int32, (CHUNK, CHUNK), 0)
    jj = lax.broadcasted_iota(jnp.int32, (CHUNK, CHUNK), 1)
    ahead = jnp.where(rev, jj - ii, ii - jj)
    return ahead >= 0, ahead > 0, (ii == jj).astype(F32)


def _inv_unit_tri(nmats, eye):
    xs = [eye - n for n in nmats]
    ps = [_hdot(n, n) for n in nmats]
    step = 2
    while True:
        xs = [x + _hdot(x, p) for x, p in zip(xs, ps)]
        step *= 2
        if step >= CHUNK:
            break
        ps = [_hdot(p, p) for p in ps]
    return xs


def _cum_lanes(x, transpose=False):
    lane = lax.broadcasted_iota(jnp.int32, x.shape, 1)
    down, up = x, x
    s = 1
    while s < CHUNK:
        down = down + _shift_rows(down, -s)
        up = up + _shift_rows(up, s)
        s *= 2
    return jnp.where((lane >= 16) if transpose else (lane < 16), down, up)


def _each(f, *lists):
    return [f(*a) for a in zip(*lists)]


def _chunk_common(qs, ks, vs, gcs, gcrs, tots, betas, rev, saved=None):
    incl, strict, eye = _chunk_masks(rev)
    es = _each(lambda gc, gcr: jnp.exp(jnp.where(incl, gc - gcr, NEG)), gcs, gcrs)
    egs = [jnp.exp(gc) for gc in gcs]
    ets = _each(lambda t, gc: jnp.exp(t - gc), tots, gcs)
    gts = [jnp.exp(t) for t in tots]
    kbs = _each(lambda k, b: k * b, ks, betas)
    kks = _each(_dot_nt, kbs, ks)
    qqs = _each(_dot_nt, qs, ks)
    if saved is None:
        nmats = _each(lambda kk, e: jnp.where(strict, kk * e, 0.0), kks, es)
        ainvs = _inv_unit_tri(nmats, eye)
        rhss = _each(lambda v, b, kb, eg: jnp.concatenate([v * b, kb * eg], axis=1), vs, betas, kbs, egs)
        sols = _each(_hdot, ainvs, rhss)
    else:
        ainvs, sols = saved
    return dict(incl=incl, strict=strict, e=es, eg=egs, et=ets, gt=gts, kb=kbs, kk=kks, ainv=ainvs, sol=sols, qq=qqs)


def _chunk_fwd(qs, ks, vs, gcs, gcrs, tots, betas, rev):
    c = _chunk_common(qs, ks, vs, gcs, gcrs, tots, betas, rev)
    incl = c["incl"]
    return _each(lambda q, k, sol, qq, e, et, eg, gt, ainv:
                 (sol[:, :HEAD], sol[:, HEAD:], k * et, q * eg, jnp.where(incl, qq * e, 0.0), gt, ainv),
                 qs, ks, c["sol"], c["qq"], c["e"], c["et"], c["eg"], c["gt"], c["ainv"])


def _chunk_bwd(qs, ks, vs, gcs, gcrs, tots, betas, rev, ainvs, sols, dus, dws, dkts, dqds, dqks, dgts):
    c = _chunk_common(qs, ks, vs, gcs, gcrs, tots, betas, rev, saved=(ainvs, sols))
    incl, strict = c["incl"], c["strict"]
    drhss = _each(lambda a, du, dw: _hdot_tn(a, jnp.concatenate([du, dw], axis=1)), c["ainv"], dus, dws)
    dns = _each(lambda drhs, sol: jnp.where(strict, -_hdot_nt(drhs, sol), 0.0), drhss, c["sol"])
    dkks = _each(lambda dn, e: dn * e, dns, c["e"])
    dqms = [jnp.where(incl, dqk, 0.0) for dqk in dqks]
    dqqs = _each(lambda dqm, e: dqm * e, dqms, c["e"])
    m_q = _each(_dot, dqqs, ks)
    m_k1 = _each(_dot_tn, dqqs, qs)
    m_k2 = _each(_dot_tn, dkks, c["kb"])
    m_kb = _each(_dot, dkks, ks)

    def finish(q, k, v, beta, e, eg, et, gt, kb, kk, qq, drhs, dn, dqm, dkt, dqd, dgt, mq, mk1, mk2, mkb):
        de = dn * kk + dqm * qq
        dq = mq + dqd * eg
        dkb = mkb + drhs[:, HEAD:] * eg
        dk = mk1 + mk2 + dkt * et + dkb * beta
        dv = drhs[:, :HEAD] * beta
        dbeta = jnp.sum(drhs[:, :HEAD] * v + dkb * k, axis=1, keepdims=True)
        deg = jnp.sum(drhs[:, HEAD:] * kb + dqd * q, axis=1, keepdims=True)
        dd = de * e
        dtd = jnp.sum(dkt * k, axis=1, keepdims=True) * et
        dgc = deg * eg - dtd + jnp.sum(dd, axis=1, keepdims=True) - jnp.sum(dd.T, axis=1, keepdims=True)
        dtot = jnp.sum(dtd, axis=0, keepdims=True) + dgt * gt
        return dq, dk, dv, dgc, dtot, dbeta

    return _each(finish, qs, ks, vs, betas, c["e"], c["eg"], c["et"], c["gt"], c["kb"], c["kk"], c["qq"],
                 drhss, dns, dqms, dkts, dqds, dgts, m_q, m_k1, m_k2, m_kb)


def gdn_chunk_pre(QKVg, P1, par, dm, name):
    nch = dm.T // CHUNK
    HD = C_HEADS * HEAD
    abcol = (4 * HD) // LANE

    def body(x_ref, ab_ref, par_ref, u_ref, w_ref, kt_ref, qd_ref, qk_ref, gt_ref, wf_ref, ai_ref, gct_s):
        d = pl.program_id(1)
        rev = d == 1
        gb = _gate_fn(ab_ref[...], par_ref[...])
        gcl = _cum_lanes(gb)
        gct_s[...] = gcl.T
        tot = jnp.sum(gb, axis=0, keepdims=True)
        hs = range(C_HEADS)
        outs = _chunk_fwd(
            [x_ref[:, pl.ds(h * HEAD, HEAD)] for h in hs],
            [x_ref[:, pl.ds((C_HEADS + h) * HEAD, HEAD)] for h in hs],
            [x_ref[:, pl.ds((2 * C_HEADS + h) * HEAD, HEAD)] for h in hs],
            [_col(gcl, d * 16 + h) for h in hs], [gct_s[pl.ds(d * 16 + h, 1), :] for h in hs],
            [_col(tot, d * 16 + h) for h in hs], [_col(gb, d * 16 + 8 + h) for h in hs], rev)
        for h, (u, w, kt, qd, qk, gt, ainv) in enumerate(outs):
            cols = pl.ds(h * HEAD, HEAD)
            u_ref[0, :, cols] = u
            w_ref[0, :, cols] = w.astype(BF16)
            kt_ref[0, :, cols] = kt.astype(BF16)
            qd_ref[0, :, cols] = qd.astype(BF16)
            qk_ref[0, :, cols] = jnp.concatenate([qk, jnp.zeros_like(qk)], axis=1).astype(BF16)
            gt_ref[0, 0, pl.ds(h, 1), :] = jnp.broadcast_to(gt, (1, HEAD))
            wf_ref[0, :, cols] = w
            ai_ref[0, :, cols] = jnp.concatenate([ainv, jnp.zeros_like(ainv)], axis=1)

    big = pl.BlockSpec((1, CHUNK, HD), lambda i, d: (d, i, 0))
    return pl.pallas_call(
        body, name=name, grid=(nch, 2),
        in_specs=[pl.BlockSpec((CHUNK, 3 * HD), lambda i, d: (i, 0)), pl.BlockSpec((CHUNK, LANE), lambda i, d: (i, abcol)),
                  pl.BlockSpec((2, LANE), lambda i, d: (0, 0))],
        out_specs=[big, big, big, big, big, pl.BlockSpec((1, 1, C_HEADS, HEAD), lambda i, d: (d, i, 0, 0)), big, big],
        out_shape=[_sds((2, dm.T, HD), F32), _sds((2, dm.T, HD), BF16), _sds((2, dm.T, HD), BF16),
                   _sds((2, dm.T, HD), BF16), _sds((2, dm.T, HD), BF16), _sds((2, nch, C_HEADS, HEAD), F32),
                   _sds((2, dm.T, HD), F32), _sds((2, dm.T, HD), F32)],
        scratch_shapes=[pltpu.VMEM((LANE, CHUNK), F32)], compiler_params=_cp(),
    )(QKVg, P1, par)


def gdn_chunk_pre_bwd(QKVg, P1, par, U, WF, AI, dU, dW, dKT, dQD, dQK, dGT, dm, name):
    nch = dm.T // CHUNK
    HD = C_HEADS * HEAD
    abcol = (4 * HD) // LANE

    def body(x_ref, ab_ref, par_ref, u_ref, wf_ref, ai_ref, du_ref, dw_ref, dkt_ref, dqd_ref, dqk_ref, dgt_ref,
             dx_ref, dab_ref, dpar_ref, gct_s):
        i, d = pl.program_id(0), pl.program_id(1)
        rev = d == 1
        ab, par = ab_ref[...], par_ref[...]
        gb, gate_vjp = jax.vjp(_gate_fn, ab, par)
        gcl = _cum_lanes(gb)
        gct_s[...] = gcl.T
        tot = jnp.sum(gb, axis=0, keepdims=True)
        lane = lax.broadcasted_iota(jnp.int32, (CHUNK, LANE), 1)
        dgcl = jnp.zeros((CHUNK, LANE), F32)
        dgb = jnp.zeros((CHUNK, LANE), F32)
        first = d == 0
        hs = range(C_HEADS)
        hcols = [pl.ds(h * HEAD, HEAD) for h in hs]
        outs = _chunk_bwd(
            [x_ref[:, c] for c in hcols],
            [x_ref[:, pl.ds((C_HEADS + h) * HEAD, HEAD)] for h in hs],
            [x_ref[:, pl.ds((2 * C_HEADS + h) * HEAD, HEAD)] for h in hs],
            [_col(gcl, d * 16 + h) for h in hs], [gct_s[pl.ds(d * 16 + h, 1), :] for h in hs],
            [_col(tot, d * 16 + h) for h in hs], [_col(gb, d * 16 + 8 + h) for h in hs], rev,
            [ai_ref[0, :, pl.ds(h * HEAD, CHUNK)] for h in hs],
            [jnp.concatenate([u_ref[0, :, c], wf_ref[0, :, c]], axis=1) for c in hcols],
            [du_ref[0, :, c] for c in hcols], [dw_ref[0, :, c] for c in hcols], [dkt_ref[0, :, c] for c in hcols],
            [dqd_ref[0, :, c] for c in hcols], [dqk_ref[0, :, pl.ds(h * HEAD, CHUNK)] for h in hs],
            [dgt_ref[0, 0, pl.ds(h, 1), pl.ds(0, 1)] for h in hs])
        for h, (dq, dk, dv, dgc, dtotal, dbeta) in enumerate(outs):
            idx = d * 16 + h
            dx_ref[0, :, hcols[h]] = dq
            dx_ref[0, :, pl.ds((C_HEADS + h) * HEAD, HEAD)] = dk
            dx_ref[0, :, pl.ds((2 * C_HEADS + h) * HEAD, HEAD)] = dv
            dgcl = dgcl + jnp.where(lane == idx, dgc, 0.0)
            dgb = dgb + jnp.where(lane == idx + 8, dbeta, 0.0) + jnp.where(lane == idx, dtotal, 0.0)
        dab, dpar = gate_vjp(dgb + _cum_lanes(dgcl, transpose=True))
        dab_ref[0] = dab
        _acc(dpar_ref, (slice(None), slice(None)), dpar, (i == 0) & first)

    big = pl.BlockSpec((1, CHUNK, HD), lambda i, d: (d, i, 0))
    return pl.pallas_call(
        body, name=name, grid=(nch, 2),
        in_specs=[pl.BlockSpec((CHUNK, 3 * HD), lambda i, d: (i, 0)), pl.BlockSpec((CHUNK, LANE), lambda i, d: (i, abcol)),
                  pl.BlockSpec((2, LANE), lambda i, d: (0, 0)), big, big, big, big, big, big, big, big,
                  pl.BlockSpec((1, 1, C_HEADS, HEAD), lambda i, d: (d, i, 0, 0))],
        out_specs=[pl.BlockSpec((1, CHUNK, 3 * HD), lambda i, d: (d, i, 0)), pl.BlockSpec((1, CHUNK, LANE), lambda i, d: (d, i, 0)),
                   pl.BlockSpec((2, LANE), lambda i, d: (0, 0))],
        out_shape=[_sds((2, dm.T, 3 * HD), F32), _sds((2, dm.T, LANE), F32), _sds((2, LANE), F32)],
        scratch_shapes=[pltpu.VMEM((LANE, CHUNK), F32)], compiler_params=_cp(),
    )(QKVg, P1, par, U, WF, AI, dU, dW, dKT, dQD, dQK, dGT)


def _scan_chunk(b, d, c, dm):
    nh, nx = dm.M // CHUNK, dm.N // CHUNK
    in_h = c < nh
    pos_h = jnp.where(d == 0, c, nh - 1 - c)
    pos_x = jnp.where(d == 0, c - nh, nx - 1 - (c - nh))
    return jnp.where(in_h, dm.Tx // CHUNK + b * nh + pos_h, b * nx + pos_x)


def gdn_scan_fwd(U, W, KT, QD, QK, GT, dm, name):
    nch = dm.T // CHUNK
    HD = C_HEADS * HEAD
    nsc = (dm.M + dm.N) // CHUNK

    def body(u_ref, w_ref, kt_ref, qd_ref, qk_ref, gt_ref, o_ref, ss_ref, s_s):
        @pl.when(pl.program_id(2) == 0)
        def _():
            s_s[...] = jnp.zeros_like(s_s)

        hs = range(C_HEADS)
        blk = [pl.ds(h * HEAD, HEAD) for h in hs]
        ss = [s_s[b, :] for b in blk]
        sbs = [s.astype(BF16) for s in ss]
        for b, sb in zip(blk, sbs):
            ss_ref[0, 0, b, :] = sb
        ws = [jnp.dot(w_ref[0, :, b], sb, preferred_element_type=F32) for b, sb in zip(blk, sbs)]
        os1 = [jnp.dot(qd_ref[0, :, b], sb, preferred_element_type=F32) for b, sb in zip(blk, sbs)]
        vnbs = [(u_ref[0, :, b] - wv).astype(BF16) for b, wv in zip(blk, ws)]
        os2 = [jnp.dot(qk_ref[0, :, pl.ds(h * HEAD, CHUNK)], vnbs[h], preferred_element_type=F32) for h in hs]
        upd = [_dot_tn(kt_ref[0, :, b], vnb) for b, vnb in zip(blk, vnbs)]
        for h in hs:
            o_ref[0, :, blk[h]] = os1[h] + os2[h]
            s_s[blk[h], :] = ss[h] * gt_ref[0, 0, pl.ds(h, 1), :] + upd[h]

    big = pl.BlockSpec((1, CHUNK, HD), lambda b, d, c: (d, _scan_chunk(b, d, c, dm), 0))
    return pl.pallas_call(
        body, name=name, grid=(dm.Bl, 2, nsc),
        in_specs=[big, big, big, big, big,
                  pl.BlockSpec((1, 1, C_HEADS, HEAD), lambda b, d, c: (d, _scan_chunk(b, d, c, dm), 0, 0))],
        out_specs=[big, pl.BlockSpec((1, 1, HD, HEAD), lambda b, d, c: (d, _scan_chunk(b, d, c, dm), 0, 0))],
        out_shape=[_sds((2, dm.T, HD), F32), _sds((2, nch, HD, HEAD), BF16)],
        scratch_shapes=[pltpu.VMEM((HD, HEAD), F32)], compiler_params=_cp(),
    )(U, W, KT, QD, QK, GT)


def gdn_scan_bwd(dO, SS, U, W, KT, QD, QK, GT, dm, name):
    nch = dm.T // CHUNK
    HD = C_HEADS * HEAD
    nsc = (dm.M + dm.N) // CHUNK

    def body(do_ref, ss_ref, u_ref, w_ref, kt_ref, qd_ref, qk_ref, gt_ref,
             du_ref, dw_ref, dkt_ref, dqd_ref, dqk_ref, dgt_ref, ds_s):
        @pl.when(pl.program_id(2) == 0)
        def _():
            ds_s[...] = jnp.zeros_like(ds_s)

        hs = range(C_HEADS)
        blk = [pl.ds(h * HEAD, HEAD) for h in hs]
        sbs = [ss_ref[0, 0, b, :] for b in blk]
        ss = [s.astype(F32) for s in sbs]
        dobs = [do_ref[:, b].astype(BF16) for b in blk]
        dsns = [ds_s[b, :] for b in blk]
        dsnbs = [t.astype(BF16) for t in dsns]
        wss = [jnp.dot(w_ref[0, :, b], sb, preferred_element_type=F32) for b, sb in zip(blk, sbs)]
        dqds = [_dot_nt(dob, sb) for dob, sb in zip(dobs, sbs)]
        dv1 = [_dot_tn(qk_ref[0, :, pl.ds(h * HEAD, CHUNK)], dobs[h]) for h in hs]
        dv2 = [jnp.dot(kt_ref[0, :, b], t, preferred_element_type=F32) for b, t in zip(blk, dsnbs)]
        ds1 = [_dot_tn(qd_ref[0, :, b], dob) for b, dob in zip(blk, dobs)]
        vnbs = [(u_ref[0, :, b] - wv).astype(BF16) for b, wv in zip(blk, wss)]
        dvns = [a + b for a, b in zip(dv1, dv2)]
        dvnbs = [t.astype(BF16) for t in dvns]
        dqks = [_dot_nt(dob, vnb) for dob, vnb in zip(dobs, vnbs)]
        dkts = [_dot_nt(vnb, t) for vnb, t in zip(vnbs, dsnbs)]
        dws = [_dot_nt(t, sb) for t, sb in zip(dvnbs, sbs)]
        ds2 = [_dot_tn(w_ref[0, :, b], t) for b, t in zip(blk, dvnbs)]
        for h in hs:
            b = blk[h]
            dqd_ref[0, :, b] = dqds[h]
            dqk_ref[0, :, b] = jnp.concatenate([dqks[h], jnp.zeros_like(dqks[h])], axis=1)
            dkt_ref[0, :, b] = dkts[h]
            du_ref[0, :, b] = dvns[h]
            dw_ref[0, :, b] = -dws[h]
            dgt_ref[0, 0, pl.ds(h, 1), :] = jnp.broadcast_to(jnp.sum(dsns[h] * ss[h], keepdims=True), (1, HEAD))
            ds_s[b, :] = dsns[h] * gt_ref[0, 0, pl.ds(h, 1), :] + ds1[h] - ds2[h]

    def mem(b, d, c):
        return _scan_chunk(b, d, nsc - 1 - c, dm)

    big = pl.BlockSpec((1, CHUNK, HD), lambda b, d, c: (d, mem(b, d, c), 0))
    gts = pl.BlockSpec((1, 1, C_HEADS, HEAD), lambda b, d, c: (d, mem(b, d, c), 0, 0))
    return pl.pallas_call(
        body, name=name, grid=(dm.Bl, 2, nsc),
        in_specs=[pl.BlockSpec((CHUNK, HD), lambda b, d, c: (mem(b, d, c), 0)),
                  pl.BlockSpec((1, 1, HD, HEAD), lambda b, d, c: (d, mem(b, d, c), 0, 0)), big, big, big, big, big, gts],
        out_specs=[big, big, big, big, big, gts],
        out_shape=[_sds((2, dm.T, HD), F32)] * 5 + [_sds((2, nch, C_HEADS, HEAD), F32)],
        scratch_shapes=[pltpu.VMEM((HD, HEAD), F32)], compiler_params=_cp(),
    )(dO, SS, U, W, KT, QD, QK, GT)


def _finish_fn(o, z, gain):
    y = o * lax.rsqrt(jnp.mean(o * o, axis=-1, keepdims=True) + EPS) * gain
    return y * _silu(z)


def gdn_finish(O, P1, og, dm, name):
    TM = dm.TM
    HD = C_HEADS * HEAD
    zc = (3 * HD) // HEAD

    def body(o0_ref, o1_ref, z_ref, g_ref, y_ref):
        y_ref[...] = _finish_fn(o0_ref[0] + o1_ref[0], z_ref[...], g_ref[...]).astype(BF16)

    return pl.pallas_call(
        body, name=name, grid=(dm.ntx, C_HEADS),
        in_specs=[pl.BlockSpec((1, TM, HEAD), lambda i, j: (0, i, j)), pl.BlockSpec((1, TM, HEAD), lambda i, j: (1, i, j)),
                  pl.BlockSpec((TM, HEAD), lambda i, j: (i, zc + j)), pl.BlockSpec((1, HEAD), lambda i, j: (0, 0))],
        out_specs=pl.BlockSpec((TM, HEAD), lambda i, j: (i, j)),
        out_shape=_sds((dm.Tx, HD), BF16), compiler_params=_cp(),
    )(O, O, P1, og)


def gdn_finish_bwd(O, P1, og, dY, dm, name):
    TM = dm.TM
    HD = C_HEADS * HEAD
    zc = (3 * HD) // HEAD

    def body(o0_ref, o1_ref, z_ref, g_ref, dy_ref, do_ref, dz_ref, dg_ref):
        i, j = pl.program_id(0), pl.program_id(1)
        _, vjp = jax.vjp(_finish_fn, o0_ref[0] + o1_ref[0], z_ref[...], g_ref[...])
        do, dz, dg = vjp(dy_ref[...])
        do_ref[...] = do
        dz_ref[...] = dz
        _acc(dg_ref, (slice(None), slice(None)), dg, (i == 0) & (j == 0))

    blk = pl.BlockSpec((TM, HEAD), lambda i, j: (i, j))
    return pl.pallas_call(
        body, name=name, grid=(dm.ntx, C_HEADS),
        in_specs=[pl.BlockSpec((1, TM, HEAD), lambda i, j: (0, i, j)), pl.BlockSpec((1, TM, HEAD), lambda i, j: (1, i, j)),
                  pl.BlockSpec((TM, HEAD), lambda i, j: (i, zc + j)), pl.BlockSpec((1, HEAD), lambda i, j: (0, 0)), blk],
        out_specs=[blk, blk, pl.BlockSpec((1, HEAD), lambda i, j: (0, 0))],
        out_shape=[_sds((dm.Tx, HD), F32), _sds((dm.Tx, HD), F32), _sds((1, HEAD), F32)],
        compiler_params=_cp(),
    )(O, O, P1, og, dY)


def adaln_fwd(c_ext, w_mod, b_loc, name):
    R, D = c_ext.shape
    nl = w_mod.shape[2]
    tn = _pick(nl, 384)

    def body(c_ref, w_ref, b_ref, o_ref):
        o_ref[0] = _dot(_silu(c_ref[...]), w_ref[0]) + b_ref[0]

    return pl.pallas_call(
        body, name=name, grid=(2, nl // tn),
        in_specs=[pl.BlockSpec((R, D), lambda l, j: (0, 0)), pl.BlockSpec((1, D, tn), lambda l, j: (l, 0, j)),
                  pl.BlockSpec((1, 1, tn), lambda l, j: (l, 0, j))],
        out_specs=pl.BlockSpec((1, R, tn), lambda l, j: (l, 0, j)),
        out_shape=_sds((2, R, nl), F32), compiler_params=_cp(),
    )(c_ext, w_mod, b_loc)


def adaln_bwd(c_ext, c_ctx, w_mod, dmx, dmh, nb, name):
    R, D = c_ext.shape
    nl = w_mod.shape[2]
    tn = _pick(nl, 384)
    nj = nl // tn

    def body(c_ref, cc_ref, w_ref, dmx_ref, dmh_ref, gw_ref, dc_ref):
        l, j = pl.program_id(0), pl.program_id(1)
        dh = dmh_ref[0, 0:1, :]
        for k in range(1, N_DEV):
            dh = dh + dmh_ref[0, k:k + 1, :]
        row = lax.broadcasted_iota(jnp.int32, (R, tn), 0)
        dmat = dmx_ref[0] + jnp.where(row == nb, dh, 0.0)
        gw_ref[0] = _dot_tn(_silu(c_ref[...]), dmat)
        part = _dot_nt(jnp.broadcast_to(dh, (8, tn)), w_ref[0])[0:1]
        _acc(dc_ref, (slice(None), slice(None)), part, (l == 0) & (j == 0))

        @pl.when((l == 1) & (j == nj - 1))
        def _():
            cc = cc_ref[...]
            sg = jax.nn.sigmoid(cc)
            dc_ref[...] = dc_ref[...] * (sg * (1.0 + cc * (1.0 - sg)))

    return pl.pallas_call(
        body, name=name, grid=(2, nj),
        in_specs=[pl.BlockSpec((R, D), lambda l, j: (0, 0)), pl.BlockSpec((1, D), lambda l, j: (0, 0)),
                  pl.BlockSpec((1, D, tn), lambda l, j: (l, 0, j)), pl.BlockSpec((1, R, tn), lambda l, j: (l, 0, j)),
                  pl.BlockSpec((1, N_DEV, tn), lambda l, j: (l, 0, j))],
        out_specs=[pl.BlockSpec((1, D, tn), lambda l, j: (l, 0, j)), pl.BlockSpec((1, D), lambda l, j: (0, 0))],
        out_shape=[_sds((2, D, nl), F32), _sds((1, D), F32)], compiler_params=_cp(),
    )(c_ext, c_ctx, w_mod, dmx, dmh)


def bmod_grad(dmx, dmh, name):
    _, R, n9 = dmx.shape

    def body(dmx_ref, dmh_ref, o_ref):
        o_ref[0] = jnp.sum(dmx_ref[0], axis=0, keepdims=True) + jnp.sum(dmh_ref[0], axis=0, keepdims=True)

    return pl.pallas_call(
        body, name=name, grid=(2,),
        in_specs=[pl.BlockSpec((1, R, n9), lambda l: (l, 0, 0)), pl.BlockSpec((1, N_DEV, n9), lambda l: (l, 0, 0))],
        out_specs=pl.BlockSpec((1, 1, n9), lambda l: (l, 0, 0)), out_shape=_sds((2, 1, n9), F32),
        compiler_params=_cp(),
    )(dmx, dmh)


def adamw(gs, w, m, v, name):
    S, R, C = gs.shape
    cap = max(8, (1 << 20) // (S * C))
    tr = R
    if R > cap:
        tr = max(t for t in range(8, cap + 1, 8) if R % t == 0)

    def body(g_ref, w_ref, m_ref, v_ref, go_ref, d_ref, mo_ref, vo_ref):
        g = g_ref[0].astype(F32)
        for k in range(1, S):
            g = g + g_ref[k].astype(F32)
        mn = ADAM_B1 * m_ref[...] + (1.0 - ADAM_B1) * g
        vn = ADAM_B2 * v_ref[...] + (1.0 - ADAM_B2) * jnp.square(g)
        m_hat = mn / (1.0 - ADAM_B1 ** ADAM_STEP)
        v_hat = vn / (1.0 - ADAM_B2 ** ADAM_STEP)
        go_ref[...] = g
        d_ref[...] = -ADAM_LR * (m_hat / (jnp.sqrt(v_hat) + ADAM_EPS) + ADAM_WD * w_ref[...])
        mo_ref[...] = mn
        vo_ref[...] = vn

    blk = pl.BlockSpec((tr, C), lambda i: (i, 0))
    return pl.pallas_call(
        body, name=name, grid=(R // tr,),
        in_specs=[pl.BlockSpec((S, tr, C), lambda i: (0, i, 0)), blk, blk, blk],
        out_specs=[blk] * 4, out_shape=[_sds((R, C), F32)] * 4, compiler_params=_cp(),
    )(gs, w, m, v)


def _gather_flat(parts, dtype, name):
    flat = jnp.concatenate([p.astype(dtype).reshape(-1) for p in parts])
    n = flat.shape[0]
    pad = (-n) % LANE
    if pad:
        flat = jnp.concatenate([flat, jnp.zeros((pad,), dtype)])
    got = all_gather([flat.reshape(-1, LANE)], name)[0].reshape(N_DEV, -1)
    out, off = [], 0
    for p in parts:
        out.append(got[:, off:off + p.size].reshape((N_DEV,) + p.shape))
        off += p.size
    return out


def _cols_full(g):
    return g.transpose(1, 0, 2).reshape(g.shape[1], -1)


def _cols_split(full):
    K = full.shape[0]
    return full.reshape(K, N_DEV, -1).transpose(1, 0, 2)


def kernel(x, c, ctx, c_ctx, w_mod, b_mod, norm_g, ffn_wg, ffn_wu, ffn_wd, ab_w_in, ab_q_norm, ab_k_norm, pool_w, pool_scale, ab_w_out, gdn_w_in, gdn_conv_w, gdn_a_log, gdn_dt_bias, gdn_o_norm, gdn_w_out, loss_target, m_c_ctx, m_w_mod, m_b_mod, m_norm_g, m_ffn_wg, m_ffn_wu, m_ffn_wd, m_ab_w_in, m_ab_q_norm, m_ab_k_norm, m_pool_w, m_pool_scale, m_ab_w_out, m_gdn_w_in, m_gdn_conv_w, m_gdn_a_log, m_gdn_dt_bias, m_gdn_o_norm, m_gdn_w_out, v_c_ctx, v_w_mod, v_b_mod, v_norm_g, v_ffn_wg, v_ffn_wu, v_ffn_wd, v_ab_w_in, v_ab_q_norm, v_ab_k_norm, v_pool_w, v_pool_scale, v_ab_w_out, v_gdn_w_in, v_gdn_conv_w, v_gdn_a_log, v_gdn_dt_bias, v_gdn_o_norm, v_gdn_w_out):
    Bl, N, D = x.shape
    M = ctx.shape[1]
    F = ffn_wd.shape[2] * N_DEV
    dm = Dims(Bl, N, M, D, F)
    TM, Tx, Th, T, G = dm.TM, dm.Tx, dm.Th, dm.T, dm.G
    HD = C_HEADS * HEAD
    me = 4 * lax.axis_index("x") + 2 * lax.axis_index("y") + lax.axis_index("c")
    nb = N_DEV * Bl
    R = -(-(nb + 1) // 8) * 8
    nl = w_mod.shape[2]
    n_gdn = gdn_w_in.shape[2] * N_DEV
    n_gdn_pad = -(-n_gdn // LANE) * LANE

    big = [w.astype(BF16) for w in (ffn_wg, ffn_wu, ffn_wd, ab_w_in, ab_w_out, gdn_w_in, gdn_w_out)]
    g_wg, g_wu, g_wd, g_abin, g_about, g_gin, g_gout = all_gather(big, "gather_weights")
    g_c, g_ng, g_cw = _gather_flat([c, norm_g, gdn_conv_w], F32, "gather_small")
    W_ABIN = _cols_full(g_abin[:, 0])
    W_ABOUT = g_about[:, 0].reshape(-1, D)
    W_GIN = jnp.pad(_cols_full(g_gin[:, 0]), ((0, 0), (0, n_gdn_pad - n_gdn)))
    W_GOUT = g_gout[:, 0].reshape(-1, D)
    gains = g_ng.transpose(1, 2, 0, 3).reshape(2, 3, 1, D)
    conv_w = g_cw[:, 0].transpose(1, 0, 2).reshape(3, -1)

    c_all = g_c.reshape(nb, D)
    c_ext = jnp.concatenate([c_all, c_ctx[None], jnp.zeros((R - nb - 1, D), F32)], 0)
    b_loc = lax.dynamic_slice_in_dim(b_mod, me * nl, nl, axis=1).reshape(2, 1, nl)
    mod_loc = adaln_fwd(c_ext, w_mod, b_loc, "adaln_fwd")
    (g_mod,) = _gather_flat([mod_loc], F32, "gather_mod")
    mod_full = g_mod.transpose(1, 2, 0, 3).reshape(2, R, 9 * D)
    MOD = []
    for l in range(2):
        mine = lax.dynamic_slice_in_dim(mod_full[l], me * Bl, Bl, axis=0)
        MOD.append(jnp.concatenate([mine, mod_full[l, nb:nb + 1]], 0).reshape(G, 9, D))

    dm5, dmr = dm.with_tile(512), dm.with_tile(1024)
    tabs = _rope_tables(dmr)
    qkg = jnp.stack([ab_q_norm, ab_k_norm])
    pw = pool_w[0].astype(BF16)
    pwT = pool_w[0].transpose(0, 2, 1).astype(BF16)
    par = jnp.stack([jnp.pad(jnp.pad(p[0], ((0, 0), (0, 8))).reshape(-1), (0, LANE - 32))
                     for p in (gdn_a_log, gdn_dt_bias)])

    X0 = jnp.concatenate([x.reshape(Tx, D), ctx.reshape(Th, D)], 0)
    def ffn(X, l, s, s0, all_rows, tag):
        return ffn_fwd(X, MOD[l], gains[l, 2 * s], g_wg, g_wu, g_wd, l, s, s0, dm, all_rows, "ffn_fwd_" + tag)

    X1, Y1 = ffn(X0, 0, 0, 0, True, "00")
    P0, XN0 = modmm(X1, MOD[0], gains[0, 1], W_ABIN, 3, dm5, "ab_in_proj")
    QKV = qkv_prep(P0, qkg, tabs, dmr, "qkv_prep")
    ATT = jnp.concatenate([attn_fwd(QKV, dm5, True, "attn_fwd_x"), attn_fwd(QKV, dm, False, "attn_fwd_h")], 0)
    POOL = jnp.concatenate([pool_fwd(P0, pw, pool_scale, dm, True, "pool_fwd_x"),
                            pool_fwd(P0, pw, pool_scale, dm, False, "pool_fwd_h")], 0)
    na = A_HEADS * HEAD
    X2, YM0 = proj_res([ATT, POOL], [W_ABOUT[:na], W_ABOUT[na:]], X1, MOD[0], dm5, dm5.nt, "ab_out_proj")
    X3, Y3 = ffn(X2, 0, 1, 6, True, "01")
    X4, Y4 = ffn(X3, 1, 0, 0, True, "10")
    P1, XN1 = modmm(X4, MOD[1], gains[1, 1], W_GIN, 3, dm5, "gdn_in_proj")
    QKVg = jnp.concatenate([gdn_prep(P1, conv_w, dm, True, "gdn_prep_x"), gdn_prep(P1, conv_w, dm, False, "gdn_prep_h")], 0)
    U, W, KT, QD, QK, GT, WF, AI = gdn_chunk_pre(QKVg, P1, par, dm, "gdn_chunk_pre")
    O, SS = gdn_scan_fwd(U, W, KT, QD, QK, GT, dm, "gdn_scan_fwd")
    FIN = gdn_finish(O, P1, gdn_o_norm, dmr, "gdn_finish")
    X5, YM1 = proj_res([FIN], [W_GOUT], X4, MOD[1], dm5, dm5.ntx, "gdn_out_proj")
    X6, Y6 = ffn(X5, 1, 1, 6, False, "11")
    lvec, dX6 = loss_head(X6, loss_target.reshape(Tx, D), dmr, "loss_head")
    loss = lax.psum(lvec[0, 0], AXES)

    zrow = lambda a: jnp.concatenate([a, jnp.zeros((G - a.shape[0],) + a.shape[1:], F32)], 0) if a.shape[0] < G else a

    def ffn_back(Xin, dXo, Y, l, s, s0, all_rows, tag):
        dXi, XNb, DOb, Hb, DGb, DUb, dmod, dgain = ffn_bwd(
            Xin, dXo, Y, MOD[l], gains[l, 2 * s], g_wg, g_wu, g_wd, l, s, s0, dm, all_rows, "ffn_bwd_" + tag)
        dwg, dwu, dwd = ffn_dw(XNb, DOb, Hb, DGb, DUb, dm, "ffn_dw_" + tag)
        return dXi, zrow(dmod), dgain, dwg, dwu, dwd

    dX5, dmod_12, dgain_12, dwg11, dwu11, dwd11 = ffn_back(X5, dX6, Y6, 1, 1, 6, False, "11")
    DY1, dFIN, dgate_1 = proj_res_bwd(dX5, YM1, MOD[1], [W_GOUT], dm5, dm5.ntx, "gdn_out_proj_bwd")
    d_gout = atb(FIN, DY1, Tx, dm, "gdn_dwout")
    dOsum, dZ, d_onorm = gdn_finish_bwd(O, P1, gdn_o_norm, dFIN, dmr, "gdn_finish_bwd")
    dO_all = jnp.concatenate([dOsum, jnp.zeros((Th, HD), F32)], 0)
    dU, dW, dKT, dQD, dQK, dGT = gdn_scan_bwd(dO_all, SS, U, W, KT, QD, QK, GT, dm, "gdn_scan_bwd")
    dQKVg, dAB, dPAR = gdn_chunk_pre_bwd(QKVg, P1, par, U, WF, AI, dU, dW, dKT, dQD, dQK, dGT, dm, "gdn_chunk_pre_bwd")
    dPx, dcw_x = gdn_prep_bwd(P1, dQKVg, conv_w, dm, True, "gdn_prep_bwd_x")
    dPh, dcw_h = gdn_prep_bwd(P1, dQKVg, conv_w, dm, False, "gdn_prep_bwd_h")
    d_conv = dcw_x + dcw_h
    dP1 = jnp.concatenate([jnp.concatenate([dPx, dPh], 0), jnp.concatenate([dZ, jnp.zeros((Th, HD), F32)], 0),
                           dAB[0] + dAB[1]], axis=1).astype(BF16)
    d_gin = atb(XN1, dP1, T, dm, "gdn_dwin")[:, :n_gdn]
    dX5_full = jnp.concatenate([dX5, jnp.zeros((Th, D), F32)], 0)
    dX4, dmod_11, dgain_11 = mixin_bwd(dP1, W_GIN, X4, dX5_full, MOD[1], gains[1, 1], 3, dm, "gdn_in_proj_bwd")
    dX3, dmod_10, dgain_10, dwg10, dwu10, dwd10 = ffn_back(X3, dX4, Y4, 1, 0, 0, True, "10")
    early = [_cols_split(d_gin), d_gout.reshape(N_DEV, -1, D), _cols_split(d_conv),
             jnp.concatenate([dwg10, dwg11], 1), jnp.concatenate([dwu10, dwu11], 1), jnp.concatenate([dwd10, dwd11], 1)]
    lands = [lax.dynamic_update_index_in_dim(jnp.zeros_like(a), lax.dynamic_index_in_dim(a, me, 0, keepdims=False), me, 0)
             for a in early]
    e_send, e_recv, e_src, e_land, e_token = scatter_start(early, lands, "scatter_start_gdn")
    MOD[0] = MOD[0] + e_token[0, 0]
    dMOD1 = jnp.concatenate([dmod_10, dmod_11, zrow(dgate_1), dmod_12], 1).reshape(G, 9 * D)

    dX2, dmod_02, dgain_02, dwg01, dwu01, dwd01 = ffn_back(X2, dX3, Y3, 0, 1, 6, True, "01")
    DY0, dATT, dPOOL, dgate_0 = proj_res_bwd(dX2, YM0, MOD[0], [W_ABOUT[:na], W_ABOUT[na:]], dm5, dm5.nt, "ab_out_proj_bwd")
    d_about = jnp.concatenate([atb(ATT, DY0, T, dm, "ab_dwout_a"), atb(POOL, DY0, T, dm, "ab_dwout_p")], 0)
    dUx, dpw_x, dps_x = pool_bwd(P0, dPOOL[:Tx], pw, pwT, pool_scale, dm, True, "pool_bwd_x")
    dUh, dpw_h, dps_h = pool_bwd(P0, dPOOL[Tx:], pw, pwT, pool_scale, dm, False, "pool_bwd_h")
    dQh, dKh0, dVh0 = attn_bwd(QKV, dATT[Tx:], dm, False, None, "attn_bwd_h")
    dQx, dKh, dVh, dKx, dVx = attn_bwd(QKV, dATT[:Tx], dm5, True, (dKh0, dVh0), "attn_bwd_x")
    dQKV = jnp.concatenate([jnp.concatenate([dQx, dQh], 0), jnp.concatenate([dKx, dKh], 0), jnp.concatenate([dVx, dVh], 0)], 1)
    dPqkv, d_qkg = qkv_prep_bwd(P0, dQKV, qkg, tabs, dmr, "qkv_prep_bwd")
    dP0 = jnp.concatenate([dPqkv, jnp.concatenate([dUx, dUh], 0)], 1).astype(BF16)
    d_abin = atb(XN0, dP0, T, dm, "ab_dwin")
    dX1, dmod_01, dgain_01 = mixin_bwd(dP0, W_ABIN, X1, dX2, MOD[0], gains[0, 1], 3, dm5, "ab_in_proj_bwd")
    dX0, dmod_00, dgain_00, dwg00, dwu00, dwd00 = ffn_back(X0, dX1, Y1, 0, 0, 0, True, "00")
    dMOD0 = jnp.concatenate([dmod_00, dmod_01, dgate_0, dmod_02], 1).reshape(G, 9 * D)
    grad_x = dX0[:Tx].reshape(Bl, N, D)

    d_ng = jnp.concatenate([dgain_00, dgain_01, dgain_02, dgain_10, dgain_11, dgain_12], 0)
    nf = ffn_wg.shape[3]
    parts = [jnp.concatenate([dwg00, dwg01], 1), jnp.concatenate([dwu00, dwu01], 1), jnp.concatenate([dwd00, dwd01], 1),
             _cols_split(d_abin), d_about.reshape(N_DEV, -1, D), _cols_split(d_ng)]
    g0_wg, g0_wu, g0_wd, gs_abin, gs_about, gs_ng = scatter_blocks(parts, "scatter_grads")
    gs_gin, gs_gout, gs_conv, g1_wg, g1_wu, g1_wd = scatter_wait(e_send, e_recv, e_src, e_land, dX0, "scatter_wait_gdn")
    gs_wg, gs_wu, gs_wd = (jnp.concatenate(p, 1) for p in ((g0_wg, g1_wg), (g0_wu, g1_wu), (g0_wd, g1_wd)))

    d_alog = dPAR[0, :32].reshape(2, 16)[:, :8].reshape(1, 16)
    d_dtb = dPAR[1, :32].reshape(2, 16)[:, :8].reshape(1, 16)
    small = [d_qkg[0], d_qkg[1], (dpw_x + dpw_h).reshape(-1, HEAD), dps_x + dps_h, d_alog, d_dtb, d_onorm,
             jnp.stack([dMOD0, dMOD1])]
    gs_qn, gs_kn, gs_pw, gs_ps, gs_alog, gs_dtb, gs_on, g_dm = _gather_flat(small, F32, "gather_small_grads")
    dmx = g_dm[:, :, :Bl].transpose(1, 0, 2, 3).reshape(2, nb, 9 * D)
    dmx = jnp.concatenate([dmx, jnp.zeros((2, R - nb, 9 * D), F32)], 1)
    dmh = g_dm[:, :, Bl].transpose(1, 0, 2)
    cols_of_me = lambda a: lax.dynamic_slice_in_dim(a, me * nl, nl, axis=2)
    d_wmod, dcc = adaln_bwd(c_ext, c_ctx[None], w_mod, cols_of_me(dmx), cols_of_me(dmh), nb, "adaln_bwd")
    d_bmod = bmod_grad(dmx, dmh, "bmod_grad")
    (gs_cc,) = _gather_flat([dcc], F32, "gather_cctx_grad")

    def upd(gs, w, m, v, shape2, name):
        outs = adamw(gs.reshape((gs.shape[0],) + shape2), w.reshape(shape2), m.reshape(shape2), v.reshape(shape2), "adamw_" + name)
        return [o.reshape(w.shape) for o in outs]

    res = [
        upd(gs_cc, c_ctx, m_c_ctx, v_c_ctx, (1, D), "c_ctx"),
        upd(d_wmod[None], w_mod, m_w_mod, v_w_mod, (2 * D, nl), "w_mod"),
        upd(d_bmod[None], b_mod, m_b_mod, v_b_mod, (2, 9 * D), "b_mod"),
        upd(gs_ng, norm_g, m_norm_g, v_norm_g, (6, D // N_DEV), "norm_g"),
        upd(gs_wg, ffn_wg, m_ffn_wg, v_ffn_wg, (4 * D, nf), "ffn_wg"),
        upd(gs_wu, ffn_wu, m_ffn_wu, v_ffn_wu, (4 * D, nf), "ffn_wu"),
        upd(gs_wd, ffn_wd, m_ffn_wd, v_ffn_wd, (4 * nf, D), "ffn_wd"),
        upd(gs_abin, ab_w_in, m_ab_w_in, v_ab_w_in, (D, ab_w_in.shape[2]), "ab_w_in"),
        upd(gs_qn, ab_q_norm, m_ab_q_norm, v_ab_q_norm, (1, HEAD), "ab_q_norm"),
        upd(gs_kn, ab_k_norm, m_ab_k_norm, v_ab_k_norm, (1, HEAD), "ab_k_norm"),
        upd(gs_pw, pool_w, m_pool_w, v_pool_w, (len(POOL_WINDOWS) * HEAD, HEAD), "pool_w"),
        upd(gs_ps, pool_scale, m_pool_scale, v_pool_scale, (1, len(POOL_WINDOWS) * HEAD), "pool_scale"),
        upd(gs_about, ab_w_out, m_ab_w_out, v_ab_w_out, (ab_w_out.shape[1], D), "ab_w_out"),
        upd(gs_gin, gdn_w_in, m_gdn_w_in, v_gdn_w_in, (D, gdn_w_in.shape[2]), "gdn_w_in"),
        upd(gs_conv, gdn_conv_w, m_gdn_conv_w, v_gdn_conv_w, (3, gdn_conv_w.shape[2]), "gdn_conv_w"),
        upd(gs_alog, gdn_a_log, m_gdn_a_log, v_gdn_a_log, (1, 16), "gdn_a_log"),
        upd(gs_dtb, gdn_dt_bias, m_gdn_dt_bias, v_gdn_dt_bias, (1, 16), "gdn_dt_bias"),
        upd(gs_on, gdn_o_norm, m_gdn_o_norm, v_gdn_o_norm, (1, HEAD), "gdn_o_norm"),
        upd(gs_gout, gdn_w_out, m_gdn_w_out, v_gdn_w_out, (gdn_w_out.shape[1], D), "gdn_w_out"),
    ]
    return (loss, grad_x, *[r[0] for r in res], *[r[1] for r in res], *[r[2] for r in res], *[r[3] for r in res])
```
